```python
import math
import jax, jax.numpy as jnp
from jax import lax
import numpy as np

D_MODEL = 2048
BATCH = 8
SEQ = 2048
DEPTH = 4

CHUNK = 64
N_MIXERS = 3
N_HEADS = 16
HEAD_DIM = D_MODEL // N_HEADS
Q_BLOCK = 128
LEFT_CHUNKS = 8
BAND = LEFT_CHUNKS + 1
REL_CLIP = 128
CONV_K = 31
D_FF = ((8 * D_MODEL // 3 + 255) // 256) * 256
ALPHA = (2.0 * DEPTH) ** 0.25
BETA = (8.0 * DEPTH) ** -0.25
LN_EPS = 1e-5
N_A = (DEPTH + 2) // 3
N_B = (DEPTH + 1) // 3
N_C = DEPTH // 3

kernel_name = "hybrid_fox_chunkrel_conformer_deepnorm"


def layer_norm(x, g, b):
    xf = x.astype(jnp.float32)
    mu = jnp.mean(xf, axis=-1, keepdims=True)
    var = jnp.mean(jnp.square(xf - mu), axis=-1, keepdims=True)
    y = (xf - mu) * lax.rsqrt(var + LN_EPS) * g.astype(jnp.float32) + b.astype(jnp.float32)
    return y.astype(x.dtype)


def split_heads(qkv):
    B, S, _ = qkv.shape
    q, k, v = jnp.split(qkv, 3, axis=-1)
    shp = (B, S, N_HEADS, HEAD_DIM)
    return q.reshape(shp), k.reshape(shp), v.reshape(shp)


def fox_attention(h, w_qkv, w_f, b_f, w_o):
    B, S, _ = h.shape
    q, k, v = split_heads(h @ w_qkv)
    log_f = jax.nn.log_sigmoid((h @ w_f + b_f).astype(jnp.float32))
    c = jnp.transpose(jnp.cumsum(log_f, axis=1), (0, 2, 1))
    scale = HEAD_DIM ** -0.5
    outs = []
    for blk in range(S // Q_BLOCK):
        q0, q1 = blk * Q_BLOCK, (blk + 1) * Q_BLOCK
        s = jnp.einsum('bqhd,bkhd->bhqk', q[:, q0:q1], k[:, :q1]).astype(jnp.float32) * scale
        s = s + (c[:, :, q0:q1, None] - c[:, :, None, :q1])
        causal = (q0 + jnp.arange(Q_BLOCK))[:, None] >= jnp.arange(q1)[None, :]
        p = jax.nn.softmax(jnp.where(causal[None, None], s, -jnp.inf), axis=-1)
        outs.append(jnp.einsum('bhqk,bkhd->bqhd', p.astype(v.dtype), v[:, :q1]))
    o = jnp.concatenate(outs, axis=1).reshape(B, S, D_MODEL)
    return o @ w_o


def chunk_relpos_attention(h, w_qkv, rel_bias, w_o):
    B, S, _ = h.shape
    nc = S // CHUNK
    q, k, v = split_heads(h @ w_qkv)
    pad = LEFT_CHUNKS * CHUNK
    qc = q.reshape(B, nc, CHUNK, N_HEADS, HEAD_DIM)
    padw = ((0, 0), (pad, 0), (0, 0), (0, 0))
    k_pad = jnp.pad(k, padw).reshape(B, nc + LEFT_CHUNKS, CHUNK, N_HEADS, HEAD_DIM)
    v_pad = jnp.pad(v, padw).reshape(B, nc + LEFT_CHUNKS, CHUNK, N_HEADS, HEAD_DIM)
    band_idx = jnp.arange(nc)[:, None] + jnp.arange(BAND)[None, :]
    kb = k_pad[:, band_idx].reshape(B, nc, BAND * CHUNK, N_HEADS, HEAD_DIM)
    vb = v_pad[:, band_idx].reshape(B, nc, BAND * CHUNK, N_HEADS, HEAD_DIM)
    s = jnp.einsum('bnqhd,bnkhd->bnhqk', qc, kb).astype(jnp.float32) * (HEAD_DIM ** -0.5)
    k_off = jnp.arange(BAND * CHUNK)
    rel = pad + jnp.arange(CHUNK)[:, None] - k_off[None, :]
    bias = rel_bias[:, jnp.clip(rel, -REL_CLIP, REL_CLIP) + REL_CLIP]
    s = s + bias[None, None].astype(jnp.float32)
    valid = (jnp.arange(nc)[:, None] * CHUNK - pad + k_off[None, :]) >= 0
    p = jax.nn.softmax(jnp.where(valid[None, :, None, None, :], s, -jnp.inf), axis=-1)
    o = jnp.einsum('bnhqk,bnkhd->bnqhd', p.astype(vb.dtype), vb).reshape(B, S, D_MODEL)
    return o @ w_o


def conformer_conv(h, w_pw1, b_pw1, w_dw, b_dw, ln_g, ln_b, w_pw2, b_pw2):
    u = h @ w_pw1 + b_pw1
    a, g = jnp.split(u, 2, axis=-1)
    u = a * jax.nn.sigmoid(g)
    y = lax.conv_general_dilated(
        u, w_dw[:, None, :].astype(u.dtype), window_strides=(1,),
        padding=[(CONV_K - 1, 0)], dimension_numbers=('NWC', 'WIO', 'NWC'),
        feature_group_count=D_MODEL) + b_dw
    y = jax.nn.silu(layer_norm(y, ln_g, ln_b))
    return y @ w_pw2 + b_pw2


def swiglu_ffn(h, w_gate, w_up, w_down):
    return (jax.nn.silu(h @ w_gate) * (h @ w_up)) @ w_down


def _fwd_setup_inputs(seed: int = 0) -> dict:
    key = jax.random.key(seed)
    ks = iter(jax.random.split(key, 32))
    f32 = jnp.float32

    def nrm(shape, scale):
        return jax.random.normal(next(ks), shape, f32) * scale

    D, F, H = D_MODEL, D_FF, N_HEADS
    s_d = D ** -0.5

    def qkv_w(n):
        qk = nrm((n, D, 2 * D), s_d)
        v = nrm((n, D, D), s_d * BETA)
        return jnp.concatenate([qk, v], axis=-1)

    return {
        "x": jax.random.normal(next(ks), (BATCH, SEQ, D), f32),
        "fox_w_qkv": qkv_w(N_A),
        "fox_w_f": nrm((N_A, D, H), s_d),
        "fox_b_f": 3.0 + nrm((N_A, H), 0.5),
        "fox_w_o": nrm((N_A, D, D), s_d * BETA),
        "rel_w_qkv": qkv_w(N_B),
        "rel_bias": nrm((N_B, H, 2 * REL_CLIP + 1), 0.2),
        "rel_w_o": nrm((N_B, D, D), s_d * BETA),
        "conv_w_pw1": nrm((N_C, D, 2 * D), s_d),
        "conv_b_pw1": nrm((N_C, 2 * D), 0.02),
        "conv_w_dw": nrm((N_C, CONV_K, D), CONV_K ** -0.5),
        "conv_b_dw": nrm((N_C, D), 0.02),
        "conv_ln_g": 1.0 + nrm((N_C, D), 0.02),
        "conv_ln_b": nrm((N_C, D), 0.02),
        "conv_w_pw2": nrm((N_C, D, D), s_d * BETA),
        "conv_b_pw2": nrm((N_C, D), 0.02),
        "ffn_w_gate": nrm((DEPTH, D, F), s_d),
        "ffn_w_up": nrm((DEPTH, D, F), s_d),
        "ffn_w_down": nrm((DEPTH, F, D), F ** -0.5 * BETA),
        "ln_mix_g": 1.0 + nrm((DEPTH, D), 0.02),
        "ln_mix_b": nrm((DEPTH, D), 0.02),
        "ln_ffn_g": 1.0 + nrm((DEPTH, D), 0.02),
        "ln_ffn_b": nrm((DEPTH, D), 0.02),
    }


def _fwd_reference(x, fox_w_qkv, fox_w_f, fox_b_f, fox_w_o,
              rel_w_qkv, rel_bias, rel_w_o,
              conv_w_pw1, conv_b_pw1, conv_w_dw, conv_b_dw, conv_ln_g, conv_ln_b, conv_w_pw2, conv_b_pw2,
              ffn_w_gate, ffn_w_up, ffn_w_down,
              ln_mix_g, ln_mix_b, ln_ffn_g, ln_ffn_b):
    for i in range(DEPTH):
        kind, j = i % N_MIXERS, i // N_MIXERS
        if kind == 0:
            m = fox_attention(x, fox_w_qkv[j], fox_w_f[j], fox_b_f[j], fox_w_o[j])
        elif kind == 1:
            m = chunk_relpos_attention(x, rel_w_qkv[j], rel_bias[j], rel_w_o[j])
        else:
            m = conformer_conv(x, conv_w_pw1[j], conv_b_pw1[j], conv_w_dw[j], conv_b_dw[j],
                               conv_ln_g[j], conv_ln_b[j], conv_w_pw2[j], conv_b_pw2[j])
        x = layer_norm(ALPHA * x + m, ln_mix_g[i], ln_mix_b[i])
        f = swiglu_ffn(x, ffn_w_gate[i], ffn_w_up[i], ffn_w_down[i])
        x = layer_norm(ALPHA * x + f, ln_ffn_g[i], ln_ffn_b[i])
    return x


import jax as _jax
import jax.numpy as _jnp

TWIN_FORMAT = 'train_step'
FWD_PARAMS = ['x', 'fox_w_qkv', 'fox_w_f', 'fox_b_f', 'fox_w_o', 'rel_w_qkv', 'rel_bias', 'rel_w_o', 'conv_w_pw1', 'conv_b_pw1', 'conv_w_dw', 'conv_b_dw', 'conv_ln_g', 'conv_ln_b', 'conv_w_pw2', 'conv_b_pw2', 'ffn_w_gate', 'ffn_w_up', 'ffn_w_down', 'ln_mix_g', 'ln_mix_b', 'ln_ffn_g', 'ln_ffn_b']
TWIN_WEIGHTS = ['fox_w_qkv', 'fox_w_f', 'fox_b_f', 'fox_w_o', 'rel_w_qkv', 'rel_bias', 'rel_w_o', 'conv_w_pw1', 'conv_b_pw1', 'conv_w_dw', 'conv_b_dw', 'conv_ln_g', 'conv_ln_b', 'conv_w_pw2', 'conv_b_pw2', 'ffn_w_gate', 'ffn_w_up', 'ffn_w_down', 'ln_mix_g', 'ln_mix_b', 'ln_ffn_g', 'ln_ffn_b']
TWIN_DIFF_INPUT = 'x'
TWIN_INPUTS = ['x', 'fox_w_qkv', 'fox_w_f', 'fox_b_f', 'fox_w_o', 'rel_w_qkv', 'rel_bias', 'rel_w_o', 'conv_w_pw1', 'conv_b_pw1', 'conv_w_dw', 'conv_b_dw', 'conv_ln_g', 'conv_ln_b', 'conv_w_pw2', 'conv_b_pw2', 'ffn_w_gate', 'ffn_w_up', 'ffn_w_down', 'ln_mix_g', 'ln_mix_b', 'ln_ffn_g', 'ln_ffn_b', 'loss_target', 'm_fox_w_qkv', 'm_fox_w_f', 'm_fox_b_f', 'm_fox_w_o', 'm_rel_w_qkv', 'm_rel_bias', 'm_rel_w_o', 'm_conv_w_pw1', 'm_conv_b_pw1', 'm_conv_w_dw', 'm_conv_b_dw', 'm_conv_ln_g', 'm_conv_ln_b', 'm_conv_w_pw2', 'm_conv_b_pw2', 'm_ffn_w_gate', 'm_ffn_w_up', 'm_ffn_w_down', 'm_ln_mix_g', 'm_ln_mix_b', 'm_ln_ffn_g', 'm_ln_ffn_b', 'v_fox_w_qkv', 'v_fox_w_f', 'v_fox_b_f', 'v_fox_w_o', 'v_rel_w_qkv', 'v_rel_bias', 'v_rel_w_o', 'v_conv_w_pw1', 'v_conv_b_pw1', 'v_conv_w_dw', 'v_conv_b_dw', 'v_conv_ln_g', 'v_conv_ln_b', 'v_conv_w_pw2', 'v_conv_b_pw2', 'v_ffn_w_gate', 'v_ffn_w_up', 'v_ffn_w_down', 'v_ln_mix_g', 'v_ln_mix_b', 'v_ln_ffn_g', 'v_ln_ffn_b']
TWIN_OUTPUTS = ['loss', 'grad_x', 'grad_fox_w_qkv', 'grad_fox_w_f', 'grad_fox_b_f', 'grad_fox_w_o', 'grad_rel_w_qkv', 'grad_rel_bias', 'grad_rel_w_o', 'grad_conv_w_pw1', 'grad_conv_b_pw1', 'grad_conv_w_dw', 'grad_conv_b_dw', 'grad_conv_ln_g', 'grad_conv_ln_b', 'grad_conv_w_pw2', 'grad_conv_b_pw2', 'grad_ffn_w_gate', 'grad_ffn_w_up', 'grad_ffn_w_down', 'grad_ln_mix_g', 'grad_ln_mix_b', 'grad_ln_ffn_g', 'grad_ln_ffn_b', 'delta_fox_w_qkv', 'delta_fox_w_f', 'delta_fox_b_f', 'delta_fox_w_o', 'delta_rel_w_qkv', 'delta_rel_bias', 'delta_rel_w_o', 'delta_conv_w_pw1', 'delta_conv_b_pw1', 'delta_conv_w_dw', 'delta_conv_b_dw', 'delta_conv_ln_g', 'delta_conv_ln_b', 'delta_conv_w_pw2', 'delta_conv_b_pw2', 'delta_ffn_w_gate', 'delta_ffn_w_up', 'delta_ffn_w_down', 'delta_ln_mix_g', 'delta_ln_mix_b', 'delta_ln_ffn_g', 'delta_ln_ffn_b', 'new_m_fox_w_qkv', 'new_m_fox_w_f', 'new_m_fox_b_f', 'new_m_fox_w_o', 'new_m_rel_w_qkv', 'new_m_rel_bias', 'new_m_rel_w_o', 'new_m_conv_w_pw1', 'new_m_conv_b_pw1', 'new_m_conv_w_dw', 'new_m_conv_b_dw', 'new_m_conv_ln_g', 'new_m_conv_ln_b', 'new_m_conv_w_pw2', 'new_m_conv_b_pw2', 'new_m_ffn_w_gate', 'new_m_ffn_w_up', 'new_m_ffn_w_down', 'new_m_ln_mix_g', 'new_m_ln_mix_b', 'new_m_ln_ffn_g', 'new_m_ln_ffn_b', 'new_v_fox_w_qkv', 'new_v_fox_w_f', 'new_v_fox_b_f', 'new_v_fox_w_o', 'new_v_rel_w_qkv', 'new_v_rel_bias', 'new_v_rel_w_o', 'new_v_conv_w_pw1', 'new_v_conv_b_pw1', 'new_v_conv_w_dw', 'new_v_conv_b_dw', 'new_v_conv_ln_g', 'new_v_conv_ln_b', 'new_v_conv_w_pw2', 'new_v_conv_b_pw2', 'new_v_ffn_w_gate', 'new_v_ffn_w_up', 'new_v_ffn_w_down', 'new_v_ln_mix_g', 'new_v_ln_mix_b', 'new_v_ln_ffn_g', 'new_v_ln_ffn_b']
TWIN_LEAF_KINDS = {'loss': 'loss', 'grad_x': 'grad_x', 'grad_fox_w_qkv': 'grad_w', 'grad_fox_w_f': 'grad_w', 'grad_fox_b_f': 'grad_w', 'grad_fox_w_o': 'grad_w', 'grad_rel_w_qkv': 'grad_w', 'grad_rel_bias': 'grad_w', 'grad_rel_w_o': 'grad_w', 'grad_conv_w_pw1': 'grad_w', 'grad_conv_b_pw1': 'grad_w', 'grad_conv_w_dw': 'grad_w', 'grad_conv_b_dw': 'grad_w', 'grad_conv_ln_g': 'grad_w', 'grad_conv_ln_b': 'grad_w', 'grad_conv_w_pw2': 'grad_w', 'grad_conv_b_pw2': 'grad_w', 'grad_ffn_w_gate': 'grad_w', 'grad_ffn_w_up': 'grad_w', 'grad_ffn_w_down': 'grad_w', 'grad_ln_mix_g': 'grad_w', 'grad_ln_mix_b': 'grad_w', 'grad_ln_ffn_g': 'grad_w', 'grad_ln_ffn_b': 'grad_w', 'delta_fox_w_qkv': 'delta_w', 'delta_fox_w_f': 'delta_w', 'delta_fox_b_f': 'delta_w', 'delta_fox_w_o': 'delta_w', 'delta_rel_w_qkv': 'delta_w', 'delta_rel_bias': 'delta_w', 'delta_rel_w_o': 'delta_w', 'delta_conv_w_pw1': 'delta_w', 'delta_conv_b_pw1': 'delta_w', 'delta_conv_w_dw': 'delta_w', 'delta_conv_b_dw': 'delta_w', 'delta_conv_ln_g': 'delta_w', 'delta_conv_ln_b': 'delta_w', 'delta_conv_w_pw2': 'delta_w', 'delta_conv_b_pw2': 'delta_w', 'delta_ffn_w_gate': 'delta_w', 'delta_ffn_w_up': 'delta_w', 'delta_ffn_w_down': 'delta_w', 'delta_ln_mix_g': 'delta_w', 'delta_ln_mix_b': 'delta_w', 'delta_ln_ffn_g': 'delta_w', 'delta_ln_ffn_b': 'delta_w', 'new_m_fox_w_qkv': 'new_m', 'new_m_fox_w_f': 'new_m', 'new_m_fox_b_f': 'new_m', 'new_m_fox_w_o': 'new_m', 'new_m_rel_w_qkv': 'new_m', 'new_m_rel_bias': 'new_m', 'new_m_rel_w_o': 'new_m', 'new_m_conv_w_pw1': 'new_m', 'new_m_conv_b_pw1': 'new_m', 'new_m_conv_w_dw': 'new_m', 'new_m_conv_b_dw': 'new_m', 'new_m_conv_ln_g': 'new_m', 'new_m_conv_ln_b': 'new_m', 'new_m_conv_w_pw2': 'new_m', 'new_m_conv_b_pw2': 'new_m', 'new_m_ffn_w_gate': 'new_m', 'new_m_ffn_w_up': 'new_m', 'new_m_ffn_w_down': 'new_m', 'new_m_ln_mix_g': 'new_m', 'new_m_ln_mix_b': 'new_m', 'new_m_ln_ffn_g': 'new_m', 'new_m_ln_ffn_b': 'new_m', 'new_v_fox_w_qkv': 'new_v', 'new_v_fox_w_f': 'new_v', 'new_v_fox_b_f': 'new_v', 'new_v_fox_w_o': 'new_v', 'new_v_rel_w_qkv': 'new_v', 'new_v_rel_bias': 'new_v', 'new_v_rel_w_o': 'new_v', 'new_v_conv_w_pw1': 'new_v', 'new_v_conv_b_pw1': 'new_v', 'new_v_conv_w_dw': 'new_v', 'new_v_conv_b_dw': 'new_v', 'new_v_conv_ln_g': 'new_v', 'new_v_conv_ln_b': 'new_v', 'new_v_conv_w_pw2': 'new_v', 'new_v_conv_b_pw2': 'new_v', 'new_v_ffn_w_gate': 'new_v', 'new_v_ffn_w_up': 'new_v', 'new_v_ffn_w_down': 'new_v', 'new_v_ln_mix_g': 'new_v', 'new_v_ln_mix_b': 'new_v', 'new_v_ln_ffn_g': 'new_v', 'new_v_ln_ffn_b': 'new_v'}


def _forward(args):
    return _fwd_reference(*[args[k] for k in FWD_PARAMS])


def _output_shape():
    out = _jax.eval_shape(lambda: _forward(_fwd_setup_inputs(0)))
    return out.shape, out.dtype

N_MICROBATCH = 1
ADAM_LR = 0.001
ADAM_B1 = 0.9
ADAM_B2 = 0.999
ADAM_EPS = 1e-08
ADAM_WD = 0.01
ADAM_STEP = 10
PER_EXAMPLE_BATCH_AXIS = {'x': 0, 'loss_target': 0}
SHARED_INPUTS = []
_WEIGHT_DTYPES = {'fox_w_qkv': _jnp.float32, 'fox_w_f': _jnp.float32, 'fox_b_f': _jnp.float32, 'fox_w_o': _jnp.float32, 'rel_w_qkv': _jnp.float32, 'rel_bias': _jnp.float32, 'rel_w_o': _jnp.float32, 'conv_w_pw1': _jnp.float32, 'conv_b_pw1': _jnp.float32, 'conv_w_dw': _jnp.float32, 'conv_b_dw': _jnp.float32, 'conv_ln_g': _jnp.float32, 'conv_ln_b': _jnp.float32, 'conv_w_pw2': _jnp.float32, 'conv_b_pw2': _jnp.float32, 'ffn_w_gate': _jnp.float32, 'ffn_w_up': _jnp.float32, 'ffn_w_down': _jnp.float32, 'ln_mix_g': _jnp.float32, 'ln_mix_b': _jnp.float32, 'ln_ffn_g': _jnp.float32, 'ln_ffn_b': _jnp.float32}
MOMENT_SCALE = {'fox_w_qkv': 3.405299e-03, 'fox_w_f': 4.531085e-03, 'fox_b_f': 1.409050e-02, 'fox_w_o': 5.336517e-03, 'rel_w_qkv': 1.651327e-03, 'rel_bias': 6.636980e-04, 'rel_w_o': 2.511064e-03, 'conv_w_pw1': 7.465593e-03, 'conv_b_pw1': 1.243923e-02, 'conv_w_dw': 9.818496e-03, 'conv_b_dw': 3.147543e-02, 'conv_ln_g': 1.567213e-02, 'conv_ln_b': 1.829021e-02, 'conv_w_pw2': 2.667729e-02, 'conv_b_pw2': 8.822884e-02, 'ffn_w_gate': 5.855917e-03, 'ffn_w_up': 5.671655e-03, 'ffn_w_down': 2.237743e-02, 'ln_mix_g': 2.826563e-01, 'ln_mix_b': 1.487640e-01, 'ln_ffn_g': 4.025292e+00, 'ln_ffn_b': 3.085606e-01}


def _to_microbatches(a, axis):
    t = _jnp.moveaxis(a, axis, 0)
    t = t.reshape((N_MICROBATCH, t.shape[0] // N_MICROBATCH) + t.shape[1:])
    return _jnp.moveaxis(t, 1, axis + 1)


def setup_inputs(seed: int = 0) -> dict:
    inp = _fwd_setup_inputs(seed)
    key = _jax.random.fold_in(_jax.random.key(seed), 7919)
    shape, _ = _output_shape()
    out = dict(inp)
    out["loss_target"] = _jax.random.normal(_jax.random.fold_in(key, 0), shape, _jnp.float32)
    for i, name in enumerate(TWIN_WEIGHTS):
        w = inp[name].astype(_jnp.float32)
        if MOMENT_SCALE is None:
            s = _jnp.sqrt(_jnp.mean(_jnp.square(w)) + 1e-30)
        else:
            s = MOMENT_SCALE[name]
        km, kv = _jax.random.split(_jax.random.fold_in(key, i + 1))
        out[name] = w
        out["m_" + name] = s * _jax.random.normal(km, w.shape, _jnp.float32)
        out["v_" + name] = (s * s) * _jax.random.uniform(kv, w.shape, _jnp.float32, 0.5, 1.5)
    if N_MICROBATCH > 1:
        for name, axis in PER_EXAMPLE_BATCH_AXIS.items():
            out[name] = _to_microbatches(out[name], axis)
    return {'x': out['x'], 'fox_w_qkv': out['fox_w_qkv'], 'fox_w_f': out['fox_w_f'], 'fox_b_f': out['fox_b_f'], 'fox_w_o': out['fox_w_o'], 'rel_w_qkv': out['rel_w_qkv'], 'rel_bias': out['rel_bias'], 'rel_w_o': out['rel_w_o'], 'conv_w_pw1': out['conv_w_pw1'], 'conv_b_pw1': out['conv_b_pw1'], 'conv_w_dw': out['conv_w_dw'], 'conv_b_dw': out['conv_b_dw'], 'conv_ln_g': out['conv_ln_g'], 'conv_ln_b': out['conv_ln_b'], 'conv_w_pw2': out['conv_w_pw2'], 'conv_b_pw2': out['conv_b_pw2'], 'ffn_w_gate': out['ffn_w_gate'], 'ffn_w_up': out['ffn_w_up'], 'ffn_w_down': out['ffn_w_down'], 'ln_mix_g': out['ln_mix_g'], 'ln_mix_b': out['ln_mix_b'], 'ln_ffn_g': out['ln_ffn_g'], 'ln_ffn_b': out['ln_ffn_b'], 'loss_target': out['loss_target'], 'm_fox_w_qkv': out['m_fox_w_qkv'], 'm_fox_w_f': out['m_fox_w_f'], 'm_fox_b_f': out['m_fox_b_f'], 'm_fox_w_o': out['m_fox_w_o'], 'm_rel_w_qkv': out['m_rel_w_qkv'], 'm_rel_bias': out['m_rel_bias'], 'm_rel_w_o': out['m_rel_w_o'], 'm_conv_w_pw1': out['m_conv_w_pw1'], 'm_conv_b_pw1': out['m_conv_b_pw1'], 'm_conv_w_dw': out['m_conv_w_dw'], 'm_conv_b_dw': out['m_conv_b_dw'], 'm_conv_ln_g': out['m_conv_ln_g'], 'm_conv_ln_b': out['m_conv_ln_b'], 'm_conv_w_pw2': out['m_conv_w_pw2'], 'm_conv_b_pw2': out['m_conv_b_pw2'], 'm_ffn_w_gate': out['m_ffn_w_gate'], 'm_ffn_w_up': out['m_ffn_w_up'], 'm_ffn_w_down': out['m_ffn_w_down'], 'm_ln_mix_g': out['m_ln_mix_g'], 'm_ln_mix_b': out['m_ln_mix_b'], 'm_ln_ffn_g': out['m_ln_ffn_g'], 'm_ln_ffn_b': out['m_ln_ffn_b'], 'v_fox_w_qkv': out['v_fox_w_qkv'], 'v_fox_w_f': out['v_fox_w_f'], 'v_fox_b_f': out['v_fox_b_f'], 'v_fox_w_o': out['v_fox_w_o'], 'v_rel_w_qkv': out['v_rel_w_qkv'], 'v_rel_bias': out['v_rel_bias'], 'v_rel_w_o': out['v_rel_w_o'], 'v_conv_w_pw1': out['v_conv_w_pw1'], 'v_conv_b_pw1': out['v_conv_b_pw1'], 'v_conv_w_dw': out['v_conv_w_dw'], 'v_conv_b_dw': out['v_conv_b_dw'], 'v_conv_ln_g': out['v_conv_ln_g'], 'v_conv_ln_b': out['v_conv_ln_b'], 'v_conv_w_pw2': out['v_conv_w_pw2'], 'v_conv_b_pw2': out['v_conv_b_pw2'], 'v_ffn_w_gate': out['v_ffn_w_gate'], 'v_ffn_w_up': out['v_ffn_w_up'], 'v_ffn_w_down': out['v_ffn_w_down'], 'v_ln_mix_g': out['v_ln_mix_g'], 'v_ln_mix_b': out['v_ln_mix_b'], 'v_ln_ffn_g': out['v_ln_ffn_g'], 'v_ln_ffn_b': out['v_ln_ffn_b']}


def _loss(weights, diff, rest, loss_target):
    with _jax.named_scope("forward"):
        args = {**rest, TWIN_DIFF_INPUT: diff, **{k: w.astype(_WEIGHT_DTYPES[k]) for k, w in weights.items()}}
        y = _forward(args)
    with _jax.named_scope("loss_head"):
        err = _jnp.square(y.astype(_jnp.float32) - loss_target)
        return 0.5 * _jnp.sum(_jnp.mean(err, axis=-1)) if err.ndim else 0.5 * err


def _adamw(w, g, m, v):
    m = ADAM_B1 * m + (1.0 - ADAM_B1) * g
    v = ADAM_B2 * v + (1.0 - ADAM_B2) * _jnp.square(g)
    m_hat = m / (1.0 - ADAM_B1 ** ADAM_STEP)
    v_hat = v / (1.0 - ADAM_B2 ** ADAM_STEP)
    delta = -ADAM_LR * (m_hat / (_jnp.sqrt(v_hat) + ADAM_EPS) + ADAM_WD * w)
    return delta, m, v


def reference(x, fox_w_qkv, fox_w_f, fox_b_f, fox_w_o, rel_w_qkv, rel_bias, rel_w_o, conv_w_pw1, conv_b_pw1, conv_w_dw, conv_b_dw, conv_ln_g, conv_ln_b, conv_w_pw2, conv_b_pw2, ffn_w_gate, ffn_w_up, ffn_w_down, ln_mix_g, ln_mix_b, ln_ffn_g, ln_ffn_b, loss_target, m_fox_w_qkv, m_fox_w_f, m_fox_b_f, m_fox_w_o, m_rel_w_qkv, m_rel_bias, m_rel_w_o, m_conv_w_pw1, m_conv_b_pw1, m_conv_w_dw, m_conv_b_dw, m_conv_ln_g, m_conv_ln_b, m_conv_w_pw2, m_conv_b_pw2, m_ffn_w_gate, m_ffn_w_up, m_ffn_w_down, m_ln_mix_g, m_ln_mix_b, m_ln_ffn_g, m_ln_ffn_b, v_fox_w_qkv, v_fox_w_f, v_fox_b_f, v_fox_w_o, v_rel_w_qkv, v_rel_bias, v_rel_w_o, v_conv_w_pw1, v_conv_b_pw1, v_conv_w_dw, v_conv_b_dw, v_conv_ln_g, v_conv_ln_b, v_conv_w_pw2, v_conv_b_pw2, v_ffn_w_gate, v_ffn_w_up, v_ffn_w_down, v_ln_mix_g, v_ln_mix_b, v_ln_ffn_g, v_ln_ffn_b):
    given = dict(x=x, fox_w_qkv=fox_w_qkv, fox_w_f=fox_w_f, fox_b_f=fox_b_f, fox_w_o=fox_w_o, rel_w_qkv=rel_w_qkv, rel_bias=rel_bias, rel_w_o=rel_w_o, conv_w_pw1=conv_w_pw1, conv_b_pw1=conv_b_pw1, conv_w_dw=conv_w_dw, conv_b_dw=conv_b_dw, conv_ln_g=conv_ln_g, conv_ln_b=conv_ln_b, conv_w_pw2=conv_w_pw2, conv_b_pw2=conv_b_pw2, ffn_w_gate=ffn_w_gate, ffn_w_up=ffn_w_up, ffn_w_down=ffn_w_down, ln_mix_g=ln_mix_g, ln_mix_b=ln_mix_b, ln_ffn_g=ln_ffn_g, ln_ffn_b=ln_ffn_b, loss_target=loss_target, m_fox_w_qkv=m_fox_w_qkv, m_fox_w_f=m_fox_w_f, m_fox_b_f=m_fox_b_f, m_fox_w_o=m_fox_w_o, m_rel_w_qkv=m_rel_w_qkv, m_rel_bias=m_rel_bias, m_rel_w_o=m_rel_w_o, m_conv_w_pw1=m_conv_w_pw1, m_conv_b_pw1=m_conv_b_pw1, m_conv_w_dw=m_conv_w_dw, m_conv_b_dw=m_conv_b_dw, m_conv_ln_g=m_conv_ln_g, m_conv_ln_b=m_conv_ln_b, m_conv_w_pw2=m_conv_w_pw2, m_conv_b_pw2=m_conv_b_pw2, m_ffn_w_gate=m_ffn_w_gate, m_ffn_w_up=m_ffn_w_up, m_ffn_w_down=m_ffn_w_down, m_ln_mix_g=m_ln_mix_g, m_ln_mix_b=m_ln_mix_b, m_ln_ffn_g=m_ln_ffn_g, m_ln_ffn_b=m_ln_ffn_b, v_fox_w_qkv=v_fox_w_qkv, v_fox_w_f=v_fox_w_f, v_fox_b_f=v_fox_b_f, v_fox_w_o=v_fox_w_o, v_rel_w_qkv=v_rel_w_qkv, v_rel_bias=v_rel_bias, v_rel_w_o=v_rel_w_o, v_conv_w_pw1=v_conv_w_pw1, v_conv_b_pw1=v_conv_b_pw1, v_conv_w_dw=v_conv_w_dw, v_conv_b_dw=v_conv_b_dw, v_conv_ln_g=v_conv_ln_g, v_conv_ln_b=v_conv_ln_b, v_conv_w_pw2=v_conv_w_pw2, v_conv_b_pw2=v_conv_b_pw2, v_ffn_w_gate=v_ffn_w_gate, v_ffn_w_up=v_ffn_w_up, v_ffn_w_down=v_ffn_w_down, v_ln_mix_g=v_ln_mix_g, v_ln_mix_b=v_ln_mix_b, v_ln_ffn_g=v_ln_ffn_g, v_ln_ffn_b=v_ln_ffn_b)
    weights = {n: given[n] for n in TWIN_WEIGHTS}
    shared = {n: given[n] for n in SHARED_INPUTS}
    per_example = {n: given[n] for n in ['x']}
    grad_fn = _jax.value_and_grad(_loss, argnums=(0, 1))

    def one_microbatch(ex, loss_target):
        ex = dict(ex)
        diff = ex.pop(TWIN_DIFF_INPUT)
        return grad_fn(weights, diff, {**shared, **ex}, loss_target)

    if N_MICROBATCH == 1:
        loss, (grad_w, grad_x) = one_microbatch(per_example, given["loss_target"])
    else:
        def body(carry, xs):
            loss_sum, grad_sum = carry
            l_k, (gw_k, gx_k) = one_microbatch(xs[0], xs[1])
            with _jax.named_scope("update"):
                return (loss_sum + l_k, _jax.tree.map(_jnp.add, grad_sum, gw_k)), gx_k

        init = (_jnp.zeros((), _jnp.float32), _jax.tree.map(_jnp.zeros_like, weights))
        (loss, grad_w), grad_x = _jax.lax.scan(body, init, (per_example, given["loss_target"]))
    with _jax.named_scope("update"):
        delta_w, new_m, new_v = {}, {}, {}
        for n in TWIN_WEIGHTS:
            delta_w[n], new_m[n], new_v[n] = _adamw(weights[n], grad_w[n], given["m_" + n], given["v_" + n])
    return (loss, grad_x, *[grad_w[n] for n in TWIN_WEIGHTS], *[delta_w[n] for n in TWIN_WEIGHTS],
            *[new_m[n] for n in TWIN_WEIGHTS], *[new_v[n] for n in TWIN_WEIGHTS])
```

```python
import functools
import math

import jax
import jax.numpy as jnp
from jax import lax
from jax.experimental import pallas as pl
from jax.experimental.pallas import tpu as pltpu

F32 = jnp.float32
BF16 = jnp.bfloat16
MESH_IDS = pl.DeviceIdType.MESH
HIGHEST = lax.Precision.HIGHEST

N_CHIPS = 4
DEPTH = 4
N_FOX = 2
HEAD_DIM = 128
CHUNK = 64
LEFT_CHUNKS = 8
BAND_KEYS = (LEFT_CHUNKS + 1) * CHUNK
PAD_KEYS = LEFT_CHUNKS * CHUNK
REL_CLIP = 128
REL_TABLE = 2 * REL_CLIP + 1
REL_TABLE_PAD = 384
CONV_K = 31
CONV_HALO = 32
ALPHA = (2.0 * DEPTH) ** 0.25
LN_EPS = 1e-5
ADAM_LR, ADAM_B1, ADAM_B2, ADAM_EPS, ADAM_WD, ADAM_STEP = 0.001, 0.9, 0.999, 1e-08, 0.01, 10
NEG_BIG = -1e30
VMEM_LIMIT_V7X = 56 * 1024 * 1024
LANES = 128
SUBLANES = 8


def _cparams(*sem):
    return pltpu.CompilerParams(dimension_semantics=sem if sem else None, vmem_limit_bytes=VMEM_LIMIT_V7X)


def _pick(dim, target):
    best = None
    for t in range(LANES, min(dim, target) + 1, LANES):
        if dim % t == 0:
            best = t
    return best if best is not None else dim


def _dot_nt(a, b):
    return lax.dot_general(a, b, (((1,), (1,)), ((), ())), preferred_element_type=F32)


def _dot_tn(a, b):
    return lax.dot_general(a, b, (((0,), (0,)), ((), ())), preferred_element_type=F32)


def _sigmoid(z):
    return 1.0 / (1.0 + jnp.exp(-z))


def _rowwise(name, fn, ins, outs, S, tm=256):
    tm = min(tm, S)
    arrs, in_specs = [], []
    for it in ins:
        if isinstance(it, tuple) and it[0] == 'full':
            a = it[1]
            in_specs.append(pl.BlockSpec(a.shape, lambda i, _n=a.ndim: (0,) * _n))
        elif isinstance(it, tuple) and it[0] == 'cols':
            _, a, width, blk = it
            in_specs.append(pl.BlockSpec((tm, width), lambda i, _b=blk: (i, _b)))
        elif isinstance(it, tuple) and it[0] == 'off':
            _, a, off = it
            in_specs.append(pl.BlockSpec((tm, a.shape[1]), lambda i, _o=off: (i + _o, 0)))
        else:
            a = it
            in_specs.append(pl.BlockSpec((tm, a.shape[1]), lambda i: (i, 0)))
        arrs.append(a)
    out_shape, out_specs = [], []
    for kind, shp, dt in outs:
        if kind == 'rows':
            out_shape.append(jax.ShapeDtypeStruct((S, shp), dt))
            out_specs.append(pl.BlockSpec((tm, shp), lambda i: (i, 0)))
        else:
            out_shape.append(jax.ShapeDtypeStruct(shp, dt))
            out_specs.append(pl.BlockSpec(shp, lambda i, _n=len(shp): (0,) * _n))
    n_in = len(arrs)

    def body(*refs):
        vals = fn(*[r[...] for r in refs[:n_in]])
        first = pl.program_id(0) == 0
        for (kind, _, _), r, v in zip(outs, refs[n_in:], vals):
            if kind == 'rows':
                r[...] = v.astype(r.dtype)
            else:
                @pl.when(first)
                def _(r=r, v=v):
                    r[...] = v.astype(r.dtype)

                @pl.when(jnp.logical_not(first))
                def _(r=r, v=v):
                    r[...] += v.astype(r.dtype)

    has_acc = any(k != 'rows' for k, _, _ in outs)
    res = pl.pallas_call(
        body, name=name, grid=(S // tm,), in_specs=in_specs, out_specs=out_specs, out_shape=out_shape,
        compiler_params=_cparams("arbitrary" if has_acc else "parallel"),
    )(*arrs)
    return res


def _mm_col(name, a, wg, row_start, bias=None, out_dtype=BF16):
    S, K = a.shape
    _, _, Ns = wg.shape
    rb = row_start // K
    tm = min(512, S)

    def body(a_ref, w_ref, *rest):
        acc = jnp.dot(a_ref[...].astype(BF16), w_ref[...], preferred_element_type=F32)
        if bias is not None:
            acc = acc + rest[0][...]
        rest[-1][...] = acc.astype(out_dtype)

    in_specs = [pl.BlockSpec((tm, K), lambda j, m: (m, 0)), pl.BlockSpec((None, K, Ns), lambda j, m: (j, rb, 0))]
    args = [a, wg]
    if bias is not None:
        in_specs.append(pl.BlockSpec((1, Ns), lambda j, m: (0, j)))
        args.append(bias)
    return pl.pallas_call(
        body, name=name, grid=(N_CHIPS, S // tm), in_specs=in_specs,
        out_specs=pl.BlockSpec((tm, Ns), lambda j, m: (m, j)),
        out_shape=jax.ShapeDtypeStruct((S, N_CHIPS * Ns), out_dtype),
        compiler_params=_cparams("parallel", "parallel"),
    )(*args)


def _mm_row(name, a, wg, row_start, Ks, bias=None):
    S = a.shape[0]
    N = wg.shape[2]
    rb = row_start // Ks
    tm = min(512, S)

    def body(a_ref, w_ref, *rest):
        o_ref = rest[-1]
        j = pl.program_id(1)
        d = jnp.dot(a_ref[...].astype(BF16), w_ref[...], preferred_element_type=F32)

        @pl.when(j == 0)
        def _():
            o_ref[...] = d + rest[0][...] if bias is not None else d

        @pl.when(j > 0)
        def _():
            o_ref[...] += d

    in_specs = [pl.BlockSpec((tm, Ks), lambda m, j: (m, j)), pl.BlockSpec((None, Ks, N), lambda m, j: (j, rb, 0))]
    args = [a, wg]
    if bias is not None:
        in_specs.append(pl.BlockSpec((1, N), lambda m, j: (0, 0)))
        args.append(bias)
    return pl.pallas_call(
        body, name=name, grid=(S // tm, N_CHIPS), in_specs=in_specs,
        out_specs=pl.BlockSpec((tm, N), lambda m, j: (m, 0)),
        out_shape=jax.ShapeDtypeStruct((S, N), F32),
        compiler_params=_cparams("parallel", "arbitrary"),
    )(*args)


def _mm_col_t(name, pairs, wg, K):
    S = pairs[0][0].shape[0]
    Ns = wg.shape[2]
    tm = min(512, S)
    n = len(pairs)

    def body(*refs):
        o_ref = refs[-1]
        j = pl.program_id(1)
        d = _dot_nt(refs[0][...], refs[n][...])
        for p in range(1, n):
            d = d + _dot_nt(refs[p][...], refs[n + p][...])

        @pl.when(j == 0)
        def _():
            o_ref[...] = d

        @pl.when(j > 0)
        def _():
            o_ref[...] += d

    in_specs = [pl.BlockSpec((tm, Ns), lambda m, j: (m, j)) for _ in pairs]
    in_specs += [pl.BlockSpec((None, K, Ns), lambda m, j, _rb=rs // K: (j, _rb, 0)) for _, rs in pairs]
    return pl.pallas_call(
        body, name=name, grid=(S // tm, N_CHIPS), in_specs=in_specs,
        out_specs=pl.BlockSpec((tm, K), lambda m, j: (m, 0)),
        out_shape=jax.ShapeDtypeStruct((S, K), F32),
        compiler_params=_cparams("parallel", "arbitrary"),
    )(*[dy for dy, _ in pairs], *[wg for _ in pairs])


def _mm_row_t(name, dy, wg, row_start, Ks, out_dtype):
    S, N = dy.shape
    rb = row_start // Ks
    tm = min(512, S)

    def body(dy_ref, w_ref, o_ref):
        o_ref[...] = _dot_nt(dy_ref[...], w_ref[...]).astype(out_dtype)

    return pl.pallas_call(
        body, name=name, grid=(N_CHIPS, S // tm),
        in_specs=[pl.BlockSpec((tm, N), lambda j, m: (m, 0)), pl.BlockSpec((None, Ks, N), lambda j, m: (j, rb, 0))],
        out_specs=pl.BlockSpec((tm, Ks), lambda j, m: (m, j)),
        out_shape=jax.ShapeDtypeStruct((S, N_CHIPS * Ks), out_dtype),
        compiler_params=_cparams("parallel", "parallel"),
    )(dy, wg)


def _mm_dw(name, a, dy, dg, row_start, kind):
    S = a.shape[0]
    _, _, W = dg.shape
    if kind == 'col':
        K = a.shape[1]
        rows = K
        tk, tn = _pick(K, 512), W
        a_map = lambda j, nb, kb: (0, kb)
        dy_map = lambda j, nb, kb: (0, j * (W // tn) + nb)
    else:
        rows = a.shape[1] // N_CHIPS
        tk = rows if rows * S * 2 * 2 <= 12 * 1024 * 1024 else _pick(rows, 512)
        tn = _pick(W, 1024)
        a_map = lambda j, nb, kb: (0, j * (rows // tk) + kb)
        dy_map = lambda j, nb, kb: (0, nb)
    rb = row_start // tk
    assert row_start % tk == 0

    def body(a_ref, dy_ref, dg_in, o_ref):
        del dg_in
        o_ref[...] = _dot_tn(a_ref[...], dy_ref[...]).astype(o_ref.dtype)

    return pl.pallas_call(
        body, name=name, grid=(N_CHIPS, W // tn, rows // tk),
        in_specs=[pl.BlockSpec((S, tk), a_map), pl.BlockSpec((S, tn), dy_map), pl.BlockSpec(memory_space=pl.ANY)],
        out_specs=pl.BlockSpec((None, tk, tn), lambda j, nb, kb: (j, rb + kb, nb)),
        out_shape=jax.ShapeDtypeStruct(dg.shape, dg.dtype),
        input_output_aliases={2: 0},
        compiler_params=_cparams("parallel", "parallel", "parallel"),
    )(a, dy, dg)


def _fox_probs(q, k, c_blk, crow, h, qi, tq, S):
    s = _dot_nt(q, k) * (HEAD_DIM ** -0.5)
    lane = lax.broadcasted_iota(jnp.int32, c_blk.shape, 1)
    ccol = jnp.sum(jnp.where(lane == h, c_blk, 0.0), axis=1, keepdims=True)
    s = s + (ccol - crow)
    t_idx = qi * tq + lax.broadcasted_iota(jnp.int32, (tq, S), 0)
    s_idx = lax.broadcasted_iota(jnp.int32, (tq, S), 1)
    s = jnp.where(s_idx <= t_idx, s, NEG_BIG)
    p = jnp.exp(s - jnp.max(s, axis=1, keepdims=True))
    return p / jnp.sum(p, axis=1, keepdims=True)


def _fox_fwd(qkv, c, crow, H):
    S = qkv.shape[0]
    tq = min(256, S)

    def body(q_ref, k_ref, v_ref, c_ref, crow_ref, o_ref):
        p = _fox_probs(q_ref[...], k_ref[...], c_ref[...], crow_ref[...], pl.program_id(0), pl.program_id(1), tq, S)
        o_ref[...] = jnp.dot(p.astype(BF16), v_ref[...], preferred_element_type=F32).astype(o_ref.dtype)

    return pl.pallas_call(
        body, name="fox_attn_fwd", grid=(H, S // tq),
        in_specs=[pl.BlockSpec((tq, HEAD_DIM), lambda h, i: (i, h)),
                  pl.BlockSpec((S, HEAD_DIM), lambda h, i: (0, H + h)),
                  pl.BlockSpec((S, HEAD_DIM), lambda h, i: (0, 2 * H + h)),
                  pl.BlockSpec((tq, H), lambda h, i: (i, 0)),
                  pl.BlockSpec((None, 1, S), lambda h, i: (h, 0, 0))],
        out_specs=pl.BlockSpec((tq, HEAD_DIM), lambda h, i: (i, h)),
        out_shape=jax.ShapeDtypeStruct((S, H * HEAD_DIM), BF16),
        compiler_params=_cparams("parallel", "parallel"),
    )(qkv, qkv, qkv, c, crow)


def _fox_bwd(qkv, c, crow, do, H):
    S = qkv.shape[0]
    tq = min(256, S)
    nq = S // tq

    def body(q_ref, k_ref, v_ref, c_ref, crow_ref, do_ref, dq_ref, dk_ref, dv_ref, dc_ref, dk_acc, dv_acc):
        qi = pl.program_id(1)
        q, k, v, do_ = q_ref[...], k_ref[...], v_ref[...], do_ref[...]
        p = _fox_probs(q, k, c_ref[...], crow_ref[...], pl.program_id(0), qi, tq, S)
        dv_part = _dot_tn(p.astype(BF16), do_)
        dp = _dot_nt(do_, v)
        ds = p * (dp - jnp.sum(p * dp, axis=1, keepdims=True))
        dsb = (ds * (HEAD_DIM ** -0.5)).astype(BF16)
        dq_ref[...] = jnp.dot(dsb, k, preferred_element_type=F32).astype(dq_ref.dtype)
        dk_part = _dot_tn(dsb, q)
        dc_part = -jnp.sum(ds, axis=0, keepdims=True)

        @pl.when(qi == 0)
        def _():
            dk_acc[...] = dk_part
            dv_acc[...] = dv_part
            dc_ref[...] = dc_part

        @pl.when(qi > 0)
        def _():
            dk_acc[...] += dk_part
            dv_acc[...] += dv_part
            dc_ref[...] += dc_part

        @pl.when(qi == nq - 1)
        def _():
            dk_ref[...] = dk_acc[...].astype(dk_ref.dtype)
            dv_ref[...] = dv_acc[...].astype(dv_ref.dtype)

    D = H * HEAD_DIM
    return pl.pallas_call(
        body, name="fox_attn_bwd", grid=(H, nq),
        in_specs=[pl.BlockSpec((tq, HEAD_DIM), lambda h, i: (i, h)),
                  pl.BlockSpec((S, HEAD_DIM), lambda h, i: (0, H + h)),
                  pl.BlockSpec((S, HEAD_DIM), lambda h, i: (0, 2 * H + h)),
                  pl.BlockSpec((tq, H), lambda h, i: (i, 0)),
                  pl.BlockSpec((None, 1, S), lambda h, i: (h, 0, 0)),
                  pl.BlockSpec((tq, HEAD_DIM), lambda h, i: (i, h))],
        out_specs=[pl.BlockSpec((tq, HEAD_DIM), lambda h, i: (i, h)),
                   pl.BlockSpec((S, HEAD_DIM), lambda h, i: (0, h)),
                   pl.BlockSpec((S, HEAD_DIM), lambda h, i: (0, h)),
                   pl.BlockSpec((None, 1, S), lambda h, i: (h, 0, 0))],
        out_shape=[jax.ShapeDtypeStruct((S, D), BF16), jax.ShapeDtypeStruct((S, D), BF16),
                   jax.ShapeDtypeStruct((S, D), BF16), jax.ShapeDtypeStruct((H, 1, S), F32)],
        scratch_shapes=[pltpu.VMEM((S, HEAD_DIM), F32), pltpu.VMEM((S, HEAD_DIM), F32)],
        compiler_params=_cparams("parallel", "arbitrary"),
    )(qkv, qkv, qkv, c, crow, do)


def _cumsum_rows(name, xin, reverse):
    S, H = xin.shape
    tb = min(256, S)
    nb = S // tb

    def body(x_ref, o_ref):
        r = lax.broadcasted_iota(jnp.int32, (tb, tb), 0)
        cidx = lax.broadcasted_iota(jnp.int32, (tb, tb), 1)
        tri = (r <= cidx if reverse else r >= cidx).astype(F32)

        def step(b, carry):
            bb = nb - 1 - b if reverse else b
            rows = pl.ds(pl.multiple_of(bb * tb, tb), tb)
            blk = x_ref[rows, :]
            o_ref[rows, :] = jnp.dot(tri, blk, precision=HIGHEST, preferred_element_type=F32) + carry
            return carry + jnp.sum(blk, axis=0, keepdims=True)

        lax.fori_loop(0, nb, step, jnp.zeros((1, H), F32))

    return pl.pallas_call(
        body, name=name, out_shape=jax.ShapeDtypeStruct((S, H), F32),
        in_specs=[pl.BlockSpec(memory_space=pltpu.VMEM)], out_specs=pl.BlockSpec(memory_space=pltpu.VMEM),
        compiler_params=_cparams(),
    )(xin)


def _rel_onehot(i, transposed):
    shp = (REL_TABLE_PAD, BAND_KEYS) if transposed else (BAND_KEYS, REL_TABLE_PAD)
    j = lax.broadcasted_iota(jnp.int32, shp, 1 if transposed else 0)
    r = lax.broadcasted_iota(jnp.int32, shp, 0 if transposed else 1)
    return (jnp.clip(PAD_KEYS + i - j, -REL_CLIP, REL_CLIP) + REL_CLIP == r).astype(F32)


def _rel_expand(rb_pad):
    H = rb_pad.shape[0]

    def body(rb_ref, o_ref):
        def step(i, _):
            o_ref[i] = jnp.dot(rb_ref[...], _rel_onehot(i, True), precision=HIGHEST, preferred_element_type=F32)
            return 0
        lax.fori_loop(0, CHUNK, step, 0)

    return pl.pallas_call(
        body, name="rel_bias_expand", out_shape=jax.ShapeDtypeStruct((CHUNK, H, BAND_KEYS), F32),
        in_specs=[pl.BlockSpec(memory_space=pltpu.VMEM)], out_specs=pl.BlockSpec(memory_space=pltpu.VMEM),
        compiler_params=_cparams(),
    )(rb_pad)


def _rel_reduce(dbt):
    H = dbt.shape[1]

    def body(d_ref, o_ref):
        def step(i, acc):
            return acc + jnp.dot(d_ref[i], _rel_onehot(i, False), precision=HIGHEST, preferred_element_type=F32)
        o_ref[...] = lax.fori_loop(0, CHUNK, step, jnp.zeros((H, REL_TABLE_PAD), F32))

    return pl.pallas_call(
        body, name="rel_bias_reduce", out_shape=jax.ShapeDtypeStruct((H, REL_TABLE_PAD), F32),
        in_specs=[pl.BlockSpec(memory_space=pltpu.VMEM)], out_specs=pl.BlockSpec(memory_space=pltpu.VMEM),
        compiler_params=_cparams(),
    )(dbt)


def _rel_probs(q, kb, bias, n):
    s = _dot_nt(q, kb) * (HEAD_DIM ** -0.5) + bias
    j = lax.broadcasted_iota(jnp.int32, (CHUNK, BAND_KEYS), 1)
    s = jnp.where(j >= PAD_KEYS - n * CHUNK, s, NEG_BIG)
    p = jnp.exp(s - jnp.max(s, axis=1, keepdims=True))
    return p / jnp.sum(p, axis=1, keepdims=True)


def _rel_fwd(qkv, bias, H):
    S = qkv.shape[0]

    def body(q_ref, k_ref, v_ref, b_ref, o_ref, kpad, vpad):
        kpad[0:PAD_KEYS, :] = jnp.zeros((PAD_KEYS, HEAD_DIM), BF16)
        vpad[0:PAD_KEYS, :] = jnp.zeros((PAD_KEYS, HEAD_DIM), BF16)
        kpad[PAD_KEYS:PAD_KEYS + S, :] = k_ref[...]
        vpad[PAD_KEYS:PAD_KEYS + S, :] = v_ref[...]
        bias_t = b_ref[...]

        def chunk(n, _):
            rows = pl.ds(pl.multiple_of(n * CHUNK, CHUNK), CHUNK)
            band = pl.ds(pl.multiple_of(n * CHUNK, CHUNK), BAND_KEYS)
            p = _rel_probs(q_ref[rows, :], kpad[band, :], bias_t, n)
            o_ref[rows, :] = jnp.dot(p.astype(BF16), vpad[band, :], preferred_element_type=F32).astype(o_ref.dtype)
            return 0

        lax.fori_loop(0, S // CHUNK, chunk, 0)

    return pl.pallas_call(
        body, name="rel_attn_fwd", grid=(H,),
        in_specs=[pl.BlockSpec((S, HEAD_DIM), lambda h: (0, h)),
                  pl.BlockSpec((S, HEAD_DIM), lambda h: (0, H + h)),
                  pl.BlockSpec((S, HEAD_DIM), lambda h: (0, 2 * H + h)),
                  pl.BlockSpec((None, CHUNK, BAND_KEYS), lambda h: (h, 0, 0))],
        out_specs=pl.BlockSpec((S, HEAD_DIM), lambda h: (0, h)),
        out_shape=jax.ShapeDtypeStruct((S, H * HEAD_DIM), BF16),
        scratch_shapes=[pltpu.VMEM((S + PAD_KEYS, HEAD_DIM), BF16), pltpu.VMEM((S + PAD_KEYS, HEAD_DIM), BF16)],
        compiler_params=_cparams("parallel"),
    )(qkv, qkv, qkv, bias)


def _rel_bwd(qkv, bias, do, H):
    S = qkv.shape[0]
    D = H * HEAD_DIM

    def body(q_ref, k_ref, v_ref, b_ref, do_ref, dq_ref, dk_ref, dv_ref, db_ref, kpad, vpad, dkpad, dvpad):
        kpad[0:PAD_KEYS, :] = jnp.zeros((PAD_KEYS, HEAD_DIM), BF16)
        vpad[0:PAD_KEYS, :] = jnp.zeros((PAD_KEYS, HEAD_DIM), BF16)
        kpad[PAD_KEYS:PAD_KEYS + S, :] = k_ref[...]
        vpad[PAD_KEYS:PAD_KEYS + S, :] = v_ref[...]
        dkpad[...] = jnp.zeros_like(dkpad)
        dvpad[...] = jnp.zeros_like(dvpad)
        db_ref[...] = jnp.zeros_like(db_ref)
        bias_t = b_ref[...]

        def chunk(n, _):
            rows = pl.ds(pl.multiple_of(n * CHUNK, CHUNK), CHUNK)
            band = pl.ds(pl.multiple_of(n * CHUNK, CHUNK), BAND_KEYS)
            q, kb, vb, do_ = q_ref[rows, :], kpad[band, :], vpad[band, :], do_ref[rows, :]
            p = _rel_probs(q, kb, bias_t, n)
            dvpad[band, :] += _dot_tn(p.astype(BF16), do_)
            dp = _dot_nt(do_, vb)
            ds = p * (dp - jnp.sum(p * dp, axis=1, keepdims=True))
            db_ref[...] += ds
            dsb = (ds * (HEAD_DIM ** -0.5)).astype(BF16)
            dq_ref[rows, :] = jnp.dot(dsb, kb, preferred_element_type=F32).astype(dq_ref.dtype)
            dkpad[band, :] += _dot_tn(dsb, q)
            return 0

        lax.fori_loop(0, S // CHUNK, chunk, 0)
        dk_ref[...] = dkpad[PAD_KEYS:PAD_KEYS + S, :].astype(dk_ref.dtype)
        dv_ref[...] = dvpad[PAD_KEYS:PAD_KEYS + S, :].astype(dv_ref.dtype)

    head = lambda h: (0, h)
    return pl.pallas_call(
        body, name="rel_attn_bwd", grid=(H,),
        in_specs=[pl.BlockSpec((S, HEAD_DIM), head),
                  pl.BlockSpec((S, HEAD_DIM), lambda h: (0, H + h)),
                  pl.BlockSpec((S, HEAD_DIM), lambda h: (0, 2 * H + h)),
                  pl.BlockSpec((None, CHUNK, BAND_KEYS), lambda h: (h, 0, 0)),
                  pl.BlockSpec((S, HEAD_DIM), head)],
        out_specs=[pl.BlockSpec((S, HEAD_DIM), head), pl.BlockSpec((S, HEAD_DIM), head), pl.BlockSpec((S, HEAD_DIM), head),
                   pl.BlockSpec((None, CHUNK, BAND_KEYS), lambda h: (h, 0, 0))],
        out_shape=[jax.ShapeDtypeStruct((S, D), BF16), jax.ShapeDtypeStruct((S, D), BF16), jax.ShapeDtypeStruct((S, D), BF16),
                   jax.ShapeDtypeStruct((H, CHUNK, BAND_KEYS), F32)],
        scratch_shapes=[pltpu.VMEM((S + PAD_KEYS, HEAD_DIM), BF16), pltpu.VMEM((S + PAD_KEYS, HEAD_DIM), BF16),
                        pltpu.VMEM((S + PAD_KEYS, HEAD_DIM), F32), pltpu.VMEM((S + PAD_KEYS, HEAD_DIM), F32)],
        compiler_params=_cparams("parallel"),
    )(qkv, qkv, qkv, bias, do)


def _conv_taps(win, tt, reverse):
    n = tt + 2 * CONV_HALO
    for k in range(CONV_K):
        off = (CONV_K - 1 - k) if reverse else (k - (CONV_K - 1))
        sh = (-off) % n
        rolled = pltpu.roll(win, sh, 0) if sh else win
        yield k, rolled[CONV_HALO:CONV_HALO + tt, :]


def _fill_padded(pad_ref, x_ref, S):
    tc = pad_ref.shape[1]
    pad_ref[0:CONV_HALO, :] = jnp.zeros((CONV_HALO, tc), F32)
    pad_ref[CONV_HALO + S:CONV_HALO + S + CONV_HALO, :] = jnp.zeros((CONV_HALO, tc), F32)
    pad_ref[CONV_HALO:CONV_HALO + S, :] = x_ref[...]


def _dwconv(name, xin, w32, bias, reverse):
    S, D = xin.shape
    tc = min(256, D)
    tt = min(256, S)

    def body(x_ref, w_ref, b_ref, y_ref, pad_ref):
        _fill_padded(pad_ref, x_ref, S)
        def tile(ti, _):
            t0 = pl.multiple_of(ti * tt, tt)
            win = pad_ref[pl.ds(t0, tt + 2 * CONV_HALO), :]
            acc = jnp.zeros((tt, tc), F32) + b_ref[...]
            for k, shifted in _conv_taps(win, tt, reverse):
                acc = acc + w_ref[pl.ds(k, 1), :] * shifted
            y_ref[pl.ds(t0, tt), :] = acc
            return 0

        lax.fori_loop(0, S // tt, tile, 0)

    return pl.pallas_call(
        body, name=name, grid=(D // tc,),
        in_specs=[pl.BlockSpec((S, tc), lambda i: (0, i)), pl.BlockSpec((CONV_HALO, tc), lambda i: (0, i)),
                  pl.BlockSpec((1, tc), lambda i: (0, i))],
        out_specs=pl.BlockSpec((S, tc), lambda i: (0, i)),
        out_shape=jax.ShapeDtypeStruct((S, D), F32),
        scratch_shapes=[pltpu.VMEM((S + 2 * CONV_HALO, tc), F32)],
        compiler_params=_cparams("parallel"),
    )(xin, w32, bias)


def _dwconv_dw(xin, dy):
    S, D = xin.shape
    tc = min(256, D)
    tt = min(256, S)

    def body(x_ref, dy_ref, o_ref, pad_ref):
        _fill_padded(pad_ref, x_ref, S)

        def tile(ti, acc):
            t0 = pl.multiple_of(ti * tt, tt)
            win = pad_ref[pl.ds(t0, tt + 2 * CONV_HALO), :]
            dyt = dy_ref[pl.ds(t0, tt), :]
            ridx = lax.broadcasted_iota(jnp.int32, (CONV_HALO, tc), 0)
            upd = jnp.zeros((CONV_HALO, tc), F32)
            for k, shifted in _conv_taps(win, tt, False):
                upd = jnp.where(ridx == k, jnp.sum(dyt * shifted, axis=0, keepdims=True), upd)
            return acc + upd

        o_ref[...] = lax.fori_loop(0, S // tt, tile, jnp.zeros((CONV_HALO, tc), F32))

    return pl.pallas_call(
        body, name="dwconv_dw", grid=(D // tc,),
        in_specs=[pl.BlockSpec((S, tc), lambda i: (0, i)), pl.BlockSpec((S, tc), lambda i: (0, i))],
        out_specs=pl.BlockSpec((CONV_HALO, tc), lambda i: (0, i)),
        out_shape=jax.ShapeDtypeStruct((CONV_HALO, D), F32),
        scratch_shapes=[pltpu.VMEM((S + 2 * CONV_HALO, tc), F32)],
        compiler_params=_cparams("parallel"),
    )(xin, dy)


def _place():
    x, y, c = lax.axis_index("x"), lax.axis_index("y"), lax.axis_index("c")
    chips = [(1 - x, y), (x, 1 - y), (1 - x, 1 - y)]
    return x, y, c, chips


def _remote(src, dst, ssem, rsem, dev):
    return pltpu.make_async_remote_copy(src_ref=src, dst_ref=dst, send_sem=ssem, recv_sem=rsem,
                                        device_id=dev, device_id_type=MESH_IDS)


_ANY = pl.BlockSpec(memory_space=pl.ANY)


def _gather_group(name, own):
    R, W = own.shape
    Rh = R // 2

    def body(own_ref, out_ref, ssems, rsems, lsem):
        x, y, c, chips = _place()
        me = 2 * x + y
        sib = (x, y, 1 - c)
        mine = pl.ds(pl.multiple_of(c * Rh, SUBLANES), Rh)
        theirs = pl.ds(pl.multiple_of((1 - c) * Rh, SUBLANES), Rh)
        local = pltpu.make_async_copy(own_ref, out_ref.at[me], lsem)
        local.start()
        sends = []
        for j, (px, py) in enumerate(chips):
            cp = _remote(own_ref.at[mine], out_ref.at[me, mine], ssems.at[j], rsems.at[j], (px, py, c))
            cp.start()
            sends.append(cp)
        for j, (px, py) in enumerate(chips):
            blk = out_ref.at[2 * px + py, mine]
            _remote(blk, blk, ssems.at[j], rsems.at[j], (px, py, c)).wait_recv()
            fw = _remote(blk, blk, ssems.at[3 + j], rsems.at[3 + j], sib)
            fw.start()
            sends.append(fw)
        for j, (px, py) in enumerate(chips):
            blk = out_ref.at[2 * px + py, theirs]
            _remote(blk, blk, ssems.at[3 + j], rsems.at[3 + j], sib).wait_recv()
        for cp in sends:
            cp.wait_send()
        local.wait()

    return pl.pallas_call(
        body, name=name, in_specs=[_ANY], out_specs=_ANY,
        out_shape=jax.ShapeDtypeStruct((N_CHIPS, R, W), own.dtype),
        scratch_shapes=[pltpu.SemaphoreType.DMA((6,)), pltpu.SemaphoreType.DMA((6,)), pltpu.SemaphoreType.DMA],
        compiler_params=pltpu.CompilerParams(has_side_effects=True),
    )(own)


def _swap_halves(name, dg):
    _, R, W = dg.shape
    Rh = R // 2

    def body(src, land, ssems, rsems):
        x, y, c, _ = _place()
        theirs = pl.ds(pl.multiple_of((1 - c) * Rh, SUBLANES), Rh)
        cps = [_remote(src.at[j, theirs], land.at[j], ssems.at[j], rsems.at[j], (x, y, 1 - c)) for j in range(N_CHIPS)]
        for cp in cps:
            cp.start()
        for cp in cps:
            cp.wait()

    return pl.pallas_call(
        body, name=name, in_specs=[_ANY], out_specs=_ANY,
        out_shape=jax.ShapeDtypeStruct((N_CHIPS, Rh, W), dg.dtype),
        scratch_shapes=[pltpu.SemaphoreType.DMA((N_CHIPS,)), pltpu.SemaphoreType.DMA((N_CHIPS,))],
        compiler_params=pltpu.CompilerParams(has_side_effects=True),
    )(dg)


def _scatter_chips(name, pb):
    def body(src, land, ssems, rsems, lsem):
        x, y, c, chips = _place()
        me = 2 * x + y
        local = pltpu.make_async_copy(src.at[me], land.at[me], lsem)
        local.start()
        cps = [_remote(src.at[2 * px + py], land.at[me], ssems.at[j], rsems.at[j], (px, py, c))
               for j, (px, py) in enumerate(chips)]
        for cp in cps:
            cp.start()
        for j, (px, py) in enumerate(chips):
            blk = land.at[2 * px + py]
            _remote(blk, blk, ssems.at[j], rsems.at[j], (px, py, c)).wait_recv()
        for cp in cps:
            cp.wait_send()
        local.wait()

    return pl.pallas_call(
        body, name=name, in_specs=[_ANY], out_specs=_ANY,
        out_shape=jax.ShapeDtypeStruct(pb.shape, pb.dtype),
        scratch_shapes=[pltpu.SemaphoreType.DMA((3,)), pltpu.SemaphoreType.DMA((3,)), pltpu.SemaphoreType.DMA],
        compiler_params=pltpu.CompilerParams(has_side_effects=True),
    )(pb)


def _share_halves(name, gh):
    Rh, W = gh.shape

    def body(src, out, ssem, rsem, lsem):
        x, y, c, _ = _place()
        mine = pl.ds(pl.multiple_of(c * Rh, SUBLANES), Rh)
        theirs = pl.ds(pl.multiple_of((1 - c) * Rh, SUBLANES), Rh)
        local = pltpu.make_async_copy(src, out.at[mine], lsem)
        local.start()
        cp = _remote(src, out.at[mine], ssem, rsem, (x, y, 1 - c))
        cp.start()
        _remote(src, out.at[theirs], ssem, rsem, (x, y, 1 - c)).wait_recv()
        cp.wait_send()
        local.wait()

    return pl.pallas_call(
        body, name=name, in_specs=[_ANY], out_specs=_ANY,
        out_shape=jax.ShapeDtypeStruct((2 * Rh, W), gh.dtype),
        scratch_shapes=[pltpu.SemaphoreType.DMA, pltpu.SemaphoreType.DMA, pltpu.SemaphoreType.DMA],
        compiler_params=pltpu.CompilerParams(has_side_effects=True),
    )(gh)


def _broadcast_small(name, buf):
    R = buf.shape[0]

    def body(src, out, ssems, rsems):
        x, y, c, _ = _place()
        me = 4 * x + 2 * y + c
        out[me] = src[...]
        peers = []
        for mask in range(1, 8):
            fx, fy, fc = (mask >> 2) & 1, (mask >> 1) & 1, mask & 1
            peers.append((1 - x if fx else x, 1 - y if fy else y, 1 - c if fc else c))
        cps = [_remote(src, out.at[me], ssems.at[k], rsems.at[k], p) for k, p in enumerate(peers)]
        for cp in cps:
            cp.start()
        for k, (px, py, pc) in enumerate(peers):
            blk = out.at[4 * px + 2 * py + pc]
            _remote(blk, blk, ssems.at[k], rsems.at[k], (px, py, pc)).wait_recv()
        for cp in cps:
            cp.wait_send()

    return pl.pallas_call(
        body, name=name, in_specs=[pl.BlockSpec(memory_space=pltpu.VMEM)], out_specs=pl.BlockSpec(memory_space=pltpu.VMEM),
        out_shape=jax.ShapeDtypeStruct((8, R, LANES), F32),
        scratch_shapes=[pltpu.SemaphoreType.DMA((7,)), pltpu.SemaphoreType.DMA((7,))],
        compiler_params=pltpu.CompilerParams(has_side_effects=True, vmem_limit_bytes=VMEM_LIMIT_V7X),
    )(buf)


def _sum_slabs(name, slabs):
    n, R, _ = slabs.shape

    def body(s_ref, o_ref):
        acc = s_ref[0]
        for k in range(1, n):
            acc = acc + s_ref[k]
        o_ref[...] = acc

    return pl.pallas_call(
        body, name=name, out_shape=jax.ShapeDtypeStruct((R, LANES), F32),
        in_specs=[pl.BlockSpec(memory_space=pltpu.VMEM)], out_specs=pl.BlockSpec(memory_space=pltpu.VMEM),
        compiler_params=_cparams(),
    )(slabs)


def _pair_sum(name, dg, land, c_arr):
    _, R, W = dg.shape
    Rh = R // 2
    tr = _pick_rows(Rh)
    nb = Rh // tr

    def body(c_ref, a_ref, b_ref, o_ref):
        del c_ref
        o_ref[...] = (a_ref[...].astype(F32) + b_ref[...].astype(F32)).astype(o_ref.dtype)

    return pl.pallas_call(
        body, name=name,
        grid_spec=pltpu.PrefetchScalarGridSpec(
            num_scalar_prefetch=1, grid=(N_CHIPS, nb),
            in_specs=[pl.BlockSpec((None, tr, W), lambda j, i, c_ref: (j, c_ref[0] * nb + i, 0)),
                      pl.BlockSpec((None, tr, W), lambda j, i, c_ref: (j, i, 0))],
            out_specs=pl.BlockSpec((None, tr, W), lambda j, i, c_ref: (j, i, 0))),
        out_shape=jax.ShapeDtypeStruct((N_CHIPS, Rh, W), BF16),
        compiler_params=_cparams("parallel", "parallel"),
    )(c_arr, dg, land)


def _pick_rows(rows, target=512):
    best = SUBLANES
    for t in range(SUBLANES, min(rows, target) + 1, SUBLANES):
        if rows % t == 0:
            best = t
    return best


def _chip_sum(name, land):
    _, Rh, W = land.shape
    tr = _pick_rows(Rh)

    def body(l_ref, o_ref):
        acc = l_ref[0].astype(F32)
        for k in range(1, N_CHIPS):
            acc = acc + l_ref[k].astype(F32)
        o_ref[...] = acc

    return pl.pallas_call(
        body, name=name, grid=(Rh // tr,),
        in_specs=[pl.BlockSpec((N_CHIPS, tr, W), lambda i: (0, i, 0))],
        out_specs=pl.BlockSpec((tr, W), lambda i: (i, 0)),
        out_shape=jax.ShapeDtypeStruct((Rh, W), F32),
        compiler_params=_cparams("parallel"),
    )(land)


def _ln_stats(z):
    mu = jnp.mean(z, axis=1, keepdims=True)
    zc = z - mu
    rstd = lax.rsqrt(jnp.mean(zc * zc, axis=1, keepdims=True) + LN_EPS)
    return zc * rstd, rstd


def _ln_fwd(name, xin, m, g, b):
    S, D = xin.shape

    def fn(x_, m_, g_, b_):
        xhat, rstd = _ln_stats(ALPHA * x_ + m_)
        y = xhat * g_ + b_
        return y, y, xhat, rstd

    return _rowwise(name, fn, [xin, m, ('full', g), ('full', b)],
                    [('rows', D, F32), ('rows', D, BF16), ('rows', D, F32), ('rows', 1, F32)], S)


def _ln_bwd_core(dy, xhat, rstd, g):
    dxh = dy * g
    return rstd * (dxh - jnp.mean(dxh, axis=1, keepdims=True) - xhat * jnp.mean(dxh * xhat, axis=1, keepdims=True))


def _ln_bwd(name, terms, xhat, rstd, g):
    S, D = xhat.shape
    scales = [s for _, s in terms]
    n = len(terms)

    def fn(*v):
        dy = v[0] * scales[0] if scales[0] != 1.0 else v[0]
        for t in range(1, n):
            dy = dy + (v[t] * scales[t] if scales[t] != 1.0 else v[t])
        xh, rs, g_ = v[n], v[n + 1], v[n + 2]
        dz = _ln_bwd_core(dy, xh, rs, g_)
        return dz, dz, jnp.sum(dy * xh, axis=0, keepdims=True), jnp.sum(dy, axis=0, keepdims=True)

    return _rowwise(name, fn, [a for a, _ in terms] + [xhat, rstd, ('full', g)],
                    [('rows', D, F32), ('rows', D, BF16), ('acc', (1, D), F32), ('acc', (1, D), F32)], S)


def _adamw_math(w, g, m, v):
    m2 = ADAM_B1 * m + (1.0 - ADAM_B1) * g
    v2 = ADAM_B2 * v + (1.0 - ADAM_B2) * (g * g)
    m_hat = m2 / (1.0 - ADAM_B1 ** ADAM_STEP)
    v_hat = v2 / (1.0 - ADAM_B2 ** ADAM_STEP)
    delta = -ADAM_LR * (m_hat / (jnp.sqrt(v_hat) + ADAM_EPS) + ADAM_WD * w)
    return delta, m2, v2


def _adamw(name, w, gfull, row_start, m, v):
    rows, W = w.shape
    tr = math.gcd(math.gcd(rows, row_start), 256) if row_start else math.gcd(rows, 256)

    def fn(w_, g_, m_, v_):
        d, m2, v2 = _adamw_math(w_, g_, m_, v_)
        return g_, d, m2, v2

    return _rowwise(name, fn, [w, ('off', gfull, row_start // tr), m, v], [('rows', W, F32)] * 4, rows, tm=tr)


def _pack(arrs):
    flat = jnp.concatenate([a.reshape(-1).astype(F32) for a in arrs])
    tile = SUBLANES * LANES
    n = -(-flat.shape[0] // tile) * tile
    return jnp.pad(flat, (0, n - flat.shape[0])).reshape(-1, LANES)


def _unpack(buf, shapes):
    flat = buf.reshape(-1)
    out, pos = [], 0
    for shp in shapes:
        n = math.prod(shp)
        out.append(flat[pos:pos + n].reshape(shp))
        pos += n
    return out


BIG = ['fox_w_qkv', 'fox_w_o', 'rel_w_qkv', 'rel_w_o', 'conv_w_pw1', 'conv_w_pw2', 'ffn_w_gate', 'ffn_w_up', 'ffn_w_down']
SMALL_SHARDED = ['fox_w_f', 'conv_b_pw1', 'conv_w_dw', 'conv_b_dw', 'conv_ln_g', 'conv_ln_b', 'conv_b_pw2']
SMALL_SHARD_AXIS = {'fox_w_f': 1, 'conv_b_pw1': 1, 'conv_w_dw': 2, 'conv_b_dw': 1, 'conv_ln_g': 1, 'conv_ln_b': 1, 'conv_b_pw2': 1}
SMALL_REPL = ['fox_b_f', 'rel_bias', 'ln_mix_g', 'ln_mix_b', 'ln_ffn_g', 'ln_ffn_b']
SMALL = SMALL_SHARDED + SMALL_REPL
WEIGHTS = ['fox_w_qkv', 'fox_w_f', 'fox_b_f', 'fox_w_o', 'rel_w_qkv', 'rel_bias', 'rel_w_o', 'conv_w_pw1', 'conv_b_pw1',
           'conv_w_dw', 'conv_b_dw', 'conv_ln_g', 'conv_ln_b', 'conv_w_pw2', 'conv_b_pw2', 'ffn_w_gate', 'ffn_w_up',
           'ffn_w_down', 'ln_mix_g', 'ln_mix_b', 'ln_ffn_g', 'ln_ffn_b']


def kernel(x, fox_w_qkv, fox_w_f, fox_b_f, fox_w_o, rel_w_qkv, rel_bias, rel_w_o, conv_w_pw1, conv_b_pw1, conv_w_dw, conv_b_dw, conv_ln_g, conv_ln_b, conv_w_pw2, conv_b_pw2, ffn_w_gate, ffn_w_up, ffn_w_down, ln_mix_g, ln_mix_b, ln_ffn_g, ln_ffn_b, loss_target, m_fox_w_qkv, m_fox_w_f, m_fox_b_f, m_fox_w_o, m_rel_w_qkv, m_rel_bias, m_rel_w_o, m_conv_w_pw1, m_conv_b_pw1, m_conv_w_dw, m_conv_b_dw, m_conv_ln_g, m_conv_ln_b, m_conv_w_pw2, m_conv_b_pw2, m_ffn_w_gate, m_ffn_w_up, m_ffn_w_down, m_ln_mix_g, m_ln_mix_b, m_ln_ffn_g, m_ln_ffn_b, v_fox_w_qkv, v_fox_w_f, v_fox_b_f, v_fox_w_o, v_rel_w_qkv, v_rel_bias, v_rel_w_o, v_conv_w_pw1, v_conv_b_pw1, v_conv_w_dw, v_conv_b_dw, v_conv_ln_g, v_conv_ln_b, v_conv_w_pw2, v_conv_b_pw2, v_ffn_w_gate, v_ffn_w_up, v_ffn_w_down, v_ln_mix_g, v_ln_mix_b, v_ln_ffn_g, v_ln_ffn_b):
    A = dict(locals())
    Wt = {n: A[n] for n in WEIGHTS}
    Mo = {n: A['m_' + n] for n in WEIGHTS}
    Vo = {n: A['v_' + n] for n in WEIGHTS}

    _, S, D = x.shape
    H = D // HEAD_DIM
    Ds = D // N_CHIPS
    Nq = fox_w_qkv.shape[2]
    Np = conv_w_pw1.shape[2]
    Fs = ffn_w_gate.shape[2]
    my_x, my_y, my_c = lax.axis_index("x"), lax.axis_index("y"), lax.axis_index("c")
    my_chip = 2 * my_x + my_y
    c_arr = jnp.reshape(my_c, (1,)).astype(jnp.int32)

    wo_base = DEPTH * Fs
    where = {
        'fox_w_qkv': ('qkv', 0), 'rel_w_qkv': ('qkv', N_FOX * D),
        'ffn_w_gate': ('ffn', 0), 'ffn_w_up': ('ffn', DEPTH * D),
        'conv_w_pw1': ('pw1', 0),
        'ffn_w_down': ('dm', 0), 'fox_w_o': ('dm', wo_base), 'rel_w_o': ('dm', wo_base + N_FOX * Ds),
        'conv_w_pw2': ('dm', wo_base + (N_FOX + 1) * Ds),
    }
    members = {'qkv': ['fox_w_qkv', 'rel_w_qkv'], 'ffn': ['ffn_w_gate', 'ffn_w_up'], 'pw1': ['conv_w_pw1'],
               'dm': ['ffn_w_down', 'fox_w_o', 'rel_w_o', 'conv_w_pw2']}
    flat2 = lambda a: a.reshape(-1, a.shape[-1])
    own = {g: jnp.concatenate([flat2(Wt[n]).astype(BF16) for n in ms], axis=0) for g, ms in members.items()}
    WG = {g: _gather_group("gather_" + g, own[g]) for g in own}
    DG = {g: lax.empty(WG[g].shape, BF16) for g in own}

    small_shapes = [Wt[n].shape for n in SMALL_SHARDED]
    slabs = _broadcast_small("gather_small", _pack([Wt[n] for n in SMALL_SHARDED]))
    per_chip = [_unpack(slabs[2 * j], small_shapes) for j in range(N_CHIPS)]
    full = {n: jnp.concatenate([per_chip[j][i] for j in range(N_CHIPS)], axis=SMALL_SHARD_AXIS[n])
            for i, n in enumerate(SMALL_SHARDED)}
    row = lambda v: v.reshape(1, -1)

    SG = {}

    def ffn_fwd(i, xb):
        hg = _mm_col(f"ffn{i}_gate", xb, WG['ffn'], i * D)
        hu = _mm_col(f"ffn{i}_up", xb, WG['ffn'], (DEPTH + i) * D)
        act, = _rowwise(f"ffn{i}_act", lambda g_, u_: [g_.astype(F32) * _sigmoid(g_.astype(F32)) * u_.astype(F32)],
                        [hg, hu], [('rows', N_CHIPS * Fs, BF16)], S)
        f = _mm_row(f"ffn{i}_down", act, WG['dm'], i * Fs, Fs)
        return f, (hg, hu, act)

    def ffn_bwd(i, xb, saved, dzb):
        hg, hu, act = saved
        DG['dm'] = _mm_dw(f"ffn{i}_dw_down", act, dzb, DG['dm'], i * Fs, 'row')
        dact = _mm_row_t(f"ffn{i}_dact", dzb, WG['dm'], i * Fs, Fs, F32)

        def fn(da_, g_, u_):
            g32, u32 = g_.astype(F32), u_.astype(F32)
            sg = _sigmoid(g32)
            silu = g32 * sg
            return da_ * u32 * (sg * (1.0 + g32 * (1.0 - sg))), da_ * silu

        dhg, dhu = _rowwise(f"ffn{i}_dact_split", fn, [dact, hg, hu],
                            [('rows', N_CHIPS * Fs, BF16), ('rows', N_CHIPS * Fs, BF16)], S)
        DG['ffn'] = _mm_dw(f"ffn{i}_dw_gate", xb, dhg, DG['ffn'], i * D, 'col')
        DG['ffn'] = _mm_dw(f"ffn{i}_dw_up", xb, dhu, DG['ffn'], (DEPTH + i) * D, 'col')
        return _mm_col_t(f"ffn{i}_dx", [(dhg, i * D), (dhu, (DEPTH + i) * D)], WG['ffn'], D)

    def fox_fwd(j, xb):
        qkv = _mm_col(f"fox{j}_qkv", xb, WG['qkv'], j * D)
        wf = full['fox_w_f'][j].astype(BF16)
        def gate_fn(x_, w_, b_):
            z_ = jnp.dot(x_, w_, preferred_element_type=F32) + b_
            return z_, jnp.minimum(z_, 0.0) - jnp.log(1.0 + jnp.exp(-jnp.abs(z_)))

        z, logf = _rowwise(f"fox{j}_gate", gate_fn, [xb, ('full', wf), ('full', row(fox_b_f[j]))],
                           [('rows', H, F32), ('rows', H, F32)], S)
        c = _cumsum_rows(f"fox{j}_cumsum", logf, False)
        crow = c.T.reshape(H, 1, S)
        o = _fox_fwd(qkv, c, crow, H)
        m = _mm_row(f"fox{j}_wo", o, WG['dm'], wo_base + j * Ds, Ds)
        return m, (qkv, z, c, crow, o, wf)

    def fox_bwd(j, xb, saved, dzb):
        qkv, z, c, crow, o, wf = saved
        DG['dm'] = _mm_dw(f"fox{j}_dw_o", o, dzb, DG['dm'], wo_base + j * Ds, 'row')
        do = _mm_row_t(f"fox{j}_do", dzb, WG['dm'], wo_base + j * Ds, Ds, BF16)
        dq, dk, dv, dcrow = _fox_bwd(qkv, c, crow, do, H)
        dqkv = jnp.concatenate([dq, dk, dv], axis=1)
        dlogf = _cumsum_rows(f"fox{j}_rcumsum", dcrow.reshape(H, S).T, True)

        def fn(x_, dl_, z_, w_):
            dz_ = dl_ * _sigmoid(-z_)
            dzb_ = dz_.astype(BF16)
            return _dot_nt(dzb_, w_), _dot_tn(x_, dzb_), jnp.sum(dz_, axis=0, keepdims=True)

        dh_f, dwf, dbf = _rowwise(f"fox{j}_gate_bwd", fn, [xb, dlogf, z, ('full', wf)],
                                  [('rows', D, F32), ('acc', (D, H), F32), ('acc', (1, H), F32)], S)
        SG.setdefault('fox_w_f', [None] * N_FOX)[j] = dwf
        SG.setdefault('fox_b_f', [None] * N_FOX)[j] = dbf.reshape(H)
        DG['qkv'] = _mm_dw(f"fox{j}_dw_qkv", xb, dqkv, DG['qkv'], j * D, 'col')
        dh = _mm_col_t(f"fox{j}_dx", [(dqkv, j * D)], WG['qkv'], D)
        return [dh, dh_f]

    def rel_fwd(xb):
        qkv = _mm_col("rel_qkv", xb, WG['qkv'], N_FOX * D)
        rb_pad = jnp.pad(rel_bias[0], ((0, 0), (0, REL_TABLE_PAD - REL_TABLE)))
        bias = jnp.transpose(_rel_expand(rb_pad), (1, 0, 2))
        o = _rel_fwd(qkv, bias, H)
        m = _mm_row("rel_wo", o, WG['dm'], wo_base + N_FOX * Ds, Ds)
        return m, (qkv, bias, o)

    def rel_bwd(xb, saved, dzb):
        qkv, bias, o = saved
        DG['dm'] = _mm_dw("rel_dw_o", o, dzb, DG['dm'], wo_base + N_FOX * Ds, 'row')
        do = _mm_row_t("rel_do", dzb, WG['dm'], wo_base + N_FOX * Ds, Ds, BF16)
        dq, dk, dv, dbias = _rel_bwd(qkv, bias, do, H)
        SG['rel_bias'] = _rel_reduce(jnp.transpose(dbias, (1, 0, 2)))[:, :REL_TABLE].reshape(1, H, REL_TABLE)
        dqkv = jnp.concatenate([dq, dk, dv], axis=1)
        DG['qkv'] = _mm_dw("rel_dw_qkv", xb, dqkv, DG['qkv'], N_FOX * D, 'col')
        return [_mm_col_t("rel_dx", [(dqkv, N_FOX * D)], WG['qkv'], D)]

    w_dw32 = jnp.pad(full['conv_w_dw'][0], ((0, CONV_HALO - CONV_K), (0, 0)))
    cg, cb = full['conv_ln_g'], full['conv_ln_b']

    def conv_fwd(xb):
        u = _mm_col("conv_pw1", xb, WG['pw1'], 0, bias=full['conv_b_pw1'], out_dtype=F32)
        u2, = _rowwise("conv_glu", lambda a_, g_: [a_ * _sigmoid(g_)],
                       [('cols', u, D, 0), ('cols', u, D, 1)], [('rows', D, F32)], S)
        yc = _dwconv("conv_dw", u2, w_dw32, full['conv_b_dw'], False)

        def fn(y_, g_, b_):
            xhat, rstd = _ln_stats(y_)
            ln = xhat * g_ + b_
            return ln * _sigmoid(ln), xhat, rstd

        zc, xhat, rstd = _rowwise("conv_ln_silu", fn, [yc, ('full', cg), ('full', cb)],
                                  [('rows', D, BF16), ('rows', D, F32), ('rows', 1, F32)], S)
        m = _mm_row("conv_pw2", zc, WG['dm'], wo_base + (N_FOX + 1) * Ds, Ds, bias=full['conv_b_pw2'])
        return m, (u, u2, zc, xhat, rstd)

    def conv_bwd(xb, saved, dz, dzb):
        u, u2, zc, xhat, rstd = saved
        r0 = wo_base + (N_FOX + 1) * Ds
        DG['dm'] = _mm_dw("conv_dw_pw2", zc, dzb, DG['dm'], r0, 'row')
        dzc = _mm_row_t("conv_dzc", dzb, WG['dm'], r0, Ds, F32)

        def fn(dm_, dzc_, xh_, rs_, g_, b_):
            ln = xh_ * g_ + b_
            sg = _sigmoid(ln)
            dln = dzc_ * (sg * (1.0 + ln * (1.0 - sg)))
            dyc = _ln_bwd_core(dln, xh_, rs_, g_)
            col = lambda t: jnp.sum(t, axis=0, keepdims=True)
            return dyc, col(dm_), col(dln * xh_), col(dln), col(dyc)

        dyc, SG['conv_b_pw2'], SG['conv_ln_g'], SG['conv_ln_b'], SG['conv_b_dw'] = _rowwise(
            "conv_ln_silu_bwd", fn, [dz, dzc, xhat, rstd, ('full', cg), ('full', cb)],
            [('rows', D, F32)] + [('acc', (1, D), F32)] * 4, S)
        du2 = _dwconv("conv_dw_bwd_x", dyc, w_dw32, jnp.zeros((1, D), F32), True)
        SG['conv_w_dw'] = _dwconv_dw(u2, dyc)[:CONV_K].reshape(1, CONV_K, D)

        def fn2(du2_, a_, g_):
            sg = _sigmoid(g_)
            da, dgt = du2_ * sg, du2_ * a_ * sg * (1.0 - sg)
            return da, dgt, jnp.sum(da, axis=0, keepdims=True), jnp.sum(dgt, axis=0, keepdims=True)

        da, dgt, dba, dbg = _rowwise("conv_glu_bwd", fn2, [du2, ('cols', u, D, 0), ('cols', u, D, 1)],
                                     [('rows', D, BF16), ('rows', D, BF16), ('acc', (1, D), F32), ('acc', (1, D), F32)], S)
        SG['conv_b_pw1'] = jnp.concatenate([dba, dbg], axis=1)
        du = jnp.concatenate([da, dgt], axis=1)
        DG['pw1'] = _mm_dw("conv_dw_pw1", xb, du, DG['pw1'], 0, 'col')
        return [_mm_col_t("conv_dx", [(du, 0)], WG['pw1'], D)]

    xs = x[0]
    xs_b = xs.astype(BF16)
    tape = []
    for i in range(DEPTH):
        kind, j = i % 3, i // 3
        if kind == 0:
            m, msaved = fox_fwd(j, xs_b)
        elif kind == 1:
            m, msaved = rel_fwd(xs_b)
        else:
            m, msaved = conv_fwd(xs_b)
        xm, xm_b, xhat1, rstd1 = _ln_fwd(f"ln_mix{i}", xs, m, row(ln_mix_g[i]), row(ln_mix_b[i]))
        f, fsaved = ffn_fwd(i, xm_b)
        xo, xo_b, xhat2, rstd2 = _ln_fwd(f"ln_ffn{i}", xm, f, row(ln_ffn_g[i]), row(ln_ffn_b[i]))
        tape.append((xs_b, msaved, xhat1, rstd1, xm_b, fsaved, xhat2, rstd2))
        xs, xs_b = xo, xo_b

    def loss_fn(y_, t_):
        e = y_ - t_
        return e * (1.0 / D), jnp.sum(e * e, axis=0, keepdims=True)

    dy, sq = _rowwise("loss", loss_fn, [xs, loss_target[0]], [('rows', D, F32), ('acc', (1, D), F32)], S)
    loss = lax.psum(jnp.sum(sq) * (0.5 / D), ("x", "y", "c"))

    terms = [(dy, 1.0)]
    g_mix, b_mix, g_ffn, b_ffn = [None] * DEPTH, [None] * DEPTH, [None] * DEPTH, [None] * DEPTH
    for i in reversed(range(DEPTH)):
        kind, j = i % 3, i // 3
        xin_b, msaved, xhat1, rstd1, xm_b, fsaved, xhat2, rstd2 = tape[i]
        dz2, dz2b, g_ffn[i], b_ffn[i] = _ln_bwd(f"ln_ffn{i}_bwd", terms, xhat2, rstd2, row(ln_ffn_g[i]))
        dx_ffn = ffn_bwd(i, xm_b, fsaved, dz2b)
        dz1, dz1b, g_mix[i], b_mix[i] = _ln_bwd(f"ln_mix{i}_bwd", [(dz2, ALPHA), (dx_ffn, 1.0)], xhat1, rstd1, row(ln_mix_g[i]))
        if kind == 0:
            mix_terms = fox_bwd(j, xin_b, msaved, dz1b)
        elif kind == 1:
            mix_terms = rel_bwd(xin_b, msaved, dz1b)
        else:
            mix_terms = conv_bwd(xin_b, msaved, dz1, dz1b)
        terms = [(dz1, ALPHA)] + [(t, 1.0) for t in mix_terms]

    def gx_fn(*v):
        acc = v[0] * ALPHA
        for t in v[1:]:
            acc = acc + t
        return [acc]

    grad_x, = _rowwise("grad_x", gx_fn, [a for a, _ in terms], [('rows', D, F32)], S)
    grad_x = grad_x.reshape(1, S, D)

    SG['fox_w_f'] = jnp.stack(SG['fox_w_f'])
    SG['fox_b_f'] = jnp.stack(SG['fox_b_f'])
    SG['ln_mix_g'] = jnp.concatenate(g_mix, axis=0)
    SG['ln_mix_b'] = jnp.concatenate(b_mix, axis=0)
    SG['ln_ffn_g'] = jnp.concatenate(g_ffn, axis=0)
    SG['ln_ffn_b'] = jnp.concatenate(b_ffn, axis=0)

    grads, deltas, new_m, new_v = {}, {}, {}, {}

    GF = {}
    for g in own:
        land = _swap_halves("pair_swap_" + g, DG[g])
        pb = _pair_sum("pair_sum_" + g, DG[g], land, c_arr)
        land2 = _scatter_chips("chip_scatter_" + g, pb)
        gh = _chip_sum("chip_sum_" + g, land2)
        GF[g] = _share_halves("pair_share_" + g, gh)

    for n in BIG:
        g, r0 = where[n]
        outs = _adamw("adamw_" + n, flat2(Wt[n]), GF[g], r0, flat2(Mo[n]), flat2(Vo[n]))
        grads[n], deltas[n], new_m[n], new_v[n] = [o.reshape(Wt[n].shape) for o in outs]

    full_shapes = [SG[n].shape for n in SMALL]
    summed = _sum_slabs("small_sum", _broadcast_small("small_exchange", _pack([SG[n] for n in SMALL])))
    gsm = dict(zip(SMALL, _unpack(summed, full_shapes)))
    for n in SMALL_SHARDED:
        ax = SMALL_SHARD_AXIS[n]
        width = Wt[n].shape[ax]
        gsm[n] = lax.dynamic_slice_in_dim(gsm[n], my_chip * width, width, axis=ax)
    own_shapes = [Wt[n].shape for n in SMALL]
    packed = [_pack([src[n] for n in SMALL]) for src in (Wt, gsm, Mo, Vo)]
    rows_small = packed[0].shape[0]

    def small_fn(w_, g_, m_, v_):
        return _adamw_math(w_, g_, m_, v_)

    sd, sm, sv = _rowwise("adamw_small", small_fn, packed, [('rows', LANES, F32)] * 3, rows_small, tm=rows_small)
    for n, d_, m_, v_ in zip(SMALL, _unpack(sd, own_shapes), _unpack(sm, own_shapes), _unpack(sv, own_shapes)):
        grads[n], deltas[n], new_m[n], new_v[n] = gsm[n], d_, m_, v_

    return (loss, grad_x, *[grads[n] for n in WEIGHTS], *[deltas[n] for n in WEIGHTS],
            *[new_m[n] for n in WEIGHTS], *[new_v[n] for n in WEIGHTS])
```

```python
import functools
import math

import jax
import jax.numpy as jnp
from jax import lax
from jax.experimental import pallas as pl
from jax.experimental.pallas import tpu as pltpu

F32 = jnp.float32
BF16 = jnp.bfloat16
MESH_IDS = pl.DeviceIdType.MESH
HIGHEST = lax.Precision.HIGHEST

N_CHIPS = 4
DEPTH = 4
N_FOX = 2
HEAD_DIM = 128
CHUNK = 64
LEFT_CHUNKS = 8
BAND_KEYS = (LEFT_CHUNKS + 1) * CHUNK
PAD_KEYS = LEFT_CHUNKS * CHUNK
REL_CLIP = 128
REL_TABLE = 2 * REL_CLIP + 1
REL_TABLE_PAD = 384
CONV_K = 31
CONV_HALO = 32
ALPHA = (2.0 * DEPTH) ** 0.25
LN_EPS = 1e-5
ADAM_LR, ADAM_B1, ADAM_B2, ADAM_EPS, ADAM_WD, ADAM_STEP = 0.001, 0.9, 0.999, 1e-08, 0.01, 10
NEG_BIG = -1e30
VMEM_LIMIT_V7X = 56 * 1024 * 1024
LANES = 128
SUBLANES = 8


def _cparams(*sem):
    return pltpu.CompilerParams(dimension_semantics=sem if sem else None, vmem_limit_bytes=VMEM_LIMIT_V7X)


def _pick(dim, target):
    best = None
    for t in range(LANES, min(dim, target) + 1, LANES):
        if dim % t == 0:
            best = t
    return best if best is not None else dim


def _dot_nt(a, b):
    return lax.dot_general(a, b, (((1,), (1,)), ((), ())), preferred_element_type=F32)


def _dot_tn(a, b):
    return lax.dot_general(a, b, (((0,), (0,)), ((), ())), preferred_element_type=F32)


def _sigmoid(z):
    return 1.0 / (1.0 + jnp.exp(-z))


def _rowwise(name, fn, ins, outs, S, tm=256):
    tm = min(tm, S)
    arrs, in_specs = [], []
    for it in ins:
        if isinstance(it, tuple) and it[0] == 'full':
            a = it[1]
            in_specs.append(pl.BlockSpec(a.shape, lambda i, _n=a.ndim: (0,) * _n))
        elif isinstance(it, tuple) and it[0] == 'cols':
            _, a, width, blk = it
            in_specs.append(pl.BlockSpec((tm, width), lambda i, _b=blk: (i, _b)))
        elif isinstance(it, tuple) and it[0] == 'off':
            _, a, off = it
            in_specs.append(pl.BlockSpec((tm, a.shape[1]), lambda i, _o=off: (i + _o, 0)))
        else:
            a = it
            in_specs.append(pl.BlockSpec((tm, a.shape[1]), lambda i: (i, 0)))
        arrs.append(a)
    out_shape, out_specs = [], []
    for kind, shp, dt in outs:
        if kind == 'rows':
            out_shape.append(jax.ShapeDtypeStruct((S, shp), dt))
            out_specs.append(pl.BlockSpec((tm, shp), lambda i: (i, 0)))
        else:
            out_shape.append(jax.ShapeDtypeStruct(shp, dt))
            out_specs.append(pl.BlockSpec(shp, lambda i, _n=len(shp): (0,) * _n))
    n_in = len(arrs)

    def body(*refs):
        vals = fn(*[r[...] for r in refs[:n_in]])
        first = pl.program_id(0) == 0
        for (kind, _, _), r, v in zip(outs, refs[n_in:], vals):
            if kind == 'rows':
                r[...] = v.astype(r.dtype)
            else:
                @pl.when(first)
                def _(r=r, v=v):
                    r[...] = v.astype(r.dtype)

                @pl.when(jnp.logical_not(first))
                def _(r=r, v=v):
                    r[...] += v.astype(r.dtype)

    has_acc = any(k != 'rows' for k, _, _ in outs)
    res = pl.pallas_call(
        body, name=name, grid=(S // tm,), in_specs=in_specs, out_specs=out_specs, out_shape=out_shape,
        compiler_params=_cparams("arbitrary" if has_acc else "parallel"),
    )(*arrs)
    return res


def _mm_col(name, a, wg, row_start, bias=None, out_dtype=BF16):
    S, K = a.shape
    _, _, Ns = wg.shape
    rb = row_start // K
    tm = min(512, S)

    def body(a_ref, w_ref, *rest):
        acc = jnp.dot(a_ref[...].astype(BF16), w_ref[...], preferred_element_type=F32)
        if bias is not None:
            acc = acc + rest[0][...]
        rest[-1][...] = acc.astype(out_dtype)

    in_specs = [pl.BlockSpec((tm, K), lambda j, m: (m, 0)), pl.BlockSpec((None, K, Ns), lambda j, m: (j, rb, 0))]
    args = [a, wg]
    if bias is not None:
        in_specs.append(pl.BlockSpec((1, Ns), lambda j, m: (0, j)))
        args.append(bias)
    return pl.pallas_call(
        body, name=name, grid=(N_CHIPS, S // tm), in_specs=in_specs,
        out_specs=pl.BlockSpec((tm, Ns), lambda j, m: (m, j)),
        out_shape=jax.ShapeDtypeStruct((S, N_CHIPS * Ns), out_dtype),
        compiler_params=_cparams("parallel", "parallel"),
    )(*args)


def _mm_row(name, a, wg, row_start, Ks, bias=None):
    S = a.shape[0]
    N = wg.shape[2]
    rb = row_start // Ks
    tm = min(512, S)

    def body(a_ref, w_ref, *rest):
        o_ref = rest[-1]
        j = pl.program_id(1)
        d = jnp.dot(a_ref[...].astype(BF16), w_ref[...], preferred_element_type=F32)

        @pl.when(j == 0)
        def _():
            o_ref[...] = d + rest[0][...] if bias is not None else d

        @pl.when(j > 0)
        def _():
            o_ref[...] += d

    in_specs = [pl.BlockSpec((tm, Ks), lambda m, j: (m, j)), pl.BlockSpec((None, Ks, N), lambda m, j: (j, rb, 0))]
    args = [a, wg]
    if bias is not None:
        in_specs.append(pl.BlockSpec((1, N), lambda m, j: (0, 0)))
        args.append(bias)
    return pl.pallas_call(
        body, name=name, grid=(S // tm, N_CHIPS), in_specs=in_specs,
        out_specs=pl.BlockSpec((tm, N), lambda m, j: (m, 0)),
        out_shape=jax.ShapeDtypeStruct((S, N), F32),
        compiler_params=_cparams("parallel", "arbitrary"),
    )(*args)


def _mm_col_t(name, pairs, wg, K):
    S = pairs[0][0].shape[0]
    Ns = wg.shape[2]
    tm = min(512, S)
    n = len(pairs)

    def body(*refs):
        o_ref = refs[-1]
        j = pl.program_id(1)
        d = _dot_nt(refs[0][...], refs[n][...])
        for p in range(1, n):
            d = d + _dot_nt(refs[p][...], refs[n + p][...])

        @pl.when(j == 0)
        def _():
            o_ref[...] = d

        @pl.when(j > 0)
        def _():
            o_ref[...] += d

    in_specs = [pl.BlockSpec((tm, Ns), lambda m, j: (m, j)) for _ in pairs]
    in_specs += [pl.BlockSpec((None, K, Ns), lambda m, j, _rb=rs // K: (j, _rb, 0)) for _, rs in pairs]
    return pl.pallas_call(
        body, name=name, grid=(S // tm, N_CHIPS), in_specs=in_specs,
        out_specs=pl.BlockSpec((tm, K), lambda m, j: (m, 0)),
        out_shape=jax.ShapeDtypeStruct((S, K), F32),
        compiler_params=_cparams("parallel", "arbitrary"),
    )(*[dy for dy, _ in pairs], *[wg for _ in pairs])


def _mm_row_t(name, dy, wg, row_start, Ks, out_dtype):
    S, N = dy.shape
    rb = row_start // Ks
    tm = min(512, S)

    def body(dy_ref, w_ref, o_ref):
        o_ref[...] = _dot_nt(dy_ref[...], w_ref[...]).astype(out_dtype)

    return pl.pallas_call(
        body, name=name, grid=(N_CHIPS, S // tm),
        in_specs=[pl.BlockSpec((tm, N), lambda j, m: (m, 0)), pl.BlockSpec((None, Ks, N), lambda j, m: (j, rb, 0))],
        out_specs=pl.BlockSpec((tm, Ks), lambda j, m: (m, j)),
        out_shape=jax.ShapeDtypeStruct((S, N_CHIPS * Ks), out_dtype),
        compiler_params=_cparams("parallel", "parallel"),
    )(dy, wg)


def _mm_dw(name, a, dy, dg, row_start, kind):
    S = a.shape[0]
    _, _, W = dg.shape
    if kind == 'col':
        K = a.shape[1]
        rows = K
        tk, tn = _pick(K, 512), W
        a_map = lambda j, nb, kb: (0, kb)
        dy_map = lambda j, nb, kb: (0, j * (W // tn) + nb)
    else:
        rows = a.shape[1] // N_CHIPS
        tk = rows if rows * S * 2 * 2 <= 12 * 1024 * 1024 else _pick(rows, 512)
        tn = _pick(W, 1024)
        a_map = lambda j, nb, kb: (0, j * (rows // tk) + kb)
        dy_map = lambda j, nb, kb: (0, nb)
    rb = row_start // tk
    assert row_start % tk == 0

    def body(a_ref, dy_ref, dg_in, o_ref):
        del dg_in
        o_ref[...] = _dot_tn(a_ref[...], dy_ref[...]).astype(o_ref.dtype)

    return pl.pallas_call(
        body, name=name, grid=(N_CHIPS, W // tn, rows // tk),
        in_specs=[pl.BlockSpec((S, tk), a_map), pl.BlockSpec((S, tn), dy_map), pl.BlockSpec(memory_space=pl.ANY)],
        out_specs=pl.BlockSpec((None, tk, tn), lambda j, nb, kb: (j, rb + kb, nb)),
        out_shape=jax.ShapeDtypeStruct(dg.shape, dg.dtype),
        input_output_aliases={2: 0},
        compiler_params=_cparams("parallel", "parallel", "parallel"),
    )(a, dy, dg)


def _fox_probs(q, k, c_blk, crow, h, qi, tq):
    n = k.shape[0]
    s = _dot_nt(q, k) * (HEAD_DIM ** -0.5)
    lane = lax.broadcasted_iota(jnp.int32, c_blk.shape, 1)
    ccol = jnp.sum(jnp.where(lane == h, c_blk, 0.0), axis=1, keepdims=True)
    s = s + (ccol - crow)
    t_idx = qi * tq + lax.broadcasted_iota(jnp.int32, (tq, n), 0)
    s_idx = lax.broadcasted_iota(jnp.int32, (tq, n), 1)
    s = jnp.where(s_idx <= t_idx, s, NEG_BIG)
    p = jnp.exp(s - jnp.max(s, axis=1, keepdims=True))
    return p / jnp.sum(p, axis=1, keepdims=True)


def _per_query_block(qi, nq, tq, fn):
    for qv in range(nq):
        @pl.when(qi == qv)
        def _(qv=qv):
            fn(qv, (qv + 1) * tq)


def _fox_fwd(qkv, c, crow, H):
    S = qkv.shape[0]
    tq = min(256, S)

    def body(q_ref, k_ref, v_ref, c_ref, crow_ref, o_ref):
        def block(qv, n):
            p = _fox_probs(q_ref[...], k_ref[0:n, :], c_ref[...], crow_ref[:, 0:n], pl.program_id(0), qv, tq)
            o_ref[...] = jnp.dot(p.astype(BF16), v_ref[0:n, :], preferred_element_type=F32).astype(o_ref.dtype)

        _per_query_block(pl.program_id(1), S // tq, tq, block)

    return pl.pallas_call(
        body, name="fox_attn_fwd", grid=(H, S // tq),
        in_specs=[pl.BlockSpec((tq, HEAD_DIM), lambda h, i: (i, h)),
                  pl.BlockSpec((S, HEAD_DIM), lambda h, i: (0, H + h)),
                  pl.BlockSpec((S, HEAD_DIM), lambda h, i: (0, 2 * H + h)),
                  pl.BlockSpec((tq, H), lambda h, i: (i, 0)),
                  pl.BlockSpec((None, 1, S), lambda h, i: (h, 0, 0))],
        out_specs=pl.BlockSpec((tq, HEAD_DIM), lambda h, i: (i, h)),
        out_shape=jax.ShapeDtypeStruct((S, H * HEAD_DIM), BF16),
        compiler_params=_cparams("parallel", "parallel"),
    )(qkv, qkv, qkv, c, crow)


def _fox_bwd(qkv, c, crow, do, H):
    S = qkv.shape[0]
    tq = min(256, S)
    nq = S // tq

    def body(q_ref, k_ref, v_ref, c_ref, crow_ref, do_ref, dq_ref, dk_ref, dv_ref, dc_ref, dk_acc, dv_acc):
        qi = pl.program_id(1)

        @pl.when(qi == 0)
        def _():
            dk_acc[...] = jnp.zeros_like(dk_acc)
            dv_acc[...] = jnp.zeros_like(dv_acc)
            dc_ref[...] = jnp.zeros_like(dc_ref)

        def block(qv, n):
            q, k, v, do_ = q_ref[...], k_ref[0:n, :], v_ref[0:n, :], do_ref[...]
            p = _fox_probs(q, k, c_ref[...], crow_ref[:, 0:n], pl.program_id(0), qv, tq)
            dv_acc[0:n, :] += _dot_tn(p.astype(BF16), do_)
            dp = _dot_nt(do_, v)
            ds = p * (dp - jnp.sum(p * dp, axis=1, keepdims=True))
            dsb = (ds * (HEAD_DIM ** -0.5)).astype(BF16)
            dq_ref[...] = jnp.dot(dsb, k, preferred_element_type=F32).astype(dq_ref.dtype)
            dk_acc[0:n, :] += _dot_tn(dsb, q)
            dc_ref[:, 0:n] += -jnp.sum(ds, axis=0, keepdims=True)

        _per_query_block(qi, nq, tq, block)

        @pl.when(qi == nq - 1)
        def _():
            dk_ref[...] = dk_acc[...].astype(dk_ref.dtype)
            dv_ref[...] = dv_acc[...].astype(dv_ref.dtype)

    D = H * HEAD_DIM
    return pl.pallas_call(
        body, name="fox_attn_bwd", grid=(H, nq),
        in_specs=[pl.BlockSpec((tq, HEAD_DIM), lambda h, i: (i, h)),
                  pl.BlockSpec((S, HEAD_DIM), lambda h, i: (0, H + h)),
                  pl.BlockSpec((S, HEAD_DIM), lambda h, i: (0, 2 * H + h)),
                  pl.BlockSpec((tq, H), lambda h, i: (i, 0)),
                  pl.BlockSpec((None, 1, S), lambda h, i: (h, 0, 0)),
                  pl.BlockSpec((tq, HEAD_DIM), lambda h, i: (i, h))],
        out_specs=[pl.BlockSpec((tq, HEAD_DIM), lambda h, i: (i, h)),
                   pl.BlockSpec((S, HEAD_DIM), lambda h, i: (0, h)),
                   pl.BlockSpec((S, HEAD_DIM), lambda h, i: (0, h)),
                   pl.BlockSpec((None, 1, S), lambda h, i: (h, 0, 0))],
        out_shape=[jax.ShapeDtypeStruct((S, D), BF16), jax.ShapeDtypeStruct((S, D), BF16),
                   jax.ShapeDtypeStruct((S, D), BF16), jax.ShapeDtypeStruct((H, 1, S), F32)],
        scratch_shapes=[pltpu.VMEM((S, HEAD_DIM), F32), pltpu.VMEM((S, HEAD_DIM), F32)],
        compiler_params=_cparams("parallel", "arbitrary"),
    )(qkv, qkv, qkv, c, crow, do)


def _cumsum_rows(name, xin, reverse):
    S, H = xin.shape
    tb = min(256, S)
    nb = S // tb

    def body(x_ref, o_ref):
        r = lax.broadcasted_iota(jnp.int32, (tb, tb), 0)
        cidx = lax.broadcasted_iota(jnp.int32, (tb, tb), 1)
        tri = (r <= cidx if reverse else r >= cidx).astype(F32)

        def step(b, carry):
            bb = nb - 1 - b if reverse else b
            rows = pl.ds(pl.multiple_of(bb * tb, tb), tb)
            blk = x_ref[rows, :]
            o_ref[rows, :] = jnp.dot(tri, blk, precision=HIGHEST, preferred_element_type=F32) + carry
            return carry + jnp.sum(blk, axis=0, keepdims=True)

        lax.fori_loop(0, nb, step, jnp.zeros((1, H), F32))

    return pl.pallas_call(
        body, name=name, out_shape=jax.ShapeDtypeStruct((S, H), F32),
        in_specs=[pl.BlockSpec(memory_space=pltpu.VMEM)], out_specs=pl.BlockSpec(memory_space=pltpu.VMEM),
        compiler_params=_cparams(),
    )(xin)


def _rel_onehot(i, transposed):
    shp = (REL_TABLE_PAD, BAND_KEYS) if transposed else (BAND_KEYS, REL_TABLE_PAD)
    j = lax.broadcasted_iota(jnp.int32, shp, 1 if transposed else 0)
    r = lax.broadcasted_iota(jnp.int32, shp, 0 if transposed else 1)
    return (jnp.clip(PAD_KEYS + i - j, -REL_CLIP, REL_CLIP) + REL_CLIP == r).astype(F32)


def _rel_expand(rb_pad):
    H = rb_pad.shape[0]

    def body(rb_ref, o_ref):
        def step(i, _):
            o_ref[i] = jnp.dot(rb_ref[...], _rel_onehot(i, True), precision=HIGHEST, preferred_element_type=F32)
            return 0
        lax.fori_loop(0, CHUNK, step, 0)

    return pl.pallas_call(
        body, name="rel_bias_expand", out_shape=jax.ShapeDtypeStruct((CHUNK, H, BAND_KEYS), F32),
        in_specs=[pl.BlockSpec(memory_space=pltpu.VMEM)], out_specs=pl.BlockSpec(memory_space=pltpu.VMEM),
        compiler_params=_cparams(),
    )(rb_pad)


def _rel_reduce(dbt):
    H = dbt.shape[1]

    def body(d_ref, o_ref):
        def step(i, acc):
            return acc + jnp.dot(d_ref[i], _rel_onehot(i, False), precision=HIGHEST, preferred_element_type=F32)
        o_ref[...] = lax.fori_loop(0, CHUNK, step, jnp.zeros((H, REL_TABLE_PAD), F32))

    return pl.pallas_call(
        body, name="rel_bias_reduce", out_shape=jax.ShapeDtypeStruct((H, REL_TABLE_PAD), F32),
        in_specs=[pl.BlockSpec(memory_space=pltpu.VMEM)], out_specs=pl.BlockSpec(memory_space=pltpu.VMEM),
        compiler_params=_cparams(),
    )(dbt)


def _rel_probs(q, kb, bias, n):
    s = _dot_nt(q, kb) * (HEAD_DIM ** -0.5) + bias
    j = lax.broadcasted_iota(jnp.int32, (CHUNK, BAND_KEYS), 1)
    s = jnp.where(j >= PAD_KEYS - n * CHUNK, s, NEG_BIG)
    p = jnp.exp(s - jnp.max(s, axis=1, keepdims=True))
    return p / jnp.sum(p, axis=1, keepdims=True)


def _rel_fwd(qkv, bias, H):
    S = qkv.shape[0]

    def body(q_ref, k_ref, v_ref, b_ref, o_ref, kpad, vpad):
        kpad[0:PAD_KEYS, :] = jnp.zeros((PAD_KEYS, HEAD_DIM), BF16)
        vpad[0:PAD_KEYS, :] = jnp.zeros((PAD_KEYS, HEAD_DIM), BF16)
        kpad[PAD_KEYS:PAD_KEYS + S, :] = k_ref[...]
        vpad[PAD_KEYS:PAD_KEYS + S, :] = v_ref[...]
        bias_t = b_ref[...]

        def chunk(n, _):
            rows = pl.ds(pl.multiple_of(n * CHUNK, CHUNK), CHUNK)
            band = pl.ds(pl.multiple_of(n * CHUNK, CHUNK), BAND_KEYS)
            p = _rel_probs(q_ref[rows, :], kpad[band, :], bias_t, n)
            o_ref[rows, :] = jnp.dot(p.astype(BF16), vpad[band, :], preferred_element_type=F32).astype(o_ref.dtype)
            return 0

        lax.fori_loop(0, S // CHUNK, chunk, 0)

    return pl.pallas_call(
        body, name="rel_attn_fwd", grid=(H,),
        in_specs=[pl.BlockSpec((S, HEAD_DIM), lambda h: (0, h)),
                  pl.BlockSpec((S, HEAD_DIM), lambda h: (0, H + h)),
                  pl.BlockSpec((S, HEAD_DIM), lambda h: (0, 2 * H + h)),
                  pl.BlockSpec((None, CHUNK, BAND_KEYS), lambda h: (h, 0, 0))],
        out_specs=pl.BlockSpec((S, HEAD_DIM), lambda h: (0, h)),
        out_shape=jax.ShapeDtypeStruct((S, H * HEAD_DIM), BF16),
        scratch_shapes=[pltpu.VMEM((S + PAD_KEYS, HEAD_DIM), BF16), pltpu.VMEM((S + PAD_KEYS, HEAD_DIM), BF16)],
        compiler_params=_cparams("parallel"),
    )(qkv, qkv, qkv, bias)


def _rel_bwd(qkv, bias, do, H):
    S = qkv.shape[0]
    D = H * HEAD_DIM

    def body(q_ref, k_ref, v_ref, b_ref, do_ref, dq_ref, dk_ref, dv_ref, db_ref, kpad, vpad, dkpad, dvpad):
        kpad[0:PAD_KEYS, :] = jnp.zeros((PAD_KEYS, HEAD_DIM), BF16)
        vpad[0:PAD_KEYS, :] = jnp.zeros((PAD_KEYS, HEAD_DIM), BF16)
        kpad[PAD_KEYS:PAD_KEYS + S, :] = k_ref[...]
        vpad[PAD_KEYS:PAD_KEYS + S, :] = v_ref[...]
        dkpad[...] = jnp.zeros_like(dkpad)
        dvpad[...] = jnp.zeros_like(dvpad)
        db_ref[...] = jnp.zeros_like(db_ref)
        bias_t = b_ref[...]

        def chunk(n, _):
            rows = pl.ds(pl.multiple_of(n * CHUNK, CHUNK), CHUNK)
            band = pl.ds(pl.multiple_of(n * CHUNK, CHUNK), BAND_KEYS)
            q, kb, vb, do_ = q_ref[rows, :], kpad[band, :], vpad[band, :], do_ref[rows, :]
            p = _rel_probs(q, kb, bias_t, n)
            dvpad[band, :] += _dot_tn(p.astype(BF16), do_)
            dp = _dot_nt(do_, vb)
            ds = p * (dp - jnp.sum(p * dp, axis=1, keepdims=True))
            db_ref[...] += ds
            dsb = (ds * (HEAD_DIM ** -0.5)).astype(BF16)
            dq_ref[rows, :] = jnp.dot(dsb, kb, preferred_element_type=F32).astype(dq_ref.dtype)
            dkpad[band, :] += _dot_tn(dsb, q)
            return 0

        lax.fori_loop(0, S // CHUNK, chunk, 0)
        dk_ref[...] = dkpad[PAD_KEYS:PAD_KEYS + S, :].astype(dk_ref.dtype)
        dv_ref[...] = dvpad[PAD_KEYS:PAD_KEYS + S, :].astype(dv_ref.dtype)

    head = lambda h: (0, h)
    return pl.pallas_call(
        body, name="rel_attn_bwd", grid=(H,),
        in_specs=[pl.BlockSpec((S, HEAD_DIM), head),
                  pl.BlockSpec((S, HEAD_DIM), lambda h: (0, H + h)),
                  pl.BlockSpec((S, HEAD_DIM), lambda h: (0, 2 * H + h)),
                  pl.BlockSpec((None, CHUNK, BAND_KEYS), lambda h: (h, 0, 0)),
                  pl.BlockSpec((S, HEAD_DIM), head)],
        out_specs=[pl.BlockSpec((S, HEAD_DIM), head), pl.BlockSpec((S, HEAD_DIM), head), pl.BlockSpec((S, HEAD_DIM), head),
                   pl.BlockSpec((None, CHUNK, BAND_KEYS), lambda h: (h, 0, 0))],
        out_shape=[jax.ShapeDtypeStruct((S, D), BF16), jax.ShapeDtypeStruct((S, D), BF16), jax.ShapeDtypeStruct((S, D), BF16),
                   jax.ShapeDtypeStruct((H, CHUNK, BAND_KEYS), F32)],
        scratch_shapes=[pltpu.VMEM((S + PAD_KEYS, HEAD_DIM), BF16), pltpu.VMEM((S + PAD_KEYS, HEAD_DIM), BF16),
                        pltpu.VMEM((S + PAD_KEYS, HEAD_DIM), F32), pltpu.VMEM((S + PAD_KEYS, HEAD_DIM), F32)],
        compiler_params=_cparams("parallel"),
    )(qkv, qkv, qkv, bias, do)


def _conv_taps(win, tt, reverse):
    n = tt + 2 * CONV_HALO
    for k in range(CONV_K):
        off = (CONV_K - 1 - k) if reverse else (k - (CONV_K - 1))
        sh = (-off) % n
        rolled = pltpu.roll(win, sh, 0) if sh else win
        yield k, rolled[CONV_HALO:CONV_HALO + tt, :]


def _fill_padded(pad_ref, x_ref, S):
    tc = pad_ref.shape[1]
    pad_ref[0:CONV_HALO, :] = jnp.zeros((CONV_HALO, tc), F32)
    pad_ref[CONV_HALO + S:CONV_HALO + S + CONV_HALO, :] = jnp.zeros((CONV_HALO, tc), F32)
    pad_ref[CONV_HALO:CONV_HALO + S, :] = x_ref[...]


def _dwconv(name, xin, w32, bias, reverse):
    S, D = xin.shape
    tc = min(256, D)
    tt = min(256, S)

    def body(x_ref, w_ref, b_ref, y_ref, pad_ref):
        _fill_padded(pad_ref, x_ref, S)
        def tile(ti, _):
            t0 = pl.multiple_of(ti * tt, tt)
            win = pad_ref[pl.ds(t0, tt + 2 * CONV_HALO), :]
            acc = jnp.zeros((tt, tc), F32) + b_ref[...]
            for k, shifted in _conv_taps(win, tt, reverse):
                acc = acc + w_ref[pl.ds(k, 1), :] * shifted
            y_ref[pl.ds(t0, tt), :] = acc
            return 0

        lax.fori_loop(0, S // tt, tile, 0)

    return pl.pallas_call(
        body, name=name, grid=(D // tc,),
        in_specs=[pl.BlockSpec((S, tc), lambda i: (0, i)), pl.BlockSpec((CONV_HALO, tc), lambda i: (0, i)),
                  pl.BlockSpec((1, tc), lambda i: (0, i))],
        out_specs=pl.BlockSpec((S, tc), lambda i: (0, i)),
        out_shape=jax.ShapeDtypeStruct((S, D), F32),
        scratch_shapes=[pltpu.VMEM((S + 2 * CONV_HALO, tc), F32)],
        compiler_params=_cparams("parallel"),
    )(xin, w32, bias)


def _dwconv_dw(xin, dy):
    S, D = xin.shape
    tc = min(256, D)
    tt = min(256, S)

    def body(x_ref, dy_ref, o_ref, pad_ref):
        _fill_padded(pad_ref, x_ref, S)

        def tile(ti, acc):
            t0 = pl.multiple_of(ti * tt, tt)
            win = pad_ref[pl.ds(t0, tt + 2 * CONV_HALO), :]
            dyt = dy_ref[pl.ds(t0, tt), :]
            ridx = lax.broadcasted_iota(jnp.int32, (CONV_HALO, tc), 0)
            upd = jnp.zeros((CONV_HALO, tc), F32)
            for k, shifted in _conv_taps(win, tt, False):
                upd = jnp.where(ridx == k, jnp.sum(dyt * shifted, axis=0, keepdims=True), upd)
            return acc + upd

        o_ref[...] = lax.fori_loop(0, S // tt, tile, jnp.zeros((CONV_HALO, tc), F32))

    return pl.pallas_call(
        body, name="dwconv_dw", grid=(D // tc,),
        in_specs=[pl.BlockSpec((S, tc), lambda i: (0, i)), pl.BlockSpec((S, tc), lambda i: (0, i))],
        out_specs=pl.BlockSpec((CONV_HALO, tc), lambda i: (0, i)),
        out_shape=jax.ShapeDtypeStruct((CONV_HALO, D), F32),
        scratch_shapes=[pltpu.VMEM((S + 2 * CONV_HALO, tc), F32)],
        compiler_params=_cparams("parallel"),
    )(xin, dy)


def _place():
    x, y, c = lax.axis_index("x"), lax.axis_index("y"), lax.axis_index("c")
    chips = [(1 - x, y), (x, 1 - y), (1 - x, 1 - y)]
    return x, y, c, chips


def _remote(src, dst, ssem, rsem, dev):
    return pltpu.make_async_remote_copy(src_ref=src, dst_ref=dst, send_sem=ssem, recv_sem=rsem,
                                        device_id=dev, device_id_type=MESH_IDS)


_ANY = pl.BlockSpec(memory_space=pl.ANY)


def _place_own(name, own, place):
    R, W = own.shape
    tr = _pick_rows(R)

    def body(p_ref, a_ref, o_ref):
        del p_ref
        o_ref[...] = a_ref[...]

    return pl.pallas_call(
        body, name=name,
        grid_spec=pltpu.PrefetchScalarGridSpec(
            num_scalar_prefetch=1, grid=(R // tr,),
            in_specs=[pl.BlockSpec((tr, W), lambda i, p: (i, 0))],
            out_specs=pl.BlockSpec((None, tr, W), lambda i, p: (p[1], i, 0))),
        out_shape=jax.ShapeDtypeStruct((N_CHIPS, R, W), own.dtype),
        compiler_params=_cparams("parallel"),
    )(place, own)


def _gather_group(name, wg):
    _, R, W = wg.shape
    Rh = R // 2

    def body(in_ref, out_ref, ssems, rsems):
        del in_ref
        x, y, c, chips = _place()
        me = 2 * x + y
        sib = (x, y, 1 - c)
        mine = pl.ds(pl.multiple_of(c * Rh, SUBLANES), Rh)
        theirs = pl.ds(pl.multiple_of((1 - c) * Rh, SUBLANES), Rh)
        sends = []
        for j, (px, py) in enumerate(chips):
            cp = _remote(out_ref.at[me, mine], out_ref.at[me, mine], ssems.at[j], rsems.at[j], (px, py, c))
            cp.start()
            sends.append(cp)
        for j, (px, py) in enumerate(chips):
            blk = out_ref.at[2 * px + py, mine]
            _remote(blk, blk, ssems.at[j], rsems.at[j], (px, py, c)).wait_recv()
            fw = _remote(blk, blk, ssems.at[3 + j], rsems.at[3 + j], sib)
            fw.start()
            sends.append(fw)
        for j, (px, py) in enumerate(chips):
            blk = out_ref.at[2 * px + py, theirs]
            _remote(blk, blk, ssems.at[3 + j], rsems.at[3 + j], sib).wait_recv()
        for cp in sends:
            cp.wait_send()

    return pl.pallas_call(
        body, name=name, in_specs=[_ANY], out_specs=_ANY,
        out_shape=jax.ShapeDtypeStruct(wg.shape, wg.dtype), input_output_aliases={0: 0},
        scratch_shapes=[pltpu.SemaphoreType.DMA((6,)), pltpu.SemaphoreType.DMA((6,))],
        compiler_params=pltpu.CompilerParams(has_side_effects=True),
    )(wg)


def _swap_halves(name, dg):
    _, R, W = dg.shape
    Rh = R // 2

    def body(src, land, ssems, rsems):
        x, y, c, _ = _place()
        theirs = pl.ds(pl.multiple_of((1 - c) * Rh, SUBLANES), Rh)
        cps = [_remote(src.at[j, theirs], land.at[j], ssems.at[j], rsems.at[j], (x, y, 1 - c)) for j in range(N_CHIPS)]
        for cp in cps:
            cp.start()
        for cp in cps:
            cp.wait()

    return pl.pallas_call(
        body, name=name, in_specs=[_ANY], out_specs=_ANY,
        out_shape=jax.ShapeDtypeStruct((N_CHIPS, Rh, W), dg.dtype),
        scratch_shapes=[pltpu.SemaphoreType.DMA((N_CHIPS,)), pltpu.SemaphoreType.DMA((N_CHIPS,))],
        compiler_params=pltpu.CompilerParams(has_side_effects=True),
    )(dg)


def _scatter_chips(name, pb):
    def body(src, land, ssems, rsems):
        x, y, c, chips = _place()
        me = 2 * x + y
        cps = [_remote(src.at[2 * px + py], land.at[me], ssems.at[j], rsems.at[j], (px, py, c))
               for j, (px, py) in enumerate(chips)]
        for cp in cps:
            cp.start()
        for j, (px, py) in enumerate(chips):
            blk = land.at[2 * px + py]
            _remote(blk, blk, ssems.at[j], rsems.at[j], (px, py, c)).wait_recv()
        for cp in cps:
            cp.wait_send()

    return pl.pallas_call(
        body, name=name, in_specs=[_ANY], out_specs=_ANY,
        out_shape=jax.ShapeDtypeStruct(pb.shape, pb.dtype),
        scratch_shapes=[pltpu.SemaphoreType.DMA((3,)), pltpu.SemaphoreType.DMA((3,))],
        compiler_params=pltpu.CompilerParams(has_side_effects=True),
    )(pb)


def _share_halves(name, gf):
    R, W = gf.shape
    Rh = R // 2

    def body(in_ref, out, ssem, rsem):
        del in_ref
        x, y, c, _ = _place()
        mine = out.at[pl.ds(pl.multiple_of(c * Rh, SUBLANES), Rh)]
        theirs = out.at[pl.ds(pl.multiple_of((1 - c) * Rh, SUBLANES), Rh)]
        cp = _remote(mine, mine, ssem, rsem, (x, y, 1 - c))
        cp.start()
        _remote(theirs, theirs, ssem, rsem, (x, y, 1 - c)).wait_recv()
        cp.wait_send()

    return pl.pallas_call(
        body, name=name, in_specs=[_ANY], out_specs=_ANY,
        out_shape=jax.ShapeDtypeStruct(gf.shape, gf.dtype), input_output_aliases={0: 0},
        scratch_shapes=[pltpu.SemaphoreType.DMA, pltpu.SemaphoreType.DMA],
        compiler_params=pltpu.CompilerParams(has_side_effects=True),
    )(gf)


def _broadcast_small(name, buf):
    R = buf.shape[0]

    def body(src, out, ssems, rsems):
        x, y, c, _ = _place()
        me = 4 * x + 2 * y + c
        out[me] = src[...]
        peers = []
        for mask in range(1, 8):
            fx, fy, fc = (mask >> 2) & 1, (mask >> 1) & 1, mask & 1
            peers.append((1 - x if fx else x, 1 - y if fy else y, 1 - c if fc else c))
        cps = [_remote(src, out.at[me], ssems.at[k], rsems.at[k], p) for k, p in enumerate(peers)]
        for cp in cps:
            cp.start()
        for k, (px, py, pc) in enumerate(peers):
            blk = out.at[4 * px + 2 * py + pc]
            _remote(blk, blk, ssems.at[k], rsems.at[k], (px, py, pc)).wait_recv()
        for cp in cps:
            cp.wait_send()

    return pl.pallas_call(
        body, name=name, in_specs=[pl.BlockSpec(memory_space=pltpu.VMEM)], out_specs=pl.BlockSpec(memory_space=pltpu.VMEM),
        out_shape=jax.ShapeDtypeStruct((8, R, LANES), F32),
        scratch_shapes=[pltpu.SemaphoreType.DMA((7,)), pltpu.SemaphoreType.DMA((7,))],
        compiler_params=pltpu.CompilerParams(has_side_effects=True, vmem_limit_bytes=VMEM_LIMIT_V7X),
    )(buf)


def _sum_slabs(name, slabs):
    n, R, _ = slabs.shape

    def body(s_ref, o_ref):
        acc = s_ref[0]
        for k in range(1, n):
            acc = acc + s_ref[k]
        o_ref[...] = acc

    return pl.pallas_call(
        body, name=name, out_shape=jax.ShapeDtypeStruct((R, LANES), F32),
        in_specs=[pl.BlockSpec(memory_space=pltpu.VMEM)], out_specs=pl.BlockSpec(memory_space=pltpu.VMEM),
        compiler_params=_cparams(),
    )(slabs)


def _pair_sum(name, dg, land, c_arr):
    _, R, W = dg.shape
    Rh = R // 2
    tr = _pick_rows(Rh)
    nb = Rh // tr

    def body(c_ref, a_ref, b_ref, o_ref):
        del c_ref
        o_ref[...] = (a_ref[...].astype(F32) + b_ref[...].astype(F32)).astype(o_ref.dtype)

    return pl.pallas_call(
        body, name=name,
        grid_spec=pltpu.PrefetchScalarGridSpec(
            num_scalar_prefetch=1, grid=(N_CHIPS, nb),
            in_specs=[pl.BlockSpec((None, tr, W), lambda j, i, c_ref: (j, c_ref[0] * nb + i, 0)),
                      pl.BlockSpec((None, tr, W), lambda j, i, c_ref: (j, i, 0))],
            out_specs=pl.BlockSpec((None, tr, W), lambda j, i, c_ref: (j, i, 0))),
        out_shape=jax.ShapeDtypeStruct((N_CHIPS, Rh, W), BF16),
        compiler_params=_cparams("parallel", "parallel"),
    )(c_arr, dg, land)


def _pick_rows(rows, target=512):
    best = SUBLANES
    for t in range(SUBLANES, min(rows, target) + 1, SUBLANES):
        if rows % t == 0:
            best = t
    return best


def _chip_sum(name, pb, land, place):
    _, Rh, W = land.shape
    tr = _pick_rows(Rh)
    nb = Rh // tr

    def body(p_ref, own_ref, lx_ref, ly_ref, ld_ref, o_ref):
        del p_ref
        o_ref[...] = ((own_ref[...].astype(F32) + lx_ref[...].astype(F32)) + ly_ref[...].astype(F32)) + ld_ref[...].astype(F32)

    slab = lambda flip: pl.BlockSpec((None, tr, W), lambda i, p, _f=flip: (p[1] ^ _f, i, 0))
    return pl.pallas_call(
        body, name=name,
        grid_spec=pltpu.PrefetchScalarGridSpec(
            num_scalar_prefetch=1, grid=(nb,),
            in_specs=[slab(0), slab(2), slab(1), slab(3)],
            out_specs=pl.BlockSpec((tr, W), lambda i, p: (p[0] * nb + i, 0))),
        out_shape=jax.ShapeDtypeStruct((2 * Rh, W), F32),
        compiler_params=_cparams("parallel"),
    )(place, pb, land, land, land)


def _ln_stats(z):
    mu = jnp.mean(z, axis=1, keepdims=True)
    zc = z - mu
    rstd = lax.rsqrt(jnp.mean(zc * zc, axis=1, keepdims=True) + LN_EPS)
    return zc * rstd, rstd


def _ln_fwd(name, xin, m, g, b):
    S, D = xin.shape

    def fn(x_, m_, g_, b_):
        xhat, rstd = _ln_stats(ALPHA * x_ + m_)
        y = xhat * g_ + b_
        return y, y, xhat, rstd

    return _rowwise(name, fn, [xin, m, ('full', g), ('full', b)],
                    [('rows', D, F32), ('rows', D, BF16), ('rows', D, F32), ('rows', 1, F32)], S)


def _ln_bwd_core(dy, xhat, rstd, g):
    dxh = dy * g
    return rstd * (dxh - jnp.mean(dxh, axis=1, keepdims=True) - xhat * jnp.mean(dxh * xhat, axis=1, keepdims=True))


def _ln_bwd(name, terms, xhat, rstd, g):
    S, D = xhat.shape
    scales = [s for _, s in terms]
    n = len(terms)

    def fn(*v):
        dy = v[0] * scales[0] if scales[0] != 1.0 else v[0]
        for t in range(1, n):
            dy = dy + (v[t] * scales[t] if scales[t] != 1.0 else v[t])
        xh, rs, g_ = v[n], v[n + 1], v[n + 2]
        dz = _ln_bwd_core(dy, xh, rs, g_)
        return dz, dz, jnp.sum(dy * xh, axis=0, keepdims=True), jnp.sum(dy, axis=0, keepdims=True)

    return _rowwise(name, fn, [a for a, _ in terms] + [xhat, rstd, ('full', g)],
                    [('rows', D, F32), ('rows', D, BF16), ('acc', (1, D), F32), ('acc', (1, D), F32)], S)


def _adamw_math(w, g, m, v):
    m2 = ADAM_B1 * m + (1.0 - ADAM_B1) * g
    v2 = ADAM_B2 * v + (1.0 - ADAM_B2) * (g * g)
    m_hat = m2 / (1.0 - ADAM_B1 ** ADAM_STEP)
    v_hat = v2 / (1.0 - ADAM_B2 ** ADAM_STEP)
    delta = -ADAM_LR * (m_hat / (jnp.sqrt(v_hat) + ADAM_EPS) + ADAM_WD * w)
    return delta, m2, v2


def _adamw(name, w, gfull, row_start, m, v):
    rows, W = w.shape
    tr = math.gcd(math.gcd(rows, row_start), 256) if row_start else math.gcd(rows, 256)

    def fn(w_, g_, m_, v_):
        d, m2, v2 = _adamw_math(w_, g_, m_, v_)
        return g_, d, m2, v2

    return _rowwise(name, fn, [w, ('off', gfull, row_start // tr), m, v], [('rows', W, F32)] * 4, rows, tm=tr)


def _pack(arrs):
    flat = jnp.concatenate([a.reshape(-1).astype(F32) for a in arrs])
    tile = SUBLANES * LANES
    n = -(-flat.shape[0] // tile) * tile
    return jnp.pad(flat, (0, n - flat.shape[0])).reshape(-1, LANES)


def _unpack(buf, shapes):
    flat = buf.reshape(-1)
    out, pos = [], 0
    for shp in shapes:
        n = math.prod(shp)
        out.append(flat[pos:pos + n].reshape(shp))
        pos += n
    return out


BIG = ['fox_w_qkv', 'fox_w_o', 'rel_w_qkv', 'rel_w_o', 'conv_w_pw1', 'conv_w_pw2', 'ffn_w_gate', 'ffn_w_up', 'ffn_w_down']
SMALL_SHARDED = ['fox_w_f', 'conv_b_pw1', 'conv_w_dw', 'conv_b_dw', 'conv_ln_g', 'conv_ln_b', 'conv_b_pw2']
SMALL_SHARD_AXIS = {'fox_w_f': 1, 'conv_b_pw1': 1, 'conv_w_dw': 2, 'conv_b_dw': 1, 'conv_ln_g': 1, 'conv_ln_b': 1, 'conv_b_pw2': 1}
SMALL_REPL = ['fox_b_f', 'rel_bias', 'ln_mix_g', 'ln_mix_b', 'ln_ffn_g', 'ln_ffn_b']
SMALL = SMALL_SHARDED + SMALL_REPL
WEIGHTS = ['fox_w_qkv', 'fox_w_f', 'fox_b_f', 'fox_w_o', 'rel_w_qkv', 'rel_bias', 'rel_w_o', 'conv_w_pw1', 'conv_b_pw1',
           'conv_w_dw', 'conv_b_dw', 'conv_ln_g', 'conv_ln_b', 'conv_w_pw2', 'conv_b_pw2', 'ffn_w_gate', 'ffn_w_up',
           'ffn_w_down', 'ln_mix_g', 'ln_mix_b', 'ln_ffn_g', 'ln_ffn_b']


def kernel(x, fox_w_qkv, fox_w_f, fox_b_f, fox_w_o, rel_w_qkv, rel_bias, rel_w_o, conv_w_pw1, conv_b_pw1, conv_w_dw, conv_b_dw, conv_ln_g, conv_ln_b, conv_w_pw2, conv_b_pw2, ffn_w_gate, ffn_w_up, ffn_w_down, ln_mix_g, ln_mix_b, ln_ffn_g, ln_ffn_b, loss_target, m_fox_w_qkv, m_fox_w_f, m_fox_b_f, m_fox_w_o, m_rel_w_qkv, m_rel_bias, m_rel_w_o, m_conv_w_pw1, m_conv_b_pw1, m_conv_w_dw, m_conv_b_dw, m_conv_ln_g, m_conv_ln_b, m_conv_w_pw2, m_conv_b_pw2, m_ffn_w_gate, m_ffn_w_up, m_ffn_w_down, m_ln_mix_g, m_ln_mix_b, m_ln_ffn_g, m_ln_ffn_b, v_fox_w_qkv, v_fox_w_f, v_fox_b_f, v_fox_w_o, v_rel_w_qkv, v_rel_bias, v_rel_w_o, v_conv_w_pw1, v_conv_b_pw1, v_conv_w_dw, v_conv_b_dw, v_conv_ln_g, v_conv_ln_b, v_conv_w_pw2, v_conv_b_pw2, v_ffn_w_gate, v_ffn_w_up, v_ffn_w_down, v_ln_mix_g, v_ln_mix_b, v_ln_ffn_g, v_ln_ffn_b):
    A = dict(locals())
    Wt = {n: A[n] for n in WEIGHTS}
    Mo = {n: A['m_' + n] for n in WEIGHTS}
    Vo = {n: A['v_' + n] for n in WEIGHTS}

    _, S, D = x.shape
    H = D // HEAD_DIM
    Ds = D // N_CHIPS
    Nq = fox_w_qkv.shape[2]
    Np = conv_w_pw1.shape[2]
    Fs = ffn_w_gate.shape[2]
    my_x, my_y, my_c = lax.axis_index("x"), lax.axis_index("y"), lax.axis_index("c")
    my_chip = 2 * my_x + my_y
    place = jnp.stack([my_c, my_chip]).astype(jnp.int32)

    wo_base = DEPTH * Fs
    where = {
        'fox_w_qkv': ('qkv', 0), 'rel_w_qkv': ('qkv', N_FOX * D),
        'ffn_w_gate': ('ffn', 0), 'ffn_w_up': ('ffn', DEPTH * D),
        'conv_w_pw1': ('pw1', 0),
        'ffn_w_down': ('dm', 0), 'fox_w_o': ('dm', wo_base), 'rel_w_o': ('dm', wo_base + N_FOX * Ds),
        'conv_w_pw2': ('dm', wo_base + (N_FOX + 1) * Ds),
    }
    members = {'qkv': ['fox_w_qkv', 'rel_w_qkv'], 'ffn': ['ffn_w_gate', 'ffn_w_up'], 'pw1': ['conv_w_pw1'],
               'dm': ['ffn_w_down', 'fox_w_o', 'rel_w_o', 'conv_w_pw2']}
    flat2 = lambda a: a.reshape(-1, a.shape[-1])
    own = {g: jnp.concatenate([flat2(Wt[n]).astype(BF16) for n in ms], axis=0) for g, ms in members.items()}
    WG = {g: _gather_group("gather_" + g, _place_own("place_" + g, own[g], place)) for g in own}
    DG = {g: lax.empty(WG[g].shape, BF16) for g in own}

    small_shapes = [Wt[n].shape for n in SMALL_SHARDED]
    slabs = _broadcast_small("gather_small", _pack([Wt[n] for n in SMALL_SHARDED]))
    per_chip = [_unpack(slabs[2 * j], small_shapes) for j in range(N_CHIPS)]
    full = {n: jnp.concatenate([per_chip[j][i] for j in range(N_CHIPS)], axis=SMALL_SHARD_AXIS[n])
            for i, n in enumerate(SMALL_SHARDED)}
    row = lambda v: v.reshape(1, -1)

    SG = {}

    def ffn_fwd(i, xb):
        hg = _mm_col(f"ffn{i}_gate", xb, WG['ffn'], i * D)
        hu = _mm_col(f"ffn{i}_up", xb, WG['ffn'], (DEPTH + i) * D)
        act, = _rowwise(f"ffn{i}_act", lambda g_, u_: [g_.astype(F32) * _sigmoid(g_.astype(F32)) * u_.astype(F32)],
                        [hg, hu], [('rows', N_CHIPS * Fs, BF16)], S)
        f = _mm_row(f"ffn{i}_down", act, WG['dm'], i * Fs, Fs)
        return f, (hg, hu, act)

    def ffn_bwd(i, xb, saved, dzb):
        hg, hu, act = saved
        DG['dm'] = _mm_dw(f"ffn{i}_dw_down", act, dzb, DG['dm'], i * Fs, 'row')
        dact = _mm_row_t(f"ffn{i}_dact", dzb, WG['dm'], i * Fs, Fs, F32)

        def fn(da_, g_, u_):
            g32, u32 = g_.astype(F32), u_.astype(F32)
            sg = _sigmoid(g32)
            silu = g32 * sg
            return da_ * u32 * (sg * (1.0 + g32 * (1.0 - sg))), da_ * silu

        dhg, dhu = _rowwise(f"ffn{i}_dact_split", fn, [dact, hg, hu],
                            [('rows', N_CHIPS * Fs, BF16), ('rows', N_CHIPS * Fs, BF16)], S)
        DG['ffn'] = _mm_dw(f"ffn{i}_dw_gate", xb, dhg, DG['ffn'], i * D, 'col')
        DG['ffn'] = _mm_dw(f"ffn{i}_dw_up", xb, dhu, DG['ffn'], (DEPTH + i) * D, 'col')
        return _mm_col_t(f"ffn{i}_dx", [(dhg, i * D), (dhu, (DEPTH + i) * D)], WG['ffn'], D)

    def fox_fwd(j, xb):
        qkv = _mm_col(f"fox{j}_qkv", xb, WG['qkv'], j * D)
        wf = full['fox_w_f'][j].astype(BF16)
        def gate_fn(x_, w_, b_):
            z_ = jnp.dot(x_, w_, preferred_element_type=F32) + b_
            return z_, jnp.minimum(z_, 0.0) - jnp.log(1.0 + jnp.exp(-jnp.abs(z_)))

        z, logf = _rowwise(f"fox{j}_gate", gate_fn, [xb, ('full', wf), ('full', row(fox_b_f[j]))],
                           [('rows', H, F32), ('rows', H, F32)], S)
        c = _cumsum_rows(f"fox{j}_cumsum", logf, False)
        crow = c.T.reshape(H, 1, S)
        o = _fox_fwd(qkv, c, crow, H)
        m = _mm_row(f"fox{j}_wo", o, WG['dm'], wo_base + j * Ds, Ds)
        return m, (qkv, z, c, crow, o, wf)

    def fox_bwd(j, xb, saved, dzb):
        qkv, z, c, crow, o, wf = saved
        DG['dm'] = _mm_dw(f"fox{j}_dw_o", o, dzb, DG['dm'], wo_base + j * Ds, 'row')
        do = _mm_row_t(f"fox{j}_do", dzb, WG['dm'], wo_base + j * Ds, Ds, BF16)
        dq, dk, dv, dcrow = _fox_bwd(qkv, c, crow, do, H)
        dqkv = jnp.concatenate([dq, dk, dv], axis=1)
        dlogf = _cumsum_rows(f"fox{j}_rcumsum", dcrow.reshape(H, S).T, True)

        def fn(x_, dl_, z_, w_):
            dz_ = dl_ * _sigmoid(-z_)
            dzb_ = dz_.astype(BF16)
            return _dot_nt(dzb_, w_), _dot_tn(x_, dzb_), jnp.sum(dz_, axis=0, keepdims=True)

        dh_f, dwf, dbf = _rowwise(f"fox{j}_gate_bwd", fn, [xb, dlogf, z, ('full', wf)],
                                  [('rows', D, F32), ('acc', (D, H), F32), ('acc', (1, H), F32)], S)
        SG.setdefault('fox_w_f', [None] * N_FOX)[j] = dwf
        SG.setdefault('fox_b_f', [None] * N_FOX)[j] = dbf.reshape(H)
        DG['qkv'] = _mm_dw(f"fox{j}_dw_qkv", xb, dqkv, DG['qkv'], j * D, 'col')
        dh = _mm_col_t(f"fox{j}_dx", [(dqkv, j * D)], WG['qkv'], D)
        return [dh, dh_f]

    def rel_fwd(xb):
        qkv = _mm_col("rel_qkv", xb, WG['qkv'], N_FOX * D)
        rb_pad = jnp.pad(rel_bias[0], ((0, 0), (0, REL_TABLE_PAD - REL_TABLE)))
        bias = jnp.transpose(_rel_expand(rb_pad), (1, 0, 2))
        o = _rel_fwd(qkv, bias, H)
        m = _mm_row("rel_wo", o, WG['dm'], wo_base + N_FOX * Ds, Ds)
        return m, (qkv, bias, o)

    def rel_bwd(xb, saved, dzb):
        qkv, bias, o = saved
        DG['dm'] = _mm_dw("rel_dw_o", o, dzb, DG['dm'], wo_base + N_FOX * Ds, 'row')
        do = _mm_row_t("rel_do", dzb, WG['dm'], wo_base + N_FOX * Ds, Ds, BF16)
        dq, dk, dv, dbias = _rel_bwd(qkv, bias, do, H)
        SG['rel_bias'] = _rel_reduce(jnp.transpose(dbias, (1, 0, 2)))[:, :REL_TABLE].reshape(1, H, REL_TABLE)
        dqkv = jnp.concatenate([dq, dk, dv], axis=1)
        DG['qkv'] = _mm_dw("rel_dw_qkv", xb, dqkv, DG['qkv'], N_FOX * D, 'col')
        return [_mm_col_t("rel_dx", [(dqkv, N_FOX * D)], WG['qkv'], D)]

    w_dw32 = jnp.pad(full['conv_w_dw'][0], ((0, CONV_HALO - CONV_K), (0, 0)))
    cg, cb = full['conv_ln_g'], full['conv_ln_b']

    def conv_fwd(xb):
        u = _mm_col("conv_pw1", xb, WG['pw1'], 0, bias=full['conv_b_pw1'], out_dtype=F32)
        u2, = _rowwise("conv_glu", lambda a_, g_: [a_ * _sigmoid(g_)],
                       [('cols', u, D, 0), ('cols', u, D, 1)], [('rows', D, F32)], S)
        yc = _dwconv("conv_dw", u2, w_dw32, full['conv_b_dw'], False)

        def fn(y_, g_, b_):
            xhat, rstd = _ln_stats(y_)
            ln = xhat * g_ + b_
            return ln * _sigmoid(ln), xhat, rstd

        zc, xhat, rstd = _rowwise("conv_ln_silu", fn, [yc, ('full', cg), ('full', cb)],
                                  [('rows', D, BF16), ('rows', D, F32), ('rows', 1, F32)], S)
        m = _mm_row("conv_pw2", zc, WG['dm'], wo_base + (N_FOX + 1) * Ds, Ds, bias=full['conv_b_pw2'])
        return m, (u, u2, zc, xhat, rstd)

    def conv_bwd(xb, saved, dz, dzb):
        u, u2, zc, xhat, rstd = saved
        r0 = wo_base + (N_FOX + 1) * Ds
        DG['dm'] = _mm_dw("conv_dw_pw2", zc, dzb, DG['dm'], r0, 'row')
        dzc = _mm_row_t("conv_dzc", dzb, WG['dm'], r0, Ds, F32)

        def fn(dm_, dzc_, xh_, rs_, g_, b_):
            ln = xh_ * g_ + b_
            sg = _sigmoid(ln)
            dln = dzc_ * (sg * (1.0 + ln * (1.0 - sg)))
            dyc = _ln_bwd_core(dln, xh_, rs_, g_)
            col = lambda t: jnp.sum(t, axis=0, keepdims=True)
            return dyc, col(dm_), col(dln * xh_), col(dln), col(dyc)

        dyc, SG['conv_b_pw2'], SG['conv_ln_g'], SG['conv_ln_b'], SG['conv_b_dw'] = _rowwise(
            "conv_ln_silu_bwd", fn, [dz, dzc, xhat, rstd, ('full', cg), ('full', cb)],
            [('rows', D, F32)] + [('acc', (1, D), F32)] * 4, S)
        du2 = _dwconv("conv_dw_bwd_x", dyc, w_dw32, jnp.zeros((1, D), F32), True)
        SG['conv_w_dw'] = _dwconv_dw(u2, dyc)[:CONV_K].reshape(1, CONV_K, D)

        def fn2(du2_, a_, g_):
            sg = _sigmoid(g_)
            da, dgt = du2_ * sg, du2_ * a_ * sg * (1.0 - sg)
            return da, dgt, jnp.sum(da, axis=0, keepdims=True), jnp.sum(dgt, axis=0, keepdims=True)

        da, dgt, dba, dbg = _rowwise("conv_glu_bwd", fn2, [du2, ('cols', u, D, 0), ('cols', u, D, 1)],
                                     [('rows', D, BF16), ('rows', D, BF16), ('acc', (1, D), F32), ('acc', (1, D), F32)], S)
        SG['conv_b_pw1'] = jnp.concatenate([dba, dbg], axis=1)
        du = jnp.concatenate([da, dgt], axis=1)
        DG['pw1'] = _mm_dw("conv_dw_pw1", xb, du, DG['pw1'], 0, 'col')
        return [_mm_col_t("conv_dx", [(du, 0)], WG['pw1'], D)]

    xs = x[0]
    xs_b = xs.astype(BF16)
    tape = []
    for i in range(DEPTH):
        kind, j = i % 3, i // 3
        if kind == 0:
            m, msaved = fox_fwd(j, xs_b)
        elif kind == 1:
            m, msaved = rel_fwd(xs_b)
        else:
            m, msaved = conv_fwd(xs_b)
        xm, xm_b, xhat1, rstd1 = _ln_fwd(f"ln_mix{i}", xs, m, row(ln_mix_g[i]), row(ln_mix_b[i]))
        f, fsaved = ffn_fwd(i, xm_b)
        xo, xo_b, xhat2, rstd2 = _ln_fwd(f"ln_ffn{i}", xm, f, row(ln_ffn_g[i]), row(ln_ffn_b[i]))
        tape.append((xs_b, msaved, xhat1, rstd1, xm_b, fsaved, xhat2, rstd2))
        xs, xs_b = xo, xo_b

    def loss_fn(y_, t_):
        e = y_ - t_
        return e * (1.0 / D), jnp.sum(e * e, axis=0, keepdims=True)

    dy, sq = _rowwise("loss", loss_fn, [xs, loss_target[0]], [('rows', D, F32), ('acc', (1, D), F32)], S)
    loss = lax.psum(jnp.sum(sq) * (0.5 / D), ("x", "y", "c"))

    terms = [(dy, 1.0)]
    g_mix, b_mix, g_ffn, b_ffn = [None] * DEPTH, [None] * DEPTH, [None] * DEPTH, [None] * DEPTH
    for i in reversed(range(DEPTH)):
        kind, j = i % 3, i // 3
        xin_b, msaved, xhat1, rstd1, xm_b, fsaved, xhat2, rstd2 = tape[i]
        dz2, dz2b, g_ffn[i], b_ffn[i] = _ln_bwd(f"ln_ffn{i}_bwd", terms, xhat2, rstd2, row(ln_ffn_g[i]))
        dx_ffn = ffn_bwd(i, xm_b, fsaved, dz2b)
        dz1, dz1b, g_mix[i], b_mix[i] = _ln_bwd(f"ln_mix{i}_bwd", [(dz2, ALPHA), (dx_ffn, 1.0)], xhat1, rstd1, row(ln_mix_g[i]))
        if kind == 0:
            mix_terms = fox_bwd(j, xin_b, msaved, dz1b)
        elif kind == 1:
            mix_terms = rel_bwd(xin_b, msaved, dz1b)
        else:
            mix_terms = conv_bwd(xin_b, msaved, dz1, dz1b)
        terms = [(dz1, ALPHA)] + [(t, 1.0) for t in mix_terms]

    def gx_fn(*v):
        acc = v[0] * ALPHA
        for t in v[1:]:
            acc = acc + t
        return [acc]

    grad_x, = _rowwise("grad_x", gx_fn, [a for a, _ in terms], [('rows', D, F32)], S)
    grad_x = grad_x.reshape(1, S, D)

    SG['fox_w_f'] = jnp.stack(SG['fox_w_f'])
    SG['fox_b_f'] = jnp.stack(SG['fox_b_f'])
    SG['ln_mix_g'] = jnp.concatenate(g_mix, axis=0)
    SG['ln_mix_b'] = jnp.concatenate(b_mix, axis=0)
    SG['ln_ffn_g'] = jnp.concatenate(g_ffn, axis=0)
    SG['ln_ffn_b'] = jnp.concatenate(b_ffn, axis=0)

    grads, deltas, new_m, new_v = {}, {}, {}, {}

    GF = {}
    for g in own:
        land = _swap_halves("pair_swap_" + g, DG[g])
        pb = _pair_sum("pair_sum_" + g, DG[g], land, place)
        land2 = _scatter_chips("chip_scatter_" + g, pb)
        GF[g] = _share_halves("pair_share_" + g, _chip_sum("chip_sum_" + g, pb, land2, place))

    for n in BIG:
        g, r0 = where[n]
        outs = _adamw("adamw_" + n, flat2(Wt[n]), GF[g], r0, flat2(Mo[n]), flat2(Vo[n]))
        grads[n], deltas[n], new_m[n], new_v[n] = [o.reshape(Wt[n].shape) for o in outs]

    full_shapes = [SG[n].shape for n in SMALL]
    summed = _sum_slabs("small_sum", _broadcast_small("small_exchange", _pack([SG[n] for n in SMALL])))
    gsm = dict(zip(SMALL, _unpack(summed, full_shapes)))
    for n in SMALL_SHARDED:
        ax = SMALL_SHARD_AXIS[n]
        width = Wt[n].shape[ax]
        gsm[n] = lax.dynamic_slice_in_dim(gsm[n], my_chip * width, width, axis=ax)
    own_shapes = [Wt[n].shape for n in SMALL]
    packed = [_pack([src[n] for n in SMALL]) for src in (Wt, gsm, Mo, Vo)]
    rows_small = packed[0].shape[0]

    def small_fn(w_, g_, m_, v_):
        return _adamw_math(w_, g_, m_, v_)

    sd, sm, sv = _rowwise("adamw_small", small_fn, packed, [('rows', LANES, F32)] * 3, rows_small, tm=rows_small)
    for n, d_, m_, v_ in zip(SMALL, _unpack(sd, own_shapes), _unpack(sm, own_shapes), _unpack(sv, own_shapes)):
        grads[n], deltas[n], new_m[n], new_v[n] = gsm[n], d_, m_, v_

    return (loss, grad_x, *[grads[n] for n in WEIGHTS], *[deltas[n] for n in WEIGHTS],
            *[new_m[n] for n in WEIGHTS], *[new_v[n] for n in WEIGHTS])
```

```python
import functools
import math

import jax
import jax.numpy as jnp
from jax import lax
from jax.experimental import pallas as pl
from jax.experimental.pallas import tpu as pltpu

F32 = jnp.float32
BF16 = jnp.bfloat16
MESH_IDS = pl.DeviceIdType.MESH
HIGHEST = lax.Precision.HIGHEST

N_CHIPS = 4
DEPTH = 4
N_FOX = 2
HEAD_DIM = 128
CHUNK = 64
LEFT_CHUNKS = 8
BAND_KEYS = (LEFT_CHUNKS + 1) * CHUNK
PAD_KEYS = LEFT_CHUNKS * CHUNK
REL_CLIP = 128
REL_TABLE = 2 * REL_CLIP + 1
REL_TABLE_PAD = 384
CONV_K = 31
CONV_HALO = 32
ALPHA = (2.0 * DEPTH) ** 0.25
LN_EPS = 1e-5
ADAM_LR, ADAM_B1, ADAM_B2, ADAM_EPS, ADAM_WD, ADAM_STEP = 0.001, 0.9, 0.999, 1e-08, 0.01, 10
NEG_BIG = -1e30
VMEM_LIMIT_V7X = 56 * 1024 * 1024
LANES = 128
SUBLANES = 8


def _cparams(*sem):
    return pltpu.CompilerParams(dimension_semantics=sem if sem else None, vmem_limit_bytes=VMEM_LIMIT_V7X)


def _pick(dim, target):
    best = None
    for t in range(LANES, min(dim, target) + 1, LANES):
        if dim % t == 0:
            best = t
    return best if best is not None else dim


def _dot_nt(a, b):
    return lax.dot_general(a, b, (((1,), (1,)), ((), ())), preferred_element_type=F32)


def _dot_tn(a, b):
    return lax.dot_general(a, b, (((0,), (0,)), ((), ())), preferred_element_type=F32)


def _sigmoid(z):
    return 1.0 / (1.0 + jnp.exp(-z))


def _rowwise(name, fn, ins, outs, S, tm=256):
    tm = min(tm, S)
    arrs, in_specs = [], []
    for it in ins:
        if isinstance(it, tuple) and it[0] == 'full':
            a = it[1]
            in_specs.append(pl.BlockSpec(a.shape, lambda i, _n=a.ndim: (0,) * _n))
        elif isinstance(it, tuple) and it[0] == 'cols':
            _, a, width, blk = it
            in_specs.append(pl.BlockSpec((tm, width), lambda i, _b=blk: (i, _b)))
        elif isinstance(it, tuple) and it[0] == 'off':
            _, a, off = it
            in_specs.append(pl.BlockSpec((tm, a.shape[1]), lambda i, _o=off: (i + _o, 0)))
        else:
            a = it
            in_specs.append(pl.BlockSpec((tm, a.shape[1]), lambda i: (i, 0)))
        arrs.append(a)
    out_shape, out_specs = [], []
    for kind, shp, dt in outs:
        if kind == 'rows':
            out_shape.append(jax.ShapeDtypeStruct((S, shp), dt))
            out_specs.append(pl.BlockSpec((tm, shp), lambda i: (i, 0)))
        else:
            out_shape.append(jax.ShapeDtypeStruct(shp, dt))
            out_specs.append(pl.BlockSpec(shp, lambda i, _n=len(shp): (0,) * _n))
    n_in = len(arrs)

    def body(*refs):
        vals = fn(*[r[...] for r in refs[:n_in]])
        first = pl.program_id(0) == 0
        for (kind, _, _), r, v in zip(outs, refs[n_in:], vals):
            if kind == 'rows':
                r[...] = v.astype(r.dtype)
            else:
                @pl.when(first)
                def _(r=r, v=v):
                    r[...] = v.astype(r.dtype)

                @pl.when(jnp.logical_not(first))
                def _(r=r, v=v):
                    r[...] += v.astype(r.dtype)

    has_acc = any(k != 'rows' for k, _, _ in outs)
    res = pl.pallas_call(
        body, name=name, grid=(S // tm,), in_specs=in_specs, out_specs=out_specs, out_shape=out_shape,
        compiler_params=_cparams("arbitrary" if has_acc else "parallel"),
    )(*arrs)
    return res


def _mm_col(name, a, wg, row_start, bias=None, out_dtype=BF16):
    S, K = a.shape
    _, _, Ns = wg.shape
    rb = row_start // K
    tm = min(512, S)

    def body(a_ref, w_ref, *rest):
        acc = jnp.dot(a_ref[...].astype(BF16), w_ref[...], preferred_element_type=F32)
        if bias is not None:
            acc = acc + rest[0][...]
        rest[-1][...] = acc.astype(out_dtype)

    in_specs = [pl.BlockSpec((tm, K), lambda j, m: (m, 0)), pl.BlockSpec((None, K, Ns), lambda j, m: (j, rb, 0))]
    args = [a, wg]
    if bias is not None:
        in_specs.append(pl.BlockSpec((1, Ns), lambda j, m: (0, j)))
        args.append(bias)
    return pl.pallas_call(
        body, name=name, grid=(N_CHIPS, S // tm), in_specs=in_specs,
        out_specs=pl.BlockSpec((tm, Ns), lambda j, m: (m, j)),
        out_shape=jax.ShapeDtypeStruct((S, N_CHIPS * Ns), out_dtype),
        compiler_params=_cparams("parallel", "parallel"),
    )(*args)


def _mm_row(name, a, wg, row_start, Ks, bias=None):
    S = a.shape[0]
    N = wg.shape[2]
    rb = row_start // Ks
    tm = min(512, S)

    def body(a_ref, w_ref, *rest):
        o_ref = rest[-1]
        j = pl.program_id(1)
        d = jnp.dot(a_ref[...].astype(BF16), w_ref[...], preferred_element_type=F32)

        @pl.when(j == 0)
        def _():
            o_ref[...] = d + rest[0][...] if bias is not None else d

        @pl.when(j > 0)
        def _():
            o_ref[...] += d

    in_specs = [pl.BlockSpec((tm, Ks), lambda m, j: (m, j)), pl.BlockSpec((None, Ks, N), lambda m, j: (j, rb, 0))]
    args = [a, wg]
    if bias is not None:
        in_specs.append(pl.BlockSpec((1, N), lambda m, j: (0, 0)))
        args.append(bias)
    return pl.pallas_call(
        body, name=name, grid=(S // tm, N_CHIPS), in_specs=in_specs,
        out_specs=pl.BlockSpec((tm, N), lambda m, j: (m, 0)),
        out_shape=jax.ShapeDtypeStruct((S, N), F32),
        compiler_params=_cparams("parallel", "arbitrary"),
    )(*args)


def _mm_col_t(name, pairs, wg, K):
    S = pairs[0][0].shape[0]
    Ns = wg.shape[2]
    tm = min(512, S)
    n = len(pairs)

    def body(*refs):
        o_ref = refs[-1]
        j = pl.program_id(1)
        d = _dot_nt(refs[0][...], refs[n][...])
        for p in range(1, n):
            d = d + _dot_nt(refs[p][...], refs[n + p][...])

        @pl.when(j == 0)
        def _():
            o_ref[...] = d

        @pl.when(j > 0)
        def _():
            o_ref[...] += d

    in_specs = [pl.BlockSpec((tm, Ns), lambda m, j: (m, j)) for _ in pairs]
    in_specs += [pl.BlockSpec((None, K, Ns), lambda m, j, _rb=rs // K: (j, _rb, 0)) for _, rs in pairs]
    return pl.pallas_call(
        body, name=name, grid=(S // tm, N_CHIPS), in_specs=in_specs,
        out_specs=pl.BlockSpec((tm, K), lambda m, j: (m, 0)),
        out_shape=jax.ShapeDtypeStruct((S, K), F32),
        compiler_params=_cparams("parallel", "arbitrary"),
    )(*[dy for dy, _ in pairs], *[wg for _ in pairs])


def _mm_row_t(name, dy, wg, row_start, Ks, out_dtype):
    S, N = dy.shape
    rb = row_start // Ks
    tm = min(512, S)

    def body(dy_ref, w_ref, o_ref):
        o_ref[...] = _dot_nt(dy_ref[...], w_ref[...]).astype(out_dtype)

    return pl.pallas_call(
        body, name=name, grid=(N_CHIPS, S // tm),
        in_specs=[pl.BlockSpec((tm, N), lambda j, m: (m, 0)), pl.BlockSpec((None, Ks, N), lambda j, m: (j, rb, 0))],
        out_specs=pl.BlockSpec((tm, Ks), lambda j, m: (m, j)),
        out_shape=jax.ShapeDtypeStruct((S, N_CHIPS * Ks), out_dtype),
        compiler_params=_cparams("parallel", "parallel"),
    )(dy, wg)


def _mm_dw(name, a, dy, dg, row_start, kind):
    S = a.shape[0]
    _, _, W = dg.shape
    if kind == 'col':
        K = a.shape[1]
        rows = K
        tk, tn = _pick(K, 512), W
        a_map = lambda j, nb, kb: (0, kb)
        dy_map = lambda j, nb, kb: (0, j * (W // tn) + nb)
    else:
        rows = a.shape[1] // N_CHIPS
        tk = rows if rows * S * 2 * 2 <= 12 * 1024 * 1024 else _pick(rows, 512)
        tn = _pick(W, 1024)
        a_map = lambda j, nb, kb: (0, j * (rows // tk) + kb)
        dy_map = lambda j, nb, kb: (0, nb)
    rb = row_start // tk
    assert row_start % tk == 0

    def body(a_ref, dy_ref, dg_in, o_ref):
        del dg_in
        o_ref[...] = _dot_tn(a_ref[...], dy_ref[...]).astype(o_ref.dtype)

    return pl.pallas_call(
        body, name=name, grid=(N_CHIPS, W // tn, rows // tk),
        in_specs=[pl.BlockSpec((S, tk), a_map), pl.BlockSpec((S, tn), dy_map), pl.BlockSpec(memory_space=pl.ANY)],
        out_specs=pl.BlockSpec((None, tk, tn), lambda j, nb, kb: (j, rb + kb, nb)),
        out_shape=jax.ShapeDtypeStruct(dg.shape, dg.dtype),
        input_output_aliases={2: 0},
        compiler_params=_cparams("parallel", "parallel", "parallel"),
    )(a, dy, dg)


def _fox_probs(q, k, c_blk, crow, h, qi, tq):
    n = k.shape[0]
    s = _dot_nt(q, k) * (HEAD_DIM ** -0.5)
    lane = lax.broadcasted_iota(jnp.int32, c_blk.shape, 1)
    ccol = jnp.sum(jnp.where(lane == h, c_blk, 0.0), axis=1, keepdims=True)
    s = s + (ccol - crow)
    t_idx = qi * tq + lax.broadcasted_iota(jnp.int32, (tq, n), 0)
    s_idx = lax.broadcasted_iota(jnp.int32, (tq, n), 1)
    s = jnp.where(s_idx <= t_idx, s, NEG_BIG)
    p = jnp.exp(s - jnp.max(s, axis=1, keepdims=True))
    return p / jnp.sum(p, axis=1, keepdims=True)


def _per_query_block(qi, nq, tq, fn):
    for qv in range(nq):
        @pl.when(qi == qv)
        def _(qv=qv):
            fn(qv, (qv + 1) * tq)


def _fox_fwd(qkv, c, crow, H):
    S = qkv.shape[0]
    tq = min(256, S)

    def body(q_ref, k_ref, v_ref, c_ref, crow_ref, o_ref):
        def block(qv, n):
            p = _fox_probs(q_ref[...], k_ref[0:n, :], c_ref[...], crow_ref[:, 0:n], pl.program_id(0), qv, tq)
            o_ref[...] = jnp.dot(p.astype(BF16), v_ref[0:n, :], preferred_element_type=F32).astype(o_ref.dtype)

        _per_query_block(pl.program_id(1), S // tq, tq, block)

    return pl.pallas_call(
        body, name="fox_attn_fwd", grid=(H, S // tq),
        in_specs=[pl.BlockSpec((tq, HEAD_DIM), lambda h, i: (i, h)),
                  pl.BlockSpec((S, HEAD_DIM), lambda h, i: (0, H + h)),
                  pl.BlockSpec((S, HEAD_DIM), lambda h, i: (0, 2 * H + h)),
                  pl.BlockSpec((tq, H), lambda h, i: (i, 0)),
                  pl.BlockSpec((None, 1, S), lambda h, i: (h, 0, 0))],
        out_specs=pl.BlockSpec((tq, HEAD_DIM), lambda h, i: (i, h)),
        out_shape=jax.ShapeDtypeStruct((S, H * HEAD_DIM), BF16),
        compiler_params=_cparams("parallel", "parallel"),
    )(qkv, qkv, qkv, c, crow)


def _fox_bwd(qkv, c, crow, do, H):
    S = qkv.shape[0]
    tq = min(256, S)
    nq = S // tq

    def body(q_ref, k_ref, v_ref, c_ref, crow_ref, do_ref, dq_ref, dk_ref, dv_ref, dc_ref, dk_acc, dv_acc):
        qi = pl.program_id(1)

        @pl.when(qi == 0)
        def _():
            dk_acc[...] = jnp.zeros_like(dk_acc)
            dv_acc[...] = jnp.zeros_like(dv_acc)
            dc_ref[...] = jnp.zeros_like(dc_ref)

        def block(qv, n):
            q, k, v, do_ = q_ref[...], k_ref[0:n, :], v_ref[0:n, :], do_ref[...]
            p = _fox_probs(q, k, c_ref[...], crow_ref[:, 0:n], pl.program_id(0), qv, tq)
            dv_acc[0:n, :] += _dot_tn(p.astype(BF16), do_)
            dp = _dot_nt(do_, v)
            ds = p * (dp - jnp.sum(p * dp, axis=1, keepdims=True))
            dsb = (ds * (HEAD_DIM ** -0.5)).astype(BF16)
            dq_ref[...] = jnp.dot(dsb, k, preferred_element_type=F32).astype(dq_ref.dtype)
            dk_acc[0:n, :] += _dot_tn(dsb, q)
            dc_ref[:, 0:n] += -jnp.sum(ds, axis=0, keepdims=True)

        _per_query_block(qi, nq, tq, block)

        @pl.when(qi == nq - 1)
        def _():
            dk_ref[...] = dk_acc[...].astype(dk_ref.dtype)
            dv_ref[...] = dv_acc[...].astype(dv_ref.dtype)

    D = H * HEAD_DIM
    return pl.pallas_call(
        body, name="fox_attn_bwd", grid=(H, nq),
        in_specs=[pl.BlockSpec((tq, HEAD_DIM), lambda h, i: (i, h)),
                  pl.BlockSpec((S, HEAD_DIM), lambda h, i: (0, H + h)),
                  pl.BlockSpec((S, HEAD_DIM), lambda h, i: (0, 2 * H + h)),
                  pl.BlockSpec((tq, H), lambda h, i: (i, 0)),
                  pl.BlockSpec((None, 1, S), lambda h, i: (h, 0, 0)),
                  pl.BlockSpec((tq, HEAD_DIM), lambda h, i: (i, h))],
        out_specs=[pl.BlockSpec((tq, HEAD_DIM), lambda h, i: (i, h)),
                   pl.BlockSpec((S, HEAD_DIM), lambda h, i: (0, h)),
                   pl.BlockSpec((S, HEAD_DIM), lambda h, i: (0, h)),
                   pl.BlockSpec((None, 1, S), lambda h, i: (h, 0, 0))],
        out_shape=[jax.ShapeDtypeStruct((S, D), BF16), jax.ShapeDtypeStruct((S, D), BF16),
                   jax.ShapeDtypeStruct((S, D), BF16), jax.ShapeDtypeStruct((H, 1, S), F32)],
        scratch_shapes=[pltpu.VMEM((S, HEAD_DIM), F32), pltpu.VMEM((S, HEAD_DIM), F32)],
        compiler_params=_cparams("parallel", "arbitrary"),
    )(qkv, qkv, qkv, c, crow, do)


def _cumsum_rows(name, xin, reverse):
    S, H = xin.shape
    tb = min(256, S)
    nb = S // tb

    def body(x_ref, o_ref):
        r = lax.broadcasted_iota(jnp.int32, (tb, tb), 0)
        cidx = lax.broadcasted_iota(jnp.int32, (tb, tb), 1)
        tri = (r <= cidx if reverse else r >= cidx).astype(F32)

        def step(b, carry):
            bb = nb - 1 - b if reverse else b
            rows = pl.ds(pl.multiple_of(bb * tb, tb), tb)
            blk = x_ref[rows, :]
            o_ref[rows, :] = jnp.dot(tri, blk, precision=HIGHEST, preferred_element_type=F32) + carry
            return carry + jnp.sum(blk, axis=0, keepdims=True)

        lax.fori_loop(0, nb, step, jnp.zeros((1, H), F32))

    return pl.pallas_call(
        body, name=name, out_shape=jax.ShapeDtypeStruct((S, H), F32),
        in_specs=[pl.BlockSpec(memory_space=pltpu.VMEM)], out_specs=pl.BlockSpec(memory_space=pltpu.VMEM),
        compiler_params=_cparams(),
    )(xin)


def _rel_onehot(i, transposed):
    shp = (REL_TABLE_PAD, BAND_KEYS) if transposed else (BAND_KEYS, REL_TABLE_PAD)
    j = lax.broadcasted_iota(jnp.int32, shp, 1 if transposed else 0)
    r = lax.broadcasted_iota(jnp.int32, shp, 0 if transposed else 1)
    return (jnp.clip(PAD_KEYS + i - j, -REL_CLIP, REL_CLIP) + REL_CLIP == r).astype(F32)


def _rel_expand(rb_pad):
    H = rb_pad.shape[0]

    def body(rb_ref, o_ref):
        def step(i, _):
            o_ref[i] = jnp.dot(rb_ref[...], _rel_onehot(i, True), precision=HIGHEST, preferred_element_type=F32)
            return 0
        lax.fori_loop(0, CHUNK, step, 0)

    return pl.pallas_call(
        body, name="rel_bias_expand", out_shape=jax.ShapeDtypeStruct((CHUNK, H, BAND_KEYS), F32),
        in_specs=[pl.BlockSpec(memory_space=pltpu.VMEM)], out_specs=pl.BlockSpec(memory_space=pltpu.VMEM),
        compiler_params=_cparams(),
    )(rb_pad)


def _rel_reduce(dbt):
    H = dbt.shape[1]

    def body(d_ref, o_ref):
        def step(i, acc):
            return acc + jnp.dot(d_ref[i], _rel_onehot(i, False), precision=HIGHEST, preferred_element_type=F32)
        o_ref[...] = lax.fori_loop(0, CHUNK, step, jnp.zeros((H, REL_TABLE_PAD), F32))

    return pl.pallas_call(
        body, name="rel_bias_reduce", out_shape=jax.ShapeDtypeStruct((H, REL_TABLE_PAD), F32),
        in_specs=[pl.BlockSpec(memory_space=pltpu.VMEM)], out_specs=pl.BlockSpec(memory_space=pltpu.VMEM),
        compiler_params=_cparams(),
    )(dbt)


def _rel_probs(q, kb, bias, n):
    s = _dot_nt(q, kb) * (HEAD_DIM ** -0.5) + bias
    j = lax.broadcasted_iota(jnp.int32, (CHUNK, BAND_KEYS), 1)
    s = jnp.where(j >= PAD_KEYS - n * CHUNK, s, NEG_BIG)
    p = jnp.exp(s - jnp.max(s, axis=1, keepdims=True))
    return p / jnp.sum(p, axis=1, keepdims=True)


def _rel_fwd(qkv, bias, H):
    S = qkv.shape[0]

    def body(q_ref, k_ref, v_ref, b_ref, o_ref, kpad, vpad):
        kpad[0:PAD_KEYS, :] = jnp.zeros((PAD_KEYS, HEAD_DIM), BF16)
        vpad[0:PAD_KEYS, :] = jnp.zeros((PAD_KEYS, HEAD_DIM), BF16)
        kpad[PAD_KEYS:PAD_KEYS + S, :] = k_ref[...]
        vpad[PAD_KEYS:PAD_KEYS + S, :] = v_ref[...]
        bias_t = b_ref[...]

        def chunk(n, _):
            rows = pl.ds(pl.multiple_of(n * CHUNK, CHUNK), CHUNK)
            band = pl.ds(pl.multiple_of(n * CHUNK, CHUNK), BAND_KEYS)
            p = _rel_probs(q_ref[rows, :], kpad[band, :], bias_t, n)
            o_ref[rows, :] = jnp.dot(p.astype(BF16), vpad[band, :], preferred_element_type=F32).astype(o_ref.dtype)
            return 0

        lax.fori_loop(0, S // CHUNK, chunk, 0)

    return pl.pallas_call(
        body, name="rel_attn_fwd", grid=(H,),
        in_specs=[pl.BlockSpec((S, HEAD_DIM), lambda h: (0, h)),
                  pl.BlockSpec((S, HEAD_DIM), lambda h: (0, H + h)),
                  pl.BlockSpec((S, HEAD_DIM), lambda h: (0, 2 * H + h)),
                  pl.BlockSpec((None, CHUNK, BAND_KEYS), lambda h: (h, 0, 0))],
        out_specs=pl.BlockSpec((S, HEAD_DIM), lambda h: (0, h)),
        out_shape=jax.ShapeDtypeStruct((S, H * HEAD_DIM), BF16),
        scratch_shapes=[pltpu.VMEM((S + PAD_KEYS, HEAD_DIM), BF16), pltpu.VMEM((S + PAD_KEYS, HEAD_DIM), BF16)],
        compiler_params=_cparams("parallel"),
    )(qkv, qkv, qkv, bias)


def _rel_bwd(qkv, bias, do, H):
    S = qkv.shape[0]
    D = H * HEAD_DIM

    def body(q_ref, k_ref, v_ref, b_ref, do_ref, dq_ref, dk_ref, dv_ref, db_ref, kpad, vpad, dkpad, dvpad):
        kpad[0:PAD_KEYS, :] = jnp.zeros((PAD_KEYS, HEAD_DIM), BF16)
        vpad[0:PAD_KEYS, :] = jnp.zeros((PAD_KEYS, HEAD_DIM), BF16)
        kpad[PAD_KEYS:PAD_KEYS + S, :] = k_ref[...]
        vpad[PAD_KEYS:PAD_KEYS + S, :] = v_ref[...]
        dkpad[...] = jnp.zeros_like(dkpad)
        dvpad[...] = jnp.zeros_like(dvpad)
        db_ref[...] = jnp.zeros_like(db_ref)
        bias_t = b_ref[...]

        def chunk(n, _):
            rows = pl.ds(pl.multiple_of(n * CHUNK, CHUNK), CHUNK)
            band = pl.ds(pl.multiple_of(n * CHUNK, CHUNK), BAND_KEYS)
            q, kb, vb, do_ = q_ref[rows, :], kpad[band, :], vpad[band, :], do_ref[rows, :]
            p = _rel_probs(q, kb, bias_t, n)
            dvpad[band, :] += _dot_tn(p.astype(BF16), do_)
            dp = _dot_nt(do_, vb)
            ds = p * (dp - jnp.sum(p * dp, axis=1, keepdims=True))
            db_ref[...] += ds
            dsb = (ds * (HEAD_DIM ** -0.5)).astype(BF16)
            dq_ref[rows, :] = jnp.dot(dsb, kb, preferred_element_type=F32).astype(dq_ref.dtype)
            dkpad[band, :] += _dot_tn(dsb, q)
            return 0

        lax.fori_loop(0, S // CHUNK, chunk, 0)
        dk_ref[...] = dkpad[PAD_KEYS:PAD_KEYS + S, :].astype(dk_ref.dtype)
        dv_ref[...] = dvpad[PAD_KEYS:PAD_KEYS + S, :].astype(dv_ref.dtype)

    head = lambda h: (0, h)
    return pl.pallas_call(
        body, name="rel_attn_bwd", grid=(H,),
        in_specs=[pl.BlockSpec((S, HEAD_DIM), head),
                  pl.BlockSpec((S, HEAD_DIM), lambda h: (0, H + h)),
                  pl.BlockSpec((S, HEAD_DIM), lambda h: (0, 2 * H + h)),
                  pl.BlockSpec((None, CHUNK, BAND_KEYS), lambda h: (h, 0, 0)),
                  pl.BlockSpec((S, HEAD_DIM), head)],
        out_specs=[pl.BlockSpec((S, HEAD_DIM), head), pl.BlockSpec((S, HEAD_DIM), head), pl.BlockSpec((S, HEAD_DIM), head),
                   pl.BlockSpec((None, CHUNK, BAND_KEYS), lambda h: (h, 0, 0))],
        out_shape=[jax.ShapeDtypeStruct((S, D), BF16), jax.ShapeDtypeStruct((S, D), BF16), jax.ShapeDtypeStruct((S, D), BF16),
                   jax.ShapeDtypeStruct((H, CHUNK, BAND_KEYS), F32)],
        scratch_shapes=[pltpu.VMEM((S + PAD_KEYS, HEAD_DIM), BF16), pltpu.VMEM((S + PAD_KEYS, HEAD_DIM), BF16),
                        pltpu.VMEM((S + PAD_KEYS, HEAD_DIM), F32), pltpu.VMEM((S + PAD_KEYS, HEAD_DIM), F32)],
        compiler_params=_cparams("parallel"),
    )(qkv, qkv, qkv, bias, do)


def _conv_taps(win, tt, reverse):
    n = tt + 2 * CONV_HALO
    for k in range(CONV_K):
        off = (CONV_K - 1 - k) if reverse else (k - (CONV_K - 1))
        sh = (-off) % n
        rolled = pltpu.roll(win, sh, 0) if sh else win
        yield k, rolled[CONV_HALO:CONV_HALO + tt, :]


def _fill_padded(pad_ref, x_ref, S):
    tc = pad_ref.shape[1]
    pad_ref[0:CONV_HALO, :] = jnp.zeros((CONV_HALO, tc), F32)
    pad_ref[CONV_HALO + S:CONV_HALO + S + CONV_HALO, :] = jnp.zeros((CONV_HALO, tc), F32)
    pad_ref[CONV_HALO:CONV_HALO + S, :] = x_ref[...]


def _dwconv(name, xin, w32, bias, reverse):
    S, D = xin.shape
    tc = min(256, D)
    tt = min(256, S)

    def body(x_ref, w_ref, b_ref, y_ref, pad_ref):
        _fill_padded(pad_ref, x_ref, S)
        def tile(ti, _):
            t0 = pl.multiple_of(ti * tt, tt)
            win = pad_ref[pl.ds(t0, tt + 2 * CONV_HALO), :]
            acc = jnp.zeros((tt, tc), F32) + b_ref[...]
            for k, shifted in _conv_taps(win, tt, reverse):
                acc = acc + w_ref[pl.ds(k, 1), :] * shifted
            y_ref[pl.ds(t0, tt), :] = acc
            return 0

        lax.fori_loop(0, S // tt, tile, 0)

    return pl.pallas_call(
        body, name=name, grid=(D // tc,),
        in_specs=[pl.BlockSpec((S, tc), lambda i: (0, i)), pl.BlockSpec((CONV_HALO, tc), lambda i: (0, i)),
                  pl.BlockSpec((1, tc), lambda i: (0, i))],
        out_specs=pl.BlockSpec((S, tc), lambda i: (0, i)),
        out_shape=jax.ShapeDtypeStruct((S, D), F32),
        scratch_shapes=[pltpu.VMEM((S + 2 * CONV_HALO, tc), F32)],
        compiler_params=_cparams("parallel"),
    )(xin, w32, bias)


def _dwconv_dw(xin, dy):
    S, D = xin.shape
    tc = min(256, D)
    tt = min(256, S)

    def body(x_ref, dy_ref, o_ref, pad_ref):
        _fill_padded(pad_ref, x_ref, S)

        def tile(ti, acc):
            t0 = pl.multiple_of(ti * tt, tt)
            win = pad_ref[pl.ds(t0, tt + 2 * CONV_HALO), :]
            dyt = dy_ref[pl.ds(t0, tt), :]
            ridx = lax.broadcasted_iota(jnp.int32, (CONV_HALO, tc), 0)
            upd = jnp.zeros((CONV_HALO, tc), F32)
            for k, shifted in _conv_taps(win, tt, False):
                upd = jnp.where(ridx == k, jnp.sum(dyt * shifted, axis=0, keepdims=True), upd)
            return acc + upd

        o_ref[...] = lax.fori_loop(0, S // tt, tile, jnp.zeros((CONV_HALO, tc), F32))

    return pl.pallas_call(
        body, name="dwconv_dw", grid=(D // tc,),
        in_specs=[pl.BlockSpec((S, tc), lambda i: (0, i)), pl.BlockSpec((S, tc), lambda i: (0, i))],
        out_specs=pl.BlockSpec((CONV_HALO, tc), lambda i: (0, i)),
        out_shape=jax.ShapeDtypeStruct((CONV_HALO, D), F32),
        scratch_shapes=[pltpu.VMEM((S + 2 * CONV_HALO, tc), F32)],
        compiler_params=_cparams("parallel"),
    )(xin, dy)


def _place():
    x, y, c = lax.axis_index("x"), lax.axis_index("y"), lax.axis_index("c")
    chips = [(1 - x, y), (x, 1 - y), (1 - x, 1 - y)]
    return x, y, c, chips


def _remote(src, dst, ssem, rsem, dev):
    return pltpu.make_async_remote_copy(src_ref=src, dst_ref=dst, send_sem=ssem, recv_sem=rsem,
                                        device_id=dev, device_id_type=MESH_IDS)


_ANY = pl.BlockSpec(memory_space=pl.ANY)


def _place_own(name, own, place):
    R, W = own.shape
    tr = _pick_rows(R)

    def body(p_ref, a_ref, o_ref):
        del p_ref
        o_ref[...] = a_ref[...]

    return pl.pallas_call(
        body, name=name,
        grid_spec=pltpu.PrefetchScalarGridSpec(
            num_scalar_prefetch=1, grid=(R // tr,),
            in_specs=[pl.BlockSpec((tr, W), lambda i, p: (i, 0))],
            out_specs=pl.BlockSpec((None, tr, W), lambda i, p: (p[1], i, 0))),
        out_shape=jax.ShapeDtypeStruct((N_CHIPS, R, W), own.dtype),
        compiler_params=_cparams("parallel"),
    )(place, own)


_HBM = pl.BlockSpec(memory_space=pltpu.HBM)
_SEM = pl.BlockSpec(memory_space=pltpu.SEMAPHORE)
GROUPS = ('qkv', 'ffn', 'pw1', 'dm')


def _half_rows(c, r0, n):
    return pl.ds(pl.multiple_of(r0 + c * (n // 2), SUBLANES), n // 2)


def _gather_start(wgs, layers):
    G, L = len(wgs), len(layers)

    def body(*refs):
        outs = refs[G:]
        ssems, rsems, bufs = outs[:L], outs[L:2 * L], outs[2 * L:]
        x, y, c, chips = _place()
        me = 2 * x + y
        for li, pieces in enumerate(layers):
            for pi, (g, r0, n) in enumerate(pieces):
                blk = bufs[g].at[me, _half_rows(c, r0, n)]
                for j, (px, py) in enumerate(chips):
                    _remote(blk, blk, ssems[li].at[3 * pi + j], rsems[li].at[3 * pi + j], (px, py, c)).start()

    sem_shapes = [pltpu.SemaphoreType.DMA((3 * len(p),)) for p in layers]
    res = pl.pallas_call(
        body, name="gather_start", in_specs=[_HBM] * G,
        out_specs=[_SEM] * (2 * L) + [_HBM] * G,
        out_shape=sem_shapes + sem_shapes + [pltpu.HBM(w.shape, w.dtype) for w in wgs],
        input_output_aliases={g: 2 * L + g for g in range(G)},
        compiler_params=pltpu.CompilerParams(has_side_effects=pltpu.SideEffectType.DATAFLOW_SIDE_EFFECTING),
    )(*[pltpu.with_memory_space_constraint(w, pltpu.HBM) for w in wgs])
    return res[:L], res[L:2 * L], list(res[2 * L:])


def _gather_wait(name, wgs, ssem, rsem, pieces, after):
    G = len(wgs)

    def body(*refs):
        ssem_ref, rsem_ref = refs[G], refs[G + 1]
        bufs = refs[G + 3:]
        x, y, c, chips = _place()
        me = 2 * x + y
        for pi, (g, r0, n) in enumerate(pieces):
            rows = _half_rows(c, r0, n)
            for j, (px, py) in enumerate(chips):
                cp = _remote(bufs[g].at[me, rows], bufs[g].at[2 * px + py, rows],
                             ssem_ref.at[3 * pi + j], rsem_ref.at[3 * pi + j], (px, py, c))
                cp.wait_send()
                cp.wait_recv()

    return list(pl.pallas_call(
        body, name=name, in_specs=[_HBM] * G + [_SEM, _SEM, _ANY], out_specs=[_HBM] * G,
        out_shape=[pltpu.HBM(w.shape, w.dtype) for w in wgs],
        input_output_aliases={g: g for g in range(G)},
        compiler_params=pltpu.CompilerParams(has_side_effects=pltpu.SideEffectType.DATAFLOW_SIDE_EFFECTING),
    )(*wgs, ssem, rsem, after))


def _gather_forward(name, wgs, pieces):
    G = len(wgs)
    n_cp = 3 * len(pieces)

    def body(*refs):
        bufs, ssems, rsems = refs[G:2 * G], refs[2 * G], refs[2 * G + 1]
        x, y, c, chips = _place()
        sib = (x, y, 1 - c)
        cps = []
        for pi, (g, r0, n) in enumerate(pieces):
            for j, (px, py) in enumerate(chips):
                blk = bufs[g].at[2 * px + py, _half_rows(c, r0, n)]
                cps.append(_remote(blk, blk, ssems.at[3 * pi + j], rsems.at[3 * pi + j], sib))
        for cp in cps:
            cp.start()
        for pi, (g, r0, n) in enumerate(pieces):
            for j, (px, py) in enumerate(chips):
                blk = bufs[g].at[2 * px + py, _half_rows(1 - c, r0, n)]
                _remote(blk, blk, ssems.at[3 * pi + j], rsems.at[3 * pi + j], sib).wait_recv()
        for cp in cps:
            cp.wait_send()

    return list(pl.pallas_call(
        body, name=name, in_specs=[_ANY] * G, out_specs=[_ANY] * G,
        out_shape=[jax.ShapeDtypeStruct(w.shape, w.dtype) for w in wgs],
        input_output_aliases={g: g for g in range(G)},
        scratch_shapes=[pltpu.SemaphoreType.DMA((n_cp,)), pltpu.SemaphoreType.DMA((n_cp,))],
        compiler_params=pltpu.CompilerParams(has_side_effects=True),
    )(*wgs))


def _swap_halves(name, dg):
    _, R, W = dg.shape
    Rh = R // 2

    def body(src, land, ssems, rsems):
        x, y, c, _ = _place()
        theirs = pl.ds(pl.multiple_of((1 - c) * Rh, SUBLANES), Rh)
        cps = [_remote(src.at[j, theirs], land.at[j], ssems.at[j], rsems.at[j], (x, y, 1 - c)) for j in range(N_CHIPS)]
        for cp in cps:
            cp.start()
        for cp in cps:
            cp.wait()

    return pl.pallas_call(
        body, name=name, in_specs=[_ANY], out_specs=_ANY,
        out_shape=jax.ShapeDtypeStruct((N_CHIPS, Rh, W), dg.dtype),
        scratch_shapes=[pltpu.SemaphoreType.DMA((N_CHIPS,)), pltpu.SemaphoreType.DMA((N_CHIPS,))],
        compiler_params=pltpu.CompilerParams(has_side_effects=True),
    )(dg)


def _scatter_chips(name, pb):
    def body(src, land, ssems, rsems):
        x, y, c, chips = _place()
        me = 2 * x + y
        cps = [_remote(src.at[2 * px + py], land.at[me], ssems.at[j], rsems.at[j], (px, py, c))
               for j, (px, py) in enumerate(chips)]
        for cp in cps:
            cp.start()
        for j, (px, py) in enumerate(chips):
            blk = land.at[2 * px + py]
            _remote(blk, blk, ssems.at[j], rsems.at[j], (px, py, c)).wait_recv()
        for cp in cps:
            cp.wait_send()

    return pl.pallas_call(
        body, name=name, in_specs=[_ANY], out_specs=_ANY,
        out_shape=jax.ShapeDtypeStruct(pb.shape, pb.dtype),
        scratch_shapes=[pltpu.SemaphoreType.DMA((3,)), pltpu.SemaphoreType.DMA((3,))],
        compiler_params=pltpu.CompilerParams(has_side_effects=True),
    )(pb)


def _share_halves(name, gf):
    R, W = gf.shape
    Rh = R // 2

    def body(in_ref, out, ssem, rsem):
        del in_ref
        x, y, c, _ = _place()
        mine = out.at[pl.ds(pl.multiple_of(c * Rh, SUBLANES), Rh)]
        theirs = out.at[pl.ds(pl.multiple_of((1 - c) * Rh, SUBLANES), Rh)]
        cp = _remote(mine, mine, ssem, rsem, (x, y, 1 - c))
        cp.start()
        _remote(theirs, theirs, ssem, rsem, (x, y, 1 - c)).wait_recv()
        cp.wait_send()

    return pl.pallas_call(
        body, name=name, in_specs=[_ANY], out_specs=_ANY,
        out_shape=jax.ShapeDtypeStruct(gf.shape, gf.dtype), input_output_aliases={0: 0},
        scratch_shapes=[pltpu.SemaphoreType.DMA, pltpu.SemaphoreType.DMA],
        compiler_params=pltpu.CompilerParams(has_side_effects=True),
    )(gf)


def _broadcast_small(name, buf):
    R = buf.shape[0]

    def body(src, out, ssems, rsems):
        x, y, c, _ = _place()
        me = 4 * x + 2 * y + c
        out[me] = src[...]
        peers = []
        for mask in range(1, 8):
            fx, fy, fc = (mask >> 2) & 1, (mask >> 1) & 1, mask & 1
            peers.append((1 - x if fx else x, 1 - y if fy else y, 1 - c if fc else c))
        cps = [_remote(src, out.at[me], ssems.at[k], rsems.at[k], p) for k, p in enumerate(peers)]
        for cp in cps:
            cp.start()
        for k, (px, py, pc) in enumerate(peers):
            blk = out.at[4 * px + 2 * py + pc]
            _remote(blk, blk, ssems.at[k], rsems.at[k], (px, py, pc)).wait_recv()
        for cp in cps:
            cp.wait_send()

    return pl.pallas_call(
        body, name=name, in_specs=[pl.BlockSpec(memory_space=pltpu.VMEM)], out_specs=pl.BlockSpec(memory_space=pltpu.VMEM),
        out_shape=jax.ShapeDtypeStruct((8, R, LANES), F32),
        scratch_shapes=[pltpu.SemaphoreType.DMA((7,)), pltpu.SemaphoreType.DMA((7,))],
        compiler_params=pltpu.CompilerParams(has_side_effects=True, vmem_limit_bytes=VMEM_LIMIT_V7X),
    )(buf)


def _sum_slabs(name, slabs):
    n, R, _ = slabs.shape

    def body(s_ref, o_ref):
        acc = s_ref[0]
        for k in range(1, n):
            acc = acc + s_ref[k]
        o_ref[...] = acc

    return pl.pallas_call(
        body, name=name, out_shape=jax.ShapeDtypeStruct((R, LANES), F32),
        in_specs=[pl.BlockSpec(memory_space=pltpu.VMEM)], out_specs=pl.BlockSpec(memory_space=pltpu.VMEM),
        compiler_params=_cparams(),
    )(slabs)


def _pair_sum(name, dg, land, c_arr):
    _, R, W = dg.shape
    Rh = R // 2
    tr = _pick_rows(Rh)
    nb = Rh // tr

    def body(c_ref, a_ref, b_ref, o_ref):
        del c_ref
        o_ref[...] = (a_ref[...].astype(F32) + b_ref[...].astype(F32)).astype(o_ref.dtype)

    return pl.pallas_call(
        body, name=name,
        grid_spec=pltpu.PrefetchScalarGridSpec(
            num_scalar_prefetch=1, grid=(N_CHIPS, nb),
            in_specs=[pl.BlockSpec((None, tr, W), lambda j, i, c_ref: (j, c_ref[0] * nb + i, 0)),
                      pl.BlockSpec((None, tr, W), lambda j, i, c_ref: (j, i, 0))],
            out_specs=pl.BlockSpec((None, tr, W), lambda j, i, c_ref: (j, i, 0))),
        out_shape=jax.ShapeDtypeStruct((N_CHIPS, Rh, W), BF16),
        compiler_params=_cparams("parallel", "parallel"),
    )(c_arr, dg, land)


def _pick_rows(rows, target=512):
    best = SUBLANES
    for t in range(SUBLANES, min(rows, target) + 1, SUBLANES):
        if rows % t == 0:
            best = t
    return best


def _chip_sum(name, pb, land, place):
    _, Rh, W = land.shape
    tr = _pick_rows(Rh)
    nb = Rh // tr

    def body(p_ref, own_ref, lx_ref, ly_ref, ld_ref, o_ref):
        del p_ref
        o_ref[...] = ((own_ref[...].astype(F32) + lx_ref[...].astype(F32)) + ly_ref[...].astype(F32)) + ld_ref[...].astype(F32)

    slab = lambda flip: pl.BlockSpec((None, tr, W), lambda i, p, _f=flip: (p[1] ^ _f, i, 0))
    return pl.pallas_call(
        body, name=name,
        grid_spec=pltpu.PrefetchScalarGridSpec(
            num_scalar_prefetch=1, grid=(nb,),
            in_specs=[slab(0), slab(2), slab(1), slab(3)],
            out_specs=pl.BlockSpec((tr, W), lambda i, p: (p[0] * nb + i, 0))),
        out_shape=jax.ShapeDtypeStruct((2 * Rh, W), F32),
        compiler_params=_cparams("parallel"),
    )(place, pb, land, land, land)


def _ln_stats(z):
    mu = jnp.mean(z, axis=1, keepdims=True)
    zc = z - mu
    rstd = lax.rsqrt(jnp.mean(zc * zc, axis=1, keepdims=True) + LN_EPS)
    return zc * rstd, rstd


def _ln_fwd(name, xin, m, g, b):
    S, D = xin.shape

    def fn(x_, m_, g_, b_):
        xhat, rstd = _ln_stats(ALPHA * x_ + m_)
        y = xhat * g_ + b_
        return y, y, xhat, rstd

    return _rowwise(name, fn, [xin, m, ('full', g), ('full', b)],
                    [('rows', D, F32), ('rows', D, BF16), ('rows', D, F32), ('rows', 1, F32)], S)


def _ln_bwd_core(dy, xhat, rstd, g):
    dxh = dy * g
    return rstd * (dxh - jnp.mean(dxh, axis=1, keepdims=True) - xhat * jnp.mean(dxh * xhat, axis=1, keepdims=True))


def _ln_bwd(name, terms, xhat, rstd, g):
    S, D = xhat.shape
    scales = [s for _, s in terms]
    n = len(terms)

    def fn(*v):
        dy = v[0] * scales[0] if scales[0] != 1.0 else v[0]
        for t in range(1, n):
            dy = dy + (v[t] * scales[t] if scales[t] != 1.0 else v[t])
        xh, rs, g_ = v[n], v[n + 1], v[n + 2]
        dz = _ln_bwd_core(dy, xh, rs, g_)
        return dz, dz, jnp.sum(dy * xh, axis=0, keepdims=True), jnp.sum(dy, axis=0, keepdims=True)

    return _rowwise(name, fn, [a for a, _ in terms] + [xhat, rstd, ('full', g)],
                    [('rows', D, F32), ('rows', D, BF16), ('acc', (1, D), F32), ('acc', (1, D), F32)], S)


def _adamw_math(w, g, m, v):
    m2 = ADAM_B1 * m + (1.0 - ADAM_B1) * g
    v2 = ADAM_B2 * v + (1.0 - ADAM_B2) * (g * g)
    m_hat = m2 / (1.0 - ADAM_B1 ** ADAM_STEP)
    v_hat = v2 / (1.0 - ADAM_B2 ** ADAM_STEP)
    delta = -ADAM_LR * (m_hat / (jnp.sqrt(v_hat) + ADAM_EPS) + ADAM_WD * w)
    return delta, m2, v2


def _adamw(name, w, gfull, row_start, m, v):
    rows, W = w.shape
    tr = math.gcd(math.gcd(rows, row_start), 256) if row_start else math.gcd(rows, 256)

    def fn(w_, g_, m_, v_):
        d, m2, v2 = _adamw_math(w_, g_, m_, v_)
        return g_, d, m2, v2

    return _rowwise(name, fn, [w, ('off', gfull, row_start // tr), m, v], [('rows', W, F32)] * 4, rows, tm=tr)


def _pack(arrs):
    flat = jnp.concatenate([a.reshape(-1).astype(F32) for a in arrs])
    tile = SUBLANES * LANES
    n = -(-flat.shape[0] // tile) * tile
    return jnp.pad(flat, (0, n - flat.shape[0])).reshape(-1, LANES)


def _unpack(buf, shapes):
    flat = buf.reshape(-1)
    out, pos = [], 0
    for shp in shapes:
        n = math.prod(shp)
        out.append(flat[pos:pos + n].reshape(shp))
        pos += n
    return out


BIG = ['fox_w_qkv', 'fox_w_o', 'rel_w_qkv', 'rel_w_o', 'conv_w_pw1', 'conv_w_pw2', 'ffn_w_gate', 'ffn_w_up', 'ffn_w_down']
SMALL_SHARDED = ['fox_w_f', 'conv_b_pw1', 'conv_w_dw', 'conv_b_dw', 'conv_ln_g', 'conv_ln_b', 'conv_b_pw2']
SMALL_SHARD_AXIS = {'fox_w_f': 1, 'conv_b_pw1': 1, 'conv_w_dw': 2, 'conv_b_dw': 1, 'conv_ln_g': 1, 'conv_ln_b': 1, 'conv_b_pw2': 1}
SMALL_REPL = ['fox_b_f', 'rel_bias', 'ln_mix_g', 'ln_mix_b', 'ln_ffn_g', 'ln_ffn_b']
SMALL = SMALL_SHARDED + SMALL_REPL
WEIGHTS = ['fox_w_qkv', 'fox_w_f', 'fox_b_f', 'fox_w_o', 'rel_w_qkv', 'rel_bias', 'rel_w_o', 'conv_w_pw1', 'conv_b_pw1',
           'conv_w_dw', 'conv_b_dw', 'conv_ln_g', 'conv_ln_b', 'conv_w_pw2', 'conv_b_pw2', 'ffn_w_gate', 'ffn_w_up',
           'ffn_w_down', 'ln_mix_g', 'ln_mix_b', 'ln_ffn_g', 'ln_ffn_b']


def kernel(x, fox_w_qkv, fox_w_f, fox_b_f, fox_w_o, rel_w_qkv, rel_bias, rel_w_o, conv_w_pw1, conv_b_pw1, conv_w_dw, conv_b_dw, conv_ln_g, conv_ln_b, conv_w_pw2, conv_b_pw2, ffn_w_gate, ffn_w_up, ffn_w_down, ln_mix_g, ln_mix_b, ln_ffn_g, ln_ffn_b, loss_target, m_fox_w_qkv, m_fox_w_f, m_fox_b_f, m_fox_w_o, m_rel_w_qkv, m_rel_bias, m_rel_w_o, m_conv_w_pw1, m_conv_b_pw1, m_conv_w_dw, m_conv_b_dw, m_conv_ln_g, m_conv_ln_b, m_conv_w_pw2, m_conv_b_pw2, m_ffn_w_gate, m_ffn_w_up, m_ffn_w_down, m_ln_mix_g, m_ln_mix_b, m_ln_ffn_g, m_ln_ffn_b, v_fox_w_qkv, v_fox_w_f, v_fox_b_f, v_fox_w_o, v_rel_w_qkv, v_rel_bias, v_rel_w_o, v_conv_w_pw1, v_conv_b_pw1, v_conv_w_dw, v_conv_b_dw, v_conv_ln_g, v_conv_ln_b, v_conv_w_pw2, v_conv_b_pw2, v_ffn_w_gate, v_ffn_w_up, v_ffn_w_down, v_ln_mix_g, v_ln_mix_b, v_ln_ffn_g, v_ln_ffn_b):
    A = dict(locals())
    Wt = {n: A[n] for n in WEIGHTS}
    Mo = {n: A['m_' + n] for n in WEIGHTS}
    Vo = {n: A['v_' + n] for n in WEIGHTS}

    _, S, D = x.shape
    H = D // HEAD_DIM
    Ds = D // N_CHIPS
    Nq = fox_w_qkv.shape[2]
    Np = conv_w_pw1.shape[2]
    Fs = ffn_w_gate.shape[2]
    my_x, my_y, my_c = lax.axis_index("x"), lax.axis_index("y"), lax.axis_index("c")
    my_chip = 2 * my_x + my_y
    place = jnp.stack([my_c, my_chip]).astype(jnp.int32)

    wo_base = DEPTH * Fs
    where = {
        'fox_w_qkv': ('qkv', 0), 'rel_w_qkv': ('qkv', N_FOX * D),
        'ffn_w_gate': ('ffn', 0), 'ffn_w_up': ('ffn', DEPTH * D),
        'conv_w_pw1': ('pw1', 0),
        'ffn_w_down': ('dm', 0), 'fox_w_o': ('dm', wo_base), 'rel_w_o': ('dm', wo_base + N_FOX * Ds),
        'conv_w_pw2': ('dm', wo_base + (N_FOX + 1) * Ds),
    }
    members = {'qkv': ['fox_w_qkv', 'rel_w_qkv'], 'ffn': ['ffn_w_gate', 'ffn_w_up'], 'pw1': ['conv_w_pw1'],
               'dm': ['ffn_w_down', 'fox_w_o', 'rel_w_o', 'conv_w_pw2']}
    flat2 = lambda a: a.reshape(-1, a.shape[-1])
    own = {g: jnp.concatenate([flat2(Wt[n]).astype(BF16) for n in ms], axis=0) for g, ms in members.items()}

    def layer_pieces(i):
        kind, j = i % 3, i // 3
        slot = j if kind == 0 else (N_FOX if kind == 1 else N_FOX + 1)
        w_in = (GROUPS.index('pw1'), 0, D) if kind == 2 else (GROUPS.index('qkv'), slot * D, D)
        return [w_in, (GROUPS.index('dm'), wo_base + slot * Ds, Ds), (GROUPS.index('ffn'), i * D, D),
                (GROUPS.index('ffn'), (DEPTH + i) * D, D), (GROUPS.index('dm'), i * Fs, Fs)]

    layers = [layer_pieces(i) for i in range(DEPTH)]
    gather_ssems, gather_rsems, wg_list = _gather_start([_place_own("place_" + g, own[g], place) for g in GROUPS], layers)
    WG = dict(zip(GROUPS, wg_list))
    DG = {g: lax.empty(WG[g].shape, BF16) for g in own}

    small_shapes = [Wt[n].shape for n in SMALL_SHARDED]
    slabs = _broadcast_small("gather_small", _pack([Wt[n] for n in SMALL_SHARDED]))
    per_chip = [_unpack(slabs[2 * j], small_shapes) for j in range(N_CHIPS)]
    full = {n: jnp.concatenate([per_chip[j][i] for j in range(N_CHIPS)], axis=SMALL_SHARD_AXIS[n])
            for i, n in enumerate(SMALL_SHARDED)}
    row = lambda v: v.reshape(1, -1)

    SG = {}

    def ffn_fwd(i, xb):
        hg = _mm_col(f"ffn{i}_gate", xb, WG['ffn'], i * D)
        hu = _mm_col(f"ffn{i}_up", xb, WG['ffn'], (DEPTH + i) * D)
        act, = _rowwise(f"ffn{i}_act", lambda g_, u_: [g_.astype(F32) * _sigmoid(g_.astype(F32)) * u_.astype(F32)],
                        [hg, hu], [('rows', N_CHIPS * Fs, BF16)], S)
        f = _mm_row(f"ffn{i}_down", act, WG['dm'], i * Fs, Fs)
        return f, (hg, hu, act)

    def ffn_bwd(i, xb, saved, dzb):
        hg, hu, act = saved
        DG['dm'] = _mm_dw(f"ffn{i}_dw_down", act, dzb, DG['dm'], i * Fs, 'row')
        dact = _mm_row_t(f"ffn{i}_dact", dzb, WG['dm'], i * Fs, Fs, F32)

        def fn(da_, g_, u_):
            g32, u32 = g_.astype(F32), u_.astype(F32)
            sg = _sigmoid(g32)
            silu = g32 * sg
            return da_ * u32 * (sg * (1.0 + g32 * (1.0 - sg))), da_ * silu

        dhg, dhu = _rowwise(f"ffn{i}_dact_split", fn, [dact, hg, hu],
                            [('rows', N_CHIPS * Fs, BF16), ('rows', N_CHIPS * Fs, BF16)], S)
        DG['ffn'] = _mm_dw(f"ffn{i}_dw_gate", xb, dhg, DG['ffn'], i * D, 'col')
        DG['ffn'] = _mm_dw(f"ffn{i}_dw_up", xb, dhu, DG['ffn'], (DEPTH + i) * D, 'col')
        return _mm_col_t(f"ffn{i}_dx", [(dhg, i * D), (dhu, (DEPTH + i) * D)], WG['ffn'], D)

    def fox_fwd(j, xb):
        qkv = _mm_col(f"fox{j}_qkv", xb, WG['qkv'], j * D)
        wf = full['fox_w_f'][j].astype(BF16)
        def gate_fn(x_, w_, b_):
            z_ = jnp.dot(x_, w_, preferred_element_type=F32) + b_
            return z_, jnp.minimum(z_, 0.0) - jnp.log(1.0 + jnp.exp(-jnp.abs(z_)))

        z, logf = _rowwise(f"fox{j}_gate", gate_fn, [xb, ('full', wf), ('full', row(fox_b_f[j]))],
                           [('rows', H, F32), ('rows', H, F32)], S)
        c = _cumsum_rows(f"fox{j}_cumsum", logf, False)
        crow = c.T.reshape(H, 1, S)
        o = _fox_fwd(qkv, c, crow, H)
        m = _mm_row(f"fox{j}_wo", o, WG['dm'], wo_base + j * Ds, Ds)
        return m, (qkv, z, c, crow, o, wf)

    def fox_bwd(j, xb, saved, dzb):
        qkv, z, c, crow, o, wf = saved
        DG['dm'] = _mm_dw(f"fox{j}_dw_o", o, dzb, DG['dm'], wo_base + j * Ds, 'row')
        do = _mm_row_t(f"fox{j}_do", dzb, WG['dm'], wo_base + j * Ds, Ds, BF16)
        dq, dk, dv, dcrow = _fox_bwd(qkv, c, crow, do, H)
        dqkv = jnp.concatenate([dq, dk, dv], axis=1)
        dlogf = _cumsum_rows(f"fox{j}_rcumsum", dcrow.reshape(H, S).T, True)

        def fn(x_, dl_, z_, w_):
            dz_ = dl_ * _sigmoid(-z_)
            dzb_ = dz_.astype(BF16)
            return _dot_nt(dzb_, w_), _dot_tn(x_, dzb_), jnp.sum(dz_, axis=0, keepdims=True)

        dh_f, dwf, dbf = _rowwise(f"fox{j}_gate_bwd", fn, [xb, dlogf, z, ('full', wf)],
                                  [('rows', D, F32), ('acc', (D, H), F32), ('acc', (1, H), F32)], S)
        SG.setdefault('fox_w_f', [None] * N_FOX)[j] = dwf
        SG.setdefault('fox_b_f', [None] * N_FOX)[j] = dbf.reshape(H)
        DG['qkv'] = _mm_dw(f"fox{j}_dw_qkv", xb, dqkv, DG['qkv'], j * D, 'col')
        dh = _mm_col_t(f"fox{j}_dx", [(dqkv, j * D)], WG['qkv'], D)
        return [dh, dh_f]

    def rel_fwd(xb):
        qkv = _mm_col("rel_qkv", xb, WG['qkv'], N_FOX * D)
        rb_pad = jnp.pad(rel_bias[0], ((0, 0), (0, REL_TABLE_PAD - REL_TABLE)))
        bias = jnp.transpose(_rel_expand(rb_pad), (1, 0, 2))
        o = _rel_fwd(qkv, bias, H)
        m = _mm_row("rel_wo", o, WG['dm'], wo_base + N_FOX * Ds, Ds)
        return m, (qkv, bias, o)

    def rel_bwd(xb, saved, dzb):
        qkv, bias, o = saved
        DG['dm'] = _mm_dw("rel_dw_o", o, dzb, DG['dm'], wo_base + N_FOX * Ds, 'row')
        do = _mm_row_t("rel_do", dzb, WG['dm'], wo_base + N_FOX * Ds, Ds, BF16)
        dq, dk, dv, dbias = _rel_bwd(qkv, bias, do, H)
        SG['rel_bias'] = _rel_reduce(jnp.transpose(dbias, (1, 0, 2)))[:, :REL_TABLE].reshape(1, H, REL_TABLE)
        dqkv = jnp.concatenate([dq, dk, dv], axis=1)
        DG['qkv'] = _mm_dw("rel_dw_qkv", xb, dqkv, DG['qkv'], N_FOX * D, 'col')
        return [_mm_col_t("rel_dx", [(dqkv, N_FOX * D)], WG['qkv'], D)]

    w_dw32 = jnp.pad(full['conv_w_dw'][0], ((0, CONV_HALO - CONV_K), (0, 0)))
    cg, cb = full['conv_ln_g'], full['conv_ln_b']

    def conv_fwd(xb):
        u = _mm_col("conv_pw1", xb, WG['pw1'], 0, bias=full['conv_b_pw1'], out_dtype=F32)
        u2, = _rowwise("conv_glu", lambda a_, g_: [a_ * _sigmoid(g_)],
                       [('cols', u, D, 0), ('cols', u, D, 1)], [('rows', D, F32)], S)
        yc = _dwconv("conv_dw", u2, w_dw32, full['conv_b_dw'], False)

        def fn(y_, g_, b_):
            xhat, rstd = _ln_stats(y_)
            ln = xhat * g_ + b_
            return ln * _sigmoid(ln), xhat, rstd

        zc, xhat, rstd = _rowwise("conv_ln_silu", fn, [yc, ('full', cg), ('full', cb)],
                                  [('rows', D, BF16), ('rows', D, F32), ('rows', 1, F32)], S)
        m = _mm_row("conv_pw2", zc, WG['dm'], wo_base + (N_FOX + 1) * Ds, Ds, bias=full['conv_b_pw2'])
        return m, (u, u2, zc, xhat, rstd)

    def conv_bwd(xb, saved, dz, dzb):
        u, u2, zc, xhat, rstd = saved
        r0 = wo_base + (N_FOX + 1) * Ds
        DG['dm'] = _mm_dw("conv_dw_pw2", zc, dzb, DG['dm'], r0, 'row')
        dzc = _mm_row_t("conv_dzc", dzb, WG['dm'], r0, Ds, F32)

        def fn(dm_, dzc_, xh_, rs_, g_, b_):
            ln = xh_ * g_ + b_
            sg = _sigmoid(ln)
            dln = dzc_ * (sg * (1.0 + ln * (1.0 - sg)))
            dyc = _ln_bwd_core(dln, xh_, rs_, g_)
            col = lambda t: jnp.sum(t, axis=0, keepdims=True)
            return dyc, col(dm_), col(dln * xh_), col(dln), col(dyc)

        dyc, SG['conv_b_pw2'], SG['conv_ln_g'], SG['conv_ln_b'], SG['conv_b_dw'] = _rowwise(
            "conv_ln_silu_bwd", fn, [dz, dzc, xhat, rstd, ('full', cg), ('full', cb)],
            [('rows', D, F32)] + [('acc', (1, D), F32)] * 4, S)
        du2 = _dwconv("conv_dw_bwd_x", dyc, w_dw32, jnp.zeros((1, D), F32), True)
        SG['conv_w_dw'] = _dwconv_dw(u2, dyc)[:CONV_K].reshape(1, CONV_K, D)

        def fn2(du2_, a_, g_):
            sg = _sigmoid(g_)
            da, dgt = du2_ * sg, du2_ * a_ * sg * (1.0 - sg)
            return da, dgt, jnp.sum(da, axis=0, keepdims=True), jnp.sum(dgt, axis=0, keepdims=True)

        da, dgt, dba, dbg = _rowwise("conv_glu_bwd", fn2, [du2, ('cols', u, D, 0), ('cols', u, D, 1)],
                                     [('rows', D, BF16), ('rows', D, BF16), ('acc', (1, D), F32), ('acc', (1, D), F32)], S)
        SG['conv_b_pw1'] = jnp.concatenate([dba, dbg], axis=1)
        du = jnp.concatenate([da, dgt], axis=1)
        DG['pw1'] = _mm_dw("conv_dw_pw1", xb, du, DG['pw1'], 0, 'col')
        return [_mm_col_t("conv_dx", [(du, 0)], WG['pw1'], D)]

    xs = x[0]
    xs_b = xs.astype(BF16)
    tape = []
    for i in range(DEPTH):
        kind, j = i % 3, i // 3
        wg_list = _gather_wait(f"gather_wait{i}", wg_list, gather_ssems[i], gather_rsems[i], layers[i], xs)
        wg_list = _gather_forward(f"gather_fwd{i}", wg_list, layers[i])
        WG.update(zip(GROUPS, wg_list))
        if kind == 0:
            m, msaved = fox_fwd(j, xs_b)
        elif kind == 1:
            m, msaved = rel_fwd(xs_b)
        else:
            m, msaved = conv_fwd(xs_b)
        xm, xm_b, xhat1, rstd1 = _ln_fwd(f"ln_mix{i}", xs, m, row(ln_mix_g[i]), row(ln_mix_b[i]))
        f, fsaved = ffn_fwd(i, xm_b)
        xo, xo_b, xhat2, rstd2 = _ln_fwd(f"ln_ffn{i}", xm, f, row(ln_ffn_g[i]), row(ln_ffn_b[i]))
        tape.append((xs_b, msaved, xhat1, rstd1, xm_b, fsaved, xhat2, rstd2))
        xs, xs_b = xo, xo_b

    def loss_fn(y_, t_):
        e = y_ - t_
        return e * (1.0 / D), jnp.sum(e * e, axis=0, keepdims=True)

    dy, sq = _rowwise("loss", loss_fn, [xs, loss_target[0]], [('rows', D, F32), ('acc', (1, D), F32)], S)
    loss = lax.psum(jnp.sum(sq) * (0.5 / D), ("x", "y", "c"))

    terms = [(dy, 1.0)]
    g_mix, b_mix, g_ffn, b_ffn = [None] * DEPTH, [None] * DEPTH, [None] * DEPTH, [None] * DEPTH
    for i in reversed(range(DEPTH)):
        kind, j = i % 3, i // 3
        xin_b, msaved, xhat1, rstd1, xm_b, fsaved, xhat2, rstd2 = tape[i]
        dz2, dz2b, g_ffn[i], b_ffn[i] = _ln_bwd(f"ln_ffn{i}_bwd", terms, xhat2, rstd2, row(ln_ffn_g[i]))
        dx_ffn = ffn_bwd(i, xm_b, fsaved, dz2b)
        dz1, dz1b, g_mix[i], b_mix[i] = _ln_bwd(f"ln_mix{i}_bwd", [(dz2, ALPHA), (dx_ffn, 1.0)], xhat1, rstd1, row(ln_mix_g[i]))
        if kind == 0:
            mix_terms = fox_bwd(j, xin_b, msaved, dz1b)
        elif kind == 1:
            mix_terms = rel_bwd(xin_b, msaved, dz1b)
        else:
            mix_terms = conv_bwd(xin_b, msaved, dz1, dz1b)
        terms = [(dz1, ALPHA)] + [(t, 1.0) for t in mix_terms]

    def gx_fn(*v):
        acc = v[0] * ALPHA
        for t in v[1:]:
            acc = acc + t
        return [acc]

    grad_x, = _rowwise("grad_x", gx_fn, [a for a, _ in terms], [('rows', D, F32)], S)
    grad_x = grad_x.reshape(1, S, D)

    SG['fox_w_f'] = jnp.stack(SG['fox_w_f'])
    SG['fox_b_f'] = jnp.stack(SG['fox_b_f'])
    SG['ln_mix_g'] = jnp.concatenate(g_mix, axis=0)
    SG['ln_mix_b'] = jnp.concatenate(b_mix, axis=0)
    SG['ln_ffn_g'] = jnp.concatenate(g_ffn, axis=0)
    SG['ln_ffn_b'] = jnp.concatenate(b_ffn, axis=0)

    grads, deltas, new_m, new_v = {}, {}, {}, {}

    GF = {}
    for g in own:
        land = _swap_halves("pair_swap_" + g, DG[g])
        pb = _pair_sum("pair_sum_" + g, DG[g], land, place)
        land2 = _scatter_chips("chip_scatter_" + g, pb)
        GF[g] = _share_halves("pair_share_" + g, _chip_sum("chip_sum_" + g, pb, land2, place))

    for n in BIG:
        g, r0 = where[n]
        outs = _adamw("adamw_" + n, flat2(Wt[n]), GF[g], r0, flat2(Mo[n]), flat2(Vo[n]))
        grads[n], deltas[n], new_m[n], new_v[n] = [o.reshape(Wt[n].shape) for o in outs]

    full_shapes = [SG[n].shape for n in SMALL]
    summed = _sum_slabs("small_sum", _broadcast_small("small_exchange", _pack([SG[n] for n in SMALL])))
    gsm = dict(zip(SMALL, _unpack(summed, full_shapes)))
    for n in SMALL_SHARDED:
        ax = SMALL_SHARD_AXIS[n]
        width = Wt[n].shape[ax]
        gsm[n] = lax.dynamic_slice_in_dim(gsm[n], my_chip * width, width, axis=ax)
    own_shapes = [Wt[n].shape for n in SMALL]
    packed = [_pack([src[n] for n in SMALL]) for src in (Wt, gsm, Mo, Vo)]
    rows_small = packed[0].shape[0]

    def small_fn(w_, g_, m_, v_):
        return _adamw_math(w_, g_, m_, v_)

    sd, sm, sv = _rowwise("adamw_small", small_fn, packed, [('rows', LANES, F32)] * 3, rows_small, tm=rows_small)
    for n, d_, m_, v_ in zip(SMALL, _unpack(sd, own_shapes), _unpack(sm, own_shapes), _unpack(sv, own_shapes)):
        grads[n], deltas[n], new_m[n], new_v[n] = gsm[n], d_, m_, v_

    return (loss, grad_x, *[grads[n] for n in WEIGHTS], *[deltas[n] for n in WEIGHTS],
            *[new_m[n] for n in WEIGHTS], *[new_v[n] for n in WEIGHTS])
```

```python
import functools
import math

import jax
import jax.numpy as jnp
from jax import lax
from jax.experimental import pallas as pl
from jax.experimental.pallas import tpu as pltpu

F32 = jnp.float32
BF16 = jnp.bfloat16
MESH_IDS = pl.DeviceIdType.MESH
HIGHEST = lax.Precision.HIGHEST

N_CHIPS = 4
DEPTH = 4
N_FOX = 2
HEAD_DIM = 128
CHUNK = 64
LEFT_CHUNKS = 8
BAND_KEYS = (LEFT_CHUNKS + 1) * CHUNK
PAD_KEYS = LEFT_CHUNKS * CHUNK
REL_CLIP = 128
REL_TABLE = 2 * REL_CLIP + 1
REL_TABLE_PAD = 384
CONV_K = 31
CONV_HALO = 32
ALPHA = (2.0 * DEPTH) ** 0.25
LN_EPS = 1e-5
ADAM_LR, ADAM_B1, ADAM_B2, ADAM_EPS, ADAM_WD, ADAM_STEP = 0.001, 0.9, 0.999, 1e-08, 0.01, 10
NEG_BIG = -1e30
VMEM_LIMIT_V7X = 56 * 1024 * 1024
LANES = 128
SUBLANES = 8


def _cparams(*sem):
    return pltpu.CompilerParams(dimension_semantics=sem if sem else None, vmem_limit_bytes=VMEM_LIMIT_V7X)


def _pick(dim, target):
    best = None
    for t in range(LANES, min(dim, target) + 1, LANES):
        if dim % t == 0:
            best = t
    return best if best is not None else dim


def _dot_nt(a, b):
    return lax.dot_general(a, b, (((1,), (1,)), ((), ())), preferred_element_type=F32)


def _dot_tn(a, b):
    return lax.dot_general(a, b, (((0,), (0,)), ((), ())), preferred_element_type=F32)


def _sigmoid(z):
    return 1.0 / (1.0 + jnp.exp(-z))


def _rowwise(name, fn, ins, outs, S, tm=256):
    tm = min(tm, S)
    arrs, in_specs = [], []
    for it in ins:
        if isinstance(it, tuple) and it[0] == 'full':
            a = it[1]
            in_specs.append(pl.BlockSpec(a.shape, lambda i, _n=a.ndim: (0,) * _n))
        elif isinstance(it, tuple) and it[0] == 'cols':
            _, a, width, blk = it
            in_specs.append(pl.BlockSpec((tm, width), lambda i, _b=blk: (i, _b)))
        elif isinstance(it, tuple) and it[0] == 'off':
            _, a, off = it
            in_specs.append(pl.BlockSpec((tm, a.shape[1]), lambda i, _o=off: (i + _o, 0)))
        else:
            a = it
            in_specs.append(pl.BlockSpec((tm, a.shape[1]), lambda i: (i, 0)))
        arrs.append(a)
    out_shape, out_specs = [], []
    for kind, shp, dt in outs:
        if kind == 'rows':
            out_shape.append(jax.ShapeDtypeStruct((S, shp), dt))
            out_specs.append(pl.BlockSpec((tm, shp), lambda i: (i, 0)))
        else:
            out_shape.append(jax.ShapeDtypeStruct(shp, dt))
            out_specs.append(pl.BlockSpec(shp, lambda i, _n=len(shp): (0,) * _n))
    n_in = len(arrs)

    def body(*refs):
        vals = fn(*[r[...] for r in refs[:n_in]])
        first = pl.program_id(0) == 0
        for (kind, _, _), r, v in zip(outs, refs[n_in:], vals):
            if kind == 'rows':
                r[...] = v.astype(r.dtype)
            else:
                @pl.when(first)
                def _(r=r, v=v):
                    r[...] = v.astype(r.dtype)

                @pl.when(jnp.logical_not(first))
                def _(r=r, v=v):
                    r[...] += v.astype(r.dtype)

    has_acc = any(k != 'rows' for k, _, _ in outs)
    res = pl.pallas_call(
        body, name=name, grid=(S // tm,), in_specs=in_specs, out_specs=out_specs, out_shape=out_shape,
        compiler_params=_cparams("arbitrary" if has_acc else "parallel"),
    )(*arrs)
    return res


def _mm_col(name, a, wg, row_start, bias=None, out_dtype=BF16):
    S, K = a.shape
    _, _, Ns = wg.shape
    rb = row_start // K
    tm = min(512, S)

    def body(a_ref, w_ref, *rest):
        acc = jnp.dot(a_ref[...].astype(BF16), w_ref[...], preferred_element_type=F32)
        if bias is not None:
            acc = acc + rest[0][...]
        rest[-1][...] = acc.astype(out_dtype)

    in_specs = [pl.BlockSpec((tm, K), lambda j, m: (m, 0)), pl.BlockSpec((None, K, Ns), lambda j, m: (j, rb, 0))]
    args = [a, wg]
    if bias is not None:
        in_specs.append(pl.BlockSpec((1, Ns), lambda j, m: (0, j)))
        args.append(bias)
    return pl.pallas_call(
        body, name=name, grid=(N_CHIPS, S // tm), in_specs=in_specs,
        out_specs=pl.BlockSpec((tm, Ns), lambda j, m: (m, j)),
        out_shape=jax.ShapeDtypeStruct((S, N_CHIPS * Ns), out_dtype),
        compiler_params=_cparams("parallel", "parallel"),
    )(*args)


def _mm_row(name, a, wg, row_start, Ks, bias=None):
    S = a.shape[0]
    N = wg.shape[2]
    rb = row_start // Ks
    tm = min(512, S)

    def body(a_ref, w_ref, *rest):
        o_ref = rest[-1]
        j = pl.program_id(1)
        d = jnp.dot(a_ref[...].astype(BF16), w_ref[...], preferred_element_type=F32)

        @pl.when(j == 0)
        def _():
            o_ref[...] = d + rest[0][...] if bias is not None else d

        @pl.when(j > 0)
        def _():
            o_ref[...] += d

    in_specs = [pl.BlockSpec((tm, Ks), lambda m, j: (m, j)), pl.BlockSpec((None, Ks, N), lambda m, j: (j, rb, 0))]
    args = [a, wg]
    if bias is not None:
        in_specs.append(pl.BlockSpec((1, N), lambda m, j: (0, 0)))
        args.append(bias)
    return pl.pallas_call(
        body, name=name, grid=(S // tm, N_CHIPS), in_specs=in_specs,
        out_specs=pl.BlockSpec((tm, N), lambda m, j: (m, 0)),
        out_shape=jax.ShapeDtypeStruct((S, N), F32),
        compiler_params=_cparams("parallel", "arbitrary"),
    )(*args)


def _mm_col_t(name, pairs, wg, K):
    S = pairs[0][0].shape[0]
    Ns = wg.shape[2]
    tm = min(512, S)
    n = len(pairs)

    def body(*refs):
        o_ref = refs[-1]
        j = pl.program_id(1)
        d = _dot_nt(refs[0][...], refs[n][...])
        for p in range(1, n):
            d = d + _dot_nt(refs[p][...], refs[n + p][...])

        @pl.when(j == 0)
        def _():
            o_ref[...] = d

        @pl.when(j > 0)
        def _():
            o_ref[...] += d

    in_specs = [pl.BlockSpec((tm, Ns), lambda m, j: (m, j)) for _ in pairs]
    in_specs += [pl.BlockSpec((None, K, Ns), lambda m, j, _rb=rs // K: (j, _rb, 0)) for _, rs in pairs]
    return pl.pallas_call(
        body, name=name, grid=(S // tm, N_CHIPS), in_specs=in_specs,
        out_specs=pl.BlockSpec((tm, K), lambda m, j: (m, 0)),
        out_shape=jax.ShapeDtypeStruct((S, K), F32),
        compiler_params=_cparams("parallel", "arbitrary"),
    )(*[dy for dy, _ in pairs], *[wg for _ in pairs])


def _mm_row_t(name, dy, wg, row_start, Ks, out_dtype):
    S, N = dy.shape
    rb = row_start // Ks
    tm = min(512, S)

    def body(dy_ref, w_ref, o_ref):
        o_ref[...] = _dot_nt(dy_ref[...], w_ref[...]).astype(out_dtype)

    return pl.pallas_call(
        body, name=name, grid=(N_CHIPS, S // tm),
        in_specs=[pl.BlockSpec((tm, N), lambda j, m: (m, 0)), pl.BlockSpec((None, Ks, N), lambda j, m: (j, rb, 0))],
        out_specs=pl.BlockSpec((tm, Ks), lambda j, m: (m, j)),
        out_shape=jax.ShapeDtypeStruct((S, N_CHIPS * Ks), out_dtype),
        compiler_params=_cparams("parallel", "parallel"),
    )(dy, wg)


def _mm_dw(name, a, dy, dg, row_start, kind):
    S = a.shape[0]
    _, _, W = dg.shape
    if kind == 'col':
        K = a.shape[1]
        rows = K
        tk, tn = _pick(K, 512), W
        a_map = lambda j, nb, kb: (0, kb)
        dy_map = lambda j, nb, kb: (0, j * (W // tn) + nb)
    else:
        rows = a.shape[1] // N_CHIPS
        tk = rows if rows * S * 2 * 2 <= 12 * 1024 * 1024 else _pick(rows, 512)
        tn = _pick(W, 1024)
        a_map = lambda j, nb, kb: (0, j * (rows // tk) + kb)
        dy_map = lambda j, nb, kb: (0, nb)
    rb = row_start // tk
    assert row_start % tk == 0

    def body(a_ref, dy_ref, dg_in, o_ref):
        del dg_in
        o_ref[...] = _dot_tn(a_ref[...], dy_ref[...]).astype(o_ref.dtype)

    return pl.pallas_call(
        body, name=name, grid=(N_CHIPS, W // tn, rows // tk),
        in_specs=[pl.BlockSpec((S, tk), a_map), pl.BlockSpec((S, tn), dy_map), pl.BlockSpec(memory_space=pl.ANY)],
        out_specs=pl.BlockSpec((None, tk, tn), lambda j, nb, kb: (j, rb + kb, nb)),
        out_shape=jax.ShapeDtypeStruct(dg.shape, dg.dtype),
        input_output_aliases={2: 0},
        compiler_params=_cparams("parallel", "parallel", "parallel"),
    )(a, dy, dg)


def _fox_probs(q, k, c_blk, crow, h, qi, tq):
    n = k.shape[0]
    s = _dot_nt(q, k) * (HEAD_DIM ** -0.5)
    lane = lax.broadcasted_iota(jnp.int32, c_blk.shape, 1)
    ccol = jnp.sum(jnp.where(lane == h, c_blk, 0.0), axis=1, keepdims=True)
    s = s + (ccol - crow)
    t_idx = qi * tq + lax.broadcasted_iota(jnp.int32, (tq, n), 0)
    s_idx = lax.broadcasted_iota(jnp.int32, (tq, n), 1)
    s = jnp.where(s_idx <= t_idx, s, NEG_BIG)
    p = jnp.exp(s - jnp.max(s, axis=1, keepdims=True))
    return p / jnp.sum(p, axis=1, keepdims=True)


def _per_query_block(qi, nq, tq, fn):
    for qv in range(nq):
        @pl.when(qi == qv)
        def _(qv=qv):
            fn(qv, (qv + 1) * tq)


def _fox_fwd(qkv, c, crow, H):
    S = qkv.shape[0]
    tq = min(256, S)

    def body(q_ref, k_ref, v_ref, c_ref, crow_ref, o_ref):
        def block(qv, n):
            p = _fox_probs(q_ref[...], k_ref[0:n, :], c_ref[...], crow_ref[:, 0:n], pl.program_id(0), qv, tq)
            o_ref[...] = jnp.dot(p.astype(BF16), v_ref[0:n, :], preferred_element_type=F32).astype(o_ref.dtype)

        _per_query_block(pl.program_id(1), S // tq, tq, block)

    return pl.pallas_call(
        body, name="fox_attn_fwd", grid=(H, S // tq),
        in_specs=[pl.BlockSpec((tq, HEAD_DIM), lambda h, i: (i, h)),
                  pl.BlockSpec((S, HEAD_DIM), lambda h, i: (0, H + h)),
                  pl.BlockSpec((S, HEAD_DIM), lambda h, i: (0, 2 * H + h)),
                  pl.BlockSpec((tq, H), lambda h, i: (i, 0)),
                  pl.BlockSpec((None, 1, S), lambda h, i: (h, 0, 0))],
        out_specs=pl.BlockSpec((tq, HEAD_DIM), lambda h, i: (i, h)),
        out_shape=jax.ShapeDtypeStruct((S, H * HEAD_DIM), BF16),
        compiler_params=_cparams("parallel", "parallel"),
    )(qkv, qkv, qkv, c, crow)


def _fox_bwd(qkv, c, crow, do, H):
    S = qkv.shape[0]
    tq = min(256, S)
    nq = S // tq

    def body(q_ref, k_ref, v_ref, c_ref, crow_ref, do_ref, dq_ref, dk_ref, dv_ref, dc_ref, dk_acc, dv_acc):
        qi = pl.program_id(1)

        @pl.when(qi == 0)
        def _():
            dk_acc[...] = jnp.zeros_like(dk_acc)
            dv_acc[...] = jnp.zeros_like(dv_acc)
            dc_ref[...] = jnp.zeros_like(dc_ref)

        def block(qv, n):
            q, k, v, do_ = q_ref[...], k_ref[0:n, :], v_ref[0:n, :], do_ref[...]
            p = _fox_probs(q, k, c_ref[...], crow_ref[:, 0:n], pl.program_id(0), qv, tq)
            dv_acc[0:n, :] += _dot_tn(p.astype(BF16), do_)
            dp = _dot_nt(do_, v)
            ds = p * (dp - jnp.sum(p * dp, axis=1, keepdims=True))
            dsb = (ds * (HEAD_DIM ** -0.5)).astype(BF16)
            dq_ref[...] = jnp.dot(dsb, k, preferred_element_type=F32).astype(dq_ref.dtype)
            dk_acc[0:n, :] += _dot_tn(dsb, q)
            dc_ref[:, 0:n] += -jnp.sum(ds, axis=0, keepdims=True)

        _per_query_block(qi, nq, tq, block)

        @pl.when(qi == nq - 1)
        def _():
            dk_ref[...] = dk_acc[...].astype(dk_ref.dtype)
            dv_ref[...] = dv_acc[...].astype(dv_ref.dtype)

    D = H * HEAD_DIM
    return pl.pallas_call(
        body, name="fox_attn_bwd", grid=(H, nq),
        in_specs=[pl.BlockSpec((tq, HEAD_DIM), lambda h, i: (i, h)),
                  pl.BlockSpec((S, HEAD_DIM), lambda h, i: (0, H + h)),
                  pl.BlockSpec((S, HEAD_DIM), lambda h, i: (0, 2 * H + h)),
                  pl.BlockSpec((tq, H), lambda h, i: (i, 0)),
                  pl.BlockSpec((None, 1, S), lambda h, i: (h, 0, 0)),
                  pl.BlockSpec((tq, HEAD_DIM), lambda h, i: (i, h))],
        out_specs=[pl.BlockSpec((tq, HEAD_DIM), lambda h, i: (i, h)),
                   pl.BlockSpec((S, HEAD_DIM), lambda h, i: (0, h)),
                   pl.BlockSpec((S, HEAD_DIM), lambda h, i: (0, h)),
                   pl.BlockSpec((None, 1, S), lambda h, i: (h, 0, 0))],
        out_shape=[jax.ShapeDtypeStruct((S, D), BF16), jax.ShapeDtypeStruct((S, D), BF16),
                   jax.ShapeDtypeStruct((S, D), BF16), jax.ShapeDtypeStruct((H, 1, S), F32)],
        scratch_shapes=[pltpu.VMEM((S, HEAD_DIM), F32), pltpu.VMEM((S, HEAD_DIM), F32)],
        compiler_params=_cparams("parallel", "arbitrary"),
    )(qkv, qkv, qkv, c, crow, do)


def _cumsum_rows(name, xin, reverse):
    S, H = xin.shape
    tb = min(256, S)
    nb = S // tb

    def body(x_ref, o_ref):
        r = lax.broadcasted_iota(jnp.int32, (tb, tb), 0)
        cidx = lax.broadcasted_iota(jnp.int32, (tb, tb), 1)
        tri = (r <= cidx if reverse else r >= cidx).astype(F32)

        def step(b, carry):
            bb = nb - 1 - b if reverse else b
            rows = pl.ds(pl.multiple_of(bb * tb, tb), tb)
            blk = x_ref[rows, :]
            o_ref[rows, :] = jnp.dot(tri, blk, precision=HIGHEST, preferred_element_type=F32) + carry
            return carry + jnp.sum(blk, axis=0, keepdims=True)

        lax.fori_loop(0, nb, step, jnp.zeros((1, H), F32))

    return pl.pallas_call(
        body, name=name, out_shape=jax.ShapeDtypeStruct((S, H), F32),
        in_specs=[pl.BlockSpec(memory_space=pltpu.VMEM)], out_specs=pl.BlockSpec(memory_space=pltpu.VMEM),
        compiler_params=_cparams(),
    )(xin)


def _rel_onehot(i, transposed):
    shp = (REL_TABLE_PAD, BAND_KEYS) if transposed else (BAND_KEYS, REL_TABLE_PAD)
    j = lax.broadcasted_iota(jnp.int32, shp, 1 if transposed else 0)
    r = lax.broadcasted_iota(jnp.int32, shp, 0 if transposed else 1)
    return (jnp.clip(PAD_KEYS + i - j, -REL_CLIP, REL_CLIP) + REL_CLIP == r).astype(F32)


def _rel_expand(rb_pad):
    H = rb_pad.shape[0]

    def body(rb_ref, o_ref):
        def step(i, _):
            o_ref[i] = jnp.dot(rb_ref[...], _rel_onehot(i, True), precision=HIGHEST, preferred_element_type=F32)
            return 0
        lax.fori_loop(0, CHUNK, step, 0)

    return pl.pallas_call(
        body, name="rel_bias_expand", out_shape=jax.ShapeDtypeStruct((CHUNK, H, BAND_KEYS), F32),
        in_specs=[pl.BlockSpec(memory_space=pltpu.VMEM)], out_specs=pl.BlockSpec(memory_space=pltpu.VMEM),
        compiler_params=_cparams(),
    )(rb_pad)


def _rel_reduce(dbt):
    H = dbt.shape[1]

    def body(d_ref, o_ref):
        def step(i, acc):
            return acc + jnp.dot(d_ref[i], _rel_onehot(i, False), precision=HIGHEST, preferred_element_type=F32)
        o_ref[...] = lax.fori_loop(0, CHUNK, step, jnp.zeros((H, REL_TABLE_PAD), F32))

    return pl.pallas_call(
        body, name="rel_bias_reduce", out_shape=jax.ShapeDtypeStruct((H, REL_TABLE_PAD), F32),
        in_specs=[pl.BlockSpec(memory_space=pltpu.VMEM)], out_specs=pl.BlockSpec(memory_space=pltpu.VMEM),
        compiler_params=_cparams(),
    )(dbt)


def _rel_probs(q, kb, bias, n):
    s = _dot_nt(q, kb) * (HEAD_DIM ** -0.5) + bias
    j = lax.broadcasted_iota(jnp.int32, (CHUNK, BAND_KEYS), 1)
    s = jnp.where(j >= PAD_KEYS - n * CHUNK, s, NEG_BIG)
    p = jnp.exp(s - jnp.max(s, axis=1, keepdims=True))
    return p / jnp.sum(p, axis=1, keepdims=True)


def _rel_fwd(qkv, bias, H):
    S = qkv.shape[0]

    def body(q_ref, k_ref, v_ref, b_ref, o_ref, kpad, vpad):
        kpad[0:PAD_KEYS, :] = jnp.zeros((PAD_KEYS, HEAD_DIM), BF16)
        vpad[0:PAD_KEYS, :] = jnp.zeros((PAD_KEYS, HEAD_DIM), BF16)
        kpad[PAD_KEYS:PAD_KEYS + S, :] = k_ref[...]
        vpad[PAD_KEYS:PAD_KEYS + S, :] = v_ref[...]
        bias_t = b_ref[...]

        def chunk(n, _):
            rows = pl.ds(pl.multiple_of(n * CHUNK, CHUNK), CHUNK)
            band = pl.ds(pl.multiple_of(n * CHUNK, CHUNK), BAND_KEYS)
            p = _rel_probs(q_ref[rows, :], kpad[band, :], bias_t, n)
            o_ref[rows, :] = jnp.dot(p.astype(BF16), vpad[band, :], preferred_element_type=F32).astype(o_ref.dtype)
            return 0

        lax.fori_loop(0, S // CHUNK, chunk, 0)

    return pl.pallas_call(
        body, name="rel_attn_fwd", grid=(H,),
        in_specs=[pl.BlockSpec((S, HEAD_DIM), lambda h: (0, h)),
                  pl.BlockSpec((S, HEAD_DIM), lambda h: (0, H + h)),
                  pl.BlockSpec((S, HEAD_DIM), lambda h: (0, 2 * H + h)),
                  pl.BlockSpec((None, CHUNK, BAND_KEYS), lambda h: (h, 0, 0))],
        out_specs=pl.BlockSpec((S, HEAD_DIM), lambda h: (0, h)),
        out_shape=jax.ShapeDtypeStruct((S, H * HEAD_DIM), BF16),
        scratch_shapes=[pltpu.VMEM((S + PAD_KEYS, HEAD_DIM), BF16), pltpu.VMEM((S + PAD_KEYS, HEAD_DIM), BF16)],
        compiler_params=_cparams("parallel"),
    )(qkv, qkv, qkv, bias)


def _rel_bwd(qkv, bias, do, H):
    S = qkv.shape[0]
    D = H * HEAD_DIM

    def body(q_ref, k_ref, v_ref, b_ref, do_ref, dq_ref, dk_ref, dv_ref, db_ref, kpad, vpad, dkpad, dvpad):
        kpad[0:PAD_KEYS, :] = jnp.zeros((PAD_KEYS, HEAD_DIM), BF16)
        vpad[0:PAD_KEYS, :] = jnp.zeros((PAD_KEYS, HEAD_DIM), BF16)
        kpad[PAD_KEYS:PAD_KEYS + S, :] = k_ref[...]
        vpad[PAD_KEYS:PAD_KEYS + S, :] = v_ref[...]
        dkpad[...] = jnp.zeros_like(dkpad)
        dvpad[...] = jnp.zeros_like(dvpad)
        db_ref[...] = jnp.zeros_like(db_ref)
        bias_t = b_ref[...]

        def chunk(n, _):
            rows = pl.ds(pl.multiple_of(n * CHUNK, CHUNK), CHUNK)
            band = pl.ds(pl.multiple_of(n * CHUNK, CHUNK), BAND_KEYS)
            q, kb, vb, do_ = q_ref[rows, :], kpad[band, :], vpad[band, :], do_ref[rows, :]
            p = _rel_probs(q, kb, bias_t, n)
            dvpad[band, :] += _dot_tn(p.astype(BF16), do_)
            dp = _dot_nt(do_, vb)
            ds = p * (dp - jnp.sum(p * dp, axis=1, keepdims=True))
            db_ref[...] += ds
            dsb = (ds * (HEAD_DIM ** -0.5)).astype(BF16)
            dq_ref[rows, :] = jnp.dot(dsb, kb, preferred_element_type=F32).astype(dq_ref.dtype)
            dkpad[band, :] += _dot_tn(dsb, q)
            return 0

        lax.fori_loop(0, S // CHUNK, chunk, 0)
        dk_ref[...] = dkpad[PAD_KEYS:PAD_KEYS + S, :].astype(dk_ref.dtype)
        dv_ref[...] = dvpad[PAD_KEYS:PAD_KEYS + S, :].astype(dv_ref.dtype)

    head = lambda h: (0, h)
    return pl.pallas_call(
        body, name="rel_attn_bwd", grid=(H,),
        in_specs=[pl.BlockSpec((S, HEAD_DIM), head),
                  pl.BlockSpec((S, HEAD_DIM), lambda h: (0, H + h)),
                  pl.BlockSpec((S, HEAD_DIM), lambda h: (0, 2 * H + h)),
                  pl.BlockSpec((None, CHUNK, BAND_KEYS), lambda h: (h, 0, 0)),
                  pl.BlockSpec((S, HEAD_DIM), head)],
        out_specs=[pl.BlockSpec((S, HEAD_DIM), head), pl.BlockSpec((S, HEAD_DIM), head), pl.BlockSpec((S, HEAD_DIM), head),
                   pl.BlockSpec((None, CHUNK, BAND_KEYS), lambda h: (h, 0, 0))],
        out_shape=[jax.ShapeDtypeStruct((S, D), BF16), jax.ShapeDtypeStruct((S, D), BF16), jax.ShapeDtypeStruct((S, D), BF16),
                   jax.ShapeDtypeStruct((H, CHUNK, BAND_KEYS), F32)],
        scratch_shapes=[pltpu.VMEM((S + PAD_KEYS, HEAD_DIM), BF16), pltpu.VMEM((S + PAD_KEYS, HEAD_DIM), BF16),
                        pltpu.VMEM((S + PAD_KEYS, HEAD_DIM), F32), pltpu.VMEM((S + PAD_KEYS, HEAD_DIM), F32)],
        compiler_params=_cparams("parallel"),
    )(qkv, qkv, qkv, bias, do)


def _conv_taps(win, tt, reverse):
    n = tt + 2 * CONV_HALO
    for k in range(CONV_K):
        off = (CONV_K - 1 - k) if reverse else (k - (CONV_K - 1))
        sh = (-off) % n
        rolled = pltpu.roll(win, sh, 0) if sh else win
        yield k, rolled[CONV_HALO:CONV_HALO + tt, :]


def _fill_padded(pad_ref, x_ref, S):
    tc = pad_ref.shape[1]
    pad_ref[0:CONV_HALO, :] = jnp.zeros((CONV_HALO, tc), F32)
    pad_ref[CONV_HALO + S:CONV_HALO + S + CONV_HALO, :] = jnp.zeros((CONV_HALO, tc), F32)
    pad_ref[CONV_HALO:CONV_HALO + S, :] = x_ref[...]


def _dwconv(name, xin, w32, bias, reverse):
    S, D = xin.shape
    tc = min(256, D)
    tt = min(256, S)

    def body(x_ref, w_ref, b_ref, y_ref, pad_ref):
        _fill_padded(pad_ref, x_ref, S)
        def tile(ti, _):
            t0 = pl.multiple_of(ti * tt, tt)
            win = pad_ref[pl.ds(t0, tt + 2 * CONV_HALO), :]
            acc = jnp.zeros((tt, tc), F32) + b_ref[...]
            for k, shifted in _conv_taps(win, tt, reverse):
                acc = acc + w_ref[pl.ds(k, 1), :] * shifted
            y_ref[pl.ds(t0, tt), :] = acc
            return 0

        lax.fori_loop(0, S // tt, tile, 0)

    return pl.pallas_call(
        body, name=name, grid=(D // tc,),
        in_specs=[pl.BlockSpec((S, tc), lambda i: (0, i)), pl.BlockSpec((CONV_HALO, tc), lambda i: (0, i)),
                  pl.BlockSpec((1, tc), lambda i: (0, i))],
        out_specs=pl.BlockSpec((S, tc), lambda i: (0, i)),
        out_shape=jax.ShapeDtypeStruct((S, D), F32),
        scratch_shapes=[pltpu.VMEM((S + 2 * CONV_HALO, tc), F32)],
        compiler_params=_cparams("parallel"),
    )(xin, w32, bias)


def _dwconv_dw(xin, dy):
    S, D = xin.shape
    tc = min(256, D)
    tt = min(256, S)

    def body(x_ref, dy_ref, o_ref, pad_ref):
        _fill_padded(pad_ref, x_ref, S)

        def tile(ti, acc):
            t0 = pl.multiple_of(ti * tt, tt)
            win = pad_ref[pl.ds(t0, tt + 2 * CONV_HALO), :]
            dyt = dy_ref[pl.ds(t0, tt), :]
            ridx = lax.broadcasted_iota(jnp.int32, (CONV_HALO, tc), 0)
            upd = jnp.zeros((CONV_HALO, tc), F32)
            for k, shifted in _conv_taps(win, tt, False):
                upd = jnp.where(ridx == k, jnp.sum(dyt * shifted, axis=0, keepdims=True), upd)
            return acc + upd

        o_ref[...] = lax.fori_loop(0, S // tt, tile, jnp.zeros((CONV_HALO, tc), F32))

    return pl.pallas_call(
        body, name="dwconv_dw", grid=(D // tc,),
        in_specs=[pl.BlockSpec((S, tc), lambda i: (0, i)), pl.BlockSpec((S, tc), lambda i: (0, i))],
        out_specs=pl.BlockSpec((CONV_HALO, tc), lambda i: (0, i)),
        out_shape=jax.ShapeDtypeStruct((CONV_HALO, D), F32),
        scratch_shapes=[pltpu.VMEM((S + 2 * CONV_HALO, tc), F32)],
        compiler_params=_cparams("parallel"),
    )(xin, dy)


def _place():
    x, y, c = lax.axis_index("x"), lax.axis_index("y"), lax.axis_index("c")
    chips = [(1 - x, y), (x, 1 - y), (1 - x, 1 - y)]
    return x, y, c, chips


def _remote(src, dst, ssem, rsem, dev):
    return pltpu.make_async_remote_copy(src_ref=src, dst_ref=dst, send_sem=ssem, recv_sem=rsem,
                                        device_id=dev, device_id_type=MESH_IDS)


_ANY = pl.BlockSpec(memory_space=pl.ANY)


def _place_own(name, own, place):
    R, W = own.shape
    tr = _pick_rows(R)

    def body(p_ref, a_ref, o_ref):
        del p_ref
        o_ref[...] = a_ref[...]

    return pl.pallas_call(
        body, name=name,
        grid_spec=pltpu.PrefetchScalarGridSpec(
            num_scalar_prefetch=1, grid=(R // tr,),
            in_specs=[pl.BlockSpec((tr, W), lambda i, p: (i, 0))],
            out_specs=pl.BlockSpec((None, tr, W), lambda i, p: (p[1], i, 0))),
        out_shape=jax.ShapeDtypeStruct((N_CHIPS, R, W), own.dtype),
        compiler_params=_cparams("parallel"),
    )(place, own)


_HBM = pl.BlockSpec(memory_space=pltpu.HBM)
_SEM = pl.BlockSpec(memory_space=pltpu.SEMAPHORE)
GROUPS = ('qkv', 'ffn', 'pw1', 'dm')


def _half_rows(c, r0, n):
    return pl.ds(pl.multiple_of(r0 + c * (n // 2), SUBLANES), n // 2)


def _gather_start(wgs, layers, after):
    G, L = len(wgs), len(layers)

    def body(*refs):
        outs = refs[G + 1:]
        ssems, rsems, bufs = outs[:L], outs[L:2 * L], outs[2 * L:]
        x, y, c, chips = _place()
        me = 2 * x + y
        for li, pieces in enumerate(layers):
            for pi, (g, r0, n) in enumerate(pieces):
                blk = bufs[g].at[me, _half_rows(c, r0, n)]
                for j, (px, py) in enumerate(chips):
                    _remote(blk, blk, ssems[li].at[3 * pi + j], rsems[li].at[3 * pi + j], (px, py, c)).start()

    sem_shapes = [pltpu.SemaphoreType.DMA((3 * len(p),)) for p in layers]
    res = pl.pallas_call(
        body, name="gather_start", in_specs=[_HBM] * G + [_ANY],
        out_specs=[_SEM] * (2 * L) + [_HBM] * G,
        out_shape=sem_shapes + sem_shapes + [pltpu.HBM(w.shape, w.dtype) for w in wgs],
        input_output_aliases={g: 2 * L + g for g in range(G)},
        compiler_params=pltpu.CompilerParams(has_side_effects=pltpu.SideEffectType.DATAFLOW_SIDE_EFFECTING),
    )(*[pltpu.with_memory_space_constraint(w, pltpu.HBM) for w in wgs], after)
    return res[:L], res[L:2 * L], list(res[2 * L:])


def _gather_wait(name, wgs, ssem, rsem, pieces, after):
    G = len(wgs)

    def body(*refs):
        ssem_ref, rsem_ref = refs[G], refs[G + 1]
        bufs = refs[G + 3:]
        x, y, c, chips = _place()
        me = 2 * x + y
        for pi, (g, r0, n) in enumerate(pieces):
            rows = _half_rows(c, r0, n)
            for j, (px, py) in enumerate(chips):
                cp = _remote(bufs[g].at[me, rows], bufs[g].at[2 * px + py, rows],
                             ssem_ref.at[3 * pi + j], rsem_ref.at[3 * pi + j], (px, py, c))
                cp.wait_send()
                cp.wait_recv()

    return list(pl.pallas_call(
        body, name=name, in_specs=[_HBM] * G + [_SEM, _SEM, _ANY], out_specs=[_HBM] * G,
        out_shape=[pltpu.HBM(w.shape, w.dtype) for w in wgs],
        input_output_aliases={g: g for g in range(G)},
        compiler_params=pltpu.CompilerParams(has_side_effects=pltpu.SideEffectType.DATAFLOW_SIDE_EFFECTING),
    )(*wgs, ssem, rsem, after))


def _gather_forward(name, wgs, pieces):
    G = len(wgs)
    n_cp = 3 * len(pieces)

    def body(*refs):
        bufs, ssems, rsems = refs[G:2 * G], refs[2 * G], refs[2 * G + 1]
        x, y, c, chips = _place()
        sib = (x, y, 1 - c)
        cps = []
        for pi, (g, r0, n) in enumerate(pieces):
            for j, (px, py) in enumerate(chips):
                blk = bufs[g].at[2 * px + py, _half_rows(c, r0, n)]
                cps.append(_remote(blk, blk, ssems.at[3 * pi + j], rsems.at[3 * pi + j], sib))
        for cp in cps:
            cp.start()
        for pi, (g, r0, n) in enumerate(pieces):
            for j, (px, py) in enumerate(chips):
                blk = bufs[g].at[2 * px + py, _half_rows(1 - c, r0, n)]
                _remote(blk, blk, ssems.at[3 * pi + j], rsems.at[3 * pi + j], sib).wait_recv()
        for cp in cps:
            cp.wait_send()

    return list(pl.pallas_call(
        body, name=name, in_specs=[_ANY] * G, out_specs=[_ANY] * G,
        out_shape=[jax.ShapeDtypeStruct(w.shape, w.dtype) for w in wgs],
        input_output_aliases={g: g for g in range(G)},
        scratch_shapes=[pltpu.SemaphoreType.DMA((n_cp,)), pltpu.SemaphoreType.DMA((n_cp,))],
        compiler_params=pltpu.CompilerParams(has_side_effects=True),
    )(*wgs))


def _swap_halves(name, dgs, pieces):
    G = len(dgs)
    n_cp = N_CHIPS * len(pieces)

    def body(*refs):
        srcs, lands, ssems, rsems = refs[:G], refs[G:G + len(pieces)], refs[-2], refs[-1]
        x, y, c, _ = _place()
        cps = [_remote(srcs[g].at[j, _half_rows(1 - c, r0, n)], lands[pi].at[j],
                       ssems.at[N_CHIPS * pi + j], rsems.at[N_CHIPS * pi + j], (x, y, 1 - c))
               for pi, (g, r0, n) in enumerate(pieces) for j in range(N_CHIPS)]
        for cp in cps:
            cp.start()
        for cp in cps:
            cp.wait()

    return list(pl.pallas_call(
        body, name=name, in_specs=[_ANY] * G, out_specs=[_ANY] * len(pieces),
        out_shape=[jax.ShapeDtypeStruct((N_CHIPS, n // 2, dgs[g].shape[2]), dgs[g].dtype) for g, _, n in pieces],
        scratch_shapes=[pltpu.SemaphoreType.DMA((n_cp,)), pltpu.SemaphoreType.DMA((n_cp,))],
        compiler_params=pltpu.CompilerParams(has_side_effects=True),
    )(*dgs))


def _scatter_start(name, pbs):
    P = len(pbs)

    def body(*refs):
        outs = refs[2 * P:]
        ssems, rsems, src, land = outs[0], outs[1], outs[2:2 + P], outs[2 + P:]
        x, y, c, chips = _place()
        me = 2 * x + y
        for pi in range(P):
            for j, (px, py) in enumerate(chips):
                _remote(src[pi].at[2 * px + py], land[pi].at[me], ssems.at[3 * pi + j], rsems.at[3 * pi + j], (px, py, c)).start()

    sems = pltpu.SemaphoreType.DMA((3 * P,))
    hbm = [pltpu.HBM(p.shape, p.dtype) for p in pbs]
    res = pl.pallas_call(
        body, name=name, in_specs=[_HBM] * (2 * P), out_specs=[_SEM, _SEM] + [_HBM] * (2 * P),
        out_shape=[sems, sems] + hbm + hbm,
        input_output_aliases={k: 2 + k for k in range(2 * P)},
        compiler_params=pltpu.CompilerParams(has_side_effects=pltpu.SideEffectType.DATAFLOW_SIDE_EFFECTING),
    )(*[pltpu.with_memory_space_constraint(p, pltpu.HBM) for p in pbs],
      *[pltpu.with_memory_space_constraint(lax.empty(p.shape, p.dtype), pltpu.HBM) for p in pbs])
    return res[0], res[1], list(res[2:2 + P]), list(res[2 + P:])


def _scatter_wait(name, ssem, rsem, pbs, lands, after):
    P = len(pbs)

    def body(*refs):
        ssems, rsems = refs[2 * P], refs[2 * P + 1]
        outs = refs[2 * P + 3:]
        src, land = outs[:P], outs[P:]
        x, y, c, chips = _place()
        for pi in range(P):
            for j, (px, py) in enumerate(chips):
                cp = _remote(src[pi].at[2 * px + py], land[pi].at[2 * px + py], ssems.at[3 * pi + j], rsems.at[3 * pi + j], (px, py, c))
                cp.wait_send()
                cp.wait_recv()

    hbm = [pltpu.HBM(p.shape, p.dtype) for p in pbs]
    res = pl.pallas_call(
        body, name=name, in_specs=[_HBM] * (2 * P) + [_SEM, _SEM, _ANY], out_specs=[_HBM] * (2 * P),
        out_shape=hbm + hbm, input_output_aliases={k: k for k in range(2 * P)},
        compiler_params=pltpu.CompilerParams(has_side_effects=pltpu.SideEffectType.DATAFLOW_SIDE_EFFECTING),
    )(*pbs, *lands, ssem, rsem, after)
    return list(res[:P]), list(res[P:])


def _share_halves(name, gf, pieces):
    def body(in_ref, out, ssems, rsems):
        del in_ref
        x, y, c, _ = _place()
        cps = []
        for pi, (r0, n) in enumerate(pieces):
            mine = out.at[_half_rows(c, r0, n)]
            cps.append(_remote(mine, mine, ssems.at[pi], rsems.at[pi], (x, y, 1 - c)))
        for cp in cps:
            cp.start()
        for pi, (r0, n) in enumerate(pieces):
            theirs = out.at[_half_rows(1 - c, r0, n)]
            _remote(theirs, theirs, ssems.at[pi], rsems.at[pi], (x, y, 1 - c)).wait_recv()
        for cp in cps:
            cp.wait_send()

    return pl.pallas_call(
        body, name=name, in_specs=[_ANY], out_specs=_ANY,
        out_shape=jax.ShapeDtypeStruct(gf.shape, gf.dtype), input_output_aliases={0: 0},
        scratch_shapes=[pltpu.SemaphoreType.DMA((len(pieces),)), pltpu.SemaphoreType.DMA((len(pieces),))],
        compiler_params=pltpu.CompilerParams(has_side_effects=True),
    )(gf)


def _broadcast_small(name, buf):
    R = buf.shape[0]

    def body(src, out, ssems, rsems):
        x, y, c, _ = _place()
        me = 4 * x + 2 * y + c
        out[me] = src[...]
        peers = []
        for mask in range(1, 8):
            fx, fy, fc = (mask >> 2) & 1, (mask >> 1) & 1, mask & 1
            peers.append((1 - x if fx else x, 1 - y if fy else y, 1 - c if fc else c))
        cps = [_remote(src, out.at[me], ssems.at[k], rsems.at[k], p) for k, p in enumerate(peers)]
        for cp in cps:
            cp.start()
        for k, (px, py, pc) in enumerate(peers):
            blk = out.at[4 * px + 2 * py + pc]
            _remote(blk, blk, ssems.at[k], rsems.at[k], (px, py, pc)).wait_recv()
        for cp in cps:
            cp.wait_send()

    return pl.pallas_call(
        body, name=name, in_specs=[pl.BlockSpec(memory_space=pltpu.VMEM)], out_specs=pl.BlockSpec(memory_space=pltpu.VMEM),
        out_shape=jax.ShapeDtypeStruct((8, R, LANES), F32),
        scratch_shapes=[pltpu.SemaphoreType.DMA((7,)), pltpu.SemaphoreType.DMA((7,))],
        compiler_params=pltpu.CompilerParams(has_side_effects=True, vmem_limit_bytes=VMEM_LIMIT_V7X),
    )(buf)


def _sum_slabs(name, slabs):
    n, R, _ = slabs.shape

    def body(s_ref, o_ref):
        acc = s_ref[0]
        for k in range(1, n):
            acc = acc + s_ref[k]
        o_ref[...] = acc

    return pl.pallas_call(
        body, name=name, out_shape=jax.ShapeDtypeStruct((R, LANES), F32),
        in_specs=[pl.BlockSpec(memory_space=pltpu.VMEM)], out_specs=pl.BlockSpec(memory_space=pltpu.VMEM),
        compiler_params=_cparams(),
    )(slabs)


def _pick_rows(rows, target=512):
    best = SUBLANES
    for t in range(SUBLANES, min(rows, target) + 1, SUBLANES):
        if rows % t == 0:
            best = t
    return best


def _half_tile(r0, n):
    return _pick_rows(math.gcd(r0, n // 2) if r0 else n // 2)


def _pair_sum(name, dg, land, place, r0, n):
    W = dg.shape[2]
    tr = _half_tile(r0, n)
    nb = (n // 2) // tr

    def body(p_ref, a_ref, b_ref, o_ref):
        del p_ref
        o_ref[...] = (a_ref[...].astype(F32) + b_ref[...].astype(F32)).astype(o_ref.dtype)

    return pl.pallas_call(
        body, name=name,
        grid_spec=pltpu.PrefetchScalarGridSpec(
            num_scalar_prefetch=1, grid=(N_CHIPS, nb),
            in_specs=[pl.BlockSpec((None, tr, W), lambda j, i, p: (j, r0 // tr + p[0] * nb + i, 0)),
                      pl.BlockSpec((None, tr, W), lambda j, i, p: (j, i, 0))],
            out_specs=pl.BlockSpec((None, tr, W), lambda j, i, p: (j, i, 0))),
        out_shape=jax.ShapeDtypeStruct((N_CHIPS, n // 2, W), BF16),
        compiler_params=_cparams("parallel", "parallel"),
    )(place, dg, land)


def _chip_sum(name, pb, land, place, gf, r0, n):
    W = gf.shape[1]
    tr = _half_tile(r0, n)
    nb = (n // 2) // tr

    def body(p_ref, own_ref, lx_ref, ly_ref, ld_ref, gf_in, o_ref):
        del p_ref, gf_in
        o_ref[...] = ((own_ref[...].astype(F32) + lx_ref[...].astype(F32)) + ly_ref[...].astype(F32)) + ld_ref[...].astype(F32)

    slab = lambda flip: pl.BlockSpec((None, tr, W), lambda i, p, _f=flip: (p[1] ^ _f, i, 0))
    return pl.pallas_call(
        body, name=name,
        grid_spec=pltpu.PrefetchScalarGridSpec(
            num_scalar_prefetch=1, grid=(nb,),
            in_specs=[slab(0), slab(2), slab(1), slab(3), pl.BlockSpec(memory_space=pl.ANY)],
            out_specs=pl.BlockSpec((tr, W), lambda i, p: (r0 // tr + p[0] * nb + i, 0))),
        out_shape=jax.ShapeDtypeStruct(gf.shape, F32),
        input_output_aliases={5: 0},
        compiler_params=_cparams("parallel"),
    )(place, pb, land, land, land, gf)


def _ln_stats(z):
    mu = jnp.mean(z, axis=1, keepdims=True)
    zc = z - mu
    rstd = lax.rsqrt(jnp.mean(zc * zc, axis=1, keepdims=True) + LN_EPS)
    return zc * rstd, rstd


def _ln_fwd(name, xin, m, g, b):
    S, D = xin.shape

    def fn(x_, m_, g_, b_):
        xhat, rstd = _ln_stats(ALPHA * x_ + m_)
        y = xhat * g_ + b_
        return y, y, xhat, rstd

    return _rowwise(name, fn, [xin, m, ('full', g), ('full', b)],
                    [('rows', D, F32), ('rows', D, BF16), ('rows', D, F32), ('rows', 1, F32)], S)


def _ln_bwd_core(dy, xhat, rstd, g):
    dxh = dy * g
    return rstd * (dxh - jnp.mean(dxh, axis=1, keepdims=True) - xhat * jnp.mean(dxh * xhat, axis=1, keepdims=True))


def _ln_bwd(name, terms, xhat, rstd, g):
    S, D = xhat.shape
    scales = [s for _, s in terms]
    n = len(terms)

    def fn(*v):
        dy = v[0] * scales[0] if scales[0] != 1.0 else v[0]
        for t in range(1, n):
            dy = dy + (v[t] * scales[t] if scales[t] != 1.0 else v[t])
        xh, rs, g_ = v[n], v[n + 1], v[n + 2]
        dz = _ln_bwd_core(dy, xh, rs, g_)
        return dz, dz, jnp.sum(dy * xh, axis=0, keepdims=True), jnp.sum(dy, axis=0, keepdims=True)

    return _rowwise(name, fn, [a for a, _ in terms] + [xhat, rstd, ('full', g)],
                    [('rows', D, F32), ('rows', D, BF16), ('acc', (1, D), F32), ('acc', (1, D), F32)], S)


def _adamw_math(w, g, m, v):
    m2 = ADAM_B1 * m + (1.0 - ADAM_B1) * g
    v2 = ADAM_B2 * v + (1.0 - ADAM_B2) * (g * g)
    m_hat = m2 / (1.0 - ADAM_B1 ** ADAM_STEP)
    v_hat = v2 / (1.0 - ADAM_B2 ** ADAM_STEP)
    delta = -ADAM_LR * (m_hat / (jnp.sqrt(v_hat) + ADAM_EPS) + ADAM_WD * w)
    return delta, m2, v2


def _adamw(name, w, gfull, row_start, m, v):
    rows, W = w.shape
    tr = math.gcd(math.gcd(rows, row_start), 256) if row_start else math.gcd(rows, 256)

    def fn(w_, g_, m_, v_):
        d, m2, v2 = _adamw_math(w_, g_, m_, v_)
        return g_, d, m2, v2

    return _rowwise(name, fn, [w, ('off', gfull, row_start // tr), m, v], [('rows', W, F32)] * 4, rows, tm=tr)


def _pack(arrs):
    flat = jnp.concatenate([a.reshape(-1).astype(F32) for a in arrs])
    tile = SUBLANES * LANES
    n = -(-flat.shape[0] // tile) * tile
    return jnp.pad(flat, (0, n - flat.shape[0])).reshape(-1, LANES)


def _unpack(buf, shapes):
    flat = buf.reshape(-1)
    out, pos = [], 0
    for shp in shapes:
        n = math.prod(shp)
        out.append(flat[pos:pos + n].reshape(shp))
        pos += n
    return out


BIG = ['fox_w_qkv', 'fox_w_o', 'rel_w_qkv', 'rel_w_o', 'conv_w_pw1', 'conv_w_pw2', 'ffn_w_gate', 'ffn_w_up', 'ffn_w_down']
SMALL_SHARDED = ['fox_w_f', 'conv_b_pw1', 'conv_w_dw', 'conv_b_dw', 'conv_ln_g', 'conv_ln_b', 'conv_b_pw2']
SMALL_SHARD_AXIS = {'fox_w_f': 1, 'conv_b_pw1': 1, 'conv_w_dw': 2, 'conv_b_dw': 1, 'conv_ln_g': 1, 'conv_ln_b': 1, 'conv_b_pw2': 1}
SMALL_REPL = ['fox_b_f', 'rel_bias', 'ln_mix_g', 'ln_mix_b', 'ln_ffn_g', 'ln_ffn_b']
SMALL = SMALL_SHARDED + SMALL_REPL
WEIGHTS = ['fox_w_qkv', 'fox_w_f', 'fox_b_f', 'fox_w_o', 'rel_w_qkv', 'rel_bias', 'rel_w_o', 'conv_w_pw1', 'conv_b_pw1',
           'conv_w_dw', 'conv_b_dw', 'conv_ln_g', 'conv_ln_b', 'conv_w_pw2', 'conv_b_pw2', 'ffn_w_gate', 'ffn_w_up',
           'ffn_w_down', 'ln_mix_g', 'ln_mix_b', 'ln_ffn_g', 'ln_ffn_b']


def kernel(x, fox_w_qkv, fox_w_f, fox_b_f, fox_w_o, rel_w_qkv, rel_bias, rel_w_o, conv_w_pw1, conv_b_pw1, conv_w_dw, conv_b_dw, conv_ln_g, conv_ln_b, conv_w_pw2, conv_b_pw2, ffn_w_gate, ffn_w_up, ffn_w_down, ln_mix_g, ln_mix_b, ln_ffn_g, ln_ffn_b, loss_target, m_fox_w_qkv, m_fox_w_f, m_fox_b_f, m_fox_w_o, m_rel_w_qkv, m_rel_bias, m_rel_w_o, m_conv_w_pw1, m_conv_b_pw1, m_conv_w_dw, m_conv_b_dw, m_conv_ln_g, m_conv_ln_b, m_conv_w_pw2, m_conv_b_pw2, m_ffn_w_gate, m_ffn_w_up, m_ffn_w_down, m_ln_mix_g, m_ln_mix_b, m_ln_ffn_g, m_ln_ffn_b, v_fox_w_qkv, v_fox_w_f, v_fox_b_f, v_fox_w_o, v_rel_w_qkv, v_rel_bias, v_rel_w_o, v_conv_w_pw1, v_conv_b_pw1, v_conv_w_dw, v_conv_b_dw, v_conv_ln_g, v_conv_ln_b, v_conv_w_pw2, v_conv_b_pw2, v_ffn_w_gate, v_ffn_w_up, v_ffn_w_down, v_ln_mix_g, v_ln_mix_b, v_ln_ffn_g, v_ln_ffn_b):
    A = dict(locals())
    Wt = {n: A[n] for n in WEIGHTS}
    Mo = {n: A['m_' + n] for n in WEIGHTS}
    Vo = {n: A['v_' + n] for n in WEIGHTS}

    _, S, D = x.shape
    H = D // HEAD_DIM
    Ds = D // N_CHIPS
    Nq = fox_w_qkv.shape[2]
    Np = conv_w_pw1.shape[2]
    Fs = ffn_w_gate.shape[2]
    my_x, my_y, my_c = lax.axis_index("x"), lax.axis_index("y"), lax.axis_index("c")
    my_chip = 2 * my_x + my_y
    place = jnp.stack([my_c, my_chip]).astype(jnp.int32)

    wo_base = DEPTH * Fs
    where = {
        'fox_w_qkv': ('qkv', 0), 'rel_w_qkv': ('qkv', N_FOX * D),
        'ffn_w_gate': ('ffn', 0), 'ffn_w_up': ('ffn', DEPTH * D),
        'conv_w_pw1': ('pw1', 0),
        'ffn_w_down': ('dm', 0), 'fox_w_o': ('dm', wo_base), 'rel_w_o': ('dm', wo_base + N_FOX * Ds),
        'conv_w_pw2': ('dm', wo_base + (N_FOX + 1) * Ds),
    }
    members = {'qkv': ['fox_w_qkv', 'rel_w_qkv'], 'ffn': ['ffn_w_gate', 'ffn_w_up'], 'pw1': ['conv_w_pw1'],
               'dm': ['ffn_w_down', 'fox_w_o', 'rel_w_o', 'conv_w_pw2']}
    flat2 = lambda a: a.reshape(-1, a.shape[-1])
    own = {g: jnp.concatenate([flat2(Wt[n]).astype(BF16) for n in ms], axis=0) for g, ms in members.items()}

    def layer_pieces(i):
        kind, j = i % 3, i // 3
        slot = j if kind == 0 else (N_FOX if kind == 1 else N_FOX + 1)
        w_in = (GROUPS.index('pw1'), 0, D) if kind == 2 else (GROUPS.index('qkv'), slot * D, D)
        return [w_in, (GROUPS.index('dm'), wo_base + slot * Ds, Ds), (GROUPS.index('ffn'), i * D, D),
                (GROUPS.index('ffn'), (DEPTH + i) * D, D), (GROUPS.index('dm'), i * Fs, Fs)]

    small_shapes = [Wt[n].shape for n in SMALL_SHARDED]
    slabs = _broadcast_small("gather_small", _pack([Wt[n] for n in SMALL_SHARDED]))

    layers = [layer_pieces(i) for i in range(DEPTH)]
    gather_ssems, gather_rsems, wg_list = _gather_start(
        [_place_own("place_" + g, own[g], place) for g in GROUPS], layers, slabs)
    WG = dict(zip(GROUPS, wg_list))
    DG = {g: lax.empty(WG[g].shape, BF16) for g in own}

    per_chip = [_unpack(slabs[2 * j], small_shapes) for j in range(N_CHIPS)]
    full = {n: jnp.concatenate([per_chip[j][i] for j in range(N_CHIPS)], axis=SMALL_SHARD_AXIS[n])
            for i, n in enumerate(SMALL_SHARDED)}
    row = lambda v: v.reshape(1, -1)

    SG = {}

    def ffn_fwd(i, xb):
        hg = _mm_col(f"ffn{i}_gate", xb, WG['ffn'], i * D)
        hu = _mm_col(f"ffn{i}_up", xb, WG['ffn'], (DEPTH + i) * D)
        act, = _rowwise(f"ffn{i}_act", lambda g_, u_: [g_.astype(F32) * _sigmoid(g_.astype(F32)) * u_.astype(F32)],
                        [hg, hu], [('rows', N_CHIPS * Fs, BF16)], S)
        f = _mm_row(f"ffn{i}_down", act, WG['dm'], i * Fs, Fs)
        return f, (hg, hu, act)

    def ffn_bwd(i, xb, saved, dzb):
        hg, hu, act = saved
        DG['dm'] = _mm_dw(f"ffn{i}_dw_down", act, dzb, DG['dm'], i * Fs, 'row')
        dact = _mm_row_t(f"ffn{i}_dact", dzb, WG['dm'], i * Fs, Fs, F32)

        def fn(da_, g_, u_):
            g32, u32 = g_.astype(F32), u_.astype(F32)
            sg = _sigmoid(g32)
            silu = g32 * sg
            return da_ * u32 * (sg * (1.0 + g32 * (1.0 - sg))), da_ * silu

        dhg, dhu = _rowwise(f"ffn{i}_dact_split", fn, [dact, hg, hu],
                            [('rows', N_CHIPS * Fs, BF16), ('rows', N_CHIPS * Fs, BF16)], S)
        DG['ffn'] = _mm_dw(f"ffn{i}_dw_gate", xb, dhg, DG['ffn'], i * D, 'col')
        DG['ffn'] = _mm_dw(f"ffn{i}_dw_up", xb, dhu, DG['ffn'], (DEPTH + i) * D, 'col')
        return _mm_col_t(f"ffn{i}_dx", [(dhg, i * D), (dhu, (DEPTH + i) * D)], WG['ffn'], D)

    def fox_fwd(j, xb):
        qkv = _mm_col(f"fox{j}_qkv", xb, WG['qkv'], j * D)
        wf = full['fox_w_f'][j].astype(BF16)
        def gate_fn(x_, w_, b_):
            z_ = jnp.dot(x_, w_, preferred_element_type=F32) + b_
            return z_, jnp.minimum(z_, 0.0) - jnp.log(1.0 + jnp.exp(-jnp.abs(z_)))

        z, logf = _rowwise(f"fox{j}_gate", gate_fn, [xb, ('full', wf), ('full', row(fox_b_f[j]))],
                           [('rows', H, F32), ('rows', H, F32)], S)
        c = _cumsum_rows(f"fox{j}_cumsum", logf, False)
        crow = c.T.reshape(H, 1, S)
        o = _fox_fwd(qkv, c, crow, H)
        m = _mm_row(f"fox{j}_wo", o, WG['dm'], wo_base + j * Ds, Ds)
        return m, (qkv, z, c, crow, o, wf)

    def fox_bwd(j, xb, saved, dzb):
        qkv, z, c, crow, o, wf = saved
        DG['dm'] = _mm_dw(f"fox{j}_dw_o", o, dzb, DG['dm'], wo_base + j * Ds, 'row')
        do = _mm_row_t(f"fox{j}_do", dzb, WG['dm'], wo_base + j * Ds, Ds, BF16)
        dq, dk, dv, dcrow = _fox_bwd(qkv, c, crow, do, H)
        dqkv = jnp.concatenate([dq, dk, dv], axis=1)
        dlogf = _cumsum_rows(f"fox{j}_rcumsum", dcrow.reshape(H, S).T, True)

        def fn(x_, dl_, z_, w_):
            dz_ = dl_ * _sigmoid(-z_)
            dzb_ = dz_.astype(BF16)
            return _dot_nt(dzb_, w_), _dot_tn(x_, dzb_), jnp.sum(dz_, axis=0, keepdims=True)

        dh_f, dwf, dbf = _rowwise(f"fox{j}_gate_bwd", fn, [xb, dlogf, z, ('full', wf)],
                                  [('rows', D, F32), ('acc', (D, H), F32), ('acc', (1, H), F32)], S)
        SG.setdefault('fox_w_f', [None] * N_FOX)[j] = dwf
        SG.setdefault('fox_b_f', [None] * N_FOX)[j] = dbf.reshape(H)
        DG['qkv'] = _mm_dw(f"fox{j}_dw_qkv", xb, dqkv, DG['qkv'], j * D, 'col')
        dh = _mm_col_t(f"fox{j}_dx", [(dqkv, j * D)], WG['qkv'], D)
        return [dh, dh_f]

    def rel_fwd(xb):
        qkv = _mm_col("rel_qkv", xb, WG['qkv'], N_FOX * D)
        rb_pad = jnp.pad(rel_bias[0], ((0, 0), (0, REL_TABLE_PAD - REL_TABLE)))
        bias = jnp.transpose(_rel_expand(rb_pad), (1, 0, 2))
        o = _rel_fwd(qkv, bias, H)
        m = _mm_row("rel_wo", o, WG['dm'], wo_base + N_FOX * Ds, Ds)
        return m, (qkv, bias, o)

    def rel_bwd(xb, saved, dzb):
        qkv, bias, o = saved
        DG['dm'] = _mm_dw("rel_dw_o", o, dzb, DG['dm'], wo_base + N_FOX * Ds, 'row')
        do = _mm_row_t("rel_do", dzb, WG['dm'], wo_base + N_FOX * Ds, Ds, BF16)
        dq, dk, dv, dbias = _rel_bwd(qkv, bias, do, H)
        SG['rel_bias'] = _rel_reduce(jnp.transpose(dbias, (1, 0, 2)))[:, :REL_TABLE].reshape(1, H, REL_TABLE)
        dqkv = jnp.concatenate([dq, dk, dv], axis=1)
        DG['qkv'] = _mm_dw("rel_dw_qkv", xb, dqkv, DG['qkv'], N_FOX * D, 'col')
        return [_mm_col_t("rel_dx", [(dqkv, N_FOX * D)], WG['qkv'], D)]

    w_dw32 = jnp.pad(full['conv_w_dw'][0], ((0, CONV_HALO - CONV_K), (0, 0)))
    cg, cb = full['conv_ln_g'], full['conv_ln_b']

    def conv_fwd(xb):
        u = _mm_col("conv_pw1", xb, WG['pw1'], 0, bias=full['conv_b_pw1'], out_dtype=F32)
        u2, = _rowwise("conv_glu", lambda a_, g_: [a_ * _sigmoid(g_)],
                       [('cols', u, D, 0), ('cols', u, D, 1)], [('rows', D, F32)], S)
        yc = _dwconv("conv_dw", u2, w_dw32, full['conv_b_dw'], False)

        def fn(y_, g_, b_):
            xhat, rstd = _ln_stats(y_)
            ln = xhat * g_ + b_
            return ln * _sigmoid(ln), xhat, rstd

        zc, xhat, rstd = _rowwise("conv_ln_silu", fn, [yc, ('full', cg), ('full', cb)],
                                  [('rows', D, BF16), ('rows', D, F32), ('rows', 1, F32)], S)
        m = _mm_row("conv_pw2", zc, WG['dm'], wo_base + (N_FOX + 1) * Ds, Ds, bias=full['conv_b_pw2'])
        return m, (u, u2, zc, xhat, rstd)

    def conv_bwd(xb, saved, dz, dzb):
        u, u2, zc, xhat, rstd = saved
        r0 = wo_base + (N_FOX + 1) * Ds
        DG['dm'] = _mm_dw("conv_dw_pw2", zc, dzb, DG['dm'], r0, 'row')
        dzc = _mm_row_t("conv_dzc", dzb, WG['dm'], r0, Ds, F32)

        def fn(dm_, dzc_, xh_, rs_, g_, b_):
            ln = xh_ * g_ + b_
            sg = _sigmoid(ln)
            dln = dzc_ * (sg * (1.0 + ln * (1.0 - sg)))
            dyc = _ln_bwd_core(dln, xh_, rs_, g_)
            col = lambda t: jnp.sum(t, axis=0, keepdims=True)
            return dyc, col(dm_), col(dln * xh_), col(dln), col(dyc)

        dyc, SG['conv_b_pw2'], SG['conv_ln_g'], SG['conv_ln_b'], SG['conv_b_dw'] = _rowwise(
            "conv_ln_silu_bwd", fn, [dz, dzc, xhat, rstd, ('full', cg), ('full', cb)],
            [('rows', D, F32)] + [('acc', (1, D), F32)] * 4, S)
        du2 = _dwconv("conv_dw_bwd_x", dyc, w_dw32, jnp.zeros((1, D), F32), True)
        SG['conv_w_dw'] = _dwconv_dw(u2, dyc)[:CONV_K].reshape(1, CONV_K, D)

        def fn2(du2_, a_, g_):
            sg = _sigmoid(g_)
            da, dgt = du2_ * sg, du2_ * a_ * sg * (1.0 - sg)
            return da, dgt, jnp.sum(da, axis=0, keepdims=True), jnp.sum(dgt, axis=0, keepdims=True)

        da, dgt, dba, dbg = _rowwise("conv_glu_bwd", fn2, [du2, ('cols', u, D, 0), ('cols', u, D, 1)],
                                     [('rows', D, BF16), ('rows', D, BF16), ('acc', (1, D), F32), ('acc', (1, D), F32)], S)
        SG['conv_b_pw1'] = jnp.concatenate([dba, dbg], axis=1)
        du = jnp.concatenate([da, dgt], axis=1)
        DG['pw1'] = _mm_dw("conv_dw_pw1", xb, du, DG['pw1'], 0, 'col')
        return [_mm_col_t("conv_dx", [(du, 0)], WG['pw1'], D)]

    xs = x[0]
    xs_b = xs.astype(BF16)
    tape = []
    for i in range(DEPTH):
        kind, j = i % 3, i // 3
        wg_list = _gather_wait(f"gather_wait{i}", wg_list, gather_ssems[i], gather_rsems[i], layers[i], xs)
        wg_list = _gather_forward(f"gather_fwd{i}", wg_list, layers[i])
        WG.update(zip(GROUPS, wg_list))
        if kind == 0:
            m, msaved = fox_fwd(j, xs_b)
        elif kind == 1:
            m, msaved = rel_fwd(xs_b)
        else:
            m, msaved = conv_fwd(xs_b)
        xm, xm_b, xhat1, rstd1 = _ln_fwd(f"ln_mix{i}", xs, m, row(ln_mix_g[i]), row(ln_mix_b[i]))
        f, fsaved = ffn_fwd(i, xm_b)
        xo, xo_b, xhat2, rstd2 = _ln_fwd(f"ln_ffn{i}", xm, f, row(ln_ffn_g[i]), row(ln_ffn_b[i]))
        tape.append((xs_b, msaved, xhat1, rstd1, xm_b, fsaved, xhat2, rstd2))
        xs, xs_b = xo, xo_b

    def loss_fn(y_, t_):
        e = y_ - t_
        return e * (1.0 / D), jnp.sum(e * e, axis=0, keepdims=True)

    dy, sq = _rowwise("loss", loss_fn, [xs, loss_target[0]], [('rows', D, F32), ('acc', (1, D), F32)], S)
    loss = lax.psum(jnp.sum(sq) * (0.5 / D), ("x", "y", "c"))

    GF = {g: lax.empty(WG[g].shape[1:], F32) for g in GROUPS}
    started = [None] * DEPTH

    def reduce_start(i):
        dgs = [DG[g] for g in GROUPS]
        lands = _swap_halves(f"pair_swap{i}", dgs, layers[i])
        pbs = [_pair_sum(f"pair_sum{i}_{pi}", dgs[g], lands[pi], place, r0, n) for pi, (g, r0, n) in enumerate(layers[i])]
        return _scatter_start(f"scatter_start{i}", pbs)

    def reduce_finish(i, after):
        ssem, rsem, pbs, lands2 = started[i]
        pbs, lands2 = _scatter_wait(f"scatter_wait{i}", ssem, rsem, pbs, lands2, after)
        for pi, (g, r0, n) in enumerate(layers[i]):
            GF[GROUPS[g]] = _chip_sum(f"chip_sum{i}_{pi}", pbs[pi], lands2[pi], place, GF[GROUPS[g]], r0, n)

    terms = [(dy, 1.0)]
    g_mix, b_mix, g_ffn, b_ffn = [None] * DEPTH, [None] * DEPTH, [None] * DEPTH, [None] * DEPTH
    for i in reversed(range(DEPTH)):
        kind, j = i % 3, i // 3
        xin_b, msaved, xhat1, rstd1, xm_b, fsaved, xhat2, rstd2 = tape[i]
        dz2, dz2b, g_ffn[i], b_ffn[i] = _ln_bwd(f"ln_ffn{i}_bwd", terms, xhat2, rstd2, row(ln_ffn_g[i]))
        dx_ffn = ffn_bwd(i, xm_b, fsaved, dz2b)
        dz1, dz1b, g_mix[i], b_mix[i] = _ln_bwd(f"ln_mix{i}_bwd", [(dz2, ALPHA), (dx_ffn, 1.0)], xhat1, rstd1, row(ln_mix_g[i]))
        if kind == 0:
            mix_terms = fox_bwd(j, xin_b, msaved, dz1b)
        elif kind == 1:
            mix_terms = rel_bwd(xin_b, msaved, dz1b)
        else:
            mix_terms = conv_bwd(xin_b, msaved, dz1, dz1b)
        terms = [(dz1, ALPHA)] + [(t, 1.0) for t in mix_terms]
        started[i] = reduce_start(i)
        if i + 1 < DEPTH:
            reduce_finish(i + 1, started[i][2][0])

    def gx_fn(*v):
        acc = v[0] * ALPHA
        for t in v[1:]:
            acc = acc + t
        return [acc]

    grad_x, = _rowwise("grad_x", gx_fn, [a for a, _ in terms], [('rows', D, F32)], S)
    grad_x = grad_x.reshape(1, S, D)

    SG['fox_w_f'] = jnp.stack(SG['fox_w_f'])
    SG['fox_b_f'] = jnp.stack(SG['fox_b_f'])
    SG['ln_mix_g'] = jnp.concatenate(g_mix, axis=0)
    SG['ln_mix_b'] = jnp.concatenate(b_mix, axis=0)
    SG['ln_ffn_g'] = jnp.concatenate(g_ffn, axis=0)
    SG['ln_ffn_b'] = jnp.concatenate(b_ffn, axis=0)

    grads, deltas, new_m, new_v = {}, {}, {}, {}

    reduce_finish(0, grad_x)
    for gi, g in enumerate(GROUPS):
        GF[g] = _share_halves("pair_share_" + g, GF[g], [(r0, n) for l in layers for (pg, r0, n) in l if pg == gi])

    for n in BIG:
        g, r0 = where[n]
        outs = _adamw("adamw_" + n, flat2(Wt[n]), GF[g], r0, flat2(Mo[n]), flat2(Vo[n]))
        grads[n], deltas[n], new_m[n], new_v[n] = [o.reshape(Wt[n].shape) for o in outs]

    full_shapes = [SG[n].shape for n in SMALL]
    summed = _sum_slabs("small_sum", _broadcast_small("small_exchange", _pack([SG[n] for n in SMALL])))
    gsm = dict(zip(SMALL, _unpack(summed, full_shapes)))
    for n in SMALL_SHARDED:
        ax = SMALL_SHARD_AXIS[n]
        width = Wt[n].shape[ax]
        gsm[n] = lax.dynamic_slice_in_dim(gsm[n], my_chip * width, width, axis=ax)
    own_shapes = [Wt[n].shape for n in SMALL]
    packed = [_pack([src[n] for n in SMALL]) for src in (Wt, gsm, Mo, Vo)]
    rows_small = packed[0].shape[0]

    def small_fn(w_, g_, m_, v_):
        return _adamw_math(w_, g_, m_, v_)

    sd, sm, sv = _rowwise("adamw_small", small_fn, packed, [('rows', LANES, F32)] * 3, rows_small, tm=rows_small)
    for n, d_, m_, v_ in zip(SMALL, _unpack(sd, own_shapes), _unpack(sm, own_shapes), _unpack(sv, own_shapes)):
        grads[n], deltas[n], new_m[n], new_v[n] = gsm[n], d_, m_, v_

    return (loss, grad_x, *[grads[n] for n in WEIGHTS], *[deltas[n] for n in WEIGHTS],
            *[new_m[n] for n in WEIGHTS], *[new_v[n] for n in WEIGHTS])
```

```python
import functools
import math

import jax
import jax.numpy as jnp
from jax import lax
from jax.experimental import pallas as pl
from jax.experimental.pallas import tpu as pltpu

F32 = jnp.float32
BF16 = jnp.bfloat16
MESH_IDS = pl.DeviceIdType.MESH
HIGHEST = lax.Precision.HIGHEST

N_CHIPS = 4
DEPTH = 4
N_FOX = 2
HEAD_DIM = 128
CHUNK = 64
LEFT_CHUNKS = 8
BAND_KEYS = (LEFT_CHUNKS + 1) * CHUNK
PAD_KEYS = LEFT_CHUNKS * CHUNK
REL_CLIP = 128
REL_TABLE = 2 * REL_CLIP + 1
REL_TABLE_PAD = 384
CONV_K = 31
CONV_HALO = 32
ALPHA = (2.0 * DEPTH) ** 0.25
LN_EPS = 1e-5
ADAM_LR, ADAM_B1, ADAM_B2, ADAM_EPS, ADAM_WD, ADAM_STEP = 0.001, 0.9, 0.999, 1e-08, 0.01, 10
NEG_BIG = -1e30
VMEM_LIMIT_V7X = 56 * 1024 * 1024
LANES = 128
SUBLANES = 8


def _cparams(*sem):
    return pltpu.CompilerParams(dimension_semantics=sem if sem else None, vmem_limit_bytes=VMEM_LIMIT_V7X)


def _pick(dim, target):
    best = None
    for t in range(LANES, min(dim, target) + 1, LANES):
        if dim % t == 0:
            best = t
    return best if best is not None else dim


def _dot_nt(a, b):
    return lax.dot_general(a, b, (((1,), (1,)), ((), ())), preferred_element_type=F32)


def _dot_tn(a, b):
    return lax.dot_general(a, b, (((0,), (0,)), ((), ())), preferred_element_type=F32)


def _sigmoid(z):
    return 1.0 / (1.0 + jnp.exp(-z))


def _rowwise(name, fn, ins, outs, S, tm=256):
    tm = min(tm, S)
    afters = [it[1] for it in ins if isinstance(it, tuple) and it[0] == 'after']
    ins = [it for it in ins if not (isinstance(it, tuple) and it[0] == 'after')]
    arrs, in_specs = [], []
    for it in ins:
        if isinstance(it, tuple) and it[0] == 'full':
            a = it[1]
            in_specs.append(pl.BlockSpec(a.shape, lambda i, _n=a.ndim: (0,) * _n))
        elif isinstance(it, tuple) and it[0] == 'cols':
            _, a, width, blk = it
            in_specs.append(pl.BlockSpec((tm, width), lambda i, _b=blk: (i, _b)))
        elif isinstance(it, tuple) and it[0] == 'off':
            _, a, off = it
            in_specs.append(pl.BlockSpec((tm, a.shape[1]), lambda i, _o=off: (i + _o, 0)))
        else:
            a = it
            in_specs.append(pl.BlockSpec((tm, a.shape[1]), lambda i: (i, 0)))
        arrs.append(a)
    out_shape, out_specs = [], []
    for kind, shp, dt in outs:
        if kind == 'rows':
            out_shape.append(jax.ShapeDtypeStruct((S, shp), dt))
            out_specs.append(pl.BlockSpec((tm, shp), lambda i: (i, 0)))
        else:
            out_shape.append(jax.ShapeDtypeStruct(shp, dt))
            out_specs.append(pl.BlockSpec(shp, lambda i, _n=len(shp): (0,) * _n))
    n_in = len(arrs)
    in_specs += [pl.BlockSpec(memory_space=pl.ANY)] * len(afters)

    def body(*refs):
        vals = fn(*[r[...] for r in refs[:n_in]])
        first = pl.program_id(0) == 0
        for (kind, _, _), r, v in zip(outs, refs[n_in + len(afters):], vals):
            if kind == 'rows':
                r[...] = v.astype(r.dtype)
            else:
                @pl.when(first)
                def _(r=r, v=v):
                    r[...] = v.astype(r.dtype)

                @pl.when(jnp.logical_not(first))
                def _(r=r, v=v):
                    r[...] += v.astype(r.dtype)

    has_acc = any(k != 'rows' for k, _, _ in outs)
    res = pl.pallas_call(
        body, name=name, grid=(S // tm,), in_specs=in_specs, out_specs=out_specs, out_shape=out_shape,
        compiler_params=_cparams("arbitrary" if has_acc else "parallel"),
    )(*arrs, *afters)
    return res


def _mm_col(name, a, wg, row_start, bias=None, out_dtype=BF16):
    S, K = a.shape
    _, _, Ns = wg.shape
    rb = row_start // K
    tm = min(512, S)

    def body(a_ref, w_ref, *rest):
        acc = jnp.dot(a_ref[...].astype(BF16), w_ref[...], preferred_element_type=F32)
        if bias is not None:
            acc = acc + rest[0][...]
        rest[-1][...] = acc.astype(out_dtype)

    in_specs = [pl.BlockSpec((tm, K), lambda j, m: (m, 0)), pl.BlockSpec((None, K, Ns), lambda j, m: (j, rb, 0))]
    args = [a, wg]
    if bias is not None:
        in_specs.append(pl.BlockSpec((1, Ns), lambda j, m: (0, j)))
        args.append(bias)
    return pl.pallas_call(
        body, name=name, grid=(N_CHIPS, S // tm), in_specs=in_specs,
        out_specs=pl.BlockSpec((tm, Ns), lambda j, m: (m, j)),
        out_shape=jax.ShapeDtypeStruct((S, N_CHIPS * Ns), out_dtype),
        compiler_params=_cparams("parallel", "parallel"),
    )(*args)


def _mm_row(name, a, wg, row_start, Ks, bias=None):
    S = a.shape[0]
    N = wg.shape[2]
    rb = row_start // Ks
    tm = min(512, S)

    def body(a_ref, w_ref, *rest):
        o_ref = rest[-1]
        j = pl.program_id(1)
        d = jnp.dot(a_ref[...].astype(BF16), w_ref[...], preferred_element_type=F32)

        @pl.when(j == 0)
        def _():
            o_ref[...] = d + rest[0][...] if bias is not None else d

        @pl.when(j > 0)
        def _():
            o_ref[...] += d

    in_specs = [pl.BlockSpec((tm, Ks), lambda m, j: (m, j)), pl.BlockSpec((None, Ks, N), lambda m, j: (j, rb, 0))]
    args = [a, wg]
    if bias is not None:
        in_specs.append(pl.BlockSpec((1, N), lambda m, j: (0, 0)))
        args.append(bias)
    return pl.pallas_call(
        body, name=name, grid=(S // tm, N_CHIPS), in_specs=in_specs,
        out_specs=pl.BlockSpec((tm, N), lambda m, j: (m, 0)),
        out_shape=jax.ShapeDtypeStruct((S, N), F32),
        compiler_params=_cparams("parallel", "arbitrary"),
    )(*args)


def _mm_col_t(name, pairs, wg, K):
    S = pairs[0][0].shape[0]
    Ns = wg.shape[2]
    tm = min(512, S)
    n = len(pairs)

    def body(*refs):
        o_ref = refs[-1]
        j = pl.program_id(1)
        d = _dot_nt(refs[0][...], refs[n][...])
        for p in range(1, n):
            d = d + _dot_nt(refs[p][...], refs[n + p][...])

        @pl.when(j == 0)
        def _():
            o_ref[...] = d

        @pl.when(j > 0)
        def _():
            o_ref[...] += d

    in_specs = [pl.BlockSpec((tm, Ns), lambda m, j: (m, j)) for _ in pairs]
    in_specs += [pl.BlockSpec((None, K, Ns), lambda m, j, _rb=rs // K: (j, _rb, 0)) for _, rs in pairs]
    return pl.pallas_call(
        body, name=name, grid=(S // tm, N_CHIPS), in_specs=in_specs,
        out_specs=pl.BlockSpec((tm, K), lambda m, j: (m, 0)),
        out_shape=jax.ShapeDtypeStruct((S, K), F32),
        compiler_params=_cparams("parallel", "arbitrary"),
    )(*[dy for dy, _ in pairs], *[wg for _ in pairs])


def _mm_row_t(name, dy, wg, row_start, Ks, out_dtype):
    S, N = dy.shape
    rb = row_start // Ks
    tm = min(512, S)

    def body(dy_ref, w_ref, o_ref):
        o_ref[...] = _dot_nt(dy_ref[...], w_ref[...]).astype(out_dtype)

    return pl.pallas_call(
        body, name=name, grid=(N_CHIPS, S // tm),
        in_specs=[pl.BlockSpec((tm, N), lambda j, m: (m, 0)), pl.BlockSpec((None, Ks, N), lambda j, m: (j, rb, 0))],
        out_specs=pl.BlockSpec((tm, Ks), lambda j, m: (m, j)),
        out_shape=jax.ShapeDtypeStruct((S, N_CHIPS * Ks), out_dtype),
        compiler_params=_cparams("parallel", "parallel"),
    )(dy, wg)


def _mm_dw(name, a, dy, dg, row_start, kind):
    S = a.shape[0]
    _, _, W = dg.shape
    if kind == 'col':
        K = a.shape[1]
        rows = K
        tk, tn = _pick(K, 512), W
        a_map = lambda j, nb, kb: (0, kb)
        dy_map = lambda j, nb, kb: (0, j * (W // tn) + nb)
    else:
        rows = a.shape[1] // N_CHIPS
        tk = rows if rows * S * 2 * 2 <= 12 * 1024 * 1024 else _pick(rows, 512)
        tn = _pick(W, 1024)
        a_map = lambda j, nb, kb: (0, j * (rows // tk) + kb)
        dy_map = lambda j, nb, kb: (0, nb)
    rb = row_start // tk
    assert row_start % tk == 0

    def body(a_ref, dy_ref, dg_in, o_ref):
        del dg_in
        o_ref[...] = _dot_tn(a_ref[...], dy_ref[...]).astype(o_ref.dtype)

    return pl.pallas_call(
        body, name=name, grid=(N_CHIPS, W // tn, rows // tk),
        in_specs=[pl.BlockSpec((S, tk), a_map), pl.BlockSpec((S, tn), dy_map), pl.BlockSpec(memory_space=pl.ANY)],
        out_specs=pl.BlockSpec((None, tk, tn), lambda j, nb, kb: (j, rb + kb, nb)),
        out_shape=jax.ShapeDtypeStruct(dg.shape, dg.dtype),
        input_output_aliases={2: 0},
        compiler_params=_cparams("parallel", "parallel", "parallel"),
    )(a, dy, dg)


def _fox_probs(q, k, c_blk, crow, h, qi, tq):
    n = k.shape[0]
    s = _dot_nt(q, k) * (HEAD_DIM ** -0.5)
    lane = lax.broadcasted_iota(jnp.int32, c_blk.shape, 1)
    ccol = jnp.sum(jnp.where(lane == h, c_blk, 0.0), axis=1, keepdims=True)
    s = s + (ccol - crow)
    t_idx = qi * tq + lax.broadcasted_iota(jnp.int32, (tq, n), 0)
    s_idx = lax.broadcasted_iota(jnp.int32, (tq, n), 1)
    s = jnp.where(s_idx <= t_idx, s, NEG_BIG)
    p = jnp.exp(s - jnp.max(s, axis=1, keepdims=True))
    return p / jnp.sum(p, axis=1, keepdims=True)


def _per_query_block(qi, nq, tq, fn):
    for qv in range(nq):
        @pl.when(qi == qv)
        def _(qv=qv):
            fn(qv, (qv + 1) * tq)


def _fox_fwd(qkv, c, crow, H):
    S = qkv.shape[0]
    tq = min(256, S)

    def body(q_ref, k_ref, v_ref, c_ref, crow_ref, o_ref):
        def block(qv, n):
            p = _fox_probs(q_ref[...], k_ref[0:n, :], c_ref[...], crow_ref[:, 0:n], pl.program_id(0), qv, tq)
            o_ref[...] = jnp.dot(p.astype(BF16), v_ref[0:n, :], preferred_element_type=F32).astype(o_ref.dtype)

        _per_query_block(pl.program_id(1), S // tq, tq, block)

    return pl.pallas_call(
        body, name="fox_attn_fwd", grid=(H, S // tq),
        in_specs=[pl.BlockSpec((tq, HEAD_DIM), lambda h, i: (i, h)),
                  pl.BlockSpec((S, HEAD_DIM), lambda h, i: (0, H + h)),
                  pl.BlockSpec((S, HEAD_DIM), lambda h, i: (0, 2 * H + h)),
                  pl.BlockSpec((tq, H), lambda h, i: (i, 0)),
                  pl.BlockSpec((None, 1, S), lambda h, i: (h, 0, 0))],
        out_specs=pl.BlockSpec((tq, HEAD_DIM), lambda h, i: (i, h)),
        out_shape=jax.ShapeDtypeStruct((S, H * HEAD_DIM), BF16),
        compiler_params=_cparams("parallel", "parallel"),
    )(qkv, qkv, qkv, c, crow)


def _fox_bwd(qkv, c, crow, do, H):
    S = qkv.shape[0]
    tq = min(256, S)
    nq = S // tq

    def body(q_ref, k_ref, v_ref, c_ref, crow_ref, do_ref, dq_ref, dk_ref, dv_ref, dc_ref, dk_acc, dv_acc):
        qi = pl.program_id(1)

        @pl.when(qi == 0)
        def _():
            dk_acc[...] = jnp.zeros_like(dk_acc)
            dv_acc[...] = jnp.zeros_like(dv_acc)
            dc_ref[...] = jnp.zeros_like(dc_ref)

        def block(qv, n):
            q, k, v, do_ = q_ref[...], k_ref[0:n, :], v_ref[0:n, :], do_ref[...]
            p = _fox_probs(q, k, c_ref[...], crow_ref[:, 0:n], pl.program_id(0), qv, tq)
            dv_acc[0:n, :] += _dot_tn(p.astype(BF16), do_)
            dp = _dot_nt(do_, v)
            ds = p * (dp - jnp.sum(p * dp, axis=1, keepdims=True))
            dsb = (ds * (HEAD_DIM ** -0.5)).astype(BF16)
            dq_ref[...] = jnp.dot(dsb, k, preferred_element_type=F32).astype(dq_ref.dtype)
            dk_acc[0:n, :] += _dot_tn(dsb, q)
            dc_ref[:, 0:n] += -jnp.sum(ds, axis=0, keepdims=True)

        _per_query_block(qi, nq, tq, block)

        @pl.when(qi == nq - 1)
        def _():
            dk_ref[...] = dk_acc[...].astype(dk_ref.dtype)
            dv_ref[...] = dv_acc[...].astype(dv_ref.dtype)

    D = H * HEAD_DIM
    return pl.pallas_call(
        body, name="fox_attn_bwd", grid=(H, nq),
        in_specs=[pl.BlockSpec((tq, HEAD_DIM), lambda h, i: (i, h)),
                  pl.BlockSpec((S, HEAD_DIM), lambda h, i: (0, H + h)),
                  pl.BlockSpec((S, HEAD_DIM), lambda h, i: (0, 2 * H + h)),
                  pl.BlockSpec((tq, H), lambda h, i: (i, 0)),
                  pl.BlockSpec((None, 1, S), lambda h, i: (h, 0, 0)),
                  pl.BlockSpec((tq, HEAD_DIM), lambda h, i: (i, h))],
        out_specs=[pl.BlockSpec((tq, HEAD_DIM), lambda h, i: (i, h)),
                   pl.BlockSpec((S, HEAD_DIM), lambda h, i: (0, h)),
                   pl.BlockSpec((S, HEAD_DIM), lambda h, i: (0, h)),
                   pl.BlockSpec((None, 1, S), lambda h, i: (h, 0, 0))],
        out_shape=[jax.ShapeDtypeStruct((S, D), BF16), jax.ShapeDtypeStruct((S, D), BF16),
                   jax.ShapeDtypeStruct((S, D), BF16), jax.ShapeDtypeStruct((H, 1, S), F32)],
        scratch_shapes=[pltpu.VMEM((S, HEAD_DIM), F32), pltpu.VMEM((S, HEAD_DIM), F32)],
        compiler_params=_cparams("parallel", "arbitrary"),
    )(qkv, qkv, qkv, c, crow, do)


def _cumsum_rows(name, xin, reverse):
    S, H = xin.shape
    tb = min(256, S)
    nb = S // tb

    def body(x_ref, o_ref):
        r = lax.broadcasted_iota(jnp.int32, (tb, tb), 0)
        cidx = lax.broadcasted_iota(jnp.int32, (tb, tb), 1)
        tri = (r <= cidx if reverse else r >= cidx).astype(F32)

        def step(b, carry):
            bb = nb - 1 - b if reverse else b
            rows = pl.ds(pl.multiple_of(bb * tb, tb), tb)
            blk = x_ref[rows, :]
            o_ref[rows, :] = jnp.dot(tri, blk, precision=HIGHEST, preferred_element_type=F32) + carry
            return carry + jnp.sum(blk, axis=0, keepdims=True)

        lax.fori_loop(0, nb, step, jnp.zeros((1, H), F32))

    return pl.pallas_call(
        body, name=name, out_shape=jax.ShapeDtypeStruct((S, H), F32),
        in_specs=[pl.BlockSpec(memory_space=pltpu.VMEM)], out_specs=pl.BlockSpec(memory_space=pltpu.VMEM),
        compiler_params=_cparams(),
    )(xin)


def _rel_onehot(i, transposed):
    shp = (REL_TABLE_PAD, BAND_KEYS) if transposed else (BAND_KEYS, REL_TABLE_PAD)
    j = lax.broadcasted_iota(jnp.int32, shp, 1 if transposed else 0)
    r = lax.broadcasted_iota(jnp.int32, shp, 0 if transposed else 1)
    return (jnp.clip(PAD_KEYS + i - j, -REL_CLIP, REL_CLIP) + REL_CLIP == r).astype(F32)


def _rel_expand(rb_pad):
    H = rb_pad.shape[0]

    def body(rb_ref, o_ref):
        def step(i, _):
            o_ref[i] = jnp.dot(rb_ref[...], _rel_onehot(i, True), precision=HIGHEST, preferred_element_type=F32)
            return 0
        lax.fori_loop(0, CHUNK, step, 0)

    return pl.pallas_call(
        body, name="rel_bias_expand", out_shape=jax.ShapeDtypeStruct((CHUNK, H, BAND_KEYS), F32),
        in_specs=[pl.BlockSpec(memory_space=pltpu.VMEM)], out_specs=pl.BlockSpec(memory_space=pltpu.VMEM),
        compiler_params=_cparams(),
    )(rb_pad)


def _rel_reduce(dbt):
    H = dbt.shape[1]

    def body(d_ref, o_ref):
        def step(i, acc):
            return acc + jnp.dot(d_ref[i], _rel_onehot(i, False), precision=HIGHEST, preferred_element_type=F32)
        o_ref[...] = lax.fori_loop(0, CHUNK, step, jnp.zeros((H, REL_TABLE_PAD), F32))

    return pl.pallas_call(
        body, name="rel_bias_reduce", out_shape=jax.ShapeDtypeStruct((H, REL_TABLE_PAD), F32),
        in_specs=[pl.BlockSpec(memory_space=pltpu.VMEM)], out_specs=pl.BlockSpec(memory_space=pltpu.VMEM),
        compiler_params=_cparams(),
    )(dbt)


def _rel_probs(q, kb, bias, n):
    s = _dot_nt(q, kb) * (HEAD_DIM ** -0.5) + bias
    j = lax.broadcasted_iota(jnp.int32, (CHUNK, BAND_KEYS), 1)
    s = jnp.where(j >= PAD_KEYS - n * CHUNK, s, NEG_BIG)
    p = jnp.exp(s - jnp.max(s, axis=1, keepdims=True))
    return p / jnp.sum(p, axis=1, keepdims=True)


def _rel_fwd(qkv, bias, H):
    S = qkv.shape[0]

    def body(q_ref, k_ref, v_ref, b_ref, o_ref, kpad, vpad):
        kpad[0:PAD_KEYS, :] = jnp.zeros((PAD_KEYS, HEAD_DIM), BF16)
        vpad[0:PAD_KEYS, :] = jnp.zeros((PAD_KEYS, HEAD_DIM), BF16)
        kpad[PAD_KEYS:PAD_KEYS + S, :] = k_ref[...]
        vpad[PAD_KEYS:PAD_KEYS + S, :] = v_ref[...]
        bias_t = b_ref[...]

        def chunk(n, _):
            rows = pl.ds(pl.multiple_of(n * CHUNK, CHUNK), CHUNK)
            band = pl.ds(pl.multiple_of(n * CHUNK, CHUNK), BAND_KEYS)
            p = _rel_probs(q_ref[rows, :], kpad[band, :], bias_t, n)
            o_ref[rows, :] = jnp.dot(p.astype(BF16), vpad[band, :], preferred_element_type=F32).astype(o_ref.dtype)
            return 0

        lax.fori_loop(0, S // CHUNK, chunk, 0)

    return pl.pallas_call(
        body, name="rel_attn_fwd", grid=(H,),
        in_specs=[pl.BlockSpec((S, HEAD_DIM), lambda h: (0, h)),
                  pl.BlockSpec((S, HEAD_DIM), lambda h: (0, H + h)),
                  pl.BlockSpec((S, HEAD_DIM), lambda h: (0, 2 * H + h)),
                  pl.BlockSpec((None, CHUNK, BAND_KEYS), lambda h: (h, 0, 0))],
        out_specs=pl.BlockSpec((S, HEAD_DIM), lambda h: (0, h)),
        out_shape=jax.ShapeDtypeStruct((S, H * HEAD_DIM), BF16),
        scratch_shapes=[pltpu.VMEM((S + PAD_KEYS, HEAD_DIM), BF16), pltpu.VMEM((S + PAD_KEYS, HEAD_DIM), BF16)],
        compiler_params=_cparams("parallel"),
    )(qkv, qkv, qkv, bias)


def _rel_bwd(qkv, bias, do, H):
    S = qkv.shape[0]
    D = H * HEAD_DIM

    def body(q_ref, k_ref, v_ref, b_ref, do_ref, dq_ref, dk_ref, dv_ref, db_ref, kpad, vpad, dkpad, dvpad):
        kpad[0:PAD_KEYS, :] = jnp.zeros((PAD_KEYS, HEAD_DIM), BF16)
        vpad[0:PAD_KEYS, :] = jnp.zeros((PAD_KEYS, HEAD_DIM), BF16)
        kpad[PAD_KEYS:PAD_KEYS + S, :] = k_ref[...]
        vpad[PAD_KEYS:PAD_KEYS + S, :] = v_ref[...]
        dkpad[...] = jnp.zeros_like(dkpad)
        dvpad[...] = jnp.zeros_like(dvpad)
        db_ref[...] = jnp.zeros_like(db_ref)
        bias_t = b_ref[...]

        def chunk(n, _):
            rows = pl.ds(pl.multiple_of(n * CHUNK, CHUNK), CHUNK)
            band = pl.ds(pl.multiple_of(n * CHUNK, CHUNK), BAND_KEYS)
            q, kb, vb, do_ = q_ref[rows, :], kpad[band, :], vpad[band, :], do_ref[rows, :]
            p = _rel_probs(q, kb, bias_t, n)
            dvpad[band, :] += _dot_tn(p.astype(BF16), do_)
            dp = _dot_nt(do_, vb)
            ds = p * (dp - jnp.sum(p * dp, axis=1, keepdims=True))
            db_ref[...] += ds
            dsb = (ds * (HEAD_DIM ** -0.5)).astype(BF16)
            dq_ref[rows, :] = jnp.dot(dsb, kb, preferred_element_type=F32).astype(dq_ref.dtype)
            dkpad[band, :] += _dot_tn(dsb, q)
            return 0

        lax.fori_loop(0, S // CHUNK, chunk, 0)
        dk_ref[...] = dkpad[PAD_KEYS:PAD_KEYS + S, :].astype(dk_ref.dtype)
        dv_ref[...] = dvpad[PAD_KEYS:PAD_KEYS + S, :].astype(dv_ref.dtype)

    head = lambda h: (0, h)
    return pl.pallas_call(
        body, name="rel_attn_bwd", grid=(H,),
        in_specs=[pl.BlockSpec((S, HEAD_DIM), head),
                  pl.BlockSpec((S, HEAD_DIM), lambda h: (0, H + h)),
                  pl.BlockSpec((S, HEAD_DIM), lambda h: (0, 2 * H + h)),
                  pl.BlockSpec((None, CHUNK, BAND_KEYS), lambda h: (h, 0, 0)),
                  pl.BlockSpec((S, HEAD_DIM), head)],
        out_specs=[pl.BlockSpec((S, HEAD_DIM), head), pl.BlockSpec((S, HEAD_DIM), head), pl.BlockSpec((S, HEAD_DIM), head),
                   pl.BlockSpec((None, CHUNK, BAND_KEYS), lambda h: (h, 0, 0))],
        out_shape=[jax.ShapeDtypeStruct((S, D), BF16), jax.ShapeDtypeStruct((S, D), BF16), jax.ShapeDtypeStruct((S, D), BF16),
                   jax.ShapeDtypeStruct((H, CHUNK, BAND_KEYS), F32)],
        scratch_shapes=[pltpu.VMEM((S + PAD_KEYS, HEAD_DIM), BF16), pltpu.VMEM((S + PAD_KEYS, HEAD_DIM), BF16),
                        pltpu.VMEM((S + PAD_KEYS, HEAD_DIM), F32), pltpu.VMEM((S + PAD_KEYS, HEAD_DIM), F32)],
        compiler_params=_cparams("parallel"),
    )(qkv, qkv, qkv, bias, do)


def _conv_taps(win, tt, reverse):
    n = tt + 2 * CONV_HALO
    for k in range(CONV_K):
        off = (CONV_K - 1 - k) if reverse else (k - (CONV_K - 1))
        sh = (-off) % n
        rolled = pltpu.roll(win, sh, 0) if sh else win
        yield k, rolled[CONV_HALO:CONV_HALO + tt, :]


def _fill_padded(pad_ref, x_ref, S):
    tc = pad_ref.shape[1]
    pad_ref[0:CONV_HALO, :] = jnp.zeros((CONV_HALO, tc), F32)
    pad_ref[CONV_HALO + S:CONV_HALO + S + CONV_HALO, :] = jnp.zeros((CONV_HALO, tc), F32)
    pad_ref[CONV_HALO:CONV_HALO + S, :] = x_ref[...]


def _dwconv(name, xin, w32, bias, reverse):
    S, D = xin.shape
    tc = min(256, D)
    tt = min(256, S)

    def body(x_ref, w_ref, b_ref, y_ref, pad_ref):
        _fill_padded(pad_ref, x_ref, S)
        def tile(ti, _):
            t0 = pl.multiple_of(ti * tt, tt)
            win = pad_ref[pl.ds(t0, tt + 2 * CONV_HALO), :]
            acc = jnp.zeros((tt, tc), F32) + b_ref[...]
            for k, shifted in _conv_taps(win, tt, reverse):
                acc = acc + w_ref[pl.ds(k, 1), :] * shifted
            y_ref[pl.ds(t0, tt), :] = acc
            return 0

        lax.fori_loop(0, S // tt, tile, 0)

    return pl.pallas_call(
        body, name=name, grid=(D // tc,),
        in_specs=[pl.BlockSpec((S, tc), lambda i: (0, i)), pl.BlockSpec((CONV_HALO, tc), lambda i: (0, i)),
                  pl.BlockSpec((1, tc), lambda i: (0, i))],
        out_specs=pl.BlockSpec((S, tc), lambda i: (0, i)),
        out_shape=jax.ShapeDtypeStruct((S, D), F32),
        scratch_shapes=[pltpu.VMEM((S + 2 * CONV_HALO, tc), F32)],
        compiler_params=_cparams("parallel"),
    )(xin, w32, bias)


def _dwconv_dw(xin, dy):
    S, D = xin.shape
    tc = min(256, D)
    tt = min(256, S)

    def body(x_ref, dy_ref, o_ref, pad_ref):
        _fill_padded(pad_ref, x_ref, S)

        def tile(ti, acc):
            t0 = pl.multiple_of(ti * tt, tt)
            win = pad_ref[pl.ds(t0, tt + 2 * CONV_HALO), :]
            dyt = dy_ref[pl.ds(t0, tt), :]
            ridx = lax.broadcasted_iota(jnp.int32, (CONV_HALO, tc), 0)
            upd = jnp.zeros((CONV_HALO, tc), F32)
            for k, shifted in _conv_taps(win, tt, False):
                upd = jnp.where(ridx == k, jnp.sum(dyt * shifted, axis=0, keepdims=True), upd)
            return acc + upd

        o_ref[...] = lax.fori_loop(0, S // tt, tile, jnp.zeros((CONV_HALO, tc), F32))

    return pl.pallas_call(
        body, name="dwconv_dw", grid=(D // tc,),
        in_specs=[pl.BlockSpec((S, tc), lambda i: (0, i)), pl.BlockSpec((S, tc), lambda i: (0, i))],
        out_specs=pl.BlockSpec((CONV_HALO, tc), lambda i: (0, i)),
        out_shape=jax.ShapeDtypeStruct((CONV_HALO, D), F32),
        scratch_shapes=[pltpu.VMEM((S + 2 * CONV_HALO, tc), F32)],
        compiler_params=_cparams("parallel"),
    )(xin, dy)


def _place():
    x, y, c = lax.axis_index("x"), lax.axis_index("y"), lax.axis_index("c")
    chips = [(1 - x, y), (x, 1 - y), (1 - x, 1 - y)]
    return x, y, c, chips


def _remote(src, dst, ssem, rsem, dev):
    return pltpu.make_async_remote_copy(src_ref=src, dst_ref=dst, send_sem=ssem, recv_sem=rsem,
                                        device_id=dev, device_id_type=MESH_IDS)


_ANY = pl.BlockSpec(memory_space=pl.ANY)


def _place_own(name, own, place):
    R, W = own.shape
    tr = _pick_rows(R)

    def body(p_ref, a_ref, o_ref):
        del p_ref
        o_ref[...] = a_ref[...]

    return pl.pallas_call(
        body, name=name,
        grid_spec=pltpu.PrefetchScalarGridSpec(
            num_scalar_prefetch=1, grid=(R // tr,),
            in_specs=[pl.BlockSpec((tr, W), lambda i, p: (i, 0))],
            out_specs=pl.BlockSpec((None, tr, W), lambda i, p: (p[1], i, 0))),
        out_shape=jax.ShapeDtypeStruct((N_CHIPS, R, W), own.dtype),
        compiler_params=_cparams("parallel"),
    )(place, own)


_HBM = pl.BlockSpec(memory_space=pltpu.HBM)
_SEM = pl.BlockSpec(memory_space=pltpu.SEMAPHORE)
GROUPS = ('qkv', 'ffn', 'pw1', 'dm')


def _half_rows(c, r0, n):
    return pl.ds(pl.multiple_of(r0 + c * (n // 2), SUBLANES), n // 2)


def _gather_start(wgs, layers, after):
    G, L = len(wgs), len(layers)

    def body(*refs):
        outs = refs[G + 1:]
        ssems, rsems, bufs = outs[:L], outs[L:2 * L], outs[2 * L:]
        x, y, c, chips = _place()
        me = 2 * x + y
        for li, pieces in enumerate(layers):
            for pi, (g, r0, n) in enumerate(pieces):
                blk = bufs[g].at[me, _half_rows(c, r0, n)]
                for j, (px, py) in enumerate(chips):
                    _remote(blk, blk, ssems[li].at[3 * pi + j], rsems[li].at[3 * pi + j], (px, py, c)).start()

    sem_shapes = [pltpu.SemaphoreType.DMA((3 * len(p),)) for p in layers]
    res = pl.pallas_call(
        body, name="gather_start", in_specs=[_HBM] * G + [_ANY],
        out_specs=[_SEM] * (2 * L) + [_HBM] * G,
        out_shape=sem_shapes + sem_shapes + [pltpu.HBM(w.shape, w.dtype) for w in wgs],
        input_output_aliases={g: 2 * L + g for g in range(G)},
        compiler_params=pltpu.CompilerParams(has_side_effects=pltpu.SideEffectType.DATAFLOW_SIDE_EFFECTING),
    )(*[pltpu.with_memory_space_constraint(w, pltpu.HBM) for w in wgs], after)
    return res[:L], res[L:2 * L], list(res[2 * L:])


def _gather_wait(name, wgs, ssem, rsem, pieces, after):
    G = len(wgs)

    def body(*refs):
        ssem_ref, rsem_ref = refs[G], refs[G + 1]
        bufs = refs[G + 3:]
        x, y, c, chips = _place()
        me = 2 * x + y
        for pi, (g, r0, n) in enumerate(pieces):
            rows = _half_rows(c, r0, n)
            for j, (px, py) in enumerate(chips):
                cp = _remote(bufs[g].at[me, rows], bufs[g].at[2 * px + py, rows],
                             ssem_ref.at[3 * pi + j], rsem_ref.at[3 * pi + j], (px, py, c))
                cp.wait_send()
                cp.wait_recv()

    return list(pl.pallas_call(
        body, name=name, in_specs=[_HBM] * G + [_SEM, _SEM, _ANY], out_specs=[_HBM] * G,
        out_shape=[pltpu.HBM(w.shape, w.dtype) for w in wgs],
        input_output_aliases={g: g for g in range(G)},
        compiler_params=pltpu.CompilerParams(has_side_effects=pltpu.SideEffectType.DATAFLOW_SIDE_EFFECTING),
    )(*wgs, ssem, rsem, after))


def _gather_forward(name, wgs, pieces):
    G = len(wgs)
    n_cp = 3 * len(pieces)

    def body(*refs):
        bufs, ssems, rsems = refs[G:2 * G], refs[2 * G], refs[2 * G + 1]
        x, y, c, chips = _place()
        sib = (x, y, 1 - c)
        cps = []
        for pi, (g, r0, n) in enumerate(pieces):
            for j, (px, py) in enumerate(chips):
                blk = bufs[g].at[2 * px + py, _half_rows(c, r0, n)]
                cps.append(_remote(blk, blk, ssems.at[3 * pi + j], rsems.at[3 * pi + j], sib))
        for cp in cps:
            cp.start()
        for pi, (g, r0, n) in enumerate(pieces):
            for j, (px, py) in enumerate(chips):
                blk = bufs[g].at[2 * px + py, _half_rows(1 - c, r0, n)]
                _remote(blk, blk, ssems.at[3 * pi + j], rsems.at[3 * pi + j], sib).wait_recv()
        for cp in cps:
            cp.wait_send()

    return list(pl.pallas_call(
        body, name=name, in_specs=[_ANY] * G, out_specs=[_ANY] * G,
        out_shape=[jax.ShapeDtypeStruct(w.shape, w.dtype) for w in wgs],
        input_output_aliases={g: g for g in range(G)},
        scratch_shapes=[pltpu.SemaphoreType.DMA((n_cp,)), pltpu.SemaphoreType.DMA((n_cp,))],
        compiler_params=pltpu.CompilerParams(has_side_effects=True),
    )(*wgs))


def _swap_halves(name, dgs, pieces):
    G = len(dgs)
    n_cp = N_CHIPS * len(pieces)

    def body(*refs):
        srcs, lands, ssems, rsems = refs[:G], refs[G:G + len(pieces)], refs[-2], refs[-1]
        x, y, c, _ = _place()
        cps = [_remote(srcs[g].at[j, _half_rows(1 - c, r0, n)], lands[pi].at[j],
                       ssems.at[N_CHIPS * pi + j], rsems.at[N_CHIPS * pi + j], (x, y, 1 - c))
               for pi, (g, r0, n) in enumerate(pieces) for j in range(N_CHIPS)]
        for cp in cps:
            cp.start()
        for cp in cps:
            cp.wait()

    return list(pl.pallas_call(
        body, name=name, in_specs=[_ANY] * G, out_specs=[_ANY] * len(pieces),
        out_shape=[jax.ShapeDtypeStruct((N_CHIPS, n // 2, dgs[g].shape[2]), dgs[g].dtype) for g, _, n in pieces],
        scratch_shapes=[pltpu.SemaphoreType.DMA((n_cp,)), pltpu.SemaphoreType.DMA((n_cp,))],
        compiler_params=pltpu.CompilerParams(has_side_effects=True),
    )(*dgs))


def _scatter_start(name, pbs):
    P = len(pbs)

    def body(*refs):
        outs = refs[2 * P:]
        ssems, rsems, src, land = outs[0], outs[1], outs[2:2 + P], outs[2 + P:]
        x, y, c, chips = _place()
        me = 2 * x + y
        for pi in range(P):
            for j, (px, py) in enumerate(chips):
                _remote(src[pi].at[2 * px + py], land[pi].at[me], ssems.at[3 * pi + j], rsems.at[3 * pi + j], (px, py, c)).start()

    sems = pltpu.SemaphoreType.DMA((3 * P,))
    hbm = [pltpu.HBM(p.shape, p.dtype) for p in pbs]
    res = pl.pallas_call(
        body, name=name, in_specs=[_HBM] * (2 * P), out_specs=[_SEM, _SEM] + [_HBM] * (2 * P),
        out_shape=[sems, sems] + hbm + hbm,
        input_output_aliases={k: 2 + k for k in range(2 * P)},
        compiler_params=pltpu.CompilerParams(has_side_effects=pltpu.SideEffectType.DATAFLOW_SIDE_EFFECTING),
    )(*[pltpu.with_memory_space_constraint(p, pltpu.HBM) for p in pbs],
      *[pltpu.with_memory_space_constraint(lax.empty(p.shape, p.dtype), pltpu.HBM) for p in pbs])
    return res[0], res[1], list(res[2:2 + P]), list(res[2 + P:])


def _scatter_wait(name, ssem, rsem, pbs, lands, after):
    P = len(pbs)

    def body(*refs):
        ssems, rsems = refs[2 * P], refs[2 * P + 1]
        outs = refs[2 * P + 3:]
        src, land = outs[:P], outs[P:]
        x, y, c, chips = _place()
        for pi in range(P):
            for j, (px, py) in enumerate(chips):
                cp = _remote(src[pi].at[2 * px + py], land[pi].at[2 * px + py], ssems.at[3 * pi + j], rsems.at[3 * pi + j], (px, py, c))
                cp.wait_send()
                cp.wait_recv()

    hbm = [pltpu.HBM(p.shape, p.dtype) for p in pbs]
    res = pl.pallas_call(
        body, name=name, in_specs=[_HBM] * (2 * P) + [_SEM, _SEM, _ANY], out_specs=[_HBM] * (2 * P),
        out_shape=hbm + hbm, input_output_aliases={k: k for k in range(2 * P)},
        compiler_params=pltpu.CompilerParams(has_side_effects=pltpu.SideEffectType.DATAFLOW_SIDE_EFFECTING),
    )(*pbs, *lands, ssem, rsem, after)
    return list(res[:P]), list(res[P:])


def _share_halves(name, gf, pieces):
    def body(in_ref, out, ssems, rsems):
        del in_ref
        x, y, c, _ = _place()
        cps = []
        for pi, (r0, n) in enumerate(pieces):
            mine = out.at[_half_rows(c, r0, n)]
            cps.append(_remote(mine, mine, ssems.at[pi], rsems.at[pi], (x, y, 1 - c)))
        for cp in cps:
            cp.start()
        for pi, (r0, n) in enumerate(pieces):
            theirs = out.at[_half_rows(1 - c, r0, n)]
            _remote(theirs, theirs, ssems.at[pi], rsems.at[pi], (x, y, 1 - c)).wait_recv()
        for cp in cps:
            cp.wait_send()

    return pl.pallas_call(
        body, name=name, in_specs=[_ANY], out_specs=_ANY,
        out_shape=jax.ShapeDtypeStruct(gf.shape, gf.dtype), input_output_aliases={0: 0},
        scratch_shapes=[pltpu.SemaphoreType.DMA((len(pieces),)), pltpu.SemaphoreType.DMA((len(pieces),))],
        compiler_params=pltpu.CompilerParams(has_side_effects=True),
    )(gf)


def _broadcast_small(name, buf, after=None):
    R = buf.shape[0]

    def body(src, *rest):
        out, ssems, rsems = rest[-3:]
        x, y, c, _ = _place()
        me = 4 * x + 2 * y + c
        out[me] = src[...]
        peers = []
        for mask in range(1, 8):
            fx, fy, fc = (mask >> 2) & 1, (mask >> 1) & 1, mask & 1
            peers.append((1 - x if fx else x, 1 - y if fy else y, 1 - c if fc else c))
        cps = [_remote(src, out.at[me], ssems.at[k], rsems.at[k], p) for k, p in enumerate(peers)]
        for cp in cps:
            cp.start()
        for k, (px, py, pc) in enumerate(peers):
            blk = out.at[4 * px + 2 * py + pc]
            _remote(blk, blk, ssems.at[k], rsems.at[k], (px, py, pc)).wait_recv()
        for cp in cps:
            cp.wait_send()

    return pl.pallas_call(
        body, name=name, in_specs=[pl.BlockSpec(memory_space=pltpu.VMEM)] + ([_ANY] if after is not None else []),
        out_specs=pl.BlockSpec(memory_space=pltpu.VMEM),
        out_shape=jax.ShapeDtypeStruct((8, R, LANES), F32),
        scratch_shapes=[pltpu.SemaphoreType.DMA((7,)), pltpu.SemaphoreType.DMA((7,))],
        compiler_params=pltpu.CompilerParams(has_side_effects=True, vmem_limit_bytes=VMEM_LIMIT_V7X),
    )(buf, *([after] if after is not None else []))


def _sum_slabs(name, slabs):
    n, R, _ = slabs.shape

    def body(s_ref, o_ref):
        acc = s_ref[0]
        for k in range(1, n):
            acc = acc + s_ref[k]
        o_ref[...] = acc

    return pl.pallas_call(
        body, name=name, out_shape=jax.ShapeDtypeStruct((R, LANES), F32),
        in_specs=[pl.BlockSpec(memory_space=pltpu.VMEM)], out_specs=pl.BlockSpec(memory_space=pltpu.VMEM),
        compiler_params=_cparams(),
    )(slabs)


def _pick_rows(rows, target=512):
    best = SUBLANES
    for t in range(SUBLANES, min(rows, target) + 1, SUBLANES):
        if rows % t == 0:
            best = t
    return best


def _half_tile(r0, n):
    return _pick_rows(math.gcd(r0, n // 2) if r0 else n // 2)


def _pair_sum(name, dg, land, place, r0, n):
    W = dg.shape[2]
    tr = _half_tile(r0, n)
    nb = (n // 2) // tr

    def body(p_ref, a_ref, b_ref, o_ref):
        del p_ref
        o_ref[...] = (a_ref[...].astype(F32) + b_ref[...].astype(F32)).astype(o_ref.dtype)

    return pl.pallas_call(
        body, name=name,
        grid_spec=pltpu.PrefetchScalarGridSpec(
            num_scalar_prefetch=1, grid=(N_CHIPS, nb),
            in_specs=[pl.BlockSpec((None, tr, W), lambda j, i, p: (j, r0 // tr + p[0] * nb + i, 0)),
                      pl.BlockSpec((None, tr, W), lambda j, i, p: (j, i, 0))],
            out_specs=pl.BlockSpec((None, tr, W), lambda j, i, p: (j, i, 0))),
        out_shape=jax.ShapeDtypeStruct((N_CHIPS, n // 2, W), BF16),
        compiler_params=_cparams("parallel", "parallel"),
    )(place, dg, land)


def _chip_sum(name, pb, land, place, gf, r0, n):
    W = gf.shape[1]
    tr = _half_tile(r0, n)
    nb = (n // 2) // tr

    def body(p_ref, own_ref, lx_ref, ly_ref, ld_ref, gf_in, o_ref):
        del p_ref, gf_in
        o_ref[...] = ((own_ref[...].astype(F32) + lx_ref[...].astype(F32)) + ly_ref[...].astype(F32)) + ld_ref[...].astype(F32)

    slab = lambda flip: pl.BlockSpec((None, tr, W), lambda i, p, _f=flip: (p[1] ^ _f, i, 0))
    return pl.pallas_call(
        body, name=name,
        grid_spec=pltpu.PrefetchScalarGridSpec(
            num_scalar_prefetch=1, grid=(nb,),
            in_specs=[slab(0), slab(2), slab(1), slab(3), pl.BlockSpec(memory_space=pl.ANY)],
            out_specs=pl.BlockSpec((tr, W), lambda i, p: (r0 // tr + p[0] * nb + i, 0))),
        out_shape=jax.ShapeDtypeStruct(gf.shape, F32),
        input_output_aliases={5: 0},
        compiler_params=_cparams("parallel"),
    )(place, pb, land, land, land, gf)


def _ln_stats(z):
    mu = jnp.mean(z, axis=1, keepdims=True)
    zc = z - mu
    rstd = lax.rsqrt(jnp.mean(zc * zc, axis=1, keepdims=True) + LN_EPS)
    return zc * rstd, rstd


def _ln_fwd(name, xin, m, g, b):
    S, D = xin.shape

    def fn(x_, m_, g_, b_):
        xhat, rstd = _ln_stats(ALPHA * x_ + m_)
        y = xhat * g_ + b_
        return y, y, xhat, rstd

    return _rowwise(name, fn, [xin, m, ('full', g), ('full', b)],
                    [('rows', D, F32), ('rows', D, BF16), ('rows', D, F32), ('rows', 1, F32)], S)


def _ln_bwd_core(dy, xhat, rstd, g):
    dxh = dy * g
    return rstd * (dxh - jnp.mean(dxh, axis=1, keepdims=True) - xhat * jnp.mean(dxh * xhat, axis=1, keepdims=True))


def _ln_bwd(name, terms, xhat, rstd, g, after=None):
    S, D = xhat.shape
    scales = [s for _, s in terms]
    n = len(terms)

    def fn(*v):
        dy = v[0] * scales[0] if scales[0] != 1.0 else v[0]
        for t in range(1, n):
            dy = dy + (v[t] * scales[t] if scales[t] != 1.0 else v[t])
        xh, rs, g_ = v[n], v[n + 1], v[n + 2]
        dz = _ln_bwd_core(dy, xh, rs, g_)
        return dz, dz, jnp.sum(dy * xh, axis=0, keepdims=True), jnp.sum(dy, axis=0, keepdims=True)

    return _rowwise(name, fn, [a for a, _ in terms] + [xhat, rstd, ('full', g)] + ([('after', after)] if after is not None else []),
                    [('rows', D, F32), ('rows', D, BF16), ('acc', (1, D), F32), ('acc', (1, D), F32)], S)


def _adamw_math(w, g, m, v):
    m2 = ADAM_B1 * m + (1.0 - ADAM_B1) * g
    v2 = ADAM_B2 * v + (1.0 - ADAM_B2) * (g * g)
    m_hat = m2 / (1.0 - ADAM_B1 ** ADAM_STEP)
    v_hat = v2 / (1.0 - ADAM_B2 ** ADAM_STEP)
    delta = -ADAM_LR * (m_hat / (jnp.sqrt(v_hat) + ADAM_EPS) + ADAM_WD * w)
    return delta, m2, v2


def _adamw(name, w, gfull, row_start, m, v):
    rows, W = w.shape
    tr = math.gcd(math.gcd(rows, row_start), 256) if row_start else math.gcd(rows, 256)

    def fn(w_, g_, m_, v_):
        d, m2, v2 = _adamw_math(w_, g_, m_, v_)
        return g_, d, m2, v2

    return _rowwise(name, fn, [w, ('off', gfull, row_start // tr), m, v], [('rows', W, F32)] * 4, rows, tm=tr)


def _pack(arrs):
    flat = jnp.concatenate([a.reshape(-1).astype(F32) for a in arrs])
    tile = SUBLANES * LANES
    n = -(-flat.shape[0] // tile) * tile
    return jnp.pad(flat, (0, n - flat.shape[0])).reshape(-1, LANES)


def _unpack(buf, shapes):
    flat = buf.reshape(-1)
    out, pos = [], 0
    for shp in shapes:
        n = math.prod(shp)
        out.append(flat[pos:pos + n].reshape(shp))
        pos += n
    return out


BIG = ['fox_w_qkv', 'fox_w_o', 'rel_w_qkv', 'rel_w_o', 'conv_w_pw1', 'conv_w_pw2', 'ffn_w_gate', 'ffn_w_up', 'ffn_w_down']
SMALL_SHARDED = ['fox_w_f', 'conv_b_pw1', 'conv_w_dw', 'conv_b_dw', 'conv_ln_g', 'conv_ln_b', 'conv_b_pw2']
SMALL_SHARD_AXIS = {'fox_w_f': 1, 'conv_b_pw1': 1, 'conv_w_dw': 2, 'conv_b_dw': 1, 'conv_ln_g': 1, 'conv_ln_b': 1, 'conv_b_pw2': 1}
SMALL_REPL = ['fox_b_f', 'rel_bias', 'ln_mix_g', 'ln_mix_b', 'ln_ffn_g', 'ln_ffn_b']
SMALL = SMALL_SHARDED + SMALL_REPL
WEIGHTS = ['fox_w_qkv', 'fox_w_f', 'fox_b_f', 'fox_w_o', 'rel_w_qkv', 'rel_bias', 'rel_w_o', 'conv_w_pw1', 'conv_b_pw1',
           'conv_w_dw', 'conv_b_dw', 'conv_ln_g', 'conv_ln_b', 'conv_w_pw2', 'conv_b_pw2', 'ffn_w_gate', 'ffn_w_up',
           'ffn_w_down', 'ln_mix_g', 'ln_mix_b', 'ln_ffn_g', 'ln_ffn_b']


def kernel(x, fox_w_qkv, fox_w_f, fox_b_f, fox_w_o, rel_w_qkv, rel_bias, rel_w_o, conv_w_pw1, conv_b_pw1, conv_w_dw, conv_b_dw, conv_ln_g, conv_ln_b, conv_w_pw2, conv_b_pw2, ffn_w_gate, ffn_w_up, ffn_w_down, ln_mix_g, ln_mix_b, ln_ffn_g, ln_ffn_b, loss_target, m_fox_w_qkv, m_fox_w_f, m_fox_b_f, m_fox_w_o, m_rel_w_qkv, m_rel_bias, m_rel_w_o, m_conv_w_pw1, m_conv_b_pw1, m_conv_w_dw, m_conv_b_dw, m_conv_ln_g, m_conv_ln_b, m_conv_w_pw2, m_conv_b_pw2, m_ffn_w_gate, m_ffn_w_up, m_ffn_w_down, m_ln_mix_g, m_ln_mix_b, m_ln_ffn_g, m_ln_ffn_b, v_fox_w_qkv, v_fox_w_f, v_fox_b_f, v_fox_w_o, v_rel_w_qkv, v_rel_bias, v_rel_w_o, v_conv_w_pw1, v_conv_b_pw1, v_conv_w_dw, v_conv_b_dw, v_conv_ln_g, v_conv_ln_b, v_conv_w_pw2, v_conv_b_pw2, v_ffn_w_gate, v_ffn_w_up, v_ffn_w_down, v_ln_mix_g, v_ln_mix_b, v_ln_ffn_g, v_ln_ffn_b):
    A = dict(locals())
    Wt = {n: A[n] for n in WEIGHTS}
    Mo = {n: A['m_' + n] for n in WEIGHTS}
    Vo = {n: A['v_' + n] for n in WEIGHTS}

    _, S, D = x.shape
    H = D // HEAD_DIM
    Ds = D // N_CHIPS
    Nq = fox_w_qkv.shape[2]
    Np = conv_w_pw1.shape[2]
    Fs = ffn_w_gate.shape[2]
    my_x, my_y, my_c = lax.axis_index("x"), lax.axis_index("y"), lax.axis_index("c")
    my_chip = 2 * my_x + my_y
    place = jnp.stack([my_c, my_chip]).astype(jnp.int32)

    wo_base = DEPTH * Fs
    where = {
        'fox_w_qkv': ('qkv', 0), 'rel_w_qkv': ('qkv', N_FOX * D),
        'ffn_w_gate': ('ffn', 0), 'ffn_w_up': ('ffn', DEPTH * D),
        'conv_w_pw1': ('pw1', 0),
        'ffn_w_down': ('dm', 0), 'fox_w_o': ('dm', wo_base), 'rel_w_o': ('dm', wo_base + N_FOX * Ds),
        'conv_w_pw2': ('dm', wo_base + (N_FOX + 1) * Ds),
    }
    members = {'qkv': ['fox_w_qkv', 'rel_w_qkv'], 'ffn': ['ffn_w_gate', 'ffn_w_up'], 'pw1': ['conv_w_pw1'],
               'dm': ['ffn_w_down', 'fox_w_o', 'rel_w_o', 'conv_w_pw2']}
    flat2 = lambda a: a.reshape(-1, a.shape[-1])
    own = {g: jnp.concatenate([flat2(Wt[n]).astype(BF16) for n in ms], axis=0) for g, ms in members.items()}

    def layer_pieces(i):
        kind, j = i % 3, i // 3
        slot = j if kind == 0 else (N_FOX if kind == 1 else N_FOX + 1)
        w_in = (GROUPS.index('pw1'), 0, D) if kind == 2 else (GROUPS.index('qkv'), slot * D, D)
        return [w_in, (GROUPS.index('dm'), wo_base + slot * Ds, Ds), (GROUPS.index('ffn'), i * D, D),
                (GROUPS.index('ffn'), (DEPTH + i) * D, D), (GROUPS.index('dm'), i * Fs, Fs)]

    small_shapes = [Wt[n].shape for n in SMALL_SHARDED]
    slabs = _broadcast_small("gather_small", _pack([Wt[n] for n in SMALL_SHARDED]))

    stages = [part for i in range(DEPTH) for part in (layer_pieces(i)[:2], layer_pieces(i)[2:])]
    gather_ssems, gather_rsems, wg_list = _gather_start(
        [_place_own("place_" + g, own[g], place) for g in GROUPS], stages, slabs)
    WG = dict(zip(GROUPS, wg_list))
    DG = {g: lax.empty(WG[g].shape, BF16) for g in own}

    per_chip = [_unpack(slabs[2 * j], small_shapes) for j in range(N_CHIPS)]
    full = {n: jnp.concatenate([per_chip[j][i] for j in range(N_CHIPS)], axis=SMALL_SHARD_AXIS[n])
            for i, n in enumerate(SMALL_SHARDED)}
    row = lambda v: v.reshape(1, -1)

    SG = {}

    def ffn_fwd(i, xb):
        hg = _mm_col(f"ffn{i}_gate", xb, WG['ffn'], i * D)
        hu = _mm_col(f"ffn{i}_up", xb, WG['ffn'], (DEPTH + i) * D)
        act, = _rowwise(f"ffn{i}_act", lambda g_, u_: [g_.astype(F32) * _sigmoid(g_.astype(F32)) * u_.astype(F32)],
                        [hg, hu], [('rows', N_CHIPS * Fs, BF16)], S)
        f = _mm_row(f"ffn{i}_down", act, WG['dm'], i * Fs, Fs)
        return f, (hg, hu, act)

    def ffn_bwd(i, xb, saved, dzb):
        hg, hu, act = saved
        DG['dm'] = _mm_dw(f"ffn{i}_dw_down", act, dzb, DG['dm'], i * Fs, 'row')
        dact = _mm_row_t(f"ffn{i}_dact", dzb, WG['dm'], i * Fs, Fs, F32)

        def fn(da_, g_, u_):
            g32, u32 = g_.astype(F32), u_.astype(F32)
            sg = _sigmoid(g32)
            silu = g32 * sg
            return da_ * u32 * (sg * (1.0 + g32 * (1.0 - sg))), da_ * silu

        dhg, dhu = _rowwise(f"ffn{i}_dact_split", fn, [dact, hg, hu],
                            [('rows', N_CHIPS * Fs, BF16), ('rows', N_CHIPS * Fs, BF16)], S)
        DG['ffn'] = _mm_dw(f"ffn{i}_dw_gate", xb, dhg, DG['ffn'], i * D, 'col')
        DG['ffn'] = _mm_dw(f"ffn{i}_dw_up", xb, dhu, DG['ffn'], (DEPTH + i) * D, 'col')
        return _mm_col_t(f"ffn{i}_dx", [(dhg, i * D), (dhu, (DEPTH + i) * D)], WG['ffn'], D)

    def fox_fwd(j, xb):
        qkv = _mm_col(f"fox{j}_qkv", xb, WG['qkv'], j * D)
        wf = full['fox_w_f'][j].astype(BF16)
        def gate_fn(x_, w_, b_):
            z_ = jnp.dot(x_, w_, preferred_element_type=F32) + b_
            return z_, jnp.minimum(z_, 0.0) - jnp.log(1.0 + jnp.exp(-jnp.abs(z_)))

        z, logf = _rowwise(f"fox{j}_gate", gate_fn, [xb, ('full', wf), ('full', row(fox_b_f[j]))],
                           [('rows', H, F32), ('rows', H, F32)], S)
        c = _cumsum_rows(f"fox{j}_cumsum", logf, False)
        crow = c.T.reshape(H, 1, S)
        o = _fox_fwd(qkv, c, crow, H)
        m = _mm_row(f"fox{j}_wo", o, WG['dm'], wo_base + j * Ds, Ds)
        return m, (qkv, z, c, crow, o, wf)

    def fox_bwd(j, xb, saved, dzb):
        qkv, z, c, crow, o, wf = saved
        DG['dm'] = _mm_dw(f"fox{j}_dw_o", o, dzb, DG['dm'], wo_base + j * Ds, 'row')
        do = _mm_row_t(f"fox{j}_do", dzb, WG['dm'], wo_base + j * Ds, Ds, BF16)
        dq, dk, dv, dcrow = _fox_bwd(qkv, c, crow, do, H)
        dqkv = jnp.concatenate([dq, dk, dv], axis=1)
        dlogf = _cumsum_rows(f"fox{j}_rcumsum", dcrow.reshape(H, S).T, True)

        def fn(x_, dl_, z_, w_):
            dz_ = dl_ * _sigmoid(-z_)
            dzb_ = dz_.astype(BF16)
            return _dot_nt(dzb_, w_), _dot_tn(x_, dzb_), jnp.sum(dz_, axis=0, keepdims=True)

        dh_f, dwf, dbf = _rowwise(f"fox{j}_gate_bwd", fn, [xb, dlogf, z, ('full', wf)],
                                  [('rows', D, F32), ('acc', (D, H), F32), ('acc', (1, H), F32)], S)
        SG.setdefault('fox_w_f', [None] * N_FOX)[j] = dwf
        SG.setdefault('fox_b_f', [None] * N_FOX)[j] = dbf.reshape(H)
        DG['qkv'] = _mm_dw(f"fox{j}_dw_qkv", xb, dqkv, DG['qkv'], j * D, 'col')
        dh = _mm_col_t(f"fox{j}_dx", [(dqkv, j * D)], WG['qkv'], D)
        return [dh, dh_f]

    def rel_fwd(xb):
        qkv = _mm_col("rel_qkv", xb, WG['qkv'], N_FOX * D)
        rb_pad = jnp.pad(rel_bias[0], ((0, 0), (0, REL_TABLE_PAD - REL_TABLE)))
        bias = jnp.transpose(_rel_expand(rb_pad), (1, 0, 2))
        o = _rel_fwd(qkv, bias, H)
        m = _mm_row("rel_wo", o, WG['dm'], wo_base + N_FOX * Ds, Ds)
        return m, (qkv, bias, o)

    def rel_bwd(xb, saved, dzb):
        qkv, bias, o = saved
        DG['dm'] = _mm_dw("rel_dw_o", o, dzb, DG['dm'], wo_base + N_FOX * Ds, 'row')
        do = _mm_row_t("rel_do", dzb, WG['dm'], wo_base + N_FOX * Ds, Ds, BF16)
        dq, dk, dv, dbias = _rel_bwd(qkv, bias, do, H)
        SG['rel_bias'] = _rel_reduce(jnp.transpose(dbias, (1, 0, 2)))[:, :REL_TABLE].reshape(1, H, REL_TABLE)
        dqkv = jnp.concatenate([dq, dk, dv], axis=1)
        DG['qkv'] = _mm_dw("rel_dw_qkv", xb, dqkv, DG['qkv'], N_FOX * D, 'col')
        return [_mm_col_t("rel_dx", [(dqkv, N_FOX * D)], WG['qkv'], D)]

    w_dw32 = jnp.pad(full['conv_w_dw'][0], ((0, CONV_HALO - CONV_K), (0, 0)))
    cg, cb = full['conv_ln_g'], full['conv_ln_b']

    def conv_fwd(xb):
        u = _mm_col("conv_pw1", xb, WG['pw1'], 0, bias=full['conv_b_pw1'], out_dtype=F32)
        u2, = _rowwise("conv_glu", lambda a_, g_: [a_ * _sigmoid(g_)],
                       [('cols', u, D, 0), ('cols', u, D, 1)], [('rows', D, F32)], S)
        yc = _dwconv("conv_dw", u2, w_dw32, full['conv_b_dw'], False)

        def fn(y_, g_, b_):
            xhat, rstd = _ln_stats(y_)
            ln = xhat * g_ + b_
            return ln * _sigmoid(ln), xhat, rstd

        zc, xhat, rstd = _rowwise("conv_ln_silu", fn, [yc, ('full', cg), ('full', cb)],
                                  [('rows', D, BF16), ('rows', D, F32), ('rows', 1, F32)], S)
        m = _mm_row("conv_pw2", zc, WG['dm'], wo_base + (N_FOX + 1) * Ds, Ds, bias=full['conv_b_pw2'])
        return m, (u, u2, zc, xhat, rstd)

    def conv_bwd(xb, saved, dz, dzb):
        u, u2, zc, xhat, rstd = saved
        r0 = wo_base + (N_FOX + 1) * Ds
        DG['dm'] = _mm_dw("conv_dw_pw2", zc, dzb, DG['dm'], r0, 'row')
        dzc = _mm_row_t("conv_dzc", dzb, WG['dm'], r0, Ds, F32)

        def fn(dm_, dzc_, xh_, rs_, g_, b_):
            ln = xh_ * g_ + b_
            sg = _sigmoid(ln)
            dln = dzc_ * (sg * (1.0 + ln * (1.0 - sg)))
            dyc = _ln_bwd_core(dln, xh_, rs_, g_)
            col = lambda t: jnp.sum(t, axis=0, keepdims=True)
            return dyc, col(dm_), col(dln * xh_), col(dln), col(dyc)

        dyc, SG['conv_b_pw2'], SG['conv_ln_g'], SG['conv_ln_b'], SG['conv_b_dw'] = _rowwise(
            "conv_ln_silu_bwd", fn, [dz, dzc, xhat, rstd, ('full', cg), ('full', cb)],
            [('rows', D, F32)] + [('acc', (1, D), F32)] * 4, S)
        du2 = _dwconv("conv_dw_bwd_x", dyc, w_dw32, jnp.zeros((1, D), F32), True)
        SG['conv_w_dw'] = _dwconv_dw(u2, dyc)[:CONV_K].reshape(1, CONV_K, D)

        def fn2(du2_, a_, g_):
            sg = _sigmoid(g_)
            da, dgt = du2_ * sg, du2_ * a_ * sg * (1.0 - sg)
            return da, dgt, jnp.sum(da, axis=0, keepdims=True), jnp.sum(dgt, axis=0, keepdims=True)

        da, dgt, dba, dbg = _rowwise("conv_glu_bwd", fn2, [du2, ('cols', u, D, 0), ('cols', u, D, 1)],
                                     [('rows', D, BF16), ('rows', D, BF16), ('acc', (1, D), F32), ('acc', (1, D), F32)], S)
        SG['conv_b_pw1'] = jnp.concatenate([dba, dbg], axis=1)
        du = jnp.concatenate([da, dgt], axis=1)
        DG['pw1'] = _mm_dw("conv_dw_pw1", xb, du, DG['pw1'], 0, 'col')
        return [_mm_col_t("conv_dx", [(du, 0)], WG['pw1'], D)]

    xs = x[0]
    xs_b = xs.astype(BF16)
    tape = []

    def weights_ready(s, after):
        bufs = _gather_wait(f"gather_wait{s}", [WG[g] for g in GROUPS], gather_ssems[s], gather_rsems[s], stages[s], after)
        WG.update(zip(GROUPS, _gather_forward(f"gather_fwd{s}", bufs, stages[s])))

    for i in range(DEPTH):
        kind, j = i % 3, i // 3
        weights_ready(2 * i, xs)
        if kind == 0:
            m, msaved = fox_fwd(j, xs_b)
        elif kind == 1:
            m, msaved = rel_fwd(xs_b)
        else:
            m, msaved = conv_fwd(xs_b)
        xm, xm_b, xhat1, rstd1 = _ln_fwd(f"ln_mix{i}", xs, m, row(ln_mix_g[i]), row(ln_mix_b[i]))
        weights_ready(2 * i + 1, xm)
        f, fsaved = ffn_fwd(i, xm_b)
        xo, xo_b, xhat2, rstd2 = _ln_fwd(f"ln_ffn{i}", xm, f, row(ln_ffn_g[i]), row(ln_ffn_b[i]))
        tape.append((xs_b, msaved, xhat1, rstd1, xm_b, fsaved, xhat2, rstd2))
        xs, xs_b = xo, xo_b

    def loss_fn(y_, t_):
        e = y_ - t_
        return e * (1.0 / D), jnp.sum(e * e, axis=0, keepdims=True)

    dy, sq = _rowwise("loss", loss_fn, [xs, loss_target[0]], [('rows', D, F32), ('acc', (1, D), F32)], S)
    loss = lax.psum(jnp.sum(sq) * (0.5 / D), ("x", "y", "c"))

    GF = {g: lax.empty(WG[g].shape[1:], F32) for g in GROUPS}
    started = [None] * len(stages)

    def reduce_start(s):
        dgs = [DG[g] for g in GROUPS]
        lands = _swap_halves(f"pair_swap{s}", dgs, stages[s])
        pbs = [_pair_sum(f"pair_sum{s}_{pi}", dgs[g], lands[pi], place, r0, n) for pi, (g, r0, n) in enumerate(stages[s])]
        started[s] = _scatter_start(f"scatter_start{s}", pbs)
        token = started[s][2][0]
        if s + 1 < len(stages):
            reduce_finish(s + 1, token)
        return token

    def reduce_finish(s, after):
        ssem, rsem, pbs, lands2 = started[s]
        pbs, lands2 = _scatter_wait(f"scatter_wait{s}", ssem, rsem, pbs, lands2, after)
        for pi, (g, r0, n) in enumerate(stages[s]):
            GF[GROUPS[g]] = _chip_sum(f"chip_sum{s}_{pi}", pbs[pi], lands2[pi], place, GF[GROUPS[g]], r0, n)

    terms = [(dy, 1.0)]
    token = None
    g_mix, b_mix, g_ffn, b_ffn = [None] * DEPTH, [None] * DEPTH, [None] * DEPTH, [None] * DEPTH
    for i in reversed(range(DEPTH)):
        kind, j = i % 3, i // 3
        xin_b, msaved, xhat1, rstd1, xm_b, fsaved, xhat2, rstd2 = tape[i]
        dz2, dz2b, g_ffn[i], b_ffn[i] = _ln_bwd(f"ln_ffn{i}_bwd", terms, xhat2, rstd2, row(ln_ffn_g[i]), after=token)
        dx_ffn = ffn_bwd(i, xm_b, fsaved, dz2b)
        token = reduce_start(2 * i + 1)
        dz1, dz1b, g_mix[i], b_mix[i] = _ln_bwd(f"ln_mix{i}_bwd", [(dz2, ALPHA), (dx_ffn, 1.0)], xhat1, rstd1,
                                                row(ln_mix_g[i]), after=token)
        if kind == 0:
            mix_terms = fox_bwd(j, xin_b, msaved, dz1b)
        elif kind == 1:
            mix_terms = rel_bwd(xin_b, msaved, dz1b)
        else:
            mix_terms = conv_bwd(xin_b, msaved, dz1, dz1b)
        terms = [(dz1, ALPHA)] + [(t, 1.0) for t in mix_terms]
        token = reduce_start(2 * i)

    def gx_fn(*v):
        acc = v[0] * ALPHA
        for t in v[1:]:
            acc = acc + t
        return [acc]

    grad_x, = _rowwise("grad_x", gx_fn, [a for a, _ in terms], [('rows', D, F32)], S)
    grad_x = grad_x.reshape(1, S, D)

    SG['fox_w_f'] = jnp.stack(SG['fox_w_f'])
    SG['fox_b_f'] = jnp.stack(SG['fox_b_f'])
    SG['ln_mix_g'] = jnp.concatenate(g_mix, axis=0)
    SG['ln_mix_b'] = jnp.concatenate(b_mix, axis=0)
    SG['ln_ffn_g'] = jnp.concatenate(g_ffn, axis=0)
    SG['ln_ffn_b'] = jnp.concatenate(b_ffn, axis=0)

    grads, deltas, new_m, new_v = {}, {}, {}, {}

    reduce_finish(0, grad_x)
    for gi, g in enumerate(GROUPS):
        GF[g] = _share_halves("pair_share_" + g, GF[g], [(r0, n) for st in stages for (pg, r0, n) in st if pg == gi])

    for n in BIG:
        g, r0 = where[n]
        outs = _adamw("adamw_" + n, flat2(Wt[n]), GF[g], r0, flat2(Mo[n]), flat2(Vo[n]))
        grads[n], deltas[n], new_m[n], new_v[n] = [o.reshape(Wt[n].shape) for o in outs]

    full_shapes = [SG[n].shape for n in SMALL]
    summed = _sum_slabs("small_sum", _broadcast_small("small_exchange", _pack([SG[n] for n in SMALL]), after=token))
    gsm = dict(zip(SMALL, _unpack(summed, full_shapes)))
    for n in SMALL_SHARDED:
        ax = SMALL_SHARD_AXIS[n]
        width = Wt[n].shape[ax]
        gsm[n] = lax.dynamic_slice_in_dim(gsm[n], my_chip * width, width, axis=ax)
    own_shapes = [Wt[n].shape for n in SMALL]
    packed = [_pack([src[n] for n in SMALL]) for src in (Wt, gsm, Mo, Vo)]
    rows_small = packed[0].shape[0]

    def small_fn(w_, g_, m_, v_):
        return _adamw_math(w_, g_, m_, v_)

    sd, sm, sv = _rowwise("adamw_small", small_fn, packed, [('rows', LANES, F32)] * 3, rows_small, tm=rows_small)
    for n, d_, m_, v_ in zip(SMALL, _unpack(sd, own_shapes), _unpack(sm, own_shapes), _unpack(sv, own_shapes)):
        grads[n], deltas[n], new_m[n], new_v[n] = gsm[n], d_, m_, v_

    return (loss, grad_x, *[grads[n] for n in WEIGHTS], *[deltas[n] for n in WEIGHTS],
            *[new_m[n] for n in WEIGHTS], *[new_v[n] for n in WEIGHTS])
```

```python
import functools
import math

import jax
import jax.numpy as jnp
from jax import lax
from jax.experimental import pallas as pl
from jax.experimental.pallas import tpu as pltpu

F32 = jnp.float32
BF16 = jnp.bfloat16
MESH_IDS = pl.DeviceIdType.MESH
HIGHEST = lax.Precision.HIGHEST

N_CHIPS = 4
DEPTH = 4
N_FOX = 2
HEAD_DIM = 128
CHUNK = 64
LEFT_CHUNKS = 8
BAND_KEYS = (LEFT_CHUNKS + 1) * CHUNK
PAD_KEYS = LEFT_CHUNKS * CHUNK
REL_CLIP = 128
REL_TABLE = 2 * REL_CLIP + 1
REL_TABLE_PAD = 384
REL_QB = 4 * CHUNK
REL_WIN = REL_QB + PAD_KEYS
CONV_K = 31
CONV_HALO = 32
ALPHA = (2.0 * DEPTH) ** 0.25
LN_EPS = 1e-5
ADAM_LR, ADAM_B1, ADAM_B2, ADAM_EPS, ADAM_WD, ADAM_STEP = 0.001, 0.9, 0.999, 1e-08, 0.01, 10
NEG_BIG = -1e30
VMEM_LIMIT_V7X = 56 * 1024 * 1024
LANES = 128
SUBLANES = 8


def _cparams(*sem):
    return pltpu.CompilerParams(dimension_semantics=sem if sem else None, vmem_limit_bytes=VMEM_LIMIT_V7X)


def _pick(dim, target):
    best = None
    for t in range(LANES, min(dim, target) + 1, LANES):
        if dim % t == 0:
            best = t
    return best if best is not None else dim


def _dot_nt(a, b):
    return lax.dot_general(a, b, (((1,), (1,)), ((), ())), preferred_element_type=F32)


def _dot_tn(a, b):
    return lax.dot_general(a, b, (((0,), (0,)), ((), ())), preferred_element_type=F32)


def _sigmoid(z):
    return 1.0 / (1.0 + jnp.exp(-z))


def _rowwise(name, fn, ins, outs, S, tm=256):
    tm = min(tm, S)
    afters = [it[1] for it in ins if isinstance(it, tuple) and it[0] == 'after']
    ins = [it for it in ins if not (isinstance(it, tuple) and it[0] == 'after')]
    arrs, in_specs = [], []
    for it in ins:
        if isinstance(it, tuple) and it[0] == 'full':
            a = it[1]
            in_specs.append(pl.BlockSpec(a.shape, lambda i, _n=a.ndim: (0,) * _n))
        elif isinstance(it, tuple) and it[0] == 'cols':
            _, a, width, blk = it
            in_specs.append(pl.BlockSpec((tm, width), lambda i, _b=blk: (i, _b)))
        elif isinstance(it, tuple) and it[0] == 'off':
            _, a, off = it
            in_specs.append(pl.BlockSpec((tm, a.shape[1]), lambda i, _o=off: (i + _o, 0)))
        else:
            a = it
            in_specs.append(pl.BlockSpec((tm, a.shape[1]), lambda i: (i, 0)))
        arrs.append(a)
    out_shape, out_specs = [], []
    for kind, shp, dt in outs:
        if kind == 'rows':
            out_shape.append(jax.ShapeDtypeStruct((S, shp), dt))
            out_specs.append(pl.BlockSpec((tm, shp), lambda i: (i, 0)))
        else:
            out_shape.append(jax.ShapeDtypeStruct(shp, dt))
            out_specs.append(pl.BlockSpec(shp, lambda i, _n=len(shp): (0,) * _n))
    n_in = len(arrs)
    in_specs += [pl.BlockSpec(memory_space=pl.ANY)] * len(afters)

    def body(*refs):
        vals = fn(*[r[...] for r in refs[:n_in]])
        first = pl.program_id(0) == 0
        for (kind, _, _), r, v in zip(outs, refs[n_in + len(afters):], vals):
            if kind == 'rows':
                r[...] = v.astype(r.dtype)
            else:
                @pl.when(first)
                def _(r=r, v=v):
                    r[...] = v.astype(r.dtype)

                @pl.when(jnp.logical_not(first))
                def _(r=r, v=v):
                    r[...] += v.astype(r.dtype)

    has_acc = any(k != 'rows' for k, _, _ in outs)
    res = pl.pallas_call(
        body, name=name, grid=(S // tm,), in_specs=in_specs, out_specs=out_specs, out_shape=out_shape,
        compiler_params=_cparams("arbitrary" if has_acc else "parallel"),
    )(*arrs, *afters)
    return res


def _mm_col(name, a, wg, row_start, bias=None, out_dtype=BF16):
    S, K = a.shape
    _, _, Ns = wg.shape
    rb = row_start // K
    tm = min(512, S)

    def body(a_ref, w_ref, *rest):
        acc = jnp.dot(a_ref[...].astype(BF16), w_ref[...], preferred_element_type=F32)
        if bias is not None:
            acc = acc + rest[0][...]
        rest[-1][...] = acc.astype(out_dtype)

    in_specs = [pl.BlockSpec((tm, K), lambda j, m: (m, 0)), pl.BlockSpec((None, K, Ns), lambda j, m: (j, rb, 0))]
    args = [a, wg]
    if bias is not None:
        in_specs.append(pl.BlockSpec((1, Ns), lambda j, m: (0, j)))
        args.append(bias)
    return pl.pallas_call(
        body, name=name, grid=(N_CHIPS, S // tm), in_specs=in_specs,
        out_specs=pl.BlockSpec((tm, Ns), lambda j, m: (m, j)),
        out_shape=jax.ShapeDtypeStruct((S, N_CHIPS * Ns), out_dtype),
        compiler_params=_cparams("parallel", "parallel"),
    )(*args)


def _mm_row(name, a, wg, row_start, Ks, bias=None):
    S = a.shape[0]
    N = wg.shape[2]
    rb = row_start // Ks
    tm = min(512, S)

    def body(a_ref, w_ref, *rest):
        o_ref = rest[-1]
        j = pl.program_id(1)
        d = jnp.dot(a_ref[...].astype(BF16), w_ref[...], preferred_element_type=F32)

        @pl.when(j == 0)
        def _():
            o_ref[...] = d + rest[0][...] if bias is not None else d

        @pl.when(j > 0)
        def _():
            o_ref[...] += d

    in_specs = [pl.BlockSpec((tm, Ks), lambda m, j: (m, j)), pl.BlockSpec((None, Ks, N), lambda m, j: (j, rb, 0))]
    args = [a, wg]
    if bias is not None:
        in_specs.append(pl.BlockSpec((1, N), lambda m, j: (0, 0)))
        args.append(bias)
    return pl.pallas_call(
        body, name=name, grid=(S // tm, N_CHIPS), in_specs=in_specs,
        out_specs=pl.BlockSpec((tm, N), lambda m, j: (m, 0)),
        out_shape=jax.ShapeDtypeStruct((S, N), F32),
        compiler_params=_cparams("parallel", "arbitrary"),
    )(*args)


def _mm_col_t(name, pairs, wg, K):
    S = pairs[0][0].shape[0]
    Ns = wg.shape[2]
    tm = min(512, S)
    n = len(pairs)

    def body(*refs):
        o_ref = refs[-1]
        j = pl.program_id(1)
        d = _dot_nt(refs[0][...], refs[n][...])
        for p in range(1, n):
            d = d + _dot_nt(refs[p][...], refs[n + p][...])

        @pl.when(j == 0)
        def _():
            o_ref[...] = d

        @pl.when(j > 0)
        def _():
            o_ref[...] += d

    in_specs = [pl.BlockSpec((tm, Ns), lambda m, j: (m, j)) for _ in pairs]
    in_specs += [pl.BlockSpec((None, K, Ns), lambda m, j, _rb=rs // K: (j, _rb, 0)) for _, rs in pairs]
    return pl.pallas_call(
        body, name=name, grid=(S // tm, N_CHIPS), in_specs=in_specs,
        out_specs=pl.BlockSpec((tm, K), lambda m, j: (m, 0)),
        out_shape=jax.ShapeDtypeStruct((S, K), F32),
        compiler_params=_cparams("parallel", "arbitrary"),
    )(*[dy for dy, _ in pairs], *[wg for _ in pairs])


def _mm_row_t(name, dy, wg, row_start, Ks, out_dtype):
    S, N = dy.shape
    rb = row_start // Ks
    tm = min(512, S)

    def body(dy_ref, w_ref, o_ref):
        o_ref[...] = _dot_nt(dy_ref[...], w_ref[...]).astype(out_dtype)

    return pl.pallas_call(
        body, name=name, grid=(N_CHIPS, S // tm),
        in_specs=[pl.BlockSpec((tm, N), lambda j, m: (m, 0)), pl.BlockSpec((None, Ks, N), lambda j, m: (j, rb, 0))],
        out_specs=pl.BlockSpec((tm, Ks), lambda j, m: (m, j)),
        out_shape=jax.ShapeDtypeStruct((S, N_CHIPS * Ks), out_dtype),
        compiler_params=_cparams("parallel", "parallel"),
    )(dy, wg)


def _silu_parts(g):
    sg = _sigmoid(g)
    return g * sg, sg * (1.0 + g * (1.0 - sg))


def _mm_gate_up(name, a, wg, gate_row, up_row):
    S, K = a.shape
    Ns = wg.shape[2]
    tm = min(512, S)

    def body(a_ref, wg_ref, wu_ref, hg_ref, hu_ref, act_ref):
        a_ = a_ref[...]
        hg = jnp.dot(a_, wg_ref[...], preferred_element_type=F32).astype(BF16)
        hu = jnp.dot(a_, wu_ref[...], preferred_element_type=F32).astype(BF16)
        hg_ref[...] = hg
        hu_ref[...] = hu
        act_ref[...] = (_silu_parts(hg.astype(F32))[0] * hu.astype(F32)).astype(BF16)

    out = jax.ShapeDtypeStruct((S, N_CHIPS * Ns), BF16)
    w_spec = lambda rb: pl.BlockSpec((None, K, Ns), lambda j, m: (j, rb, 0))
    o_spec = pl.BlockSpec((tm, Ns), lambda j, m: (m, j))
    return pl.pallas_call(
        body, name=name, grid=(N_CHIPS, S // tm),
        in_specs=[pl.BlockSpec((tm, K), lambda j, m: (m, 0)), w_spec(gate_row // K), w_spec(up_row // K)],
        out_specs=[o_spec, o_spec, o_spec], out_shape=[out, out, out],
        compiler_params=_cparams("parallel", "parallel"),
    )(a, wg, wg)


def _mm_dact(name, dy, wg, row_start, Ks, hg, hu):
    S, N = dy.shape
    rb = row_start // Ks
    tm = min(512, S)

    def body(dy_ref, w_ref, hg_ref, hu_ref, dhg_ref, dhu_ref):
        dact = _dot_nt(dy_ref[...], w_ref[...])
        silu, dsilu = _silu_parts(hg_ref[...].astype(F32))
        dhg_ref[...] = (dact * hu_ref[...].astype(F32) * dsilu).astype(BF16)
        dhu_ref[...] = (dact * silu).astype(BF16)

    out = jax.ShapeDtypeStruct((S, N_CHIPS * Ks), BF16)
    t_spec = pl.BlockSpec((tm, Ks), lambda j, m: (m, j))
    return pl.pallas_call(
        body, name=name, grid=(N_CHIPS, S // tm),
        in_specs=[pl.BlockSpec((tm, N), lambda j, m: (m, 0)), pl.BlockSpec((None, Ks, N), lambda j, m: (j, rb, 0)), t_spec, t_spec],
        out_specs=[t_spec, t_spec], out_shape=[out, out],
        compiler_params=_cparams("parallel", "parallel"),
    )(dy, wg, hg, hu)


def _mm_dw(name, a, dy, dg, row_start, kind):
    S = a.shape[0]
    _, _, W = dg.shape
    if kind == 'col':
        K = a.shape[1]
        rows = K
        tk, tn = _pick(K, 512), W
        a_map = lambda j, nb, kb: (0, kb)
        dy_map = lambda j, nb, kb: (0, j * (W // tn) + nb)
    else:
        rows = a.shape[1] // N_CHIPS
        tk = rows if rows * S * 2 * 2 <= 12 * 1024 * 1024 else _pick(rows, 512)
        tn = _pick(W, 1024)
        a_map = lambda j, nb, kb: (0, j * (rows // tk) + kb)
        dy_map = lambda j, nb, kb: (0, nb)
    rb = row_start // tk
    assert row_start % tk == 0

    def body(a_ref, dy_ref, dg_in, o_ref):
        del dg_in
        o_ref[...] = _dot_tn(a_ref[...], dy_ref[...]).astype(o_ref.dtype)

    return pl.pallas_call(
        body, name=name, grid=(N_CHIPS, W // tn, rows // tk),
        in_specs=[pl.BlockSpec((S, tk), a_map), pl.BlockSpec((S, tn), dy_map), pl.BlockSpec(memory_space=pl.ANY)],
        out_specs=pl.BlockSpec((None, tk, tn), lambda j, nb, kb: (j, rb + kb, nb)),
        out_shape=jax.ShapeDtypeStruct(dg.shape, dg.dtype),
        input_output_aliases={2: 0},
        compiler_params=_cparams("parallel", "parallel", "parallel"),
    )(a, dy, dg)


def _fox_probs(q, k, c_blk, crow, h, qi, tq):
    n = k.shape[0]
    s = _dot_nt(q, k) * (HEAD_DIM ** -0.5)
    lane = lax.broadcasted_iota(jnp.int32, c_blk.shape, 1)
    ccol = jnp.sum(jnp.where(lane == h, c_blk, 0.0), axis=1, keepdims=True)
    s = s + (ccol - crow)
    t_idx = qi * tq + lax.broadcasted_iota(jnp.int32, (tq, n), 0)
    s_idx = lax.broadcasted_iota(jnp.int32, (tq, n), 1)
    s = jnp.where(s_idx <= t_idx, s, NEG_BIG)
    p = jnp.exp(s - jnp.max(s, axis=1, keepdims=True))
    return p * (1.0 / jnp.sum(p, axis=1, keepdims=True))


def _per_query_block(qi, nq, tq, fn):
    for qv in range(nq):
        @pl.when(qi == qv)
        def _(qv=qv):
            fn(qv, (qv + 1) * tq)


def _fox_fwd(qkv, c, crow, H):
    S = qkv.shape[0]
    tq = min(256, S)

    def body(q_ref, k_ref, v_ref, c_ref, crow_ref, o_ref):
        def block(qv, n):
            p = _fox_probs(q_ref[...], k_ref[0:n, :], c_ref[...], crow_ref[:, 0:n], pl.program_id(0), qv, tq)
            o_ref[...] = jnp.dot(p.astype(BF16), v_ref[0:n, :], preferred_element_type=F32).astype(o_ref.dtype)

        _per_query_block(pl.program_id(1), S // tq, tq, block)

    return pl.pallas_call(
        body, name="fox_attn_fwd", grid=(H, S // tq),
        in_specs=[pl.BlockSpec((tq, HEAD_DIM), lambda h, i: (i, h)),
                  pl.BlockSpec((S, HEAD_DIM), lambda h, i: (0, H + h)),
                  pl.BlockSpec((S, HEAD_DIM), lambda h, i: (0, 2 * H + h)),
                  pl.BlockSpec((tq, H), lambda h, i: (i, 0)),
                  pl.BlockSpec((None, 1, S), lambda h, i: (h, 0, 0))],
        out_specs=pl.BlockSpec((tq, HEAD_DIM), lambda h, i: (i, h)),
        out_shape=jax.ShapeDtypeStruct((S, H * HEAD_DIM), BF16),
        compiler_params=_cparams("parallel", "parallel"),
    )(qkv, qkv, qkv, c, crow)


def _fox_bwd(qkv, c, crow, do, H):
    S = qkv.shape[0]
    tq = min(256, S)
    nq = S // tq

    def body(q_ref, k_ref, v_ref, c_ref, crow_ref, do_ref, dq_ref, dk_ref, dv_ref, dc_ref, dk_acc, dv_acc):
        qi = pl.program_id(1)

        @pl.when(qi == 0)
        def _():
            dk_acc[...] = jnp.zeros_like(dk_acc)
            dv_acc[...] = jnp.zeros_like(dv_acc)
            dc_ref[...] = jnp.zeros_like(dc_ref)

        def block(qv, n):
            q, k, v, do_ = q_ref[...], k_ref[0:n, :], v_ref[0:n, :], do_ref[...]
            p = _fox_probs(q, k, c_ref[...], crow_ref[:, 0:n], pl.program_id(0), qv, tq)
            dv_acc[0:n, :] += _dot_tn(p.astype(BF16), do_)
            dp = _dot_nt(do_, v)
            ds = p * (dp - jnp.sum(p * dp, axis=1, keepdims=True))
            dsb = (ds * (HEAD_DIM ** -0.5)).astype(BF16)
            dq_ref[...] = jnp.dot(dsb, k, preferred_element_type=F32).astype(dq_ref.dtype)
            dk_acc[0:n, :] += _dot_tn(dsb, q)
            dc_ref[:, 0:n] += -jnp.sum(ds, axis=0, keepdims=True)

        _per_query_block(qi, nq, tq, block)

        @pl.when(qi == nq - 1)
        def _():
            dk_ref[...] = dk_acc[...].astype(dk_ref.dtype)
            dv_ref[...] = dv_acc[...].astype(dv_ref.dtype)

    D = H * HEAD_DIM
    return pl.pallas_call(
        body, name="fox_attn_bwd", grid=(H, nq),
        in_specs=[pl.BlockSpec((tq, HEAD_DIM), lambda h, i: (i, h)),
                  pl.BlockSpec((S, HEAD_DIM), lambda h, i: (0, H + h)),
                  pl.BlockSpec((S, HEAD_DIM), lambda h, i: (0, 2 * H + h)),
                  pl.BlockSpec((tq, H), lambda h, i: (i, 0)),
                  pl.BlockSpec((None, 1, S), lambda h, i: (h, 0, 0)),
                  pl.BlockSpec((tq, HEAD_DIM), lambda h, i: (i, h))],
        out_specs=[pl.BlockSpec((tq, HEAD_DIM), lambda h, i: (i, h)),
                   pl.BlockSpec((S, HEAD_DIM), lambda h, i: (0, h)),
                   pl.BlockSpec((S, HEAD_DIM), lambda h, i: (0, h)),
                   pl.BlockSpec((None, 1, S), lambda h, i: (h, 0, 0))],
        out_shape=[jax.ShapeDtypeStruct((S, D), BF16), jax.ShapeDtypeStruct((S, D), BF16),
                   jax.ShapeDtypeStruct((S, D), BF16), jax.ShapeDtypeStruct((H, 1, S), F32)],
        scratch_shapes=[pltpu.VMEM((S, HEAD_DIM), F32), pltpu.VMEM((S, HEAD_DIM), F32)],
        compiler_params=_cparams("parallel", "arbitrary"),
    )(qkv, qkv, qkv, c, crow, do)


def _cumsum_rows(name, xin, reverse):
    S, H = xin.shape
    tb = min(256, S)
    nb = S // tb

    def body(x_ref, o_ref):
        r = lax.broadcasted_iota(jnp.int32, (tb, tb), 0)
        cidx = lax.broadcasted_iota(jnp.int32, (tb, tb), 1)
        tri = (r <= cidx if reverse else r >= cidx).astype(F32)

        def step(b, carry):
            bb = nb - 1 - b if reverse else b
            rows = pl.ds(pl.multiple_of(bb * tb, tb), tb)
            blk = x_ref[rows, :]
            o_ref[rows, :] = jnp.dot(tri, blk, precision=HIGHEST, preferred_element_type=F32) + carry
            return carry + jnp.sum(blk, axis=0, keepdims=True)

        lax.fori_loop(0, nb, step, jnp.zeros((1, H), F32))

    return pl.pallas_call(
        body, name=name, out_shape=jax.ShapeDtypeStruct((S, H), F32),
        in_specs=[pl.BlockSpec(memory_space=pltpu.VMEM)], out_specs=pl.BlockSpec(memory_space=pltpu.VMEM),
        compiler_params=_cparams(),
    )(xin)


def _rel_onehot(i, transposed):
    shp = (REL_TABLE_PAD, BAND_KEYS) if transposed else (BAND_KEYS, REL_TABLE_PAD)
    j = lax.broadcasted_iota(jnp.int32, shp, 1 if transposed else 0)
    r = lax.broadcasted_iota(jnp.int32, shp, 0 if transposed else 1)
    return (jnp.clip(PAD_KEYS + i - j, -REL_CLIP, REL_CLIP) + REL_CLIP == r).astype(F32)


def _rel_expand(rb_pad):
    H = rb_pad.shape[0]

    def body(rb_ref, o_ref):
        def step(i, _):
            o_ref[i] = jnp.dot(rb_ref[...], _rel_onehot(i, True), precision=HIGHEST, preferred_element_type=F32)
            return 0
        lax.fori_loop(0, CHUNK, step, 0)

    return pl.pallas_call(
        body, name="rel_bias_expand", out_shape=jax.ShapeDtypeStruct((CHUNK, H, BAND_KEYS), F32),
        in_specs=[pl.BlockSpec(memory_space=pltpu.VMEM)], out_specs=pl.BlockSpec(memory_space=pltpu.VMEM),
        compiler_params=_cparams(),
    )(rb_pad)


def _rel_reduce(dbt):
    H = dbt.shape[1]

    def body(d_ref, o_ref):
        def step(i, acc):
            j = lax.broadcasted_iota(jnp.int32, (REL_WIN, REL_TABLE_PAD), 0)
            r = lax.broadcasted_iota(jnp.int32, (REL_WIN, REL_TABLE_PAD), 1)
            onehot = (jnp.clip(PAD_KEYS + i - j, -REL_CLIP, REL_CLIP) + REL_CLIP == r).astype(BF16)
            d = d_ref[i]
            hi = d.astype(BF16)
            lo = (d - hi.astype(F32)).astype(BF16)
            return acc + (jnp.dot(hi, onehot, preferred_element_type=F32) + jnp.dot(lo, onehot, preferred_element_type=F32))
        o_ref[...] = lax.fori_loop(0, REL_QB, step, jnp.zeros((H, REL_TABLE_PAD), F32))

    return pl.pallas_call(
        body, name="rel_bias_reduce", out_shape=jax.ShapeDtypeStruct((H, REL_TABLE_PAD), F32),
        in_specs=[pl.BlockSpec(memory_space=pltpu.VMEM)], out_specs=pl.BlockSpec(memory_space=pltpu.VMEM),
        compiler_params=_cparams(),
    )(dbt)


def _rel_window_bias(bias):
    H = bias.shape[0]
    out = jnp.full((H, REL_QB, REL_WIN), NEG_BIG, F32)
    for a in range(REL_QB // CHUNK):
        out = out.at[:, a * CHUNK:(a + 1) * CHUNK, a * CHUNK:a * CHUNK + BAND_KEYS].set(bias)
    return out


def _rel_probs(q, kw, bias_w, t0):
    s = _dot_nt(q, kw) * (HEAD_DIM ** -0.5) + bias_w
    j = lax.broadcasted_iota(jnp.int32, (REL_QB, REL_WIN), 1)
    s = jnp.where(j >= PAD_KEYS - t0, s, NEG_BIG)
    p = jnp.exp(s - jnp.max(s, axis=1, keepdims=True))
    return p * (1.0 / jnp.sum(p, axis=1, keepdims=True))


def _rel_fwd(qkv, bias_w, H):
    S = qkv.shape[0]

    def body(q_ref, k_ref, v_ref, b_ref, o_ref, kpad, vpad):
        kpad[0:PAD_KEYS, :] = jnp.zeros((PAD_KEYS, HEAD_DIM), BF16)
        vpad[0:PAD_KEYS, :] = jnp.zeros((PAD_KEYS, HEAD_DIM), BF16)
        kpad[PAD_KEYS:PAD_KEYS + S, :] = k_ref[...]
        vpad[PAD_KEYS:PAD_KEYS + S, :] = v_ref[...]

        def block(n, _):
            t0 = pl.multiple_of(n * REL_QB, REL_QB)
            rows, win = pl.ds(t0, REL_QB), pl.ds(t0, REL_WIN)
            p = _rel_probs(q_ref[rows, :], kpad[win, :], b_ref[...], t0)
            o_ref[rows, :] = jnp.dot(p.astype(BF16), vpad[win, :], preferred_element_type=F32).astype(o_ref.dtype)
            return 0

        lax.fori_loop(0, S // REL_QB, block, 0)

    return pl.pallas_call(
        body, name="rel_attn_fwd", grid=(H,),
        in_specs=[pl.BlockSpec((S, HEAD_DIM), lambda h: (0, h)),
                  pl.BlockSpec((S, HEAD_DIM), lambda h: (0, H + h)),
                  pl.BlockSpec((S, HEAD_DIM), lambda h: (0, 2 * H + h)),
                  pl.BlockSpec((None, REL_QB, REL_WIN), lambda h: (h, 0, 0))],
        out_specs=pl.BlockSpec((S, HEAD_DIM), lambda h: (0, h)),
        out_shape=jax.ShapeDtypeStruct((S, H * HEAD_DIM), BF16),
        scratch_shapes=[pltpu.VMEM((S + PAD_KEYS, HEAD_DIM), BF16), pltpu.VMEM((S + PAD_KEYS, HEAD_DIM), BF16)],
        compiler_params=_cparams("parallel"),
    )(qkv, qkv, qkv, bias_w)


def _rel_bwd(qkv, bias_w, do, H):
    S = qkv.shape[0]
    D = H * HEAD_DIM

    def body(q_ref, k_ref, v_ref, b_ref, do_ref, dq_ref, dk_ref, dv_ref, db_ref, kpad, vpad, dkpad, dvpad):
        kpad[0:PAD_KEYS, :] = jnp.zeros((PAD_KEYS, HEAD_DIM), BF16)
        vpad[0:PAD_KEYS, :] = jnp.zeros((PAD_KEYS, HEAD_DIM), BF16)
        kpad[PAD_KEYS:PAD_KEYS + S, :] = k_ref[...]
        vpad[PAD_KEYS:PAD_KEYS + S, :] = v_ref[...]
        dkpad[...] = jnp.zeros_like(dkpad)
        dvpad[...] = jnp.zeros_like(dvpad)
        db_ref[...] = jnp.zeros_like(db_ref)

        def block(n, _):
            t0 = pl.multiple_of(n * REL_QB, REL_QB)
            rows, win = pl.ds(t0, REL_QB), pl.ds(t0, REL_WIN)
            q, kw, vw, do_ = q_ref[rows, :], kpad[win, :], vpad[win, :], do_ref[rows, :]
            p = _rel_probs(q, kw, b_ref[...], t0)
            dvpad[win, :] += _dot_tn(p.astype(BF16), do_)
            dp = _dot_nt(do_, vw)
            ds = p * (dp - jnp.sum(p * dp, axis=1, keepdims=True))
            db_ref[...] += ds
            dsb = (ds * (HEAD_DIM ** -0.5)).astype(BF16)
            dq_ref[rows, :] = jnp.dot(dsb, kw, preferred_element_type=F32).astype(dq_ref.dtype)
            dkpad[win, :] += _dot_tn(dsb, q)
            return 0

        lax.fori_loop(0, S // REL_QB, block, 0)
        dk_ref[...] = dkpad[PAD_KEYS:PAD_KEYS + S, :].astype(dk_ref.dtype)
        dv_ref[...] = dvpad[PAD_KEYS:PAD_KEYS + S, :].astype(dv_ref.dtype)

    head = lambda h: (0, h)
    return pl.pallas_call(
        body, name="rel_attn_bwd", grid=(H,),
        in_specs=[pl.BlockSpec((S, HEAD_DIM), head),
                  pl.BlockSpec((S, HEAD_DIM), lambda h: (0, H + h)),
                  pl.BlockSpec((S, HEAD_DIM), lambda h: (0, 2 * H + h)),
                  pl.BlockSpec((None, REL_QB, REL_WIN), lambda h: (h, 0, 0)),
                  pl.BlockSpec((S, HEAD_DIM), head)],
        out_specs=[pl.BlockSpec((S, HEAD_DIM), head), pl.BlockSpec((S, HEAD_DIM), head), pl.BlockSpec((S, HEAD_DIM), head),
                   pl.BlockSpec((None, REL_QB, REL_WIN), lambda h: (h, 0, 0))],
        out_shape=[jax.ShapeDtypeStruct((S, D), BF16), jax.ShapeDtypeStruct((S, D), BF16), jax.ShapeDtypeStruct((S, D), BF16),
                   jax.ShapeDtypeStruct((H, REL_QB, REL_WIN), F32)],
        scratch_shapes=[pltpu.VMEM((S + PAD_KEYS, HEAD_DIM), BF16), pltpu.VMEM((S + PAD_KEYS, HEAD_DIM), BF16),
                        pltpu.VMEM((S + PAD_KEYS, HEAD_DIM), F32), pltpu.VMEM((S + PAD_KEYS, HEAD_DIM), F32)],
        compiler_params=_cparams("parallel"),
    )(qkv, qkv, qkv, bias_w, do)


def _conv_taps(win, tt, reverse):
    n = tt + 2 * CONV_HALO
    for k in range(CONV_K):
        off = (CONV_K - 1 - k) if reverse else (k - (CONV_K - 1))
        sh = (-off) % n
        rolled = pltpu.roll(win, sh, 0) if sh else win
        yield k, rolled[CONV_HALO:CONV_HALO + tt, :]


def _fill_padded(pad_ref, x_ref, S):
    tc = pad_ref.shape[1]
    pad_ref[0:CONV_HALO, :] = jnp.zeros((CONV_HALO, tc), F32)
    pad_ref[CONV_HALO + S:CONV_HALO + S + CONV_HALO, :] = jnp.zeros((CONV_HALO, tc), F32)
    pad_ref[CONV_HALO:CONV_HALO + S, :] = x_ref[...]


def _dwconv(name, xin, w32, bias, reverse):
    S, D = xin.shape
    tc = min(256, D)
    tt = min(256, S)

    def body(x_ref, w_ref, b_ref, y_ref, pad_ref):
        _fill_padded(pad_ref, x_ref, S)
        def tile(ti, _):
            t0 = pl.multiple_of(ti * tt, tt)
            win = pad_ref[pl.ds(t0, tt + 2 * CONV_HALO), :]
            acc = jnp.zeros((tt, tc), F32) + b_ref[...]
            for k, shifted in _conv_taps(win, tt, reverse):
                acc = acc + w_ref[pl.ds(k, 1), :] * shifted
            y_ref[pl.ds(t0, tt), :] = acc
            return 0

        lax.fori_loop(0, S // tt, tile, 0)

    return pl.pallas_call(
        body, name=name, grid=(D // tc,),
        in_specs=[pl.BlockSpec((S, tc), lambda i: (0, i)), pl.BlockSpec((CONV_HALO, tc), lambda i: (0, i)),
                  pl.BlockSpec((1, tc), lambda i: (0, i))],
        out_specs=pl.BlockSpec((S, tc), lambda i: (0, i)),
        out_shape=jax.ShapeDtypeStruct((S, D), F32),
        scratch_shapes=[pltpu.VMEM((S + 2 * CONV_HALO, tc), F32)],
        compiler_params=_cparams("parallel"),
    )(xin, w32, bias)


def _dwconv_dw(xin, dy):
    S, D = xin.shape
    tc = min(256, D)
    tt = min(256, S)

    def body(x_ref, dy_ref, o_ref, pad_ref):
        _fill_padded(pad_ref, x_ref, S)

        def tile(ti, acc):
            t0 = pl.multiple_of(ti * tt, tt)
            win = pad_ref[pl.ds(t0, tt + 2 * CONV_HALO), :]
            dyt = dy_ref[pl.ds(t0, tt), :]
            ridx = lax.broadcasted_iota(jnp.int32, (CONV_HALO, tc), 0)
            upd = jnp.zeros((CONV_HALO, tc), F32)
            for k, shifted in _conv_taps(win, tt, False):
                upd = jnp.where(ridx == k, jnp.sum(dyt * shifted, axis=0, keepdims=True), upd)
            return acc + upd

        o_ref[...] = lax.fori_loop(0, S // tt, tile, jnp.zeros((CONV_HALO, tc), F32))

    return pl.pallas_call(
        body, name="dwconv_dw", grid=(D // tc,),
        in_specs=[pl.BlockSpec((S, tc), lambda i: (0, i)), pl.BlockSpec((S, tc), lambda i: (0, i))],
        out_specs=pl.BlockSpec((CONV_HALO, tc), lambda i: (0, i)),
        out_shape=jax.ShapeDtypeStruct((CONV_HALO, D), F32),
        scratch_shapes=[pltpu.VMEM((S + 2 * CONV_HALO, tc), F32)],
        compiler_params=_cparams("parallel"),
    )(xin, dy)


def _place():
    x, y, c = lax.axis_index("x"), lax.axis_index("y"), lax.axis_index("c")
    chips = [(1 - x, y), (x, 1 - y), (1 - x, 1 - y)]
    return x, y, c, chips


def _remote(src, dst, ssem, rsem, dev):
    return pltpu.make_async_remote_copy(src_ref=src, dst_ref=dst, send_sem=ssem, recv_sem=rsem,
                                        device_id=dev, device_id_type=MESH_IDS)


_ANY = pl.BlockSpec(memory_space=pl.ANY)


def _place_own(name, own, place):
    R, W = own.shape
    tr = _pick_rows(R)

    def body(p_ref, a_ref, o_ref):
        del p_ref
        o_ref[...] = a_ref[...]

    return pl.pallas_call(
        body, name=name,
        grid_spec=pltpu.PrefetchScalarGridSpec(
            num_scalar_prefetch=1, grid=(R // tr,),
            in_specs=[pl.BlockSpec((tr, W), lambda i, p: (i, 0))],
            out_specs=pl.BlockSpec((None, tr, W), lambda i, p: (p[1], i, 0))),
        out_shape=jax.ShapeDtypeStruct((N_CHIPS, R, W), own.dtype),
        compiler_params=_cparams("parallel"),
    )(place, own)


_HBM = pl.BlockSpec(memory_space=pltpu.HBM)
_SEM = pl.BlockSpec(memory_space=pltpu.SEMAPHORE)
GROUPS = ('qkv', 'ffn', 'pw1', 'dm')


def _half_rows(c, r0, n):
    return pl.ds(pl.multiple_of(r0 + c * (n // 2), SUBLANES), n // 2)


def _gather_start(wgs, layers, after):
    G, L = len(wgs), len(layers)

    def body(*refs):
        outs = refs[G + 1:]
        ssems, rsems, bufs = outs[:L], outs[L:2 * L], outs[2 * L:]
        x, y, c, chips = _place()
        me = 2 * x + y
        for li, pieces in enumerate(layers):
            for pi, (g, r0, n) in enumerate(pieces):
                blk = bufs[g].at[me, _half_rows(c, r0, n)]
                for j, (px, py) in enumerate(chips):
                    _remote(blk, blk, ssems[li].at[3 * pi + j], rsems[li].at[3 * pi + j], (px, py, c)).start()

    sem_shapes = [pltpu.SemaphoreType.DMA((3 * len(p),)) for p in layers]
    res = pl.pallas_call(
        body, name="gather_start", in_specs=[_HBM] * G + [_ANY],
        out_specs=[_SEM] * (2 * L) + [_HBM] * G,
        out_shape=sem_shapes + sem_shapes + [pltpu.HBM(w.shape, w.dtype) for w in wgs],
        input_output_aliases={g: 2 * L + g for g in range(G)},
        compiler_params=pltpu.CompilerParams(has_side_effects=pltpu.SideEffectType.DATAFLOW_SIDE_EFFECTING),
    )(*[pltpu.with_memory_space_constraint(w, pltpu.HBM) for w in wgs], after)
    return res[:L], res[L:2 * L], list(res[2 * L:])


def _gather_wait(name, wgs, ssem, rsem, pieces, after):
    G = len(wgs)

    def body(*refs):
        ssem_ref, rsem_ref = refs[G], refs[G + 1]
        bufs = refs[G + 3:]
        x, y, c, chips = _place()
        me = 2 * x + y
        for pi, (g, r0, n) in enumerate(pieces):
            rows = _half_rows(c, r0, n)
            for j, (px, py) in enumerate(chips):
                cp = _remote(bufs[g].at[me, rows], bufs[g].at[2 * px + py, rows],
                             ssem_ref.at[3 * pi + j], rsem_ref.at[3 * pi + j], (px, py, c))
                cp.wait_send()
                cp.wait_recv()

    return list(pl.pallas_call(
        body, name=name, in_specs=[_HBM] * G + [_SEM, _SEM, _ANY], out_specs=[_HBM] * G,
        out_shape=[pltpu.HBM(w.shape, w.dtype) for w in wgs],
        input_output_aliases={g: g for g in range(G)},
        compiler_params=pltpu.CompilerParams(has_side_effects=pltpu.SideEffectType.DATAFLOW_SIDE_EFFECTING),
    )(*wgs, ssem, rsem, after))


def _gather_forward(name, wgs, pieces):
    G = len(wgs)
    n_cp = 3 * len(pieces)

    def body(*refs):
        bufs, ssems, rsems = refs[G:2 * G], refs[2 * G], refs[2 * G + 1]
        x, y, c, chips = _place()
        sib = (x, y, 1 - c)
        cps = []
        for pi, (g, r0, n) in enumerate(pieces):
            for j, (px, py) in enumerate(chips):
                blk = bufs[g].at[2 * px + py, _half_rows(c, r0, n)]
                cps.append(_remote(blk, blk, ssems.at[3 * pi + j], rsems.at[3 * pi + j], sib))
        for cp in cps:
            cp.start()
        for pi, (g, r0, n) in enumerate(pieces):
            for j, (px, py) in enumerate(chips):
                blk = bufs[g].at[2 * px + py, _half_rows(1 - c, r0, n)]
                _remote(blk, blk, ssems.at[3 * pi + j], rsems.at[3 * pi + j], sib).wait_recv()
        for cp in cps:
            cp.wait_send()

    return list(pl.pallas_call(
        body, name=name, in_specs=[_ANY] * G, out_specs=[_ANY] * G,
        out_shape=[jax.ShapeDtypeStruct(w.shape, w.dtype) for w in wgs],
        input_output_aliases={g: g for g in range(G)},
        scratch_shapes=[pltpu.SemaphoreType.DMA((n_cp,)), pltpu.SemaphoreType.DMA((n_cp,))],
        compiler_params=pltpu.CompilerParams(has_side_effects=True),
    )(*wgs))


def _swap_halves(name, dgs, pieces):
    G = len(dgs)
    n_cp = N_CHIPS * len(pieces)

    def body(*refs):
        srcs, lands, ssems, rsems = refs[:G], refs[G:G + len(pieces)], refs[-2], refs[-1]
        x, y, c, _ = _place()
        cps = [_remote(srcs[g].at[j, _half_rows(1 - c, r0, n)], lands[pi].at[j],
                       ssems.at[N_CHIPS * pi + j], rsems.at[N_CHIPS * pi + j], (x, y, 1 - c))
               for pi, (g, r0, n) in enumerate(pieces) for j in range(N_CHIPS)]
        for cp in cps:
            cp.start()
        for cp in cps:
            cp.wait()

    return list(pl.pallas_call(
        body, name=name, in_specs=[_ANY] * G, out_specs=[_ANY] * len(pieces),
        out_shape=[jax.ShapeDtypeStruct((N_CHIPS, n // 2, dgs[g].shape[2]), dgs[g].dtype) for g, _, n in pieces],
        scratch_shapes=[pltpu.SemaphoreType.DMA((n_cp,)), pltpu.SemaphoreType.DMA((n_cp,))],
        compiler_params=pltpu.CompilerParams(has_side_effects=True),
    )(*dgs))


def _scatter_start(name, pbs):
    P = len(pbs)

    def body(*refs):
        outs = refs[2 * P:]
        ssems, rsems, src, land = outs[0], outs[1], outs[2:2 + P], outs[2 + P:]
        x, y, c, chips = _place()
        me = 2 * x + y
        for pi in range(P):
            for j, (px, py) in enumerate(chips):
                _remote(src[pi].at[2 * px + py], land[pi].at[me], ssems.at[3 * pi + j], rsems.at[3 * pi + j], (px, py, c)).start()

    sems = pltpu.SemaphoreType.DMA((3 * P,))
    hbm = [pltpu.HBM(p.shape, p.dtype) for p in pbs]
    res = pl.pallas_call(
        body, name=name, in_specs=[_HBM] * (2 * P), out_specs=[_SEM, _SEM] + [_HBM] * (2 * P),
        out_shape=[sems, sems] + hbm + hbm,
        input_output_aliases={k: 2 + k for k in range(2 * P)},
        compiler_params=pltpu.CompilerParams(has_side_effects=pltpu.SideEffectType.DATAFLOW_SIDE_EFFECTING),
    )(*[pltpu.with_memory_space_constraint(p, pltpu.HBM) for p in pbs],
      *[pltpu.with_memory_space_constraint(lax.empty(p.shape, p.dtype), pltpu.HBM) for p in pbs])
    return res[0], res[1], list(res[2:2 + P]), list(res[2 + P:])


def _scatter_wait(name, ssem, rsem, pbs, lands, after):
    P = len(pbs)

    def body(*refs):
        ssems, rsems = refs[2 * P], refs[2 * P + 1]
        outs = refs[2 * P + 3:]
        src, land = outs[:P], outs[P:]
        x, y, c, chips = _place()
        for pi in range(P):
            for j, (px, py) in enumerate(chips):
                cp = _remote(src[pi].at[2 * px + py], land[pi].at[2 * px + py], ssems.at[3 * pi + j], rsems.at[3 * pi + j], (px, py, c))
                cp.wait_send()
                cp.wait_recv()

    hbm = [pltpu.HBM(p.shape, p.dtype) for p in pbs]
    res = pl.pallas_call(
        body, name=name, in_specs=[_HBM] * (2 * P) + [_SEM, _SEM, _ANY], out_specs=[_HBM] * (2 * P),
        out_shape=hbm + hbm, input_output_aliases={k: k for k in range(2 * P)},
        compiler_params=pltpu.CompilerParams(has_side_effects=pltpu.SideEffectType.DATAFLOW_SIDE_EFFECTING),
    )(*pbs, *lands, ssem, rsem, after)
    return list(res[:P]), list(res[P:])


def _share_halves(name, gf, pieces):
    def body(in_ref, out, ssems, rsems):
        del in_ref
        x, y, c, _ = _place()
        cps = []
        for pi, (r0, n) in enumerate(pieces):
            mine = out.at[_half_rows(c, r0, n)]
            cps.append(_remote(mine, mine, ssems.at[pi], rsems.at[pi], (x, y, 1 - c)))
        for cp in cps:
            cp.start()
        for pi, (r0, n) in enumerate(pieces):
            theirs = out.at[_half_rows(1 - c, r0, n)]
            _remote(theirs, theirs, ssems.at[pi], rsems.at[pi], (x, y, 1 - c)).wait_recv()
        for cp in cps:
            cp.wait_send()

    return pl.pallas_call(
        body, name=name, in_specs=[_ANY], out_specs=_ANY,
        out_shape=jax.ShapeDtypeStruct(gf.shape, gf.dtype), input_output_aliases={0: 0},
        scratch_shapes=[pltpu.SemaphoreType.DMA((len(pieces),)), pltpu.SemaphoreType.DMA((len(pieces),))],
        compiler_params=pltpu.CompilerParams(has_side_effects=True),
    )(gf)


def _broadcast_small(name, buf, after=None):
    R = buf.shape[0]

    def body(src, *rest):
        out, ssems, rsems = rest[-3:]
        x, y, c, _ = _place()
        me = 4 * x + 2 * y + c
        out[me] = src[...]
        peers = []
        for mask in range(1, 8):
            fx, fy, fc = (mask >> 2) & 1, (mask >> 1) & 1, mask & 1
            peers.append((1 - x if fx else x, 1 - y if fy else y, 1 - c if fc else c))
        cps = [_remote(src, out.at[me], ssems.at[k], rsems.at[k], p) for k, p in enumerate(peers)]
        for cp in cps:
            cp.start()
        for k, (px, py, pc) in enumerate(peers):
            blk = out.at[4 * px + 2 * py + pc]
            _remote(blk, blk, ssems.at[k], rsems.at[k], (px, py, pc)).wait_recv()
        for cp in cps:
            cp.wait_send()

    return pl.pallas_call(
        body, name=name, in_specs=[pl.BlockSpec(memory_space=pltpu.VMEM)] + ([_ANY] if after is not None else []),
        out_specs=pl.BlockSpec(memory_space=pltpu.VMEM),
        out_shape=jax.ShapeDtypeStruct((8, R, LANES), F32),
        scratch_shapes=[pltpu.SemaphoreType.DMA((7,)), pltpu.SemaphoreType.DMA((7,))],
        compiler_params=pltpu.CompilerParams(has_side_effects=True, vmem_limit_bytes=VMEM_LIMIT_V7X),
    )(buf, *([after] if after is not None else []))


def _sum_slabs(name, slabs):
    n, R, _ = slabs.shape

    def body(s_ref, o_ref):
        acc = s_ref[0]
        for k in range(1, n):
            acc = acc + s_ref[k]
        o_ref[...] = acc

    return pl.pallas_call(
        body, name=name, out_shape=jax.ShapeDtypeStruct((R, LANES), F32),
        in_specs=[pl.BlockSpec(memory_space=pltpu.VMEM)], out_specs=pl.BlockSpec(memory_space=pltpu.VMEM),
        compiler_params=_cparams(),
    )(slabs)


def _pick_rows(rows, target=512):
    best = SUBLANES
    for t in range(SUBLANES, min(rows, target) + 1, SUBLANES):
        if rows % t == 0:
            best = t
    return best


def _half_tile(r0, n):
    return _pick_rows(math.gcd(r0, n // 2) if r0 else n // 2)


def _pair_sum(name, dg, land, place, r0, n):
    W = dg.shape[2]
    tr = _half_tile(r0, n)
    nb = (n // 2) // tr

    def body(p_ref, a_ref, b_ref, o_ref):
        del p_ref
        o_ref[...] = (a_ref[...].astype(F32) + b_ref[...].astype(F32)).astype(o_ref.dtype)

    return pl.pallas_call(
        body, name=name,
        grid_spec=pltpu.PrefetchScalarGridSpec(
            num_scalar_prefetch=1, grid=(N_CHIPS, nb),
            in_specs=[pl.BlockSpec((None, tr, W), lambda j, i, p: (j, r0 // tr + p[0] * nb + i, 0)),
                      pl.BlockSpec((None, tr, W), lambda j, i, p: (j, i, 0))],
            out_specs=pl.BlockSpec((None, tr, W), lambda j, i, p: (j, i, 0))),
        out_shape=jax.ShapeDtypeStruct((N_CHIPS, n // 2, W), BF16),
        compiler_params=_cparams("parallel", "parallel"),
    )(place, dg, land)


def _chip_sum(name, pb, land, place, gf, r0, n):
    W = gf.shape[1]
    tr = _half_tile(r0, n)
    nb = (n // 2) // tr

    def body(p_ref, own_ref, lx_ref, ly_ref, ld_ref, gf_in, o_ref):
        del p_ref, gf_in
        o_ref[...] = ((own_ref[...].astype(F32) + lx_ref[...].astype(F32)) + ly_ref[...].astype(F32)) + ld_ref[...].astype(F32)

    slab = lambda flip: pl.BlockSpec((None, tr, W), lambda i, p, _f=flip: (p[1] ^ _f, i, 0))
    return pl.pallas_call(
        body, name=name,
        grid_spec=pltpu.PrefetchScalarGridSpec(
            num_scalar_prefetch=1, grid=(nb,),
            in_specs=[slab(0), slab(2), slab(1), slab(3), pl.BlockSpec(memory_space=pl.ANY)],
            out_specs=pl.BlockSpec((tr, W), lambda i, p: (r0 // tr + p[0] * nb + i, 0))),
        out_shape=jax.ShapeDtypeStruct(gf.shape, F32),
        input_output_aliases={5: 0},
        compiler_params=_cparams("parallel"),
    )(place, pb, land, land, land, gf)


def _ln_stats(z):
    mu = jnp.mean(z, axis=1, keepdims=True)
    zc = z - mu
    rstd = lax.rsqrt(jnp.mean(zc * zc, axis=1, keepdims=True) + LN_EPS)
    return zc * rstd, rstd


def _ln_fwd(name, xin, m, g, b):
    S, D = xin.shape

    def fn(x_, m_, g_, b_):
        xhat, rstd = _ln_stats(ALPHA * x_ + m_)
        y = xhat * g_ + b_
        return y, y, xhat, rstd

    return _rowwise(name, fn, [xin, m, ('full', g), ('full', b)],
                    [('rows', D, F32), ('rows', D, BF16), ('rows', D, F32), ('rows', 1, F32)], S)


def _ln_bwd_core(dy, xhat, rstd, g):
    dxh = dy * g
    return rstd * (dxh - jnp.mean(dxh, axis=1, keepdims=True) - xhat * jnp.mean(dxh * xhat, axis=1, keepdims=True))


def _ln_bwd(name, terms, xhat, rstd, g, after=None):
    S, D = xhat.shape
    scales = [s for _, s in terms]
    n = len(terms)

    def fn(*v):
        dy = v[0] * scales[0] if scales[0] != 1.0 else v[0]
        for t in range(1, n):
            dy = dy + (v[t] * scales[t] if scales[t] != 1.0 else v[t])
        xh, rs, g_ = v[n], v[n + 1], v[n + 2]
        dz = _ln_bwd_core(dy, xh, rs, g_)
        return dz, dz, jnp.sum(dy * xh, axis=0, keepdims=True), jnp.sum(dy, axis=0, keepdims=True)

    return _rowwise(name, fn, [a for a, _ in terms] + [xhat, rstd, ('full', g)] + ([('after', after)] if after is not None else []),
                    [('rows', D, F32), ('rows', D, BF16), ('acc', (1, D), F32), ('acc', (1, D), F32)], S)


def _adamw_math(w, g, m, v):
    m2 = ADAM_B1 * m + (1.0 - ADAM_B1) * g
    v2 = ADAM_B2 * v + (1.0 - ADAM_B2) * (g * g)
    m_hat = m2 / (1.0 - ADAM_B1 ** ADAM_STEP)
    v_hat = v2 / (1.0 - ADAM_B2 ** ADAM_STEP)
    delta = -ADAM_LR * (m_hat / (jnp.sqrt(v_hat) + ADAM_EPS) + ADAM_WD * w)
    return delta, m2, v2


def _adamw(name, w, gfull, row_start, m, v):
    rows, W = w.shape
    tr = math.gcd(math.gcd(rows, row_start), 256) if row_start else math.gcd(rows, 256)

    def fn(w_, g_, m_, v_):
        d, m2, v2 = _adamw_math(w_, g_, m_, v_)
        return g_, d, m2, v2

    return _rowwise(name, fn, [w, ('off', gfull, row_start // tr), m, v], [('rows', W, F32)] * 4, rows, tm=tr)


def _pack(arrs):
    flat = jnp.concatenate([a.reshape(-1).astype(F32) for a in arrs])
    tile = SUBLANES * LANES
    n = -(-flat.shape[0] // tile) * tile
    return jnp.pad(flat, (0, n - flat.shape[0])).reshape(-1, LANES)


def _unpack(buf, shapes):
    flat = buf.reshape(-1)
    out, pos = [], 0
    for shp in shapes:
        n = math.prod(shp)
        out.append(flat[pos:pos + n].reshape(shp))
        pos += n
    return out


BIG = ['fox_w_qkv', 'fox_w_o', 'rel_w_qkv', 'rel_w_o', 'conv_w_pw1', 'conv_w_pw2', 'ffn_w_gate', 'ffn_w_up', 'ffn_w_down']
SMALL_SHARDED = ['fox_w_f', 'conv_b_pw1', 'conv_w_dw', 'conv_b_dw', 'conv_ln_g', 'conv_ln_b', 'conv_b_pw2']
SMALL_SHARD_AXIS = {'fox_w_f': 1, 'conv_b_pw1': 1, 'conv_w_dw': 2, 'conv_b_dw': 1, 'conv_ln_g': 1, 'conv_ln_b': 1, 'conv_b_pw2': 1}
SMALL_REPL = ['fox_b_f', 'rel_bias', 'ln_mix_g', 'ln_mix_b', 'ln_ffn_g', 'ln_ffn_b']
SMALL = SMALL_SHARDED + SMALL_REPL
WEIGHTS = ['fox_w_qkv', 'fox_w_f', 'fox_b_f', 'fox_w_o', 'rel_w_qkv', 'rel_bias', 'rel_w_o', 'conv_w_pw1', 'conv_b_pw1',
           'conv_w_dw', 'conv_b_dw', 'conv_ln_g', 'conv_ln_b', 'conv_w_pw2', 'conv_b_pw2', 'ffn_w_gate', 'ffn_w_up',
           'ffn_w_down', 'ln_mix_g', 'ln_mix_b', 'ln_ffn_g', 'ln_ffn_b']


def kernel(x, fox_w_qkv, fox_w_f, fox_b_f, fox_w_o, rel_w_qkv, rel_bias, rel_w_o, conv_w_pw1, conv_b_pw1, conv_w_dw, conv_b_dw, conv_ln_g, conv_ln_b, conv_w_pw2, conv_b_pw2, ffn_w_gate, ffn_w_up, ffn_w_down, ln_mix_g, ln_mix_b, ln_ffn_g, ln_ffn_b, loss_target, m_fox_w_qkv, m_fox_w_f, m_fox_b_f, m_fox_w_o, m_rel_w_qkv, m_rel_bias, m_rel_w_o, m_conv_w_pw1, m_conv_b_pw1, m_conv_w_dw, m_conv_b_dw, m_conv_ln_g, m_conv_ln_b, m_conv_w_pw2, m_conv_b_pw2, m_ffn_w_gate, m_ffn_w_up, m_ffn_w_down, m_ln_mix_g, m_ln_mix_b, m_ln_ffn_g, m_ln_ffn_b, v_fox_w_qkv, v_fox_w_f, v_fox_b_f, v_fox_w_o, v_rel_w_qkv, v_rel_bias, v_rel_w_o, v_conv_w_pw1, v_conv_b_pw1, v_conv_w_dw, v_conv_b_dw, v_conv_ln_g, v_conv_ln_b, v_conv_w_pw2, v_conv_b_pw2, v_ffn_w_gate, v_ffn_w_up, v_ffn_w_down, v_ln_mix_g, v_ln_mix_b, v_ln_ffn_g, v_ln_ffn_b):
    A = dict(locals())
    Wt = {n: A[n] for n in WEIGHTS}
    Mo = {n: A['m_' + n] for n in WEIGHTS}
    Vo = {n: A['v_' + n] for n in WEIGHTS}

    _, S, D = x.shape
    H = D // HEAD_DIM
    Ds = D // N_CHIPS
    Nq = fox_w_qkv.shape[2]
    Np = conv_w_pw1.shape[2]
    Fs = ffn_w_gate.shape[2]
    my_x, my_y, my_c = lax.axis_index("x"), lax.axis_index("y"), lax.axis_index("c")
    my_chip = 2 * my_x + my_y
    place = jnp.stack([my_c, my_chip]).astype(jnp.int32)

    wo_base = DEPTH * Fs
    where = {
        'fox_w_qkv': ('qkv', 0), 'rel_w_qkv': ('qkv', N_FOX * D),
        'ffn_w_gate': ('ffn', 0), 'ffn_w_up': ('ffn', DEPTH * D),
        'conv_w_pw1': ('pw1', 0),
        'ffn_w_down': ('dm', 0), 'fox_w_o': ('dm', wo_base), 'rel_w_o': ('dm', wo_base + N_FOX * Ds),
        'conv_w_pw2': ('dm', wo_base + (N_FOX + 1) * Ds),
    }
    members = {'qkv': ['fox_w_qkv', 'rel_w_qkv'], 'ffn': ['ffn_w_gate', 'ffn_w_up'], 'pw1': ['conv_w_pw1'],
               'dm': ['ffn_w_down', 'fox_w_o', 'rel_w_o', 'conv_w_pw2']}
    flat2 = lambda a: a.reshape(-1, a.shape[-1])
    own = {g: jnp.concatenate([flat2(Wt[n]).astype(BF16) for n in ms], axis=0) for g, ms in members.items()}

    def layer_pieces(i):
        kind, j = i % 3, i // 3
        slot = j if kind == 0 else (N_FOX if kind == 1 else N_FOX + 1)
        w_in = (GROUPS.index('pw1'), 0, D) if kind == 2 else (GROUPS.index('qkv'), slot * D, D)
        return [w_in, (GROUPS.index('dm'), wo_base + slot * Ds, Ds), (GROUPS.index('ffn'), i * D, D),
                (GROUPS.index('ffn'), (DEPTH + i) * D, D), (GROUPS.index('dm'), i * Fs, Fs)]

    small_shapes = [Wt[n].shape for n in SMALL_SHARDED]
    slabs = _broadcast_small("gather_small", _pack([Wt[n] for n in SMALL_SHARDED]))

    stages = [part for i in range(DEPTH) for part in (layer_pieces(i)[:2], layer_pieces(i)[2:])]
    gather_ssems, gather_rsems, wg_list = _gather_start(
        [_place_own("place_" + g, own[g], place) for g in GROUPS], stages, slabs)
    WG = dict(zip(GROUPS, wg_list))
    DG = {g: lax.empty(WG[g].shape, BF16) for g in own}

    per_chip = [_unpack(slabs[2 * j], small_shapes) for j in range(N_CHIPS)]
    full = {n: jnp.concatenate([per_chip[j][i] for j in range(N_CHIPS)], axis=SMALL_SHARD_AXIS[n])
            for i, n in enumerate(SMALL_SHARDED)}
    row = lambda v: v.reshape(1, -1)

    SG = {}

    def ffn_fwd(i, xb):
        hg, hu, act = _mm_gate_up(f"ffn{i}_gate_up", xb, WG['ffn'], i * D, (DEPTH + i) * D)
        f = _mm_row(f"ffn{i}_down", act, WG['dm'], i * Fs, Fs)
        return f, (hg, hu, act)

    def ffn_bwd(i, xb, saved, dzb):
        hg, hu, act = saved
        DG['dm'] = _mm_dw(f"ffn{i}_dw_down", act, dzb, DG['dm'], i * Fs, 'row')
        dhg, dhu = _mm_dact(f"ffn{i}_dact", dzb, WG['dm'], i * Fs, Fs, hg, hu)
        DG['ffn'] = _mm_dw(f"ffn{i}_dw_gate", xb, dhg, DG['ffn'], i * D, 'col')
        DG['ffn'] = _mm_dw(f"ffn{i}_dw_up", xb, dhu, DG['ffn'], (DEPTH + i) * D, 'col')
        return _mm_col_t(f"ffn{i}_dx", [(dhg, i * D), (dhu, (DEPTH + i) * D)], WG['ffn'], D)

    def fox_fwd(j, xb):
        qkv = _mm_col(f"fox{j}_qkv", xb, WG['qkv'], j * D)
        wf = full['fox_w_f'][j].astype(BF16)
        def gate_fn(x_, w_, b_):
            z_ = jnp.dot(x_, w_, preferred_element_type=F32) + b_
            return z_, jnp.minimum(z_, 0.0) - jnp.log(1.0 + jnp.exp(-jnp.abs(z_)))

        z, logf = _rowwise(f"fox{j}_gate", gate_fn, [xb, ('full', wf), ('full', row(fox_b_f[j]))],
                           [('rows', H, F32), ('rows', H, F32)], S)
        c = _cumsum_rows(f"fox{j}_cumsum", logf, False)
        crow = c.T.reshape(H, 1, S)
        o = _fox_fwd(qkv, c, crow, H)
        m = _mm_row(f"fox{j}_wo", o, WG['dm'], wo_base + j * Ds, Ds)
        return m, (qkv, z, c, crow, o, wf)

    def fox_bwd(j, xb, saved, dzb):
        qkv, z, c, crow, o, wf = saved
        DG['dm'] = _mm_dw(f"fox{j}_dw_o", o, dzb, DG['dm'], wo_base + j * Ds, 'row')
        do = _mm_row_t(f"fox{j}_do", dzb, WG['dm'], wo_base + j * Ds, Ds, BF16)
        dq, dk, dv, dcrow = _fox_bwd(qkv, c, crow, do, H)
        dqkv = jnp.concatenate([dq, dk, dv], axis=1)
        dlogf = _cumsum_rows(f"fox{j}_rcumsum", dcrow.reshape(H, S).T, True)

        def fn(x_, dl_, z_, w_):
            dz_ = dl_ * _sigmoid(-z_)
            dzb_ = dz_.astype(BF16)
            return _dot_nt(dzb_, w_), _dot_tn(x_, dzb_), jnp.sum(dz_, axis=0, keepdims=True)

        dh_f, dwf, dbf = _rowwise(f"fox{j}_gate_bwd", fn, [xb, dlogf, z, ('full', wf)],
                                  [('rows', D, F32), ('acc', (D, H), F32), ('acc', (1, H), F32)], S)
        SG.setdefault('fox_w_f', [None] * N_FOX)[j] = dwf
        SG.setdefault('fox_b_f', [None] * N_FOX)[j] = dbf.reshape(H)
        DG['qkv'] = _mm_dw(f"fox{j}_dw_qkv", xb, dqkv, DG['qkv'], j * D, 'col')
        dh = _mm_col_t(f"fox{j}_dx", [(dqkv, j * D)], WG['qkv'], D)
        return [dh, dh_f]

    def rel_fwd(xb):
        qkv = _mm_col("rel_qkv", xb, WG['qkv'], N_FOX * D)
        rb_pad = jnp.pad(rel_bias[0], ((0, 0), (0, REL_TABLE_PAD - REL_TABLE)))
        bias = _rel_window_bias(jnp.transpose(_rel_expand(rb_pad), (1, 0, 2)))
        o = _rel_fwd(qkv, bias, H)
        m = _mm_row("rel_wo", o, WG['dm'], wo_base + N_FOX * Ds, Ds)
        return m, (qkv, bias, o)

    def rel_bwd(xb, saved, dzb):
        qkv, bias, o = saved
        DG['dm'] = _mm_dw("rel_dw_o", o, dzb, DG['dm'], wo_base + N_FOX * Ds, 'row')
        do = _mm_row_t("rel_do", dzb, WG['dm'], wo_base + N_FOX * Ds, Ds, BF16)
        dq, dk, dv, dbias = _rel_bwd(qkv, bias, do, H)
        SG['rel_bias'] = _rel_reduce(jnp.transpose(dbias, (1, 0, 2)))[:, :REL_TABLE].reshape(1, H, REL_TABLE)
        dqkv = jnp.concatenate([dq, dk, dv], axis=1)
        DG['qkv'] = _mm_dw("rel_dw_qkv", xb, dqkv, DG['qkv'], N_FOX * D, 'col')
        return [_mm_col_t("rel_dx", [(dqkv, N_FOX * D)], WG['qkv'], D)]

    w_dw32 = jnp.pad(full['conv_w_dw'][0], ((0, CONV_HALO - CONV_K), (0, 0)))
    cg, cb = full['conv_ln_g'], full['conv_ln_b']

    def conv_fwd(xb):
        u = _mm_col("conv_pw1", xb, WG['pw1'], 0, bias=full['conv_b_pw1'], out_dtype=F32)
        u2, = _rowwise("conv_glu", lambda a_, g_: [a_ * _sigmoid(g_)],
                       [('cols', u, D, 0), ('cols', u, D, 1)], [('rows', D, F32)], S)
        yc = _dwconv("conv_dw", u2, w_dw32, full['conv_b_dw'], False)

        def fn(y_, g_, b_):
            xhat, rstd = _ln_stats(y_)
            ln = xhat * g_ + b_
            return ln * _sigmoid(ln), xhat, rstd

        zc, xhat, rstd = _rowwise("conv_ln_silu", fn, [yc, ('full', cg), ('full', cb)],
                                  [('rows', D, BF16), ('rows', D, F32), ('rows', 1, F32)], S)
        m = _mm_row("conv_pw2", zc, WG['dm'], wo_base + (N_FOX + 1) * Ds, Ds, bias=full['conv_b_pw2'])
        return m, (u, u2, zc, xhat, rstd)

    def conv_bwd(xb, saved, dz, dzb):
        u, u2, zc, xhat, rstd = saved
        r0 = wo_base + (N_FOX + 1) * Ds
        DG['dm'] = _mm_dw("conv_dw_pw2", zc, dzb, DG['dm'], r0, 'row')
        dzc = _mm_row_t("conv_dzc", dzb, WG['dm'], r0, Ds, F32)

        def fn(dm_, dzc_, xh_, rs_, g_, b_):
            ln = xh_ * g_ + b_
            sg = _sigmoid(ln)
            dln = dzc_ * (sg * (1.0 + ln * (1.0 - sg)))
            dyc = _ln_bwd_core(dln, xh_, rs_, g_)
            col = lambda t: jnp.sum(t, axis=0, keepdims=True)
            return dyc, col(dm_), col(dln * xh_), col(dln), col(dyc)

        dyc, SG['conv_b_pw2'], SG['conv_ln_g'], SG['conv_ln_b'], SG['conv_b_dw'] = _rowwise(
            "conv_ln_silu_bwd", fn, [dz, dzc, xhat, rstd, ('full', cg), ('full', cb)],
            [('rows', D, F32)] + [('acc', (1, D), F32)] * 4, S)
        du2 = _dwconv("conv_dw_bwd_x", dyc, w_dw32, jnp.zeros((1, D), F32), True)
        SG['conv_w_dw'] = _dwconv_dw(u2, dyc)[:CONV_K].reshape(1, CONV_K, D)

        def fn2(du2_, a_, g_):
            sg = _sigmoid(g_)
            da, dgt = du2_ * sg, du2_ * a_ * sg * (1.0 - sg)
            return da, dgt, jnp.sum(da, axis=0, keepdims=True), jnp.sum(dgt, axis=0, keepdims=True)

        da, dgt, dba, dbg = _rowwise("conv_glu_bwd", fn2, [du2, ('cols', u, D, 0), ('cols', u, D, 1)],
                                     [('rows', D, BF16), ('rows', D, BF16), ('acc', (1, D), F32), ('acc', (1, D), F32)], S)
        SG['conv_b_pw1'] = jnp.concatenate([dba, dbg], axis=1)
        du = jnp.concatenate([da, dgt], axis=1)
        DG['pw1'] = _mm_dw("conv_dw_pw1", xb, du, DG['pw1'], 0, 'col')
        return [_mm_col_t("conv_dx", [(du, 0)], WG['pw1'], D)]

    xs = x[0]
    xs_b = xs.astype(BF16)
    tape = []

    def weights_ready(s, after):
        bufs = _gather_wait(f"gather_wait{s}", [WG[g] for g in GROUPS], gather_ssems[s], gather_rsems[s], stages[s], after)
        WG.update(zip(GROUPS, _gather_forward(f"gather_fwd{s}", bufs, stages[s])))

    for i in range(DEPTH):
        kind, j = i % 3, i // 3
        weights_ready(2 * i, xs)
        if kind == 0:
            m, msaved = fox_fwd(j, xs_b)
        elif kind == 1:
            m, msaved = rel_fwd(xs_b)
        else:
            m, msaved = conv_fwd(xs_b)
        xm, xm_b, xhat1, rstd1 = _ln_fwd(f"ln_mix{i}", xs, m, row(ln_mix_g[i]), row(ln_mix_b[i]))
        weights_ready(2 * i + 1, xm)
        f, fsaved = ffn_fwd(i, xm_b)
        xo, xo_b, xhat2, rstd2 = _ln_fwd(f"ln_ffn{i}", xm, f, row(ln_ffn_g[i]), row(ln_ffn_b[i]))
        tape.append((xs_b, msaved, xhat1, rstd1, xm_b, fsaved, xhat2, rstd2))
        xs, xs_b = xo, xo_b

    def loss_fn(y_, t_):
        e = y_ - t_
        return e * (1.0 / D), jnp.sum(e * e, axis=0, keepdims=True)

    dy, sq = _rowwise("loss", loss_fn, [xs, loss_target[0]], [('rows', D, F32), ('acc', (1, D), F32)], S)
    loss = lax.psum(jnp.sum(sq) * (0.5 / D), ("x", "y", "c"))

    GF = {g: lax.empty(WG[g].shape[1:], F32) for g in GROUPS}
    started = [None] * len(stages)

    def reduce_start(s):
        dgs = [DG[g] for g in GROUPS]
        lands = _swap_halves(f"pair_swap{s}", dgs, stages[s])
        pbs = [_pair_sum(f"pair_sum{s}_{pi}", dgs[g], lands[pi], place, r0, n) for pi, (g, r0, n) in enumerate(stages[s])]
        started[s] = _scatter_start(f"scatter_start{s}", pbs)
        token = started[s][2][0]
        if s + 1 < len(stages):
            reduce_finish(s + 1, token)
        return token

    def reduce_finish(s, after):
        ssem, rsem, pbs, lands2 = started[s]
        pbs, lands2 = _scatter_wait(f"scatter_wait{s}", ssem, rsem, pbs, lands2, after)
        for pi, (g, r0, n) in enumerate(stages[s]):
            GF[GROUPS[g]] = _chip_sum(f"chip_sum{s}_{pi}", pbs[pi], lands2[pi], place, GF[GROUPS[g]], r0, n)

    terms = [(dy, 1.0)]
    token = None
    g_mix, b_mix, g_ffn, b_ffn = [None] * DEPTH, [None] * DEPTH, [None] * DEPTH, [None] * DEPTH
    for i in reversed(range(DEPTH)):
        kind, j = i % 3, i // 3
        xin_b, msaved, xhat1, rstd1, xm_b, fsaved, xhat2, rstd2 = tape[i]
        dz2, dz2b, g_ffn[i], b_ffn[i] = _ln_bwd(f"ln_ffn{i}_bwd", terms, xhat2, rstd2, row(ln_ffn_g[i]), after=token)
        dx_ffn = ffn_bwd(i, xm_b, fsaved, dz2b)
        token = reduce_start(2 * i + 1)
        dz1, dz1b, g_mix[i], b_mix[i] = _ln_bwd(f"ln_mix{i}_bwd", [(dz2, ALPHA), (dx_ffn, 1.0)], xhat1, rstd1,
                                                row(ln_mix_g[i]), after=token)
        if kind == 0:
            mix_terms = fox_bwd(j, xin_b, msaved, dz1b)
        elif kind == 1:
            mix_terms = rel_bwd(xin_b, msaved, dz1b)
        else:
            mix_terms = conv_bwd(xin_b, msaved, dz1, dz1b)
        terms = [(dz1, ALPHA)] + [(t, 1.0) for t in mix_terms]
        token = reduce_start(2 * i)

    def gx_fn(*v):
        acc = v[0] * ALPHA
        for t in v[1:]:
            acc = acc + t
        return [acc]

    grad_x, = _rowwise("grad_x", gx_fn, [a for a, _ in terms], [('rows', D, F32)], S)
    grad_x = grad_x.reshape(1, S, D)

    SG['fox_w_f'] = jnp.stack(SG['fox_w_f'])
    SG['fox_b_f'] = jnp.stack(SG['fox_b_f'])
    SG['ln_mix_g'] = jnp.concatenate(g_mix, axis=0)
    SG['ln_mix_b'] = jnp.concatenate(b_mix, axis=0)
    SG['ln_ffn_g'] = jnp.concatenate(g_ffn, axis=0)
    SG['ln_ffn_b'] = jnp.concatenate(b_ffn, axis=0)

    grads, deltas, new_m, new_v = {}, {}, {}, {}

    reduce_finish(0, grad_x)
    links_idle = GF['dm']
    for gi, g in enumerate(GROUPS):
        GF[g] = _share_halves("pair_share_" + g, GF[g], [(r0, n) for st in stages for (pg, r0, n) in st if pg == gi])

    for n in BIG:
        g, r0 = where[n]
        outs = _adamw("adamw_" + n, flat2(Wt[n]), GF[g], r0, flat2(Mo[n]), flat2(Vo[n]))
        grads[n], deltas[n], new_m[n], new_v[n] = [o.reshape(Wt[n].shape) for o in outs]

    full_shapes = [SG[n].shape for n in SMALL]
    summed = _sum_slabs("small_sum", _broadcast_small("small_exchange", _pack([SG[n] for n in SMALL]), after=links_idle))
    gsm = dict(zip(SMALL, _unpack(summed, full_shapes)))
    for n in SMALL_SHARDED:
        ax = SMALL_SHARD_AXIS[n]
        width = Wt[n].shape[ax]
        gsm[n] = lax.dynamic_slice_in_dim(gsm[n], my_chip * width, width, axis=ax)
    own_shapes = [Wt[n].shape for n in SMALL]
    packed = [_pack([src[n] for n in SMALL]) for src in (Wt, gsm, Mo, Vo)]
    rows_small = packed[0].shape[0]

    def small_fn(w_, g_, m_, v_):
        return _adamw_math(w_, g_, m_, v_)

    sd, sm, sv = _rowwise("adamw_small", small_fn, packed, [('rows', LANES, F32)] * 3, rows_small, tm=rows_small)
    for n, d_, m_, v_ in zip(SMALL, _unpack(sd, own_shapes), _unpack(sm, own_shapes), _unpack(sv, own_shapes)):
        grads[n], deltas[n], new_m[n], new_v[n] = gsm[n], d_, m_, v_

    return (loss, grad_x, *[grads[n] for n in WEIGHTS], *[deltas[n] for n in WEIGHTS],
            *[new_m[n] for n in WEIGHTS], *[new_v[n] for n in WEIGHTS])
```

```python
import functools
import math

import jax
import jax.numpy as jnp
from jax import lax
from jax.experimental import pallas as pl
from jax.experimental.pallas import tpu as pltpu

F32 = jnp.float32
BF16 = jnp.bfloat16
MESH_IDS = pl.DeviceIdType.MESH
HIGHEST = lax.Precision.HIGHEST

N_CHIPS = 4
DEPTH = 4
N_FOX = 2
HEAD_DIM = 128
CHUNK = 64
LEFT_CHUNKS = 8
BAND_KEYS = (LEFT_CHUNKS + 1) * CHUNK
PAD_KEYS = LEFT_CHUNKS * CHUNK
REL_CLIP = 128
REL_TABLE = 2 * REL_CLIP + 1
REL_TABLE_PAD = 384
REL_QB = 4 * CHUNK
REL_WIN = REL_QB + PAD_KEYS
CONV_K = 31
CONV_HALO = 32
ALPHA = (2.0 * DEPTH) ** 0.25
LN_EPS = 1e-5
ADAM_LR, ADAM_B1, ADAM_B2, ADAM_EPS, ADAM_WD, ADAM_STEP = 0.001, 0.9, 0.999, 1e-08, 0.01, 10
NEG_BIG = -1e30
VMEM_LIMIT_V7X = 56 * 1024 * 1024
LANES = 128
SUBLANES = 8
MM_ROWS = 1024


def _cparams(*sem):
    return pltpu.CompilerParams(dimension_semantics=sem if sem else None, vmem_limit_bytes=VMEM_LIMIT_V7X)


def _pick(dim, target):
    best = None
    for t in range(LANES, min(dim, target) + 1, LANES):
        if dim % t == 0:
            best = t
    return best if best is not None else dim


def _dot_nt(a, b):
    return lax.dot_general(a, b, (((1,), (1,)), ((), ())), preferred_element_type=F32)


def _dot_tn(a, b):
    return lax.dot_general(a, b, (((0,), (0,)), ((), ())), preferred_element_type=F32)


def _sigmoid(z):
    return 1.0 / (1.0 + jnp.exp(-z))


def _rowwise(name, fn, ins, outs, S, tm=256):
    tm = min(tm, S)
    afters = [it[1] for it in ins if isinstance(it, tuple) and it[0] == 'after']
    ins = [it for it in ins if not (isinstance(it, tuple) and it[0] == 'after')]
    arrs, in_specs = [], []
    for it in ins:
        if isinstance(it, tuple) and it[0] == 'full':
            a = it[1]
            in_specs.append(pl.BlockSpec(a.shape, lambda i, _n=a.ndim: (0,) * _n))
        elif isinstance(it, tuple) and it[0] == 'cols':
            _, a, width, blk = it
            in_specs.append(pl.BlockSpec((tm, width), lambda i, _b=blk: (i, _b)))
        elif isinstance(it, tuple) and it[0] == 'off':
            _, a, off = it
            in_specs.append(pl.BlockSpec((tm, a.shape[1]), lambda i, _o=off: (i + _o, 0)))
        else:
            a = it
            in_specs.append(pl.BlockSpec((tm, a.shape[1]), lambda i: (i, 0)))
        arrs.append(a)
    out_shape, out_specs = [], []
    for kind, shp, dt in outs:
        if kind == 'rows':
            out_shape.append(jax.ShapeDtypeStruct((S, shp), dt))
            out_specs.append(pl.BlockSpec((tm, shp), lambda i: (i, 0)))
        else:
            out_shape.append(jax.ShapeDtypeStruct(shp, dt))
            out_specs.append(pl.BlockSpec(shp, lambda i, _n=len(shp): (0,) * _n))
    n_in = len(arrs)
    in_specs += [pl.BlockSpec(memory_space=pl.ANY)] * len(afters)

    def body(*refs):
        vals = fn(*[r[...] for r in refs[:n_in]])
        first = pl.program_id(0) == 0
        for (kind, _, _), r, v in zip(outs, refs[n_in + len(afters):], vals):
            if kind == 'rows':
                r[...] = v.astype(r.dtype)
            else:
                @pl.when(first)
                def _(r=r, v=v):
                    r[...] = v.astype(r.dtype)

                @pl.when(jnp.logical_not(first))
                def _(r=r, v=v):
                    r[...] += v.astype(r.dtype)

    has_acc = any(k != 'rows' for k, _, _ in outs)
    res = pl.pallas_call(
        body, name=name, grid=(S // tm,), in_specs=in_specs, out_specs=out_specs, out_shape=out_shape,
        compiler_params=_cparams("arbitrary" if has_acc else "parallel"),
    )(*arrs, *afters)
    return res


def _mm_col(name, a, wg, row_start, bias=None, out_dtype=BF16):
    S, K = a.shape
    _, _, Ns = wg.shape
    rb = row_start // K
    tm = min(MM_ROWS, S)

    def body(a_ref, w_ref, *rest):
        acc = jnp.dot(a_ref[...].astype(BF16), w_ref[...], preferred_element_type=F32)
        if bias is not None:
            acc = acc + rest[0][...]
        rest[-1][...] = acc.astype(out_dtype)

    in_specs = [pl.BlockSpec((tm, K), lambda j, m: (m, 0)), pl.BlockSpec((None, K, Ns), lambda j, m: (j, rb, 0))]
    args = [a, wg]
    if bias is not None:
        in_specs.append(pl.BlockSpec((1, Ns), lambda j, m: (0, j)))
        args.append(bias)
    return pl.pallas_call(
        body, name=name, grid=(N_CHIPS, S // tm), in_specs=in_specs,
        out_specs=pl.BlockSpec((tm, Ns), lambda j, m: (m, j)),
        out_shape=jax.ShapeDtypeStruct((S, N_CHIPS * Ns), out_dtype),
        compiler_params=_cparams("parallel", "parallel"),
    )(*args)


def _mm_row(name, a, wg, row_start, Ks, bias=None):
    S = a.shape[0]
    N = wg.shape[2]
    rb = row_start // Ks
    tm = min(MM_ROWS, S)

    def body(a_ref, w_ref, *rest):
        o_ref = rest[-1]
        j = pl.program_id(1)
        d = jnp.dot(a_ref[...].astype(BF16), w_ref[...], preferred_element_type=F32)

        @pl.when(j == 0)
        def _():
            o_ref[...] = d + rest[0][...] if bias is not None else d

        @pl.when(j > 0)
        def _():
            o_ref[...] += d

    in_specs = [pl.BlockSpec((tm, Ks), lambda m, j: (m, j)), pl.BlockSpec((None, Ks, N), lambda m, j: (j, rb, 0))]
    args = [a, wg]
    if bias is not None:
        in_specs.append(pl.BlockSpec((1, N), lambda m, j: (0, 0)))
        args.append(bias)
    return pl.pallas_call(
        body, name=name, grid=(S // tm, N_CHIPS), in_specs=in_specs,
        out_specs=pl.BlockSpec((tm, N), lambda m, j: (m, 0)),
        out_shape=jax.ShapeDtypeStruct((S, N), F32),
        compiler_params=_cparams("parallel", "arbitrary"),
    )(*args)


def _mm_col_t(name, pairs, wg, K):
    S = pairs[0][0].shape[0]
    Ns = wg.shape[2]
    tm = min(MM_ROWS, S)
    n = len(pairs)
    tkk = K // n if (K // n) % LANES == 0 else K

    def body(*refs):
        o_ref = refs[-1]
        j = pl.program_id(2)
        d = _dot_nt(refs[0][...], refs[n][...])
        for p in range(1, n):
            d = d + _dot_nt(refs[p][...], refs[n + p][...])

        @pl.when(j == 0)
        def _():
            o_ref[...] = d

        @pl.when(j > 0)
        def _():
            o_ref[...] += d

    in_specs = [pl.BlockSpec((tm, Ns), lambda m, kb, j: (m, j)) for _ in pairs]
    in_specs += [pl.BlockSpec((None, tkk, Ns), lambda m, kb, j, _rb=rs // tkk: (j, _rb + kb, 0)) for _, rs in pairs]
    return pl.pallas_call(
        body, name=name, grid=(S // tm, K // tkk, N_CHIPS), in_specs=in_specs,
        out_specs=pl.BlockSpec((tm, tkk), lambda m, kb, j: (m, kb)),
        out_shape=jax.ShapeDtypeStruct((S, K), F32),
        compiler_params=_cparams("parallel", "parallel", "arbitrary"),
    )(*[dy for dy, _ in pairs], *[wg for _ in pairs])


def _mm_row_t(name, dy, wg, row_start, Ks, out_dtype):
    S, N = dy.shape
    rb = row_start // Ks
    tm = min(MM_ROWS, S)

    def body(dy_ref, w_ref, o_ref):
        o_ref[...] = _dot_nt(dy_ref[...], w_ref[...]).astype(out_dtype)

    return pl.pallas_call(
        body, name=name, grid=(N_CHIPS, S // tm),
        in_specs=[pl.BlockSpec((tm, N), lambda j, m: (m, 0)), pl.BlockSpec((None, Ks, N), lambda j, m: (j, rb, 0))],
        out_specs=pl.BlockSpec((tm, Ks), lambda j, m: (m, j)),
        out_shape=jax.ShapeDtypeStruct((S, N_CHIPS * Ks), out_dtype),
        compiler_params=_cparams("parallel", "parallel"),
    )(dy, wg)


def _silu_parts(g):
    sg = _sigmoid(g)
    return g * sg, sg * (1.0 + g * (1.0 - sg))


def _mm_gate_up(name, a, wg, gate_row, up_row):
    S, K = a.shape
    Ns = wg.shape[2]
    tm = min(MM_ROWS, S)

    def body(a_ref, wg_ref, wu_ref, hg_ref, hu_ref, act_ref):
        a_ = a_ref[...]
        hg = jnp.dot(a_, wg_ref[...], preferred_element_type=F32).astype(BF16)
        hu = jnp.dot(a_, wu_ref[...], preferred_element_type=F32).astype(BF16)
        hg_ref[...] = hg
        hu_ref[...] = hu
        act_ref[...] = (_silu_parts(hg.astype(F32))[0] * hu.astype(F32)).astype(BF16)

    out = jax.ShapeDtypeStruct((S, N_CHIPS * Ns), BF16)
    w_spec = lambda rb: pl.BlockSpec((None, K, Ns), lambda j, m: (j, rb, 0), pipeline_mode=pl.Buffered(1))
    o_spec = pl.BlockSpec((tm, Ns), lambda j, m: (m, j))
    return pl.pallas_call(
        body, name=name, grid=(N_CHIPS, S // tm),
        in_specs=[pl.BlockSpec((tm, K), lambda j, m: (m, 0)), w_spec(gate_row // K), w_spec(up_row // K)],
        out_specs=[o_spec, o_spec, o_spec], out_shape=[out, out, out],
        compiler_params=_cparams("parallel", "parallel"),
    )(a, wg, wg)


def _mm_dact(name, dy, wg, row_start, Ks, hg, hu):
    S, N = dy.shape
    rb = row_start // Ks
    tm = min(512, S)

    def body(dy_ref, w_ref, hg_ref, hu_ref, dhg_ref, dhu_ref):
        dact = _dot_nt(dy_ref[...], w_ref[...])
        silu, dsilu = _silu_parts(hg_ref[...].astype(F32))
        dhg_ref[...] = (dact * hu_ref[...].astype(F32) * dsilu).astype(BF16)
        dhu_ref[...] = (dact * silu).astype(BF16)

    out = jax.ShapeDtypeStruct((S, N_CHIPS * Ks), BF16)
    t_spec = pl.BlockSpec((tm, Ks), lambda j, m: (m, j))
    return pl.pallas_call(
        body, name=name, grid=(N_CHIPS, S // tm),
        in_specs=[pl.BlockSpec((tm, N), lambda j, m: (m, 0)), pl.BlockSpec((None, Ks, N), lambda j, m: (j, rb, 0)), t_spec, t_spec],
        out_specs=[t_spec, t_spec], out_shape=[out, out],
        compiler_params=_cparams("parallel", "parallel"),
    )(dy, wg, hg, hu)


def _mm_dw(name, a, dy, dg, row_start, kind):
    S = a.shape[0]
    _, _, W = dg.shape
    if kind == 'col':
        K = a.shape[1]
        rows = K
        tk, tn = _pick(K, MM_ROWS), W
        a_map = lambda j, nb, kb: (0, kb)
        dy_map = lambda j, nb, kb: (0, j * (W // tn) + nb)
    else:
        rows = a.shape[1] // N_CHIPS
        tk = rows if rows * S * 2 * 2 <= 12 * 1024 * 1024 else _pick(rows, 512)
        tn = _pick(W, 1024)
        a_map = lambda j, nb, kb: (0, j * (rows // tk) + kb)
        dy_map = lambda j, nb, kb: (0, nb)
    rb = row_start // tk
    assert row_start % tk == 0

    def body(a_ref, dy_ref, dg_in, o_ref):
        del dg_in
        o_ref[...] = _dot_tn(a_ref[...], dy_ref[...]).astype(o_ref.dtype)

    return pl.pallas_call(
        body, name=name, grid=(N_CHIPS, W // tn, rows // tk),
        in_specs=[pl.BlockSpec((S, tk), a_map), pl.BlockSpec((S, tn), dy_map), pl.BlockSpec(memory_space=pl.ANY)],
        out_specs=pl.BlockSpec((None, tk, tn), lambda j, nb, kb: (j, rb + kb, nb)),
        out_shape=jax.ShapeDtypeStruct(dg.shape, dg.dtype),
        input_output_aliases={2: 0},
        compiler_params=_cparams("parallel", "parallel", "parallel"),
    )(a, dy, dg)


def _fox_probs(q, k, c_blk, crow, h, qi, tq):
    n = k.shape[0]
    s = _dot_nt(q, k) * (HEAD_DIM ** -0.5)
    lane = lax.broadcasted_iota(jnp.int32, c_blk.shape, 1)
    ccol = jnp.sum(jnp.where(lane == h, c_blk, 0.0), axis=1, keepdims=True)
    s = s + (ccol - crow)
    t_idx = qi * tq + lax.broadcasted_iota(jnp.int32, (tq, n), 0)
    s_idx = lax.broadcasted_iota(jnp.int32, (tq, n), 1)
    s = jnp.where(s_idx <= t_idx, s, NEG_BIG)
    p = jnp.exp(s - jnp.max(s, axis=1, keepdims=True))
    return p * (1.0 / jnp.sum(p, axis=1, keepdims=True))


def _per_query_block(qi, nq, tq, fn):
    for qv in range(nq):
        @pl.when(qi == qv)
        def _(qv=qv):
            fn(qv, (qv + 1) * tq)


def _fox_fwd(qkv, c, crow, H):
    S = qkv.shape[0]
    tq = min(256, S)

    def body(q_ref, k_ref, v_ref, c_ref, crow_ref, o_ref):
        def block(qv, n):
            p = _fox_probs(q_ref[...], k_ref[0:n, :], c_ref[...], crow_ref[:, 0:n], pl.program_id(0), qv, tq)
            o_ref[...] = jnp.dot(p.astype(BF16), v_ref[0:n, :], preferred_element_type=F32).astype(o_ref.dtype)

        _per_query_block(pl.program_id(1), S // tq, tq, block)

    return pl.pallas_call(
        body, name="fox_attn_fwd", grid=(H, S // tq),
        in_specs=[pl.BlockSpec((tq, HEAD_DIM), lambda h, i: (i, h)),
                  pl.BlockSpec((S, HEAD_DIM), lambda h, i: (0, H + h)),
                  pl.BlockSpec((S, HEAD_DIM), lambda h, i: (0, 2 * H + h)),
                  pl.BlockSpec((tq, H), lambda h, i: (i, 0)),
                  pl.BlockSpec((None, 1, S), lambda h, i: (h, 0, 0))],
        out_specs=pl.BlockSpec((tq, HEAD_DIM), lambda h, i: (i, h)),
        out_shape=jax.ShapeDtypeStruct((S, H * HEAD_DIM), BF16),
        compiler_params=_cparams("parallel", "parallel"),
    )(qkv, qkv, qkv, c, crow)


def _fox_bwd(qkv, c, crow, do, H):
    S = qkv.shape[0]
    tq = min(256, S)
    nq = S // tq

    def body(q_ref, k_ref, v_ref, c_ref, crow_ref, do_ref, dq_ref, dk_ref, dv_ref, dc_ref, dk_acc, dv_acc):
        qi = pl.program_id(1)

        @pl.when(qi == 0)
        def _():
            dk_acc[...] = jnp.zeros_like(dk_acc)
            dv_acc[...] = jnp.zeros_like(dv_acc)
            dc_ref[...] = jnp.zeros_like(dc_ref)

        def block(qv, n):
            q, k, v, do_ = q_ref[...], k_ref[0:n, :], v_ref[0:n, :], do_ref[...]
            p = _fox_probs(q, k, c_ref[...], crow_ref[:, 0:n], pl.program_id(0), qv, tq)
            dv_acc[0:n, :] += _dot_tn(p.astype(BF16), do_)
            dp = _dot_nt(do_, v)
            ds = p * (dp - jnp.sum(p * dp, axis=1, keepdims=True))
            dsb = (ds * (HEAD_DIM ** -0.5)).astype(BF16)
            dq_ref[...] = jnp.dot(dsb, k, preferred_element_type=F32).astype(dq_ref.dtype)
            dk_acc[0:n, :] += _dot_tn(dsb, q)
            dc_ref[:, 0:n] += -jnp.sum(ds, axis=0, keepdims=True)

        _per_query_block(qi, nq, tq, block)

        @pl.when(qi == nq - 1)
        def _():
            dk_ref[...] = dk_acc[...].astype(dk_ref.dtype)
            dv_ref[...] = dv_acc[...].astype(dv_ref.dtype)

    D = H * HEAD_DIM
    return pl.pallas_call(
        body, name="fox_attn_bwd", grid=(H, nq),
        in_specs=[pl.BlockSpec((tq, HEAD_DIM), lambda h, i: (i, h)),
                  pl.BlockSpec((S, HEAD_DIM), lambda h, i: (0, H + h)),
                  pl.BlockSpec((S, HEAD_DIM), lambda h, i: (0, 2 * H + h)),
                  pl.BlockSpec((tq, H), lambda h, i: (i, 0)),
                  pl.BlockSpec((None, 1, S), lambda h, i: (h, 0, 0)),
                  pl.BlockSpec((tq, HEAD_DIM), lambda h, i: (i, h))],
        out_specs=[pl.BlockSpec((tq, HEAD_DIM), lambda h, i: (i, h)),
                   pl.BlockSpec((S, HEAD_DIM), lambda h, i: (0, h)),
                   pl.BlockSpec((S, HEAD_DIM), lambda h, i: (0, h)),
                   pl.BlockSpec((None, 1, S), lambda h, i: (h, 0, 0))],
        out_shape=[jax.ShapeDtypeStruct((S, D), BF16), jax.ShapeDtypeStruct((S, D), BF16),
                   jax.ShapeDtypeStruct((S, D), BF16), jax.ShapeDtypeStruct((H, 1, S), F32)],
        scratch_shapes=[pltpu.VMEM((S, HEAD_DIM), F32), pltpu.VMEM((S, HEAD_DIM), F32)],
        compiler_params=_cparams("parallel", "arbitrary"),
    )(qkv, qkv, qkv, c, crow, do)


def _cumsum_rows(name, xin, reverse):
    S, H = xin.shape
    tb = min(256, S)
    nb = S // tb

    def body(x_ref, o_ref):
        r = lax.broadcasted_iota(jnp.int32, (tb, tb), 0)
        cidx = lax.broadcasted_iota(jnp.int32, (tb, tb), 1)
        tri = (r <= cidx if reverse else r >= cidx).astype(F32)

        def step(b, carry):
            bb = nb - 1 - b if reverse else b
            rows = pl.ds(pl.multiple_of(bb * tb, tb), tb)
            blk = x_ref[rows, :]
            o_ref[rows, :] = jnp.dot(tri, blk, precision=HIGHEST, preferred_element_type=F32) + carry
            return carry + jnp.sum(blk, axis=0, keepdims=True)

        lax.fori_loop(0, nb, step, jnp.zeros((1, H), F32))

    return pl.pallas_call(
        body, name=name, out_shape=jax.ShapeDtypeStruct((S, H), F32),
        in_specs=[pl.BlockSpec(memory_space=pltpu.VMEM)], out_specs=pl.BlockSpec(memory_space=pltpu.VMEM),
        compiler_params=_cparams(),
    )(xin)


def _rel_onehot(i, transposed):
    shp = (REL_TABLE_PAD, BAND_KEYS) if transposed else (BAND_KEYS, REL_TABLE_PAD)
    j = lax.broadcasted_iota(jnp.int32, shp, 1 if transposed else 0)
    r = lax.broadcasted_iota(jnp.int32, shp, 0 if transposed else 1)
    return (jnp.clip(PAD_KEYS + i - j, -REL_CLIP, REL_CLIP) + REL_CLIP == r).astype(F32)


def _rel_expand(rb_pad):
    H = rb_pad.shape[0]

    def body(rb_ref, o_ref):
        def step(i, _):
            o_ref[i] = jnp.dot(rb_ref[...], _rel_onehot(i, True), precision=HIGHEST, preferred_element_type=F32)
            return 0
        lax.fori_loop(0, CHUNK, step, 0)

    return pl.pallas_call(
        body, name="rel_bias_expand", out_shape=jax.ShapeDtypeStruct((CHUNK, H, BAND_KEYS), F32),
        in_specs=[pl.BlockSpec(memory_space=pltpu.VMEM)], out_specs=pl.BlockSpec(memory_space=pltpu.VMEM),
        compiler_params=_cparams(),
    )(rb_pad)


def _rel_reduce(dbt):
    H = dbt.shape[1]

    def body(d_ref, o_ref):
        def step(i, acc):
            j = lax.broadcasted_iota(jnp.int32, (REL_WIN, REL_TABLE_PAD), 0)
            r = lax.broadcasted_iota(jnp.int32, (REL_WIN, REL_TABLE_PAD), 1)
            onehot = (jnp.clip(PAD_KEYS + i - j, -REL_CLIP, REL_CLIP) + REL_CLIP == r).astype(BF16)
            d = d_ref[i]
            hi = d.astype(BF16)
            lo = (d - hi.astype(F32)).astype(BF16)
            return acc + (jnp.dot(hi, onehot, preferred_element_type=F32) + jnp.dot(lo, onehot, preferred_element_type=F32))
        o_ref[...] = lax.fori_loop(0, REL_QB, step, jnp.zeros((H, REL_TABLE_PAD), F32))

    return pl.pallas_call(
        body, name="rel_bias_reduce", out_shape=jax.ShapeDtypeStruct((H, REL_TABLE_PAD), F32),
        in_specs=[pl.BlockSpec(memory_space=pltpu.VMEM)], out_specs=pl.BlockSpec(memory_space=pltpu.VMEM),
        compiler_params=_cparams(),
    )(dbt)


def _rel_window_bias(bias):
    H = bias.shape[0]
    out = jnp.full((H, REL_QB, REL_WIN), NEG_BIG, F32)
    for a in range(REL_QB // CHUNK):
        out = out.at[:, a * CHUNK:(a + 1) * CHUNK, a * CHUNK:a * CHUNK + BAND_KEYS].set(bias)
    return out


def _rel_probs(q, kw, bias_w, t0):
    s = _dot_nt(q, kw) * (HEAD_DIM ** -0.5) + bias_w
    j = lax.broadcasted_iota(jnp.int32, (REL_QB, REL_WIN), 1)
    s = jnp.where(j >= PAD_KEYS - t0, s, NEG_BIG)
    p = jnp.exp(s - jnp.max(s, axis=1, keepdims=True))
    return p * (1.0 / jnp.sum(p, axis=1, keepdims=True))


def _rel_fwd(qkv, bias_w, H):
    S = qkv.shape[0]

    def body(q_ref, k_ref, v_ref, b_ref, o_ref, kpad, vpad):
        kpad[0:PAD_KEYS, :] = jnp.zeros((PAD_KEYS, HEAD_DIM), BF16)
        vpad[0:PAD_KEYS, :] = jnp.zeros((PAD_KEYS, HEAD_DIM), BF16)
        kpad[PAD_KEYS:PAD_KEYS + S, :] = k_ref[...]
        vpad[PAD_KEYS:PAD_KEYS + S, :] = v_ref[...]

        def block(n, _):
            t0 = pl.multiple_of(n * REL_QB, REL_QB)
            rows, win = pl.ds(t0, REL_QB), pl.ds(t0, REL_WIN)
            p = _rel_probs(q_ref[rows, :], kpad[win, :], b_ref[...], t0)
            o_ref[rows, :] = jnp.dot(p.astype(BF16), vpad[win, :], preferred_element_type=F32).astype(o_ref.dtype)
            return 0

        lax.fori_loop(0, S // REL_QB, block, 0)

    return pl.pallas_call(
        body, name="rel_attn_fwd", grid=(H,),
        in_specs=[pl.BlockSpec((S, HEAD_DIM), lambda h: (0, h)),
                  pl.BlockSpec((S, HEAD_DIM), lambda h: (0, H + h)),
                  pl.BlockSpec((S, HEAD_DIM), lambda h: (0, 2 * H + h)),
                  pl.BlockSpec((None, REL_QB, REL_WIN), lambda h: (h, 0, 0))],
        out_specs=pl.BlockSpec((S, HEAD_DIM), lambda h: (0, h)),
        out_shape=jax.ShapeDtypeStruct((S, H * HEAD_DIM), BF16),
        scratch_shapes=[pltpu.VMEM((S + PAD_KEYS, HEAD_DIM), BF16), pltpu.VMEM((S + PAD_KEYS, HEAD_DIM), BF16)],
        compiler_params=_cparams("parallel"),
    )(qkv, qkv, qkv, bias_w)


def _rel_bwd(qkv, bias_w, do, H):
    S = qkv.shape[0]
    D = H * HEAD_DIM

    def body(q_ref, k_ref, v_ref, b_ref, do_ref, dq_ref, dk_ref, dv_ref, db_ref, kpad, vpad, dkpad, dvpad):
        kpad[0:PAD_KEYS, :] = jnp.zeros((PAD_KEYS, HEAD_DIM), BF16)
        vpad[0:PAD_KEYS, :] = jnp.zeros((PAD_KEYS, HEAD_DIM), BF16)
        kpad[PAD_KEYS:PAD_KEYS + S, :] = k_ref[...]
        vpad[PAD_KEYS:PAD_KEYS + S, :] = v_ref[...]
        dkpad[...] = jnp.zeros_like(dkpad)
        dvpad[...] = jnp.zeros_like(dvpad)
        db_ref[...] = jnp.zeros_like(db_ref)

        def block(n, _):
            t0 = pl.multiple_of(n * REL_QB, REL_QB)
            rows, win = pl.ds(t0, REL_QB), pl.ds(t0, REL_WIN)
            q, kw, vw, do_ = q_ref[rows, :], kpad[win, :], vpad[win, :], do_ref[rows, :]
            p = _rel_probs(q, kw, b_ref[...], t0)
            dvpad[win, :] += _dot_tn(p.astype(BF16), do_)
            dp = _dot_nt(do_, vw)
            ds = p * (dp - jnp.sum(p * dp, axis=1, keepdims=True))
            db_ref[...] += ds
            dsb = (ds * (HEAD_DIM ** -0.5)).astype(BF16)
            dq_ref[rows, :] = jnp.dot(dsb, kw, preferred_element_type=F32).astype(dq_ref.dtype)
            dkpad[win, :] += _dot_tn(dsb, q)
            return 0

        lax.fori_loop(0, S // REL_QB, block, 0)
        dk_ref[...] = dkpad[PAD_KEYS:PAD_KEYS + S, :].astype(dk_ref.dtype)
        dv_ref[...] = dvpad[PAD_KEYS:PAD_KEYS + S, :].astype(dv_ref.dtype)

    head = lambda h: (0, h)
    return pl.pallas_call(
        body, name="rel_attn_bwd", grid=(H,),
        in_specs=[pl.BlockSpec((S, HEAD_DIM), head),
                  pl.BlockSpec((S, HEAD_DIM), lambda h: (0, H + h)),
                  pl.BlockSpec((S, HEAD_DIM), lambda h: (0, 2 * H + h)),
                  pl.BlockSpec((None, REL_QB, REL_WIN), lambda h: (h, 0, 0)),
                  pl.BlockSpec((S, HEAD_DIM), head)],
        out_specs=[pl.BlockSpec((S, HEAD_DIM), head), pl.BlockSpec((S, HEAD_DIM), head), pl.BlockSpec((S, HEAD_DIM), head),
                   pl.BlockSpec((None, REL_QB, REL_WIN), lambda h: (h, 0, 0))],
        out_shape=[jax.ShapeDtypeStruct((S, D), BF16), jax.ShapeDtypeStruct((S, D), BF16), jax.ShapeDtypeStruct((S, D), BF16),
                   jax.ShapeDtypeStruct((H, REL_QB, REL_WIN), F32)],
        scratch_shapes=[pltpu.VMEM((S + PAD_KEYS, HEAD_DIM), BF16), pltpu.VMEM((S + PAD_KEYS, HEAD_DIM), BF16),
                        pltpu.VMEM((S + PAD_KEYS, HEAD_DIM), F32), pltpu.VMEM((S + PAD_KEYS, HEAD_DIM), F32)],
        compiler_params=_cparams("parallel"),
    )(qkv, qkv, qkv, bias_w, do)


def _conv_taps(win, tt, reverse):
    n = tt + 2 * CONV_HALO
    for k in range(CONV_K):
        off = (CONV_K - 1 - k) if reverse else (k - (CONV_K - 1))
        sh = (-off) % n
        rolled = pltpu.roll(win, sh, 0) if sh else win
        yield k, rolled[CONV_HALO:CONV_HALO + tt, :]


def _fill_padded(pad_ref, x_ref, S):
    tc = pad_ref.shape[1]
    pad_ref[0:CONV_HALO, :] = jnp.zeros((CONV_HALO, tc), F32)
    pad_ref[CONV_HALO + S:CONV_HALO + S + CONV_HALO, :] = jnp.zeros((CONV_HALO, tc), F32)
    pad_ref[CONV_HALO:CONV_HALO + S, :] = x_ref[...]


def _dwconv(name, xin, w32, bias, reverse):
    S, D = xin.shape
    tc = min(256, D)
    tt = min(256, S)

    def body(x_ref, w_ref, b_ref, y_ref, pad_ref):
        _fill_padded(pad_ref, x_ref, S)
        def tile(ti, _):
            t0 = pl.multiple_of(ti * tt, tt)
            win = pad_ref[pl.ds(t0, tt + 2 * CONV_HALO), :]
            acc = jnp.zeros((tt, tc), F32) + b_ref[...]
            for k, shifted in _conv_taps(win, tt, reverse):
                acc = acc + w_ref[pl.ds(k, 1), :] * shifted
            y_ref[pl.ds(t0, tt), :] = acc
            return 0

        lax.fori_loop(0, S // tt, tile, 0)

    return pl.pallas_call(
        body, name=name, grid=(D // tc,),
        in_specs=[pl.BlockSpec((S, tc), lambda i: (0, i)), pl.BlockSpec((CONV_HALO, tc), lambda i: (0, i)),
                  pl.BlockSpec((1, tc), lambda i: (0, i))],
        out_specs=pl.BlockSpec((S, tc), lambda i: (0, i)),
        out_shape=jax.ShapeDtypeStruct((S, D), F32),
        scratch_shapes=[pltpu.VMEM((S + 2 * CONV_HALO, tc), F32)],
        compiler_params=_cparams("parallel"),
    )(xin, w32, bias)


def _dwconv_dw(xin, dy):
    S, D = xin.shape
    tc = min(256, D)
    tt = min(256, S)

    def body(x_ref, dy_ref, o_ref, pad_ref):
        _fill_padded(pad_ref, x_ref, S)

        def tile(ti, acc):
            t0 = pl.multiple_of(ti * tt, tt)
            win = pad_ref[pl.ds(t0, tt + 2 * CONV_HALO), :]
            dyt = dy_ref[pl.ds(t0, tt), :]
            ridx = lax.broadcasted_iota(jnp.int32, (CONV_HALO, tc), 0)
            upd = jnp.zeros((CONV_HALO, tc), F32)
            for k, shifted in _conv_taps(win, tt, False):
                upd = jnp.where(ridx == k, jnp.sum(dyt * shifted, axis=0, keepdims=True), upd)
            return acc + upd

        o_ref[...] = lax.fori_loop(0, S // tt, tile, jnp.zeros((CONV_HALO, tc), F32))

    return pl.pallas_call(
        body, name="dwconv_dw", grid=(D // tc,),
        in_specs=[pl.BlockSpec((S, tc), lambda i: (0, i)), pl.BlockSpec((S, tc), lambda i: (0, i))],
        out_specs=pl.BlockSpec((CONV_HALO, tc), lambda i: (0, i)),
        out_shape=jax.ShapeDtypeStruct((CONV_HALO, D), F32),
        scratch_shapes=[pltpu.VMEM((S + 2 * CONV_HALO, tc), F32)],
        compiler_params=_cparams("parallel"),
    )(xin, dy)


def _place():
    x, y, c = lax.axis_index("x"), lax.axis_index("y"), lax.axis_index("c")
    chips = [(1 - x, y), (x, 1 - y), (1 - x, 1 - y)]
    return x, y, c, chips


def _remote(src, dst, ssem, rsem, dev):
    return pltpu.make_async_remote_copy(src_ref=src, dst_ref=dst, send_sem=ssem, recv_sem=rsem,
                                        device_id=dev, device_id_type=MESH_IDS)


_ANY = pl.BlockSpec(memory_space=pl.ANY)


def _place_own(name, own, place):
    R, W = own.shape
    tr = _pick_rows(R)

    def body(p_ref, a_ref, o_ref):
        del p_ref
        o_ref[...] = a_ref[...]

    return pl.pallas_call(
        body, name=name,
        grid_spec=pltpu.PrefetchScalarGridSpec(
            num_scalar_prefetch=1, grid=(R // tr,),
            in_specs=[pl.BlockSpec((tr, W), lambda i, p: (i, 0))],
            out_specs=pl.BlockSpec((None, tr, W), lambda i, p: (p[1], i, 0))),
        out_shape=jax.ShapeDtypeStruct((N_CHIPS, R, W), own.dtype),
        compiler_params=_cparams("parallel"),
    )(place, own)


_HBM = pl.BlockSpec(memory_space=pltpu.HBM)
_SEM = pl.BlockSpec(memory_space=pltpu.SEMAPHORE)
GROUPS = ('qkv', 'ffn', 'pw1', 'dm')


def _half_rows(c, r0, n):
    return pl.ds(pl.multiple_of(r0 + c * (n // 2), SUBLANES), n // 2)


def _gather_start(name, wgs, layers, after):
    G, L = len(wgs), len(layers)

    def body(*refs):
        outs = refs[G + 1:]
        ssems, rsems, bufs = outs[:L], outs[L:2 * L], outs[2 * L:]
        x, y, c, chips = _place()
        me = 2 * x + y
        for li, pieces in enumerate(layers):
            for pi, (g, r0, n) in enumerate(pieces):
                blk = bufs[g].at[me, _half_rows(c, r0, n)]
                for j, (px, py) in enumerate(chips):
                    _remote(blk, blk, ssems[li].at[3 * pi + j], rsems[li].at[3 * pi + j], (px, py, c)).start()

    sem_shapes = [pltpu.SemaphoreType.DMA((3 * len(p),)) for p in layers]
    res = pl.pallas_call(
        body, name=name, in_specs=[_HBM] * G + [_ANY],
        out_specs=[_SEM] * (2 * L) + [_HBM] * G,
        out_shape=sem_shapes + sem_shapes + [pltpu.HBM(w.shape, w.dtype) for w in wgs],
        input_output_aliases={g: 2 * L + g for g in range(G)},
        compiler_params=pltpu.CompilerParams(has_side_effects=pltpu.SideEffectType.DATAFLOW_SIDE_EFFECTING),
    )(*[pltpu.with_memory_space_constraint(w, pltpu.HBM) for w in wgs], after)
    return res[:L], res[L:2 * L], list(res[2 * L:])


def _gather_wait(name, wgs, ssem, rsem, pieces, after):
    G = len(wgs)

    def body(*refs):
        ssem_ref, rsem_ref = refs[G], refs[G + 1]
        bufs = refs[G + 3:]
        x, y, c, chips = _place()
        me = 2 * x + y
        for pi, (g, r0, n) in enumerate(pieces):
            rows = _half_rows(c, r0, n)
            for j, (px, py) in enumerate(chips):
                cp = _remote(bufs[g].at[me, rows], bufs[g].at[2 * px + py, rows],
                             ssem_ref.at[3 * pi + j], rsem_ref.at[3 * pi + j], (px, py, c))
                cp.wait_send()
                cp.wait_recv()

    return list(pl.pallas_call(
        body, name=name, in_specs=[_HBM] * G + [_SEM, _SEM, _ANY], out_specs=[_HBM] * G,
        out_shape=[pltpu.HBM(w.shape, w.dtype) for w in wgs],
        input_output_aliases={g: g for g in range(G)},
        compiler_params=pltpu.CompilerParams(has_side_effects=pltpu.SideEffectType.DATAFLOW_SIDE_EFFECTING),
    )(*wgs, ssem, rsem, after))


def _gather_forward(name, wgs, pieces):
    G = len(wgs)
    n_cp = 3 * len(pieces)

    def body(*refs):
        bufs, ssems, rsems = refs[G:2 * G], refs[2 * G], refs[2 * G + 1]
        x, y, c, chips = _place()
        sib = (x, y, 1 - c)
        cps = []
        for pi, (g, r0, n) in enumerate(pieces):
            for j, (px, py) in enumerate(chips):
                blk = bufs[g].at[2 * px + py, _half_rows(c, r0, n)]
                cps.append(_remote(blk, blk, ssems.at[3 * pi + j], rsems.at[3 * pi + j], sib))
        for cp in cps:
            cp.start()
        for pi, (g, r0, n) in enumerate(pieces):
            for j, (px, py) in enumerate(chips):
                blk = bufs[g].at[2 * px + py, _half_rows(1 - c, r0, n)]
                _remote(blk, blk, ssems.at[3 * pi + j], rsems.at[3 * pi + j], sib).wait_recv()
        for cp in cps:
            cp.wait_send()

    return list(pl.pallas_call(
        body, name=name, in_specs=[_ANY] * G, out_specs=[_ANY] * G,
        out_shape=[jax.ShapeDtypeStruct(w.shape, w.dtype) for w in wgs],
        input_output_aliases={g: g for g in range(G)},
        scratch_shapes=[pltpu.SemaphoreType.DMA((n_cp,)), pltpu.SemaphoreType.DMA((n_cp,))],
        compiler_params=pltpu.CompilerParams(has_side_effects=True),
    )(*wgs))


def _swap_halves(name, dgs, pieces):
    G = len(dgs)
    n_cp = N_CHIPS * len(pieces)

    def body(*refs):
        srcs, lands, ssems, rsems = refs[:G], refs[G:G + len(pieces)], refs[-2], refs[-1]
        x, y, c, _ = _place()
        cps = [_remote(srcs[g].at[j, _half_rows(1 - c, r0, n)], lands[pi].at[j],
                       ssems.at[N_CHIPS * pi + j], rsems.at[N_CHIPS * pi + j], (x, y, 1 - c))
               for pi, (g, r0, n) in enumerate(pieces) for j in range(N_CHIPS)]
        for cp in cps:
            cp.start()
        for cp in cps:
            cp.wait()

    return list(pl.pallas_call(
        body, name=name, in_specs=[_ANY] * G, out_specs=[_ANY] * len(pieces),
        out_shape=[jax.ShapeDtypeStruct((N_CHIPS, n // 2, dgs[g].shape[2]), dgs[g].dtype) for g, _, n in pieces],
        scratch_shapes=[pltpu.SemaphoreType.DMA((n_cp,)), pltpu.SemaphoreType.DMA((n_cp,))],
        compiler_params=pltpu.CompilerParams(has_side_effects=True),
    )(*dgs))


def _scatter_start(name, pbs):
    P = len(pbs)

    def body(*refs):
        outs = refs[2 * P:]
        ssems, rsems, src, land = outs[0], outs[1], outs[2:2 + P], outs[2 + P:]
        x, y, c, chips = _place()
        me = 2 * x + y
        for pi in range(P):
            for j, (px, py) in enumerate(chips):
                _remote(src[pi].at[2 * px + py], land[pi].at[me], ssems.at[3 * pi + j], rsems.at[3 * pi + j], (px, py, c)).start()

    sems = pltpu.SemaphoreType.DMA((3 * P,))
    hbm = [pltpu.HBM(p.shape, p.dtype) for p in pbs]
    res = pl.pallas_call(
        body, name=name, in_specs=[_HBM] * (2 * P), out_specs=[_SEM, _SEM] + [_HBM] * (2 * P),
        out_shape=[sems, sems] + hbm + hbm,
        input_output_aliases={k: 2 + k for k in range(2 * P)},
        compiler_params=pltpu.CompilerParams(has_side_effects=pltpu.SideEffectType.DATAFLOW_SIDE_EFFECTING),
    )(*[pltpu.with_memory_space_constraint(p, pltpu.HBM) for p in pbs],
      *[pltpu.with_memory_space_constraint(lax.empty(p.shape, p.dtype), pltpu.HBM) for p in pbs])
    return res[0], res[1], list(res[2:2 + P]), list(res[2 + P:])


def _scatter_wait(name, ssem, rsem, pbs, lands, after):
    P = len(pbs)

    def body(*refs):
        ssems, rsems = refs[2 * P], refs[2 * P + 1]
        outs = refs[2 * P + 3:]
        src, land = outs[:P], outs[P:]
        x, y, c, chips = _place()
        for pi in range(P):
            for j, (px, py) in enumerate(chips):
                cp = _remote(src[pi].at[2 * px + py], land[pi].at[2 * px + py], ssems.at[3 * pi + j], rsems.at[3 * pi + j], (px, py, c))
                cp.wait_send()
                cp.wait_recv()

    hbm = [pltpu.HBM(p.shape, p.dtype) for p in pbs]
    res = pl.pallas_call(
        body, name=name, in_specs=[_HBM] * (2 * P) + [_SEM, _SEM, _ANY], out_specs=[_HBM] * (2 * P),
        out_shape=hbm + hbm, input_output_aliases={k: k for k in range(2 * P)},
        compiler_params=pltpu.CompilerParams(has_side_effects=pltpu.SideEffectType.DATAFLOW_SIDE_EFFECTING),
    )(*pbs, *lands, ssem, rsem, after)
    return list(res[:P]), list(res[P:])


def _share_halves(name, gf, pieces):
    def body(in_ref, out, ssems, rsems):
        del in_ref
        x, y, c, _ = _place()
        cps = []
        for pi, (r0, n) in enumerate(pieces):
            mine = out.at[_half_rows(c, r0, n)]
            cps.append(_remote(mine, mine, ssems.at[pi], rsems.at[pi], (x, y, 1 - c)))
        for cp in cps:
            cp.start()
        for pi, (r0, n) in enumerate(pieces):
            theirs = out.at[_half_rows(1 - c, r0, n)]
            _remote(theirs, theirs, ssems.at[pi], rsems.at[pi], (x, y, 1 - c)).wait_recv()
        for cp in cps:
            cp.wait_send()

    return pl.pallas_call(
        body, name=name, in_specs=[_ANY], out_specs=_ANY,
        out_shape=jax.ShapeDtypeStruct(gf.shape, gf.dtype), input_output_aliases={0: 0},
        scratch_shapes=[pltpu.SemaphoreType.DMA((len(pieces),)), pltpu.SemaphoreType.DMA((len(pieces),))],
        compiler_params=pltpu.CompilerParams(has_side_effects=True),
    )(gf)


def _broadcast_small(name, buf, after=None):
    R = buf.shape[0]

    def body(src, *rest):
        out, ssems, rsems = rest[-3:]
        x, y, c, _ = _place()
        me = 4 * x + 2 * y + c
        out[me] = src[...]
        peers = []
        for mask in range(1, 8):
            fx, fy, fc = (mask >> 2) & 1, (mask >> 1) & 1, mask & 1
            peers.append((1 - x if fx else x, 1 - y if fy else y, 1 - c if fc else c))
        cps = [_remote(src, out.at[me], ssems.at[k], rsems.at[k], p) for k, p in enumerate(peers)]
        for cp in cps:
            cp.start()
        for k, (px, py, pc) in enumerate(peers):
            blk = out.at[4 * px + 2 * py + pc]
            _remote(blk, blk, ssems.at[k], rsems.at[k], (px, py, pc)).wait_recv()
        for cp in cps:
            cp.wait_send()

    return pl.pallas_call(
        body, name=name, in_specs=[pl.BlockSpec(memory_space=pltpu.VMEM)] + ([_ANY] if after is not None else []),
        out_specs=pl.BlockSpec(memory_space=pltpu.VMEM),
        out_shape=jax.ShapeDtypeStruct((8, R, LANES), F32),
        scratch_shapes=[pltpu.SemaphoreType.DMA((7,)), pltpu.SemaphoreType.DMA((7,))],
        compiler_params=pltpu.CompilerParams(has_side_effects=True, vmem_limit_bytes=VMEM_LIMIT_V7X),
    )(buf, *([after] if after is not None else []))


def _sum_slabs(name, slabs):
    n, R, _ = slabs.shape

    def body(s_ref, o_ref):
        acc = s_ref[0]
        for k in range(1, n):
            acc = acc + s_ref[k]
        o_ref[...] = acc

    return pl.pallas_call(
        body, name=name, out_shape=jax.ShapeDtypeStruct((R, LANES), F32),
        in_specs=[pl.BlockSpec(memory_space=pltpu.VMEM)], out_specs=pl.BlockSpec(memory_space=pltpu.VMEM),
        compiler_params=_cparams(),
    )(slabs)


def _pick_rows(rows, target=512):
    best = SUBLANES
    for t in range(SUBLANES, min(rows, target) + 1, SUBLANES):
        if rows % t == 0:
            best = t
    return best


def _half_tile(r0, n):
    return _pick_rows(math.gcd(r0, n // 2) if r0 else n // 2)


def _pair_sum(name, dg, land, place, r0, n):
    W = dg.shape[2]
    tr = _half_tile(r0, n)
    nb = (n // 2) // tr

    def body(p_ref, a_ref, b_ref, o_ref):
        del p_ref
        o_ref[...] = (a_ref[...].astype(F32) + b_ref[...].astype(F32)).astype(o_ref.dtype)

    return pl.pallas_call(
        body, name=name,
        grid_spec=pltpu.PrefetchScalarGridSpec(
            num_scalar_prefetch=1, grid=(N_CHIPS, nb),
            in_specs=[pl.BlockSpec((None, tr, W), lambda j, i, p: (j, r0 // tr + p[0] * nb + i, 0)),
                      pl.BlockSpec((None, tr, W), lambda j, i, p: (j, i, 0))],
            out_specs=pl.BlockSpec((None, tr, W), lambda j, i, p: (j, i, 0))),
        out_shape=jax.ShapeDtypeStruct((N_CHIPS, n // 2, W), BF16),
        compiler_params=_cparams("parallel", "parallel"),
    )(place, dg, land)


def _chip_sum(name, pb, land, place, gf, r0, n):
    W = gf.shape[1]
    tr = _half_tile(r0, n)
    nb = (n // 2) // tr

    def body(p_ref, own_ref, lx_ref, ly_ref, ld_ref, gf_in, o_ref):
        del p_ref, gf_in
        o_ref[...] = ((own_ref[...].astype(F32) + lx_ref[...].astype(F32)) + ly_ref[...].astype(F32)) + ld_ref[...].astype(F32)

    slab = lambda flip: pl.BlockSpec((None, tr, W), lambda i, p, _f=flip: (p[1] ^ _f, i, 0))
    return pl.pallas_call(
        body, name=name,
        grid_spec=pltpu.PrefetchScalarGridSpec(
            num_scalar_prefetch=1, grid=(nb,),
            in_specs=[slab(0), slab(2), slab(1), slab(3), pl.BlockSpec(memory_space=pl.ANY)],
            out_specs=pl.BlockSpec((tr, W), lambda i, p: (r0 // tr + p[0] * nb + i, 0))),
        out_shape=jax.ShapeDtypeStruct(gf.shape, F32),
        input_output_aliases={5: 0},
        compiler_params=_cparams("parallel"),
    )(place, pb, land, land, land, gf)


def _ln_stats(z):
    mu = jnp.mean(z, axis=1, keepdims=True)
    zc = z - mu
    rstd = lax.rsqrt(jnp.mean(zc * zc, axis=1, keepdims=True) + LN_EPS)
    return zc * rstd, rstd


def _ln_fwd(name, xin, m, g, b):
    S, D = xin.shape

    def fn(x_, m_, g_, b_):
        xhat, rstd = _ln_stats(ALPHA * x_ + m_)
        y = xhat * g_ + b_
        return y, y, xhat, rstd

    return _rowwise(name, fn, [xin, m, ('full', g), ('full', b)],
                    [('rows', D, F32), ('rows', D, BF16), ('rows', D, F32), ('rows', 1, F32)], S)


def _ln_bwd_core(dy, xhat, rstd, g):
    dxh = dy * g
    return rstd * (dxh - jnp.mean(dxh, axis=1, keepdims=True) - xhat * jnp.mean(dxh * xhat, axis=1, keepdims=True))


def _ln_bwd(name, terms, xhat, rstd, g, after=None):
    S, D = xhat.shape
    scales = [s for _, s in terms]
    n = len(terms)

    def fn(*v):
        dy = v[0] * scales[0] if scales[0] != 1.0 else v[0]
        for t in range(1, n):
            dy = dy + (v[t] * scales[t] if scales[t] != 1.0 else v[t])
        xh, rs, g_ = v[n], v[n + 1], v[n + 2]
        dz = _ln_bwd_core(dy, xh, rs, g_)
        return dz, dz, jnp.sum(dy * xh, axis=0, keepdims=True), jnp.sum(dy, axis=0, keepdims=True)

    return _rowwise(name, fn, [a for a, _ in terms] + [xhat, rstd, ('full', g)] + ([('after', after)] if after is not None else []),
                    [('rows', D, F32), ('rows', D, BF16), ('acc', (1, D), F32), ('acc', (1, D), F32)], S)


def _adamw_math(w, g, m, v):
    m2 = ADAM_B1 * m + (1.0 - ADAM_B1) * g
    v2 = ADAM_B2 * v + (1.0 - ADAM_B2) * (g * g)
    m_hat = m2 / (1.0 - ADAM_B1 ** ADAM_STEP)
    v_hat = v2 / (1.0 - ADAM_B2 ** ADAM_STEP)
    delta = -ADAM_LR * (m_hat / (jnp.sqrt(v_hat) + ADAM_EPS) + ADAM_WD * w)
    return delta, m2, v2


def _adamw(name, w, gfull, row_start, m, v):
    rows, W = w.shape
    tr = math.gcd(math.gcd(rows, row_start), 256) if row_start else math.gcd(rows, 256)

    def fn(w_, g_, m_, v_):
        d, m2, v2 = _adamw_math(w_, g_, m_, v_)
        return g_, d, m2, v2

    return _rowwise(name, fn, [w, ('off', gfull, row_start // tr), m, v], [('rows', W, F32)] * 4, rows, tm=tr)


def _pack(arrs):
    flat = jnp.concatenate([a.reshape(-1).astype(F32) for a in arrs])
    tile = SUBLANES * LANES
    n = -(-flat.shape[0] // tile) * tile
    return jnp.pad(flat, (0, n - flat.shape[0])).reshape(-1, LANES)


def _unpack(buf, shapes):
    flat = buf.reshape(-1)
    out, pos = [], 0
    for shp in shapes:
        n = math.prod(shp)
        out.append(flat[pos:pos + n].reshape(shp))
        pos += n
    return out


BIG = ['fox_w_qkv', 'fox_w_o', 'rel_w_qkv', 'rel_w_o', 'conv_w_pw1', 'conv_w_pw2', 'ffn_w_gate', 'ffn_w_up', 'ffn_w_down']
SMALL_SHARDED = ['fox_w_f', 'conv_b_pw1', 'conv_w_dw', 'conv_b_dw', 'conv_ln_g', 'conv_ln_b', 'conv_b_pw2']
SMALL_SHARD_AXIS = {'fox_w_f': 1, 'conv_b_pw1': 1, 'conv_w_dw': 2, 'conv_b_dw': 1, 'conv_ln_g': 1, 'conv_ln_b': 1, 'conv_b_pw2': 1}
SMALL_REPL = ['fox_b_f', 'rel_bias', 'ln_mix_g', 'ln_mix_b', 'ln_ffn_g', 'ln_ffn_b']
SMALL = SMALL_SHARDED + SMALL_REPL
WEIGHTS = ['fox_w_qkv', 'fox_w_f', 'fox_b_f', 'fox_w_o', 'rel_w_qkv', 'rel_bias', 'rel_w_o', 'conv_w_pw1', 'conv_b_pw1',
           'conv_w_dw', 'conv_b_dw', 'conv_ln_g', 'conv_ln_b', 'conv_w_pw2', 'conv_b_pw2', 'ffn_w_gate', 'ffn_w_up',
           'ffn_w_down', 'ln_mix_g', 'ln_mix_b', 'ln_ffn_g', 'ln_ffn_b']


def kernel(x, fox_w_qkv, fox_w_f, fox_b_f, fox_w_o, rel_w_qkv, rel_bias, rel_w_o, conv_w_pw1, conv_b_pw1, conv_w_dw, conv_b_dw, conv_ln_g, conv_ln_b, conv_w_pw2, conv_b_pw2, ffn_w_gate, ffn_w_up, ffn_w_down, ln_mix_g, ln_mix_b, ln_ffn_g, ln_ffn_b, loss_target, m_fox_w_qkv, m_fox_w_f, m_fox_b_f, m_fox_w_o, m_rel_w_qkv, m_rel_bias, m_rel_w_o, m_conv_w_pw1, m_conv_b_pw1, m_conv_w_dw, m_conv_b_dw, m_conv_ln_g, m_conv_ln_b, m_conv_w_pw2, m_conv_b_pw2, m_ffn_w_gate, m_ffn_w_up, m_ffn_w_down, m_ln_mix_g, m_ln_mix_b, m_ln_ffn_g, m_ln_ffn_b, v_fox_w_qkv, v_fox_w_f, v_fox_b_f, v_fox_w_o, v_rel_w_qkv, v_rel_bias, v_rel_w_o, v_conv_w_pw1, v_conv_b_pw1, v_conv_w_dw, v_conv_b_dw, v_conv_ln_g, v_conv_ln_b, v_conv_w_pw2, v_conv_b_pw2, v_ffn_w_gate, v_ffn_w_up, v_ffn_w_down, v_ln_mix_g, v_ln_mix_b, v_ln_ffn_g, v_ln_ffn_b):
    A = dict(locals())
    Wt = {n: A[n] for n in WEIGHTS}
    Mo = {n: A['m_' + n] for n in WEIGHTS}
    Vo = {n: A['v_' + n] for n in WEIGHTS}

    _, S, D = x.shape
    H = D // HEAD_DIM
    Ds = D // N_CHIPS
    Nq = fox_w_qkv.shape[2]
    Np = conv_w_pw1.shape[2]
    Fs = ffn_w_gate.shape[2]
    my_x, my_y, my_c = lax.axis_index("x"), lax.axis_index("y"), lax.axis_index("c")
    my_chip = 2 * my_x + my_y
    place = jnp.stack([my_c, my_chip]).astype(jnp.int32)

    wo_base = DEPTH * Fs
    where = {
        'fox_w_qkv': ('qkv', 0), 'rel_w_qkv': ('qkv', N_FOX * D),
        'ffn_w_gate': ('ffn', 0), 'ffn_w_up': ('ffn', DEPTH * D),
        'conv_w_pw1': ('pw1', 0),
        'ffn_w_down': ('dm', 0), 'fox_w_o': ('dm', wo_base), 'rel_w_o': ('dm', wo_base + N_FOX * Ds),
        'conv_w_pw2': ('dm', wo_base + (N_FOX + 1) * Ds),
    }
    members = {'qkv': ['fox_w_qkv', 'rel_w_qkv'], 'ffn': ['ffn_w_gate', 'ffn_w_up'], 'pw1': ['conv_w_pw1'],
               'dm': ['ffn_w_down', 'fox_w_o', 'rel_w_o', 'conv_w_pw2']}
    flat2 = lambda a: a.reshape(-1, a.shape[-1])
    own = {g: jnp.concatenate([flat2(Wt[n]).astype(BF16) for n in ms], axis=0) for g, ms in members.items()}

    def layer_pieces(i):
        kind, j = i % 3, i // 3
        slot = j if kind == 0 else (N_FOX if kind == 1 else N_FOX + 1)
        w_in = (GROUPS.index('pw1'), 0, D) if kind == 2 else (GROUPS.index('qkv'), slot * D, D)
        return [w_in, (GROUPS.index('dm'), wo_base + slot * Ds, Ds), (GROUPS.index('ffn'), i * D, D),
                (GROUPS.index('ffn'), (DEPTH + i) * D, D), (GROUPS.index('dm'), i * Fs, Fs)]

    small_shapes = [Wt[n].shape for n in SMALL_SHARDED]
    slabs = _broadcast_small("gather_small", _pack([Wt[n] for n in SMALL_SHARDED]))

    stages = [part for i in range(DEPTH) for part in (layer_pieces(i)[:2], layer_pieces(i)[2:])]
    gstages = [part for i in range(DEPTH) for part in (layer_pieces(i)[:1], layer_pieces(i)[1:])]
    (g_first, r_first, n_first), = gstages[0]
    buf_first = _place_own("place_" + GROUPS[g_first], own[GROUPS[g_first]], place)
    ssems_first, rsems_first, (buf_first,) = _gather_start("gather_start_first", [buf_first], [[(0, r_first, n_first)]], slabs)
    ssems_rest, rsems_rest, wg_list = _gather_start(
        "gather_start_rest", [buf_first if gi == g_first else _place_own("place_" + g, own[g], place) for gi, g in enumerate(GROUPS)],
        gstages[1:], slabs)
    gather_ssems, gather_rsems = list(ssems_first) + list(ssems_rest), list(rsems_first) + list(rsems_rest)
    WG = dict(zip(GROUPS, wg_list))
    DG = {g: lax.empty(WG[g].shape, BF16) for g in own}

    per_chip = [_unpack(slabs[2 * j], small_shapes) for j in range(N_CHIPS)]
    full = {n: jnp.concatenate([per_chip[j][i] for j in range(N_CHIPS)], axis=SMALL_SHARD_AXIS[n])
            for i, n in enumerate(SMALL_SHARDED)}
    row = lambda v: v.reshape(1, -1)

    SG = {}

    def ffn_fwd(i, xb):
        hg, hu, act = _mm_gate_up(f"ffn{i}_gate_up", xb, WG['ffn'], i * D, (DEPTH + i) * D)
        f = _mm_row(f"ffn{i}_down", act, WG['dm'], i * Fs, Fs)
        return f, (hg, hu, act)

    def ffn_bwd(i, xb, saved, dzb):
        hg, hu, act = saved
        DG['dm'] = _mm_dw(f"ffn{i}_dw_down", act, dzb, DG['dm'], i * Fs, 'row')
        dhg, dhu = _mm_dact(f"ffn{i}_dact", dzb, WG['dm'], i * Fs, Fs, hg, hu)
        DG['ffn'] = _mm_dw(f"ffn{i}_dw_gate", xb, dhg, DG['ffn'], i * D, 'col')
        DG['ffn'] = _mm_dw(f"ffn{i}_dw_up", xb, dhu, DG['ffn'], (DEPTH + i) * D, 'col')
        return _mm_col_t(f"ffn{i}_dx", [(dhg, i * D), (dhu, (DEPTH + i) * D)], WG['ffn'], D)

    def fox_fwd(j, xb, rest_stage):
        qkv = _mm_col(f"fox{j}_qkv", xb, WG['qkv'], j * D)
        wf = full['fox_w_f'][j].astype(BF16)
        def gate_fn(x_, w_, b_):
            z_ = jnp.dot(x_, w_, preferred_element_type=F32) + b_
            return z_, jnp.minimum(z_, 0.0) - jnp.log(1.0 + jnp.exp(-jnp.abs(z_)))

        z, logf = _rowwise(f"fox{j}_gate", gate_fn, [xb, ('full', wf), ('full', row(fox_b_f[j]))],
                           [('rows', H, F32), ('rows', H, F32)], S)
        c = _cumsum_rows(f"fox{j}_cumsum", logf, False)
        crow = c.T.reshape(H, 1, S)
        o = _fox_fwd(qkv, c, crow, H)
        weights_ready(rest_stage, o)
        m = _mm_row(f"fox{j}_wo", o, WG['dm'], wo_base + j * Ds, Ds)
        return m, (qkv, z, c, crow, o, wf)

    def fox_bwd(j, xb, saved, dzb):
        qkv, z, c, crow, o, wf = saved
        DG['dm'] = _mm_dw(f"fox{j}_dw_o", o, dzb, DG['dm'], wo_base + j * Ds, 'row')
        do = _mm_row_t(f"fox{j}_do", dzb, WG['dm'], wo_base + j * Ds, Ds, BF16)
        dq, dk, dv, dcrow = _fox_bwd(qkv, c, crow, do, H)
        dqkv = jnp.concatenate([dq, dk, dv], axis=1)
        dlogf = _cumsum_rows(f"fox{j}_rcumsum", dcrow.reshape(H, S).T, True)

        def fn(x_, dl_, z_, w_):
            dz_ = dl_ * _sigmoid(-z_)
            dzb_ = dz_.astype(BF16)
            return _dot_nt(dzb_, w_), _dot_tn(x_, dzb_), jnp.sum(dz_, axis=0, keepdims=True)

        dh_f, dwf, dbf = _rowwise(f"fox{j}_gate_bwd", fn, [xb, dlogf, z, ('full', wf)],
                                  [('rows', D, F32), ('acc', (D, H), F32), ('acc', (1, H), F32)], S)
        SG.setdefault('fox_w_f', [None] * N_FOX)[j] = dwf
        SG.setdefault('fox_b_f', [None] * N_FOX)[j] = dbf.reshape(H)
        DG['qkv'] = _mm_dw(f"fox{j}_dw_qkv", xb, dqkv, DG['qkv'], j * D, 'col')
        dh = _mm_col_t(f"fox{j}_dx", [(dqkv, j * D)], WG['qkv'], D)
        return [dh, dh_f]

    def rel_fwd(xb, rest_stage):
        qkv = _mm_col("rel_qkv", xb, WG['qkv'], N_FOX * D)
        rb_pad = jnp.pad(rel_bias[0], ((0, 0), (0, REL_TABLE_PAD - REL_TABLE)))
        bias = _rel_window_bias(jnp.transpose(_rel_expand(rb_pad), (1, 0, 2)))
        o = _rel_fwd(qkv, bias, H)
        weights_ready(rest_stage, o)
        m = _mm_row("rel_wo", o, WG['dm'], wo_base + N_FOX * Ds, Ds)
        return m, (qkv, bias, o)

    def rel_bwd(xb, saved, dzb):
        qkv, bias, o = saved
        DG['dm'] = _mm_dw("rel_dw_o", o, dzb, DG['dm'], wo_base + N_FOX * Ds, 'row')
        do = _mm_row_t("rel_do", dzb, WG['dm'], wo_base + N_FOX * Ds, Ds, BF16)
        dq, dk, dv, dbias = _rel_bwd(qkv, bias, do, H)
        SG['rel_bias'] = _rel_reduce(jnp.transpose(dbias, (1, 0, 2)))[:, :REL_TABLE].reshape(1, H, REL_TABLE)
        dqkv = jnp.concatenate([dq, dk, dv], axis=1)
        DG['qkv'] = _mm_dw("rel_dw_qkv", xb, dqkv, DG['qkv'], N_FOX * D, 'col')
        return [_mm_col_t("rel_dx", [(dqkv, N_FOX * D)], WG['qkv'], D)]

    w_dw32 = jnp.pad(full['conv_w_dw'][0], ((0, CONV_HALO - CONV_K), (0, 0)))
    cg, cb = full['conv_ln_g'], full['conv_ln_b']

    def conv_fwd(xb, rest_stage):
        u = _mm_col("conv_pw1", xb, WG['pw1'], 0, bias=full['conv_b_pw1'], out_dtype=F32)
        u2, = _rowwise("conv_glu", lambda a_, g_: [a_ * _sigmoid(g_)],
                       [('cols', u, D, 0), ('cols', u, D, 1)], [('rows', D, F32)], S)
        yc = _dwconv("conv_dw", u2, w_dw32, full['conv_b_dw'], False)

        def fn(y_, g_, b_):
            xhat, rstd = _ln_stats(y_)
            ln = xhat * g_ + b_
            return ln * _sigmoid(ln), xhat, rstd

        zc, xhat, rstd = _rowwise("conv_ln_silu", fn, [yc, ('full', cg), ('full', cb)],
                                  [('rows', D, BF16), ('rows', D, F32), ('rows', 1, F32)], S)
        weights_ready(rest_stage, zc)
        m = _mm_row("conv_pw2", zc, WG['dm'], wo_base + (N_FOX + 1) * Ds, Ds, bias=full['conv_b_pw2'])
        return m, (u, u2, zc, xhat, rstd)

    def conv_bwd(xb, saved, dz, dzb):
        u, u2, zc, xhat, rstd = saved
        r0 = wo_base + (N_FOX + 1) * Ds
        DG['dm'] = _mm_dw("conv_dw_pw2", zc, dzb, DG['dm'], r0, 'row')
        dzc = _mm_row_t("conv_dzc", dzb, WG['dm'], r0, Ds, F32)

        def fn(dm_, dzc_, xh_, rs_, g_, b_):
            ln = xh_ * g_ + b_
            sg = _sigmoid(ln)
            dln = dzc_ * (sg * (1.0 + ln * (1.0 - sg)))
            dyc = _ln_bwd_core(dln, xh_, rs_, g_)
            col = lambda t: jnp.sum(t, axis=0, keepdims=True)
            return dyc, col(dm_), col(dln * xh_), col(dln), col(dyc)

        dyc, SG['conv_b_pw2'], SG['conv_ln_g'], SG['conv_ln_b'], SG['conv_b_dw'] = _rowwise(
            "conv_ln_silu_bwd", fn, [dz, dzc, xhat, rstd, ('full', cg), ('full', cb)],
            [('rows', D, F32)] + [('acc', (1, D), F32)] * 4, S)
        du2 = _dwconv("conv_dw_bwd_x", dyc, w_dw32, jnp.zeros((1, D), F32), True)
        SG['conv_w_dw'] = _dwconv_dw(u2, dyc)[:CONV_K].reshape(1, CONV_K, D)

        def fn2(du2_, a_, g_):
            sg = _sigmoid(g_)
            da, dgt = du2_ * sg, du2_ * a_ * sg * (1.0 - sg)
            return da, dgt, jnp.sum(da, axis=0, keepdims=True), jnp.sum(dgt, axis=0, keepdims=True)

        da, dgt, dba, dbg = _rowwise("conv_glu_bwd", fn2, [du2, ('cols', u, D, 0), ('cols', u, D, 1)],
                                     [('rows', D, BF16), ('rows', D, BF16), ('acc', (1, D), F32), ('acc', (1, D), F32)], S)
        SG['conv_b_pw1'] = jnp.concatenate([dba, dbg], axis=1)
        du = jnp.concatenate([da, dgt], axis=1)
        DG['pw1'] = _mm_dw("conv_dw_pw1", xb, du, DG['pw1'], 0, 'col')
        return [_mm_col_t("conv_dx", [(du, 0)], WG['pw1'], D)]

    xs = x[0]
    xs_b = xs.astype(BF16)
    tape = []

    def weights_ready(s, after):
        bufs = _gather_wait(f"gather_wait{s}", [WG[g] for g in GROUPS], gather_ssems[s], gather_rsems[s], gstages[s], after)
        WG.update(zip(GROUPS, _gather_forward(f"gather_fwd{s}", bufs, gstages[s])))

    for i in range(DEPTH):
        kind, j = i % 3, i // 3
        weights_ready(2 * i, xs)
        if kind == 0:
            m, msaved = fox_fwd(j, xs_b, 2 * i + 1)
        elif kind == 1:
            m, msaved = rel_fwd(xs_b, 2 * i + 1)
        else:
            m, msaved = conv_fwd(xs_b, 2 * i + 1)
        xm, xm_b, xhat1, rstd1 = _ln_fwd(f"ln_mix{i}", xs, m, row(ln_mix_g[i]), row(ln_mix_b[i]))
        f, fsaved = ffn_fwd(i, xm_b)
        xo, xo_b, xhat2, rstd2 = _ln_fwd(f"ln_ffn{i}", xm, f, row(ln_ffn_g[i]), row(ln_ffn_b[i]))
        tape.append((xs_b, msaved, xhat1, rstd1, xm_b, fsaved, xhat2, rstd2))
        xs, xs_b = xo, xo_b

    def loss_fn(y_, t_):
        e = y_ - t_
        return e * (1.0 / D), jnp.sum(e * e, axis=0, keepdims=True)

    dy, sq = _rowwise("loss", loss_fn, [xs, loss_target[0]], [('rows', D, F32), ('acc', (1, D), F32)], S)
    loss = lax.psum(jnp.sum(sq) * (0.5 / D), ("x", "y", "c"))

    GF = {g: lax.empty(WG[g].shape[1:], F32) for g in GROUPS}
    started = [None] * len(stages)

    def reduce_start(s):
        dgs = [DG[g] for g in GROUPS]
        lands = _swap_halves(f"pair_swap{s}", dgs, stages[s])
        pbs = [_pair_sum(f"pair_sum{s}_{pi}", dgs[g], lands[pi], place, r0, n) for pi, (g, r0, n) in enumerate(stages[s])]
        started[s] = _scatter_start(f"scatter_start{s}", pbs)
        token = started[s][2][0]
        if s + 1 < len(stages):
            reduce_finish(s + 1, token)
        return token

    def reduce_finish(s, after):
        ssem, rsem, pbs, lands2 = started[s]
        pbs, lands2 = _scatter_wait(f"scatter_wait{s}", ssem, rsem, pbs, lands2, after)
        for pi, (g, r0, n) in enumerate(stages[s]):
            GF[GROUPS[g]] = _chip_sum(f"chip_sum{s}_{pi}", pbs[pi], lands2[pi], place, GF[GROUPS[g]], r0, n)

    terms = [(dy, 1.0)]
    token = None
    g_mix, b_mix, g_ffn, b_ffn = [None] * DEPTH, [None] * DEPTH, [None] * DEPTH, [None] * DEPTH
    for i in reversed(range(DEPTH)):
        kind, j = i % 3, i // 3
        xin_b, msaved, xhat1, rstd1, xm_b, fsaved, xhat2, rstd2 = tape[i]
        dz2, dz2b, g_ffn[i], b_ffn[i] = _ln_bwd(f"ln_ffn{i}_bwd", terms, xhat2, rstd2, row(ln_ffn_g[i]), after=token)
        dx_ffn = ffn_bwd(i, xm_b, fsaved, dz2b)
        token = reduce_start(2 * i + 1)
        dz1, dz1b, g_mix[i], b_mix[i] = _ln_bwd(f"ln_mix{i}_bwd", [(dz2, ALPHA), (dx_ffn, 1.0)], xhat1, rstd1,
                                                row(ln_mix_g[i]), after=token)
        if kind == 0:
            mix_terms = fox_bwd(j, xin_b, msaved, dz1b)
        elif kind == 1:
            mix_terms = rel_bwd(xin_b, msaved, dz1b)
        else:
            mix_terms = conv_bwd(xin_b, msaved, dz1, dz1b)
        terms = [(dz1, ALPHA)] + [(t, 1.0) for t in mix_terms]
        token = reduce_start(2 * i)

    def gx_fn(*v):
        acc = v[0] * ALPHA
        for t in v[1:]:
            acc = acc + t
        return [acc]

    grad_x, = _rowwise("grad_x", gx_fn, [a for a, _ in terms], [('rows', D, F32)], S)
    grad_x = grad_x.reshape(1, S, D)

    SG['fox_w_f'] = jnp.stack(SG['fox_w_f'])
    SG['fox_b_f'] = jnp.stack(SG['fox_b_f'])
    SG['ln_mix_g'] = jnp.concatenate(g_mix, axis=0)
    SG['ln_mix_b'] = jnp.concatenate(b_mix, axis=0)
    SG['ln_ffn_g'] = jnp.concatenate(g_ffn, axis=0)
    SG['ln_ffn_b'] = jnp.concatenate(b_ffn, axis=0)

    grads, deltas, new_m, new_v = {}, {}, {}, {}

    reduce_finish(0, grad_x)
    links_idle = GF['dm']
    for gi, g in enumerate(GROUPS):
        GF[g] = _share_halves("pair_share_" + g, GF[g], [(r0, n) for st in stages for (pg, r0, n) in st if pg == gi])

    for n in BIG:
        g, r0 = where[n]
        outs = _adamw("adamw_" + n, flat2(Wt[n]), GF[g], r0, flat2(Mo[n]), flat2(Vo[n]))
        grads[n], deltas[n], new_m[n], new_v[n] = [o.reshape(Wt[n].shape) for o in outs]

    full_shapes = [SG[n].shape for n in SMALL]
    summed = _sum_slabs("small_sum", _broadcast_small("small_exchange", _pack([SG[n] for n in SMALL]), after=links_idle))
    gsm = dict(zip(SMALL, _unpack(summed, full_shapes)))
    for n in SMALL_SHARDED:
        ax = SMALL_SHARD_AXIS[n]
        width = Wt[n].shape[ax]
        gsm[n] = lax.dynamic_slice_in_dim(gsm[n], my_chip * width, width, axis=ax)
    own_shapes = [Wt[n].shape for n in SMALL]
    packed = [_pack([src[n] for n in SMALL]) for src in (Wt, gsm, Mo, Vo)]
    rows_small = packed[0].shape[0]

    def small_fn(w_, g_, m_, v_):
        return _adamw_math(w_, g_, m_, v_)

    sd, sm, sv = _rowwise("adamw_small", small_fn, packed, [('rows', LANES, F32)] * 3, rows_small, tm=rows_small)
    for n, d_, m_, v_ in zip(SMALL, _unpack(sd, own_shapes), _unpack(sm, own_shapes), _unpack(sv, own_shapes)):
        grads[n], deltas[n], new_m[n], new_v[n] = gsm[n], d_, m_, v_

    return (loss, grad_x, *[grads[n] for n in WEIGHTS], *[deltas[n] for n in WEIGHTS],
            *[new_m[n] for n in WEIGHTS], *[new_v[n] for n in WEIGHTS])
```

```python
import functools
import math

import jax
import jax.numpy as jnp
from jax import lax
from jax.experimental import pallas as pl
from jax.experimental.pallas import tpu as pltpu

F32 = jnp.float32
BF16 = jnp.bfloat16
MESH_IDS = pl.DeviceIdType.MESH
HIGHEST = lax.Precision.HIGHEST

N_CHIPS = 4
DEPTH = 4
N_FOX = 2
HEAD_DIM = 128
CHUNK = 64
LEFT_CHUNKS = 8
BAND_KEYS = (LEFT_CHUNKS + 1) * CHUNK
PAD_KEYS = LEFT_CHUNKS * CHUNK
REL_CLIP = 128
REL_TABLE = 2 * REL_CLIP + 1
REL_TABLE_PAD = 384
REL_QB = 4 * CHUNK
REL_WIN = REL_QB + PAD_KEYS
CONV_K = 31
CONV_HALO = 32
ALPHA = (2.0 * DEPTH) ** 0.25
LN_EPS = 1e-5
ADAM_LR, ADAM_B1, ADAM_B2, ADAM_EPS, ADAM_WD, ADAM_STEP = 0.001, 0.9, 0.999, 1e-08, 0.01, 10
NEG_BIG = -1e30
VMEM_LIMIT_V7X = 56 * 1024 * 1024
LANES = 128
SUBLANES = 8
MM_ROWS = 1024


def _cparams(*sem):
    return pltpu.CompilerParams(dimension_semantics=sem if sem else None, vmem_limit_bytes=VMEM_LIMIT_V7X)


def _pick(dim, target):
    best = None
    for t in range(LANES, min(dim, target) + 1, LANES):
        if dim % t == 0:
            best = t
    return best if best is not None else dim


def _dot_nt(a, b):
    return lax.dot_general(a, b, (((1,), (1,)), ((), ())), preferred_element_type=F32)


def _dot_tn(a, b):
    return lax.dot_general(a, b, (((0,), (0,)), ((), ())), preferred_element_type=F32)


def _sigmoid(z):
    return 1.0 / (1.0 + jnp.exp(-z))


def _rowwise(name, fn, ins, outs, S, tm=256):
    tm = min(tm, S)
    afters = [it[1] for it in ins if isinstance(it, tuple) and it[0] == 'after']
    ins = [it for it in ins if not (isinstance(it, tuple) and it[0] == 'after')]
    arrs, in_specs = [], []
    for it in ins:
        if isinstance(it, tuple) and it[0] == 'full':
            a = it[1]
            in_specs.append(pl.BlockSpec(a.shape, lambda i, _n=a.ndim: (0,) * _n))
        elif isinstance(it, tuple) and it[0] == 'cols':
            _, a, width, blk = it
            in_specs.append(pl.BlockSpec((tm, width), lambda i, _b=blk: (i, _b)))
        elif isinstance(it, tuple) and it[0] == 'off':
            _, a, off = it
            in_specs.append(pl.BlockSpec((tm, a.shape[1]), lambda i, _o=off: (i + _o, 0)))
        else:
            a = it
            in_specs.append(pl.BlockSpec((tm, a.shape[1]), lambda i: (i, 0)))
        arrs.append(a)
    out_shape, out_specs = [], []
    for kind, shp, dt in outs:
        if kind == 'rows':
            out_shape.append(jax.ShapeDtypeStruct((S, shp), dt))
            out_specs.append(pl.BlockSpec((tm, shp), lambda i: (i, 0)))
        else:
            out_shape.append(jax.ShapeDtypeStruct(shp, dt))
            out_specs.append(pl.BlockSpec(shp, lambda i, _n=len(shp): (0,) * _n))
    n_in = len(arrs)
    in_specs += [pl.BlockSpec(memory_space=pl.ANY)] * len(afters)

    def body(*refs):
        vals = fn(*[r[...] for r in refs[:n_in]])
        first = pl.program_id(0) == 0
        for (kind, _, _), r, v in zip(outs, refs[n_in + len(afters):], vals):
            if kind == 'rows':
                r[...] = v.astype(r.dtype)
            else:
                @pl.when(first)
                def _(r=r, v=v):
                    r[...] = v.astype(r.dtype)

                @pl.when(jnp.logical_not(first))
                def _(r=r, v=v):
                    r[...] += v.astype(r.dtype)

    has_acc = any(k != 'rows' for k, _, _ in outs)
    res = pl.pallas_call(
        body, name=name, grid=(S // tm,), in_specs=in_specs, out_specs=out_specs, out_shape=out_shape,
        compiler_params=_cparams("arbitrary" if has_acc else "parallel"),
    )(*arrs, *afters)
    return res


def _mm_col(name, a, wg, row_start, bias=None, out_dtype=BF16):
    S, K = a.shape
    _, _, Ns = wg.shape
    rb = row_start // K
    tm = min(MM_ROWS, S)

    def body(a_ref, w_ref, *rest):
        acc = jnp.dot(a_ref[...].astype(BF16), w_ref[...], preferred_element_type=F32)
        if bias is not None:
            acc = acc + rest[0][...]
        rest[-1][...] = acc.astype(out_dtype)

    in_specs = [pl.BlockSpec((tm, K), lambda j, m: (m, 0)), pl.BlockSpec((None, K, Ns), lambda j, m: (j, rb, 0))]
    args = [a, wg]
    if bias is not None:
        in_specs.append(pl.BlockSpec((1, Ns), lambda j, m: (0, j)))
        args.append(bias)
    return pl.pallas_call(
        body, name=name, grid=(N_CHIPS, S // tm), in_specs=in_specs,
        out_specs=pl.BlockSpec((tm, Ns), lambda j, m: (m, j)),
        out_shape=jax.ShapeDtypeStruct((S, N_CHIPS * Ns), out_dtype),
        compiler_params=_cparams("parallel", "parallel"),
    )(*args)


def _mm_row(name, a, wg, row_start, Ks, bias=None):
    S = a.shape[0]
    N = wg.shape[2]
    rb = row_start // Ks
    tm = min(MM_ROWS, S)

    def body(a_ref, w_ref, *rest):
        o_ref = rest[-1]
        j = pl.program_id(1)
        d = jnp.dot(a_ref[...].astype(BF16), w_ref[...], preferred_element_type=F32)

        @pl.when(j == 0)
        def _():
            o_ref[...] = d + rest[0][...] if bias is not None else d

        @pl.when(j > 0)
        def _():
            o_ref[...] += d

    in_specs = [pl.BlockSpec((tm, Ks), lambda m, j: (m, j)), pl.BlockSpec((None, Ks, N), lambda m, j: (j, rb, 0))]
    args = [a, wg]
    if bias is not None:
        in_specs.append(pl.BlockSpec((1, N), lambda m, j: (0, 0)))
        args.append(bias)
    return pl.pallas_call(
        body, name=name, grid=(S // tm, N_CHIPS), in_specs=in_specs,
        out_specs=pl.BlockSpec((tm, N), lambda m, j: (m, 0)),
        out_shape=jax.ShapeDtypeStruct((S, N), F32),
        compiler_params=_cparams("parallel", "arbitrary"),
    )(*args)


def _mm_col_t(name, pairs, wg, K):
    S = pairs[0][0].shape[0]
    Ns = wg.shape[2]
    tm = min(MM_ROWS, S)
    n = len(pairs)
    tkk = K // n if (K // n) % LANES == 0 else K

    def body(*refs):
        o_ref = refs[-1]
        j = pl.program_id(2)
        d = _dot_nt(refs[0][...], refs[n][...])
        for p in range(1, n):
            d = d + _dot_nt(refs[p][...], refs[n + p][...])

        @pl.when(j == 0)
        def _():
            o_ref[...] = d

        @pl.when(j > 0)
        def _():
            o_ref[...] += d

    in_specs = [pl.BlockSpec((tm, Ns), lambda m, kb, j: (m, j)) for _ in pairs]
    in_specs += [pl.BlockSpec((None, tkk, Ns), lambda m, kb, j, _rb=rs // tkk: (j, _rb + kb, 0)) for _, rs in pairs]
    return pl.pallas_call(
        body, name=name, grid=(S // tm, K // tkk, N_CHIPS), in_specs=in_specs,
        out_specs=pl.BlockSpec((tm, tkk), lambda m, kb, j: (m, kb)),
        out_shape=jax.ShapeDtypeStruct((S, K), F32),
        compiler_params=_cparams("parallel", "parallel", "arbitrary"),
    )(*[dy for dy, _ in pairs], *[wg for _ in pairs])


def _mm_row_t(name, dy, wg, row_start, Ks, out_dtype):
    S, N = dy.shape
    rb = row_start // Ks
    tm = min(MM_ROWS, S)

    def body(dy_ref, w_ref, o_ref):
        o_ref[...] = _dot_nt(dy_ref[...], w_ref[...]).astype(out_dtype)

    return pl.pallas_call(
        body, name=name, grid=(N_CHIPS, S // tm),
        in_specs=[pl.BlockSpec((tm, N), lambda j, m: (m, 0)), pl.BlockSpec((None, Ks, N), lambda j, m: (j, rb, 0))],
        out_specs=pl.BlockSpec((tm, Ks), lambda j, m: (m, j)),
        out_shape=jax.ShapeDtypeStruct((S, N_CHIPS * Ks), out_dtype),
        compiler_params=_cparams("parallel", "parallel"),
    )(dy, wg)


def _silu_parts(g):
    sg = _sigmoid(g)
    return g * sg, sg * (1.0 + g * (1.0 - sg))


def _mm_gate_up(name, a, wg, gate_row, up_row):
    S, K = a.shape
    Ns = wg.shape[2]
    tm = min(MM_ROWS // 2, S)

    def body(a_ref, wg_ref, wu_ref, hg_ref, hu_ref, act_ref):
        a_ = a_ref[...]
        hg = jnp.dot(a_, wg_ref[...], preferred_element_type=F32).astype(BF16)
        hu = jnp.dot(a_, wu_ref[...], preferred_element_type=F32).astype(BF16)
        hg_ref[...] = hg
        hu_ref[...] = hu
        act_ref[...] = (_silu_parts(hg.astype(F32))[0] * hu.astype(F32)).astype(BF16)

    out = jax.ShapeDtypeStruct((S, N_CHIPS * Ns), BF16)
    w_spec = lambda rb: pl.BlockSpec((None, K, Ns), lambda j, m: (j, rb, 0))
    o_spec = pl.BlockSpec((tm, Ns), lambda j, m: (m, j))
    return pl.pallas_call(
        body, name=name, grid=(N_CHIPS, S // tm),
        in_specs=[pl.BlockSpec((tm, K), lambda j, m: (m, 0)), w_spec(gate_row // K), w_spec(up_row // K)],
        out_specs=[o_spec, o_spec, o_spec], out_shape=[out, out, out],
        compiler_params=_cparams("parallel", "parallel"),
    )(a, wg, wg)


def _mm_dact(name, dy, wg, row_start, Ks, hg, hu):
    S, N = dy.shape
    rb = row_start // Ks
    tm = min(512, S)

    def body(dy_ref, w_ref, hg_ref, hu_ref, dhg_ref, dhu_ref):
        dact = _dot_nt(dy_ref[...], w_ref[...])
        silu, dsilu = _silu_parts(hg_ref[...].astype(F32))
        dhg_ref[...] = (dact * hu_ref[...].astype(F32) * dsilu).astype(BF16)
        dhu_ref[...] = (dact * silu).astype(BF16)

    out = jax.ShapeDtypeStruct((S, N_CHIPS * Ks), BF16)
    t_spec = pl.BlockSpec((tm, Ks), lambda j, m: (m, j))
    return pl.pallas_call(
        body, name=name, grid=(N_CHIPS, S // tm),
        in_specs=[pl.BlockSpec((tm, N), lambda j, m: (m, 0)), pl.BlockSpec((None, Ks, N), lambda j, m: (j, rb, 0)), t_spec, t_spec],
        out_specs=[t_spec, t_spec], out_shape=[out, out],
        compiler_params=_cparams("parallel", "parallel"),
    )(dy, wg, hg, hu)


def _mm_dw(name, a, dy, dg, row_start, kind):
    S = a.shape[0]
    _, _, W = dg.shape
    if kind == 'col':
        K = a.shape[1]
        rows = K
        tk, tn = _pick(K, MM_ROWS), W
        a_map = lambda j, nb, kb: (0, kb)
        dy_map = lambda j, nb, kb: (0, j * (W // tn) + nb)
    else:
        rows = a.shape[1] // N_CHIPS
        tk = rows if rows * S * 2 * 2 <= 12 * 1024 * 1024 else _pick(rows, 512)
        tn = _pick(W, 1024)
        a_map = lambda j, nb, kb: (0, j * (rows // tk) + kb)
        dy_map = lambda j, nb, kb: (0, nb)
    rb = row_start // tk
    assert row_start % tk == 0

    def body(a_ref, dy_ref, dg_in, o_ref):
        del dg_in
        o_ref[...] = _dot_tn(a_ref[...], dy_ref[...]).astype(o_ref.dtype)

    return pl.pallas_call(
        body, name=name, grid=(N_CHIPS, W // tn, rows // tk),
        in_specs=[pl.BlockSpec((S, tk), a_map), pl.BlockSpec((S, tn), dy_map), pl.BlockSpec(memory_space=pl.ANY)],
        out_specs=pl.BlockSpec((None, tk, tn), lambda j, nb, kb: (j, rb + kb, nb)),
        out_shape=jax.ShapeDtypeStruct(dg.shape, dg.dtype),
        input_output_aliases={2: 0},
        compiler_params=_cparams("parallel", "parallel", "parallel"),
    )(a, dy, dg)


def _fox_probs(q, k, c_blk, crow, h, qi, tq):
    n = k.shape[0]
    s = _dot_nt(q, k) * (HEAD_DIM ** -0.5)
    lane = lax.broadcasted_iota(jnp.int32, c_blk.shape, 1)
    ccol = jnp.sum(jnp.where(lane == h, c_blk, 0.0), axis=1, keepdims=True)
    s = s + (ccol - crow)
    t_idx = qi * tq + lax.broadcasted_iota(jnp.int32, (tq, n), 0)
    s_idx = lax.broadcasted_iota(jnp.int32, (tq, n), 1)
    s = jnp.where(s_idx <= t_idx, s, NEG_BIG)
    p = jnp.exp(s - jnp.max(s, axis=1, keepdims=True))
    return p * (1.0 / jnp.sum(p, axis=1, keepdims=True))


def _per_query_block(qi, nq, tq, fn):
    for qv in range(nq):
        @pl.when(qi == qv)
        def _(qv=qv):
            fn(qv, (qv + 1) * tq)


FOX_HEADS = 2


def _head_cols(a):
    return slice(a * HEAD_DIM, (a + 1) * HEAD_DIM)


def _fox_fwd(qkv, c, crow, H):
    S = qkv.shape[0]
    tq = min(256, S)
    hw = FOX_HEADS * HEAD_DIM
    G = H // FOX_HEADS

    def body(q_ref, k_ref, v_ref, c_ref, crow_ref, o_ref):
        def block(qv, n):
            for a in range(FOX_HEADS):
                cols = _head_cols(a)
                p = _fox_probs(q_ref[:, cols], k_ref[0:n, cols], c_ref[...], crow_ref[a, :, 0:n],
                               FOX_HEADS * pl.program_id(0) + a, qv, tq)
                o_ref[:, cols] = jnp.dot(p.astype(BF16), v_ref[0:n, cols], preferred_element_type=F32).astype(o_ref.dtype)

        _per_query_block(pl.program_id(1), S // tq, tq, block)

    return pl.pallas_call(
        body, name="fox_attn_fwd", grid=(G, S // tq),
        in_specs=[pl.BlockSpec((tq, hw), lambda g, i: (i, g)),
                  pl.BlockSpec((S, hw), lambda g, i: (0, G + g)),
                  pl.BlockSpec((S, hw), lambda g, i: (0, 2 * G + g)),
                  pl.BlockSpec((tq, H), lambda g, i: (i, 0)),
                  pl.BlockSpec((FOX_HEADS, 1, S), lambda g, i: (g, 0, 0))],
        out_specs=pl.BlockSpec((tq, hw), lambda g, i: (i, g)),
        out_shape=jax.ShapeDtypeStruct((S, H * HEAD_DIM), BF16),
        compiler_params=_cparams("parallel", "parallel"),
    )(qkv, qkv, qkv, c, crow)


def _fox_bwd(qkv, c, crow, do, H):
    S = qkv.shape[0]
    tq = min(256, S)
    nq = S // tq
    hw = FOX_HEADS * HEAD_DIM
    G = H // FOX_HEADS

    def body(q_ref, k_ref, v_ref, c_ref, crow_ref, do_ref, dq_ref, dk_ref, dv_ref, dc_ref, dk_acc, dv_acc):
        qi = pl.program_id(1)

        @pl.when(qi == 0)
        def _():
            dk_acc[...] = jnp.zeros_like(dk_acc)
            dv_acc[...] = jnp.zeros_like(dv_acc)
            dc_ref[...] = jnp.zeros_like(dc_ref)

        def block(qv, n):
            for a in range(FOX_HEADS):
                cols = _head_cols(a)
                q, k, v, do_ = q_ref[:, cols], k_ref[0:n, cols], v_ref[0:n, cols], do_ref[:, cols]
                p = _fox_probs(q, k, c_ref[...], crow_ref[a, :, 0:n], FOX_HEADS * pl.program_id(0) + a, qv, tq)
                dv_acc[0:n, cols] += _dot_tn(p.astype(BF16), do_)
                dp = _dot_nt(do_, v)
                ds = p * (dp - jnp.sum(p * dp, axis=1, keepdims=True))
                dsb = (ds * (HEAD_DIM ** -0.5)).astype(BF16)
                dq_ref[:, cols] = jnp.dot(dsb, k, preferred_element_type=F32).astype(dq_ref.dtype)
                dk_acc[0:n, cols] += _dot_tn(dsb, q)
                dc_ref[a, :, 0:n] += -jnp.sum(ds, axis=0, keepdims=True)

        _per_query_block(qi, nq, tq, block)

        @pl.when(qi == nq - 1)
        def _():
            dk_ref[...] = dk_acc[...].astype(dk_ref.dtype)
            dv_ref[...] = dv_acc[...].astype(dv_ref.dtype)

    D = H * HEAD_DIM
    return pl.pallas_call(
        body, name="fox_attn_bwd", grid=(G, nq),
        in_specs=[pl.BlockSpec((tq, hw), lambda g, i: (i, g)),
                  pl.BlockSpec((S, hw), lambda g, i: (0, G + g)),
                  pl.BlockSpec((S, hw), lambda g, i: (0, 2 * G + g)),
                  pl.BlockSpec((tq, H), lambda g, i: (i, 0)),
                  pl.BlockSpec((FOX_HEADS, 1, S), lambda g, i: (g, 0, 0)),
                  pl.BlockSpec((tq, hw), lambda g, i: (i, g))],
        out_specs=[pl.BlockSpec((tq, hw), lambda g, i: (i, g)),
                   pl.BlockSpec((S, hw), lambda g, i: (0, g)),
                   pl.BlockSpec((S, hw), lambda g, i: (0, g)),
                   pl.BlockSpec((FOX_HEADS, 1, S), lambda g, i: (g, 0, 0))],
        out_shape=[jax.ShapeDtypeStruct((S, D), BF16), jax.ShapeDtypeStruct((S, D), BF16),
                   jax.ShapeDtypeStruct((S, D), BF16), jax.ShapeDtypeStruct((H, 1, S), F32)],
        scratch_shapes=[pltpu.VMEM((S, hw), F32), pltpu.VMEM((S, hw), F32)],
        compiler_params=_cparams("parallel", "arbitrary"),
    )(qkv, qkv, qkv, c, crow, do)


def _cumsum_rows(name, xin, reverse):
    S, H = xin.shape
    tb = min(256, S)
    nb = S // tb

    def body(x_ref, o_ref):
        r = lax.broadcasted_iota(jnp.int32, (tb, tb), 0)
        cidx = lax.broadcasted_iota(jnp.int32, (tb, tb), 1)
        tri = (r <= cidx if reverse else r >= cidx).astype(F32)

        def step(b, carry):
            bb = nb - 1 - b if reverse else b
            rows = pl.ds(pl.multiple_of(bb * tb, tb), tb)
            blk = x_ref[rows, :]
            o_ref[rows, :] = jnp.dot(tri, blk, precision=HIGHEST, preferred_element_type=F32) + carry
            return carry + jnp.sum(blk, axis=0, keepdims=True)

        lax.fori_loop(0, nb, step, jnp.zeros((1, H), F32))

    return pl.pallas_call(
        body, name=name, out_shape=jax.ShapeDtypeStruct((S, H), F32),
        in_specs=[pl.BlockSpec(memory_space=pltpu.VMEM)], out_specs=pl.BlockSpec(memory_space=pltpu.VMEM),
        compiler_params=_cparams(),
    )(xin)


def _rel_onehot(i, transposed):
    shp = (REL_TABLE_PAD, BAND_KEYS) if transposed else (BAND_KEYS, REL_TABLE_PAD)
    j = lax.broadcasted_iota(jnp.int32, shp, 1 if transposed else 0)
    r = lax.broadcasted_iota(jnp.int32, shp, 0 if transposed else 1)
    return (jnp.clip(PAD_KEYS + i - j, -REL_CLIP, REL_CLIP) + REL_CLIP == r).astype(F32)


def _rel_expand(rb_pad):
    H = rb_pad.shape[0]

    def body(rb_ref, o_ref):
        def step(i, _):
            o_ref[i] = jnp.dot(rb_ref[...], _rel_onehot(i, True), precision=HIGHEST, preferred_element_type=F32)
            return 0
        lax.fori_loop(0, CHUNK, step, 0)

    return pl.pallas_call(
        body, name="rel_bias_expand", out_shape=jax.ShapeDtypeStruct((CHUNK, H, BAND_KEYS), F32),
        in_specs=[pl.BlockSpec(memory_space=pltpu.VMEM)], out_specs=pl.BlockSpec(memory_space=pltpu.VMEM),
        compiler_params=_cparams(),
    )(rb_pad)


def _rel_reduce(dbt):
    H = dbt.shape[1]

    def body(d_ref, o_ref):
        def step(i, acc):
            j = lax.broadcasted_iota(jnp.int32, (REL_WIN, REL_TABLE_PAD), 0)
            r = lax.broadcasted_iota(jnp.int32, (REL_WIN, REL_TABLE_PAD), 1)
            onehot = (jnp.clip(PAD_KEYS + i - j, -REL_CLIP, REL_CLIP) + REL_CLIP == r).astype(BF16)
            d = d_ref[i]
            hi = d.astype(BF16)
            lo = (d - hi.astype(F32)).astype(BF16)
            return acc + (jnp.dot(hi, onehot, preferred_element_type=F32) + jnp.dot(lo, onehot, preferred_element_type=F32))
        o_ref[...] = lax.fori_loop(0, REL_QB, step, jnp.zeros((H, REL_TABLE_PAD), F32))

    return pl.pallas_call(
        body, name="rel_bias_reduce", out_shape=jax.ShapeDtypeStruct((H, REL_TABLE_PAD), F32),
        in_specs=[pl.BlockSpec(memory_space=pltpu.VMEM)], out_specs=pl.BlockSpec(memory_space=pltpu.VMEM),
        compiler_params=_cparams(),
    )(dbt)


def _rel_window_bias(bias):
    H = bias.shape[0]
    out = jnp.full((H, REL_QB, REL_WIN), NEG_BIG, F32)
    for a in range(REL_QB // CHUNK):
        out = out.at[:, a * CHUNK:(a + 1) * CHUNK, a * CHUNK:a * CHUNK + BAND_KEYS].set(bias)
    return out


def _rel_probs(q, kw, bias_w, t0):
    s = _dot_nt(q, kw) * (HEAD_DIM ** -0.5) + bias_w
    j = lax.broadcasted_iota(jnp.int32, (REL_QB, REL_WIN), 1)
    s = jnp.where(j >= PAD_KEYS - t0, s, NEG_BIG)
    p = jnp.exp(s - jnp.max(s, axis=1, keepdims=True))
    return p * (1.0 / jnp.sum(p, axis=1, keepdims=True))


def _rel_fwd(qkv, bias_w, H):
    S = qkv.shape[0]

    def body(q_ref, k_ref, v_ref, b_ref, o_ref, kpad, vpad):
        kpad[0:PAD_KEYS, :] = jnp.zeros((PAD_KEYS, HEAD_DIM), BF16)
        vpad[0:PAD_KEYS, :] = jnp.zeros((PAD_KEYS, HEAD_DIM), BF16)
        kpad[PAD_KEYS:PAD_KEYS + S, :] = k_ref[...]
        vpad[PAD_KEYS:PAD_KEYS + S, :] = v_ref[...]

        def block(n, _):
            t0 = pl.multiple_of(n * REL_QB, REL_QB)
            rows, win = pl.ds(t0, REL_QB), pl.ds(t0, REL_WIN)
            p = _rel_probs(q_ref[rows, :], kpad[win, :], b_ref[...], t0)
            o_ref[rows, :] = jnp.dot(p.astype(BF16), vpad[win, :], preferred_element_type=F32).astype(o_ref.dtype)
            return 0

        lax.fori_loop(0, S // REL_QB, block, 0, unroll=2)

    return pl.pallas_call(
        body, name="rel_attn_fwd", grid=(H,),
        in_specs=[pl.BlockSpec((S, HEAD_DIM), lambda h: (0, h)),
                  pl.BlockSpec((S, HEAD_DIM), lambda h: (0, H + h)),
                  pl.BlockSpec((S, HEAD_DIM), lambda h: (0, 2 * H + h)),
                  pl.BlockSpec((None, REL_QB, REL_WIN), lambda h: (h, 0, 0))],
        out_specs=pl.BlockSpec((S, HEAD_DIM), lambda h: (0, h)),
        out_shape=jax.ShapeDtypeStruct((S, H * HEAD_DIM), BF16),
        scratch_shapes=[pltpu.VMEM((S + PAD_KEYS, HEAD_DIM), BF16), pltpu.VMEM((S + PAD_KEYS, HEAD_DIM), BF16)],
        compiler_params=_cparams("parallel"),
    )(qkv, qkv, qkv, bias_w)


def _rel_bwd(qkv, bias_w, do, H):
    S = qkv.shape[0]
    D = H * HEAD_DIM

    def body(q_ref, k_ref, v_ref, b_ref, do_ref, dq_ref, dk_ref, dv_ref, db_ref, kpad, vpad, dkpad, dvpad):
        kpad[0:PAD_KEYS, :] = jnp.zeros((PAD_KEYS, HEAD_DIM), BF16)
        vpad[0:PAD_KEYS, :] = jnp.zeros((PAD_KEYS, HEAD_DIM), BF16)
        kpad[PAD_KEYS:PAD_KEYS + S, :] = k_ref[...]
        vpad[PAD_KEYS:PAD_KEYS + S, :] = v_ref[...]
        dkpad[...] = jnp.zeros_like(dkpad)
        dvpad[...] = jnp.zeros_like(dvpad)
        db_ref[...] = jnp.zeros_like(db_ref)

        def block(n, _):
            t0 = pl.multiple_of(n * REL_QB, REL_QB)
            rows, win = pl.ds(t0, REL_QB), pl.ds(t0, REL_WIN)
            q, kw, vw, do_ = q_ref[rows, :], kpad[win, :], vpad[win, :], do_ref[rows, :]
            p = _rel_probs(q, kw, b_ref[...], t0)
            dvpad[win, :] += _dot_tn(p.astype(BF16), do_)
            dp = _dot_nt(do_, vw)
            ds = p * (dp - jnp.sum(p * dp, axis=1, keepdims=True))
            db_ref[...] += ds
            dsb = (ds * (HEAD_DIM ** -0.5)).astype(BF16)
            dq_ref[rows, :] = jnp.dot(dsb, kw, preferred_element_type=F32).astype(dq_ref.dtype)
            dkpad[win, :] += _dot_tn(dsb, q)
            return 0

        lax.fori_loop(0, S // REL_QB, block, 0, unroll=2)
        dk_ref[...] = dkpad[PAD_KEYS:PAD_KEYS + S, :].astype(dk_ref.dtype)
        dv_ref[...] = dvpad[PAD_KEYS:PAD_KEYS + S, :].astype(dv_ref.dtype)

    head = lambda h: (0, h)
    return pl.pallas_call(
        body, name="rel_attn_bwd", grid=(H,),
        in_specs=[pl.BlockSpec((S, HEAD_DIM), head),
                  pl.BlockSpec((S, HEAD_DIM), lambda h: (0, H + h)),
                  pl.BlockSpec((S, HEAD_DIM), lambda h: (0, 2 * H + h)),
                  pl.BlockSpec((None, REL_QB, REL_WIN), lambda h: (h, 0, 0)),
                  pl.BlockSpec((S, HEAD_DIM), head)],
        out_specs=[pl.BlockSpec((S, HEAD_DIM), head), pl.BlockSpec((S, HEAD_DIM), head), pl.BlockSpec((S, HEAD_DIM), head),
                   pl.BlockSpec((None, REL_QB, REL_WIN), lambda h: (h, 0, 0))],
        out_shape=[jax.ShapeDtypeStruct((S, D), BF16), jax.ShapeDtypeStruct((S, D), BF16), jax.ShapeDtypeStruct((S, D), BF16),
                   jax.ShapeDtypeStruct((H, REL_QB, REL_WIN), F32)],
        scratch_shapes=[pltpu.VMEM((S + PAD_KEYS, HEAD_DIM), BF16), pltpu.VMEM((S + PAD_KEYS, HEAD_DIM), BF16),
                        pltpu.VMEM((S + PAD_KEYS, HEAD_DIM), F32), pltpu.VMEM((S + PAD_KEYS, HEAD_DIM), F32)],
        compiler_params=_cparams("parallel"),
    )(qkv, qkv, qkv, bias_w, do)


def _conv_taps(win, tt, reverse):
    n = tt + 2 * CONV_HALO
    for k in range(CONV_K):
        off = (CONV_K - 1 - k) if reverse else (k - (CONV_K - 1))
        sh = (-off) % n
        rolled = pltpu.roll(win, sh, 0) if sh else win
        yield k, rolled[CONV_HALO:CONV_HALO + tt, :]


def _fill_padded(pad_ref, x_ref, S):
    tc = pad_ref.shape[1]
    pad_ref[0:CONV_HALO, :] = jnp.zeros((CONV_HALO, tc), F32)
    pad_ref[CONV_HALO + S:CONV_HALO + S + CONV_HALO, :] = jnp.zeros((CONV_HALO, tc), F32)
    pad_ref[CONV_HALO:CONV_HALO + S, :] = x_ref[...]


def _dwconv(name, xin, w32, bias, reverse):
    S, D = xin.shape
    tc = min(256, D)
    tt = min(256, S)

    def body(x_ref, w_ref, b_ref, y_ref, pad_ref):
        _fill_padded(pad_ref, x_ref, S)
        def tile(ti, _):
            t0 = pl.multiple_of(ti * tt, tt)
            win = pad_ref[pl.ds(t0, tt + 2 * CONV_HALO), :]
            acc = jnp.zeros((tt, tc), F32) + b_ref[...]
            for k, shifted in _conv_taps(win, tt, reverse):
                acc = acc + w_ref[pl.ds(k, 1), :] * shifted
            y_ref[pl.ds(t0, tt), :] = acc
            return 0

        lax.fori_loop(0, S // tt, tile, 0)

    return pl.pallas_call(
        body, name=name, grid=(D // tc,),
        in_specs=[pl.BlockSpec((S, tc), lambda i: (0, i)), pl.BlockSpec((CONV_HALO, tc), lambda i: (0, i)),
                  pl.BlockSpec((1, tc), lambda i: (0, i))],
        out_specs=pl.BlockSpec((S, tc), lambda i: (0, i)),
        out_shape=jax.ShapeDtypeStruct((S, D), F32),
        scratch_shapes=[pltpu.VMEM((S + 2 * CONV_HALO, tc), F32)],
        compiler_params=_cparams("parallel"),
    )(xin, w32, bias)


def _dwconv_dw(xin, dy):
    S, D = xin.shape
    tc = min(256, D)
    tt = min(256, S)

    def body(x_ref, dy_ref, o_ref, pad_ref):
        _fill_padded(pad_ref, x_ref, S)

        def tile(ti, acc):
            t0 = pl.multiple_of(ti * tt, tt)
            win = pad_ref[pl.ds(t0, tt + 2 * CONV_HALO), :]
            dyt = dy_ref[pl.ds(t0, tt), :]
            ridx = lax.broadcasted_iota(jnp.int32, (CONV_HALO, tc), 0)
            upd = jnp.zeros((CONV_HALO, tc), F32)
            for k, shifted in _conv_taps(win, tt, False):
                upd = jnp.where(ridx == k, jnp.sum(dyt * shifted, axis=0, keepdims=True), upd)
            return acc + upd

        o_ref[...] = lax.fori_loop(0, S // tt, tile, jnp.zeros((CONV_HALO, tc), F32))

    return pl.pallas_call(
        body, name="dwconv_dw", grid=(D // tc,),
        in_specs=[pl.BlockSpec((S, tc), lambda i: (0, i)), pl.BlockSpec((S, tc), lambda i: (0, i))],
        out_specs=pl.BlockSpec((CONV_HALO, tc), lambda i: (0, i)),
        out_shape=jax.ShapeDtypeStruct((CONV_HALO, D), F32),
        scratch_shapes=[pltpu.VMEM((S + 2 * CONV_HALO, tc), F32)],
        compiler_params=_cparams("parallel"),
    )(xin, dy)


def _place():
    x, y, c = lax.axis_index("x"), lax.axis_index("y"), lax.axis_index("c")
    chips = [(1 - x, y), (x, 1 - y), (1 - x, 1 - y)]
    return x, y, c, chips


def _remote(src, dst, ssem, rsem, dev):
    return pltpu.make_async_remote_copy(src_ref=src, dst_ref=dst, send_sem=ssem, recv_sem=rsem,
                                        device_id=dev, device_id_type=MESH_IDS)


_ANY = pl.BlockSpec(memory_space=pl.ANY)


def _place_own(name, own, place):
    R, W = own.shape
    tr = _pick_rows(R)

    def body(p_ref, a_ref, o_ref):
        del p_ref
        o_ref[...] = a_ref[...]

    return pl.pallas_call(
        body, name=name,
        grid_spec=pltpu.PrefetchScalarGridSpec(
            num_scalar_prefetch=1, grid=(R // tr,),
            in_specs=[pl.BlockSpec((tr, W), lambda i, p: (i, 0))],
            out_specs=pl.BlockSpec((None, tr, W), lambda i, p: (p[1], i, 0))),
        out_shape=jax.ShapeDtypeStruct((N_CHIPS, R, W), own.dtype),
        compiler_params=_cparams("parallel"),
    )(place, own)


_HBM = pl.BlockSpec(memory_space=pltpu.HBM)
_SEM = pl.BlockSpec(memory_space=pltpu.SEMAPHORE)
GROUPS = ('qkv', 'ffn', 'pw1', 'dm')


def _half_rows(c, r0, n):
    return pl.ds(pl.multiple_of(r0 + c * (n // 2), SUBLANES), n // 2)


def _gather_start(name, wgs, layers, after):
    G, L = len(wgs), len(layers)

    def body(*refs):
        outs = refs[G + 1:]
        ssems, rsems, bufs = outs[:L], outs[L:2 * L], outs[2 * L:]
        x, y, c, chips = _place()
        me = 2 * x + y
        for li, pieces in enumerate(layers):
            for pi, (g, r0, n) in enumerate(pieces):
                blk = bufs[g].at[me, _half_rows(c, r0, n)]
                for j, (px, py) in enumerate(chips):
                    _remote(blk, blk, ssems[li].at[3 * pi + j], rsems[li].at[3 * pi + j], (px, py, c)).start()

    sem_shapes = [pltpu.SemaphoreType.DMA((3 * len(p),)) for p in layers]
    res = pl.pallas_call(
        body, name=name, in_specs=[_HBM] * G + [_ANY],
        out_specs=[_SEM] * (2 * L) + [_HBM] * G,
        out_shape=sem_shapes + sem_shapes + [pltpu.HBM(w.shape, w.dtype) for w in wgs],
        input_output_aliases={g: 2 * L + g for g in range(G)},
        compiler_params=pltpu.CompilerParams(has_side_effects=pltpu.SideEffectType.DATAFLOW_SIDE_EFFECTING),
    )(*[pltpu.with_memory_space_constraint(w, pltpu.HBM) for w in wgs], after)
    return res[:L], res[L:2 * L], list(res[2 * L:])


def _gather_wait(name, wgs, ssem, rsem, pieces, after):
    G = len(wgs)

    def body(*refs):
        ssem_ref, rsem_ref = refs[G], refs[G + 1]
        bufs = refs[G + 3:]
        x, y, c, chips = _place()
        me = 2 * x + y
        for pi, (g, r0, n) in enumerate(pieces):
            rows = _half_rows(c, r0, n)
            for j, (px, py) in enumerate(chips):
                cp = _remote(bufs[g].at[me, rows], bufs[g].at[2 * px + py, rows],
                             ssem_ref.at[3 * pi + j], rsem_ref.at[3 * pi + j], (px, py, c))
                cp.wait_send()
                cp.wait_recv()

    return list(pl.pallas_call(
        body, name=name, in_specs=[_HBM] * G + [_SEM, _SEM, _ANY], out_specs=[_HBM] * G,
        out_shape=[pltpu.HBM(w.shape, w.dtype) for w in wgs],
        input_output_aliases={g: g for g in range(G)},
        compiler_params=pltpu.CompilerParams(has_side_effects=pltpu.SideEffectType.DATAFLOW_SIDE_EFFECTING),
    )(*wgs, ssem, rsem, after))


def _gather_forward(name, wgs, pieces):
    G = len(wgs)
    n_cp = 3 * len(pieces)

    def body(*refs):
        bufs, ssems, rsems = refs[G:2 * G], refs[2 * G], refs[2 * G + 1]
        x, y, c, chips = _place()
        sib = (x, y, 1 - c)
        cps = []
        for pi, (g, r0, n) in enumerate(pieces):
            for j, (px, py) in enumerate(chips):
                blk = bufs[g].at[2 * px + py, _half_rows(c, r0, n)]
                cps.append(_remote(blk, blk, ssems.at[3 * pi + j], rsems.at[3 * pi + j], sib))
        for cp in cps:
            cp.start()
        for pi, (g, r0, n) in enumerate(pieces):
            for j, (px, py) in enumerate(chips):
                blk = bufs[g].at[2 * px + py, _half_rows(1 - c, r0, n)]
                _remote(blk, blk, ssems.at[3 * pi + j], rsems.at[3 * pi + j], sib).wait_recv()
        for cp in cps:
            cp.wait_send()

    return list(pl.pallas_call(
        body, name=name, in_specs=[_ANY] * G, out_specs=[_ANY] * G,
        out_shape=[jax.ShapeDtypeStruct(w.shape, w.dtype) for w in wgs],
        input_output_aliases={g: g for g in range(G)},
        scratch_shapes=[pltpu.SemaphoreType.DMA((n_cp,)), pltpu.SemaphoreType.DMA((n_cp,))],
        compiler_params=pltpu.CompilerParams(has_side_effects=True),
    )(*wgs))


def _swap_halves(name, dgs, pieces):
    G = len(dgs)
    n_cp = N_CHIPS * len(pieces)

    def body(*refs):
        srcs, lands, ssems, rsems = refs[:G], refs[G:G + len(pieces)], refs[-2], refs[-1]
        x, y, c, _ = _place()
        cps = [_remote(srcs[g].at[j, _half_rows(1 - c, r0, n)], lands[pi].at[j],
                       ssems.at[N_CHIPS * pi + j], rsems.at[N_CHIPS * pi + j], (x, y, 1 - c))
               for pi, (g, r0, n) in enumerate(pieces) for j in range(N_CHIPS)]
        for cp in cps:
            cp.start()
        for cp in cps:
            cp.wait()

    return list(pl.pallas_call(
        body, name=name, in_specs=[_ANY] * G, out_specs=[_ANY] * len(pieces),
        out_shape=[jax.ShapeDtypeStruct((N_CHIPS, n // 2, dgs[g].shape[2]), dgs[g].dtype) for g, _, n in pieces],
        scratch_shapes=[pltpu.SemaphoreType.DMA((n_cp,)), pltpu.SemaphoreType.DMA((n_cp,))],
        compiler_params=pltpu.CompilerParams(has_side_effects=True),
    )(*dgs))


def _scatter_start(name, pbs):
    P = len(pbs)

    def body(*refs):
        outs = refs[2 * P:]
        ssems, rsems, src, land = outs[0], outs[1], outs[2:2 + P], outs[2 + P:]
        x, y, c, chips = _place()
        me = 2 * x + y
        for pi in range(P):
            for j, (px, py) in enumerate(chips):
                _remote(src[pi].at[2 * px + py], land[pi].at[me], ssems.at[3 * pi + j], rsems.at[3 * pi + j], (px, py, c)).start()

    sems = pltpu.SemaphoreType.DMA((3 * P,))
    hbm = [pltpu.HBM(p.shape, p.dtype) for p in pbs]
    res = pl.pallas_call(
        body, name=name, in_specs=[_HBM] * (2 * P), out_specs=[_SEM, _SEM] + [_HBM] * (2 * P),
        out_shape=[sems, sems] + hbm + hbm,
        input_output_aliases={k: 2 + k for k in range(2 * P)},
        compiler_params=pltpu.CompilerParams(has_side_effects=pltpu.SideEffectType.DATAFLOW_SIDE_EFFECTING),
    )(*[pltpu.with_memory_space_constraint(p, pltpu.HBM) for p in pbs],
      *[pltpu.with_memory_space_constraint(lax.empty(p.shape, p.dtype), pltpu.HBM) for p in pbs])
    return res[0], res[1], list(res[2:2 + P]), list(res[2 + P:])


def _scatter_wait(name, ssem, rsem, pbs, lands, after):
    P = len(pbs)

    def body(*refs):
        ssems, rsems = refs[2 * P], refs[2 * P + 1]
        outs = refs[2 * P + 3:]
        src, land = outs[:P], outs[P:]
        x, y, c, chips = _place()
        for pi in range(P):
            for j, (px, py) in enumerate(chips):
                cp = _remote(src[pi].at[2 * px + py], land[pi].at[2 * px + py], ssems.at[3 * pi + j], rsems.at[3 * pi + j], (px, py, c))
                cp.wait_send()
                cp.wait_recv()

    hbm = [pltpu.HBM(p.shape, p.dtype) for p in pbs]
    res = pl.pallas_call(
        body, name=name, in_specs=[_HBM] * (2 * P) + [_SEM, _SEM, _ANY], out_specs=[_HBM] * (2 * P),
        out_shape=hbm + hbm, input_output_aliases={k: k for k in range(2 * P)},
        compiler_params=pltpu.CompilerParams(has_side_effects=pltpu.SideEffectType.DATAFLOW_SIDE_EFFECTING),
    )(*pbs, *lands, ssem, rsem, after)
    return list(res[:P]), list(res[P:])


def _share_halves(name, gf, pieces):
    def body(in_ref, out, ssems, rsems):
        del in_ref
        x, y, c, _ = _place()
        cps = []
        for pi, (r0, n) in enumerate(pieces):
            mine = out.at[_half_rows(c, r0, n)]
            cps.append(_remote(mine, mine, ssems.at[pi], rsems.at[pi], (x, y, 1 - c)))
        for cp in cps:
            cp.start()
        for pi, (r0, n) in enumerate(pieces):
            theirs = out.at[_half_rows(1 - c, r0, n)]
            _remote(theirs, theirs, ssems.at[pi], rsems.at[pi], (x, y, 1 - c)).wait_recv()
        for cp in cps:
            cp.wait_send()

    return pl.pallas_call(
        body, name=name, in_specs=[_ANY], out_specs=_ANY,
        out_shape=jax.ShapeDtypeStruct(gf.shape, gf.dtype), input_output_aliases={0: 0},
        scratch_shapes=[pltpu.SemaphoreType.DMA((len(pieces),)), pltpu.SemaphoreType.DMA((len(pieces),))],
        compiler_params=pltpu.CompilerParams(has_side_effects=True),
    )(gf)


def _broadcast_small(name, buf, after=None):
    R = buf.shape[0]

    def body(src, *rest):
        out, ssems, rsems = rest[-3:]
        x, y, c, _ = _place()
        me = 4 * x + 2 * y + c
        out[me] = src[...]
        peers = []
        for mask in range(1, 8):
            fx, fy, fc = (mask >> 2) & 1, (mask >> 1) & 1, mask & 1
            peers.append((1 - x if fx else x, 1 - y if fy else y, 1 - c if fc else c))
        cps = [_remote(src, out.at[me], ssems.at[k], rsems.at[k], p) for k, p in enumerate(peers)]
        for cp in cps:
            cp.start()
        for k, (px, py, pc) in enumerate(peers):
            blk = out.at[4 * px + 2 * py + pc]
            _remote(blk, blk, ssems.at[k], rsems.at[k], (px, py, pc)).wait_recv()
        for cp in cps:
            cp.wait_send()

    return pl.pallas_call(
        body, name=name, in_specs=[pl.BlockSpec(memory_space=pltpu.VMEM)] + ([_ANY] if after is not None else []),
        out_specs=pl.BlockSpec(memory_space=pltpu.VMEM),
        out_shape=jax.ShapeDtypeStruct((8, R, LANES), F32),
        scratch_shapes=[pltpu.SemaphoreType.DMA((7,)), pltpu.SemaphoreType.DMA((7,))],
        compiler_params=pltpu.CompilerParams(has_side_effects=True, vmem_limit_bytes=VMEM_LIMIT_V7X),
    )(buf, *([after] if after is not None else []))


def _sum_slabs(name, slabs):
    n, R, _ = slabs.shape

    def body(s_ref, o_ref):
        acc = s_ref[0]
        for k in range(1, n):
            acc = acc + s_ref[k]
        o_ref[...] = acc

    return pl.pallas_call(
        body, name=name, out_shape=jax.ShapeDtypeStruct((R, LANES), F32),
        in_specs=[pl.BlockSpec(memory_space=pltpu.VMEM)], out_specs=pl.BlockSpec(memory_space=pltpu.VMEM),
        compiler_params=_cparams(),
    )(slabs)


def _pick_rows(rows, target=512):
    best = SUBLANES
    for t in range(SUBLANES, min(rows, target) + 1, SUBLANES):
        if rows % t == 0:
            best = t
    return best


def _half_tile(r0, n):
    return _pick_rows(math.gcd(r0, n // 2) if r0 else n // 2)


def _pair_sum(name, dg, land, place, r0, n):
    W = dg.shape[2]
    tr = _half_tile(r0, n)
    nb = (n // 2) // tr

    def body(p_ref, a_ref, b_ref, o_ref):
        del p_ref
        o_ref[...] = (a_ref[...].astype(F32) + b_ref[...].astype(F32)).astype(o_ref.dtype)

    return pl.pallas_call(
        body, name=name,
        grid_spec=pltpu.PrefetchScalarGridSpec(
            num_scalar_prefetch=1, grid=(N_CHIPS, nb),
            in_specs=[pl.BlockSpec((None, tr, W), lambda j, i, p: (j, r0 // tr + p[0] * nb + i, 0)),
                      pl.BlockSpec((None, tr, W), lambda j, i, p: (j, i, 0))],
            out_specs=pl.BlockSpec((None, tr, W), lambda j, i, p: (j, i, 0))),
        out_shape=jax.ShapeDtypeStruct((N_CHIPS, n // 2, W), BF16),
        compiler_params=_cparams("parallel", "parallel"),
    )(place, dg, land)


def _chip_sum(name, pb, land, place, gf, r0, n):
    W = gf.shape[1]
    tr = _half_tile(r0, n)
    nb = (n // 2) // tr

    def body(p_ref, own_ref, lx_ref, ly_ref, ld_ref, gf_in, o_ref):
        del p_ref, gf_in
        o_ref[...] = ((own_ref[...].astype(F32) + lx_ref[...].astype(F32)) + ly_ref[...].astype(F32)) + ld_ref[...].astype(F32)

    slab = lambda flip: pl.BlockSpec((None, tr, W), lambda i, p, _f=flip: (p[1] ^ _f, i, 0))
    return pl.pallas_call(
        body, name=name,
        grid_spec=pltpu.PrefetchScalarGridSpec(
            num_scalar_prefetch=1, grid=(nb,),
            in_specs=[slab(0), slab(2), slab(1), slab(3), pl.BlockSpec(memory_space=pl.ANY)],
            out_specs=pl.BlockSpec((tr, W), lambda i, p: (r0 // tr + p[0] * nb + i, 0))),
        out_shape=jax.ShapeDtypeStruct(gf.shape, F32),
        input_output_aliases={5: 0},
        compiler_params=_cparams("parallel"),
    )(place, pb, land, land, land, gf)


def _ln_stats(z):
    mu = jnp.mean(z, axis=1, keepdims=True)
    zc = z - mu
    rstd = lax.rsqrt(jnp.mean(zc * zc, axis=1, keepdims=True) + LN_EPS)
    return zc * rstd, rstd


def _ln_fwd(name, xin, m, g, b):
    S, D = xin.shape

    def fn(x_, m_, g_, b_):
        xhat, rstd = _ln_stats(ALPHA * x_ + m_)
        y = xhat * g_ + b_
        return y, y, xhat, rstd

    return _rowwise(name, fn, [xin, m, ('full', g), ('full', b)],
                    [('rows', D, F32), ('rows', D, BF16), ('rows', D, F32), ('rows', 1, F32)], S)


def _ln_bwd_core(dy, xhat, rstd, g):
    dxh = dy * g
    return rstd * (dxh - jnp.mean(dxh, axis=1, keepdims=True) - xhat * jnp.mean(dxh * xhat, axis=1, keepdims=True))


def _ln_bwd(name, terms, xhat, rstd, g, after=None):
    S, D = xhat.shape
    scales = [s for _, s in terms]
    n = len(terms)

    def fn(*v):
        dy = v[0] * scales[0] if scales[0] != 1.0 else v[0]
        for t in range(1, n):
            dy = dy + (v[t] * scales[t] if scales[t] != 1.0 else v[t])
        xh, rs, g_ = v[n], v[n + 1], v[n + 2]
        dz = _ln_bwd_core(dy, xh, rs, g_)
        return dz, dz, jnp.sum(dy * xh, axis=0, keepdims=True), jnp.sum(dy, axis=0, keepdims=True)

    return _rowwise(name, fn, [a for a, _ in terms] + [xhat, rstd, ('full', g)] + ([('after', after)] if after is not None else []),
                    [('rows', D, F32), ('rows', D, BF16), ('acc', (1, D), F32), ('acc', (1, D), F32)], S)


def _adamw_math(w, g, m, v):
    m2 = ADAM_B1 * m + (1.0 - ADAM_B1) * g
    v2 = ADAM_B2 * v + (1.0 - ADAM_B2) * (g * g)
    m_hat = m2 / (1.0 - ADAM_B1 ** ADAM_STEP)
    v_hat = v2 / (1.0 - ADAM_B2 ** ADAM_STEP)
    delta = -ADAM_LR * (m_hat / (jnp.sqrt(v_hat) + ADAM_EPS) + ADAM_WD * w)
    return delta, m2, v2


def _adamw(name, w, gfull, row_start, m, v):
    rows, W = w.shape
    tr = math.gcd(math.gcd(rows, row_start), 256) if row_start else math.gcd(rows, 256)

    def fn(w_, g_, m_, v_):
        d, m2, v2 = _adamw_math(w_, g_, m_, v_)
        return g_, d, m2, v2

    return _rowwise(name, fn, [w, ('off', gfull, row_start // tr), m, v], [('rows', W, F32)] * 4, rows, tm=tr)


def _pack(arrs):
    flat = jnp.concatenate([a.reshape(-1).astype(F32) for a in arrs])
    tile = SUBLANES * LANES
    n = -(-flat.shape[0] // tile) * tile
    return jnp.pad(flat, (0, n - flat.shape[0])).reshape(-1, LANES)


def _unpack(buf, shapes):
    flat = buf.reshape(-1)
    out, pos = [], 0
    for shp in shapes:
        n = math.prod(shp)
        out.append(flat[pos:pos + n].reshape(shp))
        pos += n
    return out


BIG = ['fox_w_qkv', 'fox_w_o', 'rel_w_qkv', 'rel_w_o', 'conv_w_pw1', 'conv_w_pw2', 'ffn_w_gate', 'ffn_w_up', 'ffn_w_down']
SMALL_SHARDED = ['fox_w_f', 'conv_b_pw1', 'conv_w_dw', 'conv_b_dw', 'conv_ln_g', 'conv_ln_b', 'conv_b_pw2']
SMALL_SHARD_AXIS = {'fox_w_f': 1, 'conv_b_pw1': 1, 'conv_w_dw': 2, 'conv_b_dw': 1, 'conv_ln_g': 1, 'conv_ln_b': 1, 'conv_b_pw2': 1}
SMALL_REPL = ['fox_b_f', 'rel_bias', 'ln_mix_g', 'ln_mix_b', 'ln_ffn_g', 'ln_ffn_b']
SMALL = SMALL_SHARDED + SMALL_REPL
WEIGHTS = ['fox_w_qkv', 'fox_w_f', 'fox_b_f', 'fox_w_o', 'rel_w_qkv', 'rel_bias', 'rel_w_o', 'conv_w_pw1', 'conv_b_pw1',
           'conv_w_dw', 'conv_b_dw', 'conv_ln_g', 'conv_ln_b', 'conv_w_pw2', 'conv_b_pw2', 'ffn_w_gate', 'ffn_w_up',
           'ffn_w_down', 'ln_mix_g', 'ln_mix_b', 'ln_ffn_g', 'ln_ffn_b']


def kernel(x, fox_w_qkv, fox_w_f, fox_b_f, fox_w_o, rel_w_qkv, rel_bias, rel_w_o, conv_w_pw1, conv_b_pw1, conv_w_dw, conv_b_dw, conv_ln_g, conv_ln_b, conv_w_pw2, conv_b_pw2, ffn_w_gate, ffn_w_up, ffn_w_down, ln_mix_g, ln_mix_b, ln_ffn_g, ln_ffn_b, loss_target, m_fox_w_qkv, m_fox_w_f, m_fox_b_f, m_fox_w_o, m_rel_w_qkv, m_rel_bias, m_rel_w_o, m_conv_w_pw1, m_conv_b_pw1, m_conv_w_dw, m_conv_b_dw, m_conv_ln_g, m_conv_ln_b, m_conv_w_pw2, m_conv_b_pw2, m_ffn_w_gate, m_ffn_w_up, m_ffn_w_down, m_ln_mix_g, m_ln_mix_b, m_ln_ffn_g, m_ln_ffn_b, v_fox_w_qkv, v_fox_w_f, v_fox_b_f, v_fox_w_o, v_rel_w_qkv, v_rel_bias, v_rel_w_o, v_conv_w_pw1, v_conv_b_pw1, v_conv_w_dw, v_conv_b_dw, v_conv_ln_g, v_conv_ln_b, v_conv_w_pw2, v_conv_b_pw2, v_ffn_w_gate, v_ffn_w_up, v_ffn_w_down, v_ln_mix_g, v_ln_mix_b, v_ln_ffn_g, v_ln_ffn_b):
    A = dict(locals())
    Wt = {n: A[n] for n in WEIGHTS}
    Mo = {n: A['m_' + n] for n in WEIGHTS}
    Vo = {n: A['v_' + n] for n in WEIGHTS}

    _, S, D = x.shape
    H = D // HEAD_DIM
    Ds = D // N_CHIPS
    Nq = fox_w_qkv.shape[2]
    Np = conv_w_pw1.shape[2]
    Fs = ffn_w_gate.shape[2]
    my_x, my_y, my_c = lax.axis_index("x"), lax.axis_index("y"), lax.axis_index("c")
    my_chip = 2 * my_x + my_y
    place = jnp.stack([my_c, my_chip]).astype(jnp.int32)

    wo_base = DEPTH * Fs
    where = {
        'fox_w_qkv': ('qkv', 0), 'rel_w_qkv': ('qkv', N_FOX * D),
        'ffn_w_gate': ('ffn', 0), 'ffn_w_up': ('ffn', DEPTH * D),
        'conv_w_pw1': ('pw1', 0),
        'ffn_w_down': ('dm', 0), 'fox_w_o': ('dm', wo_base), 'rel_w_o': ('dm', wo_base + N_FOX * Ds),
        'conv_w_pw2': ('dm', wo_base + (N_FOX + 1) * Ds),
    }
    members = {'qkv': ['fox_w_qkv', 'rel_w_qkv'], 'ffn': ['ffn_w_gate', 'ffn_w_up'], 'pw1': ['conv_w_pw1'],
               'dm': ['ffn_w_down', 'fox_w_o', 'rel_w_o', 'conv_w_pw2']}
    flat2 = lambda a: a.reshape(-1, a.shape[-1])
    own = {g: jnp.concatenate([flat2(Wt[n]).astype(BF16) for n in ms], axis=0) for g, ms in members.items()}

    def layer_pieces(i):
        kind, j = i % 3, i // 3
        slot = j if kind == 0 else (N_FOX if kind == 1 else N_FOX + 1)
        w_in = (GROUPS.index('pw1'), 0, D) if kind == 2 else (GROUPS.index('qkv'), slot * D, D)
        return [w_in, (GROUPS.index('dm'), wo_base + slot * Ds, Ds), (GROUPS.index('ffn'), i * D, D),
                (GROUPS.index('ffn'), (DEPTH + i) * D, D), (GROUPS.index('dm'), i * Fs, Fs)]

    small_shapes = [Wt[n].shape for n in SMALL_SHARDED]
    slabs = _broadcast_small("gather_small", _pack([Wt[n] for n in SMALL_SHARDED]))

    stages = [part for i in range(DEPTH) for part in (layer_pieces(i)[:2], layer_pieces(i)[2:])]
    gstages = [part for i in range(DEPTH) for part in (layer_pieces(i)[:1], layer_pieces(i)[1:])]
    (g_first, r_first, n_first), = gstages[0]
    buf_first = _place_own("place_" + GROUPS[g_first], own[GROUPS[g_first]], place)
    ssems_first, rsems_first, (buf_first,) = _gather_start("gather_start_first", [buf_first], [[(0, r_first, n_first)]], slabs)
    ssems_rest, rsems_rest, wg_list = _gather_start(
        "gather_start_rest", [buf_first if gi == g_first else _place_own("place_" + g, own[g], place) for gi, g in enumerate(GROUPS)],
        gstages[1:], slabs)
    gather_ssems, gather_rsems = list(ssems_first) + list(ssems_rest), list(rsems_first) + list(rsems_rest)
    WG = dict(zip(GROUPS, wg_list))
    DG = {g: lax.empty(WG[g].shape, BF16) for g in own}

    per_chip = [_unpack(slabs[2 * j], small_shapes) for j in range(N_CHIPS)]
    full = {n: jnp.concatenate([per_chip[j][i] for j in range(N_CHIPS)], axis=SMALL_SHARD_AXIS[n])
            for i, n in enumerate(SMALL_SHARDED)}
    row = lambda v: v.reshape(1, -1)

    SG = {}

    def ffn_fwd(i, xb):
        hg, hu, act = _mm_gate_up(f"ffn{i}_gate_up", xb, WG['ffn'], i * D, (DEPTH + i) * D)
        f = _mm_row(f"ffn{i}_down", act, WG['dm'], i * Fs, Fs)
        return f, (hg, hu, act)

    def ffn_bwd(i, xb, saved, dzb):
        hg, hu, act = saved
        DG['dm'] = _mm_dw(f"ffn{i}_dw_down", act, dzb, DG['dm'], i * Fs, 'row')
        dhg, dhu = _mm_dact(f"ffn{i}_dact", dzb, WG['dm'], i * Fs, Fs, hg, hu)
        DG['ffn'] = _mm_dw(f"ffn{i}_dw_gate", xb, dhg, DG['ffn'], i * D, 'col')
        DG['ffn'] = _mm_dw(f"ffn{i}_dw_up", xb, dhu, DG['ffn'], (DEPTH + i) * D, 'col')
        return _mm_col_t(f"ffn{i}_dx", [(dhg, i * D), (dhu, (DEPTH + i) * D)], WG['ffn'], D)

    def fox_fwd(j, xb, rest_stage):
        qkv = _mm_col(f"fox{j}_qkv", xb, WG['qkv'], j * D)
        wf = full['fox_w_f'][j].astype(BF16)
        def gate_fn(x_, w_, b_):
            z_ = jnp.dot(x_, w_, preferred_element_type=F32) + b_
            return z_, jnp.minimum(z_, 0.0) - jnp.log(1.0 + jnp.exp(-jnp.abs(z_)))

        z, logf = _rowwise(f"fox{j}_gate", gate_fn, [xb, ('full', wf), ('full', row(fox_b_f[j]))],
                           [('rows', H, F32), ('rows', H, F32)], S)
        c = _cumsum_rows(f"fox{j}_cumsum", logf, False)
        crow = c.T.reshape(H, 1, S)
        o = _fox_fwd(qkv, c, crow, H)
        weights_ready(rest_stage, o)
        m = _mm_row(f"fox{j}_wo", o, WG['dm'], wo_base + j * Ds, Ds)
        return m, (qkv, z, c, crow, o, wf)

    def fox_bwd(j, xb, saved, dzb):
        qkv, z, c, crow, o, wf = saved
        DG['dm'] = _mm_dw(f"fox{j}_dw_o", o, dzb, DG['dm'], wo_base + j * Ds, 'row')
        do = _mm_row_t(f"fox{j}_do", dzb, WG['dm'], wo_base + j * Ds, Ds, BF16)
        dq, dk, dv, dcrow = _fox_bwd(qkv, c, crow, do, H)
        dqkv = jnp.concatenate([dq, dk, dv], axis=1)
        dlogf = _cumsum_rows(f"fox{j}_rcumsum", dcrow.reshape(H, S).T, True)

        def fn(x_, dl_, z_, w_):
            dz_ = dl_ * _sigmoid(-z_)
            dzb_ = dz_.astype(BF16)
            return _dot_nt(dzb_, w_), _dot_tn(x_, dzb_), jnp.sum(dz_, axis=0, keepdims=True)

        dh_f, dwf, dbf = _rowwise(f"fox{j}_gate_bwd", fn, [xb, dlogf, z, ('full', wf)],
                                  [('rows', D, F32), ('acc', (D, H), F32), ('acc', (1, H), F32)], S)
        SG.setdefault('fox_w_f', [None] * N_FOX)[j] = dwf
        SG.setdefault('fox_b_f', [None] * N_FOX)[j] = dbf.reshape(H)
        DG['qkv'] = _mm_dw(f"fox{j}_dw_qkv", xb, dqkv, DG['qkv'], j * D, 'col')
        dh = _mm_col_t(f"fox{j}_dx", [(dqkv, j * D)], WG['qkv'], D)
        return [dh, dh_f]

    def rel_fwd(xb, rest_stage):
        qkv = _mm_col("rel_qkv", xb, WG['qkv'], N_FOX * D)
        rb_pad = jnp.pad(rel_bias[0], ((0, 0), (0, REL_TABLE_PAD - REL_TABLE)))
        bias = _rel_window_bias(jnp.transpose(_rel_expand(rb_pad), (1, 0, 2)))
        o = _rel_fwd(qkv, bias, H)
        weights_ready(rest_stage, o)
        m = _mm_row("rel_wo", o, WG['dm'], wo_base + N_FOX * Ds, Ds)
        return m, (qkv, bias, o)

    def rel_bwd(xb, saved, dzb):
        qkv, bias, o = saved
        DG['dm'] = _mm_dw("rel_dw_o", o, dzb, DG['dm'], wo_base + N_FOX * Ds, 'row')
        do = _mm_row_t("rel_do", dzb, WG['dm'], wo_base + N_FOX * Ds, Ds, BF16)
        dq, dk, dv, dbias = _rel_bwd(qkv, bias, do, H)
        SG['rel_bias'] = _rel_reduce(jnp.transpose(dbias, (1, 0, 2)))[:, :REL_TABLE].reshape(1, H, REL_TABLE)
        dqkv = jnp.concatenate([dq, dk, dv], axis=1)
        DG['qkv'] = _mm_dw("rel_dw_qkv", xb, dqkv, DG['qkv'], N_FOX * D, 'col')
        return [_mm_col_t("rel_dx", [(dqkv, N_FOX * D)], WG['qkv'], D)]

    w_dw32 = jnp.pad(full['conv_w_dw'][0], ((0, CONV_HALO - CONV_K), (0, 0)))
    cg, cb = full['conv_ln_g'], full['conv_ln_b']

    def conv_fwd(xb, rest_stage):
        u = _mm_col("conv_pw1", xb, WG['pw1'], 0, bias=full['conv_b_pw1'], out_dtype=F32)
        u2, = _rowwise("conv_glu", lambda a_, g_: [a_ * _sigmoid(g_)],
                       [('cols', u, D, 0), ('cols', u, D, 1)], [('rows', D, F32)], S)
        yc = _dwconv("conv_dw", u2, w_dw32, full['conv_b_dw'], False)

        def fn(y_, g_, b_):
            xhat, rstd = _ln_stats(y_)
            ln = xhat * g_ + b_
            return ln * _sigmoid(ln), xhat, rstd

        zc, xhat, rstd = _rowwise("conv_ln_silu", fn, [yc, ('full', cg), ('full', cb)],
                                  [('rows', D, BF16), ('rows', D, F32), ('rows', 1, F32)], S)
        weights_ready(rest_stage, zc)
        m = _mm_row("conv_pw2", zc, WG['dm'], wo_base + (N_FOX + 1) * Ds, Ds, bias=full['conv_b_pw2'])
        return m, (u, u2, zc, xhat, rstd)

    def conv_bwd(xb, saved, dz, dzb):
        u, u2, zc, xhat, rstd = saved
        r0 = wo_base + (N_FOX + 1) * Ds
        DG['dm'] = _mm_dw("conv_dw_pw2", zc, dzb, DG['dm'], r0, 'row')
        dzc = _mm_row_t("conv_dzc", dzb, WG['dm'], r0, Ds, F32)

        def fn(dm_, dzc_, xh_, rs_, g_, b_):
            ln = xh_ * g_ + b_
            sg = _sigmoid(ln)
            dln = dzc_ * (sg * (1.0 + ln * (1.0 - sg)))
            dyc = _ln_bwd_core(dln, xh_, rs_, g_)
            col = lambda t: jnp.sum(t, axis=0, keepdims=True)
            return dyc, col(dm_), col(dln * xh_), col(dln), col(dyc)

        dyc, SG['conv_b_pw2'], SG['conv_ln_g'], SG['conv_ln_b'], SG['conv_b_dw'] = _rowwise(
            "conv_ln_silu_bwd", fn, [dz, dzc, xhat, rstd, ('full', cg), ('full', cb)],
            [('rows', D, F32)] + [('acc', (1, D), F32)] * 4, S)
        du2 = _dwconv("conv_dw_bwd_x", dyc, w_dw32, jnp.zeros((1, D), F32), True)
        SG['conv_w_dw'] = _dwconv_dw(u2, dyc)[:CONV_K].reshape(1, CONV_K, D)

        def fn2(du2_, a_, g_):
            sg = _sigmoid(g_)
            da, dgt = du2_ * sg, du2_ * a_ * sg * (1.0 - sg)
            return da, dgt, jnp.sum(da, axis=0, keepdims=True), jnp.sum(dgt, axis=0, keepdims=True)

        da, dgt, dba, dbg = _rowwise("conv_glu_bwd", fn2, [du2, ('cols', u, D, 0), ('cols', u, D, 1)],
                                     [('rows', D, BF16), ('rows', D, BF16), ('acc', (1, D), F32), ('acc', (1, D), F32)], S)
        SG['conv_b_pw1'] = jnp.concatenate([dba, dbg], axis=1)
        du = jnp.concatenate([da, dgt], axis=1)
        DG['pw1'] = _mm_dw("conv_dw_pw1", xb, du, DG['pw1'], 0, 'col')
        return [_mm_col_t("conv_dx", [(du, 0)], WG['pw1'], D)]

    xs = x[0]
    xs_b = xs.astype(BF16)
    tape = []

    def weights_ready(s, after):
        bufs = _gather_wait(f"gather_wait{s}", [WG[g] for g in GROUPS], gather_ssems[s], gather_rsems[s], gstages[s], after)
        WG.update(zip(GROUPS, _gather_forward(f"gather_fwd{s}", bufs, gstages[s])))

    for i in range(DEPTH):
        kind, j = i % 3, i // 3
        weights_ready(2 * i, xs)
        if kind == 0:
            m, msaved = fox_fwd(j, xs_b, 2 * i + 1)
        elif kind == 1:
            m, msaved = rel_fwd(xs_b, 2 * i + 1)
        else:
            m, msaved = conv_fwd(xs_b, 2 * i + 1)
        xm, xm_b, xhat1, rstd1 = _ln_fwd(f"ln_mix{i}", xs, m, row(ln_mix_g[i]), row(ln_mix_b[i]))
        f, fsaved = ffn_fwd(i, xm_b)
        xo, xo_b, xhat2, rstd2 = _ln_fwd(f"ln_ffn{i}", xm, f, row(ln_ffn_g[i]), row(ln_ffn_b[i]))
        tape.append((xs_b, msaved, xhat1, rstd1, xm_b, fsaved, xhat2, rstd2))
        xs, xs_b = xo, xo_b

    def loss_fn(y_, t_):
        e = y_ - t_
        return e * (1.0 / D), jnp.sum(e * e, axis=0, keepdims=True)

    dy, sq = _rowwise("loss", loss_fn, [xs, loss_target[0]], [('rows', D, F32), ('acc', (1, D), F32)], S)
    loss = lax.psum(jnp.sum(sq) * (0.5 / D), ("x", "y", "c"))

    GF = {g: lax.empty(WG[g].shape[1:], F32) for g in GROUPS}
    started = [None] * len(stages)

    def reduce_start(s):
        dgs = [DG[g] for g in GROUPS]
        lands = _swap_halves(f"pair_swap{s}", dgs, stages[s])
        pbs = [_pair_sum(f"pair_sum{s}_{pi}", dgs[g], lands[pi], place, r0, n) for pi, (g, r0, n) in enumerate(stages[s])]
        started[s] = _scatter_start(f"scatter_start{s}", pbs)
        token = started[s][2][0]
        if s + 1 < len(stages):
            reduce_finish(s + 1, token)
        return token

    def reduce_finish(s, after):
        ssem, rsem, pbs, lands2 = started[s]
        pbs, lands2 = _scatter_wait(f"scatter_wait{s}", ssem, rsem, pbs, lands2, after)
        for pi, (g, r0, n) in enumerate(stages[s]):
            GF[GROUPS[g]] = _chip_sum(f"chip_sum{s}_{pi}", pbs[pi], lands2[pi], place, GF[GROUPS[g]], r0, n)

    terms = [(dy, 1.0)]
    token = None
    g_mix, b_mix, g_ffn, b_ffn = [None] * DEPTH, [None] * DEPTH, [None] * DEPTH, [None] * DEPTH
    for i in reversed(range(DEPTH)):
        kind, j = i % 3, i // 3
        xin_b, msaved, xhat1, rstd1, xm_b, fsaved, xhat2, rstd2 = tape[i]
        dz2, dz2b, g_ffn[i], b_ffn[i] = _ln_bwd(f"ln_ffn{i}_bwd", terms, xhat2, rstd2, row(ln_ffn_g[i]), after=token)
        dx_ffn = ffn_bwd(i, xm_b, fsaved, dz2b)
        token = reduce_start(2 * i + 1)
        dz1, dz1b, g_mix[i], b_mix[i] = _ln_bwd(f"ln_mix{i}_bwd", [(dz2, ALPHA), (dx_ffn, 1.0)], xhat1, rstd1,
                                                row(ln_mix_g[i]), after=token)
        if kind == 0:
            mix_terms = fox_bwd(j, xin_b, msaved, dz1b)
        elif kind == 1:
            mix_terms = rel_bwd(xin_b, msaved, dz1b)
        else:
            mix_terms = conv_bwd(xin_b, msaved, dz1, dz1b)
        terms = [(dz1, ALPHA)] + [(t, 1.0) for t in mix_terms]
        token = reduce_start(2 * i)

    def gx_fn(*v):
        acc = v[0] * ALPHA
        for t in v[1:]:
            acc = acc + t
        return [acc]

    grad_x, = _rowwise("grad_x", gx_fn, [a for a, _ in terms], [('rows', D, F32)], S)
    grad_x = grad_x.reshape(1, S, D)

    SG['fox_w_f'] = jnp.stack(SG['fox_w_f'])
    SG['fox_b_f'] = jnp.stack(SG['fox_b_f'])
    SG['ln_mix_g'] = jnp.concatenate(g_mix, axis=0)
    SG['ln_mix_b'] = jnp.concatenate(b_mix, axis=0)
    SG['ln_ffn_g'] = jnp.concatenate(g_ffn, axis=0)
    SG['ln_ffn_b'] = jnp.concatenate(b_ffn, axis=0)

    grads, deltas, new_m, new_v = {}, {}, {}, {}

    reduce_finish(0, grad_x)
    links_idle = GF['dm']
    for gi, g in enumerate(GROUPS):
        GF[g] = _share_halves("pair_share_" + g, GF[g], [(r0, n) for st in stages for (pg, r0, n) in st if pg == gi])

    for n in BIG:
        g, r0 = where[n]
        outs = _adamw("adamw_" + n, flat2(Wt[n]), GF[g], r0, flat2(Mo[n]), flat2(Vo[n]))
        grads[n], deltas[n], new_m[n], new_v[n] = [o.reshape(Wt[n].shape) for o in outs]

    full_shapes = [SG[n].shape for n in SMALL]
    summed = _sum_slabs("small_sum", _broadcast_small("small_exchange", _pack([SG[n] for n in SMALL]), after=links_idle))
    gsm = dict(zip(SMALL, _unpack(summed, full_shapes)))
    for n in SMALL_SHARDED:
        ax = SMALL_SHARD_AXIS[n]
        width = Wt[n].shape[ax]
        gsm[n] = lax.dynamic_slice_in_dim(gsm[n], my_chip * width, width, axis=ax)
    own_shapes = [Wt[n].shape for n in SMALL]
    packed = [_pack([src[n] for n in SMALL]) for src in (Wt, gsm, Mo, Vo)]
    rows_small = packed[0].shape[0]

    def small_fn(w_, g_, m_, v_):
        return _adamw_math(w_, g_, m_, v_)

    sd, sm, sv = _rowwise("adamw_small", small_fn, packed, [('rows', LANES, F32)] * 3, rows_small, tm=rows_small)
    for n, d_, m_, v_ in zip(SMALL, _unpack(sd, own_shapes), _unpack(sm, own_shapes), _unpack(sv, own_shapes)):
        grads[n], deltas[n], new_m[n], new_v[n] = gsm[n], d_, m_, v_

    return (loss, grad_x, *[grads[n] for n in WEIGHTS], *[deltas[n] for n in WEIGHTS],
            *[new_m[n] for n in WEIGHTS], *[new_v[n] for n in WEIGHTS])
```

```python
import functools
import math

import jax
import jax.numpy as jnp
from jax import lax
from jax.experimental import pallas as pl
from jax.experimental.pallas import tpu as pltpu

F32 = jnp.float32
BF16 = jnp.bfloat16
MESH_IDS = pl.DeviceIdType.MESH
HIGHEST = lax.Precision.HIGHEST

N_CHIPS = 4
DEPTH = 4
N_FOX = 2
HEAD_DIM = 128
CHUNK = 64
LEFT_CHUNKS = 8
BAND_KEYS = (LEFT_CHUNKS + 1) * CHUNK
PAD_KEYS = LEFT_CHUNKS * CHUNK
REL_CLIP = 128
REL_TABLE = 2 * REL_CLIP + 1
REL_TABLE_PAD = 384
REL_QB = 4 * CHUNK
REL_WIN = REL_QB + PAD_KEYS
CONV_K = 31
CONV_HALO = 32
ALPHA = (2.0 * DEPTH) ** 0.25
LN_EPS = 1e-5
ADAM_LR, ADAM_B1, ADAM_B2, ADAM_EPS, ADAM_WD, ADAM_STEP = 0.001, 0.9, 0.999, 1e-08, 0.01, 10
NEG_BIG = -1e30
VMEM_LIMIT_V7X = 56 * 1024 * 1024
LANES = 128
SUBLANES = 8
MM_ROWS = 1024


def _cparams(*sem):
    return pltpu.CompilerParams(dimension_semantics=sem if sem else None, vmem_limit_bytes=VMEM_LIMIT_V7X)


def _pick(dim, target):
    best = None
    for t in range(LANES, min(dim, target) + 1, LANES):
        if dim % t == 0:
            best = t
    return best if best is not None else dim


def _dot_nt(a, b):
    return lax.dot_general(a, b, (((1,), (1,)), ((), ())), preferred_element_type=F32)


def _dot_tn(a, b):
    return lax.dot_general(a, b, (((0,), (0,)), ((), ())), preferred_element_type=F32)


def _sigmoid(z):
    return 1.0 / (1.0 + jnp.exp(-z))


def _rowwise(name, fn, ins, outs, S, tm=256):
    tm = min(tm, S)
    afters = [it[1] for it in ins if isinstance(it, tuple) and it[0] == 'after']
    ins = [it for it in ins if not (isinstance(it, tuple) and it[0] == 'after')]
    arrs, in_specs = [], []
    for it in ins:
        if isinstance(it, tuple) and it[0] == 'full':
            a = it[1]
            in_specs.append(pl.BlockSpec(a.shape, lambda i, _n=a.ndim: (0,) * _n))
        elif isinstance(it, tuple) and it[0] == 'cols':
            _, a, width, blk = it
            in_specs.append(pl.BlockSpec((tm, width), lambda i, _b=blk: (i, _b)))
        elif isinstance(it, tuple) and it[0] == 'off':
            _, a, off = it
            in_specs.append(pl.BlockSpec((tm, a.shape[1]), lambda i, _o=off: (i + _o, 0)))
        else:
            a = it
            in_specs.append(pl.BlockSpec((tm, a.shape[1]), lambda i: (i, 0)))
        arrs.append(a)
    out_shape, out_specs = [], []
    for kind, shp, dt in outs:
        if kind == 'rows':
            out_shape.append(jax.ShapeDtypeStruct((S, shp), dt))
            out_specs.append(pl.BlockSpec((tm, shp), lambda i: (i, 0)))
        else:
            out_shape.append(jax.ShapeDtypeStruct(shp, dt))
            out_specs.append(pl.BlockSpec(shp, lambda i, _n=len(shp): (0,) * _n))
    n_in = len(arrs)
    in_specs += [pl.BlockSpec(memory_space=pl.ANY)] * len(afters)

    def body(*refs):
        vals = fn(*[r[...] for r in refs[:n_in]])
        first = pl.program_id(0) == 0
        for (kind, _, _), r, v in zip(outs, refs[n_in + len(afters):], vals):
            if kind == 'rows':
                r[...] = v.astype(r.dtype)
            else:
                @pl.when(first)
                def _(r=r, v=v):
                    r[...] = v.astype(r.dtype)

                @pl.when(jnp.logical_not(first))
                def _(r=r, v=v):
                    r[...] += v.astype(r.dtype)

    has_acc = any(k != 'rows' for k, _, _ in outs)
    res = pl.pallas_call(
        body, name=name, grid=(S // tm,), in_specs=in_specs, out_specs=out_specs, out_shape=out_shape,
        compiler_params=_cparams("arbitrary" if has_acc else "parallel"),
    )(*arrs, *afters)
    return res


def _mm_col(name, a, wg, row_start, bias=None, out_dtype=BF16):
    S, K = a.shape
    _, _, Ns = wg.shape
    rb = row_start // K
    tm = min(MM_ROWS, S)

    def body(a_ref, w_ref, *rest):
        acc = jnp.dot(a_ref[...].astype(BF16), w_ref[...], preferred_element_type=F32)
        if bias is not None:
            acc = acc + rest[0][...]
        rest[-1][...] = acc.astype(out_dtype)

    in_specs = [pl.BlockSpec((tm, K), lambda j, m: (m, 0)), pl.BlockSpec((None, K, Ns), lambda j, m: (j, rb, 0))]
    args = [a, wg]
    if bias is not None:
        in_specs.append(pl.BlockSpec((1, Ns), lambda j, m: (0, j)))
        args.append(bias)
    return pl.pallas_call(
        body, name=name, grid=(N_CHIPS, S // tm), in_specs=in_specs,
        out_specs=pl.BlockSpec((tm, Ns), lambda j, m: (m, j)),
        out_shape=jax.ShapeDtypeStruct((S, N_CHIPS * Ns), out_dtype),
        compiler_params=_cparams("parallel", "parallel"),
    )(*args)


def _mm_row(name, a, wg, row_start, Ks, bias=None):
    S = a.shape[0]
    N = wg.shape[2]
    rb = row_start // Ks
    tm = min(MM_ROWS, S)

    def body(a_ref, w_ref, *rest):
        o_ref = rest[-1]
        j = pl.program_id(1)
        d = jnp.dot(a_ref[...].astype(BF16), w_ref[...], preferred_element_type=F32)

        @pl.when(j == 0)
        def _():
            o_ref[...] = d + rest[0][...] if bias is not None else d

        @pl.when(j > 0)
        def _():
            o_ref[...] += d

    in_specs = [pl.BlockSpec((tm, Ks), lambda m, j: (m, j)), pl.BlockSpec((None, Ks, N), lambda m, j: (j, rb, 0))]
    args = [a, wg]
    if bias is not None:
        in_specs.append(pl.BlockSpec((1, N), lambda m, j: (0, 0)))
        args.append(bias)
    return pl.pallas_call(
        body, name=name, grid=(S // tm, N_CHIPS), in_specs=in_specs,
        out_specs=pl.BlockSpec((tm, N), lambda m, j: (m, 0)),
        out_shape=jax.ShapeDtypeStruct((S, N), F32),
        compiler_params=_cparams("parallel", "arbitrary"),
    )(*args)


def _mm_col_t(name, pairs, wg, K):
    S = pairs[0][0].shape[0]
    Ns = wg.shape[2]
    tm = min(MM_ROWS, S)
    n = len(pairs)
    tkk = K // n if (K // n) % LANES == 0 else K

    def body(*refs):
        o_ref = refs[-1]
        j = pl.program_id(2)
        d = _dot_nt(refs[0][...], refs[n][...])
        for p in range(1, n):
            d = d + _dot_nt(refs[p][...], refs[n + p][...])

        @pl.when(j == 0)
        def _():
            o_ref[...] = d

        @pl.when(j > 0)
        def _():
            o_ref[...] += d

    in_specs = [pl.BlockSpec((tm, Ns), lambda m, kb, j: (m, j)) for _ in pairs]
    in_specs += [pl.BlockSpec((None, tkk, Ns), lambda m, kb, j, _rb=rs // tkk: (j, _rb + kb, 0)) for _, rs in pairs]
    return pl.pallas_call(
        body, name=name, grid=(S // tm, K // tkk, N_CHIPS), in_specs=in_specs,
        out_specs=pl.BlockSpec((tm, tkk), lambda m, kb, j: (m, kb)),
        out_shape=jax.ShapeDtypeStruct((S, K), F32),
        compiler_params=_cparams("parallel", "parallel", "arbitrary"),
    )(*[dy for dy, _ in pairs], *[wg for _ in pairs])


def _mm_row_t(name, dy, wg, row_start, Ks, out_dtype):
    S, N = dy.shape
    rb = row_start // Ks
    tm = min(MM_ROWS, S)

    def body(dy_ref, w_ref, o_ref):
        o_ref[...] = _dot_nt(dy_ref[...], w_ref[...]).astype(out_dtype)

    return pl.pallas_call(
        body, name=name, grid=(N_CHIPS, S // tm),
        in_specs=[pl.BlockSpec((tm, N), lambda j, m: (m, 0)), pl.BlockSpec((None, Ks, N), lambda j, m: (j, rb, 0))],
        out_specs=pl.BlockSpec((tm, Ks), lambda j, m: (m, j)),
        out_shape=jax.ShapeDtypeStruct((S, N_CHIPS * Ks), out_dtype),
        compiler_params=_cparams("parallel", "parallel"),
    )(dy, wg)


def _silu_parts(g):
    sg = _sigmoid(g)
    return g * sg, sg * (1.0 + g * (1.0 - sg))


def _mm_gate_up(name, a, wg, gate_row, up_row):
    S, K = a.shape
    Ns = wg.shape[2]
    tm = min(MM_ROWS // 2, S)

    def body(a_ref, wg_ref, wu_ref, hg_ref, hu_ref, act_ref):
        a_ = a_ref[...]
        hg = jnp.dot(a_, wg_ref[...], preferred_element_type=F32).astype(BF16)
        hu = jnp.dot(a_, wu_ref[...], preferred_element_type=F32).astype(BF16)
        hg_ref[...] = hg
        hu_ref[...] = hu
        act_ref[...] = (_silu_parts(hg.astype(F32))[0] * hu.astype(F32)).astype(BF16)

    out = jax.ShapeDtypeStruct((S, N_CHIPS * Ns), BF16)
    w_spec = lambda rb: pl.BlockSpec((None, K, Ns), lambda j, m: (j, rb, 0))
    o_spec = pl.BlockSpec((tm, Ns), lambda j, m: (m, j))
    return pl.pallas_call(
        body, name=name, grid=(N_CHIPS, S // tm),
        in_specs=[pl.BlockSpec((tm, K), lambda j, m: (m, 0)), w_spec(gate_row // K), w_spec(up_row // K)],
        out_specs=[o_spec, o_spec, o_spec], out_shape=[out, out, out],
        compiler_params=_cparams("parallel", "parallel"),
    )(a, wg, wg)


def _mm_dact(name, dy, wg, row_start, Ks, hg, hu):
    S, N = dy.shape
    rb = row_start // Ks
    tm = min(512, S)

    def body(dy_ref, w_ref, hg_ref, hu_ref, dhg_ref, dhu_ref):
        dact = _dot_nt(dy_ref[...], w_ref[...])
        silu, dsilu = _silu_parts(hg_ref[...].astype(F32))
        dhg_ref[...] = (dact * hu_ref[...].astype(F32) * dsilu).astype(BF16)
        dhu_ref[...] = (dact * silu).astype(BF16)

    out = jax.ShapeDtypeStruct((S, N_CHIPS * Ks), BF16)
    t_spec = pl.BlockSpec((tm, Ks), lambda j, m: (m, j))
    return pl.pallas_call(
        body, name=name, grid=(N_CHIPS, S // tm),
        in_specs=[pl.BlockSpec((tm, N), lambda j, m: (m, 0)), pl.BlockSpec((None, Ks, N), lambda j, m: (j, rb, 0)), t_spec, t_spec],
        out_specs=[t_spec, t_spec], out_shape=[out, out],
        compiler_params=_cparams("parallel", "parallel"),
    )(dy, wg, hg, hu)


def _mm_dw(name, a, dy, dg, row_start, kind):
    S = a.shape[0]
    _, _, W = dg.shape
    if kind == 'col':
        K = a.shape[1]
        rows = K
        tk, tn = _pick(K, MM_ROWS), W
        a_map = lambda j, nb, kb: (0, kb)
        dy_map = lambda j, nb, kb: (0, j * (W // tn) + nb)
    else:
        rows = a.shape[1] // N_CHIPS
        tk = rows if rows * S * 2 * 2 <= 12 * 1024 * 1024 else _pick(rows, 512)
        tn = _pick(W, 1024)
        a_map = lambda j, nb, kb: (0, j * (rows // tk) + kb)
        dy_map = lambda j, nb, kb: (0, nb)
    rb = row_start // tk
    assert row_start % tk == 0

    def body(a_ref, dy_ref, dg_in, o_ref):
        del dg_in
        o_ref[...] = _dot_tn(a_ref[...], dy_ref[...]).astype(o_ref.dtype)

    return pl.pallas_call(
        body, name=name, grid=(N_CHIPS, W // tn, rows // tk),
        in_specs=[pl.BlockSpec((S, tk), a_map), pl.BlockSpec((S, tn), dy_map), pl.BlockSpec(memory_space=pl.ANY)],
        out_specs=pl.BlockSpec((None, tk, tn), lambda j, nb, kb: (j, rb + kb, nb)),
        out_shape=jax.ShapeDtypeStruct(dg.shape, dg.dtype),
        input_output_aliases={2: 0},
        compiler_params=_cparams("parallel", "parallel", "parallel"),
    )(a, dy, dg)


def _fox_probs(q, k, c_blk, crow, h, qi, tq):
    n = k.shape[0]
    s = _dot_nt(q, k) * (HEAD_DIM ** -0.5)
    lane = lax.broadcasted_iota(jnp.int32, c_blk.shape, 1)
    ccol = jnp.sum(jnp.where(lane == h, c_blk, 0.0), axis=1, keepdims=True)
    s = s + (ccol - crow)
    t_idx = qi * tq + lax.broadcasted_iota(jnp.int32, (tq, n), 0)
    s_idx = lax.broadcasted_iota(jnp.int32, (tq, n), 1)
    s = jnp.where(s_idx <= t_idx, s, NEG_BIG)
    p = jnp.exp(s - jnp.max(s, axis=1, keepdims=True))
    return p * (1.0 / jnp.sum(p, axis=1, keepdims=True))


def _per_query_block(qi, nq, tq, fn):
    for qv in range(nq):
        @pl.when(qi == qv)
        def _(qv=qv):
            fn(qv, (qv + 1) * tq)


FOX_HEADS = 2


def _head_cols(a):
    return slice(a * HEAD_DIM, (a + 1) * HEAD_DIM)


def _fox_fwd(qkv, c, crow, H):
    S = qkv.shape[0]
    tq = min(256, S)
    heads = 2 * FOX_HEADS
    hw = heads * HEAD_DIM
    G = H // heads

    def body(q_ref, k_ref, v_ref, c_ref, crow_ref, o_ref):
        def block(qv, n):
            for a in range(heads):
                cols = _head_cols(a)
                p = _fox_probs(q_ref[:, cols], k_ref[0:n, cols], c_ref[...], crow_ref[a, :, 0:n],
                               heads * pl.program_id(0) + a, qv, tq)
                o_ref[:, cols] = jnp.dot(p.astype(BF16), v_ref[0:n, cols], preferred_element_type=F32).astype(o_ref.dtype)

        _per_query_block(pl.program_id(1), S // tq, tq, block)

    return pl.pallas_call(
        body, name="fox_attn_fwd", grid=(G, S // tq),
        in_specs=[pl.BlockSpec((tq, hw), lambda g, i: (i, g)),
                  pl.BlockSpec((S, hw), lambda g, i: (0, G + g)),
                  pl.BlockSpec((S, hw), lambda g, i: (0, 2 * G + g)),
                  pl.BlockSpec((tq, H), lambda g, i: (i, 0)),
                  pl.BlockSpec((heads, 1, S), lambda g, i: (g, 0, 0))],
        out_specs=pl.BlockSpec((tq, hw), lambda g, i: (i, g)),
        out_shape=jax.ShapeDtypeStruct((S, H * HEAD_DIM), BF16),
        compiler_params=_cparams("parallel", "parallel"),
    )(qkv, qkv, qkv, c, crow)


def _fox_bwd(qkv, c, crow, do, H):
    S = qkv.shape[0]
    tq = min(256, S)
    nq = S // tq
    hw = FOX_HEADS * HEAD_DIM
    G = H // FOX_HEADS

    def body(q_ref, k_ref, v_ref, c_ref, crow_ref, do_ref, dq_ref, dk_ref, dv_ref, dc_ref, dk_acc, dv_acc):
        qi = pl.program_id(1)

        @pl.when(qi == 0)
        def _():
            dk_acc[...] = jnp.zeros_like(dk_acc)
            dv_acc[...] = jnp.zeros_like(dv_acc)
            dc_ref[...] = jnp.zeros_like(dc_ref)

        def block(qv, n):
            for a in range(FOX_HEADS):
                cols = _head_cols(a)
                q, k, v, do_ = q_ref[:, cols], k_ref[0:n, cols], v_ref[0:n, cols], do_ref[:, cols]
                p = _fox_probs(q, k, c_ref[...], crow_ref[a, :, 0:n], FOX_HEADS * pl.program_id(0) + a, qv, tq)
                dv_acc[0:n, cols] += _dot_tn(p.astype(BF16), do_)
                dp = _dot_nt(do_, v)
                ds = p * (dp - jnp.sum(p * dp, axis=1, keepdims=True))
                dsb = (ds * (HEAD_DIM ** -0.5)).astype(BF16)
                dq_ref[:, cols] = jnp.dot(dsb, k, preferred_element_type=F32).astype(dq_ref.dtype)
                dk_acc[0:n, cols] += _dot_tn(dsb, q)
                dc_ref[a, :, 0:n] += -jnp.sum(ds, axis=0, keepdims=True)

        _per_query_block(qi, nq, tq, block)

        @pl.when(qi == nq - 1)
        def _():
            dk_ref[...] = dk_acc[...].astype(dk_ref.dtype)
            dv_ref[...] = dv_acc[...].astype(dv_ref.dtype)

    D = H * HEAD_DIM
    return pl.pallas_call(
        body, name="fox_attn_bwd", grid=(G, nq),
        in_specs=[pl.BlockSpec((tq, hw), lambda g, i: (i, g)),
                  pl.BlockSpec((S, hw), lambda g, i: (0, G + g)),
                  pl.BlockSpec((S, hw), lambda g, i: (0, 2 * G + g)),
                  pl.BlockSpec((tq, H), lambda g, i: (i, 0)),
                  pl.BlockSpec((FOX_HEADS, 1, S), lambda g, i: (g, 0, 0)),
                  pl.BlockSpec((tq, hw), lambda g, i: (i, g))],
        out_specs=[pl.BlockSpec((tq, hw), lambda g, i: (i, g)),
                   pl.BlockSpec((S, hw), lambda g, i: (0, g)),
                   pl.BlockSpec((S, hw), lambda g, i: (0, g)),
                   pl.BlockSpec((FOX_HEADS, 1, S), lambda g, i: (g, 0, 0))],
        out_shape=[jax.ShapeDtypeStruct((S, D), BF16), jax.ShapeDtypeStruct((S, D), BF16),
                   jax.ShapeDtypeStruct((S, D), BF16), jax.ShapeDtypeStruct((H, 1, S), F32)],
        scratch_shapes=[pltpu.VMEM((S, hw), F32), pltpu.VMEM((S, hw), F32)],
        compiler_params=_cparams("parallel", "arbitrary"),
    )(qkv, qkv, qkv, c, crow, do)


def _cumsum_rows(name, xin, reverse):
    S, H = xin.shape
    tb = min(256, S)
    nb = S // tb

    def body(x_ref, o_ref):
        r = lax.broadcasted_iota(jnp.int32, (tb, tb), 0)
        cidx = lax.broadcasted_iota(jnp.int32, (tb, tb), 1)
        tri = (r <= cidx if reverse else r >= cidx).astype(F32)

        def step(b, carry):
            bb = nb - 1 - b if reverse else b
            rows = pl.ds(pl.multiple_of(bb * tb, tb), tb)
            blk = x_ref[rows, :]
            o_ref[rows, :] = jnp.dot(tri, blk, precision=HIGHEST, preferred_element_type=F32) + carry
            return carry + jnp.sum(blk, axis=0, keepdims=True)

        lax.fori_loop(0, nb, step, jnp.zeros((1, H), F32))

    return pl.pallas_call(
        body, name=name, out_shape=jax.ShapeDtypeStruct((S, H), F32),
        in_specs=[pl.BlockSpec(memory_space=pltpu.VMEM)], out_specs=pl.BlockSpec(memory_space=pltpu.VMEM),
        compiler_params=_cparams(),
    )(xin)


def _rel_onehot(i, transposed):
    shp = (REL_TABLE_PAD, BAND_KEYS) if transposed else (BAND_KEYS, REL_TABLE_PAD)
    j = lax.broadcasted_iota(jnp.int32, shp, 1 if transposed else 0)
    r = lax.broadcasted_iota(jnp.int32, shp, 0 if transposed else 1)
    return (jnp.clip(PAD_KEYS + i - j, -REL_CLIP, REL_CLIP) + REL_CLIP == r).astype(F32)


def _rel_expand(rb_pad):
    H = rb_pad.shape[0]

    def body(rb_ref, o_ref):
        def step(i, _):
            o_ref[i] = jnp.dot(rb_ref[...], _rel_onehot(i, True), precision=HIGHEST, preferred_element_type=F32)
            return 0
        lax.fori_loop(0, CHUNK, step, 0)

    return pl.pallas_call(
        body, name="rel_bias_expand", out_shape=jax.ShapeDtypeStruct((CHUNK, H, BAND_KEYS), F32),
        in_specs=[pl.BlockSpec(memory_space=pltpu.VMEM)], out_specs=pl.BlockSpec(memory_space=pltpu.VMEM),
        compiler_params=_cparams(),
    )(rb_pad)


def _rel_reduce(dbt):
    H = dbt.shape[1]

    near0 = PAD_KEYS - REL_CLIP
    assert near0 % LANES == 0 and REL_WIN - near0 == REL_TABLE_PAD

    def body(d_ref, o_ref):
        j = near0 + lax.broadcasted_iota(jnp.int32, (REL_TABLE_PAD, REL_TABLE_PAD), 0)
        r = lax.broadcasted_iota(jnp.int32, (REL_TABLE_PAD, REL_TABLE_PAD), 1)

        def step(i, carry):
            acc, far = carry
            onehot = (jnp.clip(PAD_KEYS + i - j, -REL_CLIP, REL_CLIP) + REL_CLIP == r).astype(BF16)
            d = d_ref[i]
            near = d[:, near0:]
            hi = near.astype(BF16)
            lo = (near - hi.astype(F32)).astype(BF16)
            acc = acc + (jnp.dot(hi, onehot, preferred_element_type=F32) + jnp.dot(lo, onehot, preferred_element_type=F32))
            return acc, far + jnp.sum(d[:, :near0], axis=1, keepdims=True)

        acc, far = lax.fori_loop(0, REL_QB, step, (jnp.zeros((H, REL_TABLE_PAD), F32), jnp.zeros((H, 1), F32)))
        col = lax.broadcasted_iota(jnp.int32, (H, REL_TABLE_PAD), 1)
        o_ref[...] = acc + jnp.where(col == REL_TABLE - 1, far, 0.0)

    return pl.pallas_call(
        body, name="rel_bias_reduce", out_shape=jax.ShapeDtypeStruct((H, REL_TABLE_PAD), F32),
        in_specs=[pl.BlockSpec(memory_space=pltpu.VMEM)], out_specs=pl.BlockSpec(memory_space=pltpu.VMEM),
        compiler_params=_cparams(),
    )(dbt)


def _rel_window_bias(bias):
    H = bias.shape[0]
    out = jnp.full((H, REL_QB, REL_WIN), NEG_BIG, F32)
    for a in range(REL_QB // CHUNK):
        out = out.at[:, a * CHUNK:(a + 1) * CHUNK, a * CHUNK:a * CHUNK + BAND_KEYS].set(bias)
    return out


def _rel_probs(q, kw, bias_w, t0):
    s = _dot_nt(q, kw) * (HEAD_DIM ** -0.5) + bias_w
    j = lax.broadcasted_iota(jnp.int32, (REL_QB, REL_WIN), 1)
    s = jnp.where(j >= PAD_KEYS - t0, s, NEG_BIG)
    p = jnp.exp(s - jnp.max(s, axis=1, keepdims=True))
    return p * (1.0 / jnp.sum(p, axis=1, keepdims=True))


def _rel_fwd(qkv, bias_w, H):
    S = qkv.shape[0]

    def body(q_ref, k_ref, v_ref, b_ref, o_ref, kpad, vpad):
        kpad[0:PAD_KEYS, :] = jnp.zeros((PAD_KEYS, HEAD_DIM), BF16)
        vpad[0:PAD_KEYS, :] = jnp.zeros((PAD_KEYS, HEAD_DIM), BF16)
        kpad[PAD_KEYS:PAD_KEYS + S, :] = k_ref[...]
        vpad[PAD_KEYS:PAD_KEYS + S, :] = v_ref[...]

        def block(n, _):
            t0 = pl.multiple_of(n * REL_QB, REL_QB)
            rows, win = pl.ds(t0, REL_QB), pl.ds(t0, REL_WIN)
            p = _rel_probs(q_ref[rows, :], kpad[win, :], b_ref[...], t0)
            o_ref[rows, :] = jnp.dot(p.astype(BF16), vpad[win, :], preferred_element_type=F32).astype(o_ref.dtype)
            return 0

        lax.fori_loop(0, S // REL_QB, block, 0, unroll=2)

    return pl.pallas_call(
        body, name="rel_attn_fwd", grid=(H,),
        in_specs=[pl.BlockSpec((S, HEAD_DIM), lambda h: (0, h)),
                  pl.BlockSpec((S, HEAD_DIM), lambda h: (0, H + h)),
                  pl.BlockSpec((S, HEAD_DIM), lambda h: (0, 2 * H + h)),
                  pl.BlockSpec((None, REL_QB, REL_WIN), lambda h: (h, 0, 0))],
        out_specs=pl.BlockSpec((S, HEAD_DIM), lambda h: (0, h)),
        out_shape=jax.ShapeDtypeStruct((S, H * HEAD_DIM), BF16),
        scratch_shapes=[pltpu.VMEM((S + PAD_KEYS, HEAD_DIM), BF16), pltpu.VMEM((S + PAD_KEYS, HEAD_DIM), BF16)],
        compiler_params=_cparams("parallel"),
    )(qkv, qkv, qkv, bias_w)


def _rel_bwd(qkv, bias_w, do, H):
    S = qkv.shape[0]
    D = H * HEAD_DIM

    def body(q_ref, k_ref, v_ref, b_ref, do_ref, dq_ref, dk_ref, dv_ref, db_ref, kpad, vpad, dkpad, dvpad):
        kpad[0:PAD_KEYS, :] = jnp.zeros((PAD_KEYS, HEAD_DIM), BF16)
        vpad[0:PAD_KEYS, :] = jnp.zeros((PAD_KEYS, HEAD_DIM), BF16)
        kpad[PAD_KEYS:PAD_KEYS + S, :] = k_ref[...]
        vpad[PAD_KEYS:PAD_KEYS + S, :] = v_ref[...]
        dkpad[...] = jnp.zeros_like(dkpad)
        dvpad[...] = jnp.zeros_like(dvpad)
        db_ref[...] = jnp.zeros_like(db_ref)

        def block(n, _):
            t0 = pl.multiple_of(n * REL_QB, REL_QB)
            rows, win = pl.ds(t0, REL_QB), pl.ds(t0, REL_WIN)
            q, kw, vw, do_ = q_ref[rows, :], kpad[win, :], vpad[win, :], do_ref[rows, :]
            p = _rel_probs(q, kw, b_ref[...], t0)
            dvpad[win, :] += _dot_tn(p.astype(BF16), do_)
            dp = _dot_nt(do_, vw)
            ds = p * (dp - jnp.sum(p * dp, axis=1, keepdims=True))
            db_ref[...] += ds
            dsb = (ds * (HEAD_DIM ** -0.5)).astype(BF16)
            dq_ref[rows, :] = jnp.dot(dsb, kw, preferred_element_type=F32).astype(dq_ref.dtype)
            dkpad[win, :] += _dot_tn(dsb, q)
            return 0

        lax.fori_loop(0, S // REL_QB, block, 0, unroll=2)
        dk_ref[...] = dkpad[PAD_KEYS:PAD_KEYS + S, :].astype(dk_ref.dtype)
        dv_ref[...] = dvpad[PAD_KEYS:PAD_KEYS + S, :].astype(dv_ref.dtype)

    head = lambda h: (0, h)
    return pl.pallas_call(
        body, name="rel_attn_bwd", grid=(H,),
        in_specs=[pl.BlockSpec((S, HEAD_DIM), head),
                  pl.BlockSpec((S, HEAD_DIM), lambda h: (0, H + h)),
                  pl.BlockSpec((S, HEAD_DIM), lambda h: (0, 2 * H + h)),
                  pl.BlockSpec((None, REL_QB, REL_WIN), lambda h: (h, 0, 0)),
                  pl.BlockSpec((S, HEAD_DIM), head)],
        out_specs=[pl.BlockSpec((S, HEAD_DIM), head), pl.BlockSpec((S, HEAD_DIM), head), pl.BlockSpec((S, HEAD_DIM), head),
                   pl.BlockSpec((None, REL_QB, REL_WIN), lambda h: (h, 0, 0))],
        out_shape=[jax.ShapeDtypeStruct((S, D), BF16), jax.ShapeDtypeStruct((S, D), BF16), jax.ShapeDtypeStruct((S, D), BF16),
                   jax.ShapeDtypeStruct((H, REL_QB, REL_WIN), F32)],
        scratch_shapes=[pltpu.VMEM((S + PAD_KEYS, HEAD_DIM), BF16), pltpu.VMEM((S + PAD_KEYS, HEAD_DIM), BF16),
                        pltpu.VMEM((S + PAD_KEYS, HEAD_DIM), F32), pltpu.VMEM((S + PAD_KEYS, HEAD_DIM), F32)],
        compiler_params=_cparams("parallel"),
    )(qkv, qkv, qkv, bias_w, do)


def _conv_taps(win, tt, reverse):
    n = tt + 2 * CONV_HALO
    for k in range(CONV_K):
        off = (CONV_K - 1 - k) if reverse else (k - (CONV_K - 1))
        sh = (-off) % n
        rolled = pltpu.roll(win, sh, 0) if sh else win
        yield k, rolled[CONV_HALO:CONV_HALO + tt, :]


def _fill_padded(pad_ref, x_ref, S):
    tc = pad_ref.shape[1]
    pad_ref[0:CONV_HALO, :] = jnp.zeros((CONV_HALO, tc), F32)
    pad_ref[CONV_HALO + S:CONV_HALO + S + CONV_HALO, :] = jnp.zeros((CONV_HALO, tc), F32)
    pad_ref[CONV_HALO:CONV_HALO + S, :] = x_ref[...]


def _dwconv(name, xin, w32, bias, reverse):
    S, D = xin.shape
    tc = min(256, D)
    tt = min(256, S)

    def body(x_ref, w_ref, b_ref, y_ref, pad_ref):
        _fill_padded(pad_ref, x_ref, S)
        def tile(ti, _):
            t0 = pl.multiple_of(ti * tt, tt)
            win = pad_ref[pl.ds(t0, tt + 2 * CONV_HALO), :]
            acc = jnp.zeros((tt, tc), F32) + b_ref[...]
            for k, shifted in _conv_taps(win, tt, reverse):
                acc = acc + w_ref[pl.ds(k, 1), :] * shifted
            y_ref[pl.ds(t0, tt), :] = acc
            return 0

        lax.fori_loop(0, S // tt, tile, 0)

    return pl.pallas_call(
        body, name=name, grid=(D // tc,),
        in_specs=[pl.BlockSpec((S, tc), lambda i: (0, i)), pl.BlockSpec((CONV_HALO, tc), lambda i: (0, i)),
                  pl.BlockSpec((1, tc), lambda i: (0, i))],
        out_specs=pl.BlockSpec((S, tc), lambda i: (0, i)),
        out_shape=jax.ShapeDtypeStruct((S, D), F32),
        scratch_shapes=[pltpu.VMEM((S + 2 * CONV_HALO, tc), F32)],
        compiler_params=_cparams("parallel"),
    )(xin, w32, bias)


def _dwconv_dw(xin, dy):
    S, D = xin.shape
    tc = min(256, D)
    tt = min(256, S)

    def body(x_ref, dy_ref, o_ref, pad_ref):
        _fill_padded(pad_ref, x_ref, S)

        def tile(ti, acc):
            t0 = pl.multiple_of(ti * tt, tt)
            win = pad_ref[pl.ds(t0, tt + 2 * CONV_HALO), :]
            dyt = dy_ref[pl.ds(t0, tt), :]
            ridx = lax.broadcasted_iota(jnp.int32, (CONV_HALO, tc), 0)
            upd = jnp.zeros((CONV_HALO, tc), F32)
            for k, shifted in _conv_taps(win, tt, False):
                upd = jnp.where(ridx == k, jnp.sum(dyt * shifted, axis=0, keepdims=True), upd)
            return acc + upd

        o_ref[...] = lax.fori_loop(0, S // tt, tile, jnp.zeros((CONV_HALO, tc), F32))

    return pl.pallas_call(
        body, name="dwconv_dw", grid=(D // tc,),
        in_specs=[pl.BlockSpec((S, tc), lambda i: (0, i)), pl.BlockSpec((S, tc), lambda i: (0, i))],
        out_specs=pl.BlockSpec((CONV_HALO, tc), lambda i: (0, i)),
        out_shape=jax.ShapeDtypeStruct((CONV_HALO, D), F32),
        scratch_shapes=[pltpu.VMEM((S + 2 * CONV_HALO, tc), F32)],
        compiler_params=_cparams("parallel"),
    )(xin, dy)


def _place():
    x, y, c = lax.axis_index("x"), lax.axis_index("y"), lax.axis_index("c")
    chips = [(1 - x, y), (x, 1 - y), (1 - x, 1 - y)]
    return x, y, c, chips


def _remote(src, dst, ssem, rsem, dev):
    return pltpu.make_async_remote_copy(src_ref=src, dst_ref=dst, send_sem=ssem, recv_sem=rsem,
                                        device_id=dev, device_id_type=MESH_IDS)


_ANY = pl.BlockSpec(memory_space=pl.ANY)


def _place_own(name, own, place):
    R, W = own.shape
    tr = _pick_rows(R)

    def body(p_ref, a_ref, o_ref):
        del p_ref
        o_ref[...] = a_ref[...]

    return pl.pallas_call(
        body, name=name,
        grid_spec=pltpu.PrefetchScalarGridSpec(
            num_scalar_prefetch=1, grid=(R // tr,),
            in_specs=[pl.BlockSpec((tr, W), lambda i, p: (i, 0))],
            out_specs=pl.BlockSpec((None, tr, W), lambda i, p: (p[1], i, 0))),
        out_shape=jax.ShapeDtypeStruct((N_CHIPS, R, W), own.dtype),
        compiler_params=_cparams("parallel"),
    )(place, own)


_HBM = pl.BlockSpec(memory_space=pltpu.HBM)
_SEM = pl.BlockSpec(memory_space=pltpu.SEMAPHORE)
GROUPS = ('qkv', 'ffn', 'pw1', 'dm')


def _half_rows(c, r0, n):
    return pl.ds(pl.multiple_of(r0 + c * (n // 2), SUBLANES), n // 2)


def _gather_start(name, wgs, layers, after):
    G, L = len(wgs), len(layers)

    def body(*refs):
        outs = refs[G + 1:]
        ssems, rsems, bufs = outs[:L], outs[L:2 * L], outs[2 * L:]
        x, y, c, chips = _place()
        me = 2 * x + y
        for li, pieces in enumerate(layers):
            for pi, (g, r0, n) in enumerate(pieces):
                blk = bufs[g].at[me, _half_rows(c, r0, n)]
                for j, (px, py) in enumerate(chips):
                    _remote(blk, blk, ssems[li].at[3 * pi + j], rsems[li].at[3 * pi + j], (px, py, c)).start()

    sem_shapes = [pltpu.SemaphoreType.DMA((3 * len(p),)) for p in layers]
    res = pl.pallas_call(
        body, name=name, in_specs=[_HBM] * G + [_ANY],
        out_specs=[_SEM] * (2 * L) + [_HBM] * G,
        out_shape=sem_shapes + sem_shapes + [pltpu.HBM(w.shape, w.dtype) for w in wgs],
        input_output_aliases={g: 2 * L + g for g in range(G)},
        compiler_params=pltpu.CompilerParams(has_side_effects=pltpu.SideEffectType.DATAFLOW_SIDE_EFFECTING),
    )(*[pltpu.with_memory_space_constraint(w, pltpu.HBM) for w in wgs], after)
    return res[:L], res[L:2 * L], list(res[2 * L:])


def _gather_wait(name, wgs, ssem, rsem, pieces, after):
    G = len(wgs)

    def body(*refs):
        ssem_ref, rsem_ref = refs[G], refs[G + 1]
        bufs = refs[G + 3:]
        x, y, c, chips = _place()
        me = 2 * x + y
        for pi, (g, r0, n) in enumerate(pieces):
            rows = _half_rows(c, r0, n)
            for j, (px, py) in enumerate(chips):
                cp = _remote(bufs[g].at[me, rows], bufs[g].at[2 * px + py, rows],
                             ssem_ref.at[3 * pi + j], rsem_ref.at[3 * pi + j], (px, py, c))
                cp.wait_send()
                cp.wait_recv()

    return list(pl.pallas_call(
        body, name=name, in_specs=[_HBM] * G + [_SEM, _SEM, _ANY], out_specs=[_HBM] * G,
        out_shape=[pltpu.HBM(w.shape, w.dtype) for w in wgs],
        input_output_aliases={g: g for g in range(G)},
        compiler_params=pltpu.CompilerParams(has_side_effects=pltpu.SideEffectType.DATAFLOW_SIDE_EFFECTING),
    )(*wgs, ssem, rsem, after))


def _gather_forward(name, wgs, pieces):
    G = len(wgs)
    n_cp = 3 * len(pieces)

    def body(*refs):
        bufs, ssems, rsems = refs[G:2 * G], refs[2 * G], refs[2 * G + 1]
        x, y, c, chips = _place()
        sib = (x, y, 1 - c)
        cps = []
        for pi, (g, r0, n) in enumerate(pieces):
            for j, (px, py) in enumerate(chips):
                blk = bufs[g].at[2 * px + py, _half_rows(c, r0, n)]
                cps.append(_remote(blk, blk, ssems.at[3 * pi + j], rsems.at[3 * pi + j], sib))
        for cp in cps:
            cp.start()
        for pi, (g, r0, n) in enumerate(pieces):
            for j, (px, py) in enumerate(chips):
                blk = bufs[g].at[2 * px + py, _half_rows(1 - c, r0, n)]
                _remote(blk, blk, ssems.at[3 * pi + j], rsems.at[3 * pi + j], sib).wait_recv()
        for cp in cps:
            cp.wait_send()

    return list(pl.pallas_call(
        body, name=name, in_specs=[_ANY] * G, out_specs=[_ANY] * G,
        out_shape=[jax.ShapeDtypeStruct(w.shape, w.dtype) for w in wgs],
        input_output_aliases={g: g for g in range(G)},
        scratch_shapes=[pltpu.SemaphoreType.DMA((n_cp,)), pltpu.SemaphoreType.DMA((n_cp,))],
        compiler_params=pltpu.CompilerParams(has_side_effects=True),
    )(*wgs))


def _swap_halves(name, dgs, pieces):
    G = len(dgs)
    n_cp = N_CHIPS * len(pieces)

    def body(*refs):
        srcs, lands, ssems, rsems = refs[:G], refs[G:G + len(pieces)], refs[-2], refs[-1]
        x, y, c, _ = _place()
        cps = [_remote(srcs[g].at[j, _half_rows(1 - c, r0, n)], lands[pi].at[j],
                       ssems.at[N_CHIPS * pi + j], rsems.at[N_CHIPS * pi + j], (x, y, 1 - c))
               for pi, (g, r0, n) in enumerate(pieces) for j in range(N_CHIPS)]
        for cp in cps:
            cp.start()
        for cp in cps:
            cp.wait()

    return list(pl.pallas_call(
        body, name=name, in_specs=[_ANY] * G, out_specs=[_ANY] * len(pieces),
        out_shape=[jax.ShapeDtypeStruct((N_CHIPS, n // 2, dgs[g].shape[2]), dgs[g].dtype) for g, _, n in pieces],
        scratch_shapes=[pltpu.SemaphoreType.DMA((n_cp,)), pltpu.SemaphoreType.DMA((n_cp,))],
        compiler_params=pltpu.CompilerParams(has_side_effects=True),
    )(*dgs))


def _scatter_start(name, pbs):
    P = len(pbs)

    def body(*refs):
        outs = refs[2 * P:]
        ssems, rsems, src, land = outs[0], outs[1], outs[2:2 + P], outs[2 + P:]
        x, y, c, chips = _place()
        me = 2 * x + y
        for pi in range(P):
            for j, (px, py) in enumerate(chips):
                _remote(src[pi].at[2 * px + py], land[pi].at[me], ssems.at[3 * pi + j], rsems.at[3 * pi + j], (px, py, c)).start()

    sems = pltpu.SemaphoreType.DMA((3 * P,))
    hbm = [pltpu.HBM(p.shape, p.dtype) for p in pbs]
    res = pl.pallas_call(
        body, name=name, in_specs=[_HBM] * (2 * P), out_specs=[_SEM, _SEM] + [_HBM] * (2 * P),
        out_shape=[sems, sems] + hbm + hbm,
        input_output_aliases={k: 2 + k for k in range(2 * P)},
        compiler_params=pltpu.CompilerParams(has_side_effects=pltpu.SideEffectType.DATAFLOW_SIDE_EFFECTING),
    )(*[pltpu.with_memory_space_constraint(p, pltpu.HBM) for p in pbs],
      *[pltpu.with_memory_space_constraint(lax.empty(p.shape, p.dtype), pltpu.HBM) for p in pbs])
    return res[0], res[1], list(res[2:2 + P]), list(res[2 + P:])


def _scatter_wait(name, ssem, rsem, pbs, lands, after):
    P = len(pbs)

    def body(*refs):
        ssems, rsems = refs[2 * P], refs[2 * P + 1]
        outs = refs[2 * P + 3:]
        src, land = outs[:P], outs[P:]
        x, y, c, chips = _place()
        for pi in range(P):
            for j, (px, py) in enumerate(chips):
                cp = _remote(src[pi].at[2 * px + py], land[pi].at[2 * px + py], ssems.at[3 * pi + j], rsems.at[3 * pi + j], (px, py, c))
                cp.wait_send()
                cp.wait_recv()

    hbm = [pltpu.HBM(p.shape, p.dtype) for p in pbs]
    res = pl.pallas_call(
        body, name=name, in_specs=[_HBM] * (2 * P) + [_SEM, _SEM, _ANY], out_specs=[_HBM] * (2 * P),
        out_shape=hbm + hbm, input_output_aliases={k: k for k in range(2 * P)},
        compiler_params=pltpu.CompilerParams(has_side_effects=pltpu.SideEffectType.DATAFLOW_SIDE_EFFECTING),
    )(*pbs, *lands, ssem, rsem, after)
    return list(res[:P]), list(res[P:])


def _share_halves(name, gf, pieces):
    def body(in_ref, out, ssems, rsems):
        del in_ref
        x, y, c, _ = _place()
        cps = []
        for pi, (r0, n) in enumerate(pieces):
            mine = out.at[_half_rows(c, r0, n)]
            cps.append(_remote(mine, mine, ssems.at[pi], rsems.at[pi], (x, y, 1 - c)))
        for cp in cps:
            cp.start()
        for pi, (r0, n) in enumerate(pieces):
            theirs = out.at[_half_rows(1 - c, r0, n)]
            _remote(theirs, theirs, ssems.at[pi], rsems.at[pi], (x, y, 1 - c)).wait_recv()
        for cp in cps:
            cp.wait_send()

    return pl.pallas_call(
        body, name=name, in_specs=[_ANY], out_specs=_ANY,
        out_shape=jax.ShapeDtypeStruct(gf.shape, gf.dtype), input_output_aliases={0: 0},
        scratch_shapes=[pltpu.SemaphoreType.DMA((len(pieces),)), pltpu.SemaphoreType.DMA((len(pieces),))],
        compiler_params=pltpu.CompilerParams(has_side_effects=True),
    )(gf)


def _broadcast_small(name, buf, after=None):
    R = buf.shape[0]

    def body(src, *rest):
        out, ssems, rsems = rest[-3:]
        x, y, c, _ = _place()
        me = 4 * x + 2 * y + c
        out[me] = src[...]
        peers = []
        for mask in range(1, 8):
            fx, fy, fc = (mask >> 2) & 1, (mask >> 1) & 1, mask & 1
            peers.append((1 - x if fx else x, 1 - y if fy else y, 1 - c if fc else c))
        cps = [_remote(src, out.at[me], ssems.at[k], rsems.at[k], p) for k, p in enumerate(peers)]
        for cp in cps:
            cp.start()
        for k, (px, py, pc) in enumerate(peers):
            blk = out.at[4 * px + 2 * py + pc]
            _remote(blk, blk, ssems.at[k], rsems.at[k], (px, py, pc)).wait_recv()
        for cp in cps:
            cp.wait_send()

    return pl.pallas_call(
        body, name=name, in_specs=[pl.BlockSpec(memory_space=pltpu.VMEM)] + ([_ANY] if after is not None else []),
        out_specs=pl.BlockSpec(memory_space=pltpu.VMEM),
        out_shape=jax.ShapeDtypeStruct((8, R, LANES), F32),
        scratch_shapes=[pltpu.SemaphoreType.DMA((7,)), pltpu.SemaphoreType.DMA((7,))],
        compiler_params=pltpu.CompilerParams(has_side_effects=True, vmem_limit_bytes=VMEM_LIMIT_V7X),
    )(buf, *([after] if after is not None else []))


def _sum_slabs(name, slabs):
    n, R, _ = slabs.shape

    def body(s_ref, o_ref):
        acc = s_ref[0]
        for k in range(1, n):
            acc = acc + s_ref[k]
        o_ref[...] = acc

    return pl.pallas_call(
        body, name=name, out_shape=jax.ShapeDtypeStruct((R, LANES), F32),
        in_specs=[pl.BlockSpec(memory_space=pltpu.VMEM)], out_specs=pl.BlockSpec(memory_space=pltpu.VMEM),
        compiler_params=_cparams(),
    )(slabs)


def _pick_rows(rows, target=512):
    best = SUBLANES
    for t in range(SUBLANES, min(rows, target) + 1, SUBLANES):
        if rows % t == 0:
            best = t
    return best


def _half_tile(r0, n):
    return _pick_rows(math.gcd(r0, n // 2) if r0 else n // 2)


def _pair_sum(name, dg, land, place, r0, n):
    W = dg.shape[2]
    tr = _half_tile(r0, n)
    nb = (n // 2) // tr

    def body(p_ref, a_ref, b_ref, o_ref):
        del p_ref
        o_ref[...] = (a_ref[...].astype(F32) + b_ref[...].astype(F32)).astype(o_ref.dtype)

    return pl.pallas_call(
        body, name=name,
        grid_spec=pltpu.PrefetchScalarGridSpec(
            num_scalar_prefetch=1, grid=(N_CHIPS, nb),
            in_specs=[pl.BlockSpec((None, tr, W), lambda j, i, p: (j, r0 // tr + p[0] * nb + i, 0)),
                      pl.BlockSpec((None, tr, W), lambda j, i, p: (j, i, 0))],
            out_specs=pl.BlockSpec((None, tr, W), lambda j, i, p: (j, i, 0))),
        out_shape=jax.ShapeDtypeStruct((N_CHIPS, n // 2, W), BF16),
        compiler_params=_cparams("parallel", "parallel"),
    )(place, dg, land)


def _chip_sum(name, pb, land, place, gf, r0, n):
    W = gf.shape[1]
    tr = _half_tile(r0, n)
    nb = (n // 2) // tr

    def body(p_ref, own_ref, lx_ref, ly_ref, ld_ref, gf_in, o_ref):
        del p_ref, gf_in
        o_ref[...] = ((own_ref[...].astype(F32) + lx_ref[...].astype(F32)) + ly_ref[...].astype(F32)) + ld_ref[...].astype(F32)

    slab = lambda flip: pl.BlockSpec((None, tr, W), lambda i, p, _f=flip: (p[1] ^ _f, i, 0))
    return pl.pallas_call(
        body, name=name,
        grid_spec=pltpu.PrefetchScalarGridSpec(
            num_scalar_prefetch=1, grid=(nb,),
            in_specs=[slab(0), slab(2), slab(1), slab(3), pl.BlockSpec(memory_space=pl.ANY)],
            out_specs=pl.BlockSpec((tr, W), lambda i, p: (r0 // tr + p[0] * nb + i, 0))),
        out_shape=jax.ShapeDtypeStruct(gf.shape, F32),
        input_output_aliases={5: 0},
        compiler_params=_cparams("parallel"),
    )(place, pb, land, land, land, gf)


def _ln_stats(z):
    mu = jnp.mean(z, axis=1, keepdims=True)
    zc = z - mu
    rstd = lax.rsqrt(jnp.mean(zc * zc, axis=1, keepdims=True) + LN_EPS)
    return zc * rstd, rstd


def _ln_fwd(name, xin, m, g, b):
    S, D = xin.shape

    def fn(x_, m_, g_, b_):
        xhat, rstd = _ln_stats(ALPHA * x_ + m_)
        y = xhat * g_ + b_
        return y, y, xhat, rstd

    return _rowwise(name, fn, [xin, m, ('full', g), ('full', b)],
                    [('rows', D, F32), ('rows', D, BF16), ('rows', D, F32), ('rows', 1, F32)], S)


def _ln_bwd_core(dy, xhat, rstd, g):
    dxh = dy * g
    return rstd * (dxh - jnp.mean(dxh, axis=1, keepdims=True) - xhat * jnp.mean(dxh * xhat, axis=1, keepdims=True))


def _ln_bwd(name, terms, xhat, rstd, g, after=None):
    S, D = xhat.shape
    scales = [s for _, s in terms]
    n = len(terms)

    def fn(*v):
        dy = v[0] * scales[0] if scales[0] != 1.0 else v[0]
        for t in range(1, n):
            dy = dy + (v[t] * scales[t] if scales[t] != 1.0 else v[t])
        xh, rs, g_ = v[n], v[n + 1], v[n + 2]
        dz = _ln_bwd_core(dy, xh, rs, g_)
        return dz, dz, jnp.sum(dy * xh, axis=0, keepdims=True), jnp.sum(dy, axis=0, keepdims=True)

    return _rowwise(name, fn, [a for a, _ in terms] + [xhat, rstd, ('full', g)] + ([('after', after)] if after is not None else []),
                    [('rows', D, F32), ('rows', D, BF16), ('acc', (1, D), F32), ('acc', (1, D), F32)], S)


def _adamw_math(w, g, m, v):
    m2 = ADAM_B1 * m + (1.0 - ADAM_B1) * g
    v2 = ADAM_B2 * v + (1.0 - ADAM_B2) * (g * g)
    m_hat = m2 / (1.0 - ADAM_B1 ** ADAM_STEP)
    v_hat = v2 / (1.0 - ADAM_B2 ** ADAM_STEP)
    delta = -ADAM_LR * (m_hat / (jnp.sqrt(v_hat) + ADAM_EPS) + ADAM_WD * w)
    return delta, m2, v2


def _adamw(name, w, gfull, row_start, m, v):
    rows, W = w.shape
    tr = math.gcd(math.gcd(rows, row_start), 256) if row_start else math.gcd(rows, 256)

    def fn(w_, g_, m_, v_):
        d, m2, v2 = _adamw_math(w_, g_, m_, v_)
        return g_, d, m2, v2

    return _rowwise(name, fn, [w, ('off', gfull, row_start // tr), m, v], [('rows', W, F32)] * 4, rows, tm=tr)


def _pack(arrs):
    flat = jnp.concatenate([a.reshape(-1).astype(F32) for a in arrs])
    tile = SUBLANES * LANES
    n = -(-flat.shape[0] // tile) * tile
    return jnp.pad(flat, (0, n - flat.shape[0])).reshape(-1, LANES)


def _unpack(buf, shapes):
    flat = buf.reshape(-1)
    out, pos = [], 0
    for shp in shapes:
        n = math.prod(shp)
        out.append(flat[pos:pos + n].reshape(shp))
        pos += n
    return out


BIG = ['fox_w_qkv', 'fox_w_o', 'rel_w_qkv', 'rel_w_o', 'conv_w_pw1', 'conv_w_pw2', 'ffn_w_gate', 'ffn_w_up', 'ffn_w_down']
SMALL_SHARDED = ['fox_w_f', 'conv_b_pw1', 'conv_w_dw', 'conv_b_dw', 'conv_ln_g', 'conv_ln_b', 'conv_b_pw2']
SMALL_SHARD_AXIS = {'fox_w_f': 1, 'conv_b_pw1': 1, 'conv_w_dw': 2, 'conv_b_dw': 1, 'conv_ln_g': 1, 'conv_ln_b': 1, 'conv_b_pw2': 1}
SMALL_REPL = ['fox_b_f', 'rel_bias', 'ln_mix_g', 'ln_mix_b', 'ln_ffn_g', 'ln_ffn_b']
SMALL = SMALL_SHARDED + SMALL_REPL
WEIGHTS = ['fox_w_qkv', 'fox_w_f', 'fox_b_f', 'fox_w_o', 'rel_w_qkv', 'rel_bias', 'rel_w_o', 'conv_w_pw1', 'conv_b_pw1',
           'conv_w_dw', 'conv_b_dw', 'conv_ln_g', 'conv_ln_b', 'conv_w_pw2', 'conv_b_pw2', 'ffn_w_gate', 'ffn_w_up',
           'ffn_w_down', 'ln_mix_g', 'ln_mix_b', 'ln_ffn_g', 'ln_ffn_b']


def kernel(x, fox_w_qkv, fox_w_f, fox_b_f, fox_w_o, rel_w_qkv, rel_bias, rel_w_o, conv_w_pw1, conv_b_pw1, conv_w_dw, conv_b_dw, conv_ln_g, conv_ln_b, conv_w_pw2, conv_b_pw2, ffn_w_gate, ffn_w_up, ffn_w_down, ln_mix_g, ln_mix_b, ln_ffn_g, ln_ffn_b, loss_target, m_fox_w_qkv, m_fox_w_f, m_fox_b_f, m_fox_w_o, m_rel_w_qkv, m_rel_bias, m_rel_w_o, m_conv_w_pw1, m_conv_b_pw1, m_conv_w_dw, m_conv_b_dw, m_conv_ln_g, m_conv_ln_b, m_conv_w_pw2, m_conv_b_pw2, m_ffn_w_gate, m_ffn_w_up, m_ffn_w_down, m_ln_mix_g, m_ln_mix_b, m_ln_ffn_g, m_ln_ffn_b, v_fox_w_qkv, v_fox_w_f, v_fox_b_f, v_fox_w_o, v_rel_w_qkv, v_rel_bias, v_rel_w_o, v_conv_w_pw1, v_conv_b_pw1, v_conv_w_dw, v_conv_b_dw, v_conv_ln_g, v_conv_ln_b, v_conv_w_pw2, v_conv_b_pw2, v_ffn_w_gate, v_ffn_w_up, v_ffn_w_down, v_ln_mix_g, v_ln_mix_b, v_ln_ffn_g, v_ln_ffn_b):
    A = dict(locals())
    Wt = {n: A[n] for n in WEIGHTS}
    Mo = {n: A['m_' + n] for n in WEIGHTS}
    Vo = {n: A['v_' + n] for n in WEIGHTS}

    _, S, D = x.shape
    H = D // HEAD_DIM
    Ds = D // N_CHIPS
    Nq = fox_w_qkv.shape[2]
    Np = conv_w_pw1.shape[2]
    Fs = ffn_w_gate.shape[2]
    my_x, my_y, my_c = lax.axis_index("x"), lax.axis_index("y"), lax.axis_index("c")
    my_chip = 2 * my_x + my_y
    place = jnp.stack([my_c, my_chip]).astype(jnp.int32)

    wo_base = DEPTH * Fs
    where = {
        'fox_w_qkv': ('qkv', 0), 'rel_w_qkv': ('qkv', N_FOX * D),
        'ffn_w_gate': ('ffn', 0), 'ffn_w_up': ('ffn', DEPTH * D),
        'conv_w_pw1': ('pw1', 0),
        'ffn_w_down': ('dm', 0), 'fox_w_o': ('dm', wo_base), 'rel_w_o': ('dm', wo_base + N_FOX * Ds),
        'conv_w_pw2': ('dm', wo_base + (N_FOX + 1) * Ds),
    }
    members = {'qkv': ['fox_w_qkv', 'rel_w_qkv'], 'ffn': ['ffn_w_gate', 'ffn_w_up'], 'pw1': ['conv_w_pw1'],
               'dm': ['ffn_w_down', 'fox_w_o', 'rel_w_o', 'conv_w_pw2']}
    flat2 = lambda a: a.reshape(-1, a.shape[-1])
    own = {g: jnp.concatenate([flat2(Wt[n]).astype(BF16) for n in ms], axis=0) for g, ms in members.items()}

    def layer_pieces(i):
        kind, j = i % 3, i // 3
        slot = j if kind == 0 else (N_FOX if kind == 1 else N_FOX + 1)
        w_in = (GROUPS.index('pw1'), 0, D) if kind == 2 else (GROUPS.index('qkv'), slot * D, D)
        return [w_in, (GROUPS.index('dm'), wo_base + slot * Ds, Ds), (GROUPS.index('ffn'), i * D, D),
                (GROUPS.index('ffn'), (DEPTH + i) * D, D), (GROUPS.index('dm'), i * Fs, Fs)]

    small_shapes = [Wt[n].shape for n in SMALL_SHARDED]
    slabs = _broadcast_small("gather_small", _pack([Wt[n] for n in SMALL_SHARDED]))

    stages = [part for i in range(DEPTH) for part in (layer_pieces(i)[:2], layer_pieces(i)[2:])]
    gstages = [part for i in range(DEPTH) for part in (layer_pieces(i)[:1], layer_pieces(i)[1:])]
    (g_first, r_first, n_first), = gstages[0]
    buf_first = _place_own("place_" + GROUPS[g_first], own[GROUPS[g_first]], place)
    ssems_first, rsems_first, (buf_first,) = _gather_start("gather_start_first", [buf_first], [[(0, r_first, n_first)]], slabs)
    ssems_rest, rsems_rest, wg_list = _gather_start(
        "gather_start_rest", [buf_first if gi == g_first else _place_own("place_" + g, own[g], place) for gi, g in enumerate(GROUPS)],
        gstages[1:], slabs)
    gather_ssems, gather_rsems = list(ssems_first) + list(ssems_rest), list(rsems_first) + list(rsems_rest)
    WG = dict(zip(GROUPS, wg_list))
    DG = {g: lax.empty(WG[g].shape, BF16) for g in own}

    per_chip = [_unpack(slabs[2 * j], small_shapes) for j in range(N_CHIPS)]
    full = {n: jnp.concatenate([per_chip[j][i] for j in range(N_CHIPS)], axis=SMALL_SHARD_AXIS[n])
            for i, n in enumerate(SMALL_SHARDED)}
    row = lambda v: v.reshape(1, -1)

    SG = {}

    def ffn_fwd(i, xb):
        hg, hu, act = _mm_gate_up(f"ffn{i}_gate_up", xb, WG['ffn'], i * D, (DEPTH + i) * D)
        f = _mm_row(f"ffn{i}_down", act, WG['dm'], i * Fs, Fs)
        return f, (hg, hu, act)

    def ffn_bwd(i, xb, saved, dzb):
        hg, hu, act = saved
        DG['dm'] = _mm_dw(f"ffn{i}_dw_down", act, dzb, DG['dm'], i * Fs, 'row')
        dhg, dhu = _mm_dact(f"ffn{i}_dact", dzb, WG['dm'], i * Fs, Fs, hg, hu)
        DG['ffn'] = _mm_dw(f"ffn{i}_dw_gate", xb, dhg, DG['ffn'], i * D, 'col')
        DG['ffn'] = _mm_dw(f"ffn{i}_dw_up", xb, dhu, DG['ffn'], (DEPTH + i) * D, 'col')
        return _mm_col_t(f"ffn{i}_dx", [(dhg, i * D), (dhu, (DEPTH + i) * D)], WG['ffn'], D)

    def fox_fwd(j, xb, rest_stage):
        qkv = _mm_col(f"fox{j}_qkv", xb, WG['qkv'], j * D)
        wf = full['fox_w_f'][j].astype(BF16)
        def gate_fn(x_, w_, b_):
            z_ = jnp.dot(x_, w_, preferred_element_type=F32) + b_
            return z_, jnp.minimum(z_, 0.0) - jnp.log(1.0 + jnp.exp(-jnp.abs(z_)))

        z, logf = _rowwise(f"fox{j}_gate", gate_fn, [xb, ('full', wf), ('full', row(fox_b_f[j]))],
                           [('rows', H, F32), ('rows', H, F32)], S)
        c = _cumsum_rows(f"fox{j}_cumsum", logf, False)
        crow = c.T.reshape(H, 1, S)
        o = _fox_fwd(qkv, c, crow, H)
        weights_ready(rest_stage, o)
        m = _mm_row(f"fox{j}_wo", o, WG['dm'], wo_base + j * Ds, Ds)
        return m, (qkv, z, c, crow, o, wf)

    def fox_bwd(j, xb, saved, dzb):
        qkv, z, c, crow, o, wf = saved
        DG['dm'] = _mm_dw(f"fox{j}_dw_o", o, dzb, DG['dm'], wo_base + j * Ds, 'row')
        do = _mm_row_t(f"fox{j}_do", dzb, WG['dm'], wo_base + j * Ds, Ds, BF16)
        dq, dk, dv, dcrow = _fox_bwd(qkv, c, crow, do, H)
        dqkv = jnp.concatenate([dq, dk, dv], axis=1)
        dlogf = _cumsum_rows(f"fox{j}_rcumsum", dcrow.reshape(H, S).T, True)

        def fn(x_, dl_, z_, w_):
            dz_ = dl_ * _sigmoid(-z_)
            dzb_ = dz_.astype(BF16)
            return _dot_nt(dzb_, w_), _dot_tn(x_, dzb_), jnp.sum(dz_, axis=0, keepdims=True)

        dh_f, dwf, dbf = _rowwise(f"fox{j}_gate_bwd", fn, [xb, dlogf, z, ('full', wf)],
                                  [('rows', D, F32), ('acc', (D, H), F32), ('acc', (1, H), F32)], S)
        SG.setdefault('fox_w_f', [None] * N_FOX)[j] = dwf
        SG.setdefault('fox_b_f', [None] * N_FOX)[j] = dbf.reshape(H)
        DG['qkv'] = _mm_dw(f"fox{j}_dw_qkv", xb, dqkv, DG['qkv'], j * D, 'col')
        dh = _mm_col_t(f"fox{j}_dx", [(dqkv, j * D)], WG['qkv'], D)
        return [dh, dh_f]

    def rel_fwd(xb, rest_stage):
        qkv = _mm_col("rel_qkv", xb, WG['qkv'], N_FOX * D)
        rb_pad = jnp.pad(rel_bias[0], ((0, 0), (0, REL_TABLE_PAD - REL_TABLE)))
        bias = _rel_window_bias(jnp.transpose(_rel_expand(rb_pad), (1, 0, 2)))
        o = _rel_fwd(qkv, bias, H)
        weights_ready(rest_stage, o)
        m = _mm_row("rel_wo", o, WG['dm'], wo_base + N_FOX * Ds, Ds)
        return m, (qkv, bias, o)

    def rel_bwd(xb, saved, dzb):
        qkv, bias, o = saved
        DG['dm'] = _mm_dw("rel_dw_o", o, dzb, DG['dm'], wo_base + N_FOX * Ds, 'row')
        do = _mm_row_t("rel_do", dzb, WG['dm'], wo_base + N_FOX * Ds, Ds, BF16)
        dq, dk, dv, dbias = _rel_bwd(qkv, bias, do, H)
        SG['rel_bias'] = _rel_reduce(jnp.transpose(dbias, (1, 0, 2)))[:, :REL_TABLE].reshape(1, H, REL_TABLE)
        dqkv = jnp.concatenate([dq, dk, dv], axis=1)
        DG['qkv'] = _mm_dw("rel_dw_qkv", xb, dqkv, DG['qkv'], N_FOX * D, 'col')
        return [_mm_col_t("rel_dx", [(dqkv, N_FOX * D)], WG['qkv'], D)]

    w_dw32 = jnp.pad(full['conv_w_dw'][0], ((0, CONV_HALO - CONV_K), (0, 0)))
    cg, cb = full['conv_ln_g'], full['conv_ln_b']

    def conv_fwd(xb, rest_stage):
        u = _mm_col("conv_pw1", xb, WG['pw1'], 0, bias=full['conv_b_pw1'], out_dtype=F32)
        u2, = _rowwise("conv_glu", lambda a_, g_: [a_ * _sigmoid(g_)],
                       [('cols', u, D, 0), ('cols', u, D, 1)], [('rows', D, F32)], S)
        yc = _dwconv("conv_dw", u2, w_dw32, full['conv_b_dw'], False)

        def fn(y_, g_, b_):
            xhat, rstd = _ln_stats(y_)
            ln = xhat * g_ + b_
            return ln * _sigmoid(ln), xhat, rstd

        zc, xhat, rstd = _rowwise("conv_ln_silu", fn, [yc, ('full', cg), ('full', cb)],
                                  [('rows', D, BF16), ('rows', D, F32), ('rows', 1, F32)], S)
        weights_ready(rest_stage, zc)
        m = _mm_row("conv_pw2", zc, WG['dm'], wo_base + (N_FOX + 1) * Ds, Ds, bias=full['conv_b_pw2'])
        return m, (u, u2, zc, xhat, rstd)

    def conv_bwd(xb, saved, dz, dzb):
        u, u2, zc, xhat, rstd = saved
        r0 = wo_base + (N_FOX + 1) * Ds
        DG['dm'] = _mm_dw("conv_dw_pw2", zc, dzb, DG['dm'], r0, 'row')
        dzc = _mm_row_t("conv_dzc", dzb, WG['dm'], r0, Ds, F32)

        def fn(dm_, dzc_, xh_, rs_, g_, b_):
            ln = xh_ * g_ + b_
            sg = _sigmoid(ln)
            dln = dzc_ * (sg * (1.0 + ln * (1.0 - sg)))
            dyc = _ln_bwd_core(dln, xh_, rs_, g_)
            col = lambda t: jnp.sum(t, axis=0, keepdims=True)
            return dyc, col(dm_), col(dln * xh_), col(dln), col(dyc)

        dyc, SG['conv_b_pw2'], SG['conv_ln_g'], SG['conv_ln_b'], SG['conv_b_dw'] = _rowwise(
            "conv_ln_silu_bwd", fn, [dz, dzc, xhat, rstd, ('full', cg), ('full', cb)],
            [('rows', D, F32)] + [('acc', (1, D), F32)] * 4, S)
        du2 = _dwconv("conv_dw_bwd_x", dyc, w_dw32, jnp.zeros((1, D), F32), True)
        SG['conv_w_dw'] = _dwconv_dw(u2, dyc)[:CONV_K].reshape(1, CONV_K, D)

        def fn2(du2_, a_, g_):
            sg = _sigmoid(g_)
            da, dgt = du2_ * sg, du2_ * a_ * sg * (1.0 - sg)
            return da, dgt, jnp.sum(da, axis=0, keepdims=True), jnp.sum(dgt, axis=0, keepdims=True)

        da, dgt, dba, dbg = _rowwise("conv_glu_bwd", fn2, [du2, ('cols', u, D, 0), ('cols', u, D, 1)],
                                     [('rows', D, BF16), ('rows', D, BF16), ('acc', (1, D), F32), ('acc', (1, D), F32)], S)
        SG['conv_b_pw1'] = jnp.concatenate([dba, dbg], axis=1)
        du = jnp.concatenate([da, dgt], axis=1)
        DG['pw1'] = _mm_dw("conv_dw_pw1", xb, du, DG['pw1'], 0, 'col')
        return [_mm_col_t("conv_dx", [(du, 0)], WG['pw1'], D)]

    xs = x[0]
    xs_b = xs.astype(BF16)
    tape = []

    def weights_ready(s, after):
        bufs = _gather_wait(f"gather_wait{s}", [WG[g] for g in GROUPS], gather_ssems[s], gather_rsems[s], gstages[s], after)
        WG.update(zip(GROUPS, _gather_forward(f"gather_fwd{s}", bufs, gstages[s])))

    for i in range(DEPTH):
        kind, j = i % 3, i // 3
        weights_ready(2 * i, xs)
        if kind == 0:
            m, msaved = fox_fwd(j, xs_b, 2 * i + 1)
        elif kind == 1:
            m, msaved = rel_fwd(xs_b, 2 * i + 1)
        else:
            m, msaved = conv_fwd(xs_b, 2 * i + 1)
        xm, xm_b, xhat1, rstd1 = _ln_fwd(f"ln_mix{i}", xs, m, row(ln_mix_g[i]), row(ln_mix_b[i]))
        f, fsaved = ffn_fwd(i, xm_b)
        xo, xo_b, xhat2, rstd2 = _ln_fwd(f"ln_ffn{i}", xm, f, row(ln_ffn_g[i]), row(ln_ffn_b[i]))
        tape.append((xs_b, msaved, xhat1, rstd1, xm_b, fsaved, xhat2, rstd2))
        xs, xs_b = xo, xo_b

    def loss_fn(y_, t_):
        e = y_ - t_
        return e * (1.0 / D), jnp.sum(e * e, axis=0, keepdims=True)

    dy, sq = _rowwise("loss", loss_fn, [xs, loss_target[0]], [('rows', D, F32), ('acc', (1, D), F32)], S)
    loss = lax.psum(jnp.sum(sq) * (0.5 / D), ("x", "y", "c"))

    GF = {g: lax.empty(WG[g].shape[1:], F32) for g in GROUPS}
    started = [None] * len(stages)

    def reduce_start(s):
        dgs = [DG[g] for g in GROUPS]
        lands = _swap_halves(f"pair_swap{s}", dgs, stages[s])
        pbs = [_pair_sum(f"pair_sum{s}_{pi}", dgs[g], lands[pi], place, r0, n) for pi, (g, r0, n) in enumerate(stages[s])]
        started[s] = _scatter_start(f"scatter_start{s}", pbs)
        token = started[s][2][0]
        if s + 1 < len(stages):
            reduce_finish(s + 1, token)
        return token

    def reduce_finish(s, after):
        ssem, rsem, pbs, lands2 = started[s]
        pbs, lands2 = _scatter_wait(f"scatter_wait{s}", ssem, rsem, pbs, lands2, after)
        for pi, (g, r0, n) in enumerate(stages[s]):
            GF[GROUPS[g]] = _chip_sum(f"chip_sum{s}_{pi}", pbs[pi], lands2[pi], place, GF[GROUPS[g]], r0, n)

    terms = [(dy, 1.0)]
    token = None
    g_mix, b_mix, g_ffn, b_ffn = [None] * DEPTH, [None] * DEPTH, [None] * DEPTH, [None] * DEPTH
    for i in reversed(range(DEPTH)):
        kind, j = i % 3, i // 3
        xin_b, msaved, xhat1, rstd1, xm_b, fsaved, xhat2, rstd2 = tape[i]
        dz2, dz2b, g_ffn[i], b_ffn[i] = _ln_bwd(f"ln_ffn{i}_bwd", terms, xhat2, rstd2, row(ln_ffn_g[i]), after=token)
        dx_ffn = ffn_bwd(i, xm_b, fsaved, dz2b)
        token = reduce_start(2 * i + 1)
        dz1, dz1b, g_mix[i], b_mix[i] = _ln_bwd(f"ln_mix{i}_bwd", [(dz2, ALPHA), (dx_ffn, 1.0)], xhat1, rstd1,
                                                row(ln_mix_g[i]), after=token)
        if kind == 0:
            mix_terms = fox_bwd(j, xin_b, msaved, dz1b)
        elif kind == 1:
            mix_terms = rel_bwd(xin_b, msaved, dz1b)
        else:
            mix_terms = conv_bwd(xin_b, msaved, dz1, dz1b)
        terms = [(dz1, ALPHA)] + [(t, 1.0) for t in mix_terms]
        token = reduce_start(2 * i)

    def gx_fn(*v):
        acc = v[0] * ALPHA
        for t in v[1:]:
            acc = acc + t
        return [acc]

    grad_x, = _rowwise("grad_x", gx_fn, [a for a, _ in terms], [('rows', D, F32)], S)
    grad_x = grad_x.reshape(1, S, D)

    SG['fox_w_f'] = jnp.stack(SG['fox_w_f'])
    SG['fox_b_f'] = jnp.stack(SG['fox_b_f'])
    SG['ln_mix_g'] = jnp.concatenate(g_mix, axis=0)
    SG['ln_mix_b'] = jnp.concatenate(b_mix, axis=0)
    SG['ln_ffn_g'] = jnp.concatenate(g_ffn, axis=0)
    SG['ln_ffn_b'] = jnp.concatenate(b_ffn, axis=0)

    grads, deltas, new_m, new_v = {}, {}, {}, {}

    reduce_finish(0, grad_x)
    links_idle = GF['dm']
    for gi, g in enumerate(GROUPS):
        GF[g] = _share_halves("pair_share_" + g, GF[g], [(r0, n) for st in stages for (pg, r0, n) in st if pg == gi])

    for n in BIG:
        g, r0 = where[n]
        outs = _adamw("adamw_" + n, flat2(Wt[n]), GF[g], r0, flat2(Mo[n]), flat2(Vo[n]))
        grads[n], deltas[n], new_m[n], new_v[n] = [o.reshape(Wt[n].shape) for o in outs]

    full_shapes = [SG[n].shape for n in SMALL]
    summed = _sum_slabs("small_sum", _broadcast_small("small_exchange", _pack([SG[n] for n in SMALL]), after=links_idle))
    gsm = dict(zip(SMALL, _unpack(summed, full_shapes)))
    for n in SMALL_SHARDED:
        ax = SMALL_SHARD_AXIS[n]
        width = Wt[n].shape[ax]
        gsm[n] = lax.dynamic_slice_in_dim(gsm[n], my_chip * width, width, axis=ax)
    own_shapes = [Wt[n].shape for n in SMALL]
    packed = [_pack([src[n] for n in SMALL]) for src in (Wt, gsm, Mo, Vo)]
    rows_small = packed[0].shape[0]

    def small_fn(w_, g_, m_, v_):
        return _adamw_math(w_, g_, m_, v_)

    sd, sm, sv = _rowwise("adamw_small", small_fn, packed, [('rows', LANES, F32)] * 3, rows_small, tm=rows_small)
    for n, d_, m_, v_ in zip(SMALL, _unpack(sd, own_shapes), _unpack(sm, own_shapes), _unpack(sv, own_shapes)):
        grads[n], deltas[n], new_m[n], new_v[n] = gsm[n], d_, m_, v_

    return (loss, grad_x, *[grads[n] for n in WEIGHTS], *[deltas[n] for n in WEIGHTS],
            *[new_m[n] for n in WEIGHTS], *[new_v[n] for n in WEIGHTS])
```

```python
import functools
import math

import jax
import jax.numpy as jnp
from jax import lax
from jax.experimental import pallas as pl
from jax.experimental.pallas import tpu as pltpu

F32 = jnp.float32
BF16 = jnp.bfloat16
MESH_IDS = pl.DeviceIdType.MESH
HIGHEST = lax.Precision.HIGHEST

N_CHIPS = 4
DEPTH = 4
N_FOX = 2
HEAD_DIM = 128
CHUNK = 64
LEFT_CHUNKS = 8
BAND_KEYS = (LEFT_CHUNKS + 1) * CHUNK
PAD_KEYS = LEFT_CHUNKS * CHUNK
REL_CLIP = 128
REL_TABLE = 2 * REL_CLIP + 1
REL_TABLE_PAD = 384
REL_QB = 4 * CHUNK
REL_WIN = REL_QB + PAD_KEYS
CONV_K = 31
CONV_HALO = 32
ALPHA = (2.0 * DEPTH) ** 0.25
LN_EPS = 1e-5
ADAM_LR, ADAM_B1, ADAM_B2, ADAM_EPS, ADAM_WD, ADAM_STEP = 0.001, 0.9, 0.999, 1e-08, 0.01, 10
NEG_BIG = -1e30
VMEM_LIMIT_V7X = 56 * 1024 * 1024
LANES = 128
SUBLANES = 8
MM_ROWS = 1024


def _cparams(*sem):
    return pltpu.CompilerParams(dimension_semantics=sem if sem else None, vmem_limit_bytes=VMEM_LIMIT_V7X)


def _pick(dim, target):
    best = None
    for t in range(LANES, min(dim, target) + 1, LANES):
        if dim % t == 0:
            best = t
    return best if best is not None else dim


def _dot_nt(a, b):
    return lax.dot_general(a, b, (((1,), (1,)), ((), ())), preferred_element_type=F32)


def _dot_tn(a, b):
    return lax.dot_general(a, b, (((0,), (0,)), ((), ())), preferred_element_type=F32)


def _sigmoid(z):
    return 1.0 / (1.0 + jnp.exp(-z))


def _rowwise(name, fn, ins, outs, S, tm=256):
    tm = min(tm, S)
    afters = [it[1] for it in ins if isinstance(it, tuple) and it[0] == 'after']
    ins = [it for it in ins if not (isinstance(it, tuple) and it[0] == 'after')]
    arrs, in_specs = [], []
    for it in ins:
        if isinstance(it, tuple) and it[0] == 'full':
            a = it[1]
            in_specs.append(pl.BlockSpec(a.shape, lambda i, _n=a.ndim: (0,) * _n))
        elif isinstance(it, tuple) and it[0] == 'cols':
            _, a, width, blk = it
            in_specs.append(pl.BlockSpec((tm, width), lambda i, _b=blk: (i, _b)))
        elif isinstance(it, tuple) and it[0] == 'off':
            _, a, off = it
            in_specs.append(pl.BlockSpec((tm, a.shape[1]), lambda i, _o=off: (i + _o, 0)))
        else:
            a = it
            in_specs.append(pl.BlockSpec((tm, a.shape[1]), lambda i: (i, 0)))
        arrs.append(a)
    out_shape, out_specs = [], []
    for kind, shp, dt in outs:
        if kind == 'rows':
            out_shape.append(jax.ShapeDtypeStruct((S, shp), dt))
            out_specs.append(pl.BlockSpec((tm, shp), lambda i: (i, 0)))
        else:
            out_shape.append(jax.ShapeDtypeStruct(shp, dt))
            out_specs.append(pl.BlockSpec(shp, lambda i, _n=len(shp): (0,) * _n))
    n_in = len(arrs)
    in_specs += [pl.BlockSpec(memory_space=pl.ANY)] * len(afters)

    def body(*refs):
        vals = fn(*[r[...] for r in refs[:n_in]])
        first = pl.program_id(0) == 0
        for (kind, _, _), r, v in zip(outs, refs[n_in + len(afters):], vals):
            if kind == 'rows':
                r[...] = v.astype(r.dtype)
            else:
                @pl.when(first)
                def _(r=r, v=v):
                    r[...] = v.astype(r.dtype)

                @pl.when(jnp.logical_not(first))
                def _(r=r, v=v):
                    r[...] += v.astype(r.dtype)

    has_acc = any(k != 'rows' for k, _, _ in outs)
    res = pl.pallas_call(
        body, name=name, grid=(S // tm,), in_specs=in_specs, out_specs=out_specs, out_shape=out_shape,
        compiler_params=_cparams("arbitrary" if has_acc else "parallel"),
    )(*arrs, *afters)
    return res


def _mm_col(name, a, wg, row_start, bias=None, out_dtype=BF16):
    S, K = a.shape
    _, _, Ns = wg.shape
    rb = row_start // K
    tm = min(MM_ROWS, S)

    def body(a_ref, w_ref, *rest):
        acc = jnp.dot(a_ref[...].astype(BF16), w_ref[...], preferred_element_type=F32)
        if bias is not None:
            acc = acc + rest[0][...]
        rest[-1][...] = acc.astype(out_dtype)

    in_specs = [pl.BlockSpec((tm, K), lambda j, m: (m, 0)), pl.BlockSpec((None, K, Ns), lambda j, m: (j, rb, 0))]
    args = [a, wg]
    if bias is not None:
        in_specs.append(pl.BlockSpec((1, Ns), lambda j, m: (0, j)))
        args.append(bias)
    return pl.pallas_call(
        body, name=name, grid=(N_CHIPS, S // tm), in_specs=in_specs,
        out_specs=pl.BlockSpec((tm, Ns), lambda j, m: (m, j)),
        out_shape=jax.ShapeDtypeStruct((S, N_CHIPS * Ns), out_dtype),
        compiler_params=_cparams("parallel", "parallel"),
    )(*args)


def _mm_row(name, a, wg, row_start, Ks, bias=None):
    S = a.shape[0]
    N = wg.shape[2]
    rb = row_start // Ks
    tm = min(MM_ROWS, S)

    def body(a_ref, w_ref, *rest):
        o_ref = rest[-1]
        j = pl.program_id(1)
        d = jnp.dot(a_ref[...].astype(BF16), w_ref[...], preferred_element_type=F32)

        @pl.when(j == 0)
        def _():
            o_ref[...] = d + rest[0][...] if bias is not None else d

        @pl.when(j > 0)
        def _():
            o_ref[...] += d

    in_specs = [pl.BlockSpec((tm, Ks), lambda m, j: (m, j)), pl.BlockSpec((None, Ks, N), lambda m, j: (j, rb, 0))]
    args = [a, wg]
    if bias is not None:
        in_specs.append(pl.BlockSpec((1, N), lambda m, j: (0, 0)))
        args.append(bias)
    return pl.pallas_call(
        body, name=name, grid=(S // tm, N_CHIPS), in_specs=in_specs,
        out_specs=pl.BlockSpec((tm, N), lambda m, j: (m, 0)),
        out_shape=jax.ShapeDtypeStruct((S, N), F32),
        compiler_params=_cparams("parallel", "arbitrary"),
    )(*args)


def _mm_col_t(name, pairs, wg, K):
    S = pairs[0][0].shape[0]
    Ns = wg.shape[2]
    tm = min(MM_ROWS, S)
    n = len(pairs)
    tkk = K // n if (K // n) % LANES == 0 else K

    def body(*refs):
        o_ref = refs[-1]
        j = pl.program_id(2)
        d = _dot_nt(refs[0][...], refs[n][...])
        for p in range(1, n):
            d = d + _dot_nt(refs[p][...], refs[n + p][...])

        @pl.when(j == 0)
        def _():
            o_ref[...] = d

        @pl.when(j > 0)
        def _():
            o_ref[...] += d

    in_specs = [pl.BlockSpec((tm, Ns), lambda m, kb, j: (m, j)) for _ in pairs]
    in_specs += [pl.BlockSpec((None, tkk, Ns), lambda m, kb, j, _rb=rs // tkk: (j, _rb + kb, 0)) for _, rs in pairs]
    return pl.pallas_call(
        body, name=name, grid=(S // tm, K // tkk, N_CHIPS), in_specs=in_specs,
        out_specs=pl.BlockSpec((tm, tkk), lambda m, kb, j: (m, kb)),
        out_shape=jax.ShapeDtypeStruct((S, K), F32),
        compiler_params=_cparams("parallel", "parallel", "arbitrary"),
    )(*[dy for dy, _ in pairs], *[wg for _ in pairs])


def _mm_row_t(name, dy, wg, row_start, Ks, out_dtype):
    S, N = dy.shape
    rb = row_start // Ks
    tm = min(MM_ROWS, S)

    def body(dy_ref, w_ref, o_ref):
        o_ref[...] = _dot_nt(dy_ref[...], w_ref[...]).astype(out_dtype)

    return pl.pallas_call(
        body, name=name, grid=(N_CHIPS, S // tm),
        in_specs=[pl.BlockSpec((tm, N), lambda j, m: (m, 0)), pl.BlockSpec((None, Ks, N), lambda j, m: (j, rb, 0))],
        out_specs=pl.BlockSpec((tm, Ks), lambda j, m: (m, j)),
        out_shape=jax.ShapeDtypeStruct((S, N_CHIPS * Ks), out_dtype),
        compiler_params=_cparams("parallel", "parallel"),
    )(dy, wg)


def _silu_parts(g):
    sg = _sigmoid(g)
    return g * sg, sg * (1.0 + g * (1.0 - sg))


def _mm_gate_up(name, a, wg, gate_row, up_row):
    S, K = a.shape
    Ns = wg.shape[2]
    tm = min(MM_ROWS // 2, S)

    def body(a_ref, wg_ref, wu_ref, hg_ref, hu_ref, act_ref):
        a_ = a_ref[...]
        hg = jnp.dot(a_, wg_ref[...], preferred_element_type=F32).astype(BF16)
        hu = jnp.dot(a_, wu_ref[...], preferred_element_type=F32).astype(BF16)
        hg_ref[...] = hg
        hu_ref[...] = hu
        act_ref[...] = (_silu_parts(hg.astype(F32))[0] * hu.astype(F32)).astype(BF16)

    out = jax.ShapeDtypeStruct((S, N_CHIPS * Ns), BF16)
    w_spec = lambda rb: pl.BlockSpec((None, K, Ns), lambda j, m: (j, rb, 0))
    o_spec = pl.BlockSpec((tm, Ns), lambda j, m: (m, j))
    return pl.pallas_call(
        body, name=name, grid=(N_CHIPS, S // tm),
        in_specs=[pl.BlockSpec((tm, K), lambda j, m: (m, 0)), w_spec(gate_row // K), w_spec(up_row // K)],
        out_specs=[o_spec, o_spec, o_spec], out_shape=[out, out, out],
        compiler_params=_cparams("parallel", "parallel"),
    )(a, wg, wg)


def _mm_dact(name, dy, wg, row_start, Ks, hg, hu):
    S, N = dy.shape
    rb = row_start // Ks
    tm = min(512, S)

    def body(dy_ref, w_ref, hg_ref, hu_ref, dhg_ref, dhu_ref):
        dact = _dot_nt(dy_ref[...], w_ref[...])
        silu, dsilu = _silu_parts(hg_ref[...].astype(F32))
        dhg_ref[...] = (dact * hu_ref[...].astype(F32) * dsilu).astype(BF16)
        dhu_ref[...] = (dact * silu).astype(BF16)

    out = jax.ShapeDtypeStruct((S, N_CHIPS * Ks), BF16)
    t_spec = pl.BlockSpec((tm, Ks), lambda j, m: (m, j))
    return pl.pallas_call(
        body, name=name, grid=(N_CHIPS, S // tm),
        in_specs=[pl.BlockSpec((tm, N), lambda j, m: (m, 0)), pl.BlockSpec((None, Ks, N), lambda j, m: (j, rb, 0)), t_spec, t_spec],
        out_specs=[t_spec, t_spec], out_shape=[out, out],
        compiler_params=_cparams("parallel", "parallel"),
    )(dy, wg, hg, hu)


def _mm_dw(name, a, dy, dg, row_start, kind):
    S = a.shape[0]
    _, _, W = dg.shape
    if kind == 'col':
        K = a.shape[1]
        rows = K
        tk, tn = _pick(K, MM_ROWS), W
        a_map = lambda j, nb, kb: (0, kb)
        dy_map = lambda j, nb, kb: (0, j * (W // tn) + nb)
    else:
        rows = a.shape[1] // N_CHIPS
        tk = rows if rows * S * 2 * 2 <= 12 * 1024 * 1024 else _pick(rows, 512)
        tn = _pick(W, 1024)
        a_map = lambda j, nb, kb: (0, j * (rows // tk) + kb)
        dy_map = lambda j, nb, kb: (0, nb)
    rb = row_start // tk
    assert row_start % tk == 0

    def body(a_ref, dy_ref, dg_in, o_ref):
        del dg_in
        o_ref[...] = _dot_tn(a_ref[...], dy_ref[...]).astype(o_ref.dtype)

    return pl.pallas_call(
        body, name=name, grid=(N_CHIPS, W // tn, rows // tk),
        in_specs=[pl.BlockSpec((S, tk), a_map), pl.BlockSpec((S, tn), dy_map), pl.BlockSpec(memory_space=pl.ANY)],
        out_specs=pl.BlockSpec((None, tk, tn), lambda j, nb, kb: (j, rb + kb, nb)),
        out_shape=jax.ShapeDtypeStruct(dg.shape, dg.dtype),
        input_output_aliases={2: 0},
        compiler_params=_cparams("parallel", "parallel", "parallel"),
    )(a, dy, dg)


def _fox_probs(q, k, c_blk, crow, h, qi, tq):
    n = k.shape[0]
    s = _dot_nt(q, k) * (HEAD_DIM ** -0.5)
    lane = lax.broadcasted_iota(jnp.int32, c_blk.shape, 1)
    ccol = jnp.sum(jnp.where(lane == h, c_blk, 0.0), axis=1, keepdims=True)
    s = s + (ccol - crow)
    t_idx = qi * tq + lax.broadcasted_iota(jnp.int32, (tq, n), 0)
    s_idx = lax.broadcasted_iota(jnp.int32, (tq, n), 1)
    s = jnp.where(s_idx <= t_idx, s, NEG_BIG)
    p = jnp.exp(s - jnp.max(s, axis=1, keepdims=True))
    return p * (1.0 / jnp.sum(p, axis=1, keepdims=True))


def _per_query_block(qi, nq, tq, fn):
    for qv in range(nq):
        @pl.when(qi == qv)
        def _(qv=qv):
            fn(qv, (qv + 1) * tq)


FOX_HEADS = 2


def _head_cols(a):
    return slice(a * HEAD_DIM, (a + 1) * HEAD_DIM)


def _fox_fwd(qkv, c, crow, H):
    S = qkv.shape[0]
    tq = min(256, S)
    heads = 2 * FOX_HEADS
    hw = heads * HEAD_DIM
    G = H // heads

    def body(q_ref, k_ref, v_ref, c_ref, crow_ref, o_ref):
        def block(qv, n):
            for a in range(heads):
                cols = _head_cols(a)
                p = _fox_probs(q_ref[:, cols], k_ref[0:n, cols], c_ref[...], crow_ref[a, :, 0:n],
                               heads * pl.program_id(0) + a, qv, tq)
                o_ref[:, cols] = jnp.dot(p.astype(BF16), v_ref[0:n, cols], preferred_element_type=F32).astype(o_ref.dtype)

        _per_query_block(pl.program_id(1), S // tq, tq, block)

    return pl.pallas_call(
        body, name="fox_attn_fwd", grid=(G, S // tq),
        in_specs=[pl.BlockSpec((tq, hw), lambda g, i: (i, g)),
                  pl.BlockSpec((S, hw), lambda g, i: (0, G + g)),
                  pl.BlockSpec((S, hw), lambda g, i: (0, 2 * G + g)),
                  pl.BlockSpec((tq, H), lambda g, i: (i, 0)),
                  pl.BlockSpec((heads, 1, S), lambda g, i: (g, 0, 0))],
        out_specs=pl.BlockSpec((tq, hw), lambda g, i: (i, g)),
        out_shape=jax.ShapeDtypeStruct((S, H * HEAD_DIM), BF16),
        compiler_params=_cparams("parallel", "parallel"),
    )(qkv, qkv, qkv, c, crow)


def _fox_bwd(qkv, c, crow, do, H):
    S = qkv.shape[0]
    tq = min(256, S)
    nq = S // tq
    hw = FOX_HEADS * HEAD_DIM
    G = H // FOX_HEADS

    def body(q_ref, k_ref, v_ref, c_ref, crow_ref, do_ref, dq_ref, dk_ref, dv_ref, dc_ref, dk_acc, dv_acc):
        qi = pl.program_id(1)

        @pl.when(qi == 0)
        def _():
            dk_acc[...] = jnp.zeros_like(dk_acc)
            dv_acc[...] = jnp.zeros_like(dv_acc)
            dc_ref[...] = jnp.zeros_like(dc_ref)

        def block(qv, n):
            for a in range(FOX_HEADS):
                cols = _head_cols(a)
                q, k, v, do_ = q_ref[:, cols], k_ref[0:n, cols], v_ref[0:n, cols], do_ref[:, cols]
                p = _fox_probs(q, k, c_ref[...], crow_ref[a, :, 0:n], FOX_HEADS * pl.program_id(0) + a, qv, tq)
                dv_acc[0:n, cols] += _dot_tn(p.astype(BF16), do_)
                dp = _dot_nt(do_, v)
                ds = p * (dp - jnp.sum(p * dp, axis=1, keepdims=True))
                dsb = (ds * (HEAD_DIM ** -0.5)).astype(BF16)
                dq_ref[:, cols] = jnp.dot(dsb, k, preferred_element_type=F32).astype(dq_ref.dtype)
                dk_acc[0:n, cols] += _dot_tn(dsb, q)
                dc_ref[a, :, 0:n] += -jnp.sum(ds, axis=0, keepdims=True)

        _per_query_block(qi, nq, tq, block)

        @pl.when(qi == nq - 1)
        def _():
            dk_ref[...] = dk_acc[...].astype(dk_ref.dtype)
            dv_ref[...] = dv_acc[...].astype(dv_ref.dtype)

    D = H * HEAD_DIM
    return pl.pallas_call(
        body, name="fox_attn_bwd", grid=(G, nq),
        in_specs=[pl.BlockSpec((tq, hw), lambda g, i: (i, g)),
                  pl.BlockSpec((S, hw), lambda g, i: (0, G + g)),
                  pl.BlockSpec((S, hw), lambda g, i: (0, 2 * G + g)),
                  pl.BlockSpec((tq, H), lambda g, i: (i, 0)),
                  pl.BlockSpec((FOX_HEADS, 1, S), lambda g, i: (g, 0, 0)),
                  pl.BlockSpec((tq, hw), lambda g, i: (i, g))],
        out_specs=[pl.BlockSpec((tq, hw), lambda g, i: (i, g)),
                   pl.BlockSpec((S, hw), lambda g, i: (0, g)),
                   pl.BlockSpec((S, hw), lambda g, i: (0, g)),
                   pl.BlockSpec((FOX_HEADS, 1, S), lambda g, i: (g, 0, 0))],
        out_shape=[jax.ShapeDtypeStruct((S, D), BF16), jax.ShapeDtypeStruct((S, D), BF16),
                   jax.ShapeDtypeStruct((S, D), BF16), jax.ShapeDtypeStruct((H, 1, S), F32)],
        scratch_shapes=[pltpu.VMEM((S, hw), F32), pltpu.VMEM((S, hw), F32)],
        compiler_params=_cparams("parallel", "arbitrary"),
    )(qkv, qkv, qkv, c, crow, do)


def _cumsum_rows(name, xin, reverse):
    S, H = xin.shape
    tb = min(256, S)
    nb = S // tb

    def body(x_ref, o_ref):
        r = lax.broadcasted_iota(jnp.int32, (tb, tb), 0)
        cidx = lax.broadcasted_iota(jnp.int32, (tb, tb), 1)
        tri = (r <= cidx if reverse else r >= cidx).astype(F32)

        def step(b, carry):
            bb = nb - 1 - b if reverse else b
            rows = pl.ds(pl.multiple_of(bb * tb, tb), tb)
            blk = x_ref[rows, :]
            o_ref[rows, :] = jnp.dot(tri, blk, precision=HIGHEST, preferred_element_type=F32) + carry
            return carry + jnp.sum(blk, axis=0, keepdims=True)

        lax.fori_loop(0, nb, step, jnp.zeros((1, H), F32))

    return pl.pallas_call(
        body, name=name, out_shape=jax.ShapeDtypeStruct((S, H), F32),
        in_specs=[pl.BlockSpec(memory_space=pltpu.VMEM)], out_specs=pl.BlockSpec(memory_space=pltpu.VMEM),
        compiler_params=_cparams(),
    )(xin)


def _rel_onehot(i, transposed):
    shp = (REL_TABLE_PAD, BAND_KEYS) if transposed else (BAND_KEYS, REL_TABLE_PAD)
    j = lax.broadcasted_iota(jnp.int32, shp, 1 if transposed else 0)
    r = lax.broadcasted_iota(jnp.int32, shp, 0 if transposed else 1)
    return (jnp.clip(PAD_KEYS + i - j, -REL_CLIP, REL_CLIP) + REL_CLIP == r).astype(F32)


def _rel_expand(rb_pad):
    H = rb_pad.shape[0]

    def body(rb_ref, o_ref):
        rb = rb_ref[...]
        hi = rb.astype(BF16)
        rest = rb - hi.astype(F32)
        mid = rest.astype(BF16)
        lo = (rest - mid.astype(F32)).astype(BF16)

        def step(i, _):
            onehot = _rel_onehot(i, True).astype(BF16)
            pick = lambda part: jnp.dot(part, onehot, preferred_element_type=F32)
            o_ref[i] = (pick(hi) + pick(mid)) + pick(lo)
            return 0
        lax.fori_loop(0, CHUNK, step, 0)

    return pl.pallas_call(
        body, name="rel_bias_expand", out_shape=jax.ShapeDtypeStruct((CHUNK, H, BAND_KEYS), F32),
        in_specs=[pl.BlockSpec(memory_space=pltpu.VMEM)], out_specs=pl.BlockSpec(memory_space=pltpu.VMEM),
        compiler_params=_cparams(),
    )(rb_pad)


def _rel_reduce(dbt):
    H = dbt.shape[1]

    near0 = PAD_KEYS - REL_CLIP
    assert near0 % LANES == 0 and REL_WIN - near0 == REL_TABLE_PAD

    def body(d_ref, o_ref):
        j = near0 + lax.broadcasted_iota(jnp.int32, (REL_TABLE_PAD, REL_TABLE_PAD), 0)
        r = lax.broadcasted_iota(jnp.int32, (REL_TABLE_PAD, REL_TABLE_PAD), 1)

        def step(i, carry):
            acc, far = carry
            onehot = (jnp.clip(PAD_KEYS + i - j, -REL_CLIP, REL_CLIP) + REL_CLIP == r).astype(BF16)
            d = d_ref[i]
            near = d[:, near0:]
            hi = near.astype(BF16)
            lo = (near - hi.astype(F32)).astype(BF16)
            acc = acc + (jnp.dot(hi, onehot, preferred_element_type=F32) + jnp.dot(lo, onehot, preferred_element_type=F32))
            return acc, far + jnp.sum(d[:, :near0], axis=1, keepdims=True)

        acc, far = lax.fori_loop(0, REL_QB, step, (jnp.zeros((H, REL_TABLE_PAD), F32), jnp.zeros((H, 1), F32)))
        col = lax.broadcasted_iota(jnp.int32, (H, REL_TABLE_PAD), 1)
        o_ref[...] = acc + jnp.where(col == REL_TABLE - 1, far, 0.0)

    return pl.pallas_call(
        body, name="rel_bias_reduce", out_shape=jax.ShapeDtypeStruct((H, REL_TABLE_PAD), F32),
        in_specs=[pl.BlockSpec(memory_space=pltpu.VMEM)], out_specs=pl.BlockSpec(memory_space=pltpu.VMEM),
        compiler_params=_cparams(),
    )(dbt)


def _rel_window_bias(bias):
    H = bias.shape[0]
    out = jnp.full((H, REL_QB, REL_WIN), NEG_BIG, F32)
    for a in range(REL_QB // CHUNK):
        out = out.at[:, a * CHUNK:(a + 1) * CHUNK, a * CHUNK:a * CHUNK + BAND_KEYS].set(bias)
    return out


def _rel_probs(q, kw, bias_w, t0):
    s = _dot_nt(q, kw) * (HEAD_DIM ** -0.5) + bias_w
    j = lax.broadcasted_iota(jnp.int32, (REL_QB, REL_WIN), 1)
    s = jnp.where(j >= PAD_KEYS - t0, s, NEG_BIG)
    p = jnp.exp(s - jnp.max(s, axis=1, keepdims=True))
    return p * (1.0 / jnp.sum(p, axis=1, keepdims=True))


def _rel_fwd(qkv, bias_w, H):
    S = qkv.shape[0]

    def body(q_ref, k_ref, v_ref, b_ref, o_ref, kpad, vpad):
        kpad[0:PAD_KEYS, :] = jnp.zeros((PAD_KEYS, HEAD_DIM), BF16)
        vpad[0:PAD_KEYS, :] = jnp.zeros((PAD_KEYS, HEAD_DIM), BF16)
        kpad[PAD_KEYS:PAD_KEYS + S, :] = k_ref[...]
        vpad[PAD_KEYS:PAD_KEYS + S, :] = v_ref[...]

        def block(n, _):
            t0 = pl.multiple_of(n * REL_QB, REL_QB)
            rows, win = pl.ds(t0, REL_QB), pl.ds(t0, REL_WIN)
            p = _rel_probs(q_ref[rows, :], kpad[win, :], b_ref[...], t0)
            o_ref[rows, :] = jnp.dot(p.astype(BF16), vpad[win, :], preferred_element_type=F32).astype(o_ref.dtype)
            return 0

        lax.fori_loop(0, S // REL_QB, block, 0, unroll=2)

    return pl.pallas_call(
        body, name="rel_attn_fwd", grid=(H,),
        in_specs=[pl.BlockSpec((S, HEAD_DIM), lambda h: (0, h)),
                  pl.BlockSpec((S, HEAD_DIM), lambda h: (0, H + h)),
                  pl.BlockSpec((S, HEAD_DIM), lambda h: (0, 2 * H + h)),
                  pl.BlockSpec((None, REL_QB, REL_WIN), lambda h: (h, 0, 0))],
        out_specs=pl.BlockSpec((S, HEAD_DIM), lambda h: (0, h)),
        out_shape=jax.ShapeDtypeStruct((S, H * HEAD_DIM), BF16),
        scratch_shapes=[pltpu.VMEM((S + PAD_KEYS, HEAD_DIM), BF16), pltpu.VMEM((S + PAD_KEYS, HEAD_DIM), BF16)],
        compiler_params=_cparams("parallel"),
    )(qkv, qkv, qkv, bias_w)


def _rel_bwd(qkv, bias_w, do, H):
    S = qkv.shape[0]
    D = H * HEAD_DIM

    def body(q_ref, k_ref, v_ref, b_ref, do_ref, dq_ref, dk_ref, dv_ref, db_ref, kpad, vpad, dkpad, dvpad):
        kpad[0:PAD_KEYS, :] = jnp.zeros((PAD_KEYS, HEAD_DIM), BF16)
        vpad[0:PAD_KEYS, :] = jnp.zeros((PAD_KEYS, HEAD_DIM), BF16)
        kpad[PAD_KEYS:PAD_KEYS + S, :] = k_ref[...]
        vpad[PAD_KEYS:PAD_KEYS + S, :] = v_ref[...]
        dkpad[...] = jnp.zeros_like(dkpad)
        dvpad[...] = jnp.zeros_like(dvpad)
        db_ref[...] = jnp.zeros_like(db_ref)

        def block(n, _):
            t0 = pl.multiple_of(n * REL_QB, REL_QB)
            rows, win = pl.ds(t0, REL_QB), pl.ds(t0, REL_WIN)
            q, kw, vw, do_ = q_ref[rows, :], kpad[win, :], vpad[win, :], do_ref[rows, :]
            p = _rel_probs(q, kw, b_ref[...], t0)
            dvpad[win, :] += _dot_tn(p.astype(BF16), do_)
            dp = _dot_nt(do_, vw)
            ds = p * (dp - jnp.sum(p * dp, axis=1, keepdims=True))
            db_ref[...] += ds
            dsb = (ds * (HEAD_DIM ** -0.5)).astype(BF16)
            dq_ref[rows, :] = jnp.dot(dsb, kw, preferred_element_type=F32).astype(dq_ref.dtype)
            dkpad[win, :] += _dot_tn(dsb, q)
            return 0

        lax.fori_loop(0, S // REL_QB, block, 0, unroll=2)
        dk_ref[...] = dkpad[PAD_KEYS:PAD_KEYS + S, :].astype(dk_ref.dtype)
        dv_ref[...] = dvpad[PAD_KEYS:PAD_KEYS + S, :].astype(dv_ref.dtype)

    head = lambda h: (0, h)
    return pl.pallas_call(
        body, name="rel_attn_bwd", grid=(H,),
        in_specs=[pl.BlockSpec((S, HEAD_DIM), head),
                  pl.BlockSpec((S, HEAD_DIM), lambda h: (0, H + h)),
                  pl.BlockSpec((S, HEAD_DIM), lambda h: (0, 2 * H + h)),
                  pl.BlockSpec((None, REL_QB, REL_WIN), lambda h: (h, 0, 0)),
                  pl.BlockSpec((S, HEAD_DIM), head)],
        out_specs=[pl.BlockSpec((S, HEAD_DIM), head), pl.BlockSpec((S, HEAD_DIM), head), pl.BlockSpec((S, HEAD_DIM), head),
                   pl.BlockSpec((None, REL_QB, REL_WIN), lambda h: (h, 0, 0))],
        out_shape=[jax.ShapeDtypeStruct((S, D), BF16), jax.ShapeDtypeStruct((S, D), BF16), jax.ShapeDtypeStruct((S, D), BF16),
                   jax.ShapeDtypeStruct((H, REL_QB, REL_WIN), F32)],
        scratch_shapes=[pltpu.VMEM((S + PAD_KEYS, HEAD_DIM), BF16), pltpu.VMEM((S + PAD_KEYS, HEAD_DIM), BF16),
                        pltpu.VMEM((S + PAD_KEYS, HEAD_DIM), F32), pltpu.VMEM((S + PAD_KEYS, HEAD_DIM), F32)],
        compiler_params=_cparams("parallel"),
    )(qkv, qkv, qkv, bias_w, do)


def _conv_taps(win, tt, reverse):
    n = tt + 2 * CONV_HALO
    for k in range(CONV_K):
        off = (CONV_K - 1 - k) if reverse else (k - (CONV_K - 1))
        sh = (-off) % n
        rolled = pltpu.roll(win, sh, 0) if sh else win
        yield k, rolled[CONV_HALO:CONV_HALO + tt, :]


def _fill_padded(pad_ref, x_ref, S):
    tc = pad_ref.shape[1]
    pad_ref[0:CONV_HALO, :] = jnp.zeros((CONV_HALO, tc), F32)
    pad_ref[CONV_HALO + S:CONV_HALO + S + CONV_HALO, :] = jnp.zeros((CONV_HALO, tc), F32)
    pad_ref[CONV_HALO:CONV_HALO + S, :] = x_ref[...]


def _dwconv(name, xin, w32, bias, reverse):
    S, D = xin.shape
    tc = min(256, D)
    tt = min(256, S)

    def body(x_ref, w_ref, b_ref, y_ref, pad_ref):
        _fill_padded(pad_ref, x_ref, S)
        def tile(ti, _):
            t0 = pl.multiple_of(ti * tt, tt)
            win = pad_ref[pl.ds(t0, tt + 2 * CONV_HALO), :]
            acc = jnp.zeros((tt, tc), F32) + b_ref[...]
            for k, shifted in _conv_taps(win, tt, reverse):
                acc = acc + w_ref[pl.ds(k, 1), :] * shifted
            y_ref[pl.ds(t0, tt), :] = acc
            return 0

        lax.fori_loop(0, S // tt, tile, 0)

    return pl.pallas_call(
        body, name=name, grid=(D // tc,),
        in_specs=[pl.BlockSpec((S, tc), lambda i: (0, i)), pl.BlockSpec((CONV_HALO, tc), lambda i: (0, i)),
                  pl.BlockSpec((1, tc), lambda i: (0, i))],
        out_specs=pl.BlockSpec((S, tc), lambda i: (0, i)),
        out_shape=jax.ShapeDtypeStruct((S, D), F32),
        scratch_shapes=[pltpu.VMEM((S + 2 * CONV_HALO, tc), F32)],
        compiler_params=_cparams("parallel"),
    )(xin, w32, bias)


def _dwconv_dw(xin, dy):
    S, D = xin.shape
    tc = min(256, D)
    tt = min(256, S)

    def body(x_ref, dy_ref, o_ref, pad_ref):
        _fill_padded(pad_ref, x_ref, S)

        def tile(ti, acc):
            t0 = pl.multiple_of(ti * tt, tt)
            win = pad_ref[pl.ds(t0, tt + 2 * CONV_HALO), :]
            dyt = dy_ref[pl.ds(t0, tt), :]
            ridx = lax.broadcasted_iota(jnp.int32, (CONV_HALO, tc), 0)
            upd = jnp.zeros((CONV_HALO, tc), F32)
            for k, shifted in _conv_taps(win, tt, False):
                upd = jnp.where(ridx == k, jnp.sum(dyt * shifted, axis=0, keepdims=True), upd)
            return acc + upd

        o_ref[...] = lax.fori_loop(0, S // tt, tile, jnp.zeros((CONV_HALO, tc), F32))

    return pl.pallas_call(
        body, name="dwconv_dw", grid=(D // tc,),
        in_specs=[pl.BlockSpec((S, tc), lambda i: (0, i)), pl.BlockSpec((S, tc), lambda i: (0, i))],
        out_specs=pl.BlockSpec((CONV_HALO, tc), lambda i: (0, i)),
        out_shape=jax.ShapeDtypeStruct((CONV_HALO, D), F32),
        scratch_shapes=[pltpu.VMEM((S + 2 * CONV_HALO, tc), F32)],
        compiler_params=_cparams("parallel"),
    )(xin, dy)


def _place():
    x, y, c = lax.axis_index("x"), lax.axis_index("y"), lax.axis_index("c")
    chips = [(1 - x, y), (x, 1 - y), (1 - x, 1 - y)]
    return x, y, c, chips


def _remote(src, dst, ssem, rsem, dev):
    return pltpu.make_async_remote_copy(src_ref=src, dst_ref=dst, send_sem=ssem, recv_sem=rsem,
                                        device_id=dev, device_id_type=MESH_IDS)


_ANY = pl.BlockSpec(memory_space=pl.ANY)


def _place_own(name, own, place):
    R, W = own.shape
    tr = _pick_rows(R)

    def body(p_ref, a_ref, o_ref):
        del p_ref
        o_ref[...] = a_ref[...]

    return pl.pallas_call(
        body, name=name,
        grid_spec=pltpu.PrefetchScalarGridSpec(
            num_scalar_prefetch=1, grid=(R // tr,),
            in_specs=[pl.BlockSpec((tr, W), lambda i, p: (i, 0))],
            out_specs=pl.BlockSpec((None, tr, W), lambda i, p: (p[1], i, 0))),
        out_shape=jax.ShapeDtypeStruct((N_CHIPS, R, W), own.dtype),
        compiler_params=_cparams("parallel"),
    )(place, own)


_HBM = pl.BlockSpec(memory_space=pltpu.HBM)
_SEM = pl.BlockSpec(memory_space=pltpu.SEMAPHORE)
GROUPS = ('qkv', 'ffn', 'pw1', 'dm')


def _half_rows(c, r0, n):
    return pl.ds(pl.multiple_of(r0 + c * (n // 2), SUBLANES), n // 2)


def _gather_start(name, wgs, layers, after):
    G, L = len(wgs), len(layers)

    def body(*refs):
        outs = refs[G + 1:]
        ssems, rsems, bufs = outs[:L], outs[L:2 * L], outs[2 * L:]
        x, y, c, chips = _place()
        me = 2 * x + y
        for li, pieces in enumerate(layers):
            for pi, (g, r0, n) in enumerate(pieces):
                blk = bufs[g].at[me, _half_rows(c, r0, n)]
                for j, (px, py) in enumerate(chips):
                    _remote(blk, blk, ssems[li].at[3 * pi + j], rsems[li].at[3 * pi + j], (px, py, c)).start()

    sem_shapes = [pltpu.SemaphoreType.DMA((3 * len(p),)) for p in layers]
    res = pl.pallas_call(
        body, name=name, in_specs=[_HBM] * G + [_ANY],
        out_specs=[_SEM] * (2 * L) + [_HBM] * G,
        out_shape=sem_shapes + sem_shapes + [pltpu.HBM(w.shape, w.dtype) for w in wgs],
        input_output_aliases={g: 2 * L + g for g in range(G)},
        compiler_params=pltpu.CompilerParams(has_side_effects=pltpu.SideEffectType.DATAFLOW_SIDE_EFFECTING),
    )(*[pltpu.with_memory_space_constraint(w, pltpu.HBM) for w in wgs], after)
    return res[:L], res[L:2 * L], list(res[2 * L:])


def _gather_wait(name, wgs, ssem, rsem, pieces, after):
    G = len(wgs)

    def body(*refs):
        ssem_ref, rsem_ref = refs[G], refs[G + 1]
        bufs = refs[G + 3:]
        x, y, c, chips = _place()
        me = 2 * x + y
        for pi, (g, r0, n) in enumerate(pieces):
            rows = _half_rows(c, r0, n)
            for j, (px, py) in enumerate(chips):
                cp = _remote(bufs[g].at[me, rows], bufs[g].at[2 * px + py, rows],
                             ssem_ref.at[3 * pi + j], rsem_ref.at[3 * pi + j], (px, py, c))
                cp.wait_send()
                cp.wait_recv()

    return list(pl.pallas_call(
        body, name=name, in_specs=[_HBM] * G + [_SEM, _SEM, _ANY], out_specs=[_HBM] * G,
        out_shape=[pltpu.HBM(w.shape, w.dtype) for w in wgs],
        input_output_aliases={g: g for g in range(G)},
        compiler_params=pltpu.CompilerParams(has_side_effects=pltpu.SideEffectType.DATAFLOW_SIDE_EFFECTING),
    )(*wgs, ssem, rsem, after))


def _gather_forward(name, wgs, pieces):
    G = len(wgs)
    n_cp = 3 * len(pieces)

    def body(*refs):
        bufs, ssems, rsems = refs[G:2 * G], refs[2 * G], refs[2 * G + 1]
        x, y, c, chips = _place()
        sib = (x, y, 1 - c)
        cps = []
        for pi, (g, r0, n) in enumerate(pieces):
            for j, (px, py) in enumerate(chips):
                blk = bufs[g].at[2 * px + py, _half_rows(c, r0, n)]
                cps.append(_remote(blk, blk, ssems.at[3 * pi + j], rsems.at[3 * pi + j], sib))
        for cp in cps:
            cp.start()
        for pi, (g, r0, n) in enumerate(pieces):
            for j, (px, py) in enumerate(chips):
                blk = bufs[g].at[2 * px + py, _half_rows(1 - c, r0, n)]
                _remote(blk, blk, ssems.at[3 * pi + j], rsems.at[3 * pi + j], sib).wait_recv()
        for cp in cps:
            cp.wait_send()

    return list(pl.pallas_call(
        body, name=name, in_specs=[_ANY] * G, out_specs=[_ANY] * G,
        out_shape=[jax.ShapeDtypeStruct(w.shape, w.dtype) for w in wgs],
        input_output_aliases={g: g for g in range(G)},
        scratch_shapes=[pltpu.SemaphoreType.DMA((n_cp,)), pltpu.SemaphoreType.DMA((n_cp,))],
        compiler_params=pltpu.CompilerParams(has_side_effects=True),
    )(*wgs))


def _swap_halves(name, dgs, pieces):
    G = len(dgs)
    n_cp = N_CHIPS * len(pieces)

    def body(*refs):
        srcs, lands, ssems, rsems = refs[:G], refs[G:G + len(pieces)], refs[-2], refs[-1]
        x, y, c, _ = _place()
        cps = [_remote(srcs[g].at[j, _half_rows(1 - c, r0, n)], lands[pi].at[j],
                       ssems.at[N_CHIPS * pi + j], rsems.at[N_CHIPS * pi + j], (x, y, 1 - c))
               for pi, (g, r0, n) in enumerate(pieces) for j in range(N_CHIPS)]
        for cp in cps:
            cp.start()
        for cp in cps:
            cp.wait()

    return list(pl.pallas_call(
        body, name=name, in_specs=[_ANY] * G, out_specs=[_ANY] * len(pieces),
        out_shape=[jax.ShapeDtypeStruct((N_CHIPS, n // 2, dgs[g].shape[2]), dgs[g].dtype) for g, _, n in pieces],
        scratch_shapes=[pltpu.SemaphoreType.DMA((n_cp,)), pltpu.SemaphoreType.DMA((n_cp,))],
        compiler_params=pltpu.CompilerParams(has_side_effects=True),
    )(*dgs))


def _scatter_start(name, pbs):
    P = len(pbs)

    def body(*refs):
        outs = refs[2 * P:]
        ssems, rsems, src, land = outs[0], outs[1], outs[2:2 + P], outs[2 + P:]
        x, y, c, chips = _place()
        me = 2 * x + y
        for pi in range(P):
            for j, (px, py) in enumerate(chips):
                _remote(src[pi].at[2 * px + py], land[pi].at[me], ssems.at[3 * pi + j], rsems.at[3 * pi + j], (px, py, c)).start()

    sems = pltpu.SemaphoreType.DMA((3 * P,))
    hbm = [pltpu.HBM(p.shape, p.dtype) for p in pbs]
    res = pl.pallas_call(
        body, name=name, in_specs=[_HBM] * (2 * P), out_specs=[_SEM, _SEM] + [_HBM] * (2 * P),
        out_shape=[sems, sems] + hbm + hbm,
        input_output_aliases={k: 2 + k for k in range(2 * P)},
        compiler_params=pltpu.CompilerParams(has_side_effects=pltpu.SideEffectType.DATAFLOW_SIDE_EFFECTING),
    )(*[pltpu.with_memory_space_constraint(p, pltpu.HBM) for p in pbs],
      *[pltpu.with_memory_space_constraint(lax.empty(p.shape, p.dtype), pltpu.HBM) for p in pbs])
    return res[0], res[1], list(res[2:2 + P]), list(res[2 + P:])


def _scatter_wait(name, ssem, rsem, pbs, lands, after):
    P = len(pbs)

    def body(*refs):
        ssems, rsems = refs[2 * P], refs[2 * P + 1]
        outs = refs[2 * P + 3:]
        src, land = outs[:P], outs[P:]
        x, y, c, chips = _place()
        for pi in range(P):
            for j, (px, py) in enumerate(chips):
                cp = _remote(src[pi].at[2 * px + py], land[pi].at[2 * px + py], ssems.at[3 * pi + j], rsems.at[3 * pi + j], (px, py, c))
                cp.wait_send()
                cp.wait_recv()

    hbm = [pltpu.HBM(p.shape, p.dtype) for p in pbs]
    res = pl.pallas_call(
        body, name=name, in_specs=[_HBM] * (2 * P) + [_SEM, _SEM, _ANY], out_specs=[_HBM] * (2 * P),
        out_shape=hbm + hbm, input_output_aliases={k: k for k in range(2 * P)},
        compiler_params=pltpu.CompilerParams(has_side_effects=pltpu.SideEffectType.DATAFLOW_SIDE_EFFECTING),
    )(*pbs, *lands, ssem, rsem, after)
    return list(res[:P]), list(res[P:])


def _share_halves(name, gfs, pieces):
    G = len(gfs)

    def body(*refs):
        outs, ssems, rsems = refs[G:2 * G], refs[2 * G], refs[2 * G + 1]
        x, y, c, _ = _place()
        cps = []
        for pi, (g, r0, n) in enumerate(pieces):
            mine = outs[g].at[_half_rows(c, r0, n)]
            cps.append(_remote(mine, mine, ssems.at[pi], rsems.at[pi], (x, y, 1 - c)))
        for cp in cps:
            cp.start()
        for pi, (g, r0, n) in enumerate(pieces):
            theirs = outs[g].at[_half_rows(1 - c, r0, n)]
            _remote(theirs, theirs, ssems.at[pi], rsems.at[pi], (x, y, 1 - c)).wait_recv()
        for cp in cps:
            cp.wait_send()

    return list(pl.pallas_call(
        body, name=name, in_specs=[_ANY] * G, out_specs=[_ANY] * G,
        out_shape=[jax.ShapeDtypeStruct(gf.shape, gf.dtype) for gf in gfs], input_output_aliases={g: g for g in range(G)},
        scratch_shapes=[pltpu.SemaphoreType.DMA((len(pieces),)), pltpu.SemaphoreType.DMA((len(pieces),))],
        compiler_params=pltpu.CompilerParams(has_side_effects=True),
    )(*gfs))


def _broadcast_small(name, buf, after=None):
    R = buf.shape[0]

    def body(src, *rest):
        out, ssems, rsems = rest[-3:]
        x, y, c, _ = _place()
        me = 4 * x + 2 * y + c
        out[me] = src[...]
        peers = []
        for mask in range(1, 8):
            fx, fy, fc = (mask >> 2) & 1, (mask >> 1) & 1, mask & 1
            peers.append((1 - x if fx else x, 1 - y if fy else y, 1 - c if fc else c))
        cps = [_remote(src, out.at[me], ssems.at[k], rsems.at[k], p) for k, p in enumerate(peers)]
        for cp in cps:
            cp.start()
        for k, (px, py, pc) in enumerate(peers):
            blk = out.at[4 * px + 2 * py + pc]
            _remote(blk, blk, ssems.at[k], rsems.at[k], (px, py, pc)).wait_recv()
        for cp in cps:
            cp.wait_send()

    return pl.pallas_call(
        body, name=name, in_specs=[pl.BlockSpec(memory_space=pltpu.VMEM)] + ([_ANY] if after is not None else []),
        out_specs=pl.BlockSpec(memory_space=pltpu.VMEM),
        out_shape=jax.ShapeDtypeStruct((8, R, LANES), F32),
        scratch_shapes=[pltpu.SemaphoreType.DMA((7,)), pltpu.SemaphoreType.DMA((7,))],
        compiler_params=pltpu.CompilerParams(has_side_effects=True, vmem_limit_bytes=VMEM_LIMIT_V7X),
    )(buf, *([after] if after is not None else []))


def _sum_slabs(name, slabs):
    n, R, _ = slabs.shape

    def body(s_ref, o_ref):
        acc = s_ref[0]
        for k in range(1, n):
            acc = acc + s_ref[k]
        o_ref[...] = acc

    return pl.pallas_call(
        body, name=name, out_shape=jax.ShapeDtypeStruct((R, LANES), F32),
        in_specs=[pl.BlockSpec(memory_space=pltpu.VMEM)], out_specs=pl.BlockSpec(memory_space=pltpu.VMEM),
        compiler_params=_cparams(),
    )(slabs)


def _pick_rows(rows, target=512):
    best = SUBLANES
    for t in range(SUBLANES, min(rows, target) + 1, SUBLANES):
        if rows % t == 0:
            best = t
    return best


def _half_tile(r0, n):
    return _pick_rows(math.gcd(r0, n // 2) if r0 else n // 2)


def _pair_sum(name, dg, land, place, r0, n):
    W = dg.shape[2]
    tr = _half_tile(r0, n)
    nb = (n // 2) // tr

    def body(p_ref, a_ref, b_ref, o_ref):
        del p_ref
        o_ref[...] = (a_ref[...].astype(F32) + b_ref[...].astype(F32)).astype(o_ref.dtype)

    return pl.pallas_call(
        body, name=name,
        grid_spec=pltpu.PrefetchScalarGridSpec(
            num_scalar_prefetch=1, grid=(N_CHIPS, nb),
            in_specs=[pl.BlockSpec((None, tr, W), lambda j, i, p: (j, r0 // tr + p[0] * nb + i, 0)),
                      pl.BlockSpec((None, tr, W), lambda j, i, p: (j, i, 0))],
            out_specs=pl.BlockSpec((None, tr, W), lambda j, i, p: (j, i, 0))),
        out_shape=jax.ShapeDtypeStruct((N_CHIPS, n // 2, W), BF16),
        compiler_params=_cparams("parallel", "parallel"),
    )(place, dg, land)


def _chip_sum(name, pb, land, place, gf, r0, n):
    W = gf.shape[1]
    tr = _half_tile(r0, n)
    nb = (n // 2) // tr

    def body(p_ref, own_ref, lx_ref, ly_ref, ld_ref, gf_in, o_ref):
        del p_ref, gf_in
        o_ref[...] = ((own_ref[...].astype(F32) + lx_ref[...].astype(F32)) + ly_ref[...].astype(F32)) + ld_ref[...].astype(F32)

    slab = lambda flip: pl.BlockSpec((None, tr, W), lambda i, p, _f=flip: (p[1] ^ _f, i, 0))
    return pl.pallas_call(
        body, name=name,
        grid_spec=pltpu.PrefetchScalarGridSpec(
            num_scalar_prefetch=1, grid=(nb,),
            in_specs=[slab(0), slab(2), slab(1), slab(3), pl.BlockSpec(memory_space=pl.ANY)],
            out_specs=pl.BlockSpec((tr, W), lambda i, p: (r0 // tr + p[0] * nb + i, 0))),
        out_shape=jax.ShapeDtypeStruct(gf.shape, F32),
        input_output_aliases={5: 0},
        compiler_params=_cparams("parallel"),
    )(place, pb, land, land, land, gf)


def _ln_stats(z):
    mu = jnp.mean(z, axis=1, keepdims=True)
    zc = z - mu
    rstd = lax.rsqrt(jnp.mean(zc * zc, axis=1, keepdims=True) + LN_EPS)
    return zc * rstd, rstd


def _ln_fwd(name, xin, m, g, b):
    S, D = xin.shape

    def fn(x_, m_, g_, b_):
        xhat, rstd = _ln_stats(ALPHA * x_ + m_)
        y = xhat * g_ + b_
        return y, y, xhat, rstd

    return _rowwise(name, fn, [xin, m, ('full', g), ('full', b)],
                    [('rows', D, F32), ('rows', D, BF16), ('rows', D, F32), ('rows', 1, F32)], S)


def _ln_bwd_core(dy, xhat, rstd, g):
    dxh = dy * g
    return rstd * (dxh - jnp.mean(dxh, axis=1, keepdims=True) - xhat * jnp.mean(dxh * xhat, axis=1, keepdims=True))


def _ln_bwd(name, terms, xhat, rstd, g, after=None):
    S, D = xhat.shape
    scales = [s for _, s in terms]
    n = len(terms)

    def fn(*v):
        dy = v[0] * scales[0] if scales[0] != 1.0 else v[0]
        for t in range(1, n):
            dy = dy + (v[t] * scales[t] if scales[t] != 1.0 else v[t])
        xh, rs, g_ = v[n], v[n + 1], v[n + 2]
        dz = _ln_bwd_core(dy, xh, rs, g_)
        return dz, dz, jnp.sum(dy * xh, axis=0, keepdims=True), jnp.sum(dy, axis=0, keepdims=True)

    return _rowwise(name, fn, [a for a, _ in terms] + [xhat, rstd, ('full', g)] + ([('after', after)] if after is not None else []),
                    [('rows', D, F32), ('rows', D, BF16), ('acc', (1, D), F32), ('acc', (1, D), F32)], S)


def _adamw_math(w, g, m, v):
    m2 = ADAM_B1 * m + (1.0 - ADAM_B1) * g
    v2 = ADAM_B2 * v + (1.0 - ADAM_B2) * (g * g)
    m_hat = m2 / (1.0 - ADAM_B1 ** ADAM_STEP)
    v_hat = v2 / (1.0 - ADAM_B2 ** ADAM_STEP)
    delta = -ADAM_LR * (m_hat / (jnp.sqrt(v_hat) + ADAM_EPS) + ADAM_WD * w)
    return delta, m2, v2


def _adamw(name, w, gfull, row_start, m, v):
    rows, W = w.shape
    tr = math.gcd(math.gcd(rows, row_start), 256) if row_start else math.gcd(rows, 256)

    def fn(w_, g_, m_, v_):
        d, m2, v2 = _adamw_math(w_, g_, m_, v_)
        return g_, d, m2, v2

    return _rowwise(name, fn, [w, ('off', gfull, row_start // tr), m, v], [('rows', W, F32)] * 4, rows, tm=tr)


def _pack(arrs):
    flat = jnp.concatenate([a.reshape(-1).astype(F32) for a in arrs])
    tile = SUBLANES * LANES
    n = -(-flat.shape[0] // tile) * tile
    return jnp.pad(flat, (0, n - flat.shape[0])).reshape(-1, LANES)


def _unpack(buf, shapes):
    flat = buf.reshape(-1)
    out, pos = [], 0
    for shp in shapes:
        n = math.prod(shp)
        out.append(flat[pos:pos + n].reshape(shp))
        pos += n
    return out


BIG = ['fox_w_qkv', 'fox_w_o', 'rel_w_qkv', 'rel_w_o', 'conv_w_pw1', 'conv_w_pw2', 'ffn_w_gate', 'ffn_w_up', 'ffn_w_down']
SMALL_SHARDED = ['fox_w_f', 'conv_b_pw1', 'conv_w_dw', 'conv_b_dw', 'conv_ln_g', 'conv_ln_b', 'conv_b_pw2']
SMALL_SHARD_AXIS = {'fox_w_f': 1, 'conv_b_pw1': 1, 'conv_w_dw': 2, 'conv_b_dw': 1, 'conv_ln_g': 1, 'conv_ln_b': 1, 'conv_b_pw2': 1}
SMALL_REPL = ['fox_b_f', 'rel_bias', 'ln_mix_g', 'ln_mix_b', 'ln_ffn_g', 'ln_ffn_b']
SMALL = SMALL_SHARDED + SMALL_REPL
WEIGHTS = ['fox_w_qkv', 'fox_w_f', 'fox_b_f', 'fox_w_o', 'rel_w_qkv', 'rel_bias', 'rel_w_o', 'conv_w_pw1', 'conv_b_pw1',
           'conv_w_dw', 'conv_b_dw', 'conv_ln_g', 'conv_ln_b', 'conv_w_pw2', 'conv_b_pw2', 'ffn_w_gate', 'ffn_w_up',
           'ffn_w_down', 'ln_mix_g', 'ln_mix_b', 'ln_ffn_g', 'ln_ffn_b']


def kernel(x, fox_w_qkv, fox_w_f, fox_b_f, fox_w_o, rel_w_qkv, rel_bias, rel_w_o, conv_w_pw1, conv_b_pw1, conv_w_dw, conv_b_dw, conv_ln_g, conv_ln_b, conv_w_pw2, conv_b_pw2, ffn_w_gate, ffn_w_up, ffn_w_down, ln_mix_g, ln_mix_b, ln_ffn_g, ln_ffn_b, loss_target, m_fox_w_qkv, m_fox_w_f, m_fox_b_f, m_fox_w_o, m_rel_w_qkv, m_rel_bias, m_rel_w_o, m_conv_w_pw1, m_conv_b_pw1, m_conv_w_dw, m_conv_b_dw, m_conv_ln_g, m_conv_ln_b, m_conv_w_pw2, m_conv_b_pw2, m_ffn_w_gate, m_ffn_w_up, m_ffn_w_down, m_ln_mix_g, m_ln_mix_b, m_ln_ffn_g, m_ln_ffn_b, v_fox_w_qkv, v_fox_w_f, v_fox_b_f, v_fox_w_o, v_rel_w_qkv, v_rel_bias, v_rel_w_o, v_conv_w_pw1, v_conv_b_pw1, v_conv_w_dw, v_conv_b_dw, v_conv_ln_g, v_conv_ln_b, v_conv_w_pw2, v_conv_b_pw2, v_ffn_w_gate, v_ffn_w_up, v_ffn_w_down, v_ln_mix_g, v_ln_mix_b, v_ln_ffn_g, v_ln_ffn_b):
    A = dict(locals())
    Wt = {n: A[n] for n in WEIGHTS}
    Mo = {n: A['m_' + n] for n in WEIGHTS}
    Vo = {n: A['v_' + n] for n in WEIGHTS}

    _, S, D = x.shape
    H = D // HEAD_DIM
    Ds = D // N_CHIPS
    Nq = fox_w_qkv.shape[2]
    Np = conv_w_pw1.shape[2]
    Fs = ffn_w_gate.shape[2]
    my_x, my_y, my_c = lax.axis_index("x"), lax.axis_index("y"), lax.axis_index("c")
    my_chip = 2 * my_x + my_y
    place = jnp.stack([my_c, my_chip]).astype(jnp.int32)

    wo_base = DEPTH * Fs
    where = {
        'fox_w_qkv': ('qkv', 0), 'rel_w_qkv': ('qkv', N_FOX * D),
        'ffn_w_gate': ('ffn', 0), 'ffn_w_up': ('ffn', DEPTH * D),
        'conv_w_pw1': ('pw1', 0),
        'ffn_w_down': ('dm', 0), 'fox_w_o': ('dm', wo_base), 'rel_w_o': ('dm', wo_base + N_FOX * Ds),
        'conv_w_pw2': ('dm', wo_base + (N_FOX + 1) * Ds),
    }
    members = {'qkv': ['fox_w_qkv', 'rel_w_qkv'], 'ffn': ['ffn_w_gate', 'ffn_w_up'], 'pw1': ['conv_w_pw1'],
               'dm': ['ffn_w_down', 'fox_w_o', 'rel_w_o', 'conv_w_pw2']}
    flat2 = lambda a: a.reshape(-1, a.shape[-1])
    own = {g: jnp.concatenate([flat2(Wt[n]).astype(BF16) for n in ms], axis=0) for g, ms in members.items()}

    def layer_pieces(i):
        kind, j = i % 3, i // 3
        slot = j if kind == 0 else (N_FOX if kind == 1 else N_FOX + 1)
        w_in = (GROUPS.index('pw1'), 0, D) if kind == 2 else (GROUPS.index('qkv'), slot * D, D)
        return [w_in, (GROUPS.index('dm'), wo_base + slot * Ds, Ds), (GROUPS.index('ffn'), i * D, D),
                (GROUPS.index('ffn'), (DEPTH + i) * D, D), (GROUPS.index('dm'), i * Fs, Fs)]

    small_shapes = [Wt[n].shape for n in SMALL_SHARDED]
    slabs = _broadcast_small("gather_small", _pack([Wt[n] for n in SMALL_SHARDED]))

    stages = [part for i in range(DEPTH) for part in (layer_pieces(i)[:2], layer_pieces(i)[2:])]
    gstages = [part for i in range(DEPTH) for part in (layer_pieces(i)[:1], layer_pieces(i)[1:])]
    (g_first, r_first, n_first), = gstages[0]
    buf_first = _place_own("place_" + GROUPS[g_first], own[GROUPS[g_first]], place)
    ssems_first, rsems_first, (buf_first,) = _gather_start("gather_start_first", [buf_first], [[(0, r_first, n_first)]], slabs)
    ssems_rest, rsems_rest, wg_list = _gather_start(
        "gather_start_rest", [buf_first if gi == g_first else _place_own("place_" + g, own[g], place) for gi, g in enumerate(GROUPS)],
        gstages[1:], slabs)
    gather_ssems, gather_rsems = list(ssems_first) + list(ssems_rest), list(rsems_first) + list(rsems_rest)
    WG = dict(zip(GROUPS, wg_list))
    DG = {g: lax.empty(WG[g].shape, BF16) for g in own}

    per_chip = [_unpack(slabs[2 * j], small_shapes) for j in range(N_CHIPS)]
    full = {n: jnp.concatenate([per_chip[j][i] for j in range(N_CHIPS)], axis=SMALL_SHARD_AXIS[n])
            for i, n in enumerate(SMALL_SHARDED)}
    row = lambda v: v.reshape(1, -1)

    SG = {}

    def ffn_fwd(i, xb):
        hg, hu, act = _mm_gate_up(f"ffn{i}_gate_up", xb, WG['ffn'], i * D, (DEPTH + i) * D)
        f = _mm_row(f"ffn{i}_down", act, WG['dm'], i * Fs, Fs)
        return f, (hg, hu, act)

    def ffn_bwd(i, xb, saved, dzb):
        hg, hu, act = saved
        DG['dm'] = _mm_dw(f"ffn{i}_dw_down", act, dzb, DG['dm'], i * Fs, 'row')
        dhg, dhu = _mm_dact(f"ffn{i}_dact", dzb, WG['dm'], i * Fs, Fs, hg, hu)
        DG['ffn'] = _mm_dw(f"ffn{i}_dw_gate", xb, dhg, DG['ffn'], i * D, 'col')
        DG['ffn'] = _mm_dw(f"ffn{i}_dw_up", xb, dhu, DG['ffn'], (DEPTH + i) * D, 'col')
        return _mm_col_t(f"ffn{i}_dx", [(dhg, i * D), (dhu, (DEPTH + i) * D)], WG['ffn'], D)

    def fox_fwd(j, xb, rest_stage):
        qkv = _mm_col(f"fox{j}_qkv", xb, WG['qkv'], j * D)
        wf = full['fox_w_f'][j].astype(BF16)
        def gate_fn(x_, w_, b_):
            z_ = jnp.dot(x_, w_, preferred_element_type=F32) + b_
            return z_, jnp.minimum(z_, 0.0) - jnp.log(1.0 + jnp.exp(-jnp.abs(z_)))

        z, logf = _rowwise(f"fox{j}_gate", gate_fn, [xb, ('full', wf), ('full', row(fox_b_f[j]))],
                           [('rows', H, F32), ('rows', H, F32)], S)
        c = _cumsum_rows(f"fox{j}_cumsum", logf, False)
        crow = c.T.reshape(H, 1, S)
        o = _fox_fwd(qkv, c, crow, H)
        weights_ready(rest_stage, o)
        m = _mm_row(f"fox{j}_wo", o, WG['dm'], wo_base + j * Ds, Ds)
        return m, (qkv, z, c, crow, o, wf)

    def fox_bwd(j, xb, saved, dzb):
        qkv, z, c, crow, o, wf = saved
        DG['dm'] = _mm_dw(f"fox{j}_dw_o", o, dzb, DG['dm'], wo_base + j * Ds, 'row')
        do = _mm_row_t(f"fox{j}_do", dzb, WG['dm'], wo_base + j * Ds, Ds, BF16)
        dq, dk, dv, dcrow = _fox_bwd(qkv, c, crow, do, H)
        dqkv = jnp.concatenate([dq, dk, dv], axis=1)
        dlogf = _cumsum_rows(f"fox{j}_rcumsum", dcrow.reshape(H, S).T, True)

        def fn(x_, dl_, z_, w_):
            dz_ = dl_ * _sigmoid(-z_)
            dzb_ = dz_.astype(BF16)
            return _dot_nt(dzb_, w_), _dot_tn(x_, dzb_), jnp.sum(dz_, axis=0, keepdims=True)

        dh_f, dwf, dbf = _rowwise(f"fox{j}_gate_bwd", fn, [xb, dlogf, z, ('full', wf)],
                                  [('rows', D, F32), ('acc', (D, H), F32), ('acc', (1, H), F32)], S)
        SG.setdefault('fox_w_f', [None] * N_FOX)[j] = dwf
        SG.setdefault('fox_b_f', [None] * N_FOX)[j] = dbf.reshape(H)
        DG['qkv'] = _mm_dw(f"fox{j}_dw_qkv", xb, dqkv, DG['qkv'], j * D, 'col')
        dh = _mm_col_t(f"fox{j}_dx", [(dqkv, j * D)], WG['qkv'], D)
        return [dh, dh_f]

    def rel_fwd(xb, rest_stage):
        qkv = _mm_col("rel_qkv", xb, WG['qkv'], N_FOX * D)
        rb_pad = jnp.pad(rel_bias[0], ((0, 0), (0, REL_TABLE_PAD - REL_TABLE)))
        bias = _rel_window_bias(jnp.transpose(_rel_expand(rb_pad), (1, 0, 2)))
        o = _rel_fwd(qkv, bias, H)
        weights_ready(rest_stage, o)
        m = _mm_row("rel_wo", o, WG['dm'], wo_base + N_FOX * Ds, Ds)
        return m, (qkv, bias, o)

    def rel_bwd(xb, saved, dzb):
        qkv, bias, o = saved
        DG['dm'] = _mm_dw("rel_dw_o", o, dzb, DG['dm'], wo_base + N_FOX * Ds, 'row')
        do = _mm_row_t("rel_do", dzb, WG['dm'], wo_base + N_FOX * Ds, Ds, BF16)
        dq, dk, dv, dbias = _rel_bwd(qkv, bias, do, H)
        SG['rel_bias'] = _rel_reduce(jnp.transpose(dbias, (1, 0, 2)))[:, :REL_TABLE].reshape(1, H, REL_TABLE)
        dqkv = jnp.concatenate([dq, dk, dv], axis=1)
        DG['qkv'] = _mm_dw("rel_dw_qkv", xb, dqkv, DG['qkv'], N_FOX * D, 'col')
        return [_mm_col_t("rel_dx", [(dqkv, N_FOX * D)], WG['qkv'], D)]

    w_dw32 = jnp.pad(full['conv_w_dw'][0], ((0, CONV_HALO - CONV_K), (0, 0)))
    cg, cb = full['conv_ln_g'], full['conv_ln_b']

    def conv_fwd(xb, rest_stage):
        u = _mm_col("conv_pw1", xb, WG['pw1'], 0, bias=full['conv_b_pw1'], out_dtype=F32)
        u2, = _rowwise("conv_glu", lambda a_, g_: [a_ * _sigmoid(g_)],
                       [('cols', u, D, 0), ('cols', u, D, 1)], [('rows', D, F32)], S)
        yc = _dwconv("conv_dw", u2, w_dw32, full['conv_b_dw'], False)

        def fn(y_, g_, b_):
            xhat, rstd = _ln_stats(y_)
            ln = xhat * g_ + b_
            return ln * _sigmoid(ln), xhat, rstd

        zc, xhat, rstd = _rowwise("conv_ln_silu", fn, [yc, ('full', cg), ('full', cb)],
                                  [('rows', D, BF16), ('rows', D, F32), ('rows', 1, F32)], S)
        weights_ready(rest_stage, zc)
        m = _mm_row("conv_pw2", zc, WG['dm'], wo_base + (N_FOX + 1) * Ds, Ds, bias=full['conv_b_pw2'])
        return m, (u, u2, zc, xhat, rstd)

    def conv_bwd(xb, saved, dz, dzb):
        u, u2, zc, xhat, rstd = saved
        r0 = wo_base + (N_FOX + 1) * Ds
        DG['dm'] = _mm_dw("conv_dw_pw2", zc, dzb, DG['dm'], r0, 'row')
        dzc = _mm_row_t("conv_dzc", dzb, WG['dm'], r0, Ds, F32)

        def fn(dm_, dzc_, xh_, rs_, g_, b_):
            ln = xh_ * g_ + b_
            sg = _sigmoid(ln)
            dln = dzc_ * (sg * (1.0 + ln * (1.0 - sg)))
            dyc = _ln_bwd_core(dln, xh_, rs_, g_)
            col = lambda t: jnp.sum(t, axis=0, keepdims=True)
            return dyc, col(dm_), col(dln * xh_), col(dln), col(dyc)

        dyc, SG['conv_b_pw2'], SG['conv_ln_g'], SG['conv_ln_b'], SG['conv_b_dw'] = _rowwise(
            "conv_ln_silu_bwd", fn, [dz, dzc, xhat, rstd, ('full', cg), ('full', cb)],
            [('rows', D, F32)] + [('acc', (1, D), F32)] * 4, S)
        du2 = _dwconv("conv_dw_bwd_x", dyc, w_dw32, jnp.zeros((1, D), F32), True)
        SG['conv_w_dw'] = _dwconv_dw(u2, dyc)[:CONV_K].reshape(1, CONV_K, D)

        def fn2(du2_, a_, g_):
            sg = _sigmoid(g_)
            da, dgt = du2_ * sg, du2_ * a_ * sg * (1.0 - sg)
            return da, dgt, jnp.sum(da, axis=0, keepdims=True), jnp.sum(dgt, axis=0, keepdims=True)

        da, dgt, dba, dbg = _rowwise("conv_glu_bwd", fn2, [du2, ('cols', u, D, 0), ('cols', u, D, 1)],
                                     [('rows', D, BF16), ('rows', D, BF16), ('acc', (1, D), F32), ('acc', (1, D), F32)], S)
        SG['conv_b_pw1'] = jnp.concatenate([dba, dbg], axis=1)
        du = jnp.concatenate([da, dgt], axis=1)
        DG['pw1'] = _mm_dw("conv_dw_pw1", xb, du, DG['pw1'], 0, 'col')
        return [_mm_col_t("conv_dx", [(du, 0)], WG['pw1'], D)]

    xs = x[0]
    xs_b = xs.astype(BF16)
    tape = []

    def weights_ready(s, after):
        bufs = _gather_wait(f"gather_wait{s}", [WG[g] for g in GROUPS], gather_ssems[s], gather_rsems[s], gstages[s], after)
        WG.update(zip(GROUPS, _gather_forward(f"gather_fwd{s}", bufs, gstages[s])))

    for i in range(DEPTH):
        kind, j = i % 3, i // 3
        weights_ready(2 * i, xs)
        if kind == 0:
            m, msaved = fox_fwd(j, xs_b, 2 * i + 1)
        elif kind == 1:
            m, msaved = rel_fwd(xs_b, 2 * i + 1)
        else:
            m, msaved = conv_fwd(xs_b, 2 * i + 1)
        xm, xm_b, xhat1, rstd1 = _ln_fwd(f"ln_mix{i}", xs, m, row(ln_mix_g[i]), row(ln_mix_b[i]))
        f, fsaved = ffn_fwd(i, xm_b)
        xo, xo_b, xhat2, rstd2 = _ln_fwd(f"ln_ffn{i}", xm, f, row(ln_ffn_g[i]), row(ln_ffn_b[i]))
        tape.append((xs_b, msaved, xhat1, rstd1, xm_b, fsaved, xhat2, rstd2))
        xs, xs_b = xo, xo_b

    def loss_fn(y_, t_):
        e = y_ - t_
        return e * (1.0 / D), jnp.sum(e * e, axis=0, keepdims=True)

    dy, sq = _rowwise("loss", loss_fn, [xs, loss_target[0]], [('rows', D, F32), ('acc', (1, D), F32)], S)
    loss = lax.psum(jnp.sum(sq) * (0.5 / D), ("x", "y", "c"))

    GF = {g: lax.empty(WG[g].shape[1:], F32) for g in GROUPS}
    started = [None] * len(stages)

    def reduce_start(s):
        dgs = [DG[g] for g in GROUPS]
        lands = _swap_halves(f"pair_swap{s}", dgs, stages[s])
        pbs = [_pair_sum(f"pair_sum{s}_{pi}", dgs[g], lands[pi], place, r0, n) for pi, (g, r0, n) in enumerate(stages[s])]
        started[s] = _scatter_start(f"scatter_start{s}", pbs)
        token = started[s][2][0]
        if s + 1 < len(stages):
            reduce_finish(s + 1, token)
        return token

    def reduce_finish(s, after):
        ssem, rsem, pbs, lands2 = started[s]
        pbs, lands2 = _scatter_wait(f"scatter_wait{s}", ssem, rsem, pbs, lands2, after)
        for pi, (g, r0, n) in enumerate(stages[s]):
            GF[GROUPS[g]] = _chip_sum(f"chip_sum{s}_{pi}", pbs[pi], lands2[pi], place, GF[GROUPS[g]], r0, n)

    terms = [(dy, 1.0)]
    token = None
    g_mix, b_mix, g_ffn, b_ffn = [None] * DEPTH, [None] * DEPTH, [None] * DEPTH, [None] * DEPTH
    for i in reversed(range(DEPTH)):
        kind, j = i % 3, i // 3
        xin_b, msaved, xhat1, rstd1, xm_b, fsaved, xhat2, rstd2 = tape[i]
        dz2, dz2b, g_ffn[i], b_ffn[i] = _ln_bwd(f"ln_ffn{i}_bwd", terms, xhat2, rstd2, row(ln_ffn_g[i]), after=token)
        dx_ffn = ffn_bwd(i, xm_b, fsaved, dz2b)
        token = reduce_start(2 * i + 1)
        dz1, dz1b, g_mix[i], b_mix[i] = _ln_bwd(f"ln_mix{i}_bwd", [(dz2, ALPHA), (dx_ffn, 1.0)], xhat1, rstd1,
                                                row(ln_mix_g[i]), after=token)
        if kind == 0:
            mix_terms = fox_bwd(j, xin_b, msaved, dz1b)
        elif kind == 1:
            mix_terms = rel_bwd(xin_b, msaved, dz1b)
        else:
            mix_terms = conv_bwd(xin_b, msaved, dz1, dz1b)
        terms = [(dz1, ALPHA)] + [(t, 1.0) for t in mix_terms]
        token = reduce_start(2 * i)

    def gx_fn(*v):
        acc = v[0] * ALPHA
        for t in v[1:]:
            acc = acc + t
        return [acc]

    grad_x, = _rowwise("grad_x", gx_fn, [a for a, _ in terms], [('rows', D, F32)], S)
    grad_x = grad_x.reshape(1, S, D)

    SG['fox_w_f'] = jnp.stack(SG['fox_w_f'])
    SG['fox_b_f'] = jnp.stack(SG['fox_b_f'])
    SG['ln_mix_g'] = jnp.concatenate(g_mix, axis=0)
    SG['ln_mix_b'] = jnp.concatenate(b_mix, axis=0)
    SG['ln_ffn_g'] = jnp.concatenate(g_ffn, axis=0)
    SG['ln_ffn_b'] = jnp.concatenate(b_ffn, axis=0)

    grads, deltas, new_m, new_v = {}, {}, {}, {}

    reduce_finish(0, grad_x)
    links_idle = GF['dm']
    GF = dict(zip(GROUPS, _share_halves("pair_share", [GF[g] for g in GROUPS], [p for st in stages for p in st])))

    for n in BIG:
        g, r0 = where[n]
        outs = _adamw("adamw_" + n, flat2(Wt[n]), GF[g], r0, flat2(Mo[n]), flat2(Vo[n]))
        grads[n], deltas[n], new_m[n], new_v[n] = [o.reshape(Wt[n].shape) for o in outs]

    full_shapes = [SG[n].shape for n in SMALL]
    summed = _sum_slabs("small_sum", _broadcast_small("small_exchange", _pack([SG[n] for n in SMALL]), after=links_idle))
    gsm = dict(zip(SMALL, _unpack(summed, full_shapes)))
    for n in SMALL_SHARDED:
        ax = SMALL_SHARD_AXIS[n]
        width = Wt[n].shape[ax]
        gsm[n] = lax.dynamic_slice_in_dim(gsm[n], my_chip * width, width, axis=ax)
    own_shapes = [Wt[n].shape for n in SMALL]
    packed = [_pack([src[n] for n in SMALL]) for src in (Wt, gsm, Mo, Vo)]
    rows_small = packed[0].shape[0]

    def small_fn(w_, g_, m_, v_):
        return _adamw_math(w_, g_, m_, v_)

    sd, sm, sv = _rowwise("adamw_small", small_fn, packed, [('rows', LANES, F32)] * 3, rows_small, tm=rows_small)
    for n, d_, m_, v_ in zip(SMALL, _unpack(sd, own_shapes), _unpack(sm, own_shapes), _unpack(sv, own_shapes)):
        grads[n], deltas[n], new_m[n], new_v[n] = gsm[n], d_, m_, v_

    return (loss, grad_x, *[grads[n] for n in WEIGHTS], *[deltas[n] for n in WEIGHTS],
            *[new_m[n] for n in WEIGHTS], *[new_v[n] for n in WEIGHTS])
```

```python
import functools
import math

import jax
import jax.numpy as jnp
from jax import lax
from jax.experimental import pallas as pl
from jax.experimental.pallas import tpu as pltpu

F32 = jnp.float32
BF16 = jnp.bfloat16
MESH_IDS = pl.DeviceIdType.MESH
HIGHEST = lax.Precision.HIGHEST

N_CHIPS = 4
DEPTH = 4
N_FOX = 2
HEAD_DIM = 128
CHUNK = 64
LEFT_CHUNKS = 8
BAND_KEYS = (LEFT_CHUNKS + 1) * CHUNK
PAD_KEYS = LEFT_CHUNKS * CHUNK
REL_CLIP = 128
REL_TABLE = 2 * REL_CLIP + 1
REL_TABLE_PAD = 384
REL_QB = 4 * CHUNK
REL_WIN = REL_QB + PAD_KEYS
CONV_K = 31
CONV_HALO = 32
ALPHA = (2.0 * DEPTH) ** 0.25
LN_EPS = 1e-5
ADAM_LR, ADAM_B1, ADAM_B2, ADAM_EPS, ADAM_WD, ADAM_STEP = 0.001, 0.9, 0.999, 1e-08, 0.01, 10
NEG_BIG = -1e30
VMEM_LIMIT_V7X = 56 * 1024 * 1024
LANES = 128
SUBLANES = 8
MM_ROWS = 1024


def _cparams(*sem):
    return pltpu.CompilerParams(dimension_semantics=sem if sem else None, vmem_limit_bytes=VMEM_LIMIT_V7X)


def _pick(dim, target):
    best = None
    for t in range(LANES, min(dim, target) + 1, LANES):
        if dim % t == 0:
            best = t
    return best if best is not None else dim


def _dot_nt(a, b):
    return lax.dot_general(a, b, (((1,), (1,)), ((), ())), preferred_element_type=F32)


def _dot_tn(a, b):
    return lax.dot_general(a, b, (((0,), (0,)), ((), ())), preferred_element_type=F32)


def _sigmoid(z):
    return 1.0 / (1.0 + jnp.exp(-z))


def _rowwise(name, fn, ins, outs, S, tm=256):
    tm = min(tm, S)
    afters = [it[1] for it in ins if isinstance(it, tuple) and it[0] == 'after']
    ins = [it for it in ins if not (isinstance(it, tuple) and it[0] == 'after')]
    arrs, in_specs = [], []
    for it in ins:
        if isinstance(it, tuple) and it[0] == 'full':
            a = it[1]
            in_specs.append(pl.BlockSpec(a.shape, lambda i, _n=a.ndim: (0,) * _n))
        elif isinstance(it, tuple) and it[0] == 'cols':
            _, a, width, blk = it
            in_specs.append(pl.BlockSpec((tm, width), lambda i, _b=blk: (i, _b)))
        elif isinstance(it, tuple) and it[0] == 'off':
            _, a, off = it
            in_specs.append(pl.BlockSpec((tm, a.shape[1]), lambda i, _o=off: (i + _o, 0)))
        else:
            a = it
            in_specs.append(pl.BlockSpec((tm, a.shape[1]), lambda i: (i, 0)))
        arrs.append(a)
    out_shape, out_specs = [], []
    for kind, shp, dt in outs:
        if kind == 'rows':
            out_shape.append(jax.ShapeDtypeStruct((S, shp), dt))
            out_specs.append(pl.BlockSpec((tm, shp), lambda i: (i, 0)))
        else:
            out_shape.append(jax.ShapeDtypeStruct(shp, dt))
            out_specs.append(pl.BlockSpec(shp, lambda i, _n=len(shp): (0,) * _n))
    n_in = len(arrs)
    in_specs += [pl.BlockSpec(memory_space=pl.ANY)] * len(afters)

    def body(*refs):
        vals = fn(*[r[...] for r in refs[:n_in]])
        first = pl.program_id(0) == 0
        for (kind, _, _), r, v in zip(outs, refs[n_in + len(afters):], vals):
            if kind == 'rows':
                r[...] = v.astype(r.dtype)
            else:
                @pl.when(first)
                def _(r=r, v=v):
                    r[...] = v.astype(r.dtype)

                @pl.when(jnp.logical_not(first))
                def _(r=r, v=v):
                    r[...] += v.astype(r.dtype)

    has_acc = any(k != 'rows' for k, _, _ in outs)
    res = pl.pallas_call(
        body, name=name, grid=(S // tm,), in_specs=in_specs, out_specs=out_specs, out_shape=out_shape,
        compiler_params=_cparams("arbitrary" if has_acc else "parallel"),
    )(*arrs, *afters)
    return res


def _mm_col(name, a, wg, row_start, bias=None, out_dtype=BF16):
    S, K = a.shape
    _, _, Ns = wg.shape
    rb = row_start // K
    tm = min(MM_ROWS, S)

    def body(a_ref, w_ref, *rest):
        acc = jnp.dot(a_ref[...].astype(BF16), w_ref[...], preferred_element_type=F32)
        if bias is not None:
            acc = acc + rest[0][...]
        rest[-1][...] = acc.astype(out_dtype)

    in_specs = [pl.BlockSpec((tm, K), lambda j, m: (m, 0)), pl.BlockSpec((None, K, Ns), lambda j, m: (j, rb, 0))]
    args = [a, wg]
    if bias is not None:
        in_specs.append(pl.BlockSpec((1, Ns), lambda j, m: (0, j)))
        args.append(bias)
    return pl.pallas_call(
        body, name=name, grid=(N_CHIPS, S // tm), in_specs=in_specs,
        out_specs=pl.BlockSpec((tm, Ns), lambda j, m: (m, j)),
        out_shape=jax.ShapeDtypeStruct((S, N_CHIPS * Ns), out_dtype),
        compiler_params=_cparams("parallel", "parallel"),
    )(*args)


def _mm_row(name, a, wg, row_start, Ks, bias=None):
    S = a.shape[0]
    N = wg.shape[2]
    rb = row_start // Ks
    tm = min(MM_ROWS, S)

    def body(a_ref, w_ref, *rest):
        o_ref = rest[-1]
        j = pl.program_id(1)
        d = jnp.dot(a_ref[...].astype(BF16), w_ref[...], preferred_element_type=F32)

        @pl.when(j == 0)
        def _():
            o_ref[...] = d + rest[0][...] if bias is not None else d

        @pl.when(j > 0)
        def _():
            o_ref[...] += d

    in_specs = [pl.BlockSpec((tm, Ks), lambda m, j: (m, j)), pl.BlockSpec((None, Ks, N), lambda m, j: (j, rb, 0))]
    args = [a, wg]
    if bias is not None:
        in_specs.append(pl.BlockSpec((1, N), lambda m, j: (0, 0)))
        args.append(bias)
    return pl.pallas_call(
        body, name=name, grid=(S // tm, N_CHIPS), in_specs=in_specs,
        out_specs=pl.BlockSpec((tm, N), lambda m, j: (m, 0)),
        out_shape=jax.ShapeDtypeStruct((S, N), F32),
        compiler_params=_cparams("parallel", "arbitrary"),
    )(*args)


def _mm_col_t(name, pairs, wg, K):
    S = pairs[0][0].shape[0]
    Ns = wg.shape[2]
    tm = min(MM_ROWS, S)
    n = len(pairs)
    tkk = K // n if (K // n) % LANES == 0 else K

    def body(*refs):
        o_ref = refs[-1]
        j = pl.program_id(2)
        d = _dot_nt(refs[0][...], refs[n][...])
        for p in range(1, n):
            d = d + _dot_nt(refs[p][...], refs[n + p][...])

        @pl.when(j == 0)
        def _():
            o_ref[...] = d

        @pl.when(j > 0)
        def _():
            o_ref[...] += d

    in_specs = [pl.BlockSpec((tm, Ns), lambda m, kb, j: (m, j)) for _ in pairs]
    in_specs += [pl.BlockSpec((None, tkk, Ns), lambda m, kb, j, _rb=rs // tkk: (j, _rb + kb, 0)) for _, rs in pairs]
    return pl.pallas_call(
        body, name=name, grid=(S // tm, K // tkk, N_CHIPS), in_specs=in_specs,
        out_specs=pl.BlockSpec((tm, tkk), lambda m, kb, j: (m, kb)),
        out_shape=jax.ShapeDtypeStruct((S, K), F32),
        compiler_params=_cparams("parallel", "parallel", "arbitrary"),
    )(*[dy for dy, _ in pairs], *[wg for _ in pairs])


def _mm_row_t(name, dy, wg, row_start, Ks, out_dtype):
    S, N = dy.shape
    rb = row_start // Ks
    tm = min(MM_ROWS, S)

    def body(dy_ref, w_ref, o_ref):
        o_ref[...] = _dot_nt(dy_ref[...], w_ref[...]).astype(out_dtype)

    return pl.pallas_call(
        body, name=name, grid=(N_CHIPS, S // tm),
        in_specs=[pl.BlockSpec((tm, N), lambda j, m: (m, 0)), pl.BlockSpec((None, Ks, N), lambda j, m: (j, rb, 0))],
        out_specs=pl.BlockSpec((tm, Ks), lambda j, m: (m, j)),
        out_shape=jax.ShapeDtypeStruct((S, N_CHIPS * Ks), out_dtype),
        compiler_params=_cparams("parallel", "parallel"),
    )(dy, wg)


def _silu_parts(g):
    sg = _sigmoid(g)
    return g * sg, sg * (1.0 + g * (1.0 - sg))


def _mm_gate_up(name, a, wg, gate_row, up_row):
    S, K = a.shape
    Ns = wg.shape[2]
    tm = min(MM_ROWS // 2, S)

    def body(a_ref, wg_ref, wu_ref, hg_ref, hu_ref, act_ref):
        a_ = a_ref[...]
        hg = jnp.dot(a_, wg_ref[...], preferred_element_type=F32).astype(BF16)
        hu = jnp.dot(a_, wu_ref[...], preferred_element_type=F32).astype(BF16)
        hg_ref[...] = hg
        hu_ref[...] = hu
        act_ref[...] = (_silu_parts(hg.astype(F32))[0] * hu.astype(F32)).astype(BF16)

    out = jax.ShapeDtypeStruct((S, N_CHIPS * Ns), BF16)
    w_spec = lambda rb: pl.BlockSpec((None, K, Ns), lambda j, m: (j, rb, 0))
    o_spec = pl.BlockSpec((tm, Ns), lambda j, m: (m, j))
    return pl.pallas_call(
        body, name=name, grid=(N_CHIPS, S // tm),
        in_specs=[pl.BlockSpec((tm, K), lambda j, m: (m, 0)), w_spec(gate_row // K), w_spec(up_row // K)],
        out_specs=[o_spec, o_spec, o_spec], out_shape=[out, out, out],
        compiler_params=_cparams("parallel", "parallel"),
    )(a, wg, wg)


def _mm_dact(name, dy, wg, row_start, Ks, hg, hu):
    S, N = dy.shape
    rb = row_start // Ks
    tm = min(512, S)

    def body(dy_ref, w_ref, hg_ref, hu_ref, dhg_ref, dhu_ref):
        dact = _dot_nt(dy_ref[...], w_ref[...])
        silu, dsilu = _silu_parts(hg_ref[...].astype(F32))
        dhg_ref[...] = (dact * hu_ref[...].astype(F32) * dsilu).astype(BF16)
        dhu_ref[...] = (dact * silu).astype(BF16)

    out = jax.ShapeDtypeStruct((S, N_CHIPS * Ks), BF16)
    t_spec = pl.BlockSpec((tm, Ks), lambda j, m: (m, j))
    return pl.pallas_call(
        body, name=name, grid=(N_CHIPS, S // tm),
        in_specs=[pl.BlockSpec((tm, N), lambda j, m: (m, 0)), pl.BlockSpec((None, Ks, N), lambda j, m: (j, rb, 0)), t_spec, t_spec],
        out_specs=[t_spec, t_spec], out_shape=[out, out],
        compiler_params=_cparams("parallel", "parallel"),
    )(dy, wg, hg, hu)


def _mm_dw(name, a, dy, dg, row_start, kind):
    S = a.shape[0]
    _, _, W = dg.shape
    if kind == 'col':
        K = a.shape[1]
        rows = K
        tk, tn = _pick(K, MM_ROWS), W
        a_map = lambda j, nb, kb: (0, kb)
        dy_map = lambda j, nb, kb: (0, j * (W // tn) + nb)
    else:
        rows = a.shape[1] // N_CHIPS
        tk = rows if rows * S * 2 * 2 <= 12 * 1024 * 1024 else _pick(rows, 512)
        tn = _pick(W, 1024)
        a_map = lambda j, nb, kb: (0, j * (rows // tk) + kb)
        dy_map = lambda j, nb, kb: (0, nb)
    rb = row_start // tk
    assert row_start % tk == 0

    def body(a_ref, dy_ref, dg_in, o_ref):
        del dg_in
        o_ref[...] = _dot_tn(a_ref[...], dy_ref[...]).astype(o_ref.dtype)

    return pl.pallas_call(
        body, name=name, grid=(N_CHIPS, W // tn, rows // tk),
        in_specs=[pl.BlockSpec((S, tk), a_map), pl.BlockSpec((S, tn), dy_map), pl.BlockSpec(memory_space=pl.ANY)],
        out_specs=pl.BlockSpec((None, tk, tn), lambda j, nb, kb: (j, rb + kb, nb)),
        out_shape=jax.ShapeDtypeStruct(dg.shape, dg.dtype),
        input_output_aliases={2: 0},
        compiler_params=_cparams("parallel", "parallel", "parallel"),
    )(a, dy, dg)


def _fox_probs(q, k, c_blk, crow, h, qi, tq):
    n = k.shape[0]
    s = _dot_nt(q, k) * (HEAD_DIM ** -0.5)
    lane = lax.broadcasted_iota(jnp.int32, c_blk.shape, 1)
    ccol = jnp.sum(jnp.where(lane == h, c_blk, 0.0), axis=1, keepdims=True)
    s = s + (ccol - crow)
    t_idx = qi * tq + lax.broadcasted_iota(jnp.int32, (tq, n), 0)
    s_idx = lax.broadcasted_iota(jnp.int32, (tq, n), 1)
    s = jnp.where(s_idx <= t_idx, s, NEG_BIG)
    p = jnp.exp(s - jnp.max(s, axis=1, keepdims=True))
    return p * (1.0 / jnp.sum(p, axis=1, keepdims=True))


def _per_query_block(qi, nq, tq, fn):
    for qv in range(nq):
        @pl.when(qi == qv)
        def _(qv=qv):
            fn(qv, (qv + 1) * tq)


FOX_HEADS = 4


def _head_cols(a):
    return slice(a * HEAD_DIM, (a + 1) * HEAD_DIM)


def _fox_fwd(qkv, c, crow, H):
    S = qkv.shape[0]
    tq = min(256, S)
    heads = FOX_HEADS
    hw = heads * HEAD_DIM
    G = H // heads

    def body(q_ref, k_ref, v_ref, c_ref, crow_ref, o_ref):
        def block(qv, n):
            for a in range(heads):
                cols = _head_cols(a)
                p = _fox_probs(q_ref[:, cols], k_ref[0:n, cols], c_ref[...], crow_ref[a, :, 0:n],
                               heads * pl.program_id(0) + a, qv, tq)
                o_ref[:, cols] = jnp.dot(p.astype(BF16), v_ref[0:n, cols], preferred_element_type=F32).astype(o_ref.dtype)

        _per_query_block(pl.program_id(1), S // tq, tq, block)

    return pl.pallas_call(
        body, name="fox_attn_fwd", grid=(G, S // tq),
        in_specs=[pl.BlockSpec((tq, hw), lambda g, i: (i, g)),
                  pl.BlockSpec((S, hw), lambda g, i: (0, G + g)),
                  pl.BlockSpec((S, hw), lambda g, i: (0, 2 * G + g)),
                  pl.BlockSpec((tq, H), lambda g, i: (i, 0)),
                  pl.BlockSpec((heads, 1, S), lambda g, i: (g, 0, 0))],
        out_specs=pl.BlockSpec((tq, hw), lambda g, i: (i, g)),
        out_shape=jax.ShapeDtypeStruct((S, H * HEAD_DIM), BF16),
        compiler_params=_cparams("parallel", "parallel"),
    )(qkv, qkv, qkv, c, crow)


def _fox_bwd(qkv, c, crow, do, H):
    S = qkv.shape[0]
    tq = min(256, S)
    nq = S // tq
    hw = FOX_HEADS * HEAD_DIM
    G = H // FOX_HEADS

    def body(q_ref, k_ref, v_ref, c_ref, crow_ref, do_ref, dq_ref, dk_ref, dv_ref, dc_ref, dk_acc, dv_acc):
        qi = pl.program_id(1)

        @pl.when(qi == 0)
        def _():
            dk_acc[...] = jnp.zeros_like(dk_acc)
            dv_acc[...] = jnp.zeros_like(dv_acc)
            dc_ref[...] = jnp.zeros_like(dc_ref)

        def block(qv, n):
            for a in range(FOX_HEADS):
                cols = _head_cols(a)
                q, k, v, do_ = q_ref[:, cols], k_ref[0:n, cols], v_ref[0:n, cols], do_ref[:, cols]
                p = _fox_probs(q, k, c_ref[...], crow_ref[a, :, 0:n], FOX_HEADS * pl.program_id(0) + a, qv, tq)
                dv_acc[0:n, cols] += _dot_tn(p.astype(BF16), do_)
                dp = _dot_nt(do_, v)
                ds = p * (dp - jnp.sum(p * dp, axis=1, keepdims=True))
                dsb = (ds * (HEAD_DIM ** -0.5)).astype(BF16)
                dq_ref[:, cols] = jnp.dot(dsb, k, preferred_element_type=F32).astype(dq_ref.dtype)
                dk_acc[0:n, cols] += _dot_tn(dsb, q)
                dc_ref[a, :, 0:n] += -jnp.sum(ds, axis=0, keepdims=True)

        _per_query_block(qi, nq, tq, block)

        @pl.when(qi == nq - 1)
        def _():
            dk_ref[...] = dk_acc[...].astype(dk_ref.dtype)
            dv_ref[...] = dv_acc[...].astype(dv_ref.dtype)

    D = H * HEAD_DIM
    return pl.pallas_call(
        body, name="fox_attn_bwd", grid=(G, nq),
        in_specs=[pl.BlockSpec((tq, hw), lambda g, i: (i, g)),
                  pl.BlockSpec((S, hw), lambda g, i: (0, G + g)),
                  pl.BlockSpec((S, hw), lambda g, i: (0, 2 * G + g)),
                  pl.BlockSpec((tq, H), lambda g, i: (i, 0)),
                  pl.BlockSpec((FOX_HEADS, 1, S), lambda g, i: (g, 0, 0)),
                  pl.BlockSpec((tq, hw), lambda g, i: (i, g))],
        out_specs=[pl.BlockSpec((tq, hw), lambda g, i: (i, g)),
                   pl.BlockSpec((S, hw), lambda g, i: (0, g)),
                   pl.BlockSpec((S, hw), lambda g, i: (0, g)),
                   pl.BlockSpec((FOX_HEADS, 1, S), lambda g, i: (g, 0, 0))],
        out_shape=[jax.ShapeDtypeStruct((S, D), BF16), jax.ShapeDtypeStruct((S, D), BF16),
                   jax.ShapeDtypeStruct((S, D), BF16), jax.ShapeDtypeStruct((H, 1, S), F32)],
        scratch_shapes=[pltpu.VMEM((S, hw), F32), pltpu.VMEM((S, hw), F32)],
        compiler_params=_cparams("parallel", "arbitrary"),
    )(qkv, qkv, qkv, c, crow, do)


def _cumsum_rows(name, xin, reverse):
    S, H = xin.shape
    tb = min(256, S)
    nb = S // tb

    def body(x_ref, o_ref):
        r = lax.broadcasted_iota(jnp.int32, (tb, tb), 0)
        cidx = lax.broadcasted_iota(jnp.int32, (tb, tb), 1)
        tri = (r <= cidx if reverse else r >= cidx).astype(F32)

        def step(b, carry):
            bb = nb - 1 - b if reverse else b
            rows = pl.ds(pl.multiple_of(bb * tb, tb), tb)
            blk = x_ref[rows, :]
            o_ref[rows, :] = jnp.dot(tri, blk, precision=HIGHEST, preferred_element_type=F32) + carry
            return carry + jnp.sum(blk, axis=0, keepdims=True)

        lax.fori_loop(0, nb, step, jnp.zeros((1, H), F32))

    return pl.pallas_call(
        body, name=name, out_shape=jax.ShapeDtypeStruct((S, H), F32),
        in_specs=[pl.BlockSpec(memory_space=pltpu.VMEM)], out_specs=pl.BlockSpec(memory_space=pltpu.VMEM),
        compiler_params=_cparams(),
    )(xin)


def _rel_onehot(i, transposed):
    shp = (REL_TABLE_PAD, BAND_KEYS) if transposed else (BAND_KEYS, REL_TABLE_PAD)
    j = lax.broadcasted_iota(jnp.int32, shp, 1 if transposed else 0)
    r = lax.broadcasted_iota(jnp.int32, shp, 0 if transposed else 1)
    return (jnp.clip(PAD_KEYS + i - j, -REL_CLIP, REL_CLIP) + REL_CLIP == r).astype(F32)


def _rel_expand(rb_pad):
    H = rb_pad.shape[0]

    def body(rb_ref, o_ref):
        rb = rb_ref[...]
        hi = rb.astype(BF16)
        rest = rb - hi.astype(F32)
        mid = rest.astype(BF16)
        lo = (rest - mid.astype(F32)).astype(BF16)

        def step(i, _):
            onehot = _rel_onehot(i, True).astype(BF16)
            pick = lambda part: jnp.dot(part, onehot, preferred_element_type=F32)
            o_ref[i] = (pick(hi) + pick(mid)) + pick(lo)
            return 0
        lax.fori_loop(0, CHUNK, step, 0)

    return pl.pallas_call(
        body, name="rel_bias_expand", out_shape=jax.ShapeDtypeStruct((CHUNK, H, BAND_KEYS), F32),
        in_specs=[pl.BlockSpec(memory_space=pltpu.VMEM)], out_specs=pl.BlockSpec(memory_space=pltpu.VMEM),
        compiler_params=_cparams(),
    )(rb_pad)


def _rel_reduce(dbt):
    H = dbt.shape[1]

    near0 = PAD_KEYS - REL_CLIP
    assert near0 % LANES == 0 and REL_WIN - near0 == REL_TABLE_PAD

    def body(d_ref, o_ref):
        j = near0 + lax.broadcasted_iota(jnp.int32, (REL_TABLE_PAD, REL_TABLE_PAD), 0)
        r = lax.broadcasted_iota(jnp.int32, (REL_TABLE_PAD, REL_TABLE_PAD), 1)

        def step(i, carry):
            acc, far = carry
            onehot = (jnp.clip(PAD_KEYS + i - j, -REL_CLIP, REL_CLIP) + REL_CLIP == r).astype(BF16)
            d = d_ref[i]
            near = d[:, near0:]
            hi = near.astype(BF16)
            lo = (near - hi.astype(F32)).astype(BF16)
            acc = acc + (jnp.dot(hi, onehot, preferred_element_type=F32) + jnp.dot(lo, onehot, preferred_element_type=F32))
            return acc, far + jnp.sum(d[:, :near0], axis=1, keepdims=True)

        acc, far = lax.fori_loop(0, REL_QB, step, (jnp.zeros((H, REL_TABLE_PAD), F32), jnp.zeros((H, 1), F32)))
        col = lax.broadcasted_iota(jnp.int32, (H, REL_TABLE_PAD), 1)
        o_ref[...] = acc + jnp.where(col == REL_TABLE - 1, far, 0.0)

    return pl.pallas_call(
        body, name="rel_bias_reduce", out_shape=jax.ShapeDtypeStruct((H, REL_TABLE_PAD), F32),
        in_specs=[pl.BlockSpec(memory_space=pltpu.VMEM)], out_specs=pl.BlockSpec(memory_space=pltpu.VMEM),
        compiler_params=_cparams(),
    )(dbt)


def _rel_window_bias(bias):
    H = bias.shape[0]
    out = jnp.full((H, REL_QB, REL_WIN), NEG_BIG, F32)
    for a in range(REL_QB // CHUNK):
        out = out.at[:, a * CHUNK:(a + 1) * CHUNK, a * CHUNK:a * CHUNK + BAND_KEYS].set(bias)
    return out


def _rel_probs(q, kw, bias_w, t0):
    s = _dot_nt(q, kw) * (HEAD_DIM ** -0.5) + bias_w
    j = lax.broadcasted_iota(jnp.int32, (REL_QB, REL_WIN), 1)
    s = jnp.where(j >= PAD_KEYS - t0, s, NEG_BIG)
    p = jnp.exp(s - jnp.max(s, axis=1, keepdims=True))
    return p * (1.0 / jnp.sum(p, axis=1, keepdims=True))


def _rel_fwd(qkv, bias_w, H):
    S = qkv.shape[0]

    def body(q_ref, k_ref, v_ref, b_ref, o_ref, kpad, vpad):
        kpad[0:PAD_KEYS, :] = jnp.zeros((PAD_KEYS, HEAD_DIM), BF16)
        vpad[0:PAD_KEYS, :] = jnp.zeros((PAD_KEYS, HEAD_DIM), BF16)
        kpad[PAD_KEYS:PAD_KEYS + S, :] = k_ref[...]
        vpad[PAD_KEYS:PAD_KEYS + S, :] = v_ref[...]

        def block(n, _):
            t0 = pl.multiple_of(n * REL_QB, REL_QB)
            rows, win = pl.ds(t0, REL_QB), pl.ds(t0, REL_WIN)
            p = _rel_probs(q_ref[rows, :], kpad[win, :], b_ref[...], t0)
            o_ref[rows, :] = jnp.dot(p.astype(BF16), vpad[win, :], preferred_element_type=F32).astype(o_ref.dtype)
            return 0

        lax.fori_loop(0, S // REL_QB, block, 0, unroll=2)

    return pl.pallas_call(
        body, name="rel_attn_fwd", grid=(H,),
        in_specs=[pl.BlockSpec((S, HEAD_DIM), lambda h: (0, h)),
                  pl.BlockSpec((S, HEAD_DIM), lambda h: (0, H + h)),
                  pl.BlockSpec((S, HEAD_DIM), lambda h: (0, 2 * H + h)),
                  pl.BlockSpec((None, REL_QB, REL_WIN), lambda h: (h, 0, 0))],
        out_specs=pl.BlockSpec((S, HEAD_DIM), lambda h: (0, h)),
        out_shape=jax.ShapeDtypeStruct((S, H * HEAD_DIM), BF16),
        scratch_shapes=[pltpu.VMEM((S + PAD_KEYS, HEAD_DIM), BF16), pltpu.VMEM((S + PAD_KEYS, HEAD_DIM), BF16)],
        compiler_params=_cparams("parallel"),
    )(qkv, qkv, qkv, bias_w)


def _rel_bwd(qkv, bias_w, do, H):
    S = qkv.shape[0]
    D = H * HEAD_DIM

    def body(q_ref, k_ref, v_ref, b_ref, do_ref, dq_ref, dk_ref, dv_ref, db_ref, kpad, vpad, dkpad, dvpad):
        kpad[0:PAD_KEYS, :] = jnp.zeros((PAD_KEYS, HEAD_DIM), BF16)
        vpad[0:PAD_KEYS, :] = jnp.zeros((PAD_KEYS, HEAD_DIM), BF16)
        kpad[PAD_KEYS:PAD_KEYS + S, :] = k_ref[...]
        vpad[PAD_KEYS:PAD_KEYS + S, :] = v_ref[...]
        dkpad[...] = jnp.zeros_like(dkpad)
        dvpad[...] = jnp.zeros_like(dvpad)
        db_ref[...] = jnp.zeros_like(db_ref)

        def block(n, _):
            t0 = pl.multiple_of(n * REL_QB, REL_QB)
            rows, win = pl.ds(t0, REL_QB), pl.ds(t0, REL_WIN)
            q, kw, vw, do_ = q_ref[rows, :], kpad[win, :], vpad[win, :], do_ref[rows, :]
            p = _rel_probs(q, kw, b_ref[...], t0)
            dvpad[win, :] += _dot_tn(p.astype(BF16), do_)
            dp = _dot_nt(do_, vw)
            ds = p * (dp - jnp.sum(p * dp, axis=1, keepdims=True))
            db_ref[...] += ds
            dsb = (ds * (HEAD_DIM ** -0.5)).astype(BF16)
            dq_ref[rows, :] = jnp.dot(dsb, kw, preferred_element_type=F32).astype(dq_ref.dtype)
            dkpad[win, :] += _dot_tn(dsb, q)
            return 0

        lax.fori_loop(0, S // REL_QB, block, 0, unroll=2)
        dk_ref[...] = dkpad[PAD_KEYS:PAD_KEYS + S, :].astype(dk_ref.dtype)
        dv_ref[...] = dvpad[PAD_KEYS:PAD_KEYS + S, :].astype(dv_ref.dtype)

    head = lambda h: (0, h)
    return pl.pallas_call(
        body, name="rel_attn_bwd", grid=(H,),
        in_specs=[pl.BlockSpec((S, HEAD_DIM), head),
                  pl.BlockSpec((S, HEAD_DIM), lambda h: (0, H + h)),
                  pl.BlockSpec((S, HEAD_DIM), lambda h: (0, 2 * H + h)),
                  pl.BlockSpec((None, REL_QB, REL_WIN), lambda h: (h, 0, 0)),
                  pl.BlockSpec((S, HEAD_DIM), head)],
        out_specs=[pl.BlockSpec((S, HEAD_DIM), head), pl.BlockSpec((S, HEAD_DIM), head), pl.BlockSpec((S, HEAD_DIM), head),
                   pl.BlockSpec((None, REL_QB, REL_WIN), lambda h: (h, 0, 0))],
        out_shape=[jax.ShapeDtypeStruct((S, D), BF16), jax.ShapeDtypeStruct((S, D), BF16), jax.ShapeDtypeStruct((S, D), BF16),
                   jax.ShapeDtypeStruct((H, REL_QB, REL_WIN), F32)],
        scratch_shapes=[pltpu.VMEM((S + PAD_KEYS, HEAD_DIM), BF16), pltpu.VMEM((S + PAD_KEYS, HEAD_DIM), BF16),
                        pltpu.VMEM((S + PAD_KEYS, HEAD_DIM), F32), pltpu.VMEM((S + PAD_KEYS, HEAD_DIM), F32)],
        compiler_params=_cparams("parallel"),
    )(qkv, qkv, qkv, bias_w, do)


def _conv_taps(win, tt, reverse):
    n = tt + 2 * CONV_HALO
    for k in range(CONV_K):
        off = (CONV_K - 1 - k) if reverse else (k - (CONV_K - 1))
        sh = (-off) % n
        rolled = pltpu.roll(win, sh, 0) if sh else win
        yield k, rolled[CONV_HALO:CONV_HALO + tt, :]


def _fill_padded(pad_ref, x_ref, S):
    tc = pad_ref.shape[1]
    pad_ref[0:CONV_HALO, :] = jnp.zeros((CONV_HALO, tc), F32)
    pad_ref[CONV_HALO + S:CONV_HALO + S + CONV_HALO, :] = jnp.zeros((CONV_HALO, tc), F32)
    pad_ref[CONV_HALO:CONV_HALO + S, :] = x_ref[...]


def _dwconv(name, xin, w32, bias, reverse):
    S, D = xin.shape
    tc = min(256, D)
    tt = min(256, S)

    def body(x_ref, w_ref, b_ref, y_ref, pad_ref):
        _fill_padded(pad_ref, x_ref, S)
        def tile(ti, _):
            t0 = pl.multiple_of(ti * tt, tt)
            win = pad_ref[pl.ds(t0, tt + 2 * CONV_HALO), :]
            acc = jnp.zeros((tt, tc), F32) + b_ref[...]
            for k, shifted in _conv_taps(win, tt, reverse):
                acc = acc + w_ref[pl.ds(k, 1), :] * shifted
            y_ref[pl.ds(t0, tt), :] = acc
            return 0

        lax.fori_loop(0, S // tt, tile, 0)

    return pl.pallas_call(
        body, name=name, grid=(D // tc,),
        in_specs=[pl.BlockSpec((S, tc), lambda i: (0, i)), pl.BlockSpec((CONV_HALO, tc), lambda i: (0, i)),
                  pl.BlockSpec((1, tc), lambda i: (0, i))],
        out_specs=pl.BlockSpec((S, tc), lambda i: (0, i)),
        out_shape=jax.ShapeDtypeStruct((S, D), F32),
        scratch_shapes=[pltpu.VMEM((S + 2 * CONV_HALO, tc), F32)],
        compiler_params=_cparams("parallel"),
    )(xin, w32, bias)


def _dwconv_dw(xin, dy):
    S, D = xin.shape
    tc = min(256, D)
    tt = min(256, S)

    def body(x_ref, dy_ref, o_ref, pad_ref):
        _fill_padded(pad_ref, x_ref, S)

        def tile(ti, acc):
            t0 = pl.multiple_of(ti * tt, tt)
            win = pad_ref[pl.ds(t0, tt + 2 * CONV_HALO), :]
            dyt = dy_ref[pl.ds(t0, tt), :]
            ridx = lax.broadcasted_iota(jnp.int32, (CONV_HALO, tc), 0)
            upd = jnp.zeros((CONV_HALO, tc), F32)
            for k, shifted in _conv_taps(win, tt, False):
                upd = jnp.where(ridx == k, jnp.sum(dyt * shifted, axis=0, keepdims=True), upd)
            return acc + upd

        o_ref[...] = lax.fori_loop(0, S // tt, tile, jnp.zeros((CONV_HALO, tc), F32))

    return pl.pallas_call(
        body, name="dwconv_dw", grid=(D // tc,),
        in_specs=[pl.BlockSpec((S, tc), lambda i: (0, i)), pl.BlockSpec((S, tc), lambda i: (0, i))],
        out_specs=pl.BlockSpec((CONV_HALO, tc), lambda i: (0, i)),
        out_shape=jax.ShapeDtypeStruct((CONV_HALO, D), F32),
        scratch_shapes=[pltpu.VMEM((S + 2 * CONV_HALO, tc), F32)],
        compiler_params=_cparams("parallel"),
    )(xin, dy)


def _place():
    x, y, c = lax.axis_index("x"), lax.axis_index("y"), lax.axis_index("c")
    chips = [(1 - x, y), (x, 1 - y), (1 - x, 1 - y)]
    return x, y, c, chips


def _remote(src, dst, ssem, rsem, dev):
    return pltpu.make_async_remote_copy(src_ref=src, dst_ref=dst, send_sem=ssem, recv_sem=rsem,
                                        device_id=dev, device_id_type=MESH_IDS)


_ANY = pl.BlockSpec(memory_space=pl.ANY)


def _place_own(name, own, place):
    R, W = own.shape
    tr = _pick_rows(R)

    def body(p_ref, a_ref, o_ref):
        del p_ref
        o_ref[...] = a_ref[...]

    return pl.pallas_call(
        body, name=name,
        grid_spec=pltpu.PrefetchScalarGridSpec(
            num_scalar_prefetch=1, grid=(R // tr,),
            in_specs=[pl.BlockSpec((tr, W), lambda i, p: (i, 0))],
            out_specs=pl.BlockSpec((None, tr, W), lambda i, p: (p[1], i, 0))),
        out_shape=jax.ShapeDtypeStruct((N_CHIPS, R, W), own.dtype),
        compiler_params=_cparams("parallel"),
    )(place, own)


_HBM = pl.BlockSpec(memory_space=pltpu.HBM)
_SEM = pl.BlockSpec(memory_space=pltpu.SEMAPHORE)
GROUPS = ('qkv', 'ffn', 'pw1', 'dm')


def _half_rows(c, r0, n):
    return pl.ds(pl.multiple_of(r0 + c * (n // 2), SUBLANES), n // 2)


def _gather_start(name, wgs, layers, after):
    G, L = len(wgs), len(layers)

    def body(*refs):
        outs = refs[G + 1:]
        ssems, rsems, bufs = outs[:L], outs[L:2 * L], outs[2 * L:]
        x, y, c, chips = _place()
        me = 2 * x + y
        for li, pieces in enumerate(layers):
            for pi, (g, r0, n) in enumerate(pieces):
                blk = bufs[g].at[me, _half_rows(c, r0, n)]
                for j, (px, py) in enumerate(chips):
                    _remote(blk, blk, ssems[li].at[3 * pi + j], rsems[li].at[3 * pi + j], (px, py, c)).start()

    sem_shapes = [pltpu.SemaphoreType.DMA((3 * len(p),)) for p in layers]
    res = pl.pallas_call(
        body, name=name, in_specs=[_HBM] * G + [_ANY],
        out_specs=[_SEM] * (2 * L) + [_HBM] * G,
        out_shape=sem_shapes + sem_shapes + [pltpu.HBM(w.shape, w.dtype) for w in wgs],
        input_output_aliases={g: 2 * L + g for g in range(G)},
        compiler_params=pltpu.CompilerParams(has_side_effects=pltpu.SideEffectType.DATAFLOW_SIDE_EFFECTING),
    )(*[pltpu.with_memory_space_constraint(w, pltpu.HBM) for w in wgs], after)
    return res[:L], res[L:2 * L], list(res[2 * L:])


def _gather_wait(name, wgs, ssem, rsem, pieces, after):
    G = len(wgs)

    def body(*refs):
        ssem_ref, rsem_ref = refs[G], refs[G + 1]
        bufs = refs[G + 3:]
        x, y, c, chips = _place()
        me = 2 * x + y
        for pi, (g, r0, n) in enumerate(pieces):
            rows = _half_rows(c, r0, n)
            for j, (px, py) in enumerate(chips):
                cp = _remote(bufs[g].at[me, rows], bufs[g].at[2 * px + py, rows],
                             ssem_ref.at[3 * pi + j], rsem_ref.at[3 * pi + j], (px, py, c))
                cp.wait_send()
                cp.wait_recv()

    return list(pl.pallas_call(
        body, name=name, in_specs=[_HBM] * G + [_SEM, _SEM, _ANY], out_specs=[_HBM] * G,
        out_shape=[pltpu.HBM(w.shape, w.dtype) for w in wgs],
        input_output_aliases={g: g for g in range(G)},
        compiler_params=pltpu.CompilerParams(has_side_effects=pltpu.SideEffectType.DATAFLOW_SIDE_EFFECTING),
    )(*wgs, ssem, rsem, after))


def _gather_forward(name, wgs, pieces):
    G = len(wgs)
    n_cp = 3 * len(pieces)

    def body(*refs):
        bufs, ssems, rsems = refs[G:2 * G], refs[2 * G], refs[2 * G + 1]
        x, y, c, chips = _place()
        sib = (x, y, 1 - c)
        cps = []
        for pi, (g, r0, n) in enumerate(pieces):
            for j, (px, py) in enumerate(chips):
                blk = bufs[g].at[2 * px + py, _half_rows(c, r0, n)]
                cps.append(_remote(blk, blk, ssems.at[3 * pi + j], rsems.at[3 * pi + j], sib))
        for cp in cps:
            cp.start()
        for pi, (g, r0, n) in enumerate(pieces):
            for j, (px, py) in enumerate(chips):
                blk = bufs[g].at[2 * px + py, _half_rows(1 - c, r0, n)]
                _remote(blk, blk, ssems.at[3 * pi + j], rsems.at[3 * pi + j], sib).wait_recv()
        for cp in cps:
            cp.wait_send()

    return list(pl.pallas_call(
        body, name=name, in_specs=[_ANY] * G, out_specs=[_ANY] * G,
        out_shape=[jax.ShapeDtypeStruct(w.shape, w.dtype) for w in wgs],
        input_output_aliases={g: g for g in range(G)},
        scratch_shapes=[pltpu.SemaphoreType.DMA((n_cp,)), pltpu.SemaphoreType.DMA((n_cp,))],
        compiler_params=pltpu.CompilerParams(has_side_effects=True),
    )(*wgs))


def _swap_halves(name, dgs, pieces):
    G = len(dgs)
    n_cp = N_CHIPS * len(pieces)

    def body(*refs):
        srcs, lands, ssems, rsems = refs[:G], refs[G:G + len(pieces)], refs[-2], refs[-1]
        x, y, c, _ = _place()
        cps = [_remote(srcs[g].at[j, _half_rows(1 - c, r0, n)], lands[pi].at[j],
                       ssems.at[N_CHIPS * pi + j], rsems.at[N_CHIPS * pi + j], (x, y, 1 - c))
               for pi, (g, r0, n) in enumerate(pieces) for j in range(N_CHIPS)]
        for cp in cps:
            cp.start()
        for cp in cps:
            cp.wait()

    return list(pl.pallas_call(
        body, name=name, in_specs=[_ANY] * G, out_specs=[_ANY] * len(pieces),
        out_shape=[jax.ShapeDtypeStruct((N_CHIPS, n // 2, dgs[g].shape[2]), dgs[g].dtype) for g, _, n in pieces],
        scratch_shapes=[pltpu.SemaphoreType.DMA((n_cp,)), pltpu.SemaphoreType.DMA((n_cp,))],
        compiler_params=pltpu.CompilerParams(has_side_effects=True),
    )(*dgs))


def _scatter_start(name, pbs):
    P = len(pbs)

    def body(*refs):
        outs = refs[2 * P:]
        ssems, rsems, src, land = outs[0], outs[1], outs[2:2 + P], outs[2 + P:]
        x, y, c, chips = _place()
        me = 2 * x + y
        for pi in range(P):
            for j, (px, py) in enumerate(chips):
                _remote(src[pi].at[2 * px + py], land[pi].at[me], ssems.at[3 * pi + j], rsems.at[3 * pi + j], (px, py, c)).start()

    sems = pltpu.SemaphoreType.DMA((3 * P,))
    hbm = [pltpu.HBM(p.shape, p.dtype) for p in pbs]
    res = pl.pallas_call(
        body, name=name, in_specs=[_HBM] * (2 * P), out_specs=[_SEM, _SEM] + [_HBM] * (2 * P),
        out_shape=[sems, sems] + hbm + hbm,
        input_output_aliases={k: 2 + k for k in range(2 * P)},
        compiler_params=pltpu.CompilerParams(has_side_effects=pltpu.SideEffectType.DATAFLOW_SIDE_EFFECTING),
    )(*[pltpu.with_memory_space_constraint(p, pltpu.HBM) for p in pbs],
      *[pltpu.with_memory_space_constraint(lax.empty(p.shape, p.dtype), pltpu.HBM) for p in pbs])
    return res[0], res[1], list(res[2:2 + P]), list(res[2 + P:])


def _scatter_wait(name, ssem, rsem, pbs, lands, after):
    P = len(pbs)

    def body(*refs):
        ssems, rsems = refs[2 * P], refs[2 * P + 1]
        outs = refs[2 * P + 3:]
        src, land = outs[:P], outs[P:]
        x, y, c, chips = _place()
        for pi in range(P):
            for j, (px, py) in enumerate(chips):
                cp = _remote(src[pi].at[2 * px + py], land[pi].at[2 * px + py], ssems.at[3 * pi + j], rsems.at[3 * pi + j], (px, py, c))
                cp.wait_send()
                cp.wait_recv()

    hbm = [pltpu.HBM(p.shape, p.dtype) for p in pbs]
    res = pl.pallas_call(
        body, name=name, in_specs=[_HBM] * (2 * P) + [_SEM, _SEM, _ANY], out_specs=[_HBM] * (2 * P),
        out_shape=hbm + hbm, input_output_aliases={k: k for k in range(2 * P)},
        compiler_params=pltpu.CompilerParams(has_side_effects=pltpu.SideEffectType.DATAFLOW_SIDE_EFFECTING),
    )(*pbs, *lands, ssem, rsem, after)
    return list(res[:P]), list(res[P:])


def _share_halves(name, gfs, pieces):
    G = len(gfs)

    def body(*refs):
        outs, ssems, rsems = refs[G:2 * G], refs[2 * G], refs[2 * G + 1]
        x, y, c, _ = _place()
        cps = []
        for pi, (g, r0, n) in enumerate(pieces):
            mine = outs[g].at[_half_rows(c, r0, n)]
            cps.append(_remote(mine, mine, ssems.at[pi], rsems.at[pi], (x, y, 1 - c)))
        for cp in cps:
            cp.start()
        for pi, (g, r0, n) in enumerate(pieces):
            theirs = outs[g].at[_half_rows(1 - c, r0, n)]
            _remote(theirs, theirs, ssems.at[pi], rsems.at[pi], (x, y, 1 - c)).wait_recv()
        for cp in cps:
            cp.wait_send()

    return list(pl.pallas_call(
        body, name=name, in_specs=[_ANY] * G, out_specs=[_ANY] * G,
        out_shape=[jax.ShapeDtypeStruct(gf.shape, gf.dtype) for gf in gfs], input_output_aliases={g: g for g in range(G)},
        scratch_shapes=[pltpu.SemaphoreType.DMA((len(pieces),)), pltpu.SemaphoreType.DMA((len(pieces),))],
        compiler_params=pltpu.CompilerParams(has_side_effects=True),
    )(*gfs))


def _broadcast_small(name, buf, after=None):
    R = buf.shape[0]

    def body(src, *rest):
        out, ssems, rsems = rest[-3:]
        x, y, c, _ = _place()
        me = 4 * x + 2 * y + c
        out[me] = src[...]
        peers = []
        for mask in range(1, 8):
            fx, fy, fc = (mask >> 2) & 1, (mask >> 1) & 1, mask & 1
            peers.append((1 - x if fx else x, 1 - y if fy else y, 1 - c if fc else c))
        cps = [_remote(src, out.at[me], ssems.at[k], rsems.at[k], p) for k, p in enumerate(peers)]
        for cp in cps:
            cp.start()
        for k, (px, py, pc) in enumerate(peers):
            blk = out.at[4 * px + 2 * py + pc]
            _remote(blk, blk, ssems.at[k], rsems.at[k], (px, py, pc)).wait_recv()
        for cp in cps:
            cp.wait_send()

    return pl.pallas_call(
        body, name=name, in_specs=[pl.BlockSpec(memory_space=pltpu.VMEM)] + ([_ANY] if after is not None else []),
        out_specs=pl.BlockSpec(memory_space=pltpu.VMEM),
        out_shape=jax.ShapeDtypeStruct((8, R, LANES), F32),
        scratch_shapes=[pltpu.SemaphoreType.DMA((7,)), pltpu.SemaphoreType.DMA((7,))],
        compiler_params=pltpu.CompilerParams(has_side_effects=True, vmem_limit_bytes=VMEM_LIMIT_V7X),
    )(buf, *([after] if after is not None else []))


def _sum_slabs(name, slabs):
    n, R, _ = slabs.shape

    def body(s_ref, o_ref):
        acc = s_ref[0]
        for k in range(1, n):
            acc = acc + s_ref[k]
        o_ref[...] = acc

    return pl.pallas_call(
        body, name=name, out_shape=jax.ShapeDtypeStruct((R, LANES), F32),
        in_specs=[pl.BlockSpec(memory_space=pltpu.VMEM)], out_specs=pl.BlockSpec(memory_space=pltpu.VMEM),
        compiler_params=_cparams(),
    )(slabs)


def _pick_rows(rows, target=512):
    best = SUBLANES
    for t in range(SUBLANES, min(rows, target) + 1, SUBLANES):
        if rows % t == 0:
            best = t
    return best


def _half_tile(r0, n):
    return _pick_rows(math.gcd(r0, n // 2) if r0 else n // 2)


def _pair_sum(name, dg, land, place, r0, n):
    W = dg.shape[2]
    tr = _half_tile(r0, n)
    nb = (n // 2) // tr

    def body(p_ref, a_ref, b_ref, o_ref):
        del p_ref
        o_ref[...] = (a_ref[...].astype(F32) + b_ref[...].astype(F32)).astype(o_ref.dtype)

    return pl.pallas_call(
        body, name=name,
        grid_spec=pltpu.PrefetchScalarGridSpec(
            num_scalar_prefetch=1, grid=(N_CHIPS, nb),
            in_specs=[pl.BlockSpec((None, tr, W), lambda j, i, p: (j, r0 // tr + p[0] * nb + i, 0)),
                      pl.BlockSpec((None, tr, W), lambda j, i, p: (j, i, 0))],
            out_specs=pl.BlockSpec((None, tr, W), lambda j, i, p: (j, i, 0))),
        out_shape=jax.ShapeDtypeStruct((N_CHIPS, n // 2, W), BF16),
        compiler_params=_cparams("parallel", "parallel"),
    )(place, dg, land)


def _chip_sum(name, pb, land, place, gf, r0, n):
    W = gf.shape[1]
    tr = _half_tile(r0, n)
    nb = (n // 2) // tr

    def body(p_ref, own_ref, lx_ref, ly_ref, ld_ref, gf_in, o_ref):
        del p_ref, gf_in
        o_ref[...] = ((own_ref[...].astype(F32) + lx_ref[...].astype(F32)) + ly_ref[...].astype(F32)) + ld_ref[...].astype(F32)

    slab = lambda flip: pl.BlockSpec((None, tr, W), lambda i, p, _f=flip: (p[1] ^ _f, i, 0))
    return pl.pallas_call(
        body, name=name,
        grid_spec=pltpu.PrefetchScalarGridSpec(
            num_scalar_prefetch=1, grid=(nb,),
            in_specs=[slab(0), slab(2), slab(1), slab(3), pl.BlockSpec(memory_space=pl.ANY)],
            out_specs=pl.BlockSpec((tr, W), lambda i, p: (r0 // tr + p[0] * nb + i, 0))),
        out_shape=jax.ShapeDtypeStruct(gf.shape, F32),
        input_output_aliases={5: 0},
        compiler_params=_cparams("parallel"),
    )(place, pb, land, land, land, gf)


def _ln_stats(z):
    mu = jnp.mean(z, axis=1, keepdims=True)
    zc = z - mu
    rstd = lax.rsqrt(jnp.mean(zc * zc, axis=1, keepdims=True) + LN_EPS)
    return zc * rstd, rstd


def _ln_fwd(name, xin, m, g, b):
    S, D = xin.shape

    def fn(x_, m_, g_, b_):
        xhat, rstd = _ln_stats(ALPHA * x_ + m_)
        y = xhat * g_ + b_
        return y, y, xhat, rstd

    return _rowwise(name, fn, [xin, m, ('full', g), ('full', b)],
                    [('rows', D, F32), ('rows', D, BF16), ('rows', D, F32), ('rows', 1, F32)], S)


def _ln_bwd_core(dy, xhat, rstd, g):
    dxh = dy * g
    return rstd * (dxh - jnp.mean(dxh, axis=1, keepdims=True) - xhat * jnp.mean(dxh * xhat, axis=1, keepdims=True))


def _ln_bwd(name, terms, xhat, rstd, g, after=None):
    S, D = xhat.shape
    scales = [s for _, s in terms]
    n = len(terms)

    def fn(*v):
        dy = v[0] * scales[0] if scales[0] != 1.0 else v[0]
        for t in range(1, n):
            dy = dy + (v[t] * scales[t] if scales[t] != 1.0 else v[t])
        xh, rs, g_ = v[n], v[n + 1], v[n + 2]
        dz = _ln_bwd_core(dy, xh, rs, g_)
        return dz, dz, jnp.sum(dy * xh, axis=0, keepdims=True), jnp.sum(dy, axis=0, keepdims=True)

    return _rowwise(name, fn, [a for a, _ in terms] + [xhat, rstd, ('full', g)] + ([('after', after)] if after is not None else []),
                    [('rows', D, F32), ('rows', D, BF16), ('acc', (1, D), F32), ('acc', (1, D), F32)], S)


def _adamw_math(w, g, m, v):
    m2 = ADAM_B1 * m + (1.0 - ADAM_B1) * g
    v2 = ADAM_B2 * v + (1.0 - ADAM_B2) * (g * g)
    m_hat = m2 / (1.0 - ADAM_B1 ** ADAM_STEP)
    v_hat = v2 / (1.0 - ADAM_B2 ** ADAM_STEP)
    delta = -ADAM_LR * (m_hat / (jnp.sqrt(v_hat) + ADAM_EPS) + ADAM_WD * w)
    return delta, m2, v2


def _adamw(name, w, gfull, row_start, m, v):
    rows, W = w.shape
    tr = math.gcd(math.gcd(rows, row_start), 256) if row_start else math.gcd(rows, 256)

    def fn(w_, g_, m_, v_):
        d, m2, v2 = _adamw_math(w_, g_, m_, v_)
        return g_, d, m2, v2

    return _rowwise(name, fn, [w, ('off', gfull, row_start // tr), m, v], [('rows', W, F32)] * 4, rows, tm=tr)


def _pack(arrs):
    flat = jnp.concatenate([a.reshape(-1).astype(F32) for a in arrs])
    tile = SUBLANES * LANES
    n = -(-flat.shape[0] // tile) * tile
    return jnp.pad(flat, (0, n - flat.shape[0])).reshape(-1, LANES)


def _unpack(buf, shapes):
    flat = buf.reshape(-1)
    out, pos = [], 0
    for shp in shapes:
        n = math.prod(shp)
        out.append(flat[pos:pos + n].reshape(shp))
        pos += n
    return out


BIG = ['fox_w_qkv', 'fox_w_o', 'rel_w_qkv', 'rel_w_o', 'conv_w_pw1', 'conv_w_pw2', 'ffn_w_gate', 'ffn_w_up', 'ffn_w_down']
SMALL_SHARDED = ['fox_w_f', 'conv_b_pw1', 'conv_w_dw', 'conv_b_dw', 'conv_ln_g', 'conv_ln_b', 'conv_b_pw2']
SMALL_SHARD_AXIS = {'fox_w_f': 1, 'conv_b_pw1': 1, 'conv_w_dw': 2, 'conv_b_dw': 1, 'conv_ln_g': 1, 'conv_ln_b': 1, 'conv_b_pw2': 1}
SMALL_REPL = ['fox_b_f', 'rel_bias', 'ln_mix_g', 'ln_mix_b', 'ln_ffn_g', 'ln_ffn_b']
SMALL = SMALL_SHARDED + SMALL_REPL
WEIGHTS = ['fox_w_qkv', 'fox_w_f', 'fox_b_f', 'fox_w_o', 'rel_w_qkv', 'rel_bias', 'rel_w_o', 'conv_w_pw1', 'conv_b_pw1',
           'conv_w_dw', 'conv_b_dw', 'conv_ln_g', 'conv_ln_b', 'conv_w_pw2', 'conv_b_pw2', 'ffn_w_gate', 'ffn_w_up',
           'ffn_w_down', 'ln_mix_g', 'ln_mix_b', 'ln_ffn_g', 'ln_ffn_b']


def kernel(x, fox_w_qkv, fox_w_f, fox_b_f, fox_w_o, rel_w_qkv, rel_bias, rel_w_o, conv_w_pw1, conv_b_pw1, conv_w_dw, conv_b_dw, conv_ln_g, conv_ln_b, conv_w_pw2, conv_b_pw2, ffn_w_gate, ffn_w_up, ffn_w_down, ln_mix_g, ln_mix_b, ln_ffn_g, ln_ffn_b, loss_target, m_fox_w_qkv, m_fox_w_f, m_fox_b_f, m_fox_w_o, m_rel_w_qkv, m_rel_bias, m_rel_w_o, m_conv_w_pw1, m_conv_b_pw1, m_conv_w_dw, m_conv_b_dw, m_conv_ln_g, m_conv_ln_b, m_conv_w_pw2, m_conv_b_pw2, m_ffn_w_gate, m_ffn_w_up, m_ffn_w_down, m_ln_mix_g, m_ln_mix_b, m_ln_ffn_g, m_ln_ffn_b, v_fox_w_qkv, v_fox_w_f, v_fox_b_f, v_fox_w_o, v_rel_w_qkv, v_rel_bias, v_rel_w_o, v_conv_w_pw1, v_conv_b_pw1, v_conv_w_dw, v_conv_b_dw, v_conv_ln_g, v_conv_ln_b, v_conv_w_pw2, v_conv_b_pw2, v_ffn_w_gate, v_ffn_w_up, v_ffn_w_down, v_ln_mix_g, v_ln_mix_b, v_ln_ffn_g, v_ln_ffn_b):
    A = dict(locals())
    Wt = {n: A[n] for n in WEIGHTS}
    Mo = {n: A['m_' + n] for n in WEIGHTS}
    Vo = {n: A['v_' + n] for n in WEIGHTS}

    _, S, D = x.shape
    H = D // HEAD_DIM
    Ds = D // N_CHIPS
    Nq = fox_w_qkv.shape[2]
    Np = conv_w_pw1.shape[2]
    Fs = ffn_w_gate.shape[2]
    my_x, my_y, my_c = lax.axis_index("x"), lax.axis_index("y"), lax.axis_index("c")
    my_chip = 2 * my_x + my_y
    place = jnp.stack([my_c, my_chip]).astype(jnp.int32)

    wo_base = DEPTH * Fs
    where = {
        'fox_w_qkv': ('qkv', 0), 'rel_w_qkv': ('qkv', N_FOX * D),
        'ffn_w_gate': ('ffn', 0), 'ffn_w_up': ('ffn', DEPTH * D),
        'conv_w_pw1': ('pw1', 0),
        'ffn_w_down': ('dm', 0), 'fox_w_o': ('dm', wo_base), 'rel_w_o': ('dm', wo_base + N_FOX * Ds),
        'conv_w_pw2': ('dm', wo_base + (N_FOX + 1) * Ds),
    }
    members = {'qkv': ['fox_w_qkv', 'rel_w_qkv'], 'ffn': ['ffn_w_gate', 'ffn_w_up'], 'pw1': ['conv_w_pw1'],
               'dm': ['ffn_w_down', 'fox_w_o', 'rel_w_o', 'conv_w_pw2']}
    flat2 = lambda a: a.reshape(-1, a.shape[-1])
    own = {g: jnp.concatenate([flat2(Wt[n]).astype(BF16) for n in ms], axis=0) for g, ms in members.items()}

    def layer_pieces(i):
        kind, j = i % 3, i // 3
        slot = j if kind == 0 else (N_FOX if kind == 1 else N_FOX + 1)
        w_in = (GROUPS.index('pw1'), 0, D) if kind == 2 else (GROUPS.index('qkv'), slot * D, D)
        return [w_in, (GROUPS.index('dm'), wo_base + slot * Ds, Ds), (GROUPS.index('ffn'), i * D, D),
                (GROUPS.index('ffn'), (DEPTH + i) * D, D), (GROUPS.index('dm'), i * Fs, Fs)]

    small_shapes = [Wt[n].shape for n in SMALL_SHARDED]
    slabs = _broadcast_small("gather_small", _pack([Wt[n] for n in SMALL_SHARDED]))

    stages = [part for i in range(DEPTH) for part in (layer_pieces(i)[:2], layer_pieces(i)[2:])]
    gstages = [part for i in range(DEPTH) for part in (layer_pieces(i)[:1], layer_pieces(i)[1:])]
    (g_first, r_first, n_first), = gstages[0]
    buf_first = _place_own("place_" + GROUPS[g_first], own[GROUPS[g_first]], place)
    ssems_first, rsems_first, (buf_first,) = _gather_start("gather_start_first", [buf_first], [[(0, r_first, n_first)]], slabs)
    ssems_rest, rsems_rest, wg_list = _gather_start(
        "gather_start_rest", [buf_first if gi == g_first else _place_own("place_" + g, own[g], place) for gi, g in enumerate(GROUPS)],
        gstages[1:], slabs)
    gather_ssems, gather_rsems = list(ssems_first) + list(ssems_rest), list(rsems_first) + list(rsems_rest)
    WG = dict(zip(GROUPS, wg_list))
    DG = {g: lax.empty(WG[g].shape, BF16) for g in own}

    per_chip = [_unpack(slabs[2 * j], small_shapes) for j in range(N_CHIPS)]
    full = {n: jnp.concatenate([per_chip[j][i] for j in range(N_CHIPS)], axis=SMALL_SHARD_AXIS[n])
            for i, n in enumerate(SMALL_SHARDED)}
    row = lambda v: v.reshape(1, -1)

    SG = {}

    def ffn_fwd(i, xb):
        hg, hu, act = _mm_gate_up(f"ffn{i}_gate_up", xb, WG['ffn'], i * D, (DEPTH + i) * D)
        f = _mm_row(f"ffn{i}_down", act, WG['dm'], i * Fs, Fs)
        return f, (hg, hu, act)

    def ffn_bwd(i, xb, saved, dzb):
        hg, hu, act = saved
        DG['dm'] = _mm_dw(f"ffn{i}_dw_down", act, dzb, DG['dm'], i * Fs, 'row')
        dhg, dhu = _mm_dact(f"ffn{i}_dact", dzb, WG['dm'], i * Fs, Fs, hg, hu)
        DG['ffn'] = _mm_dw(f"ffn{i}_dw_gate", xb, dhg, DG['ffn'], i * D, 'col')
        DG['ffn'] = _mm_dw(f"ffn{i}_dw_up", xb, dhu, DG['ffn'], (DEPTH + i) * D, 'col')
        return _mm_col_t(f"ffn{i}_dx", [(dhg, i * D), (dhu, (DEPTH + i) * D)], WG['ffn'], D)

    def fox_fwd(j, xb, rest_stage):
        qkv = _mm_col(f"fox{j}_qkv", xb, WG['qkv'], j * D)
        wf = full['fox_w_f'][j].astype(BF16)
        def gate_fn(x_, w_, b_):
            z_ = jnp.dot(x_, w_, preferred_element_type=F32) + b_
            return z_, jnp.minimum(z_, 0.0) - jnp.log(1.0 + jnp.exp(-jnp.abs(z_)))

        z, logf = _rowwise(f"fox{j}_gate", gate_fn, [xb, ('full', wf), ('full', row(fox_b_f[j]))],
                           [('rows', H, F32), ('rows', H, F32)], S)
        c = _cumsum_rows(f"fox{j}_cumsum", logf, False)
        crow = c.T.reshape(H, 1, S)
        o = _fox_fwd(qkv, c, crow, H)
        weights_ready(rest_stage, o)
        m = _mm_row(f"fox{j}_wo", o, WG['dm'], wo_base + j * Ds, Ds)
        return m, (qkv, z, c, crow, o, wf)

    def fox_bwd(j, xb, saved, dzb):
        qkv, z, c, crow, o, wf = saved
        DG['dm'] = _mm_dw(f"fox{j}_dw_o", o, dzb, DG['dm'], wo_base + j * Ds, 'row')
        do = _mm_row_t(f"fox{j}_do", dzb, WG['dm'], wo_base + j * Ds, Ds, BF16)
        dq, dk, dv, dcrow = _fox_bwd(qkv, c, crow, do, H)
        dqkv = jnp.concatenate([dq, dk, dv], axis=1)
        dlogf = _cumsum_rows(f"fox{j}_rcumsum", dcrow.reshape(H, S).T, True)

        def fn(x_, dl_, z_, w_):
            dz_ = dl_ * _sigmoid(-z_)
            dzb_ = dz_.astype(BF16)
            return _dot_nt(dzb_, w_), _dot_tn(x_, dzb_), jnp.sum(dz_, axis=0, keepdims=True)

        dh_f, dwf, dbf = _rowwise(f"fox{j}_gate_bwd", fn, [xb, dlogf, z, ('full', wf)],
                                  [('rows', D, F32), ('acc', (D, H), F32), ('acc', (1, H), F32)], S)
        SG.setdefault('fox_w_f', [None] * N_FOX)[j] = dwf
        SG.setdefault('fox_b_f', [None] * N_FOX)[j] = dbf.reshape(H)
        DG['qkv'] = _mm_dw(f"fox{j}_dw_qkv", xb, dqkv, DG['qkv'], j * D, 'col')
        dh = _mm_col_t(f"fox{j}_dx", [(dqkv, j * D)], WG['qkv'], D)
        return [dh, dh_f]

    def rel_fwd(xb, rest_stage):
        qkv = _mm_col("rel_qkv", xb, WG['qkv'], N_FOX * D)
        rb_pad = jnp.pad(rel_bias[0], ((0, 0), (0, REL_TABLE_PAD - REL_TABLE)))
        bias = _rel_window_bias(jnp.transpose(_rel_expand(rb_pad), (1, 0, 2)))
        o = _rel_fwd(qkv, bias, H)
        weights_ready(rest_stage, o)
        m = _mm_row("rel_wo", o, WG['dm'], wo_base + N_FOX * Ds, Ds)
        return m, (qkv, bias, o)

    def rel_bwd(xb, saved, dzb):
        qkv, bias, o = saved
        DG['dm'] = _mm_dw("rel_dw_o", o, dzb, DG['dm'], wo_base + N_FOX * Ds, 'row')
        do = _mm_row_t("rel_do", dzb, WG['dm'], wo_base + N_FOX * Ds, Ds, BF16)
        dq, dk, dv, dbias = _rel_bwd(qkv, bias, do, H)
        SG['rel_bias'] = _rel_reduce(jnp.transpose(dbias, (1, 0, 2)))[:, :REL_TABLE].reshape(1, H, REL_TABLE)
        dqkv = jnp.concatenate([dq, dk, dv], axis=1)
        DG['qkv'] = _mm_dw("rel_dw_qkv", xb, dqkv, DG['qkv'], N_FOX * D, 'col')
        return [_mm_col_t("rel_dx", [(dqkv, N_FOX * D)], WG['qkv'], D)]

    w_dw32 = jnp.pad(full['conv_w_dw'][0], ((0, CONV_HALO - CONV_K), (0, 0)))
    cg, cb = full['conv_ln_g'], full['conv_ln_b']

    def conv_fwd(xb, rest_stage):
        u = _mm_col("conv_pw1", xb, WG['pw1'], 0, bias=full['conv_b_pw1'], out_dtype=F32)
        u2, = _rowwise("conv_glu", lambda a_, g_: [a_ * _sigmoid(g_)],
                       [('cols', u, D, 0), ('cols', u, D, 1)], [('rows', D, F32)], S)
        yc = _dwconv("conv_dw", u2, w_dw32, full['conv_b_dw'], False)

        def fn(y_, g_, b_):
            xhat, rstd = _ln_stats(y_)
            ln = xhat * g_ + b_
            return ln * _sigmoid(ln), xhat, rstd

        zc, xhat, rstd = _rowwise("conv_ln_silu", fn, [yc, ('full', cg), ('full', cb)],
                                  [('rows', D, BF16), ('rows', D, F32), ('rows', 1, F32)], S)
        weights_ready(rest_stage, zc)
        m = _mm_row("conv_pw2", zc, WG['dm'], wo_base + (N_FOX + 1) * Ds, Ds, bias=full['conv_b_pw2'])
        return m, (u, u2, zc, xhat, rstd)

    def conv_bwd(xb, saved, dz, dzb):
        u, u2, zc, xhat, rstd = saved
        r0 = wo_base + (N_FOX + 1) * Ds
        DG['dm'] = _mm_dw("conv_dw_pw2", zc, dzb, DG['dm'], r0, 'row')
        dzc = _mm_row_t("conv_dzc", dzb, WG['dm'], r0, Ds, F32)

        def fn(dm_, dzc_, xh_, rs_, g_, b_):
            ln = xh_ * g_ + b_
            sg = _sigmoid(ln)
            dln = dzc_ * (sg * (1.0 + ln * (1.0 - sg)))
            dyc = _ln_bwd_core(dln, xh_, rs_, g_)
            col = lambda t: jnp.sum(t, axis=0, keepdims=True)
            return dyc, col(dm_), col(dln * xh_), col(dln), col(dyc)

        dyc, SG['conv_b_pw2'], SG['conv_ln_g'], SG['conv_ln_b'], SG['conv_b_dw'] = _rowwise(
            "conv_ln_silu_bwd", fn, [dz, dzc, xhat, rstd, ('full', cg), ('full', cb)],
            [('rows', D, F32)] + [('acc', (1, D), F32)] * 4, S)
        du2 = _dwconv("conv_dw_bwd_x", dyc, w_dw32, jnp.zeros((1, D), F32), True)
        SG['conv_w_dw'] = _dwconv_dw(u2, dyc)[:CONV_K].reshape(1, CONV_K, D)

        def fn2(du2_, a_, g_):
            sg = _sigmoid(g_)
            da, dgt = du2_ * sg, du2_ * a_ * sg * (1.0 - sg)
            return da, dgt, jnp.sum(da, axis=0, keepdims=True), jnp.sum(dgt, axis=0, keepdims=True)

        da, dgt, dba, dbg = _rowwise("conv_glu_bwd", fn2, [du2, ('cols', u, D, 0), ('cols', u, D, 1)],
                                     [('rows', D, BF16), ('rows', D, BF16), ('acc', (1, D), F32), ('acc', (1, D), F32)], S)
        SG['conv_b_pw1'] = jnp.concatenate([dba, dbg], axis=1)
        du = jnp.concatenate([da, dgt], axis=1)
        DG['pw1'] = _mm_dw("conv_dw_pw1", xb, du, DG['pw1'], 0, 'col')
        return [_mm_col_t("conv_dx", [(du, 0)], WG['pw1'], D)]

    xs = x[0]
    xs_b = xs.astype(BF16)
    tape = []

    def weights_ready(s, after):
        bufs = _gather_wait(f"gather_wait{s}", [WG[g] for g in GROUPS], gather_ssems[s], gather_rsems[s], gstages[s], after)
        WG.update(zip(GROUPS, _gather_forward(f"gather_fwd{s}", bufs, gstages[s])))

    for i in range(DEPTH):
        kind, j = i % 3, i // 3
        weights_ready(2 * i, xs)
        if kind == 0:
            m, msaved = fox_fwd(j, xs_b, 2 * i + 1)
        elif kind == 1:
            m, msaved = rel_fwd(xs_b, 2 * i + 1)
        else:
            m, msaved = conv_fwd(xs_b, 2 * i + 1)
        xm, xm_b, xhat1, rstd1 = _ln_fwd(f"ln_mix{i}", xs, m, row(ln_mix_g[i]), row(ln_mix_b[i]))
        f, fsaved = ffn_fwd(i, xm_b)
        xo, xo_b, xhat2, rstd2 = _ln_fwd(f"ln_ffn{i}", xm, f, row(ln_ffn_g[i]), row(ln_ffn_b[i]))
        tape.append((xs_b, msaved, xhat1, rstd1, xm_b, fsaved, xhat2, rstd2))
        xs, xs_b = xo, xo_b

    def loss_fn(y_, t_):
        e = y_ - t_
        return e * (1.0 / D), jnp.sum(e * e, axis=0, keepdims=True)

    dy, sq = _rowwise("loss", loss_fn, [xs, loss_target[0]], [('rows', D, F32), ('acc', (1, D), F32)], S)
    loss = lax.psum(jnp.sum(sq) * (0.5 / D), ("x", "y", "c"))

    GF = {g: lax.empty(WG[g].shape[1:], F32) for g in GROUPS}
    started = [None] * len(stages)

    def reduce_start(s):
        dgs = [DG[g] for g in GROUPS]
        lands = _swap_halves(f"pair_swap{s}", dgs, stages[s])
        pbs = [_pair_sum(f"pair_sum{s}_{pi}", dgs[g], lands[pi], place, r0, n) for pi, (g, r0, n) in enumerate(stages[s])]
        started[s] = _scatter_start(f"scatter_start{s}", pbs)
        token = started[s][2][0]
        if s + 1 < len(stages):
            reduce_finish(s + 1, token)
        return token

    def reduce_finish(s, after):
        ssem, rsem, pbs, lands2 = started[s]
        pbs, lands2 = _scatter_wait(f"scatter_wait{s}", ssem, rsem, pbs, lands2, after)
        for pi, (g, r0, n) in enumerate(stages[s]):
            GF[GROUPS[g]] = _chip_sum(f"chip_sum{s}_{pi}", pbs[pi], lands2[pi], place, GF[GROUPS[g]], r0, n)

    terms = [(dy, 1.0)]
    token = None
    g_mix, b_mix, g_ffn, b_ffn = [None] * DEPTH, [None] * DEPTH, [None] * DEPTH, [None] * DEPTH
    for i in reversed(range(DEPTH)):
        kind, j = i % 3, i // 3
        xin_b, msaved, xhat1, rstd1, xm_b, fsaved, xhat2, rstd2 = tape[i]
        dz2, dz2b, g_ffn[i], b_ffn[i] = _ln_bwd(f"ln_ffn{i}_bwd", terms, xhat2, rstd2, row(ln_ffn_g[i]), after=token)
        dx_ffn = ffn_bwd(i, xm_b, fsaved, dz2b)
        token = reduce_start(2 * i + 1)
        dz1, dz1b, g_mix[i], b_mix[i] = _ln_bwd(f"ln_mix{i}_bwd", [(dz2, ALPHA), (dx_ffn, 1.0)], xhat1, rstd1,
                                                row(ln_mix_g[i]), after=token)
        if kind == 0:
            mix_terms = fox_bwd(j, xin_b, msaved, dz1b)
        elif kind == 1:
            mix_terms = rel_bwd(xin_b, msaved, dz1b)
        else:
            mix_terms = conv_bwd(xin_b, msaved, dz1, dz1b)
        terms = [(dz1, ALPHA)] + [(t, 1.0) for t in mix_terms]
        token = reduce_start(2 * i)

    def gx_fn(*v):
        acc = v[0] * ALPHA
        for t in v[1:]:
            acc = acc + t
        return [acc]

    grad_x, = _rowwise("grad_x", gx_fn, [a for a, _ in terms], [('rows', D, F32)], S)
    grad_x = grad_x.reshape(1, S, D)

    SG['fox_w_f'] = jnp.stack(SG['fox_w_f'])
    SG['fox_b_f'] = jnp.stack(SG['fox_b_f'])
    SG['ln_mix_g'] = jnp.concatenate(g_mix, axis=0)
    SG['ln_mix_b'] = jnp.concatenate(b_mix, axis=0)
    SG['ln_ffn_g'] = jnp.concatenate(g_ffn, axis=0)
    SG['ln_ffn_b'] = jnp.concatenate(b_ffn, axis=0)

    grads, deltas, new_m, new_v = {}, {}, {}, {}

    reduce_finish(0, grad_x)
    links_idle = GF['dm']
    GF = dict(zip(GROUPS, _share_halves("pair_share", [GF[g] for g in GROUPS], [p for st in stages for p in st])))

    for n in BIG:
        g, r0 = where[n]
        outs = _adamw("adamw_" + n, flat2(Wt[n]), GF[g], r0, flat2(Mo[n]), flat2(Vo[n]))
        grads[n], deltas[n], new_m[n], new_v[n] = [o.reshape(Wt[n].shape) for o in outs]

    full_shapes = [SG[n].shape for n in SMALL]
    summed = _sum_slabs("small_sum", _broadcast_small("small_exchange", _pack([SG[n] for n in SMALL]), after=links_idle))
    gsm = dict(zip(SMALL, _unpack(summed, full_shapes)))
    for n in SMALL_SHARDED:
        ax = SMALL_SHARD_AXIS[n]
        width = Wt[n].shape[ax]
        gsm[n] = lax.dynamic_slice_in_dim(gsm[n], my_chip * width, width, axis=ax)
    own_shapes = [Wt[n].shape for n in SMALL]
    packed = [_pack([src[n] for n in SMALL]) for src in (Wt, gsm, Mo, Vo)]
    rows_small = packed[0].shape[0]

    def small_fn(w_, g_, m_, v_):
        return _adamw_math(w_, g_, m_, v_)

    sd, sm, sv = _rowwise("adamw_small", small_fn, packed, [('rows', LANES, F32)] * 3, rows_small, tm=rows_small)
    for n, d_, m_, v_ in zip(SMALL, _unpack(sd, own_shapes), _unpack(sm, own_shapes), _unpack(sv, own_shapes)):
        grads[n], deltas[n], new_m[n], new_v[n] = gsm[n], d_, m_, v_

    return (loss, grad_x, *[grads[n] for n in WEIGHTS], *[deltas[n] for n in WEIGHTS],
            *[new_m[n] for n in WEIGHTS], *[new_v[n] for n in WEIGHTS])
```

```python
import functools
import math

import jax
import jax.numpy as jnp
from jax import lax
from jax.experimental import pallas as pl
from jax.experimental.pallas import tpu as pltpu

F32 = jnp.float32
BF16 = jnp.bfloat16
MESH_IDS = pl.DeviceIdType.MESH
HIGHEST = lax.Precision.HIGHEST

N_CHIPS = 4
DEPTH = 4
N_FOX = 2
HEAD_DIM = 128
CHUNK = 64
LEFT_CHUNKS = 8
BAND_KEYS = (LEFT_CHUNKS + 1) * CHUNK
PAD_KEYS = LEFT_CHUNKS * CHUNK
REL_CLIP = 128
REL_TABLE = 2 * REL_CLIP + 1
REL_TABLE_PAD = 384
REL_QB = 4 * CHUNK
REL_WIN = REL_QB + PAD_KEYS
CONV_K = 31
CONV_HALO = 32
ALPHA = (2.0 * DEPTH) ** 0.25
LN_EPS = 1e-5
ADAM_LR, ADAM_B1, ADAM_B2, ADAM_EPS, ADAM_WD, ADAM_STEP = 0.001, 0.9, 0.999, 1e-08, 0.01, 10
NEG_BIG = -1e30
VMEM_LIMIT_V7X = 56 * 1024 * 1024
LANES = 128
SUBLANES = 8
MM_ROWS = 1024


def _cparams(*sem):
    return pltpu.CompilerParams(dimension_semantics=sem if sem else None, vmem_limit_bytes=VMEM_LIMIT_V7X)


def _pick(dim, target):
    best = None
    for t in range(LANES, min(dim, target) + 1, LANES):
        if dim % t == 0:
            best = t
    return best if best is not None else dim


def _dot_nt(a, b):
    return lax.dot_general(a, b, (((1,), (1,)), ((), ())), preferred_element_type=F32)


def _dot_tn(a, b):
    return lax.dot_general(a, b, (((0,), (0,)), ((), ())), preferred_element_type=F32)


def _sigmoid(z):
    return 1.0 / (1.0 + jnp.exp(-z))


def _rowwise(name, fn, ins, outs, S, tm=256):
    tm = min(tm, S)
    afters = [it[1] for it in ins if isinstance(it, tuple) and it[0] == 'after']
    ins = [it for it in ins if not (isinstance(it, tuple) and it[0] == 'after')]
    arrs, in_specs = [], []
    for it in ins:
        if isinstance(it, tuple) and it[0] == 'full':
            a = it[1]
            in_specs.append(pl.BlockSpec(a.shape, lambda i, _n=a.ndim: (0,) * _n))
        elif isinstance(it, tuple) and it[0] == 'cols':
            _, a, width, blk = it
            in_specs.append(pl.BlockSpec((tm, width), lambda i, _b=blk: (i, _b)))
        elif isinstance(it, tuple) and it[0] == 'off':
            _, a, off = it
            in_specs.append(pl.BlockSpec((tm, a.shape[1]), lambda i, _o=off: (i + _o, 0)))
        else:
            a = it
            in_specs.append(pl.BlockSpec((tm, a.shape[1]), lambda i: (i, 0)))
        arrs.append(a)
    out_shape, out_specs = [], []
    for kind, shp, dt in outs:
        if kind == 'rows':
            out_shape.append(jax.ShapeDtypeStruct((S, shp), dt))
            out_specs.append(pl.BlockSpec((tm, shp), lambda i: (i, 0)))
        else:
            out_shape.append(jax.ShapeDtypeStruct(shp, dt))
            out_specs.append(pl.BlockSpec(shp, lambda i, _n=len(shp): (0,) * _n))
    n_in = len(arrs)
    in_specs += [pl.BlockSpec(memory_space=pl.ANY)] * len(afters)

    def body(*refs):
        vals = fn(*[r[...] for r in refs[:n_in]])
        first = pl.program_id(0) == 0
        for (kind, _, _), r, v in zip(outs, refs[n_in + len(afters):], vals):
            if kind == 'rows':
                r[...] = v.astype(r.dtype)
            else:
                @pl.when(first)
                def _(r=r, v=v):
                    r[...] = v.astype(r.dtype)

                @pl.when(jnp.logical_not(first))
                def _(r=r, v=v):
                    r[...] += v.astype(r.dtype)

    has_acc = any(k != 'rows' for k, _, _ in outs)
    res = pl.pallas_call(
        body, name=name, grid=(S // tm,), in_specs=in_specs, out_specs=out_specs, out_shape=out_shape,
        compiler_params=_cparams("arbitrary" if has_acc else "parallel"),
    )(*arrs, *afters)
    return res


def _mm_col(name, a, wg, row_start, bias=None, out_dtype=BF16):
    S, K = a.shape
    _, _, Ns = wg.shape
    rb = row_start // K
    tm = min(MM_ROWS, S)

    def body(a_ref, w_ref, *rest):
        acc = jnp.dot(a_ref[...].astype(BF16), w_ref[...], preferred_element_type=F32)
        if bias is not None:
            acc = acc + rest[0][...]
        rest[-1][...] = acc.astype(out_dtype)

    in_specs = [pl.BlockSpec((tm, K), lambda j, m: (m, 0)), pl.BlockSpec((None, K, Ns), lambda j, m: (j, rb, 0))]
    args = [a, wg]
    if bias is not None:
        in_specs.append(pl.BlockSpec((1, Ns), lambda j, m: (0, j)))
        args.append(bias)
    return pl.pallas_call(
        body, name=name, grid=(N_CHIPS, S // tm), in_specs=in_specs,
        out_specs=pl.BlockSpec((tm, Ns), lambda j, m: (m, j)),
        out_shape=jax.ShapeDtypeStruct((S, N_CHIPS * Ns), out_dtype),
        compiler_params=_cparams("parallel", "parallel"),
    )(*args)


def _mm_row(name, a, wg, row_start, Ks, bias=None):
    S = a.shape[0]
    N = wg.shape[2]
    rb = row_start // Ks
    tm = min(MM_ROWS, S)

    def body(a_ref, w_ref, *rest):
        o_ref = rest[-1]
        j = pl.program_id(1)
        d = jnp.dot(a_ref[...].astype(BF16), w_ref[...], preferred_element_type=F32)

        @pl.when(j == 0)
        def _():
            o_ref[...] = d + rest[0][...] if bias is not None else d

        @pl.when(j > 0)
        def _():
            o_ref[...] += d

    in_specs = [pl.BlockSpec((tm, Ks), lambda m, j: (m, j)), pl.BlockSpec((None, Ks, N), lambda m, j: (j, rb, 0))]
    args = [a, wg]
    if bias is not None:
        in_specs.append(pl.BlockSpec((1, N), lambda m, j: (0, 0)))
        args.append(bias)
    return pl.pallas_call(
        body, name=name, grid=(S // tm, N_CHIPS), in_specs=in_specs,
        out_specs=pl.BlockSpec((tm, N), lambda m, j: (m, 0)),
        out_shape=jax.ShapeDtypeStruct((S, N), F32),
        compiler_params=_cparams("parallel", "arbitrary"),
    )(*args)


def _mm_col_t(name, pairs, wg, K):
    S = pairs[0][0].shape[0]
    Ns = wg.shape[2]
    tm = min(MM_ROWS, S)
    n = len(pairs)
    tkk = K // n if (K // n) % LANES == 0 else K

    def body(*refs):
        o_ref = refs[-1]
        j = pl.program_id(2)
        d = _dot_nt(refs[0][...], refs[n][...])
        for p in range(1, n):
            d = d + _dot_nt(refs[p][...], refs[n + p][...])

        @pl.when(j == 0)
        def _():
            o_ref[...] = d

        @pl.when(j > 0)
        def _():
            o_ref[...] += d

    in_specs = [pl.BlockSpec((tm, Ns), lambda m, kb, j: (m, j)) for _ in pairs]
    in_specs += [pl.BlockSpec((None, tkk, Ns), lambda m, kb, j, _rb=rs // tkk: (j, _rb + kb, 0)) for _, rs in pairs]
    return pl.pallas_call(
        body, name=name, grid=(S // tm, K // tkk, N_CHIPS), in_specs=in_specs,
        out_specs=pl.BlockSpec((tm, tkk), lambda m, kb, j: (m, kb)),
        out_shape=jax.ShapeDtypeStruct((S, K), F32),
        compiler_params=_cparams("parallel", "parallel", "arbitrary"),
    )(*[dy for dy, _ in pairs], *[wg for _ in pairs])


def _mm_row_t(name, dy, wg, row_start, Ks, out_dtype):
    S, N = dy.shape
    rb = row_start // Ks
    tm = min(MM_ROWS, S)

    def body(dy_ref, w_ref, o_ref):
        o_ref[...] = _dot_nt(dy_ref[...], w_ref[...]).astype(out_dtype)

    return pl.pallas_call(
        body, name=name, grid=(N_CHIPS, S // tm),
        in_specs=[pl.BlockSpec((tm, N), lambda j, m: (m, 0)), pl.BlockSpec((None, Ks, N), lambda j, m: (j, rb, 0))],
        out_specs=pl.BlockSpec((tm, Ks), lambda j, m: (m, j)),
        out_shape=jax.ShapeDtypeStruct((S, N_CHIPS * Ks), out_dtype),
        compiler_params=_cparams("parallel", "parallel"),
    )(dy, wg)


def _silu_parts(g):
    sg = _sigmoid(g)
    return g * sg, sg * (1.0 + g * (1.0 - sg))


def _mm_gate_up(name, a, wg, gate_row, up_row):
    S, K = a.shape
    Ns = wg.shape[2]
    tm = min(MM_ROWS // 2, S)

    def body(a_ref, wg_ref, wu_ref, hg_ref, hu_ref, act_ref):
        a_ = a_ref[...]
        hg = jnp.dot(a_, wg_ref[...], preferred_element_type=F32).astype(BF16)
        hu = jnp.dot(a_, wu_ref[...], preferred_element_type=F32).astype(BF16)
        hg_ref[...] = hg
        hu_ref[...] = hu
        act_ref[...] = (_silu_parts(hg.astype(F32))[0] * hu.astype(F32)).astype(BF16)

    out = jax.ShapeDtypeStruct((S, N_CHIPS * Ns), BF16)
    w_spec = lambda rb: pl.BlockSpec((None, K, Ns), lambda j, m: (j, rb, 0))
    o_spec = pl.BlockSpec((tm, Ns), lambda j, m: (m, j))
    return pl.pallas_call(
        body, name=name, grid=(N_CHIPS, S // tm),
        in_specs=[pl.BlockSpec((tm, K), lambda j, m: (m, 0)), w_spec(gate_row // K), w_spec(up_row // K)],
        out_specs=[o_spec, o_spec, o_spec], out_shape=[out, out, out],
        compiler_params=_cparams("parallel", "parallel"),
    )(a, wg, wg)


def _mm_dact(name, dy, wg, row_start, Ks, hg, hu):
    S, N = dy.shape
    rb = row_start // Ks
    tm = min(512, S)

    def body(dy_ref, w_ref, hg_ref, hu_ref, dhg_ref, dhu_ref):
        dact = _dot_nt(dy_ref[...], w_ref[...])
        silu, dsilu = _silu_parts(hg_ref[...].astype(F32))
        dhg_ref[...] = (dact * hu_ref[...].astype(F32) * dsilu).astype(BF16)
        dhu_ref[...] = (dact * silu).astype(BF16)

    out = jax.ShapeDtypeStruct((S, N_CHIPS * Ks), BF16)
    t_spec = pl.BlockSpec((tm, Ks), lambda j, m: (m, j))
    return pl.pallas_call(
        body, name=name, grid=(N_CHIPS, S // tm),
        in_specs=[pl.BlockSpec((tm, N), lambda j, m: (m, 0)), pl.BlockSpec((None, Ks, N), lambda j, m: (j, rb, 0)), t_spec, t_spec],
        out_specs=[t_spec, t_spec], out_shape=[out, out],
        compiler_params=_cparams("parallel", "parallel"),
    )(dy, wg, hg, hu)


def _mm_dw(name, a, dy, dg, row_start, kind):
    S = a.shape[0]
    _, _, W = dg.shape
    if kind == 'col':
        K = a.shape[1]
        rows = K
        tk, tn = _pick(K, MM_ROWS), W
        a_map = lambda j, nb, kb: (0, kb)
        dy_map = lambda j, nb, kb: (0, j * (W // tn) + nb)
    else:
        rows = a.shape[1] // N_CHIPS
        tk = rows if rows * S * 2 * 2 <= 12 * 1024 * 1024 else _pick(rows, 512)
        tn = _pick(W, 1024)
        a_map = lambda j, nb, kb: (0, j * (rows // tk) + kb)
        dy_map = lambda j, nb, kb: (0, nb)
    rb = row_start // tk
    assert row_start % tk == 0

    def body(a_ref, dy_ref, dg_in, o_ref):
        del dg_in
        o_ref[...] = _dot_tn(a_ref[...], dy_ref[...]).astype(o_ref.dtype)

    return pl.pallas_call(
        body, name=name, grid=(N_CHIPS, W // tn, rows // tk),
        in_specs=[pl.BlockSpec((S, tk), a_map), pl.BlockSpec((S, tn), dy_map), pl.BlockSpec(memory_space=pl.ANY)],
        out_specs=pl.BlockSpec((None, tk, tn), lambda j, nb, kb: (j, rb + kb, nb)),
        out_shape=jax.ShapeDtypeStruct(dg.shape, dg.dtype),
        input_output_aliases={2: 0},
        compiler_params=_cparams("parallel", "parallel", "parallel"),
    )(a, dy, dg)


def _fox_probs(q, k, c_blk, crow, h, qi, tq):
    n = k.shape[0]
    s = _dot_nt(q, k) * (HEAD_DIM ** -0.5)
    lane = lax.broadcasted_iota(jnp.int32, c_blk.shape, 1)
    ccol = jnp.sum(jnp.where(lane == h, c_blk, 0.0), axis=1, keepdims=True)
    s = s + (ccol - crow)
    t_idx = qi * tq + lax.broadcasted_iota(jnp.int32, (tq, n), 0)
    s_idx = lax.broadcasted_iota(jnp.int32, (tq, n), 1)
    s = jnp.where(s_idx <= t_idx, s, NEG_BIG)
    p = jnp.exp(s - jnp.max(s, axis=1, keepdims=True))
    return p * (1.0 / jnp.sum(p, axis=1, keepdims=True))


def _per_query_block(qi, nq, tq, fn):
    for qv in range(nq):
        @pl.when(qi == qv)
        def _(qv=qv):
            fn(qv, (qv + 1) * tq)


FOX_HEADS = 2


def _head_cols(a):
    return slice(a * HEAD_DIM, (a + 1) * HEAD_DIM)


def _fox_fwd(qkv, c, crow, H):
    S = qkv.shape[0]
    tq = min(256, S)
    heads = 2 * FOX_HEADS
    hw = heads * HEAD_DIM
    G = H // heads

    def body(q_ref, k_ref, v_ref, c_ref, crow_ref, o_ref):
        def block(qv, n):
            for a in range(heads):
                cols = _head_cols(a)
                p = _fox_probs(q_ref[:, cols], k_ref[0:n, cols], c_ref[...], crow_ref[a, :, 0:n],
                               heads * pl.program_id(0) + a, qv, tq)
                o_ref[:, cols] = jnp.dot(p.astype(BF16), v_ref[0:n, cols], preferred_element_type=F32).astype(o_ref.dtype)

        _per_query_block(pl.program_id(1), S // tq, tq, block)

    return pl.pallas_call(
        body, name="fox_attn_fwd", grid=(G, S // tq),
        in_specs=[pl.BlockSpec((tq, hw), lambda g, i: (i, g)),
                  pl.BlockSpec((S, hw), lambda g, i: (0, G + g)),
                  pl.BlockSpec((S, hw), lambda g, i: (0, 2 * G + g)),
                  pl.BlockSpec((tq, H), lambda g, i: (i, 0)),
                  pl.BlockSpec((heads, 1, S), lambda g, i: (g, 0, 0))],
        out_specs=pl.BlockSpec((tq, hw), lambda g, i: (i, g)),
        out_shape=jax.ShapeDtypeStruct((S, H * HEAD_DIM), BF16),
        compiler_params=_cparams("parallel", "parallel"),
    )(qkv, qkv, qkv, c, crow)


def _fox_bwd(qkv, c, crow, do, H):
    S = qkv.shape[0]
    tq = min(128, S)
    nq = S // tq
    hw = FOX_HEADS * HEAD_DIM
    G = H // FOX_HEADS

    def body(q_ref, k_ref, v_ref, c_ref, crow_ref, do_ref, dq_ref, dk_ref, dv_ref, dc_ref, dk_acc, dv_acc):
        qi = pl.program_id(1)

        @pl.when(qi == 0)
        def _():
            dk_acc[...] = jnp.zeros_like(dk_acc)
            dv_acc[...] = jnp.zeros_like(dv_acc)
            dc_ref[...] = jnp.zeros_like(dc_ref)

        def block(qv, n):
            for a in range(FOX_HEADS):
                cols = _head_cols(a)
                q, k, v, do_ = q_ref[:, cols], k_ref[0:n, cols], v_ref[0:n, cols], do_ref[:, cols]
                p = _fox_probs(q, k, c_ref[...], crow_ref[a, :, 0:n], FOX_HEADS * pl.program_id(0) + a, qv, tq)
                dv_acc[0:n, cols] += _dot_tn(p.astype(BF16), do_)
                dp = _dot_nt(do_, v)
                ds = p * (dp - jnp.sum(p * dp, axis=1, keepdims=True))
                dsb = (ds * (HEAD_DIM ** -0.5)).astype(BF16)
                dq_ref[:, cols] = jnp.dot(dsb, k, preferred_element_type=F32).astype(dq_ref.dtype)
                dk_acc[0:n, cols] += _dot_tn(dsb, q)
                dc_ref[a, :, 0:n] += -jnp.sum(ds, axis=0, keepdims=True)

        _per_query_block(qi, nq, tq, block)

        @pl.when(qi == nq - 1)
        def _():
            dk_ref[...] = dk_acc[...].astype(dk_ref.dtype)
            dv_ref[...] = dv_acc[...].astype(dv_ref.dtype)

    D = H * HEAD_DIM
    return pl.pallas_call(
        body, name="fox_attn_bwd", grid=(G, nq),
        in_specs=[pl.BlockSpec((tq, hw), lambda g, i: (i, g)),
                  pl.BlockSpec((S, hw), lambda g, i: (0, G + g)),
                  pl.BlockSpec((S, hw), lambda g, i: (0, 2 * G + g)),
                  pl.BlockSpec((tq, H), lambda g, i: (i, 0)),
                  pl.BlockSpec((FOX_HEADS, 1, S), lambda g, i: (g, 0, 0)),
                  pl.BlockSpec((tq, hw), lambda g, i: (i, g))],
        out_specs=[pl.BlockSpec((tq, hw), lambda g, i: (i, g)),
                   pl.BlockSpec((S, hw), lambda g, i: (0, g)),
                   pl.BlockSpec((S, hw), lambda g, i: (0, g)),
                   pl.BlockSpec((FOX_HEADS, 1, S), lambda g, i: (g, 0, 0))],
        out_shape=[jax.ShapeDtypeStruct((S, D), BF16), jax.ShapeDtypeStruct((S, D), BF16),
                   jax.ShapeDtypeStruct((S, D), BF16), jax.ShapeDtypeStruct((H, 1, S), F32)],
        scratch_shapes=[pltpu.VMEM((S, hw), F32), pltpu.VMEM((S, hw), F32)],
        compiler_params=_cparams("parallel", "arbitrary"),
    )(qkv, qkv, qkv, c, crow, do)


def _cumsum_rows(name, xin, reverse):
    S, H = xin.shape
    tb = min(256, S)
    nb = S // tb

    def body(x_ref, o_ref):
        r = lax.broadcasted_iota(jnp.int32, (tb, tb), 0)
        cidx = lax.broadcasted_iota(jnp.int32, (tb, tb), 1)
        tri = (r <= cidx if reverse else r >= cidx).astype(F32)

        def step(b, carry):
            bb = nb - 1 - b if reverse else b
            rows = pl.ds(pl.multiple_of(bb * tb, tb), tb)
            blk = x_ref[rows, :]
            o_ref[rows, :] = jnp.dot(tri, blk, precision=HIGHEST, preferred_element_type=F32) + carry
            return carry + jnp.sum(blk, axis=0, keepdims=True)

        lax.fori_loop(0, nb, step, jnp.zeros((1, H), F32))

    return pl.pallas_call(
        body, name=name, out_shape=jax.ShapeDtypeStruct((S, H), F32),
        in_specs=[pl.BlockSpec(memory_space=pltpu.VMEM)], out_specs=pl.BlockSpec(memory_space=pltpu.VMEM),
        compiler_params=_cparams(),
    )(xin)


def _rel_onehot(i, transposed):
    shp = (REL_TABLE_PAD, BAND_KEYS) if transposed else (BAND_KEYS, REL_TABLE_PAD)
    j = lax.broadcasted_iota(jnp.int32, shp, 1 if transposed else 0)
    r = lax.broadcasted_iota(jnp.int32, shp, 0 if transposed else 1)
    return (jnp.clip(PAD_KEYS + i - j, -REL_CLIP, REL_CLIP) + REL_CLIP == r).astype(F32)


def _rel_expand(rb_pad):
    H = rb_pad.shape[0]

    def body(rb_ref, o_ref):
        rb = rb_ref[...]
        hi = rb.astype(BF16)
        rest = rb - hi.astype(F32)
        mid = rest.astype(BF16)
        lo = (rest - mid.astype(F32)).astype(BF16)

        def step(i, _):
            onehot = _rel_onehot(i, True).astype(BF16)
            pick = lambda part: jnp.dot(part, onehot, preferred_element_type=F32)
            o_ref[i] = (pick(hi) + pick(mid)) + pick(lo)
            return 0
        lax.fori_loop(0, CHUNK, step, 0)

    return pl.pallas_call(
        body, name="rel_bias_expand", out_shape=jax.ShapeDtypeStruct((CHUNK, H, BAND_KEYS), F32),
        in_specs=[pl.BlockSpec(memory_space=pltpu.VMEM)], out_specs=pl.BlockSpec(memory_space=pltpu.VMEM),
        compiler_params=_cparams(),
    )(rb_pad)


def _rel_reduce(dbt):
    H = dbt.shape[1]

    near0 = PAD_KEYS - REL_CLIP
    assert near0 % LANES == 0 and REL_WIN - near0 == REL_TABLE_PAD

    def body(d_ref, o_ref):
        j = near0 + lax.broadcasted_iota(jnp.int32, (REL_TABLE_PAD, REL_TABLE_PAD), 0)
        r = lax.broadcasted_iota(jnp.int32, (REL_TABLE_PAD, REL_TABLE_PAD), 1)

        def step(i, carry):
            acc, far = carry
            onehot = (jnp.clip(PAD_KEYS + i - j, -REL_CLIP, REL_CLIP) + REL_CLIP == r).astype(BF16)
            d = d_ref[i]
            near = d[:, near0:]
            hi = near.astype(BF16)
            lo = (near - hi.astype(F32)).astype(BF16)
            acc = acc + (jnp.dot(hi, onehot, preferred_element_type=F32) + jnp.dot(lo, onehot, preferred_element_type=F32))
            return acc, far + jnp.sum(d[:, :near0], axis=1, keepdims=True)

        acc, far = lax.fori_loop(0, REL_QB, step, (jnp.zeros((H, REL_TABLE_PAD), F32), jnp.zeros((H, 1), F32)))
        col = lax.broadcasted_iota(jnp.int32, (H, REL_TABLE_PAD), 1)
        o_ref[...] = acc + jnp.where(col == REL_TABLE - 1, far, 0.0)

    return pl.pallas_call(
        body, name="rel_bias_reduce", out_shape=jax.ShapeDtypeStruct((H, REL_TABLE_PAD), F32),
        in_specs=[pl.BlockSpec(memory_space=pltpu.VMEM)], out_specs=pl.BlockSpec(memory_space=pltpu.VMEM),
        compiler_params=_cparams(),
    )(dbt)


def _rel_window_bias(bias):
    H = bias.shape[0]
    out = jnp.full((H, REL_QB, REL_WIN), NEG_BIG, F32)
    for a in range(REL_QB // CHUNK):
        out = out.at[:, a * CHUNK:(a + 1) * CHUNK, a * CHUNK:a * CHUNK + BAND_KEYS].set(bias)
    return out


def _rel_probs(q, kw, bias_w, t0):
    s = _dot_nt(q, kw) * (HEAD_DIM ** -0.5) + bias_w
    j = lax.broadcasted_iota(jnp.int32, (REL_QB, REL_WIN), 1)
    s = jnp.where(j >= PAD_KEYS - t0, s, NEG_BIG)
    p = jnp.exp(s - jnp.max(s, axis=1, keepdims=True))
    return p * (1.0 / jnp.sum(p, axis=1, keepdims=True))


def _rel_fwd(qkv, bias_w, H):
    S = qkv.shape[0]

    def body(q_ref, k_ref, v_ref, b_ref, o_ref, kpad, vpad):
        kpad[0:PAD_KEYS, :] = jnp.zeros((PAD_KEYS, HEAD_DIM), BF16)
        vpad[0:PAD_KEYS, :] = jnp.zeros((PAD_KEYS, HEAD_DIM), BF16)
        kpad[PAD_KEYS:PAD_KEYS + S, :] = k_ref[...]
        vpad[PAD_KEYS:PAD_KEYS + S, :] = v_ref[...]

        def block(n, _):
            t0 = pl.multiple_of(n * REL_QB, REL_QB)
            rows, win = pl.ds(t0, REL_QB), pl.ds(t0, REL_WIN)
            p = _rel_probs(q_ref[rows, :], kpad[win, :], b_ref[...], t0)
            o_ref[rows, :] = jnp.dot(p.astype(BF16), vpad[win, :], preferred_element_type=F32).astype(o_ref.dtype)
            return 0

        lax.fori_loop(0, S // REL_QB, block, 0, unroll=2)

    return pl.pallas_call(
        body, name="rel_attn_fwd", grid=(H,),
        in_specs=[pl.BlockSpec((S, HEAD_DIM), lambda h: (0, h)),
                  pl.BlockSpec((S, HEAD_DIM), lambda h: (0, H + h)),
                  pl.BlockSpec((S, HEAD_DIM), lambda h: (0, 2 * H + h)),
                  pl.BlockSpec((None, REL_QB, REL_WIN), lambda h: (h, 0, 0))],
        out_specs=pl.BlockSpec((S, HEAD_DIM), lambda h: (0, h)),
        out_shape=jax.ShapeDtypeStruct((S, H * HEAD_DIM), BF16),
        scratch_shapes=[pltpu.VMEM((S + PAD_KEYS, HEAD_DIM), BF16), pltpu.VMEM((S + PAD_KEYS, HEAD_DIM), BF16)],
        compiler_params=_cparams("parallel"),
    )(qkv, qkv, qkv, bias_w)


def _rel_bwd(qkv, bias_w, do, H):
    S = qkv.shape[0]
    D = H * HEAD_DIM

    def body(q_ref, k_ref, v_ref, b_ref, do_ref, dq_ref, dk_ref, dv_ref, db_ref, kpad, vpad, dkpad, dvpad):
        kpad[0:PAD_KEYS, :] = jnp.zeros((PAD_KEYS, HEAD_DIM), BF16)
        vpad[0:PAD_KEYS, :] = jnp.zeros((PAD_KEYS, HEAD_DIM), BF16)
        kpad[PAD_KEYS:PAD_KEYS + S, :] = k_ref[...]
        vpad[PAD_KEYS:PAD_KEYS + S, :] = v_ref[...]
        dkpad[...] = jnp.zeros_like(dkpad)
        dvpad[...] = jnp.zeros_like(dvpad)
        db_ref[...] = jnp.zeros_like(db_ref)

        def block(n, _):
            t0 = pl.multiple_of(n * REL_QB, REL_QB)
            rows, win = pl.ds(t0, REL_QB), pl.ds(t0, REL_WIN)
            q, kw, vw, do_ = q_ref[rows, :], kpad[win, :], vpad[win, :], do_ref[rows, :]
            p = _rel_probs(q, kw, b_ref[...], t0)
            dvpad[win, :] += _dot_tn(p.astype(BF16), do_)
            dp = _dot_nt(do_, vw)
            ds = p * (dp - jnp.sum(p * dp, axis=1, keepdims=True))
            db_ref[...] += ds
            dsb = (ds * (HEAD_DIM ** -0.5)).astype(BF16)
            dq_ref[rows, :] = jnp.dot(dsb, kw, preferred_element_type=F32).astype(dq_ref.dtype)
            dkpad[win, :] += _dot_tn(dsb, q)
            return 0

        lax.fori_loop(0, S // REL_QB, block, 0, unroll=2)
        dk_ref[...] = dkpad[PAD_KEYS:PAD_KEYS + S, :].astype(dk_ref.dtype)
        dv_ref[...] = dvpad[PAD_KEYS:PAD_KEYS + S, :].astype(dv_ref.dtype)

    head = lambda h: (0, h)
    return pl.pallas_call(
        body, name="rel_attn_bwd", grid=(H,),
        in_specs=[pl.BlockSpec((S, HEAD_DIM), head),
                  pl.BlockSpec((S, HEAD_DIM), lambda h: (0, H + h)),
                  pl.BlockSpec((S, HEAD_DIM), lambda h: (0, 2 * H + h)),
                  pl.BlockSpec((None, REL_QB, REL_WIN), lambda h: (h, 0, 0)),
                  pl.BlockSpec((S, HEAD_DIM), head)],
        out_specs=[pl.BlockSpec((S, HEAD_DIM), head), pl.BlockSpec((S, HEAD_DIM), head), pl.BlockSpec((S, HEAD_DIM), head),
                   pl.BlockSpec((None, REL_QB, REL_WIN), lambda h: (h, 0, 0))],
        out_shape=[jax.ShapeDtypeStruct((S, D), BF16), jax.ShapeDtypeStruct((S, D), BF16), jax.ShapeDtypeStruct((S, D), BF16),
                   jax.ShapeDtypeStruct((H, REL_QB, REL_WIN), F32)],
        scratch_shapes=[pltpu.VMEM((S + PAD_KEYS, HEAD_DIM), BF16), pltpu.VMEM((S + PAD_KEYS, HEAD_DIM), BF16),
                        pltpu.VMEM((S + PAD_KEYS, HEAD_DIM), F32), pltpu.VMEM((S + PAD_KEYS, HEAD_DIM), F32)],
        compiler_params=_cparams("parallel"),
    )(qkv, qkv, qkv, bias_w, do)


def _conv_taps(win, tt, reverse):
    n = tt + 2 * CONV_HALO
    for k in range(CONV_K):
        off = (CONV_K - 1 - k) if reverse else (k - (CONV_K - 1))
        sh = (-off) % n
        rolled = pltpu.roll(win, sh, 0) if sh else win
        yield k, rolled[CONV_HALO:CONV_HALO + tt, :]


def _fill_padded(pad_ref, x_ref, S):
    tc = pad_ref.shape[1]
    pad_ref[0:CONV_HALO, :] = jnp.zeros((CONV_HALO, tc), F32)
    pad_ref[CONV_HALO + S:CONV_HALO + S + CONV_HALO, :] = jnp.zeros((CONV_HALO, tc), F32)
    pad_ref[CONV_HALO:CONV_HALO + S, :] = x_ref[...]


def _dwconv(name, xin, w32, bias, reverse):
    S, D = xin.shape
    tc = min(256, D)
    tt = min(256, S)

    def body(x_ref, w_ref, b_ref, y_ref, pad_ref):
        _fill_padded(pad_ref, x_ref, S)
        def tile(ti, _):
            t0 = pl.multiple_of(ti * tt, tt)
            win = pad_ref[pl.ds(t0, tt + 2 * CONV_HALO), :]
            acc = jnp.zeros((tt, tc), F32) + b_ref[...]
            for k, shifted in _conv_taps(win, tt, reverse):
                acc = acc + w_ref[pl.ds(k, 1), :] * shifted
            y_ref[pl.ds(t0, tt), :] = acc
            return 0

        lax.fori_loop(0, S // tt, tile, 0)

    return pl.pallas_call(
        body, name=name, grid=(D // tc,),
        in_specs=[pl.BlockSpec((S, tc), lambda i: (0, i)), pl.BlockSpec((CONV_HALO, tc), lambda i: (0, i)),
                  pl.BlockSpec((1, tc), lambda i: (0, i))],
        out_specs=pl.BlockSpec((S, tc), lambda i: (0, i)),
        out_shape=jax.ShapeDtypeStruct((S, D), F32),
        scratch_shapes=[pltpu.VMEM((S + 2 * CONV_HALO, tc), F32)],
        compiler_params=_cparams("parallel"),
    )(xin, w32, bias)


def _dwconv_dw(xin, dy):
    S, D = xin.shape
    tc = min(256, D)
    tt = min(256, S)

    def body(x_ref, dy_ref, o_ref, pad_ref):
        _fill_padded(pad_ref, x_ref, S)

        def tile(ti, acc):
            t0 = pl.multiple_of(ti * tt, tt)
            win = pad_ref[pl.ds(t0, tt + 2 * CONV_HALO), :]
            dyt = dy_ref[pl.ds(t0, tt), :]
            ridx = lax.broadcasted_iota(jnp.int32, (CONV_HALO, tc), 0)
            upd = jnp.zeros((CONV_HALO, tc), F32)
            for k, shifted in _conv_taps(win, tt, False):
                upd = jnp.where(ridx == k, jnp.sum(dyt * shifted, axis=0, keepdims=True), upd)
            return acc + upd

        o_ref[...] = lax.fori_loop(0, S // tt, tile, jnp.zeros((CONV_HALO, tc), F32))

    return pl.pallas_call(
        body, name="dwconv_dw", grid=(D // tc,),
        in_specs=[pl.BlockSpec((S, tc), lambda i: (0, i)), pl.BlockSpec((S, tc), lambda i: (0, i))],
        out_specs=pl.BlockSpec((CONV_HALO, tc), lambda i: (0, i)),
        out_shape=jax.ShapeDtypeStruct((CONV_HALO, D), F32),
        scratch_shapes=[pltpu.VMEM((S + 2 * CONV_HALO, tc), F32)],
        compiler_params=_cparams("parallel"),
    )(xin, dy)


def _place():
    x, y, c = lax.axis_index("x"), lax.axis_index("y"), lax.axis_index("c")
    chips = [(1 - x, y), (x, 1 - y), (1 - x, 1 - y)]
    return x, y, c, chips


def _remote(src, dst, ssem, rsem, dev):
    return pltpu.make_async_remote_copy(src_ref=src, dst_ref=dst, send_sem=ssem, recv_sem=rsem,
                                        device_id=dev, device_id_type=MESH_IDS)


_ANY = pl.BlockSpec(memory_space=pl.ANY)


def _place_own(name, own, place):
    R, W = own.shape
    tr = _pick_rows(R)

    def body(p_ref, a_ref, o_ref):
        del p_ref
        o_ref[...] = a_ref[...]

    return pl.pallas_call(
        body, name=name,
        grid_spec=pltpu.PrefetchScalarGridSpec(
            num_scalar_prefetch=1, grid=(R // tr,),
            in_specs=[pl.BlockSpec((tr, W), lambda i, p: (i, 0))],
            out_specs=pl.BlockSpec((None, tr, W), lambda i, p: (p[1], i, 0))),
        out_shape=jax.ShapeDtypeStruct((N_CHIPS, R, W), own.dtype),
        compiler_params=_cparams("parallel"),
    )(place, own)


_HBM = pl.BlockSpec(memory_space=pltpu.HBM)
_SEM = pl.BlockSpec(memory_space=pltpu.SEMAPHORE)
GROUPS = ('qkv', 'ffn', 'pw1', 'dm')


def _half_rows(c, r0, n):
    return pl.ds(pl.multiple_of(r0 + c * (n // 2), SUBLANES), n // 2)


def _gather_start(name, wgs, layers, after):
    G, L = len(wgs), len(layers)

    def body(*refs):
        outs = refs[G + 1:]
        ssems, rsems, bufs = outs[:L], outs[L:2 * L], outs[2 * L:]
        x, y, c, chips = _place()
        me = 2 * x + y
        for li, pieces in enumerate(layers):
            for pi, (g, r0, n) in enumerate(pieces):
                blk = bufs[g].at[me, _half_rows(c, r0, n)]
                for j, (px, py) in enumerate(chips):
                    _remote(blk, blk, ssems[li].at[3 * pi + j], rsems[li].at[3 * pi + j], (px, py, c)).start()

    sem_shapes = [pltpu.SemaphoreType.DMA((3 * len(p),)) for p in layers]
    res = pl.pallas_call(
        body, name=name, in_specs=[_HBM] * G + [_ANY],
        out_specs=[_SEM] * (2 * L) + [_HBM] * G,
        out_shape=sem_shapes + sem_shapes + [pltpu.HBM(w.shape, w.dtype) for w in wgs],
        input_output_aliases={g: 2 * L + g for g in range(G)},
        compiler_params=pltpu.CompilerParams(has_side_effects=pltpu.SideEffectType.DATAFLOW_SIDE_EFFECTING),
    )(*[pltpu.with_memory_space_constraint(w, pltpu.HBM) for w in wgs], after)
    return res[:L], res[L:2 * L], list(res[2 * L:])


def _gather_wait(name, wgs, ssem, rsem, pieces, after):
    G = len(wgs)

    def body(*refs):
        ssem_ref, rsem_ref = refs[G], refs[G + 1]
        bufs = refs[G + 3:]
        x, y, c, chips = _place()
        me = 2 * x + y
        for pi, (g, r0, n) in enumerate(pieces):
            rows = _half_rows(c, r0, n)
            for j, (px, py) in enumerate(chips):
                cp = _remote(bufs[g].at[me, rows], bufs[g].at[2 * px + py, rows],
                             ssem_ref.at[3 * pi + j], rsem_ref.at[3 * pi + j], (px, py, c))
                cp.wait_send()
                cp.wait_recv()

    return list(pl.pallas_call(
        body, name=name, in_specs=[_HBM] * G + [_SEM, _SEM, _ANY], out_specs=[_HBM] * G,
        out_shape=[pltpu.HBM(w.shape, w.dtype) for w in wgs],
        input_output_aliases={g: g for g in range(G)},
        compiler_params=pltpu.CompilerParams(has_side_effects=pltpu.SideEffectType.DATAFLOW_SIDE_EFFECTING),
    )(*wgs, ssem, rsem, after))


def _gather_forward(name, wgs, pieces):
    G = len(wgs)
    n_cp = 3 * len(pieces)

    def body(*refs):
        bufs, ssems, rsems = refs[G:2 * G], refs[2 * G], refs[2 * G + 1]
        x, y, c, chips = _place()
        sib = (x, y, 1 - c)
        cps = []
        for pi, (g, r0, n) in enumerate(pieces):
            for j, (px, py) in enumerate(chips):
                blk = bufs[g].at[2 * px + py, _half_rows(c, r0, n)]
                cps.append(_remote(blk, blk, ssems.at[3 * pi + j], rsems.at[3 * pi + j], sib))
        for cp in cps:
            cp.start()
        for pi, (g, r0, n) in enumerate(pieces):
            for j, (px, py) in enumerate(chips):
                blk = bufs[g].at[2 * px + py, _half_rows(1 - c, r0, n)]
                _remote(blk, blk, ssems.at[3 * pi + j], rsems.at[3 * pi + j], sib).wait_recv()
        for cp in cps:
            cp.wait_send()

    return list(pl.pallas_call(
        body, name=name, in_specs=[_ANY] * G, out_specs=[_ANY] * G,
        out_shape=[jax.ShapeDtypeStruct(w.shape, w.dtype) for w in wgs],
        input_output_aliases={g: g for g in range(G)},
        scratch_shapes=[pltpu.SemaphoreType.DMA((n_cp,)), pltpu.SemaphoreType.DMA((n_cp,))],
        compiler_params=pltpu.CompilerParams(has_side_effects=True),
    )(*wgs))


def _swap_halves(name, dgs, pieces):
    G = len(dgs)
    n_cp = N_CHIPS * len(pieces)

    def body(*refs):
        srcs, lands, ssems, rsems = refs[:G], refs[G:G + len(pieces)], refs[-2], refs[-1]
        x, y, c, _ = _place()
        cps = [_remote(srcs[g].at[j, _half_rows(1 - c, r0, n)], lands[pi].at[j],
                       ssems.at[N_CHIPS * pi + j], rsems.at[N_CHIPS * pi + j], (x, y, 1 - c))
               for pi, (g, r0, n) in enumerate(pieces) for j in range(N_CHIPS)]
        for cp in cps:
            cp.start()
        for cp in cps:
            cp.wait()

    return list(pl.pallas_call(
        body, name=name, in_specs=[_ANY] * G, out_specs=[_ANY] * len(pieces),
        out_shape=[jax.ShapeDtypeStruct((N_CHIPS, n // 2, dgs[g].shape[2]), dgs[g].dtype) for g, _, n in pieces],
        scratch_shapes=[pltpu.SemaphoreType.DMA((n_cp,)), pltpu.SemaphoreType.DMA((n_cp,))],
        compiler_params=pltpu.CompilerParams(has_side_effects=True),
    )(*dgs))


def _scatter_start(name, pbs):
    P = len(pbs)

    def body(*refs):
        outs = refs[2 * P:]
        ssems, rsems, src, land = outs[0], outs[1], outs[2:2 + P], outs[2 + P:]
        x, y, c, chips = _place()
        me = 2 * x + y
        for pi in range(P):
            for j, (px, py) in enumerate(chips):
                _remote(src[pi].at[2 * px + py], land[pi].at[me], ssems.at[3 * pi + j], rsems.at[3 * pi + j], (px, py, c)).start()

    sems = pltpu.SemaphoreType.DMA((3 * P,))
    hbm = [pltpu.HBM(p.shape, p.dtype) for p in pbs]
    res = pl.pallas_call(
        body, name=name, in_specs=[_HBM] * (2 * P), out_specs=[_SEM, _SEM] + [_HBM] * (2 * P),
        out_shape=[sems, sems] + hbm + hbm,
        input_output_aliases={k: 2 + k for k in range(2 * P)},
        compiler_params=pltpu.CompilerParams(has_side_effects=pltpu.SideEffectType.DATAFLOW_SIDE_EFFECTING),
    )(*[pltpu.with_memory_space_constraint(p, pltpu.HBM) for p in pbs],
      *[pltpu.with_memory_space_constraint(lax.empty(p.shape, p.dtype), pltpu.HBM) for p in pbs])
    return res[0], res[1], list(res[2:2 + P]), list(res[2 + P:])


def _scatter_wait(name, ssem, rsem, pbs, lands, after):
    P = len(pbs)

    def body(*refs):
        ssems, rsems = refs[2 * P], refs[2 * P + 1]
        outs = refs[2 * P + 3:]
        src, land = outs[:P], outs[P:]
        x, y, c, chips = _place()
        for pi in range(P):
            for j, (px, py) in enumerate(chips):
                cp = _remote(src[pi].at[2 * px + py], land[pi].at[2 * px + py], ssems.at[3 * pi + j], rsems.at[3 * pi + j], (px, py, c))
                cp.wait_send()
                cp.wait_recv()

    hbm = [pltpu.HBM(p.shape, p.dtype) for p in pbs]
    res = pl.pallas_call(
        body, name=name, in_specs=[_HBM] * (2 * P) + [_SEM, _SEM, _ANY], out_specs=[_HBM] * (2 * P),
        out_shape=hbm + hbm, input_output_aliases={k: k for k in range(2 * P)},
        compiler_params=pltpu.CompilerParams(has_side_effects=pltpu.SideEffectType.DATAFLOW_SIDE_EFFECTING),
    )(*pbs, *lands, ssem, rsem, after)
    return list(res[:P]), list(res[P:])


def _share_halves(name, gfs, pieces):
    G = len(gfs)

    def body(*refs):
        outs, ssems, rsems = refs[G:2 * G], refs[2 * G], refs[2 * G + 1]
        x, y, c, _ = _place()
        cps = []
        for pi, (g, r0, n) in enumerate(pieces):
            mine = outs[g].at[_half_rows(c, r0, n)]
            cps.append(_remote(mine, mine, ssems.at[pi], rsems.at[pi], (x, y, 1 - c)))
        for cp in cps:
            cp.start()
        for pi, (g, r0, n) in enumerate(pieces):
            theirs = outs[g].at[_half_rows(1 - c, r0, n)]
            _remote(theirs, theirs, ssems.at[pi], rsems.at[pi], (x, y, 1 - c)).wait_recv()
        for cp in cps:
            cp.wait_send()

    return list(pl.pallas_call(
        body, name=name, in_specs=[_ANY] * G, out_specs=[_ANY] * G,
        out_shape=[jax.ShapeDtypeStruct(gf.shape, gf.dtype) for gf in gfs], input_output_aliases={g: g for g in range(G)},
        scratch_shapes=[pltpu.SemaphoreType.DMA((len(pieces),)), pltpu.SemaphoreType.DMA((len(pieces),))],
        compiler_params=pltpu.CompilerParams(has_side_effects=True),
    )(*gfs))


def _broadcast_small(name, buf, after=None):
    R = buf.shape[0]

    def body(src, *rest):
        out, ssems, rsems = rest[-3:]
        x, y, c, _ = _place()
        me = 4 * x + 2 * y + c
        out[me] = src[...]
        peers = []
        for mask in range(1, 8):
            fx, fy, fc = (mask >> 2) & 1, (mask >> 1) & 1, mask & 1
            peers.append((1 - x if fx else x, 1 - y if fy else y, 1 - c if fc else c))
        cps = [_remote(src, out.at[me], ssems.at[k], rsems.at[k], p) for k, p in enumerate(peers)]
        for cp in cps:
            cp.start()
        for k, (px, py, pc) in enumerate(peers):
            blk = out.at[4 * px + 2 * py + pc]
            _remote(blk, blk, ssems.at[k], rsems.at[k], (px, py, pc)).wait_recv()
        for cp in cps:
            cp.wait_send()

    return pl.pallas_call(
        body, name=name, in_specs=[pl.BlockSpec(memory_space=pltpu.VMEM)] + ([_ANY] if after is not None else []),
        out_specs=pl.BlockSpec(memory_space=pltpu.VMEM),
        out_shape=jax.ShapeDtypeStruct((8, R, LANES), F32),
        scratch_shapes=[pltpu.SemaphoreType.DMA((7,)), pltpu.SemaphoreType.DMA((7,))],
        compiler_params=pltpu.CompilerParams(has_side_effects=True, vmem_limit_bytes=VMEM_LIMIT_V7X),
    )(buf, *([after] if after is not None else []))


def _sum_slabs(name, slabs):
    n, R, _ = slabs.shape

    def body(s_ref, o_ref):
        acc = s_ref[0]
        for k in range(1, n):
            acc = acc + s_ref[k]
        o_ref[...] = acc

    return pl.pallas_call(
        body, name=name, out_shape=jax.ShapeDtypeStruct((R, LANES), F32),
        in_specs=[pl.BlockSpec(memory_space=pltpu.VMEM)], out_specs=pl.BlockSpec(memory_space=pltpu.VMEM),
        compiler_params=_cparams(),
    )(slabs)


def _pick_rows(rows, target=512):
    best = SUBLANES
    for t in range(SUBLANES, min(rows, target) + 1, SUBLANES):
        if rows % t == 0:
            best = t
    return best


def _half_tile(r0, n):
    return _pick_rows(math.gcd(r0, n // 2) if r0 else n // 2)


def _pair_sum(name, dg, land, place, r0, n):
    W = dg.shape[2]
    tr = _half_tile(r0, n)
    nb = (n // 2) // tr

    def body(p_ref, a_ref, b_ref, o_ref):
        del p_ref
        o_ref[...] = (a_ref[...].astype(F32) + b_ref[...].astype(F32)).astype(o_ref.dtype)

    return pl.pallas_call(
        body, name=name,
        grid_spec=pltpu.PrefetchScalarGridSpec(
            num_scalar_prefetch=1, grid=(N_CHIPS, nb),
            in_specs=[pl.BlockSpec((None, tr, W), lambda j, i, p: (j, r0 // tr + p[0] * nb + i, 0)),
                      pl.BlockSpec((None, tr, W), lambda j, i, p: (j, i, 0))],
            out_specs=pl.BlockSpec((None, tr, W), lambda j, i, p: (j, i, 0))),
        out_shape=jax.ShapeDtypeStruct((N_CHIPS, n // 2, W), BF16),
        compiler_params=_cparams("parallel", "parallel"),
    )(place, dg, land)


def _chip_sum(name, pb, land, place, gf, r0, n):
    W = gf.shape[1]
    tr = _half_tile(r0, n)
    nb = (n // 2) // tr

    def body(p_ref, own_ref, lx_ref, ly_ref, ld_ref, gf_in, o_ref):
        del p_ref, gf_in
        o_ref[...] = ((own_ref[...].astype(F32) + lx_ref[...].astype(F32)) + ly_ref[...].astype(F32)) + ld_ref[...].astype(F32)

    slab = lambda flip: pl.BlockSpec((None, tr, W), lambda i, p, _f=flip: (p[1] ^ _f, i, 0))
    return pl.pallas_call(
        body, name=name,
        grid_spec=pltpu.PrefetchScalarGridSpec(
            num_scalar_prefetch=1, grid=(nb,),
            in_specs=[slab(0), slab(2), slab(1), slab(3), pl.BlockSpec(memory_space=pl.ANY)],
            out_specs=pl.BlockSpec((tr, W), lambda i, p: (r0 // tr + p[0] * nb + i, 0))),
        out_shape=jax.ShapeDtypeStruct(gf.shape, F32),
        input_output_aliases={5: 0},
        compiler_params=_cparams("parallel"),
    )(place, pb, land, land, land, gf)


def _ln_stats(z):
    mu = jnp.mean(z, axis=1, keepdims=True)
    zc = z - mu
    rstd = lax.rsqrt(jnp.mean(zc * zc, axis=1, keepdims=True) + LN_EPS)
    return zc * rstd, rstd


def _ln_fwd(name, xin, m, g, b):
    S, D = xin.shape

    def fn(x_, m_, g_, b_):
        xhat, rstd = _ln_stats(ALPHA * x_ + m_)
        y = xhat * g_ + b_
        return y, y, xhat, rstd

    return _rowwise(name, fn, [xin, m, ('full', g), ('full', b)],
                    [('rows', D, F32), ('rows', D, BF16), ('rows', D, F32), ('rows', 1, F32)], S)


def _ln_bwd_core(dy, xhat, rstd, g):
    dxh = dy * g
    return rstd * (dxh - jnp.mean(dxh, axis=1, keepdims=True) - xhat * jnp.mean(dxh * xhat, axis=1, keepdims=True))


def _ln_bwd(name, terms, xhat, rstd, g, after=None):
    S, D = xhat.shape
    scales = [s for _, s in terms]
    n = len(terms)

    def fn(*v):
        dy = v[0] * scales[0] if scales[0] != 1.0 else v[0]
        for t in range(1, n):
            dy = dy + (v[t] * scales[t] if scales[t] != 1.0 else v[t])
        xh, rs, g_ = v[n], v[n + 1], v[n + 2]
        dz = _ln_bwd_core(dy, xh, rs, g_)
        return dz, dz, jnp.sum(dy * xh, axis=0, keepdims=True), jnp.sum(dy, axis=0, keepdims=True)

    return _rowwise(name, fn, [a for a, _ in terms] + [xhat, rstd, ('full', g)] + ([('after', after)] if after is not None else []),
                    [('rows', D, F32), ('rows', D, BF16), ('acc', (1, D), F32), ('acc', (1, D), F32)], S)


def _adamw_math(w, g, m, v):
    m2 = ADAM_B1 * m + (1.0 - ADAM_B1) * g
    v2 = ADAM_B2 * v + (1.0 - ADAM_B2) * (g * g)
    m_hat = m2 / (1.0 - ADAM_B1 ** ADAM_STEP)
    v_hat = v2 / (1.0 - ADAM_B2 ** ADAM_STEP)
    delta = -ADAM_LR * (m_hat / (jnp.sqrt(v_hat) + ADAM_EPS) + ADAM_WD * w)
    return delta, m2, v2


def _adamw(name, w, gfull, row_start, m, v):
    rows, W = w.shape
    tr = math.gcd(math.gcd(rows, row_start), 256) if row_start else math.gcd(rows, 256)

    def fn(w_, g_, m_, v_):
        d, m2, v2 = _adamw_math(w_, g_, m_, v_)
        return g_, d, m2, v2

    return _rowwise(name, fn, [w, ('off', gfull, row_start // tr), m, v], [('rows', W, F32)] * 4, rows, tm=tr)


def _pack(arrs):
    flat = jnp.concatenate([a.reshape(-1).astype(F32) for a in arrs])
    tile = SUBLANES * LANES
    n = -(-flat.shape[0] // tile) * tile
    return jnp.pad(flat, (0, n - flat.shape[0])).reshape(-1, LANES)


def _unpack(buf, shapes):
    flat = buf.reshape(-1)
    out, pos = [], 0
    for shp in shapes:
        n = math.prod(shp)
        out.append(flat[pos:pos + n].reshape(shp))
        pos += n
    return out


BIG = ['fox_w_qkv', 'fox_w_o', 'rel_w_qkv', 'rel_w_o', 'conv_w_pw1', 'conv_w_pw2', 'ffn_w_gate', 'ffn_w_up', 'ffn_w_down']
SMALL_SHARDED = ['fox_w_f', 'conv_b_pw1', 'conv_w_dw', 'conv_b_dw', 'conv_ln_g', 'conv_ln_b', 'conv_b_pw2']
SMALL_SHARD_AXIS = {'fox_w_f': 1, 'conv_b_pw1': 1, 'conv_w_dw': 2, 'conv_b_dw': 1, 'conv_ln_g': 1, 'conv_ln_b': 1, 'conv_b_pw2': 1}
SMALL_REPL = ['fox_b_f', 'rel_bias', 'ln_mix_g', 'ln_mix_b', 'ln_ffn_g', 'ln_ffn_b']
SMALL = SMALL_SHARDED + SMALL_REPL
WEIGHTS = ['fox_w_qkv', 'fox_w_f', 'fox_b_f', 'fox_w_o', 'rel_w_qkv', 'rel_bias', 'rel_w_o', 'conv_w_pw1', 'conv_b_pw1',
           'conv_w_dw', 'conv_b_dw', 'conv_ln_g', 'conv_ln_b', 'conv_w_pw2', 'conv_b_pw2', 'ffn_w_gate', 'ffn_w_up',
           'ffn_w_down', 'ln_mix_g', 'ln_mix_b', 'ln_ffn_g', 'ln_ffn_b']


def kernel(x, fox_w_qkv, fox_w_f, fox_b_f, fox_w_o, rel_w_qkv, rel_bias, rel_w_o, conv_w_pw1, conv_b_pw1, conv_w_dw, conv_b_dw, conv_ln_g, conv_ln_b, conv_w_pw2, conv_b_pw2, ffn_w_gate, ffn_w_up, ffn_w_down, ln_mix_g, ln_mix_b, ln_ffn_g, ln_ffn_b, loss_target, m_fox_w_qkv, m_fox_w_f, m_fox_b_f, m_fox_w_o, m_rel_w_qkv, m_rel_bias, m_rel_w_o, m_conv_w_pw1, m_conv_b_pw1, m_conv_w_dw, m_conv_b_dw, m_conv_ln_g, m_conv_ln_b, m_conv_w_pw2, m_conv_b_pw2, m_ffn_w_gate, m_ffn_w_up, m_ffn_w_down, m_ln_mix_g, m_ln_mix_b, m_ln_ffn_g, m_ln_ffn_b, v_fox_w_qkv, v_fox_w_f, v_fox_b_f, v_fox_w_o, v_rel_w_qkv, v_rel_bias, v_rel_w_o, v_conv_w_pw1, v_conv_b_pw1, v_conv_w_dw, v_conv_b_dw, v_conv_ln_g, v_conv_ln_b, v_conv_w_pw2, v_conv_b_pw2, v_ffn_w_gate, v_ffn_w_up, v_ffn_w_down, v_ln_mix_g, v_ln_mix_b, v_ln_ffn_g, v_ln_ffn_b):
    A = dict(locals())
    Wt = {n: A[n] for n in WEIGHTS}
    Mo = {n: A['m_' + n] for n in WEIGHTS}
    Vo = {n: A['v_' + n] for n in WEIGHTS}

    _, S, D = x.shape
    H = D // HEAD_DIM
    Ds = D // N_CHIPS
    Nq = fox_w_qkv.shape[2]
    Np = conv_w_pw1.shape[2]
    Fs = ffn_w_gate.shape[2]
    my_x, my_y, my_c = lax.axis_index("x"), lax.axis_index("y"), lax.axis_index("c")
    my_chip = 2 * my_x + my_y
    place = jnp.stack([my_c, my_chip]).astype(jnp.int32)

    wo_base = DEPTH * Fs
    where = {
        'fox_w_qkv': ('qkv', 0), 'rel_w_qkv': ('qkv', N_FOX * D),
        'ffn_w_gate': ('ffn', 0), 'ffn_w_up': ('ffn', DEPTH * D),
        'conv_w_pw1': ('pw1', 0),
        'ffn_w_down': ('dm', 0), 'fox_w_o': ('dm', wo_base), 'rel_w_o': ('dm', wo_base + N_FOX * Ds),
        'conv_w_pw2': ('dm', wo_base + (N_FOX + 1) * Ds),
    }
    members = {'qkv': ['fox_w_qkv', 'rel_w_qkv'], 'ffn': ['ffn_w_gate', 'ffn_w_up'], 'pw1': ['conv_w_pw1'],
               'dm': ['ffn_w_down', 'fox_w_o', 'rel_w_o', 'conv_w_pw2']}
    flat2 = lambda a: a.reshape(-1, a.shape[-1])
    own = {g: jnp.concatenate([flat2(Wt[n]).astype(BF16) for n in ms], axis=0) for g, ms in members.items()}

    def layer_pieces(i):
        kind, j = i % 3, i // 3
        slot = j if kind == 0 else (N_FOX if kind == 1 else N_FOX + 1)
        w_in = (GROUPS.index('pw1'), 0, D) if kind == 2 else (GROUPS.index('qkv'), slot * D, D)
        return [w_in, (GROUPS.index('dm'), wo_base + slot * Ds, Ds), (GROUPS.index('ffn'), i * D, D),
                (GROUPS.index('ffn'), (DEPTH + i) * D, D), (GROUPS.index('dm'), i * Fs, Fs)]

    small_shapes = [Wt[n].shape for n in SMALL_SHARDED]
    slabs = _broadcast_small("gather_small", _pack([Wt[n] for n in SMALL_SHARDED]))

    stages = [part for i in range(DEPTH) for part in (layer_pieces(i)[:2], layer_pieces(i)[2:])]
    gstages = [part for i in range(DEPTH) for part in (layer_pieces(i)[:1], layer_pieces(i)[1:])]
    (g_first, r_first, n_first), = gstages[0]
    buf_first = _place_own("place_" + GROUPS[g_first], own[GROUPS[g_first]], place)
    ssems_first, rsems_first, (buf_first,) = _gather_start("gather_start_first", [buf_first], [[(0, r_first, n_first)]], slabs)
    ssems_rest, rsems_rest, wg_list = _gather_start(
        "gather_start_rest", [buf_first if gi == g_first else _place_own("place_" + g, own[g], place) for gi, g in enumerate(GROUPS)],
        gstages[1:], slabs)
    gather_ssems, gather_rsems = list(ssems_first) + list(ssems_rest), list(rsems_first) + list(rsems_rest)
    WG = dict(zip(GROUPS, wg_list))
    DG = {g: lax.empty(WG[g].shape, BF16) for g in own}

    per_chip = [_unpack(slabs[2 * j], small_shapes) for j in range(N_CHIPS)]
    full = {n: jnp.concatenate([per_chip[j][i] for j in range(N_CHIPS)], axis=SMALL_SHARD_AXIS[n])
            for i, n in enumerate(SMALL_SHARDED)}
    row = lambda v: v.reshape(1, -1)

    SG = {}

    def ffn_fwd(i, xb):
        hg, hu, act = _mm_gate_up(f"ffn{i}_gate_up", xb, WG['ffn'], i * D, (DEPTH + i) * D)
        f = _mm_row(f"ffn{i}_down", act, WG['dm'], i * Fs, Fs)
        return f, (hg, hu, act)

    def ffn_bwd(i, xb, saved, dzb):
        hg, hu, act = saved
        DG['dm'] = _mm_dw(f"ffn{i}_dw_down", act, dzb, DG['dm'], i * Fs, 'row')
        dhg, dhu = _mm_dact(f"ffn{i}_dact", dzb, WG['dm'], i * Fs, Fs, hg, hu)
        DG['ffn'] = _mm_dw(f"ffn{i}_dw_gate", xb, dhg, DG['ffn'], i * D, 'col')
        DG['ffn'] = _mm_dw(f"ffn{i}_dw_up", xb, dhu, DG['ffn'], (DEPTH + i) * D, 'col')
        return _mm_col_t(f"ffn{i}_dx", [(dhg, i * D), (dhu, (DEPTH + i) * D)], WG['ffn'], D)

    def fox_fwd(j, xb, rest_stage):
        qkv = _mm_col(f"fox{j}_qkv", xb, WG['qkv'], j * D)
        wf = full['fox_w_f'][j].astype(BF16)
        def gate_fn(x_, w_, b_):
            z_ = jnp.dot(x_, w_, preferred_element_type=F32) + b_
            return z_, jnp.minimum(z_, 0.0) - jnp.log(1.0 + jnp.exp(-jnp.abs(z_)))

        z, logf = _rowwise(f"fox{j}_gate", gate_fn, [xb, ('full', wf), ('full', row(fox_b_f[j]))],
                           [('rows', H, F32), ('rows', H, F32)], S)
        c = _cumsum_rows(f"fox{j}_cumsum", logf, False)
        crow = c.T.reshape(H, 1, S)
        o = _fox_fwd(qkv, c, crow, H)
        weights_ready(rest_stage, o)
        m = _mm_row(f"fox{j}_wo", o, WG['dm'], wo_base + j * Ds, Ds)
        return m, (qkv, z, c, crow, o, wf)

    def fox_bwd(j, xb, saved, dzb):
        qkv, z, c, crow, o, wf = saved
        DG['dm'] = _mm_dw(f"fox{j}_dw_o", o, dzb, DG['dm'], wo_base + j * Ds, 'row')
        do = _mm_row_t(f"fox{j}_do", dzb, WG['dm'], wo_base + j * Ds, Ds, BF16)
        dq, dk, dv, dcrow = _fox_bwd(qkv, c, crow, do, H)
        dqkv = jnp.concatenate([dq, dk, dv], axis=1)
        dlogf = _cumsum_rows(f"fox{j}_rcumsum", dcrow.reshape(H, S).T, True)

        def fn(x_, dl_, z_, w_):
            dz_ = dl_ * _sigmoid(-z_)
            dzb_ = dz_.astype(BF16)
            return _dot_nt(dzb_, w_), _dot_tn(x_, dzb_), jnp.sum(dz_, axis=0, keepdims=True)

        dh_f, dwf, dbf = _rowwise(f"fox{j}_gate_bwd", fn, [xb, dlogf, z, ('full', wf)],
                                  [('rows', D, F32), ('acc', (D, H), F32), ('acc', (1, H), F32)], S)
        SG.setdefault('fox_w_f', [None] * N_FOX)[j] = dwf
        SG.setdefault('fox_b_f', [None] * N_FOX)[j] = dbf.reshape(H)
        DG['qkv'] = _mm_dw(f"fox{j}_dw_qkv", xb, dqkv, DG['qkv'], j * D, 'col')
        dh = _mm_col_t(f"fox{j}_dx", [(dqkv, j * D)], WG['qkv'], D)
        return [dh, dh_f]

    def rel_fwd(xb, rest_stage):
        qkv = _mm_col("rel_qkv", xb, WG['qkv'], N_FOX * D)
        rb_pad = jnp.pad(rel_bias[0], ((0, 0), (0, REL_TABLE_PAD - REL_TABLE)))
        bias = _rel_window_bias(jnp.transpose(_rel_expand(rb_pad), (1, 0, 2)))
        o = _rel_fwd(qkv, bias, H)
        weights_ready(rest_stage, o)
        m = _mm_row("rel_wo", o, WG['dm'], wo_base + N_FOX * Ds, Ds)
        return m, (qkv, bias, o)

    def rel_bwd(xb, saved, dzb):
        qkv, bias, o = saved
        DG['dm'] = _mm_dw("rel_dw_o", o, dzb, DG['dm'], wo_base + N_FOX * Ds, 'row')
        do = _mm_row_t("rel_do", dzb, WG['dm'], wo_base + N_FOX * Ds, Ds, BF16)
        dq, dk, dv, dbias = _rel_bwd(qkv, bias, do, H)
        SG['rel_bias'] = _rel_reduce(jnp.transpose(dbias, (1, 0, 2)))[:, :REL_TABLE].reshape(1, H, REL_TABLE)
        dqkv = jnp.concatenate([dq, dk, dv], axis=1)
        DG['qkv'] = _mm_dw("rel_dw_qkv", xb, dqkv, DG['qkv'], N_FOX * D, 'col')
        return [_mm_col_t("rel_dx", [(dqkv, N_FOX * D)], WG['qkv'], D)]

    w_dw32 = jnp.pad(full['conv_w_dw'][0], ((0, CONV_HALO - CONV_K), (0, 0)))
    cg, cb = full['conv_ln_g'], full['conv_ln_b']

    def conv_fwd(xb, rest_stage):
        u = _mm_col("conv_pw1", xb, WG['pw1'], 0, bias=full['conv_b_pw1'], out_dtype=F32)
        u2, = _rowwise("conv_glu", lambda a_, g_: [a_ * _sigmoid(g_)],
                       [('cols', u, D, 0), ('cols', u, D, 1)], [('rows', D, F32)], S)
        yc = _dwconv("conv_dw", u2, w_dw32, full['conv_b_dw'], False)

        def fn(y_, g_, b_):
            xhat, rstd = _ln_stats(y_)
            ln = xhat * g_ + b_
            return ln * _sigmoid(ln), xhat, rstd

        zc, xhat, rstd = _rowwise("conv_ln_silu", fn, [yc, ('full', cg), ('full', cb)],
                                  [('rows', D, BF16), ('rows', D, F32), ('rows', 1, F32)], S)
        weights_ready(rest_stage, zc)
        m = _mm_row("conv_pw2", zc, WG['dm'], wo_base + (N_FOX + 1) * Ds, Ds, bias=full['conv_b_pw2'])
        return m, (u, u2, zc, xhat, rstd)

    def conv_bwd(xb, saved, dz, dzb):
        u, u2, zc, xhat, rstd = saved
        r0 = wo_base + (N_FOX + 1) * Ds
        DG['dm'] = _mm_dw("conv_dw_pw2", zc, dzb, DG['dm'], r0, 'row')
        dzc = _mm_row_t("conv_dzc", dzb, WG['dm'], r0, Ds, F32)

        def fn(dm_, dzc_, xh_, rs_, g_, b_):
            ln = xh_ * g_ + b_
            sg = _sigmoid(ln)
            dln = dzc_ * (sg * (1.0 + ln * (1.0 - sg)))
            dyc = _ln_bwd_core(dln, xh_, rs_, g_)
            col = lambda t: jnp.sum(t, axis=0, keepdims=True)
            return dyc, col(dm_), col(dln * xh_), col(dln), col(dyc)

        dyc, SG['conv_b_pw2'], SG['conv_ln_g'], SG['conv_ln_b'], SG['conv_b_dw'] = _rowwise(
            "conv_ln_silu_bwd", fn, [dz, dzc, xhat, rstd, ('full', cg), ('full', cb)],
            [('rows', D, F32)] + [('acc', (1, D), F32)] * 4, S)
        du2 = _dwconv("conv_dw_bwd_x", dyc, w_dw32, jnp.zeros((1, D), F32), True)
        SG['conv_w_dw'] = _dwconv_dw(u2, dyc)[:CONV_K].reshape(1, CONV_K, D)

        def fn2(du2_, a_, g_):
            sg = _sigmoid(g_)
            da, dgt = du2_ * sg, du2_ * a_ * sg * (1.0 - sg)
            return da, dgt, jnp.sum(da, axis=0, keepdims=True), jnp.sum(dgt, axis=0, keepdims=True)

        da, dgt, dba, dbg = _rowwise("conv_glu_bwd", fn2, [du2, ('cols', u, D, 0), ('cols', u, D, 1)],
                                     [('rows', D, BF16), ('rows', D, BF16), ('acc', (1, D), F32), ('acc', (1, D), F32)], S)
        SG['conv_b_pw1'] = jnp.concatenate([dba, dbg], axis=1)
        du = jnp.concatenate([da, dgt], axis=1)
        DG['pw1'] = _mm_dw("conv_dw_pw1", xb, du, DG['pw1'], 0, 'col')
        return [_mm_col_t("conv_dx", [(du, 0)], WG['pw1'], D)]

    xs = x[0]
    xs_b = xs.astype(BF16)
    tape = []

    def weights_ready(s, after):
        bufs = _gather_wait(f"gather_wait{s}", [WG[g] for g in GROUPS], gather_ssems[s], gather_rsems[s], gstages[s], after)
        WG.update(zip(GROUPS, _gather_forward(f"gather_fwd{s}", bufs, gstages[s])))

    for i in range(DEPTH):
        kind, j = i % 3, i // 3
        weights_ready(2 * i, xs)
        if kind == 0:
            m, msaved = fox_fwd(j, xs_b, 2 * i + 1)
        elif kind == 1:
            m, msaved = rel_fwd(xs_b, 2 * i + 1)
        else:
            m, msaved = conv_fwd(xs_b, 2 * i + 1)
        xm, xm_b, xhat1, rstd1 = _ln_fwd(f"ln_mix{i}", xs, m, row(ln_mix_g[i]), row(ln_mix_b[i]))
        f, fsaved = ffn_fwd(i, xm_b)
        xo, xo_b, xhat2, rstd2 = _ln_fwd(f"ln_ffn{i}", xm, f, row(ln_ffn_g[i]), row(ln_ffn_b[i]))
        tape.append((xs_b, msaved, xhat1, rstd1, xm_b, fsaved, xhat2, rstd2))
        xs, xs_b = xo, xo_b

    def loss_fn(y_, t_):
        e = y_ - t_
        return e * (1.0 / D), jnp.sum(e * e, axis=0, keepdims=True)

    dy, sq = _rowwise("loss", loss_fn, [xs, loss_target[0]], [('rows', D, F32), ('acc', (1, D), F32)], S)
    loss = lax.psum(jnp.sum(sq) * (0.5 / D), ("x", "y", "c"))

    GF = {g: lax.empty(WG[g].shape[1:], F32) for g in GROUPS}
    started = [None] * len(stages)

    def reduce_start(s):
        dgs = [DG[g] for g in GROUPS]
        lands = _swap_halves(f"pair_swap{s}", dgs, stages[s])
        pbs = [_pair_sum(f"pair_sum{s}_{pi}", dgs[g], lands[pi], place, r0, n) for pi, (g, r0, n) in enumerate(stages[s])]
        started[s] = _scatter_start(f"scatter_start{s}", pbs)
        token = started[s][2][0]
        if s + 1 < len(stages):
            reduce_finish(s + 1, token)
        return token

    def reduce_finish(s, after):
        ssem, rsem, pbs, lands2 = started[s]
        pbs, lands2 = _scatter_wait(f"scatter_wait{s}", ssem, rsem, pbs, lands2, after)
        for pi, (g, r0, n) in enumerate(stages[s]):
            GF[GROUPS[g]] = _chip_sum(f"chip_sum{s}_{pi}", pbs[pi], lands2[pi], place, GF[GROUPS[g]], r0, n)

    terms = [(dy, 1.0)]
    token = None
    g_mix, b_mix, g_ffn, b_ffn = [None] * DEPTH, [None] * DEPTH, [None] * DEPTH, [None] * DEPTH
    for i in reversed(range(DEPTH)):
        kind, j = i % 3, i // 3
        xin_b, msaved, xhat1, rstd1, xm_b, fsaved, xhat2, rstd2 = tape[i]
        dz2, dz2b, g_ffn[i], b_ffn[i] = _ln_bwd(f"ln_ffn{i}_bwd", terms, xhat2, rstd2, row(ln_ffn_g[i]), after=token)
        dx_ffn = ffn_bwd(i, xm_b, fsaved, dz2b)
        token = reduce_start(2 * i + 1)
        dz1, dz1b, g_mix[i], b_mix[i] = _ln_bwd(f"ln_mix{i}_bwd", [(dz2, ALPHA), (dx_ffn, 1.0)], xhat1, rstd1,
                                                row(ln_mix_g[i]), after=token)
        if kind == 0:
            mix_terms = fox_bwd(j, xin_b, msaved, dz1b)
        elif kind == 1:
            mix_terms = rel_bwd(xin_b, msaved, dz1b)
        else:
            mix_terms = conv_bwd(xin_b, msaved, dz1, dz1b)
        terms = [(dz1, ALPHA)] + [(t, 1.0) for t in mix_terms]
        token = reduce_start(2 * i)

    def gx_fn(*v):
        acc = v[0] * ALPHA
        for t in v[1:]:
            acc = acc + t
        return [acc]

    grad_x, = _rowwise("grad_x", gx_fn, [a for a, _ in terms], [('rows', D, F32)], S)
    grad_x = grad_x.reshape(1, S, D)

    SG['fox_w_f'] = jnp.stack(SG['fox_w_f'])
    SG['fox_b_f'] = jnp.stack(SG['fox_b_f'])
    SG['ln_mix_g'] = jnp.concatenate(g_mix, axis=0)
    SG['ln_mix_b'] = jnp.concatenate(b_mix, axis=0)
    SG['ln_ffn_g'] = jnp.concatenate(g_ffn, axis=0)
    SG['ln_ffn_b'] = jnp.concatenate(b_ffn, axis=0)

    grads, deltas, new_m, new_v = {}, {}, {}, {}

    reduce_finish(0, grad_x)
    links_idle = GF['dm']
    GF = dict(zip(GROUPS, _share_halves("pair_share", [GF[g] for g in GROUPS], [p for st in stages for p in st])))

    for n in BIG:
        g, r0 = where[n]
        outs = _adamw("adamw_" + n, flat2(Wt[n]), GF[g], r0, flat2(Mo[n]), flat2(Vo[n]))
        grads[n], deltas[n], new_m[n], new_v[n] = [o.reshape(Wt[n].shape) for o in outs]

    full_shapes = [SG[n].shape for n in SMALL]
    summed = _sum_slabs("small_sum", _broadcast_small("small_exchange", _pack([SG[n] for n in SMALL]), after=links_idle))
    gsm = dict(zip(SMALL, _unpack(summed, full_shapes)))
    for n in SMALL_SHARDED:
        ax = SMALL_SHARD_AXIS[n]
        width = Wt[n].shape[ax]
        gsm[n] = lax.dynamic_slice_in_dim(gsm[n], my_chip * width, width, axis=ax)
    own_shapes = [Wt[n].shape for n in SMALL]
    packed = [_pack([src[n] for n in SMALL]) for src in (Wt, gsm, Mo, Vo)]
    rows_small = packed[0].shape[0]

    def small_fn(w_, g_, m_, v_):
        return _adamw_math(w_, g_, m_, v_)

    sd, sm, sv = _rowwise("adamw_small", small_fn, packed, [('rows', LANES, F32)] * 3, rows_small, tm=rows_small)
    for n, d_, m_, v_ in zip(SMALL, _unpack(sd, own_shapes), _unpack(sm, own_shapes), _unpack(sv, own_shapes)):
        grads[n], deltas[n], new_m[n], new_v[n] = gsm[n], d_, m_, v_

    return (loss, grad_x, *[grads[n] for n in WEIGHTS], *[deltas[n] for n in WEIGHTS],
            *[new_m[n] for n in WEIGHTS], *[new_v[n] for n in WEIGHTS])
```
